```python
import jax, jax.numpy as jnp
from jax import lax
import numpy as np

D_MODEL = 1024
BATCH = 16
SEQ = 4096
DEPTH = 2

CHUNK = 64
N_MIXERS = 2
RET_HEADS = 4
RET_DK = D_MODEL // RET_HEADS
RET_DV = 2 * RET_DK
RET_QK = RET_HEADS * RET_DK
RET_VW = RET_HEADS * RET_DV
ROPE_BASE = 10000.0
ATT_HEADS = 16
ATT_DH = D_MODEL // ATT_HEADS
PAST_CHUNKS = 8
BAND_PAST = PAST_CHUNKS * CHUNK
BAND = (PAST_CHUNKS + 1) * CHUNK
MAX_REL = 256
REL_TABLE = MAX_REL + CHUNK
D_FF = 4 * D_MODEL
EPS = 1e-6

kernel_name = "hybrid_retention_chunkattn_encoder"


def _rmsnorm(x, g):
    xf = x.astype(jnp.float32)
    y = xf * lax.rsqrt(jnp.mean(xf * xf, axis=-1, keepdims=True) + EPS)
    return (y * g.astype(jnp.float32)).astype(x.dtype)


def _rope(t, pos):
    d = t.shape[-1]
    half = d // 2
    inv = jnp.exp(-jnp.log(ROPE_BASE) * jnp.arange(half, dtype=jnp.float32) / half)
    ang = pos[:, None] * inv[None, :]
    cos = jnp.cos(ang)[None, :, None, :]
    sin = jnp.sin(ang)[None, :, None, :]
    tf = t.astype(jnp.float32)
    t1, t2 = tf[..., :half], tf[..., half:]
    return jnp.concatenate([t1 * cos - t2 * sin, t1 * sin + t2 * cos], axis=-1).astype(t.dtype)


def _retention(h, w_in, gn_g, w_out):
    B, S, _ = h.shape
    nc = S // CHUNK
    proj = h @ w_in
    q, k, v, g = jnp.split(proj, [RET_QK, 2 * RET_QK, 2 * RET_QK + RET_VW], axis=-1)
    q = q.reshape(B, S, RET_HEADS, RET_DK)
    k = k.reshape(B, S, RET_HEADS, RET_DK)
    v = v.reshape(B, S, RET_HEADS, RET_DV)
    pos = jnp.arange(S, dtype=jnp.float32)
    q = _rope(q, pos)
    k = _rope(k, pos) * (RET_DK ** -0.5)

    def to_chunks(t):
        return t.reshape(B, nc, CHUNK, RET_HEADS, t.shape[-1]).transpose(1, 0, 3, 2, 4)

    qc, kc, vc = to_chunks(q), to_chunks(k), to_chunks(v)
    dt = q.dtype
    log_gamma = jnp.log1p(-jnp.exp2(-5.0 - jnp.arange(RET_HEADS, dtype=jnp.float32)))
    idx = jnp.arange(CHUNK, dtype=jnp.float32)
    intra_decay = jnp.exp(log_gamma[:, None, None] * jnp.abs(idx[:, None] - idx[None, :])).astype(dt)
    key_decay = jnp.exp(log_gamma[:, None] * (CHUNK - 1 - idx)[None, :]).astype(dt)
    query_decay = jnp.exp(log_gamma[:, None] * (idx + 1.0)[None, :]).astype(dt)
    chunk_decay = jnp.exp(log_gamma * CHUNK).astype(dt)

    def step(state, inp):
        qb, kb, vb = inp
        scores = jnp.einsum('bhid,bhjd->bhij', qb, kb) * intra_decay[None]
        o_intra = jnp.einsum('bhij,bhje->bhie', scores, vb)
        o_cross = jnp.einsum('bhid,bhde->bhie', qb, state) * query_decay[None, :, :, None]
        new_state = (state * chunk_decay[None, :, None, None]
                     + jnp.einsum('bhjd,bhje->bhde', kb * key_decay[None, :, :, None], vb))
        return new_state, o_intra + o_cross

    state0 = jnp.zeros((B, RET_HEADS, RET_DK, RET_DV), dt)
    _, o = lax.scan(step, state0, (qc, kc, vc))
    o = o.transpose(1, 0, 3, 2, 4).reshape(B, S, RET_HEADS, RET_DV).astype(jnp.float32)
    mu = jnp.mean(o, axis=-1, keepdims=True)
    var = jnp.mean(jnp.square(o - mu), axis=-1, keepdims=True)
    o = ((o - mu) * lax.rsqrt(var + EPS)).reshape(B, S, RET_VW) * gn_g.astype(jnp.float32)
    y = (jax.nn.silu(g.astype(jnp.float32)) * o).astype(h.dtype)
    return y @ w_out


def _chunk_attention(h, w_in, rel_bias, w_out):
    B, S, _ = h.shape
    nc = S // CHUNK
    q, k, v = jnp.split(h @ w_in, 3, axis=-1)
    q = q.reshape(B, S, ATT_HEADS, ATT_DH)
    k = k.reshape(B, S, ATT_HEADS, ATT_DH)
    v = v.reshape(B, S, ATT_HEADS, ATT_DH)
    pad = ((0, 0), (BAND_PAST, 0), (0, 0), (0, 0))
    kp = jnp.pad(k, pad)
    vp = jnp.pad(v, pad)
    qi = jnp.arange(CHUNK)[:, None]
    kj = jnp.arange(BAND)[None, :]
    rel = kj - BAND_PAST - qi
    bidx = jnp.maximum(rel, -MAX_REL) + MAX_REL
    bias = rel_bias[:, bidx].astype(jnp.float32)
    scale = ATT_DH ** -0.5

    def one_chunk(c):
        start = c * CHUNK
        qb = lax.dynamic_slice_in_dim(q, start, CHUNK, axis=1)
        kb = lax.dynamic_slice_in_dim(kp, start, BAND, axis=1)
        vb = lax.dynamic_slice_in_dim(vp, start, BAND, axis=1)
        s = jnp.einsum('bqhd,bkhd->bhqk', qb, kb).astype(jnp.float32) * scale + bias[None]
        valid = (start - BAND_PAST + jnp.arange(BAND)) >= 0
        s = jnp.where(valid[None, None, None, :], s, -jnp.inf)
        p = jax.nn.softmax(s, axis=-1).astype(vb.dtype)
        return jnp.einsum('bhqk,bkhd->bqhd', p, vb)

    o = lax.map(one_chunk, jnp.arange(nc))
    o = o.transpose(1, 0, 2, 3, 4).reshape(B, S, ATT_HEADS * ATT_DH)
    return o @ w_out


def _sqrelu_mlp(h, w1, w2):
    u = jax.nn.relu(h @ w1)
    return (u * u) @ w2


def _fwd_setup_inputs(seed: int = 0) -> dict:
    key = jax.random.key(seed)
    ks = jax.random.split(key, 12)
    n_ret = (DEPTH + 1) // 2
    n_att = DEPTH // 2
    f = jnp.float32
    x = jax.random.normal(ks[0], (BATCH, SEQ, D_MODEL), f)
    mix_norm_g = 1.0 + 0.02 * jax.random.normal(ks[1], (DEPTH, D_MODEL), f)
    ret_w_in = jax.random.normal(ks[2], (n_ret, D_MODEL, 2 * RET_QK + 2 * RET_VW), f) * D_MODEL ** -0.5
    ret_gn_g = 1.0 + 0.02 * jax.random.normal(ks[3], (n_ret, RET_VW), f)
    ret_w_out = jax.random.normal(ks[4], (n_ret, RET_VW, D_MODEL), f) * RET_VW ** -0.5
    att_w_in = jax.random.normal(ks[5], (n_att, D_MODEL, 3 * D_MODEL), f) * D_MODEL ** -0.5
    att_rel_bias = 0.1 * jax.random.normal(ks[6], (n_att, ATT_HEADS, REL_TABLE), f)
    att_w_out = jax.random.normal(ks[7], (n_att, D_MODEL, D_MODEL), f) * D_MODEL ** -0.5
    mlp_norm_g = 1.0 + 0.02 * jax.random.normal(ks[8], (DEPTH, D_MODEL), f)
    mlp_w1 = jax.random.normal(ks[9], (DEPTH, D_MODEL, D_FF), f) * D_MODEL ** -0.5
    mlp_w2 = jax.random.normal(ks[10], (DEPTH, D_FF, D_MODEL), f) * D_FF ** -0.5
    final_norm_g = 1.0 + 0.02 * jax.random.normal(ks[11], (D_MODEL,), f)
    return {"x": x, "mix_norm_g": mix_norm_g, "ret_w_in": ret_w_in, "ret_gn_g": ret_gn_g,
            "ret_w_out": ret_w_out, "att_w_in": att_w_in, "att_rel_bias": att_rel_bias,
            "att_w_out": att_w_out, "mlp_norm_g": mlp_norm_g, "mlp_w1": mlp_w1,
            "mlp_w2": mlp_w2, "final_norm_g": final_norm_g}


def _fwd_reference(x, mix_norm_g, ret_w_in, ret_gn_g, ret_w_out, att_w_in, att_rel_bias,
              att_w_out, mlp_norm_g, mlp_w1, mlp_w2, final_norm_g):
    h = x
    for i in range(DEPTH):
        hn = _rmsnorm(h, mix_norm_g[i])
        j = i // N_MIXERS
        if i % N_MIXERS == 0:
            h = h + _retention(hn, ret_w_in[j], ret_gn_g[j], ret_w_out[j])
        else:
            h = h + _chunk_attention(hn, att_w_in[j], att_rel_bias[j], att_w_out[j])
        h = h + _sqrelu_mlp(_rmsnorm(h, mlp_norm_g[i]), mlp_w1[i], mlp_w2[i])
    return _rmsnorm(h, final_norm_g)


import jax as _jax
import jax.numpy as _jnp

TWIN_FORMAT = 'train_step'
FWD_PARAMS = ['x', 'mix_norm_g', 'ret_w_in', 'ret_gn_g', 'ret_w_out', 'att_w_in', 'att_rel_bias', 'att_w_out', 'mlp_norm_g', 'mlp_w1', 'mlp_w2', 'final_norm_g']
TWIN_WEIGHTS = ['mix_norm_g', 'ret_w_in', 'ret_gn_g', 'ret_w_out', 'att_w_in', 'att_rel_bias', 'att_w_out', 'mlp_norm_g', 'mlp_w1', 'mlp_w2', 'final_norm_g']
TWIN_DIFF_INPUT = 'x'
TWIN_INPUTS = ['x', 'mix_norm_g', 'ret_w_in', 'ret_gn_g', 'ret_w_out', 'att_w_in', 'att_rel_bias', 'att_w_out', 'mlp_norm_g', 'mlp_w1', 'mlp_w2', 'final_norm_g', 'loss_target', 'm_mix_norm_g', 'm_ret_w_in', 'm_ret_gn_g', 'm_ret_w_out', 'm_att_w_in', 'm_att_rel_bias', 'm_att_w_out', 'm_mlp_norm_g', 'm_mlp_w1', 'm_mlp_w2', 'm_final_norm_g', 'v_mix_norm_g', 'v_ret_w_in', 'v_ret_gn_g', 'v_ret_w_out', 'v_att_w_in', 'v_att_rel_bias', 'v_att_w_out', 'v_mlp_norm_g', 'v_mlp_w1', 'v_mlp_w2', 'v_final_norm_g']
TWIN_OUTPUTS = ['loss', 'grad_x', 'grad_mix_norm_g', 'grad_ret_w_in', 'grad_ret_gn_g', 'grad_ret_w_out', 'grad_att_w_in', 'grad_att_rel_bias', 'grad_att_w_out', 'grad_mlp_norm_g', 'grad_mlp_w1', 'grad_mlp_w2', 'grad_final_norm_g', 'delta_mix_norm_g', 'delta_ret_w_in', 'delta_ret_gn_g', 'delta_ret_w_out', 'delta_att_w_in', 'delta_att_rel_bias', 'delta_att_w_out', 'delta_mlp_norm_g', 'delta_mlp_w1', 'delta_mlp_w2', 'delta_final_norm_g', 'new_m_mix_norm_g', 'new_m_ret_w_in', 'new_m_ret_gn_g', 'new_m_ret_w_out', 'new_m_att_w_in', 'new_m_att_rel_bias', 'new_m_att_w_out', 'new_m_mlp_norm_g', 'new_m_mlp_w1', 'new_m_mlp_w2', 'new_m_final_norm_g', 'new_v_mix_norm_g', 'new_v_ret_w_in', 'new_v_ret_gn_g', 'new_v_ret_w_out', 'new_v_att_w_in', 'new_v_att_rel_bias', 'new_v_att_w_out', 'new_v_mlp_norm_g', 'new_v_mlp_w1', 'new_v_mlp_w2', 'new_v_final_norm_g']
TWIN_LEAF_KINDS = {'loss': 'loss', 'grad_x': 'grad_x', 'grad_mix_norm_g': 'grad_w', 'grad_ret_w_in': 'grad_w', 'grad_ret_gn_g': 'grad_w', 'grad_ret_w_out': 'grad_w', 'grad_att_w_in': 'grad_w', 'grad_att_rel_bias': 'grad_w', 'grad_att_w_out': 'grad_w', 'grad_mlp_norm_g': 'grad_w', 'grad_mlp_w1': 'grad_w', 'grad_mlp_w2': 'grad_w', 'grad_final_norm_g': 'grad_w', 'delta_mix_norm_g': 'delta_w', 'delta_ret_w_in': 'delta_w', 'delta_ret_gn_g': 'delta_w', 'delta_ret_w_out': 'delta_w', 'delta_att_w_in': 'delta_w', 'delta_att_rel_bias': 'delta_w', 'delta_att_w_out': 'delta_w', 'delta_mlp_norm_g': 'delta_w', 'delta_mlp_w1': 'delta_w', 'delta_mlp_w2': 'delta_w', 'delta_final_norm_g': 'delta_w', 'new_m_mix_norm_g': 'new_m', 'new_m_ret_w_in': 'new_m', 'new_m_ret_gn_g': 'new_m', 'new_m_ret_w_out': 'new_m', 'new_m_att_w_in': 'new_m', 'new_m_att_rel_bias': 'new_m', 'new_m_att_w_out': 'new_m', 'new_m_mlp_norm_g': 'new_m', 'new_m_mlp_w1': 'new_m', 'new_m_mlp_w2': 'new_m', 'new_m_final_norm_g': 'new_m', 'new_v_mix_norm_g': 'new_v', 'new_v_ret_w_in': 'new_v', 'new_v_ret_gn_g': 'new_v', 'new_v_ret_w_out': 'new_v', 'new_v_att_w_in': 'new_v', 'new_v_att_rel_bias': 'new_v', 'new_v_att_w_out': 'new_v', 'new_v_mlp_norm_g': 'new_v', 'new_v_mlp_w1': 'new_v', 'new_v_mlp_w2': 'new_v', 'new_v_final_norm_g': 'new_v'}


def _forward(args):
    return _fwd_reference(*[args[k] for k in FWD_PARAMS])


def _output_shape():
    out = _jax.eval_shape(lambda: _forward(_fwd_setup_inputs(0)))
    return out.shape, out.dtype

N_MICROBATCH = 1
ADAM_LR = 0.001
ADAM_B1 = 0.9
ADAM_B2 = 0.999
ADAM_EPS = 1e-08
ADAM_WD = 0.01
ADAM_STEP = 10
PER_EXAMPLE_BATCH_AXIS = {'x': 0, 'loss_target': 0}
SHARED_INPUTS = []
_WEIGHT_DTYPES = {'mix_norm_g': _jnp.float32, 'ret_w_in': _jnp.float32, 'ret_gn_g': _jnp.float32, 'ret_w_out': _jnp.float32, 'att_w_in': _jnp.float32, 'att_rel_bias': _jnp.float32, 'att_w_out': _jnp.float32, 'mlp_norm_g': _jnp.float32, 'mlp_w1': _jnp.float32, 'mlp_w2': _jnp.float32, 'final_norm_g': _jnp.float32}
MOMENT_SCALE = {'mix_norm_g': 2.181795e-01, 'ret_w_in': 1.201234e-01, 'ret_gn_g': 1.037744e-01, 'ret_w_out': 1.455454e-01, 'att_w_in': 4.221621e-02, 'att_rel_bias': 8.692947e-03, 'att_w_out': 7.750998e-02, 'mlp_norm_g': 2.084968e-01, 'mlp_w1': 9.993493e-02, 'mlp_w2': 2.074020e-01, 'final_norm_g': 6.540420e+01}


def _to_microbatches(a, axis):
    t = _jnp.moveaxis(a, axis, 0)
    t = t.reshape((N_MICROBATCH, t.shape[0] // N_MICROBATCH) + t.shape[1:])
    return _jnp.moveaxis(t, 1, axis + 1)


def setup_inputs(seed: int = 0) -> dict:
    inp = _fwd_setup_inputs(seed)
    key = _jax.random.fold_in(_jax.random.key(seed), 7919)
    shape, _ = _output_shape()
    out = dict(inp)
    out["loss_target"] = _jax.random.normal(_jax.random.fold_in(key, 0), shape, _jnp.float32)
    for i, name in enumerate(TWIN_WEIGHTS):
        w = inp[name].astype(_jnp.float32)
        if MOMENT_SCALE is None:
            s = _jnp.sqrt(_jnp.mean(_jnp.square(w)) + 1e-30)
        else:
            s = MOMENT_SCALE[name]
        km, kv = _jax.random.split(_jax.random.fold_in(key, i + 1))
        out[name] = w
        out["m_" + name] = s * _jax.random.normal(km, w.shape, _jnp.float32)
        out["v_" + name] = (s * s) * _jax.random.uniform(kv, w.shape, _jnp.float32, 0.5, 1.5)
    if N_MICROBATCH > 1:
        for name, axis in PER_EXAMPLE_BATCH_AXIS.items():
            out[name] = _to_microbatches(out[name], axis)
    return {'x': out['x'], 'mix_norm_g': out['mix_norm_g'], 'ret_w_in': out['ret_w_in'], 'ret_gn_g': out['ret_gn_g'], 'ret_w_out': out['ret_w_out'], 'att_w_in': out['att_w_in'], 'att_rel_bias': out['att_rel_bias'], 'att_w_out': out['att_w_out'], 'mlp_norm_g': out['mlp_norm_g'], 'mlp_w1': out['mlp_w1'], 'mlp_w2': out['mlp_w2'], 'final_norm_g': out['final_norm_g'], 'loss_target': out['loss_target'], 'm_mix_norm_g': out['m_mix_norm_g'], 'm_ret_w_in': out['m_ret_w_in'], 'm_ret_gn_g': out['m_ret_gn_g'], 'm_ret_w_out': out['m_ret_w_out'], 'm_att_w_in': out['m_att_w_in'], 'm_att_rel_bias': out['m_att_rel_bias'], 'm_att_w_out': out['m_att_w_out'], 'm_mlp_norm_g': out['m_mlp_norm_g'], 'm_mlp_w1': out['m_mlp_w1'], 'm_mlp_w2': out['m_mlp_w2'], 'm_final_norm_g': out['m_final_norm_g'], 'v_mix_norm_g': out['v_mix_norm_g'], 'v_ret_w_in': out['v_ret_w_in'], 'v_ret_gn_g': out['v_ret_gn_g'], 'v_ret_w_out': out['v_ret_w_out'], 'v_att_w_in': out['v_att_w_in'], 'v_att_rel_bias': out['v_att_rel_bias'], 'v_att_w_out': out['v_att_w_out'], 'v_mlp_norm_g': out['v_mlp_norm_g'], 'v_mlp_w1': out['v_mlp_w1'], 'v_mlp_w2': out['v_mlp_w2'], 'v_final_norm_g': out['v_final_norm_g']}


def _loss(weights, diff, rest, loss_target):
    with _jax.named_scope("forward"):
        args = {**rest, TWIN_DIFF_INPUT: diff, **{k: w.astype(_WEIGHT_DTYPES[k]) for k, w in weights.items()}}
        y = _forward(args)
    with _jax.named_scope("loss_head"):
        err = _jnp.square(y.astype(_jnp.float32) - loss_target)
        return 0.5 * _jnp.sum(_jnp.mean(err, axis=-1)) if err.ndim else 0.5 * err


def _adamw(w, g, m, v):
    m = ADAM_B1 * m + (1.0 - ADAM_B1) * g
    v = ADAM_B2 * v + (1.0 - ADAM_B2) * _jnp.square(g)
    m_hat = m / (1.0 - ADAM_B1 ** ADAM_STEP)
    v_hat = v / (1.0 - ADAM_B2 ** ADAM_STEP)
    delta = -ADAM_LR * (m_hat / (_jnp.sqrt(v_hat) + ADAM_EPS) + ADAM_WD * w)
    return delta, m, v


def reference(x, mix_norm_g, ret_w_in, ret_gn_g, ret_w_out, att_w_in, att_rel_bias, att_w_out, mlp_norm_g, mlp_w1, mlp_w2, final_norm_g, loss_target, m_mix_norm_g, m_ret_w_in, m_ret_gn_g, m_ret_w_out, m_att_w_in, m_att_rel_bias, m_att_w_out, m_mlp_norm_g, m_mlp_w1, m_mlp_w2, m_final_norm_g, v_mix_norm_g, v_ret_w_in, v_ret_gn_g, v_ret_w_out, v_att_w_in, v_att_rel_bias, v_att_w_out, v_mlp_norm_g, v_mlp_w1, v_mlp_w2, v_final_norm_g):
    given = dict(x=x, mix_norm_g=mix_norm_g, ret_w_in=ret_w_in, ret_gn_g=ret_gn_g, ret_w_out=ret_w_out, att_w_in=att_w_in, att_rel_bias=att_rel_bias, att_w_out=att_w_out, mlp_norm_g=mlp_norm_g, mlp_w1=mlp_w1, mlp_w2=mlp_w2, final_norm_g=final_norm_g, loss_target=loss_target, m_mix_norm_g=m_mix_norm_g, m_ret_w_in=m_ret_w_in, m_ret_gn_g=m_ret_gn_g, m_ret_w_out=m_ret_w_out, m_att_w_in=m_att_w_in, m_att_rel_bias=m_att_rel_bias, m_att_w_out=m_att_w_out, m_mlp_norm_g=m_mlp_norm_g, m_mlp_w1=m_mlp_w1, m_mlp_w2=m_mlp_w2, m_final_norm_g=m_final_norm_g, v_mix_norm_g=v_mix_norm_g, v_ret_w_in=v_ret_w_in, v_ret_gn_g=v_ret_gn_g, v_ret_w_out=v_ret_w_out, v_att_w_in=v_att_w_in, v_att_rel_bias=v_att_rel_bias, v_att_w_out=v_att_w_out, v_mlp_norm_g=v_mlp_norm_g, v_mlp_w1=v_mlp_w1, v_mlp_w2=v_mlp_w2, v_final_norm_g=v_final_norm_g)
    weights = {n: given[n] for n in TWIN_WEIGHTS}
    shared = {n: given[n] for n in SHARED_INPUTS}
    per_example = {n: given[n] for n in ['x']}
    grad_fn = _jax.value_and_grad(_loss, argnums=(0, 1))

    def one_microbatch(ex, loss_target):
        ex = dict(ex)
        diff = ex.pop(TWIN_DIFF_INPUT)
        return grad_fn(weights, diff, {**shared, **ex}, loss_target)

    if N_MICROBATCH == 1:
        loss, (grad_w, grad_x) = one_microbatch(per_example, given["loss_target"])
    else:
        def body(carry, xs):
            loss_sum, grad_sum = carry
            l_k, (gw_k, gx_k) = one_microbatch(xs[0], xs[1])
            with _jax.named_scope("update"):
                return (loss_sum + l_k, _jax.tree.map(_jnp.add, grad_sum, gw_k)), gx_k

        init = (_jnp.zeros((), _jnp.float32), _jax.tree.map(_jnp.zeros_like, weights))
        (loss, grad_w), grad_x = _jax.lax.scan(body, init, (per_example, given["loss_target"]))
    with _jax.named_scope("update"):
        delta_w, new_m, new_v = {}, {}, {}
        for n in TWIN_WEIGHTS:
            delta_w[n], new_m[n], new_v[n] = _adamw(weights[n], grad_w[n], given["m_" + n], given["v_" + n])
    return (loss, grad_x, *[grad_w[n] for n in TWIN_WEIGHTS], *[delta_w[n] for n in TWIN_WEIGHTS],
            *[new_m[n] for n in TWIN_WEIGHTS], *[new_v[n] for n in TWIN_WEIGHTS])
```

```python
import functools

import numpy as np
import jax
import jax.numpy as jnp
from jax import lax
from jax.experimental import pallas as pl
from jax.experimental.pallas import tpu as pltpu

F32, BF16 = jnp.float32, jnp.bfloat16

D_MODEL = 1024
CHUNK = 64
RET_HEADS, RET_DK, RET_DV = 4, 256, 512
RET_QK, RET_VW = RET_HEADS * RET_DK, RET_HEADS * RET_DV
RET_IN = 2 * RET_QK + 2 * RET_VW
RET_HEAD_COLS = 2 * RET_DK + 2 * RET_DV
RET_SC = 256
ROPE_BASE = 10000.0
ATT_HEADS, ATT_DH = 16, 64
ATT_PAIRS = ATT_HEADS // 2
BAND_PAST = 8 * CHUNK
MAX_REL = 256
REL_TABLE = MAX_REL + CHUNK
REL_PAD = 384
ATT_CQ = 256
ATT_KW = ATT_CQ + BAND_PAST
ATT_EXT = ATT_CQ + ATT_KW - 1
D_FF = 4 * D_MODEL
EPS = 1e-6
NEG = -1e30
N_DEV = 8

ADAM_LR, ADAM_B1, ADAM_B2, ADAM_EPS, ADAM_WD, ADAM_STEP = 0.001, 0.9, 0.999, 1e-08, 0.01, 10

VMEM_LIMIT = 48 * 1024 * 1024
MESH = pl.DeviceIdType.MESH
ANY = pl.BlockSpec(memory_space=pl.ANY)


def _params(*sem):
    return pltpu.CompilerParams(dimension_semantics=sem, vmem_limit_bytes=VMEM_LIMIT)


def _dot(a, b):
    return jnp.dot(a, b, preferred_element_type=F32)


def _dot_nt(a, b):
    return lax.dot_general(a, b, (((1,), (1,)), ((), ())), preferred_element_type=F32)


def _dot_tn(a, b):
    return lax.dot_general(a, b, (((0,), (0,)), ((), ())), preferred_element_type=F32)


def _rms(x, g):
    r = lax.rsqrt(jnp.mean(x * x, axis=-1, keepdims=True) + EPS)
    return x * r * g


def _rms_bwd(dn, x, g):
    r = lax.rsqrt(jnp.mean(x * x, axis=-1, keepdims=True) + EPS)
    xh = x * r
    dg = jnp.sum(dn * xh, axis=0, keepdims=True)
    dxh = dn * g
    dx = r * (dxh - xh * jnp.mean(dxh * xh, axis=-1, keepdims=True))
    return dx, dg


def _norm_proj(h, g, w, pieces, n_steps, out_shape, out_spec, name, tm=1024):
    T = h.shape[0]
    n_p = len(pieces)

    def body(h_ref, g_ref, *rest):
        w_refs = rest[:n_p]
        o_ref, n_ref = rest[n_p:]

        @pl.when(pl.program_id(1) == 0)
        def _():
            n_ref[...] = _rms(h_ref[...], g_ref[...]).astype(BF16)

        n = n_ref[...]
        for (width, _, start), w_ref in zip(pieces, w_refs):
            o_ref[:, start:start + width] = _dot(n, w_ref[...]).astype(o_ref.dtype)

    w_specs = [pl.BlockSpec((D_MODEL, width), functools.partial(lambda i, j, f: (0, f(j)), f=col))
               for (width, col, _) in pieces]
    return pl.pallas_call(
        body, name=name, grid=(T // tm, n_steps),
        in_specs=[pl.BlockSpec((tm, D_MODEL), lambda i, j: (i, 0)), pl.BlockSpec((1, D_MODEL), lambda i, j: (0, 0))] + w_specs,
        out_specs=[out_spec(tm), pl.BlockSpec((tm, D_MODEL), lambda i, j: (i, 0))],
        out_shape=[out_shape, jax.ShapeDtypeStruct((T, D_MODEL), BF16)],
        compiler_params=_params("arbitrary", "arbitrary"),
    )(h, g, *([w] * n_p))


def _proj_bwd(dy, dy_spec, w, pieces, n_steps, x, g, dres, name, tm=512):
    T = x.shape[0]
    n_p = len(pieces)

    def body(dy_ref, *rest):
        w_refs = rest[:n_p]
        x_ref, g_ref, dres_ref, dx_ref, dxb_ref, dg_ref = rest[n_p:]
        acc = dx_ref
        i, j = pl.program_id(0), pl.program_id(1)

        @pl.when(j == 0)
        def _():
            acc[...] = jnp.zeros_like(acc)

        @pl.when((i == 0) & (j == 0))
        def _():
            dg_ref[...] = jnp.zeros_like(dg_ref)

        for (width, _, start), w_ref in zip(pieces, w_refs):
            acc[...] += _dot_nt(dy_ref[:, start:start + width], w_ref[...])

        @pl.when(j == n_steps - 1)
        def _():
            dx, dg = _rms_bwd(acc[...], x_ref[...], g_ref[...])
            dx = dres_ref[...] + dx
            dx_ref[...] = dx
            dxb_ref[...] = dx.astype(BF16)
            dg_ref[...] += dg

    w_specs = [pl.BlockSpec((D_MODEL, width), functools.partial(lambda i, j, f: (0, f(j)), f=col))
               for (width, col, _) in pieces]
    row = pl.BlockSpec((tm, D_MODEL), lambda i, j: (i, 0))
    vec = pl.BlockSpec((1, D_MODEL), lambda i, j: (0, 0))
    return pl.pallas_call(
        body, name=name, grid=(T // tm, n_steps),
        in_specs=[dy_spec(tm)] + w_specs + [row, vec, row],
        out_specs=[row, row, vec],
        out_shape=[jax.ShapeDtypeStruct((T, D_MODEL), F32), jax.ShapeDtypeStruct((T, D_MODEL), BF16),
                   jax.ShapeDtypeStruct((1, D_MODEL), F32)],
        compiler_params=_params("arbitrary", "arbitrary"),
    )(dy, *([w] * n_p), x, g, dres)


def _matmul_res(a, w, res, name, tm=1024):
    T, K = a.shape

    def body(a_ref, w_ref, r_ref, o_ref):
        o_ref[...] = r_ref[...] + _dot(a_ref[...], w_ref[...])

    return pl.pallas_call(
        body, name=name, grid=(T // tm,),
        in_specs=[pl.BlockSpec((tm, K), lambda i: (i, 0)), pl.BlockSpec((K, D_MODEL), lambda i: (0, 0)),
                  pl.BlockSpec((tm, D_MODEL), lambda i: (i, 0))],
        out_specs=pl.BlockSpec((tm, D_MODEL), lambda i: (i, 0)),
        out_shape=jax.ShapeDtypeStruct((T, D_MODEL), F32),
        compiler_params=_params("arbitrary"),
    )(a, w, res)


def _matmul_nt(dy, w, name, tm=1024):
    T, N = dy.shape
    K = w.shape[0]

    def body(dy_ref, w_ref, o_ref):
        o_ref[...] = _dot_nt(dy_ref[...], w_ref[...]).astype(BF16)

    return pl.pallas_call(
        body, name=name, grid=(T // tm,),
        in_specs=[pl.BlockSpec((tm, N), lambda i: (i, 0)), pl.BlockSpec((K, N), lambda i: (0, 0))],
        out_specs=pl.BlockSpec((tm, K), lambda i: (i, 0)),
        out_shape=jax.ShapeDtypeStruct((T, K), BF16),
        compiler_params=_params("arbitrary"),
    )(dy, w)


def _wgrad(a, a_spec, b, b_spec, m, n, bm, bn, n_k, name, square_a=False):
    def body(a_ref, b_ref, o_ref, acc):
        k = pl.program_id(2)

        @pl.when(k == 0)
        def _():
            acc[...] = jnp.zeros_like(acc)

        av = a_ref[...]
        if square_a:
            af = av.astype(F32)
            av = (af * af).astype(BF16)
        acc[...] += _dot_tn(av, b_ref[...])

        @pl.when(k == n_k - 1)
        def _():
            o_ref[...] = acc[...].astype(BF16)

    return pl.pallas_call(
        body, name=name, grid=(m // bm, n // bn, n_k),
        in_specs=[a_spec, b_spec],
        out_specs=pl.BlockSpec((bm, bn), lambda i, j, k: (i, j)),
        out_shape=jax.ShapeDtypeStruct((m, n), BF16),
        scratch_shapes=[pltpu.VMEM((bm, bn), F32)],
        compiler_params=_params("arbitrary", "arbitrary", "arbitrary"),
    )(a, b)


def _mlp_fwd(h, g, w1, w2, layer, name, tm=1024, tf=512):
    T = h.shape[0]
    n_j = D_FF // tf

    def body(h_ref, g_ref, w1_ref, w2_ref, ho_ref, n_ref, u_ref):
        @pl.when(pl.program_id(1) == 0)
        def _():
            n_ref[...] = _rms(h_ref[...], g_ref[...]).astype(BF16)
            ho_ref[...] = h_ref[...]

        u = jnp.maximum(_dot(n_ref[...], w1_ref[...]), 0.0)
        u_ref[...] = u.astype(BF16)
        ho_ref[...] += _dot((u * u).astype(BF16), w2_ref[...])

    row = pl.BlockSpec((tm, D_MODEL), lambda i, j: (i, 0))
    return pl.pallas_call(
        body, name=name, grid=(T // tm, n_j),
        in_specs=[row, pl.BlockSpec((1, D_MODEL), lambda i, j: (0, 0)),
                  pl.BlockSpec((None, D_MODEL, tf), lambda i, j: (layer, 0, j)),
                  pl.BlockSpec((None, tf, D_MODEL), lambda i, j: (layer, j, 0))],
        out_specs=[row, row, pl.BlockSpec((tm, tf), lambda i, j: (i, j))],
        out_shape=[jax.ShapeDtypeStruct((T, D_MODEL), F32), jax.ShapeDtypeStruct((T, D_MODEL), BF16),
                   jax.ShapeDtypeStruct((T, D_FF), BF16)],
        compiler_params=_params("arbitrary", "arbitrary"),
    )(h, g, w1, w2)


def _mlp_bwd(dh, u, w1, w2, layer, h, g, name, tm=512, tf=512):
    T = h.shape[0]
    n_j = D_FF // tf

    def body(dh_ref, u_ref, w1_ref, w2_ref, h_ref, g_ref, dx_ref, dxb_ref, du_ref, dg_ref, dhb):
        acc = dx_ref
        i, j = pl.program_id(0), pl.program_id(1)

        @pl.when(j == 0)
        def _():
            dhb[...] = dh_ref[...].astype(BF16)
            acc[...] = jnp.zeros_like(acc)

        @pl.when((i == 0) & (j == 0))
        def _():
            dg_ref[...] = jnp.zeros_like(dg_ref)

        da = _dot_nt(dhb[...], w2_ref[...])
        du = (da * (2.0 * u_ref[...].astype(F32))).astype(BF16)
        du_ref[...] = du
        acc[...] += _dot_nt(du, w1_ref[...])

        @pl.when(j == n_j - 1)
        def _():
            dx, dg = _rms_bwd(acc[...], h_ref[...], g_ref[...])
            dx = dh_ref[...] + dx
            dx_ref[...] = dx
            dxb_ref[...] = dx.astype(BF16)
            dg_ref[...] += dg

    row = pl.BlockSpec((tm, D_MODEL), lambda i, j: (i, 0))
    vec = pl.BlockSpec((1, D_MODEL), lambda i, j: (0, 0))
    hid = pl.BlockSpec((tm, tf), lambda i, j: (i, j))
    return pl.pallas_call(
        body, name=name, grid=(T // tm, n_j),
        in_specs=[row, hid, pl.BlockSpec((None, D_MODEL, tf), lambda i, j: (layer, 0, j)),
                  pl.BlockSpec((None, tf, D_MODEL), lambda i, j: (layer, j, 0)), row, vec],
        out_specs=[row, row, hid, vec],
        out_shape=[jax.ShapeDtypeStruct((T, D_MODEL), F32), jax.ShapeDtypeStruct((T, D_MODEL), BF16),
                   jax.ShapeDtypeStruct((T, D_FF), BF16), jax.ShapeDtypeStruct((1, D_MODEL), F32)],
        scratch_shapes=[pltpu.VMEM((tm, D_MODEL), BF16)],
        compiler_params=_params("arbitrary", "arbitrary"),
    )(dh, u, w1, w2, h, g)


def _final_loss(h, g, target, name, tm=512):
    T = h.shape[0]

    def body(h_ref, g_ref, t_ref, dh_ref, dhb_ref, loss_ref, dg_ref):
        @pl.when(pl.program_id(0) == 0)
        def _():
            loss_ref[...] = jnp.zeros_like(loss_ref)
            dg_ref[...] = jnp.zeros_like(dg_ref)

        x = h_ref[...]
        gg = g_ref[...]
        r = lax.rsqrt(jnp.mean(x * x, axis=-1, keepdims=True) + EPS)
        xh = x * r
        e = xh * gg - t_ref[...]
        per_tok = jnp.mean(e * e, axis=-1, keepdims=True)
        loss_ref[...] += 0.5 * jnp.sum(per_tok, axis=0, keepdims=True)
        dy = e * (1.0 / D_MODEL)
        dg_ref[...] += jnp.sum(dy * xh, axis=0, keepdims=True)
        dxh = dy * gg
        dx = r * (dxh - xh * jnp.mean(dxh * xh, axis=-1, keepdims=True))
        dh_ref[...] = dx
        dhb_ref[...] = dx.astype(BF16)

    row = pl.BlockSpec((tm, D_MODEL), lambda i: (i, 0))
    vec = pl.BlockSpec((1, D_MODEL), lambda i: (0, 0))
    return pl.pallas_call(
        body, name=name, grid=(T // tm,),
        in_specs=[row, vec, row],
        out_specs=[row, row, pl.BlockSpec((8, 128), lambda i: (0, 0)), vec],
        out_shape=[jax.ShapeDtypeStruct((T, D_MODEL), F32), jax.ShapeDtypeStruct((T, D_MODEL), BF16),
                   jax.ShapeDtypeStruct((8, 128), F32), jax.ShapeDtypeStruct((1, D_MODEL), F32)],
        compiler_params=_params("arbitrary"),
    )(h, g, target)


def _ret_constants(S):
    log_gamma = jnp.log1p(-jnp.exp2(-5.0 - jnp.arange(RET_HEADS, dtype=F32)))
    idx = jnp.arange(RET_SC, dtype=F32)
    i, j = idx[:, None], idx[None, :]
    same_chunk = jnp.floor(i / CHUNK) == jnp.floor(j / CHUNK)
    mask = jnp.where((j <= i) | same_chunk, jnp.exp(log_gamma[:, None, None] * jnp.abs(i - j)[None]), 0.0)
    qdec = jnp.exp(log_gamma[:, None] * (idx + 1.0)[None, :])[:, :, None]
    kdec = jnp.exp(log_gamma[:, None] * (RET_SC - 1 - idx)[None, :])[:, :, None]
    cdec = jnp.exp(log_gamma * RET_SC)[:, None, None]
    half = RET_DK // 2
    inv = jnp.exp(-jnp.log(ROPE_BASE) * jnp.arange(half, dtype=F32) / half)
    ang = jnp.arange(S, dtype=F32)[:, None] * inv[None, :]
    return jnp.cos(ang), jnp.sin(ang), mask.astype(F32), qdec, kdec, cdec


def _rope(t, cs, sn):
    t1, t2 = t[:, :RET_DK // 2], t[:, RET_DK // 2:]
    return jnp.concatenate([t1 * cs - t2 * sn, t1 * sn + t2 * cs], axis=-1)


def _rope_bwd(d, cs, sn):
    d1, d2 = d[:, :RET_DK // 2], d[:, RET_DK // 2:]
    return jnp.concatenate([d1 * cs + d2 * sn, d2 * cs - d1 * sn], axis=-1)


def _ret_specs(B, S, reverse):
    n_sc = S // RET_SC

    def cc(c):
        return n_sc - 1 - c if reverse else c

    return dict(
        proj=pl.BlockSpec((RET_SC, RET_HEAD_COLS), lambda h, b, c: (b * n_sc + cc(c), h)),
        trig=pl.BlockSpec((RET_SC, RET_DK // 2), lambda h, b, c: (cc(c), 0)),
        mask=pl.BlockSpec((None, RET_SC, RET_SC), lambda h, b, c: (h, 0, 0)),
        dec=pl.BlockSpec((None, RET_SC, 1), lambda h, b, c: (h, 0, 0)),
        cdec=pl.BlockSpec((None, 1, 1), lambda h, b, c: (h, 0, 0)),
        gn=pl.BlockSpec((1, RET_DV), lambda h, b, c: (0, h)),
        val=pl.BlockSpec((RET_SC, RET_DV), lambda h, b, c: (b * n_sc + cc(c), h)),
        state=pl.BlockSpec((None, None, None, RET_DK, RET_DV), lambda h, b, c: (b, h, cc(c), 0, 0)),
    )


def _ret_qkvg(p_ref, cs, sn):
    q = _rope(p_ref[:, 0:RET_DK].astype(F32), cs, sn)
    k = _rope(p_ref[:, RET_DK:2 * RET_DK].astype(F32), cs, sn) * (RET_DK ** -0.5)
    v = p_ref[:, 2 * RET_DK:2 * RET_DK + RET_DV]
    gate = p_ref[:, 2 * RET_DK + RET_DV:RET_HEAD_COLS].astype(F32)
    return q, k, v, gate


def _group_norm(o):
    mu = jnp.mean(o, axis=-1, keepdims=True)
    oc = o - mu
    rstd = lax.rsqrt(jnp.mean(oc * oc, axis=-1, keepdims=True) + EPS)
    return oc * rstd, rstd


def _ret_fwd(proj, consts, gn, B, S, name):
    T = B * S
    n_sc = S // RET_SC
    cos, sin, mask, qdec, kdec, cdec = consts
    sp = _ret_specs(B, S, False)

    def body(p_ref, cos_ref, sin_ref, m_ref, qd_ref, kd_ref, cd_ref, gn_ref, y_ref, o_ref, st_ref, state):
        @pl.when(pl.program_id(2) == 0)
        def _():
            state[...] = jnp.zeros_like(state)

        q, k, v, gate = _ret_qkvg(p_ref, cos_ref[...], sin_ref[...])
        qb, kb = q.astype(BF16), k.astype(BF16)
        p = (_dot_nt(qb, kb) * m_ref[...]).astype(BF16)
        sb = state[...].astype(BF16)
        st_ref[...] = sb
        o = _dot(p, v) + _dot(qb, sb) * qd_ref[...]
        kt = (k * kd_ref[...]).astype(BF16)
        state[...] = state[...] * cd_ref[...] + _dot_tn(kt, v)
        o_ref[...] = o
        ohat, _ = _group_norm(o)
        y_ref[...] = (gate * jax.nn.sigmoid(gate) * (ohat * gn_ref[...])).astype(BF16)

    return pl.pallas_call(
        body, name=name, grid=(RET_HEADS, B, n_sc),
        in_specs=[sp["proj"], sp["trig"], sp["trig"], sp["mask"], sp["dec"], sp["dec"], sp["cdec"], sp["gn"]],
        out_specs=[sp["val"], sp["val"], sp["state"]],
        out_shape=[jax.ShapeDtypeStruct((T, RET_VW), BF16), jax.ShapeDtypeStruct((T, RET_VW), F32),
                   jax.ShapeDtypeStruct((B, RET_HEADS, n_sc, RET_DK, RET_DV), BF16)],
        scratch_shapes=[pltpu.VMEM((RET_DK, RET_DV), F32)],
        compiler_params=_params("arbitrary", "arbitrary", "arbitrary"),
    )(proj, cos, sin, mask, qdec, kdec, cdec, gn)


def _ret_bwd(proj, consts, gn, o, states, dy, B, S, name):
    T = B * S
    n_sc = S // RET_SC
    cos, sin, mask, qdec, kdec, cdec = consts
    sp = _ret_specs(B, S, True)

    def body(p_ref, cos_ref, sin_ref, m_ref, qd_ref, kd_ref, cd_ref, gn_ref, o_ref, st_ref, dy_ref,
             dp_ref, dgn_ref, dstate):
        b, c = pl.program_id(1), pl.program_id(2)

        @pl.when(c == 0)
        def _():
            dstate[...] = jnp.zeros_like(dstate)

        @pl.when((b == 0) & (c == 0))
        def _():
            dgn_ref[...] = jnp.zeros_like(dgn_ref)

        cs, sn = cos_ref[...], sin_ref[...]
        q, k, v, gate = _ret_qkvg(p_ref, cs, sn)
        qb, kb = q.astype(BF16), k.astype(BF16)
        kt = (k * kd_ref[...]).astype(BF16)
        sb = st_ref[...]
        m = m_ref[...]
        ohat, rstd = _group_norm(o_ref[...])
        gnv = gn_ref[...]
        dyv = dy_ref[...].astype(F32)
        sg = jax.nn.sigmoid(gate)
        don = dyv * (gate * sg)
        dgate = dyv * (ohat * gnv) * (sg * (1.0 + gate * (1.0 - sg)))
        dgn_ref[...] += jnp.sum(don * ohat, axis=0, keepdims=True)
        dohat = don * gnv
        do = rstd * (dohat - jnp.mean(dohat, axis=-1, keepdims=True)
                     - ohat * jnp.mean(dohat * ohat, axis=-1, keepdims=True))
        dob = do.astype(BF16)
        doq = (do * qd_ref[...]).astype(BF16)
        dsb = dstate[...].astype(BF16)
        p = (_dot_nt(qb, kb) * m).astype(BF16)
        dv = _dot_tn(p, dob) + _dot(kt, dsb)
        dkt = _dot_nt(v, dsb)
        dsc = (_dot_nt(dob, v) * m).astype(BF16)
        dq = _dot(dsc, kb) + _dot_nt(doq, sb)
        dk = (_dot_tn(dsc, qb) + dkt * kd_ref[...]) * (RET_DK ** -0.5)
        dstate[...] = dstate[...] * cd_ref[...] + _dot_tn(qb, doq)
        dp_ref[:, 0:RET_DK] = _rope_bwd(dq, cs, sn).astype(BF16)
        dp_ref[:, RET_DK:2 * RET_DK] = _rope_bwd(dk, cs, sn).astype(BF16)
        dp_ref[:, 2 * RET_DK:2 * RET_DK + RET_DV] = dv.astype(BF16)
        dp_ref[:, 2 * RET_DK + RET_DV:RET_HEAD_COLS] = dgate.astype(BF16)

    return pl.pallas_call(
        body, name=name, grid=(RET_HEADS, B, n_sc),
        in_specs=[sp["proj"], sp["trig"], sp["trig"], sp["mask"], sp["dec"], sp["dec"], sp["cdec"], sp["gn"],
                  sp["val"], sp["state"], sp["val"]],
        out_specs=[sp["proj"], sp["gn"]],
        out_shape=[jax.ShapeDtypeStruct((T, RET_HEADS * RET_HEAD_COLS), BF16), jax.ShapeDtypeStruct((1, RET_VW), F32)],
        scratch_shapes=[pltpu.VMEM((RET_DK, RET_DV), F32)],
        compiler_params=_params("arbitrary", "arbitrary", "arbitrary"),
    )(proj, cos, sin, mask, qdec, kdec, cdec, gn, o, states, dy)


def _rel_index():
    t = np.arange(ATT_EXT)
    return np.clip(t - (ATT_KW - 1), -MAX_REL, CHUNK - 1) + MAX_REL


def _bias_matrices(rel):
    idx = _rel_index()
    n_lo = int(np.argmax(idx > 0))
    n_hi = ATT_EXT - (n_lo + REL_TABLE - 1)
    ext = jnp.concatenate([jnp.broadcast_to(rel[:, :1], (ATT_HEADS, n_lo - 1)), rel,
                           jnp.broadcast_to(rel[:, -1:], (ATT_HEADS, n_hi))], axis=1)
    e_pad = jnp.pad(ext, ((0, 0), (0, 1)))
    tiled = jnp.broadcast_to(e_pad[:, None, :], (ATT_HEADS, ATT_CQ, ATT_EXT + 1))
    skew = tiled.reshape(ATT_HEADS, -1)[:, :ATT_CQ * ATT_EXT].reshape(ATT_HEADS, ATT_CQ, ATT_EXT)
    toep = skew[:, :, ATT_CQ - 1:ATT_CQ - 1 + ATT_KW]
    i = np.arange(ATT_CQ)[:, None]
    j = np.arange(ATT_KW)[None, :]
    lo = CHUNK * (i // CHUNK)
    band = np.where((j >= lo) & (j < lo + BAND_PAST + CHUNK), 0.0, NEG).astype(np.float32)
    return (toep + band[None]).reshape(ATT_PAIRS, 2, ATT_CQ, ATT_KW)


def _softmax_rows(s):
    m = jnp.max(s, axis=-1, keepdims=True)
    p = jnp.exp(s - m)
    return p * (1.0 / jnp.sum(p, axis=-1, keepdims=True))


def _attn_scores(qm, kw, bias, valid):
    s = _dot_nt(qm, kw) * (ATT_DH ** -0.5) + bias
    return jnp.where(valid, s, NEG)


def _attn_fwd(qkv3, bias, B, S, name):
    T = B * S
    n_qb = S // ATT_CQ

    def body(qkv_ref, b_ref, o_ref, kpad, vpad):
        kpad[0:BAND_PAST, :] = jnp.zeros((BAND_PAST, 128), BF16)
        vpad[0:BAND_PAST, :] = jnp.zeros((BAND_PAST, 128), BF16)
        kpad[BAND_PAST:, :] = qkv_ref[1]
        vpad[BAND_PAST:, :] = qkv_ref[2]
        lane = lax.broadcasted_iota(jnp.int32, (1, 128), 1)
        col = lax.broadcasted_iota(jnp.int32, (1, ATT_KW), 1)

        def step(qb, carry):
            qs = pl.multiple_of(qb * ATT_CQ, ATT_CQ)
            q = qkv_ref[0, pl.ds(qs, ATT_CQ), :]
            kw = kpad[pl.ds(qs, ATT_KW), :]
            vw = vpad[pl.ds(qs, ATT_KW), :]
            valid = (col + qs) >= BAND_PAST
            outs = []
            for hh in range(2):
                sel = (lane >= ATT_DH * hh) & (lane < ATT_DH * (hh + 1))
                qm = jnp.where(sel, q, jnp.zeros_like(q))
                p = _softmax_rows(_attn_scores(qm, kw, b_ref[hh], valid))
                outs.append(_dot(p.astype(BF16), vw))
            o_ref[pl.ds(qs, ATT_CQ), :] = jnp.where(lane < ATT_DH, outs[0], outs[1]).astype(BF16)
            return carry

        lax.fori_loop(0, n_qb, step, 0)

    return pl.pallas_call(
        body, name=name, grid=(ATT_PAIRS, B),
        in_specs=[pl.BlockSpec((3, S, 128), lambda hp, b: (0, b, hp)),
                  pl.BlockSpec((None, 2, ATT_CQ, ATT_KW), lambda hp, b: (hp, 0, 0, 0))],
        out_specs=pl.BlockSpec((S, 128), lambda hp, b: (b, hp)),
        out_shape=jax.ShapeDtypeStruct((T, D_MODEL), BF16),
        scratch_shapes=[pltpu.VMEM((S + BAND_PAST, 128), BF16), pltpu.VMEM((S + BAND_PAST, 128), BF16)],
        compiler_params=_params("arbitrary", "arbitrary"),
    )(qkv3, bias)


def _attn_bwd(qkv3, bias, do, B, S, name):
    T = B * S
    n_qb = S // ATT_CQ
    scale = ATT_DH ** -0.5

    def body(qkv_ref, b_ref, do_ref, dqkv_ref, db_ref, kpad, vpad, dkacc, dvacc):
        @pl.when(pl.program_id(1) == 0)
        def _():
            db_ref[...] = jnp.zeros_like(db_ref)

        kpad[0:BAND_PAST, :] = jnp.zeros((BAND_PAST, 128), BF16)
        vpad[0:BAND_PAST, :] = jnp.zeros((BAND_PAST, 128), BF16)
        kpad[BAND_PAST:, :] = qkv_ref[1]
        vpad[BAND_PAST:, :] = qkv_ref[2]
        dkacc[...] = jnp.zeros_like(dkacc)
        dvacc[...] = jnp.zeros_like(dvacc)
        lane = lax.broadcasted_iota(jnp.int32, (1, 128), 1)
        col = lax.broadcasted_iota(jnp.int32, (1, ATT_KW), 1)

        def step(qb, carry):
            qs = pl.multiple_of(qb * ATT_CQ, ATT_CQ)
            q = qkv_ref[0, pl.ds(qs, ATT_CQ), :]
            dov = do_ref[pl.ds(qs, ATT_CQ), :]
            kw = kpad[pl.ds(qs, ATT_KW), :]
            vw = vpad[pl.ds(qs, ATT_KW), :]
            valid = (col + qs) >= BAND_PAST
            dqs, dk, dv = [], None, None
            for hh in range(2):
                sel = (lane >= ATT_DH * hh) & (lane < ATT_DH * (hh + 1))
                qm = jnp.where(sel, q, jnp.zeros_like(q))
                dom = jnp.where(sel, dov, jnp.zeros_like(dov))
                p = _softmax_rows(_attn_scores(qm, kw, b_ref[hh], valid))
                dp = _dot_nt(dom, vw)
                ds = p * (dp - jnp.sum(dp * p, axis=-1, keepdims=True))
                db_ref[hh] += ds
                dsb = (ds * scale).astype(BF16)
                dqs.append(_dot(dsb, kw))
                dk_h = _dot_tn(dsb, qm)
                dv_h = _dot_tn(p.astype(BF16), dom)
                dk = dk_h if dk is None else dk + dk_h
                dv = dv_h if dv is None else dv + dv_h
            dqkv_ref[0, pl.ds(qs, ATT_CQ), :] = jnp.where(lane < ATT_DH, dqs[0], dqs[1]).astype(BF16)
            dkacc[pl.ds(qs, ATT_KW), :] += dk
            dvacc[pl.ds(qs, ATT_KW), :] += dv
            return carry

        lax.fori_loop(0, n_qb, step, 0)
        dqkv_ref[1] = dkacc[BAND_PAST:, :].astype(BF16)
        dqkv_ref[2] = dvacc[BAND_PAST:, :].astype(BF16)

    bias_spec = pl.BlockSpec((None, 2, ATT_CQ, ATT_KW), lambda hp, b: (hp, 0, 0, 0))
    qkv_spec = pl.BlockSpec((3, S, 128), lambda hp, b: (0, b, hp))
    return pl.pallas_call(
        body, name=name, grid=(ATT_PAIRS, B),
        in_specs=[qkv_spec, bias_spec, pl.BlockSpec((S, 128), lambda hp, b: (b, hp))],
        out_specs=[qkv_spec, bias_spec],
        out_shape=[jax.ShapeDtypeStruct((3, T, D_MODEL), BF16),
                   jax.ShapeDtypeStruct((ATT_PAIRS, 2, ATT_CQ, ATT_KW), F32)],
        scratch_shapes=[pltpu.VMEM((S + BAND_PAST, 128), BF16), pltpu.VMEM((S + BAND_PAST, 128), BF16),
                        pltpu.VMEM((S + BAND_PAST, 128), F32), pltpu.VMEM((S + BAND_PAST, 128), F32)],
        compiler_params=_params("arbitrary", "arbitrary"),
    )(qkv3, bias, do)


def _rel_bias_grad(dbias, name):
    d = dbias.reshape(ATT_HEADS, ATT_CQ, ATT_KW)
    d = jnp.pad(d, ((0, 0), (0, 0), (ATT_CQ - 1, 0))).reshape(ATT_HEADS, ATT_CQ * ATT_EXT)
    skew = jnp.pad(d, ((0, 0), (0, ATT_CQ))).reshape(ATT_HEADS, ATT_CQ, ATT_EXT + 1)
    fold = np.zeros((ATT_EXT + 1, REL_PAD), np.float32)
    fold[np.arange(ATT_EXT), _rel_index()] = 1.0
    fold = jnp.asarray(fold, BF16)

    def body(s_ref, f_ref, o_ref):
        x = jnp.sum(s_ref[...], axis=0, keepdims=True)
        x = jnp.broadcast_to(x, (8, ATT_EXT + 1))
        acc = jnp.zeros((8, REL_PAD), F32)
        for _ in range(3):
            part = x.astype(BF16)
            acc = acc + _dot(part, f_ref[...])
            x = x - part.astype(F32)
        o_ref[...] = acc[0:1, :]

    out = pl.pallas_call(
        body, name=name, grid=(ATT_HEADS,),
        in_specs=[pl.BlockSpec((None, ATT_CQ, ATT_EXT + 1), lambda h: (h, 0, 0)),
                  pl.BlockSpec((ATT_EXT + 1, REL_PAD), lambda h: (0, 0))],
        out_specs=pl.BlockSpec((None, 1, REL_PAD), lambda h: (h, 0, 0)),
        out_shape=jax.ShapeDtypeStruct((ATT_HEADS, 1, REL_PAD), F32),
        compiler_params=_params("arbitrary"),
    )(skew, fold)
    return out.reshape(ATT_HEADS, REL_PAD)


def _place():
    return lax.axis_index("x"), lax.axis_index("y"), lax.axis_index("c")


def _lin(p):
    return 4 * p[0] + 2 * p[1] + p[2]


def _cols(width):
    return lambda ref, blk: ref.at[:, pl.ds(pl.multiple_of(blk * width, 128), width)]


def _rows(height):
    return lambda ref, blk: ref.at[pl.ds(pl.multiple_of(blk * height, 8), height), :]


def _all_gather(shards, out_shapes, views, name):
    n = len(shards)

    def body(*refs):
        ins, outs = refs[:n], refs[n:2 * n]
        send_sems, recv_sems, local_sems = refs[2 * n:]
        x, y, c = _place()
        me, sibling = (x, y, c), (x, y, 1 - c)
        chips = [(1 - x, y), (x, 1 - y), (1 - x, 1 - y)]

        def copy(a, k, block, to, src=None):
            dst = views[a](outs[a], _lin(block))
            return pltpu.make_async_remote_copy(
                src_ref=dst if src is None else src, dst_ref=dst,
                send_sem=send_sems.at[a * 7 + k], recv_sem=recv_sems.at[a * 7 + k],
                device_id=to, device_id_type=MESH)

        mine = [pltpu.make_async_copy(ins[a], views[a](outs[a], _lin(me)), local_sems.at[a]) for a in range(n)]
        for cp in mine:
            cp.start()
        first = []
        for a in range(n):
            first.append(copy(a, 0, me, sibling, src=ins[a]))
            first += [copy(a, 1 + j, me, (*chip, c), src=ins[a]) for j, chip in enumerate(chips)]
        for cp in first:
            cp.start()
        passed = []
        for j, chip in enumerate(chips):
            for a in range(n):
                copy(a, 1 + j, (*chip, c), me).wait_recv()
                fwd = copy(a, 4 + j, (*chip, c), sibling)
                fwd.start()
                passed.append(fwd)
        for a in range(n):
            copy(a, 0, sibling, me).wait_recv()
            for j, chip in enumerate(chips):
                copy(a, 4 + j, (*chip, 1 - c), me).wait_recv()
        for cp in first + passed:
            cp.wait_send()
        for cp in mine:
            cp.wait()

    return pl.pallas_call(
        body, name=name,
        in_specs=[ANY] * n, out_specs=[ANY] * n, out_shape=out_shapes,
        scratch_shapes=[pltpu.SemaphoreType.DMA((7 * n,)), pltpu.SemaphoreType.DMA((7 * n,)),
                        pltpu.SemaphoreType.DMA((n,))],
    )(*shards)


def _reduce_scatter_send(sources, flows, land_shapes, name):
    n_src, n_land, n_f = len(sources), len(land_shapes), len(flows)

    def body(*refs):
        srcs, lands = refs[:n_src], refs[n_src:n_src + n_land]
        send_sems, recv_sems, local_sems = refs[n_src + n_land:]
        x, y, c = _place()
        me = (x, y, c)
        peers = [(x ^ dx, y ^ dy, c ^ dc) for dx in (0, 1) for dy in (0, 1) for dc in (0, 1)][1:]

        def copy(f, k, sender, to):
            si, sview, li, lview = flows[f]
            return pltpu.make_async_remote_copy(
                src_ref=sview(srcs[si], _lin(to)), dst_ref=lview(lands[li], _lin(sender)),
                send_sem=send_sems.at[f * 7 + k], recv_sem=recv_sems.at[f * 7 + k],
                device_id=to, device_id_type=MESH)

        mine = []
        for f, (si, sview, li, lview) in enumerate(flows):
            mine.append(pltpu.make_async_copy(sview(srcs[si], _lin(me)), lview(lands[li], _lin(me)), local_sems.at[f]))
        for cp in mine:
            cp.start()
        sends = [copy(f, k, me, peer) for f in range(n_f) for k, peer in enumerate(peers)]
        for cp in sends:
            cp.start()
        for f in range(n_f):
            for k, peer in enumerate(peers):
                copy(f, k, peer, me).wait_recv()
        for cp in sends:
            cp.wait_send()
        for cp in mine:
            cp.wait()

    return pl.pallas_call(
        body, name=name,
        in_specs=[ANY] * n_src, out_specs=[ANY] * n_land, out_shape=land_shapes,
        scratch_shapes=[pltpu.SemaphoreType.DMA((7 * n_f,)), pltpu.SemaphoreType.DMA((7 * n_f,)),
                        pltpu.SemaphoreType.DMA((n_f,))],
    )(*sources)


def _adamw(w, g, m, v):
    m = ADAM_B1 * m + (1.0 - ADAM_B1) * g
    v = ADAM_B2 * v + (1.0 - ADAM_B2) * (g * g)
    m_hat = m / (1.0 - ADAM_B1 ** ADAM_STEP)
    v_hat = v / (1.0 - ADAM_B2 ** ADAM_STEP)
    delta = -ADAM_LR * (m_hat / (jnp.sqrt(v_hat) + ADAM_EPS) + ADAM_WD * w)
    return delta, m, v


def _sum_devices(ref):
    g = ref[0].astype(F32)
    for d in range(1, N_DEV):
        g = g + ref[d].astype(F32)
    return g


def _adamw_reduce(land, w, m, v, name, tr=256):
    R, C = w.shape
    tr = min(tr, R)

    def body(l_ref, w_ref, m_ref, v_ref, g_out, d_out, m_out, v_out):
        g = _sum_devices(l_ref)
        g_out[...] = g
        d_out[...], m_out[...], v_out[...] = _adamw(w_ref[...], g, m_ref[...], v_ref[...])

    blk = pl.BlockSpec((tr, C), lambda i: (i, 0))
    return pl.pallas_call(
        body, name=name, grid=(R // tr,),
        in_specs=[pl.BlockSpec((N_DEV, tr, C), lambda i: (0, i, 0)), blk, blk, blk],
        out_specs=[blk] * 4, out_shape=[jax.ShapeDtypeStruct((R, C), F32)] * 4,
        compiler_params=_params("arbitrary"),
    )(land, w, m, v)


def _small_update(land_small, land_rel, w, m, v, name):
    def body(ls_ref, lr_ref, w_ref, m_ref, v_ref, g_out, d_out, m_out, v_out, rel_out):
        g = _sum_devices(ls_ref)
        g_out[...] = g
        d_out[...], m_out[...], v_out[...] = _adamw(w_ref[...], g, m_ref[...], v_ref[...])
        rel_out[...] = _sum_devices(lr_ref)

    return pl.pallas_call(
        body, name=name,
        out_shape=[jax.ShapeDtypeStruct(w.shape, F32)] * 4 + [jax.ShapeDtypeStruct(land_rel.shape[1:], F32)],
    )(land_small, land_rel, w, m, v)


def _adamw_plain(w, g, m, v, name):
    def body(w_ref, g_ref, m_ref, v_ref, d_out, m_out, v_out):
        d_out[...], m_out[...], v_out[...] = _adamw(w_ref[...], g_ref[...], m_ref[...], v_ref[...])

    return pl.pallas_call(body, name=name, out_shape=[jax.ShapeDtypeStruct(w.shape, F32)] * 3)(w, g, m, v)


def _ret_pieces():
    return [(RET_DK, lambda h: h, 0),
            (RET_DK, lambda h: RET_HEADS + h, RET_DK),
            (RET_DV, lambda h: RET_HEADS + h, 2 * RET_DK),
            (RET_DV, lambda h: 2 * RET_HEADS + h, 2 * RET_DK + RET_DV)]


def _ret_piece_of_column_block(p):
    per_head = RET_HEAD_COLS // RET_DK
    qk = jnp.where(p < RET_HEADS, per_head * p, per_head * (p - RET_HEADS) + 1)
    pv = p - 2 * RET_HEADS
    vv = per_head * (pv // 2) + 2 + pv % 2
    pg = p - 4 * RET_HEADS
    gg = per_head * (pg // 2) + 4 + pg % 2
    return jnp.where(p < 2 * RET_HEADS, qk, jnp.where(p < 4 * RET_HEADS, vv, gg))


def _local_step(x, target, w_rin, w_rout, w_ain, w_aout, w1, w2, mix_g, gn_g, rel, mlp_g, final_g):
    B, S, _ = x.shape
    T = B * S
    tk = min(T, 2048)
    n_k = T // tk
    h0 = x.reshape(T, D_MODEL)
    tgt = target.reshape(T, D_MODEL)
    consts = _ret_constants(S)
    bias = _bias_matrices(rel)
    att_pieces = [(D_MODEL, lambda j: j, 0)]


    def qkv_spec(tm):
        return pl.BlockSpec((None, tm, D_MODEL), lambda i, j: (j, i, 0))

    def head_spec(tm):
        return pl.BlockSpec((tm, RET_HEAD_COLS), lambda i, j: (i, j))

    proj, n0 = _norm_proj(h0, mix_g[0:1], w_rin, _ret_pieces(), RET_HEADS,
                          jax.ShapeDtypeStruct((T, RET_HEADS * RET_HEAD_COLS), BF16), head_spec, "ret_proj")
    y, o, states = _ret_fwd(proj, consts, gn_g, B, S, "ret_fwd")
    h1 = _matmul_res(y, w_rout, h0, "ret_out")
    h2, n1, u1 = _mlp_fwd(h1, mlp_g[0:1], w1, w2, 0, "mlp0_fwd")
    qkv3, n2 = _norm_proj(h2, mix_g[1:2], w_ain, att_pieces, 3,
                          jax.ShapeDtypeStruct((3, T, D_MODEL), BF16), qkv_spec, "att_proj")
    o2 = _attn_fwd(qkv3, bias, B, S, "att_fwd")
    h3 = _matmul_res(o2, w_aout, h2, "att_out")
    h4, n3, u3 = _mlp_fwd(h3, mlp_g[1:2], w1, w2, 1, "mlp1_fwd")
    dh4, dh4b, loss, dg_final = _final_loss(h4, final_g, tgt, "final_loss")

    def tok(width):
        return dict(a=pl.BlockSpec((tk, width), lambda i, j, k: (k, i)), b=pl.BlockSpec((tk, width), lambda i, j, k: (k, j)))

    def mlp_grads(dh, dhb, u, n, layer, h, g, tag):
        dhi, dhib, du, dg = _mlp_bwd(dh, u, w1, w2, layer, h, g, tag + "_bwd")
        gw2 = _wgrad(u, tok(1024)["a"], dhb, tok(1024)["b"], D_FF, D_MODEL, 1024, 1024, n_k, tag + "_dw2", square_a=True)
        gw1 = _wgrad(n, tok(1024)["a"], du, tok(1024)["b"], D_MODEL, D_FF, 1024, 1024, n_k, tag + "_dw1")
        return dhi, dhib, gw1, gw2, dg

    dh3, dh3b, gw1_1, gw2_1, dg_mlp1 = mlp_grads(dh4, dh4b, u3, n3, 1, h3, mlp_g[1:2], "mlp1")
    do2 = _matmul_nt(dh3b, w_aout, "att_out_bwd")
    g_aout = _wgrad(o2, tok(1024)["a"], dh3b, tok(1024)["b"], D_MODEL, D_MODEL, 1024, 1024, n_k, "att_out_dw")
    dqkv3, dbias = _attn_bwd(qkv3, bias, do2, B, S, "att_bwd")
    g_rel = _rel_bias_grad(dbias, "rel_bias_grad")
    dh2, dh2b, dg_mix1 = _proj_bwd(dqkv3, qkv_spec, w_ain, att_pieces, 3, h2, mix_g[1:2], dh3, "att_proj_bwd")
    g_ain = _wgrad(n2, tok(1024)["a"], dqkv3, pl.BlockSpec((None, tk, D_MODEL), lambda i, j, k: (j, k, 0)),
                   D_MODEL, 3 * D_MODEL, 1024, 1024, n_k, "att_proj_dw")
    dh1, dh1b, gw1_0, gw2_0, dg_mlp0 = mlp_grads(dh2, dh2b, u1, n1, 0, h1, mlp_g[0:1], "mlp0")
    dy = _matmul_nt(dh1b, w_rout, "ret_out_bwd")
    g_rout = _wgrad(y, tok(1024)["a"], dh1b, tok(1024)["b"], RET_VW, D_MODEL, 1024, 1024, n_k, "ret_out_dw")
    dproj, dgn = _ret_bwd(proj, consts, gn_g, o, states, dy, B, S, "ret_bwd")
    dx, _, dg_mix0 = _proj_bwd(dproj, head_spec, w_rin, _ret_pieces(), RET_HEADS, h0, mix_g[0:1], dh1, "ret_proj_bwd")
    g_rin = _wgrad(n0, tok(1024)["a"], dproj,
                   pl.BlockSpec((tk, RET_DK), lambda i, j, k: (k, _ret_piece_of_column_block(j))),
                   D_MODEL, RET_IN, 1024, RET_DK, n_k, "ret_proj_dw")

    small = jnp.concatenate([dg_mix0, dg_mix1, dg_mlp0, dg_mlp1, dg_final, dgn.reshape(2, D_MODEL),
                             jnp.zeros((1, D_MODEL), F32)], axis=0)
    big = (g_rin, g_rout, g_ain, g_aout, gw1_0, gw1_1, gw2_0, gw2_1)
    return loss[0, 0], dx.reshape(B, S, D_MODEL), big, small, g_rel


def kernel(x, mix_norm_g, ret_w_in, ret_gn_g, ret_w_out, att_w_in, att_rel_bias, att_w_out, mlp_norm_g, mlp_w1, mlp_w2, final_norm_g, loss_target, m_mix_norm_g, m_ret_w_in, m_ret_gn_g, m_ret_w_out, m_att_w_in, m_att_rel_bias, m_att_w_out, m_mlp_norm_g, m_mlp_w1, m_mlp_w2, m_final_norm_g, v_mix_norm_g, v_ret_w_in, v_ret_gn_g, v_ret_w_out, v_att_w_in, v_att_rel_bias, v_att_w_out, v_mlp_norm_g, v_mlp_w1, v_mlp_w2, v_final_norm_g):
    me = _lin(_place())
    rel_cols = REL_TABLE // N_DEV

    shards = [ret_w_in[0].astype(BF16), ret_w_out[0].astype(BF16), att_w_in[0].astype(BF16),
              att_w_out[0].astype(BF16), mlp_w1.astype(BF16), mlp_w2.astype(BF16), att_rel_bias[0]]
    full_shapes = [jax.ShapeDtypeStruct((D_MODEL, RET_IN), BF16), jax.ShapeDtypeStruct((RET_VW, D_MODEL), BF16),
                   jax.ShapeDtypeStruct((D_MODEL, 3 * D_MODEL), BF16), jax.ShapeDtypeStruct((D_MODEL, D_MODEL), BF16),
                   jax.ShapeDtypeStruct((2, D_MODEL, D_FF), BF16), jax.ShapeDtypeStruct((2, D_FF, D_MODEL), BF16),
                   jax.ShapeDtypeStruct((N_DEV, ATT_HEADS, rel_cols), F32)]
    w1_cols, w2_rows = D_FF // N_DEV, D_FF // N_DEV
    views = [_cols(RET_IN // N_DEV), _rows(RET_VW // N_DEV), _cols(3 * D_MODEL // N_DEV), _rows(D_MODEL // N_DEV),
             lambda ref, blk: ref.at[:, :, pl.ds(pl.multiple_of(blk * w1_cols, 128), w1_cols)],
             lambda ref, blk: ref.at[:, pl.ds(pl.multiple_of(blk * w2_rows, 8), w2_rows), :],
             lambda ref, blk: ref.at[blk]]
    w_rin, w_rout, w_ain, w_aout, w1, w2, rel_all = _all_gather(shards, full_shapes, views, "gather_weights")
    rel = rel_all.transpose(1, 0, 2).reshape(ATT_HEADS, REL_TABLE)

    loss_part, grad_x, big, small, g_rel = _local_step(
        x, loss_target, w_rin, w_rout, w_ain, w_aout, w1, w2, mix_norm_g, ret_gn_g, rel, mlp_norm_g,
        final_norm_g.reshape(1, D_MODEL))
    loss = lax.psum(loss_part, ("x", "y", "c"))

    slot = lambda ref, blk: ref.at[blk]
    whole = lambda ref, blk: ref
    layer_slot = lambda l: (lambda ref, blk: ref.at[blk, l])
    flows = [(0, _cols(RET_IN // N_DEV), 0, slot), (1, _rows(RET_VW // N_DEV), 1, slot),
             (2, _cols(3 * D_MODEL // N_DEV), 2, slot), (3, _rows(D_MODEL // N_DEV), 3, slot),
             (4, _cols(w1_cols), 4, layer_slot(0)), (5, _cols(w1_cols), 4, layer_slot(1)),
             (6, _rows(w2_rows), 5, layer_slot(0)), (7, _rows(w2_rows), 5, layer_slot(1)),
             (8, whole, 6, slot), (9, whole, 7, slot)]
    land_shapes = [jax.ShapeDtypeStruct((N_DEV, D_MODEL, RET_IN // N_DEV), BF16),
                   jax.ShapeDtypeStruct((N_DEV, RET_VW // N_DEV, D_MODEL), BF16),
                   jax.ShapeDtypeStruct((N_DEV, D_MODEL, 3 * D_MODEL // N_DEV), BF16),
                   jax.ShapeDtypeStruct((N_DEV, D_MODEL // N_DEV, D_MODEL), BF16),
                   jax.ShapeDtypeStruct((N_DEV, 2, D_MODEL, w1_cols), BF16),
                   jax.ShapeDtypeStruct((N_DEV, 2, w2_rows, D_MODEL), BF16),
                   jax.ShapeDtypeStruct((N_DEV, 8, D_MODEL), F32),
                   jax.ShapeDtypeStruct((N_DEV, ATT_HEADS, REL_PAD), F32)]
    l_rin, l_rout, l_ain, l_aout, l_w1, l_w2, l_small, l_rel = _reduce_scatter_send(
        list(big) + [small, g_rel], flows, land_shapes, "scatter_grads")

    def update(land, w, m, v, name):
        shape = w.shape
        w2d = w.reshape(-1, shape[-1])
        outs = _adamw_reduce(land.reshape(N_DEV, *w2d.shape), w2d, m.reshape(w2d.shape), v.reshape(w2d.shape), name)
        return [t.reshape(shape) for t in outs]

    u_rin = update(l_rin, ret_w_in, m_ret_w_in, v_ret_w_in, "update_ret_w_in")
    u_rout = update(l_rout, ret_w_out, m_ret_w_out, v_ret_w_out, "update_ret_w_out")
    u_ain = update(l_ain, att_w_in, m_att_w_in, v_att_w_in, "update_att_w_in")
    u_aout = update(l_aout, att_w_out, m_att_w_out, v_att_w_out, "update_att_w_out")
    u_w1 = update(l_w1, mlp_w1, m_mlp_w1, v_mlp_w1, "update_mlp_w1")
    u_w2 = update(l_w2, mlp_w2, m_mlp_w2, v_mlp_w2, "update_mlp_w2")

    def pack(mix, mlp, fin, gn):
        return jnp.concatenate([mix, mlp, fin.reshape(1, D_MODEL), gn.reshape(2, D_MODEL), jnp.zeros((1, D_MODEL), F32)], axis=0)

    small_w = pack(mix_norm_g, mlp_norm_g, final_norm_g, ret_gn_g)
    small_m = pack(m_mix_norm_g, m_mlp_norm_g, m_final_norm_g, m_ret_gn_g)
    small_v = pack(v_mix_norm_g, v_mlp_norm_g, v_final_norm_g, v_ret_gn_g)
    sg, sd, sm, sv, rel_sum = _small_update(l_small, l_rel, small_w, small_m, small_v, "update_small")
    g_rel_mine = lax.dynamic_slice(rel_sum, (0, me * rel_cols), (ATT_HEADS, rel_cols))
    rel_d, rel_m, rel_v = _adamw_plain(att_rel_bias[0], g_rel_mine, m_att_rel_bias[0], v_att_rel_bias[0], "update_rel_bias")
    u_rel = [g_rel_mine[None], rel_d[None], rel_m[None], rel_v[None]]

    def unpack(t):
        return dict(mix=t[0:2], mlp=t[2:4], fin=t[4], gn=t[5:7].reshape(1, RET_VW))

    us = [unpack(t) for t in (sg, sd, sm, sv)]
    outs = [loss, grad_x]
    for k in range(4):
        outs += [us[k]["mix"], u_rin[k], us[k]["gn"], u_rout[k], u_ain[k], u_rel[k], u_aout[k], us[k]["mlp"],
                 u_w1[k], u_w2[k], us[k]["fin"]]
    return tuple(outs)
```

```python
import functools

import numpy as np
import jax
import jax.numpy as jnp
from jax import lax
from jax.experimental import pallas as pl
from jax.experimental.pallas import tpu as pltpu

F32, BF16 = jnp.float32, jnp.bfloat16

D_MODEL = 1024
CHUNK = 64
RET_HEADS, RET_DK, RET_DV = 4, 256, 512
RET_QK, RET_VW = RET_HEADS * RET_DK, RET_HEADS * RET_DV
RET_IN = 2 * RET_QK + 2 * RET_VW
RET_HEAD_COLS = 2 * RET_DK + 2 * RET_DV
RET_SC = 256
ROPE_BASE = 10000.0
ATT_HEADS, ATT_DH = 16, 64
LANES = 128
ATT_PAIRS = ATT_HEADS * ATT_DH // LANES
BAND_PAST = 8 * CHUNK
MAX_REL = 256
REL_TABLE = MAX_REL + CHUNK
REL_PAD = 384
ATT_CQ = 256
ATT_KW = ATT_CQ + BAND_PAST
ATT_EXT = ATT_CQ + ATT_KW - 1
D_FF = 4 * D_MODEL
EPS = 1e-6
NEG = -1e30
N_DEV = 8
N_PEERS = N_DEV - 1

ADAM_LR, ADAM_B1, ADAM_B2, ADAM_EPS, ADAM_WD, ADAM_STEP = 0.001, 0.9, 0.999, 1e-08, 0.01, 10

VMEM_LIMIT = 48 * 1024 * 1024
MESH = pl.DeviceIdType.MESH
ANY = pl.BlockSpec(memory_space=pl.ANY)


def _dot(a, b):
    return jnp.dot(a, b, preferred_element_type=F32)


def _dot_nt(a, b):
    return lax.dot_general(a, b, (((1,), (1,)), ((), ())), preferred_element_type=F32)


def _dot_tn(a, b):
    return lax.dot_general(a, b, (((0,), (0,)), ((), ())), preferred_element_type=F32)


def _rms(x, g):
    r = lax.rsqrt(jnp.mean(x * x, axis=-1, keepdims=True) + EPS)
    return x * r * g


def _rms_bwd(dn, x, g):
    r = lax.rsqrt(jnp.mean(x * x, axis=-1, keepdims=True) + EPS)
    xh = x * r
    dg = jnp.sum(dn * xh, axis=0, keepdims=True)
    dxh = dn * g
    dx = r * (dxh - xh * jnp.mean(dxh * xh, axis=-1, keepdims=True))
    return dx, dg


def _place():
    return lax.axis_index("x"), lax.axis_index("y"), lax.axis_index("c")


def _lin(p):
    return 4 * p[0] + 2 * p[1] + p[2]


def _cols(width):
    return lambda ref, blk: ref.at[:, pl.ds(pl.multiple_of(blk * width, LANES), width)]


def _rows(height):
    return lambda ref, blk: ref.at[pl.ds(pl.multiple_of(blk * height, 8), height), :]


def _slot(ref, blk):
    return ref.at[blk]


def _whole(ref, blk):
    return ref


class _Exchange:
    def __init__(self, sources, flows, land_shapes):
        self.sources, self.flows, self.land_shapes = list(sources), flows, list(land_shapes)
        n_f = len(flows)
        self.sem_shapes = [pltpu.SemaphoreType.DMA((N_PEERS * n_f,)), pltpu.SemaphoreType.DMA((N_PEERS * n_f,)),
                           pltpu.SemaphoreType.DMA((n_f,))]

    def _copies(self, srcs, lands, sems):
        send_sems, recv_sems, local_sems = sems
        x, y, c = _place()
        me = (x, y, c)
        peers = [(x ^ dx, y ^ dy, c ^ dc) for dx in (0, 1) for dy in (0, 1) for dc in (0, 1)][1:]

        def copy(f, k, sender, to):
            si, sview, li, lview = self.flows[f]
            return pltpu.make_async_remote_copy(
                src_ref=sview(srcs[si], _lin(to)), dst_ref=lview(lands[li], _lin(sender)),
                send_sem=send_sems.at[f * N_PEERS + k], recv_sem=recv_sems.at[f * N_PEERS + k],
                device_id=to, device_id_type=MESH)

        mine, sends, recvs = [], [], []
        for f, (si, sview, li, lview) in enumerate(self.flows):
            mine.append(pltpu.make_async_copy(sview(srcs[si], _lin(me)), lview(lands[li], _lin(me)), local_sems.at[f]))
            for k, peer in enumerate(peers):
                sends.append(copy(f, k, me, peer))
                recvs.append(copy(f, k, peer, me))
        return mine, sends, recvs

    def start(self, srcs, lands, sems):
        mine, sends, _ = self._copies(srcs, lands, sems)
        for cp in mine + sends:
            cp.start()

    def finish(self, srcs, lands, sems):
        mine, sends, recvs = self._copies(srcs, lands, sems)
        for cp in recvs:
            cp.wait_recv()
        for cp in sends:
            cp.wait_send()
        for cp in mine:
            cp.wait()

    def run(self, name):
        n_src, n_land = len(self.sources), len(self.land_shapes)

        def body(*refs):
            srcs, lands, sems = refs[:n_src], refs[n_src:n_src + n_land], refs[n_src + n_land:]
            self.start(srcs, lands, sems)
            self.finish(srcs, lands, sems)

        return pl.pallas_call(body, name=name, in_specs=[ANY] * n_src, out_specs=[ANY] * n_land,
                              out_shape=self.land_shapes, scratch_shapes=self.sem_shapes)(*self.sources)


def _call(body, *, name, grid, in_specs, out_specs, out_shape, args, scratch_shapes=(), ride=None):
    params = pltpu.CompilerParams(dimension_semantics=("arbitrary",) * len(grid), vmem_limit_bytes=VMEM_LIMIT)
    in_specs, out_specs, out_shape, scratch_shapes = list(in_specs), list(out_specs), list(out_shape), list(scratch_shapes)
    if ride is None:
        return pl.pallas_call(body, name=name, grid=grid, in_specs=in_specs, out_specs=out_specs, out_shape=out_shape,
                              scratch_shapes=scratch_shapes, compiler_params=params)(*args)
    n_in, n_out, n_scr = len(in_specs), len(out_specs), len(scratch_shapes)
    n_src, n_land = len(ride.sources), len(ride.land_shapes)

    def riding(*refs):
        bounds = np.cumsum([n_in, n_src, n_out, n_land, n_scr])
        ins, srcs, outs, lands, scr, sems = (refs[a:b] for a, b in zip([0, *bounds], [*bounds, len(refs)]))
        ids = [pl.program_id(d) for d in range(len(grid))]
        first = functools.reduce(lambda a, b: a & b, [i == 0 for i in ids])
        last = functools.reduce(lambda a, b: a & b, [i == n - 1 for i, n in zip(ids, grid)])

        @pl.when(first)
        def _():
            ride.start(srcs, lands, sems)

        body(*ins, *outs, *scr)

        @pl.when(last)
        def _():
            ride.finish(srcs, lands, sems)

    res = pl.pallas_call(
        riding, name=name, grid=grid, in_specs=in_specs + [ANY] * n_src, out_specs=out_specs + [ANY] * n_land,
        out_shape=out_shape + ride.land_shapes, scratch_shapes=scratch_shapes + ride.sem_shapes,
        compiler_params=params)(*args, *ride.sources)
    return res[:n_out], res[n_out:]


def _all_gather(shards, out_shapes, views, name):
    n = len(shards)

    def body(*refs):
        ins, outs = refs[:n], refs[n:2 * n]
        send_sems, recv_sems, local_sems = refs[2 * n:]
        x, y, c = _place()
        me, sibling = (x, y, c), (x, y, 1 - c)
        chips = [(1 - x, y), (x, 1 - y), (1 - x, 1 - y)]

        def copy(a, k, block, to, src=None):
            dst = views[a](outs[a], _lin(block))
            return pltpu.make_async_remote_copy(
                src_ref=dst if src is None else src, dst_ref=dst,
                send_sem=send_sems.at[a * N_PEERS + k], recv_sem=recv_sems.at[a * N_PEERS + k],
                device_id=to, device_id_type=MESH)

        mine = [pltpu.make_async_copy(ins[a], views[a](outs[a], _lin(me)), local_sems.at[a]) for a in range(n)]
        for cp in mine:
            cp.start()
        first = []
        for a in range(n):
            first.append(copy(a, 0, me, sibling, src=ins[a]))
            first += [copy(a, 1 + j, me, (*chip, c), src=ins[a]) for j, chip in enumerate(chips)]
        for cp in first:
            cp.start()
        passed = []
        for j, chip in enumerate(chips):
            for a in range(n):
                copy(a, 1 + j, (*chip, c), me).wait_recv()
                fwd = copy(a, 4 + j, (*chip, c), sibling)
                fwd.start()
                passed.append(fwd)
        for a in range(n):
            copy(a, 0, sibling, me).wait_recv()
            for j, chip in enumerate(chips):
                copy(a, 4 + j, (*chip, 1 - c), me).wait_recv()
        for cp in first + passed:
            cp.wait_send()
        for cp in mine:
            cp.wait()

    return pl.pallas_call(
        body, name=name,
        in_specs=[ANY] * n, out_specs=[ANY] * n, out_shape=out_shapes,
        scratch_shapes=[pltpu.SemaphoreType.DMA((N_PEERS * n,)), pltpu.SemaphoreType.DMA((N_PEERS * n,)),
                        pltpu.SemaphoreType.DMA((n,))],
    )(*shards)


def _norm_proj(h, g, w, pieces, n_steps, out_shape, out_spec, name, tm=1024, ride=None):
    T = h.shape[0]
    n_p = len(pieces)

    def body(h_ref, g_ref, *rest):
        w_refs = rest[:n_p]
        o_ref, n_ref = rest[n_p:]

        @pl.when(pl.program_id(1) == 0)
        def _():
            n_ref[...] = _rms(h_ref[...], g_ref[...]).astype(BF16)

        n = n_ref[...]
        for (width, _, start), w_ref in zip(pieces, w_refs):
            o_ref[:, start:start + width] = _dot(n, w_ref[...]).astype(o_ref.dtype)

    w_specs = [pl.BlockSpec((D_MODEL, width), functools.partial(lambda i, j, f: (0, f(j)), f=col))
               for (width, col, _) in pieces]
    row = pl.BlockSpec((tm, D_MODEL), lambda i, j: (i, 0))
    return _call(body, name=name, grid=(T // tm, n_steps),
                 in_specs=[row, pl.BlockSpec((1, D_MODEL), lambda i, j: (0, 0))] + w_specs,
                 out_specs=[out_spec(tm), row], out_shape=[out_shape, jax.ShapeDtypeStruct((T, D_MODEL), BF16)],
                 args=(h, g, *([w] * n_p)), ride=ride)


def _proj_bwd(dy, dy_spec, w, pieces, n_steps, x, g, dres, name, tm=512, ride=None):
    T = x.shape[0]
    n_p = len(pieces)

    def body(dy_ref, *rest):
        w_refs = rest[:n_p]
        x_ref, g_ref, dres_ref, dx_ref, dxb_ref, dg_ref = rest[n_p:]
        acc = dx_ref
        i, j = pl.program_id(0), pl.program_id(1)

        @pl.when(j == 0)
        def _():
            acc[...] = jnp.zeros_like(acc)

        @pl.when((i == 0) & (j == 0))
        def _():
            dg_ref[...] = jnp.zeros_like(dg_ref)

        for (width, _, start), w_ref in zip(pieces, w_refs):
            acc[...] += _dot_nt(dy_ref[:, start:start + width], w_ref[...])

        @pl.when(j == n_steps - 1)
        def _():
            dx, dg = _rms_bwd(acc[...], x_ref[...], g_ref[...])
            dx = dres_ref[...] + dx
            dx_ref[...] = dx
            dxb_ref[...] = dx.astype(BF16)
            dg_ref[...] += dg

    w_specs = [pl.BlockSpec((D_MODEL, width), functools.partial(lambda i, j, f: (0, f(j)), f=col))
               for (width, col, _) in pieces]
    row = pl.BlockSpec((tm, D_MODEL), lambda i, j: (i, 0))
    vec = pl.BlockSpec((1, D_MODEL), lambda i, j: (0, 0))
    return _call(body, name=name, grid=(T // tm, n_steps),
                 in_specs=[dy_spec(tm)] + w_specs + [row, vec, row], out_specs=[row, row, vec],
                 out_shape=[jax.ShapeDtypeStruct((T, D_MODEL), F32), jax.ShapeDtypeStruct((T, D_MODEL), BF16),
                            jax.ShapeDtypeStruct((1, D_MODEL), F32)],
                 args=(dy, *([w] * n_p), x, g, dres), ride=ride)


def _matmul_res(a, w, res, name, tm=1024):
    T, K = a.shape

    def body(a_ref, w_ref, r_ref, o_ref):
        o_ref[...] = r_ref[...] + _dot(a_ref[...], w_ref[...])

    row = pl.BlockSpec((tm, D_MODEL), lambda i: (i, 0))
    return _call(body, name=name, grid=(T // tm,),
                 in_specs=[pl.BlockSpec((tm, K), lambda i: (i, 0)), pl.BlockSpec((K, D_MODEL), lambda i: (0, 0)), row],
                 out_specs=[row], out_shape=[jax.ShapeDtypeStruct((T, D_MODEL), F32)], args=(a, w, res))[0]


def _matmul_nt(dy, w, name, tm=1024):
    T, N = dy.shape
    K = w.shape[0]

    def body(dy_ref, w_ref, o_ref):
        o_ref[...] = _dot_nt(dy_ref[...], w_ref[...]).astype(BF16)

    return _call(body, name=name, grid=(T // tm,),
                 in_specs=[pl.BlockSpec((tm, N), lambda i: (i, 0)), pl.BlockSpec((K, N), lambda i: (0, 0))],
                 out_specs=[pl.BlockSpec((tm, K), lambda i: (i, 0))], out_shape=[jax.ShapeDtypeStruct((T, K), BF16)],
                 args=(dy, w))[0]


def _wgrad(a, a_spec, b, b_spec, m, n, bm, bn, n_k, name, square_a=False):
    def body(a_ref, b_ref, o_ref, acc):
        k = pl.program_id(2)

        @pl.when(k == 0)
        def _():
            acc[...] = jnp.zeros_like(acc)

        av = a_ref[...]
        if square_a:
            af = av.astype(F32)
            av = (af * af).astype(BF16)
        acc[...] += _dot_tn(av, b_ref[...])

        @pl.when(k == n_k - 1)
        def _():
            o_ref[...] = acc[...].astype(BF16)

    return _call(body, name=name, grid=(m // bm, n // bn, n_k), in_specs=[a_spec, b_spec],
                 out_specs=[pl.BlockSpec((bm, bn), lambda i, j, k: (i, j))], out_shape=[jax.ShapeDtypeStruct((m, n), BF16)],
                 scratch_shapes=[pltpu.VMEM((bm, bn), F32)], args=(a, b))[0]


def _mlp_fwd(h, g, w1, w2, name, tm=1024, tf=512, ride=None):
    T = h.shape[0]

    def body(h_ref, g_ref, w1_ref, w2_ref, ho_ref, n_ref, u_ref):
        @pl.when(pl.program_id(1) == 0)
        def _():
            n_ref[...] = _rms(h_ref[...], g_ref[...]).astype(BF16)
            ho_ref[...] = h_ref[...]

        u = jnp.maximum(_dot(n_ref[...], w1_ref[...]), 0.0)
        u_ref[...] = u.astype(BF16)
        ho_ref[...] += _dot((u * u).astype(BF16), w2_ref[...])

    row = pl.BlockSpec((tm, D_MODEL), lambda i, j: (i, 0))
    return _call(body, name=name, grid=(T // tm, D_FF // tf),
                 in_specs=[row, pl.BlockSpec((1, D_MODEL), lambda i, j: (0, 0)),
                           pl.BlockSpec((D_MODEL, tf), lambda i, j: (0, j)), pl.BlockSpec((tf, D_MODEL), lambda i, j: (j, 0))],
                 out_specs=[row, row, pl.BlockSpec((tm, tf), lambda i, j: (i, j))],
                 out_shape=[jax.ShapeDtypeStruct((T, D_MODEL), F32), jax.ShapeDtypeStruct((T, D_MODEL), BF16),
                            jax.ShapeDtypeStruct((T, D_FF), BF16)],
                 args=(h, g, w1, w2), ride=ride)


def _mlp_bwd(dh, u, w1, w2, h, g, name, tm=512, tf=512, ride=None):
    T = h.shape[0]
    n_j = D_FF // tf

    def body(dh_ref, u_ref, w1_ref, w2_ref, h_ref, g_ref, dx_ref, dxb_ref, du_ref, dg_ref, dhb):
        acc = dx_ref
        i, j = pl.program_id(0), pl.program_id(1)

        @pl.when(j == 0)
        def _():
            dhb[...] = dh_ref[...].astype(BF16)
            acc[...] = jnp.zeros_like(acc)

        @pl.when((i == 0) & (j == 0))
        def _():
            dg_ref[...] = jnp.zeros_like(dg_ref)

        da = _dot_nt(dhb[...], w2_ref[...])
        du = (da * (2.0 * u_ref[...].astype(F32))).astype(BF16)
        du_ref[...] = du
        acc[...] += _dot_nt(du, w1_ref[...])

        @pl.when(j == n_j - 1)
        def _():
            dx, dg = _rms_bwd(acc[...], h_ref[...], g_ref[...])
            dx = dh_ref[...] + dx
            dx_ref[...] = dx
            dxb_ref[...] = dx.astype(BF16)
            dg_ref[...] += dg

    row = pl.BlockSpec((tm, D_MODEL), lambda i, j: (i, 0))
    vec = pl.BlockSpec((1, D_MODEL), lambda i, j: (0, 0))
    hid = pl.BlockSpec((tm, tf), lambda i, j: (i, j))
    return _call(body, name=name, grid=(T // tm, n_j),
                 in_specs=[row, hid, pl.BlockSpec((D_MODEL, tf), lambda i, j: (0, j)),
                           pl.BlockSpec((tf, D_MODEL), lambda i, j: (j, 0)), row, vec],
                 out_specs=[row, row, hid, vec],
                 out_shape=[jax.ShapeDtypeStruct((T, D_MODEL), F32), jax.ShapeDtypeStruct((T, D_MODEL), BF16),
                            jax.ShapeDtypeStruct((T, D_FF), BF16), jax.ShapeDtypeStruct((1, D_MODEL), F32)],
                 scratch_shapes=[pltpu.VMEM((tm, D_MODEL), BF16)], args=(dh, u, w1, w2, h, g), ride=ride)


def _final_loss(h, g, target, name, tm=512):
    T = h.shape[0]

    def body(h_ref, g_ref, t_ref, dh_ref, dhb_ref, loss_ref, dg_ref):
        @pl.when(pl.program_id(0) == 0)
        def _():
            loss_ref[...] = jnp.zeros_like(loss_ref)
            dg_ref[...] = jnp.zeros_like(dg_ref)

        x = h_ref[...]
        gg = g_ref[...]
        r = lax.rsqrt(jnp.mean(x * x, axis=-1, keepdims=True) + EPS)
        xh = x * r
        e = xh * gg - t_ref[...]
        per_tok = jnp.mean(e * e, axis=-1, keepdims=True)
        loss_ref[...] += 0.5 * jnp.sum(per_tok, axis=0, keepdims=True)
        dy = e * (1.0 / D_MODEL)
        dg_ref[...] += jnp.sum(dy * xh, axis=0, keepdims=True)
        dxh = dy * gg
        dx = r * (dxh - xh * jnp.mean(dxh * xh, axis=-1, keepdims=True))
        dh_ref[...] = dx
        dhb_ref[...] = dx.astype(BF16)

    row = pl.BlockSpec((tm, D_MODEL), lambda i: (i, 0))
    vec = pl.BlockSpec((1, D_MODEL), lambda i: (0, 0))
    return _call(body, name=name, grid=(T // tm,), in_specs=[row, vec, row],
                 out_specs=[row, row, pl.BlockSpec((8, LANES), lambda i: (0, 0)), vec],
                 out_shape=[jax.ShapeDtypeStruct((T, D_MODEL), F32), jax.ShapeDtypeStruct((T, D_MODEL), BF16),
                            jax.ShapeDtypeStruct((8, LANES), F32), jax.ShapeDtypeStruct((1, D_MODEL), F32)],
                 args=(h, g, target))


def _ret_constants(S):
    log_gamma = jnp.log1p(-jnp.exp2(-5.0 - jnp.arange(RET_HEADS, dtype=F32)))
    idx = jnp.arange(RET_SC, dtype=F32)
    i, j = idx[:, None], idx[None, :]
    same_chunk = jnp.floor(i / CHUNK) == jnp.floor(j / CHUNK)
    mask = jnp.where((j <= i) | same_chunk, jnp.exp(log_gamma[:, None, None] * jnp.abs(i - j)[None]), 0.0)
    qdec = jnp.exp(log_gamma[:, None] * (idx + 1.0)[None, :])[:, :, None]
    kdec = jnp.exp(log_gamma[:, None] * (RET_SC - 1 - idx)[None, :])[:, :, None]
    cdec = jnp.exp(log_gamma * RET_SC)[:, None, None]
    half = RET_DK // 2
    inv = jnp.exp(-jnp.log(ROPE_BASE) * jnp.arange(half, dtype=F32) / half)
    ang = jnp.arange(S, dtype=F32)[:, None] * inv[None, :]
    return jnp.cos(ang), jnp.sin(ang), mask.astype(F32), qdec, kdec, cdec


def _rope(t, cs, sn):
    t1, t2 = t[:, :RET_DK // 2], t[:, RET_DK // 2:]
    return jnp.concatenate([t1 * cs - t2 * sn, t1 * sn + t2 * cs], axis=-1)


def _rope_bwd(d, cs, sn):
    d1, d2 = d[:, :RET_DK // 2], d[:, RET_DK // 2:]
    return jnp.concatenate([d1 * cs + d2 * sn, d2 * cs - d1 * sn], axis=-1)


def _ret_specs(B, S, reverse):
    n_sc = S // RET_SC

    def cc(c):
        return n_sc - 1 - c if reverse else c

    return dict(
        proj=pl.BlockSpec((RET_SC, RET_HEAD_COLS), lambda h, b, c: (b * n_sc + cc(c), h)),
        trig=pl.BlockSpec((RET_SC, RET_DK // 2), lambda h, b, c: (cc(c), 0)),
        mask=pl.BlockSpec((None, RET_SC, RET_SC), lambda h, b, c: (h, 0, 0)),
        dec=pl.BlockSpec((None, RET_SC, 1), lambda h, b, c: (h, 0, 0)),
        cdec=pl.BlockSpec((None, 1, 1), lambda h, b, c: (h, 0, 0)),
        gn=pl.BlockSpec((1, RET_DV), lambda h, b, c: (0, h)),
        val=pl.BlockSpec((RET_SC, RET_DV), lambda h, b, c: (b * n_sc + cc(c), h)),
        state=pl.BlockSpec((None, None, None, RET_DK, RET_DV), lambda h, b, c: (b, h, cc(c), 0, 0)),
    )


def _ret_qkvg(p_ref, cs, sn):
    q = _rope(p_ref[:, 0:RET_DK].astype(F32), cs, sn)
    k = _rope(p_ref[:, RET_DK:2 * RET_DK].astype(F32), cs, sn) * (RET_DK ** -0.5)
    v = p_ref[:, 2 * RET_DK:2 * RET_DK + RET_DV]
    gate = p_ref[:, 2 * RET_DK + RET_DV:RET_HEAD_COLS].astype(F32)
    return q, k, v, gate


def _group_norm(o):
    mu = jnp.mean(o, axis=-1, keepdims=True)
    oc = o - mu
    rstd = lax.rsqrt(jnp.mean(oc * oc, axis=-1, keepdims=True) + EPS)
    return oc * rstd, rstd


def _ret_fwd(proj, consts, gn, B, S, name, ride=None):
    T = B * S
    n_sc = S // RET_SC
    sp = _ret_specs(B, S, False)

    def body(p_ref, cos_ref, sin_ref, m_ref, qd_ref, kd_ref, cd_ref, gn_ref, y_ref, o_ref, st_ref, state):
        @pl.when(pl.program_id(2) == 0)
        def _():
            state[...] = jnp.zeros_like(state)

        q, k, v, gate = _ret_qkvg(p_ref, cos_ref[...], sin_ref[...])
        qb, kb = q.astype(BF16), k.astype(BF16)
        p = (_dot_nt(qb, kb) * m_ref[...]).astype(BF16)
        sb = state[...].astype(BF16)
        st_ref[...] = sb
        o = _dot(p, v) + _dot(qb, sb) * qd_ref[...]
        kt = (k * kd_ref[...]).astype(BF16)
        state[...] = state[...] * cd_ref[...] + _dot_tn(kt, v)
        o_ref[...] = o
        ohat, _ = _group_norm(o)
        y_ref[...] = (gate * jax.nn.sigmoid(gate) * (ohat * gn_ref[...])).astype(BF16)

    return _call(body, name=name, grid=(RET_HEADS, B, n_sc),
                 in_specs=[sp["proj"], sp["trig"], sp["trig"], sp["mask"], sp["dec"], sp["dec"], sp["cdec"], sp["gn"]],
                 out_specs=[sp["val"], sp["val"], sp["state"]],
                 out_shape=[jax.ShapeDtypeStruct((T, RET_VW), BF16), jax.ShapeDtypeStruct((T, RET_VW), F32),
                            jax.ShapeDtypeStruct((B, RET_HEADS, n_sc, RET_DK, RET_DV), BF16)],
                 scratch_shapes=[pltpu.VMEM((RET_DK, RET_DV), F32)], args=(proj, *consts, gn), ride=ride)


def _ret_bwd(proj, consts, gn, o, states, dy, B, S, name, ride=None):
    T = B * S
    n_sc = S // RET_SC
    sp = _ret_specs(B, S, True)

    def body(p_ref, cos_ref, sin_ref, m_ref, qd_ref, kd_ref, cd_ref, gn_ref, o_ref, st_ref, dy_ref,
             dp_ref, dgn_ref, dstate):
        b, c = pl.program_id(1), pl.program_id(2)

        @pl.when(c == 0)
        def _():
            dstate[...] = jnp.zeros_like(dstate)

        @pl.when((b == 0) & (c == 0))
        def _():
            dgn_ref[...] = jnp.zeros_like(dgn_ref)

        cs, sn = cos_ref[...], sin_ref[...]
        q, k, v, gate = _ret_qkvg(p_ref, cs, sn)
        qb, kb = q.astype(BF16), k.astype(BF16)
        kt = (k * kd_ref[...]).astype(BF16)
        sb = st_ref[...]
        m = m_ref[...]
        ohat, rstd = _group_norm(o_ref[...])
        gnv = gn_ref[...]
        dyv = dy_ref[...].astype(F32)
        sg = jax.nn.sigmoid(gate)
        don = dyv * (gate * sg)
        dgate = dyv * (ohat * gnv) * (sg * (1.0 + gate * (1.0 - sg)))
        dgn_ref[...] += jnp.sum(don * ohat, axis=0, keepdims=True)
        dohat = don * gnv
        do = rstd * (dohat - jnp.mean(dohat, axis=-1, keepdims=True)
                     - ohat * jnp.mean(dohat * ohat, axis=-1, keepdims=True))
        dob = do.astype(BF16)
        doq = (do * qd_ref[...]).astype(BF16)
        dsb = dstate[...].astype(BF16)
        p = (_dot_nt(qb, kb) * m).astype(BF16)
        dv = _dot_tn(p, dob) + _dot(kt, dsb)
        dkt = _dot_nt(v, dsb)
        dsc = (_dot_nt(dob, v) * m).astype(BF16)
        dq = _dot(dsc, kb) + _dot_nt(doq, sb)
        dk = (_dot_tn(dsc, qb) + dkt * kd_ref[...]) * (RET_DK ** -0.5)
        dstate[...] = dstate[...] * cd_ref[...] + _dot_tn(qb, doq)
        dp_ref[:, 0:RET_DK] = _rope_bwd(dq, cs, sn).astype(BF16)
        dp_ref[:, RET_DK:2 * RET_DK] = _rope_bwd(dk, cs, sn).astype(BF16)
        dp_ref[:, 2 * RET_DK:2 * RET_DK + RET_DV] = dv.astype(BF16)
        dp_ref[:, 2 * RET_DK + RET_DV:RET_HEAD_COLS] = dgate.astype(BF16)

    return _call(body, name=name, grid=(RET_HEADS, B, n_sc),
                 in_specs=[sp["proj"], sp["trig"], sp["trig"], sp["mask"], sp["dec"], sp["dec"], sp["cdec"], sp["gn"],
                           sp["val"], sp["state"], sp["val"]],
                 out_specs=[sp["proj"], sp["gn"]],
                 out_shape=[jax.ShapeDtypeStruct((T, RET_HEADS * RET_HEAD_COLS), BF16), jax.ShapeDtypeStruct((1, RET_VW), F32)],
                 scratch_shapes=[pltpu.VMEM((RET_DK, RET_DV), F32)], args=(proj, *consts, gn, o, states, dy), ride=ride)


def _rel_index():
    t = np.arange(ATT_EXT)
    return np.clip(t - (ATT_KW - 1), -MAX_REL, CHUNK - 1) + MAX_REL


def _bias_matrices(rel):
    idx = _rel_index()
    n_lo = int(np.argmax(idx > 0))
    n_hi = ATT_EXT - (n_lo + REL_TABLE - 1)
    ext = jnp.concatenate([jnp.broadcast_to(rel[:, :1], (ATT_HEADS, n_lo - 1)), rel,
                           jnp.broadcast_to(rel[:, -1:], (ATT_HEADS, n_hi))], axis=1)
    e_pad = jnp.pad(ext, ((0, 0), (0, 1)))
    tiled = jnp.broadcast_to(e_pad[:, None, :], (ATT_HEADS, ATT_CQ, ATT_EXT + 1))
    skew = tiled.reshape(ATT_HEADS, -1)[:, :ATT_CQ * ATT_EXT].reshape(ATT_HEADS, ATT_CQ, ATT_EXT)
    toep = skew[:, :, ATT_CQ - 1:ATT_CQ - 1 + ATT_KW]
    i = np.arange(ATT_CQ)[:, None]
    j = np.arange(ATT_KW)[None, :]
    lo = CHUNK * (i // CHUNK)
    band = np.where((j >= lo) & (j < lo + BAND_PAST + CHUNK), 0.0, NEG).astype(np.float32)
    return (toep + band[None]).reshape(ATT_PAIRS, 2, ATT_CQ, ATT_KW)


def _softmax_rows(s):
    m = jnp.max(s, axis=-1, keepdims=True)
    p = jnp.exp(s - m)
    return p * (1.0 / jnp.sum(p, axis=-1, keepdims=True))


def _attn_scores(qm, kw, bias, valid):
    s = _dot_nt(qm, kw) * (ATT_DH ** -0.5) + bias
    return jnp.where(valid, s, NEG)


def _fill_window(kpad, vpad, qkv_ref):
    kpad[0:BAND_PAST, :] = jnp.zeros((BAND_PAST, LANES), BF16)
    vpad[0:BAND_PAST, :] = jnp.zeros((BAND_PAST, LANES), BF16)
    kpad[BAND_PAST:, :] = qkv_ref[1]
    vpad[BAND_PAST:, :] = qkv_ref[2]


def _attn_fwd(qkv3, bias, B, S, name, ride=None):
    T = B * S
    n_qb = S // ATT_CQ

    def body(qkv_ref, b_ref, o_ref, kpad, vpad):
        _fill_window(kpad, vpad, qkv_ref)
        lane = lax.broadcasted_iota(jnp.int32, (1, LANES), 1)
        col = lax.broadcasted_iota(jnp.int32, (1, ATT_KW), 1)

        def step(qb, carry):
            qs = pl.multiple_of(qb * ATT_CQ, ATT_CQ)
            q = qkv_ref[0, pl.ds(qs, ATT_CQ), :]
            kw = kpad[pl.ds(qs, ATT_KW), :]
            vw = vpad[pl.ds(qs, ATT_KW), :]
            valid = (col + qs) >= BAND_PAST
            outs = []
            for hh in range(2):
                sel = (lane >= ATT_DH * hh) & (lane < ATT_DH * (hh + 1))
                qm = jnp.where(sel, q, jnp.zeros_like(q))
                p = _softmax_rows(_attn_scores(qm, kw, b_ref[hh], valid))
                outs.append(_dot(p.astype(BF16), vw))
            o_ref[pl.ds(qs, ATT_CQ), :] = jnp.where(lane < ATT_DH, outs[0], outs[1]).astype(BF16)
            return carry

        lax.fori_loop(0, n_qb, step, 0)

    return _call(body, name=name, grid=(ATT_PAIRS, B),
                 in_specs=[pl.BlockSpec((3, S, LANES), lambda hp, b: (0, b, hp)),
                           pl.BlockSpec((None, 2, ATT_CQ, ATT_KW), lambda hp, b: (hp, 0, 0, 0))],
                 out_specs=[pl.BlockSpec((S, LANES), lambda hp, b: (b, hp))],
                 out_shape=[jax.ShapeDtypeStruct((T, D_MODEL), BF16)],
                 scratch_shapes=[pltpu.VMEM((S + BAND_PAST, LANES), BF16), pltpu.VMEM((S + BAND_PAST, LANES), BF16)],
                 args=(qkv3, bias), ride=ride)


def _attn_bwd(qkv3, bias, do, B, S, name, ride=None):
    T = B * S
    n_qb = S // ATT_CQ
    scale = ATT_DH ** -0.5

    def body(qkv_ref, b_ref, do_ref, dqkv_ref, db_ref, kpad, vpad, dkacc, dvacc):
        @pl.when(pl.program_id(1) == 0)
        def _():
            db_ref[...] = jnp.zeros_like(db_ref)

        _fill_window(kpad, vpad, qkv_ref)
        dkacc[...] = jnp.zeros_like(dkacc)
        dvacc[...] = jnp.zeros_like(dvacc)
        lane = lax.broadcasted_iota(jnp.int32, (1, LANES), 1)
        col = lax.broadcasted_iota(jnp.int32, (1, ATT_KW), 1)

        def step(qb, carry):
            qs = pl.multiple_of(qb * ATT_CQ, ATT_CQ)
            q = qkv_ref[0, pl.ds(qs, ATT_CQ), :]
            dov = do_ref[pl.ds(qs, ATT_CQ), :]
            kw = kpad[pl.ds(qs, ATT_KW), :]
            vw = vpad[pl.ds(qs, ATT_KW), :]
            valid = (col + qs) >= BAND_PAST
            dqs, dk, dv = [], None, None
            for hh in range(2):
                sel = (lane >= ATT_DH * hh) & (lane < ATT_DH * (hh + 1))
                qm = jnp.where(sel, q, jnp.zeros_like(q))
                dom = jnp.where(sel, dov, jnp.zeros_like(dov))
                p = _softmax_rows(_attn_scores(qm, kw, b_ref[hh], valid))
                dp = _dot_nt(dom, vw)
                ds = p * (dp - jnp.sum(dp * p, axis=-1, keepdims=True))
                db_ref[hh] += ds
                dsb = (ds * scale).astype(BF16)
                dqs.append(_dot(dsb, kw))
                dk_h = _dot_tn(dsb, qm)
                dv_h = _dot_tn(p.astype(BF16), dom)
                dk = dk_h if dk is None else dk + dk_h
                dv = dv_h if dv is None else dv + dv_h
            dqkv_ref[0, pl.ds(qs, ATT_CQ), :] = jnp.where(lane < ATT_DH, dqs[0], dqs[1]).astype(BF16)
            dkacc[pl.ds(qs, ATT_KW), :] += dk
            dvacc[pl.ds(qs, ATT_KW), :] += dv
            return carry

        lax.fori_loop(0, n_qb, step, 0)
        dqkv_ref[1] = dkacc[BAND_PAST:, :].astype(BF16)
        dqkv_ref[2] = dvacc[BAND_PAST:, :].astype(BF16)

    bias_spec = pl.BlockSpec((None, 2, ATT_CQ, ATT_KW), lambda hp, b: (hp, 0, 0, 0))
    qkv_spec = pl.BlockSpec((3, S, LANES), lambda hp, b: (0, b, hp))
    return _call(body, name=name, grid=(ATT_PAIRS, B),
                 in_specs=[qkv_spec, bias_spec, pl.BlockSpec((S, LANES), lambda hp, b: (b, hp))],
                 out_specs=[qkv_spec, bias_spec],
                 out_shape=[jax.ShapeDtypeStruct((3, T, D_MODEL), BF16),
                            jax.ShapeDtypeStruct((ATT_PAIRS, 2, ATT_CQ, ATT_KW), F32)],
                 scratch_shapes=[pltpu.VMEM((S + BAND_PAST, LANES), BF16), pltpu.VMEM((S + BAND_PAST, LANES), BF16),
                                 pltpu.VMEM((S + BAND_PAST, LANES), F32), pltpu.VMEM((S + BAND_PAST, LANES), F32)],
                 args=(qkv3, bias, do), ride=ride)


def _rel_bias_grad(dbias, name):
    d = dbias.reshape(ATT_HEADS, ATT_CQ, ATT_KW)
    d = jnp.pad(d, ((0, 0), (0, 0), (ATT_CQ - 1, 0))).reshape(ATT_HEADS, ATT_CQ * ATT_EXT)
    skew = jnp.pad(d, ((0, 0), (0, ATT_CQ))).reshape(ATT_HEADS, ATT_CQ, ATT_EXT + 1)
    fold = np.zeros((ATT_EXT + 1, REL_PAD), np.float32)
    fold[np.arange(ATT_EXT), _rel_index()] = 1.0
    fold = jnp.asarray(fold, BF16)

    def body(s_ref, f_ref, o_ref):
        x = jnp.sum(s_ref[...], axis=0, keepdims=True)
        x = jnp.broadcast_to(x, (8, ATT_EXT + 1))
        acc = jnp.zeros((8, REL_PAD), F32)
        for _ in range(3):
            part = x.astype(BF16)
            acc = acc + _dot(part, f_ref[...])
            x = x - part.astype(F32)
        o_ref[...] = acc[0:1, :]

    out = _call(body, name=name, grid=(ATT_HEADS,),
                in_specs=[pl.BlockSpec((None, ATT_CQ, ATT_EXT + 1), lambda h: (h, 0, 0)),
                          pl.BlockSpec((ATT_EXT + 1, REL_PAD), lambda h: (0, 0))],
                out_specs=[pl.BlockSpec((None, 1, REL_PAD), lambda h: (h, 0, 0))],
                out_shape=[jax.ShapeDtypeStruct((ATT_HEADS, 1, REL_PAD), F32)], args=(skew, fold))[0]
    return out.reshape(ATT_HEADS, REL_PAD)


def _adamw(w, g, m, v):
    m = ADAM_B1 * m + (1.0 - ADAM_B1) * g
    v = ADAM_B2 * v + (1.0 - ADAM_B2) * (g * g)
    m_hat = m / (1.0 - ADAM_B1 ** ADAM_STEP)
    v_hat = v / (1.0 - ADAM_B2 ** ADAM_STEP)
    delta = -ADAM_LR * (m_hat / (jnp.sqrt(v_hat) + ADAM_EPS) + ADAM_WD * w)
    return delta, m, v


def _sum_devices(ref):
    g = ref[0].astype(F32)
    for d in range(1, N_DEV):
        g = g + ref[d].astype(F32)
    return g


def _adamw_reduce(lands, w, m, v, name, tr=256):
    L, R, C = w.shape
    tr = min(tr, R)
    n_i = R // tr

    def body(*refs):
        l_refs = refs[:L]
        w_ref, m_ref, v_ref, g_out, d_out, m_out, v_out = refs[L:]
        layer = pl.program_id(0)
        for l in range(L):
            @pl.when(layer == l)
            def _(l=l):
                g = _sum_devices(l_refs[l])
                g_out[...] = g
                d_out[...], m_out[...], v_out[...] = _adamw(w_ref[...], g, m_ref[...], v_ref[...])

    def land_spec(l):
        return pl.BlockSpec((N_DEV, tr, C), lambda ly, i: (0, jnp.where(ly == l, i, jnp.where(ly < l, 0, n_i - 1)), 0))

    blk = pl.BlockSpec((None, tr, C), lambda ly, i: (ly, i, 0))
    return _call(body, name=name, grid=(L, n_i), in_specs=[land_spec(l) for l in range(L)] + [blk, blk, blk],
                 out_specs=[blk] * 4, out_shape=[jax.ShapeDtypeStruct((L, R, C), F32)] * 4, args=(*lands, w, m, v))


def _small_update(land_small, land_rel, w, m, v, name):
    def body(ls_ref, lr_ref, w_ref, m_ref, v_ref, g_out, d_out, m_out, v_out, rel_out):
        g = _sum_devices(ls_ref)
        g_out[...] = g
        d_out[...], m_out[...], v_out[...] = _adamw(w_ref[...], g, m_ref[...], v_ref[...])
        rel_out[...] = _sum_devices(lr_ref)

    return pl.pallas_call(
        body, name=name,
        out_shape=[jax.ShapeDtypeStruct(w.shape, F32)] * 4 + [jax.ShapeDtypeStruct(land_rel.shape[1:], F32)],
    )(land_small, land_rel, w, m, v)


def _adamw_plain(w, g, m, v, name):
    def body(w_ref, g_ref, m_ref, v_ref, d_out, m_out, v_out):
        d_out[...], m_out[...], v_out[...] = _adamw(w_ref[...], g_ref[...], m_ref[...], v_ref[...])

    return pl.pallas_call(body, name=name, out_shape=[jax.ShapeDtypeStruct(w.shape, F32)] * 3)(w, g, m, v)


def _ret_pieces():
    return [(RET_DK, lambda h: h, 0),
            (RET_DK, lambda h: RET_HEADS + h, RET_DK),
            (RET_DV, lambda h: RET_HEADS + h, 2 * RET_DK),
            (RET_DV, lambda h: 2 * RET_HEADS + h, 2 * RET_DK + RET_DV)]


def _ret_piece_of_column_block(p):
    per_head = RET_HEAD_COLS // RET_DK
    qk = jnp.where(p < RET_HEADS, per_head * p, per_head * (p - RET_HEADS) + 1)
    pv = p - 2 * RET_HEADS
    vv = per_head * (pv // 2) + 2 + pv % 2
    pg = p - 4 * RET_HEADS
    gg = per_head * (pg // 2) + 4 + pg % 2
    return jnp.where(p < 2 * RET_HEADS, qk, jnp.where(p < 4 * RET_HEADS, vv, gg))


def _gather(shard, full_shape, view):
    return _Exchange([shard], [(0, _whole, 0, view)], [jax.ShapeDtypeStruct(full_shape, shard.dtype)])


def _step(x, target, shards, mix_g, gn_g, rel_shard, mlp_g, final_g):
    s_rin, s_rout, s_ain, s_aout, s_w1, s_w2 = shards
    B, S, _ = x.shape
    T = B * S
    tk = min(T, 2048)
    n_k = T // tk
    h0 = x.reshape(T, D_MODEL)
    tgt = target.reshape(T, D_MODEL)
    consts = _ret_constants(S)
    att_pieces = [(D_MODEL, lambda j: j, 0)]
    w1_cols = w2_rows = D_FF // N_DEV

    def qkv_spec(tm):
        return pl.BlockSpec((None, tm, D_MODEL), lambda i, j: (j, i, 0))

    def head_spec(tm):
        return pl.BlockSpec((tm, RET_HEAD_COLS), lambda i, j: (i, j))

    rel_cols = REL_TABLE // N_DEV
    w_rin, rel_all = _all_gather(
        [s_rin, rel_shard],
        [jax.ShapeDtypeStruct((D_MODEL, RET_IN), BF16), jax.ShapeDtypeStruct((N_DEV, ATT_HEADS, rel_cols), F32)],
        [_cols(RET_IN // N_DEV), _slot], "gather_first")
    bias = _bias_matrices(rel_all.transpose(1, 0, 2).reshape(ATT_HEADS, REL_TABLE))

    ride = _Exchange([s_rout, s_w1[0]], [(0, _whole, 0, _rows(RET_VW // N_DEV)), (1, _whole, 1, _cols(w1_cols))],
                     [jax.ShapeDtypeStruct((RET_VW, D_MODEL), BF16), jax.ShapeDtypeStruct((D_MODEL, D_FF), BF16)])
    (proj, n0), (w_rout, w1_0) = _norm_proj(h0, mix_g[0:1], w_rin, _ret_pieces(), RET_HEADS,
                                            jax.ShapeDtypeStruct((T, RET_HEADS * RET_HEAD_COLS), BF16), head_spec,
                                            "ret_proj", ride=ride)
    ride = _Exchange([s_w2[0], s_ain], [(0, _whole, 0, _rows(w2_rows)), (1, _whole, 1, _cols(3 * D_MODEL // N_DEV))],
                     [jax.ShapeDtypeStruct((D_FF, D_MODEL), BF16), jax.ShapeDtypeStruct((D_MODEL, 3 * D_MODEL), BF16)])
    (y, o, states), (w2_0, w_ain) = _ret_fwd(proj, consts, gn_g, B, S, "ret_fwd", ride=ride)
    h1 = _matmul_res(y, w_rout, h0, "ret_out")
    ride = _Exchange([s_aout, s_w1[1]], [(0, _whole, 0, _rows(D_MODEL // N_DEV)), (1, _whole, 1, _cols(w1_cols))],
                     [jax.ShapeDtypeStruct((D_MODEL, D_MODEL), BF16), jax.ShapeDtypeStruct((D_MODEL, D_FF), BF16)])
    (h2, n1, u1), (w_aout, w1_1) = _mlp_fwd(h1, mlp_g[0:1], w1_0, w2_0, "mlp0_fwd", ride=ride)
    qkv3, n2 = _norm_proj(h2, mix_g[1:2], w_ain, att_pieces, 3, jax.ShapeDtypeStruct((3, T, D_MODEL), BF16), qkv_spec,
                          "att_proj")
    (o2,), (w2_1,) = _attn_fwd(qkv3, bias, B, S, "att_fwd", ride=_gather(s_w2[1], (D_FF, D_MODEL), _rows(w2_rows)))
    h3 = _matmul_res(o2, w_aout, h2, "att_out")
    h4, n3, u3 = _mlp_fwd(h3, mlp_g[1:2], w1_1, w2_1, "mlp1_fwd")
    dh4, dh4b, loss, dg_final = _final_loss(h4, final_g, tgt, "final_loss")

    def tok(width):
        return dict(a=pl.BlockSpec((tk, width), lambda i, j, k: (k, i)), b=pl.BlockSpec((tk, width), lambda i, j, k: (k, j)))

    def mlp_grads(dh, dhb, u, n, w1, w2, h, g, tag, ride=None):
        res = _mlp_bwd(dh, u, w1, w2, h, g, tag + "_bwd", ride=ride)
        (dhi, dhib, du, dg), lands = res if ride is not None else (res, None)
        gw2 = _wgrad(u, tok(1024)["a"], dhb, tok(1024)["b"], D_FF, D_MODEL, 1024, 1024, n_k, tag + "_dw2", square_a=True)
        gw1 = _wgrad(n, tok(1024)["a"], du, tok(1024)["b"], D_MODEL, D_FF, 1024, 1024, n_k, tag + "_dw1")
        return dhi, dhib, gw1, gw2, dg, lands

    dh3, dh3b, gw1_1, gw2_1, dg_mlp1, _ = mlp_grads(dh4, dh4b, u3, n3, w1_1, w2_1, h3, mlp_g[1:2], "mlp1")
    do2 = _matmul_nt(dh3b, w_aout, "att_out_bwd")
    g_aout = _wgrad(o2, tok(1024)["a"], dh3b, tok(1024)["b"], D_MODEL, D_MODEL, 1024, 1024, n_k, "att_out_dw")
    ride = _Exchange(
        [gw1_1, gw2_1, g_aout],
        [(0, _cols(w1_cols), 0, _slot), (1, _rows(w2_rows), 1, _slot), (2, _rows(D_MODEL // N_DEV), 2, _slot)],
        [jax.ShapeDtypeStruct((N_DEV, D_MODEL, w1_cols), BF16), jax.ShapeDtypeStruct((N_DEV, w2_rows, D_MODEL), BF16),
         jax.ShapeDtypeStruct((N_DEV, D_MODEL // N_DEV, D_MODEL), BF16)])
    (dqkv3, dbias), (l_w1_1, l_w2_1, l_aout) = _attn_bwd(qkv3, bias, do2, B, S, "att_bwd", ride=ride)
    g_rel = _rel_bias_grad(dbias, "rel_bias_grad")
    dh2, dh2b, dg_mix1 = _proj_bwd(dqkv3, qkv_spec, w_ain, att_pieces, 3, h2, mix_g[1:2], dh3, "att_proj_bwd")
    g_ain = _wgrad(n2, tok(1024)["a"], dqkv3, pl.BlockSpec((None, tk, D_MODEL), lambda i, j, k: (j, k, 0)),
                   D_MODEL, 3 * D_MODEL, 1024, 1024, n_k, "att_proj_dw")
    ride = _Exchange([g_ain], [(0, _cols(3 * D_MODEL // N_DEV), 0, _slot)],
                     [jax.ShapeDtypeStruct((N_DEV, D_MODEL, 3 * D_MODEL // N_DEV), BF16)])
    dh1, dh1b, gw1_0, gw2_0, dg_mlp0, (l_ain,) = mlp_grads(dh2, dh2b, u1, n1, w1_0, w2_0, h1, mlp_g[0:1], "mlp0", ride=ride)
    dy = _matmul_nt(dh1b, w_rout, "ret_out_bwd")
    g_rout = _wgrad(y, tok(1024)["a"], dh1b, tok(1024)["b"], RET_VW, D_MODEL, 1024, 1024, n_k, "ret_out_dw")
    ride = _Exchange(
        [gw1_0, gw2_0, g_rout],
        [(0, _cols(w1_cols), 0, _slot), (1, _rows(w2_rows), 1, _slot), (2, _rows(RET_VW // N_DEV), 2, _slot)],
        [jax.ShapeDtypeStruct((N_DEV, D_MODEL, w1_cols), BF16), jax.ShapeDtypeStruct((N_DEV, w2_rows, D_MODEL), BF16),
         jax.ShapeDtypeStruct((N_DEV, RET_VW // N_DEV, D_MODEL), BF16)])
    (dproj, dgn), (l_w1_0, l_w2_0, l_rout) = _ret_bwd(proj, consts, gn_g, o, states, dy, B, S, "ret_bwd", ride=ride)
    g_rin = _wgrad(n0, tok(1024)["a"], dproj,
                   pl.BlockSpec((tk, RET_DK), lambda i, j, k: (k, _ret_piece_of_column_block(j))),
                   D_MODEL, RET_IN, 1024, RET_DK, n_k, "ret_proj_dw")
    ride = _Exchange([g_rin], [(0, _cols(RET_IN // N_DEV), 0, _slot)],
                     [jax.ShapeDtypeStruct((N_DEV, D_MODEL, RET_IN // N_DEV), BF16)])
    (dx, _, dg_mix0), (l_rin,) = _proj_bwd(dproj, head_spec, w_rin, _ret_pieces(), RET_HEADS, h0, mix_g[0:1], dh1,
                                           "ret_proj_bwd", ride=ride)

    small = jnp.concatenate([dg_mix0, dg_mix1, dg_mlp0, dg_mlp1, dg_final, dgn.reshape(2, D_MODEL),
                             jnp.zeros((1, D_MODEL), F32)], axis=0)
    l_small, l_rel = _Exchange(
        [small, g_rel], [(0, _whole, 0, _slot), (1, _whole, 1, _slot)],
        [jax.ShapeDtypeStruct((N_DEV, 8, D_MODEL), F32), jax.ShapeDtypeStruct((N_DEV, ATT_HEADS, REL_PAD), F32)],
    ).run("scatter_small")
    lands = dict(rin=l_rin, rout=l_rout, ain=l_ain, aout=l_aout, w1=(l_w1_0, l_w1_1), w2=(l_w2_0, l_w2_1),
                 small=l_small, rel=l_rel)
    return loss[0, 0], dx.reshape(B, S, D_MODEL), lands


def kernel(x, mix_norm_g, ret_w_in, ret_gn_g, ret_w_out, att_w_in, att_rel_bias, att_w_out, mlp_norm_g, mlp_w1, mlp_w2, final_norm_g, loss_target, m_mix_norm_g, m_ret_w_in, m_ret_gn_g, m_ret_w_out, m_att_w_in, m_att_rel_bias, m_att_w_out, m_mlp_norm_g, m_mlp_w1, m_mlp_w2, m_final_norm_g, v_mix_norm_g, v_ret_w_in, v_ret_gn_g, v_ret_w_out, v_att_w_in, v_att_rel_bias, v_att_w_out, v_mlp_norm_g, v_mlp_w1, v_mlp_w2, v_final_norm_g):
    me = _lin(_place())
    rel_cols = REL_TABLE // N_DEV
    shards = (ret_w_in[0].astype(BF16), ret_w_out[0].astype(BF16), att_w_in[0].astype(BF16), att_w_out[0].astype(BF16),
              (mlp_w1[0].astype(BF16), mlp_w1[1].astype(BF16)), (mlp_w2[0].astype(BF16), mlp_w2[1].astype(BF16)))
    loss_part, grad_x, lands = _step(x, loss_target, shards, mix_norm_g, ret_gn_g, att_rel_bias[0], mlp_norm_g,
                                     final_norm_g.reshape(1, D_MODEL))
    loss = lax.psum(loss_part, ("x", "y", "c"))

    u_rin = _adamw_reduce([lands["rin"]], ret_w_in, m_ret_w_in, v_ret_w_in, "update_ret_w_in")
    u_rout = _adamw_reduce([lands["rout"]], ret_w_out, m_ret_w_out, v_ret_w_out, "update_ret_w_out")
    u_ain = _adamw_reduce([lands["ain"]], att_w_in, m_att_w_in, v_att_w_in, "update_att_w_in")
    u_aout = _adamw_reduce([lands["aout"]], att_w_out, m_att_w_out, v_att_w_out, "update_att_w_out")
    u_w1 = _adamw_reduce(lands["w1"], mlp_w1, m_mlp_w1, v_mlp_w1, "update_mlp_w1")
    u_w2 = _adamw_reduce(lands["w2"], mlp_w2, m_mlp_w2, v_mlp_w2, "update_mlp_w2")

    def pack(mix, mlp, fin, gn):
        return jnp.concatenate([mix, mlp, fin.reshape(1, D_MODEL), gn.reshape(2, D_MODEL), jnp.zeros((1, D_MODEL), F32)], axis=0)

    small_w = pack(mix_norm_g, mlp_norm_g, final_norm_g, ret_gn_g)
    small_m = pack(m_mix_norm_g, m_mlp_norm_g, m_final_norm_g, m_ret_gn_g)
    small_v = pack(v_mix_norm_g, v_mlp_norm_g, v_final_norm_g, v_ret_gn_g)
    sg, sd, sm, sv, rel_sum = _small_update(lands["small"], lands["rel"], small_w, small_m, small_v, "update_small")
    g_rel_mine = lax.dynamic_slice(rel_sum, (0, me * rel_cols), (ATT_HEADS, rel_cols))
    rel_d, rel_m, rel_v = _adamw_plain(att_rel_bias[0], g_rel_mine, m_att_rel_bias[0], v_att_rel_bias[0], "update_rel_bias")
    u_rel = [g_rel_mine[None], rel_d[None], rel_m[None], rel_v[None]]

    def unpack(t):
        return dict(mix=t[0:2], mlp=t[2:4], fin=t[4], gn=t[5:7].reshape(1, RET_VW))

    us = [unpack(t) for t in (sg, sd, sm, sv)]
    outs = [loss, grad_x]
    for k in range(4):
        outs += [us[k]["mix"], u_rin[k], us[k]["gn"], u_rout[k], u_ain[k], u_rel[k], u_aout[k], us[k]["mlp"],
                 u_w1[k], u_w2[k], us[k]["fin"]]
    return tuple(outs)
```

```python
import functools

import numpy as np
import jax
import jax.numpy as jnp
from jax import lax
from jax.experimental import pallas as pl
from jax.experimental.pallas import tpu as pltpu

F32, BF16 = jnp.float32, jnp.bfloat16

D_MODEL = 1024
CHUNK = 64
RET_HEADS, RET_DK, RET_DV = 4, 256, 512
RET_QK, RET_VW = RET_HEADS * RET_DK, RET_HEADS * RET_DV
RET_IN = 2 * RET_QK + 2 * RET_VW
RET_HEAD_COLS = 2 * RET_DK + 2 * RET_DV
RET_SC = 256
ROPE_BASE = 10000.0
ATT_HEADS, ATT_DH = 16, 64
LANES = 128
ATT_PAIRS = ATT_HEADS * ATT_DH // LANES
BAND_PAST = 8 * CHUNK
MAX_REL = 256
REL_TABLE = MAX_REL + CHUNK
REL_PAD = 384
ATT_CQ = 256
ATT_KW = ATT_CQ + BAND_PAST
ATT_EXT = ATT_CQ + ATT_KW - 1
ATT_VARIANTS = BAND_PAST // ATT_CQ + 1
D_FF = 4 * D_MODEL
EPS = 1e-6
NEG = -1e30
N_DEV = 8
N_PEERS = N_DEV - 1

ADAM_LR, ADAM_B1, ADAM_B2, ADAM_EPS, ADAM_WD, ADAM_STEP = 0.001, 0.9, 0.999, 1e-08, 0.01, 10

VMEM_LIMIT = 56 * 1024 * 1024
MESH = pl.DeviceIdType.MESH
ANY = pl.BlockSpec(memory_space=pl.ANY)


def _dot(a, b):
    return jnp.dot(a, b, preferred_element_type=F32)


def _dot_nt(a, b):
    return lax.dot_general(a, b, (((1,), (1,)), ((), ())), preferred_element_type=F32)


def _dot_tn(a, b):
    return lax.dot_general(a, b, (((0,), (0,)), ((), ())), preferred_element_type=F32)


def _rms(x, g):
    r = lax.rsqrt(jnp.mean(x * x, axis=-1, keepdims=True) + EPS)
    return x * r * g


def _rms_bwd(dn, x, g):
    r = lax.rsqrt(jnp.mean(x * x, axis=-1, keepdims=True) + EPS)
    xh = x * r
    dg = jnp.sum(dn * xh, axis=0, keepdims=True)
    dxh = dn * g
    dx = r * (dxh - xh * jnp.mean(dxh * xh, axis=-1, keepdims=True))
    return dx, dg


def _place():
    return lax.axis_index("x"), lax.axis_index("y"), lax.axis_index("c")


def _lin(p):
    return 4 * p[0] + 2 * p[1] + p[2]


def _cols(width):
    return lambda ref, blk: ref.at[:, pl.ds(pl.multiple_of(blk * width, LANES), width)]


def _rows(height):
    return lambda ref, blk: ref.at[pl.ds(pl.multiple_of(blk * height, 8), height), :]


def _slot(ref, blk):
    return ref.at[blk]


def _whole(ref, blk):
    return ref


class _Exchange:
    def __init__(self, sources, flows, land_shapes):
        self.sources, self.flows, self.land_shapes = list(sources), flows, list(land_shapes)
        n_f = len(flows)
        self.sem_shapes = [pltpu.SemaphoreType.DMA((N_PEERS * n_f,)), pltpu.SemaphoreType.DMA((N_PEERS * n_f,)),
                           pltpu.SemaphoreType.DMA((n_f,))]

    def _copies(self, srcs, lands, sems):
        send_sems, recv_sems, local_sems = sems
        x, y, c = _place()
        me = (x, y, c)
        peers = [(x ^ dx, y ^ dy, c ^ dc) for dx in (0, 1) for dy in (0, 1) for dc in (0, 1)][1:]

        def copy(f, k, sender, to):
            si, sview, li, lview = self.flows[f]
            return pltpu.make_async_remote_copy(
                src_ref=sview(srcs[si], _lin(to)), dst_ref=lview(lands[li], _lin(sender)),
                send_sem=send_sems.at[f * N_PEERS + k], recv_sem=recv_sems.at[f * N_PEERS + k],
                device_id=to, device_id_type=MESH)

        mine, sends, recvs = [], [], []
        for f, (si, sview, li, lview) in enumerate(self.flows):
            mine.append(pltpu.make_async_copy(sview(srcs[si], _lin(me)), lview(lands[li], _lin(me)), local_sems.at[f]))
            for k, peer in enumerate(peers):
                sends.append(copy(f, k, me, peer))
                recvs.append(copy(f, k, peer, me))
        return mine, sends, recvs

    def start(self, srcs, lands, sems):
        mine, sends, _ = self._copies(srcs, lands, sems)
        for cp in mine + sends:
            cp.start()

    def finish(self, srcs, lands, sems):
        mine, sends, recvs = self._copies(srcs, lands, sems)
        for cp in recvs:
            cp.wait_recv()
        for cp in sends:
            cp.wait_send()
        for cp in mine:
            cp.wait()

    def run(self, name):
        n_src, n_land = len(self.sources), len(self.land_shapes)

        def body(*refs):
            srcs, lands, sems = refs[:n_src], refs[n_src:n_src + n_land], refs[n_src + n_land:]
            self.start(srcs, lands, sems)
            self.finish(srcs, lands, sems)

        return pl.pallas_call(body, name=name, in_specs=[ANY] * n_src, out_specs=[ANY] * n_land,
                              out_shape=self.land_shapes, scratch_shapes=self.sem_shapes)(*self.sources)


def _call(body, *, name, grid, in_specs, out_specs, out_shape, args, scratch_shapes=(), ride=None):
    params = pltpu.CompilerParams(dimension_semantics=("arbitrary",) * len(grid), vmem_limit_bytes=VMEM_LIMIT)
    in_specs, out_specs, out_shape, scratch_shapes = list(in_specs), list(out_specs), list(out_shape), list(scratch_shapes)
    if ride is None:
        return pl.pallas_call(body, name=name, grid=grid, in_specs=in_specs, out_specs=out_specs, out_shape=out_shape,
                              scratch_shapes=scratch_shapes, compiler_params=params)(*args)
    n_in, n_out, n_scr = len(in_specs), len(out_specs), len(scratch_shapes)
    n_src, n_land = len(ride.sources), len(ride.land_shapes)

    def riding(*refs):
        bounds = np.cumsum([n_in, n_src, n_out, n_land, n_scr])
        ins, srcs, outs, lands, scr, sems = (refs[a:b] for a, b in zip([0, *bounds], [*bounds, len(refs)]))
        ids = [pl.program_id(d) for d in range(len(grid))]
        first = functools.reduce(lambda a, b: a & b, [i == 0 for i in ids])
        last = functools.reduce(lambda a, b: a & b, [i == n - 1 for i, n in zip(ids, grid)])

        @pl.when(first)
        def _():
            ride.start(srcs, lands, sems)

        body(*ins, *outs, *scr)

        @pl.when(last)
        def _():
            ride.finish(srcs, lands, sems)

    res = pl.pallas_call(
        riding, name=name, grid=grid, in_specs=in_specs + [ANY] * n_src, out_specs=out_specs + [ANY] * n_land,
        out_shape=out_shape + ride.land_shapes, scratch_shapes=scratch_shapes + ride.sem_shapes,
        compiler_params=params)(*args, *ride.sources)
    return res[:n_out], res[n_out:]


def _all_gather(shards, out_shapes, views, name):
    n = len(shards)

    def body(*refs):
        ins, outs = refs[:n], refs[n:2 * n]
        send_sems, recv_sems, local_sems = refs[2 * n:]
        x, y, c = _place()
        me, sibling = (x, y, c), (x, y, 1 - c)
        chips = [(1 - x, y), (x, 1 - y), (1 - x, 1 - y)]

        def copy(a, k, block, to, src=None):
            dst = views[a](outs[a], _lin(block))
            return pltpu.make_async_remote_copy(
                src_ref=dst if src is None else src, dst_ref=dst,
                send_sem=send_sems.at[a * N_PEERS + k], recv_sem=recv_sems.at[a * N_PEERS + k],
                device_id=to, device_id_type=MESH)

        mine = [pltpu.make_async_copy(ins[a], views[a](outs[a], _lin(me)), local_sems.at[a]) for a in range(n)]
        for cp in mine:
            cp.start()
        first = []
        for a in range(n):
            first.append(copy(a, 0, me, sibling, src=ins[a]))
            first += [copy(a, 1 + j, me, (*chip, c), src=ins[a]) for j, chip in enumerate(chips)]
        for cp in first:
            cp.start()
        passed = []
        for j, chip in enumerate(chips):
            for a in range(n):
                copy(a, 1 + j, (*chip, c), me).wait_recv()
                fwd = copy(a, 4 + j, (*chip, c), sibling)
                fwd.start()
                passed.append(fwd)
        for a in range(n):
            copy(a, 0, sibling, me).wait_recv()
            for j, chip in enumerate(chips):
                copy(a, 4 + j, (*chip, 1 - c), me).wait_recv()
        for cp in first + passed:
            cp.wait_send()
        for cp in mine:
            cp.wait()

    return pl.pallas_call(
        body, name=name,
        in_specs=[ANY] * n, out_specs=[ANY] * n, out_shape=out_shapes,
        scratch_shapes=[pltpu.SemaphoreType.DMA((N_PEERS * n,)), pltpu.SemaphoreType.DMA((N_PEERS * n,)),
                        pltpu.SemaphoreType.DMA((n,))],
    )(*shards)


def _norm_proj(h, g, w, pieces, n_steps, out_shape, out_spec, name, tm=1024, ride=None):
    T = h.shape[0]
    n_p = len(pieces)

    def body(h_ref, g_ref, *rest):
        w_refs = rest[:n_p]
        o_ref, n_ref = rest[n_p:]

        @pl.when(pl.program_id(1) == 0)
        def _():
            n_ref[...] = _rms(h_ref[...], g_ref[...]).astype(BF16)

        n = n_ref[...]
        for (width, _, start), w_ref in zip(pieces, w_refs):
            o_ref[:, start:start + width] = _dot(n, w_ref[...]).astype(o_ref.dtype)

    w_specs = [pl.BlockSpec((D_MODEL, width), functools.partial(lambda i, j, f: (0, f(j)), f=col))
               for (width, col, _) in pieces]
    row = pl.BlockSpec((tm, D_MODEL), lambda i, j: (i, 0))
    return _call(body, name=name, grid=(T // tm, n_steps),
                 in_specs=[row, pl.BlockSpec((1, D_MODEL), lambda i, j: (0, 0))] + w_specs,
                 out_specs=[out_spec(tm), row], out_shape=[out_shape, jax.ShapeDtypeStruct((T, D_MODEL), BF16)],
                 args=(h, g, *([w] * n_p)), ride=ride)


def _proj_bwd(dy, dy_spec, w, pieces, n_steps, x, g, dres, name, tm=512, ride=None):
    T = x.shape[0]
    n_p = len(pieces)

    def body(dy_ref, *rest):
        w_refs = rest[:n_p]
        x_ref, g_ref, dres_ref, dx_ref, dxb_ref, dg_ref = rest[n_p:]
        acc = dx_ref
        i, j = pl.program_id(0), pl.program_id(1)

        @pl.when(j == 0)
        def _():
            acc[...] = jnp.zeros_like(acc)

        @pl.when((i == 0) & (j == 0))
        def _():
            dg_ref[...] = jnp.zeros_like(dg_ref)

        for (width, _, start), w_ref in zip(pieces, w_refs):
            acc[...] += _dot_nt(dy_ref[:, start:start + width], w_ref[...])

        @pl.when(j == n_steps - 1)
        def _():
            dx, dg = _rms_bwd(acc[...], x_ref[...], g_ref[...])
            dx = dres_ref[...] + dx
            dx_ref[...] = dx
            dxb_ref[...] = dx.astype(BF16)
            dg_ref[...] += dg

    w_specs = [pl.BlockSpec((D_MODEL, width), functools.partial(lambda i, j, f: (0, f(j)), f=col))
               for (width, col, _) in pieces]
    row = pl.BlockSpec((tm, D_MODEL), lambda i, j: (i, 0))
    vec = pl.BlockSpec((1, D_MODEL), lambda i, j: (0, 0))
    return _call(body, name=name, grid=(T // tm, n_steps),
                 in_specs=[dy_spec(tm)] + w_specs + [row, vec, row], out_specs=[row, row, vec],
                 out_shape=[jax.ShapeDtypeStruct((T, D_MODEL), F32), jax.ShapeDtypeStruct((T, D_MODEL), BF16),
                            jax.ShapeDtypeStruct((1, D_MODEL), F32)],
                 args=(dy, *([w] * n_p), x, g, dres), ride=ride)


def _matmul_res(a, w, res, name, tm=1024):
    T, K = a.shape

    def body(a_ref, w_ref, r_ref, o_ref):
        o_ref[...] = r_ref[...] + _dot(a_ref[...], w_ref[...])

    row = pl.BlockSpec((tm, D_MODEL), lambda i: (i, 0))
    return _call(body, name=name, grid=(T // tm,),
                 in_specs=[pl.BlockSpec((tm, K), lambda i: (i, 0)), pl.BlockSpec((K, D_MODEL), lambda i: (0, 0)), row],
                 out_specs=[row], out_shape=[jax.ShapeDtypeStruct((T, D_MODEL), F32)], args=(a, w, res))[0]


def _matmul_nt(dy, w, name, tm=1024):
    T, N = dy.shape
    K = w.shape[0]

    def body(dy_ref, w_ref, o_ref):
        o_ref[...] = _dot_nt(dy_ref[...], w_ref[...]).astype(BF16)

    return _call(body, name=name, grid=(T // tm,),
                 in_specs=[pl.BlockSpec((tm, N), lambda i: (i, 0)), pl.BlockSpec((K, N), lambda i: (0, 0))],
                 out_specs=[pl.BlockSpec((tm, K), lambda i: (i, 0))], out_shape=[jax.ShapeDtypeStruct((T, K), BF16)],
                 args=(dy, w))[0]


def _wgrad(a, a_spec, b, b_spec, m, n, bm, bn, n_k, name, square_a=False):
    def body(a_ref, b_ref, o_ref, acc):
        k = pl.program_id(2)

        @pl.when(k == 0)
        def _():
            acc[...] = jnp.zeros_like(acc)

        av = a_ref[...]
        if square_a:
            af = av.astype(F32)
            av = (af * af).astype(BF16)
        acc[...] += _dot_tn(av, b_ref[...])

        @pl.when(k == n_k - 1)
        def _():
            o_ref[...] = acc[...].astype(BF16)

    return _call(body, name=name, grid=(m // bm, n // bn, n_k), in_specs=[a_spec, b_spec],
                 out_specs=[pl.BlockSpec((bm, bn), lambda i, j, k: (i, j))], out_shape=[jax.ShapeDtypeStruct((m, n), BF16)],
                 scratch_shapes=[pltpu.VMEM((bm, bn), F32)], args=(a, b))[0]


def _mlp_fwd(h, g, w1, w2, name, tm=1024, tf=512, ride=None):
    T = h.shape[0]

    def body(h_ref, g_ref, w1_ref, w2_ref, ho_ref, n_ref, u_ref):
        @pl.when(pl.program_id(1) == 0)
        def _():
            n_ref[...] = _rms(h_ref[...], g_ref[...]).astype(BF16)
            ho_ref[...] = h_ref[...]

        u = jnp.maximum(_dot(n_ref[...], w1_ref[...]), 0.0)
        u_ref[...] = u.astype(BF16)
        ho_ref[...] += _dot((u * u).astype(BF16), w2_ref[...])

    row = pl.BlockSpec((tm, D_MODEL), lambda i, j: (i, 0))
    return _call(body, name=name, grid=(T // tm, D_FF // tf),
                 in_specs=[row, pl.BlockSpec((1, D_MODEL), lambda i, j: (0, 0)),
                           pl.BlockSpec((D_MODEL, tf), lambda i, j: (0, j)), pl.BlockSpec((tf, D_MODEL), lambda i, j: (j, 0))],
                 out_specs=[row, row, pl.BlockSpec((tm, tf), lambda i, j: (i, j))],
                 out_shape=[jax.ShapeDtypeStruct((T, D_MODEL), F32), jax.ShapeDtypeStruct((T, D_MODEL), BF16),
                            jax.ShapeDtypeStruct((T, D_FF), BF16)],
                 args=(h, g, w1, w2), ride=ride)


def _mlp_bwd(dh, u, w1, w2, h, g, name, tm=512, tf=512, ride=None):
    T = h.shape[0]
    n_j = D_FF // tf

    def body(dh_ref, u_ref, w1_ref, w2_ref, h_ref, g_ref, dx_ref, dxb_ref, du_ref, dg_ref, dhb):
        acc = dx_ref
        i, j = pl.program_id(0), pl.program_id(1)

        @pl.when(j == 0)
        def _():
            dhb[...] = dh_ref[...].astype(BF16)
            acc[...] = jnp.zeros_like(acc)

        @pl.when((i == 0) & (j == 0))
        def _():
            dg_ref[...] = jnp.zeros_like(dg_ref)

        da = _dot_nt(dhb[...], w2_ref[...])
        du = (da * (2.0 * u_ref[...].astype(F32))).astype(BF16)
        du_ref[...] = du
        acc[...] += _dot_nt(du, w1_ref[...])

        @pl.when(j == n_j - 1)
        def _():
            dx, dg = _rms_bwd(acc[...], h_ref[...], g_ref[...])
            dx = dh_ref[...] + dx
            dx_ref[...] = dx
            dxb_ref[...] = dx.astype(BF16)
            dg_ref[...] += dg

    row = pl.BlockSpec((tm, D_MODEL), lambda i, j: (i, 0))
    vec = pl.BlockSpec((1, D_MODEL), lambda i, j: (0, 0))
    hid = pl.BlockSpec((tm, tf), lambda i, j: (i, j))
    return _call(body, name=name, grid=(T // tm, n_j),
                 in_specs=[row, hid, pl.BlockSpec((D_MODEL, tf), lambda i, j: (0, j)),
                           pl.BlockSpec((tf, D_MODEL), lambda i, j: (j, 0)), row, vec],
                 out_specs=[row, row, hid, vec],
                 out_shape=[jax.ShapeDtypeStruct((T, D_MODEL), F32), jax.ShapeDtypeStruct((T, D_MODEL), BF16),
                            jax.ShapeDtypeStruct((T, D_FF), BF16), jax.ShapeDtypeStruct((1, D_MODEL), F32)],
                 scratch_shapes=[pltpu.VMEM((tm, D_MODEL), BF16)], args=(dh, u, w1, w2, h, g), ride=ride)


def _final_loss(h, g, target, name, tm=512):
    T = h.shape[0]

    def body(h_ref, g_ref, t_ref, dh_ref, dhb_ref, loss_ref, dg_ref):
        @pl.when(pl.program_id(0) == 0)
        def _():
            loss_ref[...] = jnp.zeros_like(loss_ref)
            dg_ref[...] = jnp.zeros_like(dg_ref)

        x = h_ref[...]
        gg = g_ref[...]
        r = lax.rsqrt(jnp.mean(x * x, axis=-1, keepdims=True) + EPS)
        xh = x * r
        e = xh * gg - t_ref[...]
        per_tok = jnp.mean(e * e, axis=-1, keepdims=True)
        loss_ref[...] += 0.5 * jnp.sum(per_tok, axis=0, keepdims=True)
        dy = e * (1.0 / D_MODEL)
        dg_ref[...] += jnp.sum(dy * xh, axis=0, keepdims=True)
        dxh = dy * gg
        dx = r * (dxh - xh * jnp.mean(dxh * xh, axis=-1, keepdims=True))
        dh_ref[...] = dx
        dhb_ref[...] = dx.astype(BF16)

    row = pl.BlockSpec((tm, D_MODEL), lambda i: (i, 0))
    vec = pl.BlockSpec((1, D_MODEL), lambda i: (0, 0))
    return _call(body, name=name, grid=(T // tm,), in_specs=[row, vec, row],
                 out_specs=[row, row, pl.BlockSpec((8, LANES), lambda i: (0, 0)), vec],
                 out_shape=[jax.ShapeDtypeStruct((T, D_MODEL), F32), jax.ShapeDtypeStruct((T, D_MODEL), BF16),
                            jax.ShapeDtypeStruct((8, LANES), F32), jax.ShapeDtypeStruct((1, D_MODEL), F32)],
                 args=(h, g, target))


def _ret_constants(S):
    log_gamma = jnp.log1p(-jnp.exp2(-5.0 - jnp.arange(RET_HEADS, dtype=F32)))
    idx = jnp.arange(RET_SC, dtype=F32)
    i, j = idx[:, None], idx[None, :]
    same_chunk = jnp.floor(i / CHUNK) == jnp.floor(j / CHUNK)
    mask = jnp.where((j <= i) | same_chunk, jnp.exp(log_gamma[:, None, None] * jnp.abs(i - j)[None]), 0.0)
    qdec = jnp.exp(log_gamma[:, None] * (idx + 1.0)[None, :])[:, :, None]
    kdec = jnp.exp(log_gamma[:, None] * (RET_SC - 1 - idx)[None, :])[:, :, None]
    cdec = jnp.exp(log_gamma * RET_SC)[:, None, None]
    half = RET_DK // 2
    inv = jnp.exp(-jnp.log(ROPE_BASE) * jnp.arange(half, dtype=F32) / half)
    ang = jnp.arange(S, dtype=F32)[:, None] * inv[None, :]
    return jnp.cos(ang), jnp.sin(ang), mask.astype(F32), qdec, kdec, cdec


def _rope(t, cs, sn):
    t1, t2 = t[:, :RET_DK // 2], t[:, RET_DK // 2:]
    return jnp.concatenate([t1 * cs - t2 * sn, t1 * sn + t2 * cs], axis=-1)


def _rope_bwd(d, cs, sn):
    d1, d2 = d[:, :RET_DK // 2], d[:, RET_DK // 2:]
    return jnp.concatenate([d1 * cs + d2 * sn, d2 * cs - d1 * sn], axis=-1)


def _ret_specs(B, S, reverse):
    n_sc = S // RET_SC

    def cc(c):
        return n_sc - 1 - c if reverse else c

    return dict(
        proj=pl.BlockSpec((RET_SC, RET_HEAD_COLS), lambda h, b, c: (b * n_sc + cc(c), h)),
        trig=pl.BlockSpec((RET_SC, RET_DK // 2), lambda h, b, c: (cc(c), 0)),
        mask=pl.BlockSpec((None, RET_SC, RET_SC), lambda h, b, c: (h, 0, 0)),
        dec=pl.BlockSpec((None, RET_SC, 1), lambda h, b, c: (h, 0, 0)),
        cdec=pl.BlockSpec((None, 1, 1), lambda h, b, c: (h, 0, 0)),
        gn=pl.BlockSpec((1, RET_DV), lambda h, b, c: (0, h)),
        val=pl.BlockSpec((RET_SC, RET_DV), lambda h, b, c: (b * n_sc + cc(c), h)),
        state=pl.BlockSpec((None, None, None, RET_DK, RET_DV), lambda h, b, c: (b, h, cc(c), 0, 0)),
    )


def _ret_qkvg(p_ref, cs, sn):
    q = _rope(p_ref[:, 0:RET_DK].astype(F32), cs, sn)
    k = _rope(p_ref[:, RET_DK:2 * RET_DK].astype(F32), cs, sn) * (RET_DK ** -0.5)
    v = p_ref[:, 2 * RET_DK:2 * RET_DK + RET_DV]
    gate = p_ref[:, 2 * RET_DK + RET_DV:RET_HEAD_COLS].astype(F32)
    return q, k, v, gate


def _group_norm(o):
    mu = jnp.mean(o, axis=-1, keepdims=True)
    oc = o - mu
    rstd = lax.rsqrt(jnp.mean(oc * oc, axis=-1, keepdims=True) + EPS)
    return oc * rstd, rstd


def _ret_fwd(proj, consts, gn, B, S, name, ride=None):
    T = B * S
    n_sc = S // RET_SC
    sp = _ret_specs(B, S, False)

    def body(p_ref, cos_ref, sin_ref, m_ref, qd_ref, kd_ref, cd_ref, gn_ref, y_ref, o_ref, st_ref, state):
        @pl.when(pl.program_id(2) == 0)
        def _():
            state[...] = jnp.zeros_like(state)

        q, k, v, gate = _ret_qkvg(p_ref, cos_ref[...], sin_ref[...])
        qb, kb = q.astype(BF16), k.astype(BF16)
        p = (_dot_nt(qb, kb) * m_ref[...]).astype(BF16)
        sb = state[...].astype(BF16)
        st_ref[...] = sb
        o = _dot(p, v) + _dot(qb, sb) * qd_ref[...]
        kt = (k * kd_ref[...]).astype(BF16)
        state[...] = state[...] * cd_ref[...] + _dot_tn(kt, v)
        o_ref[...] = o
        ohat, _ = _group_norm(o)
        y_ref[...] = (gate * jax.nn.sigmoid(gate) * (ohat * gn_ref[...])).astype(BF16)

    return _call(body, name=name, grid=(RET_HEADS, B, n_sc),
                 in_specs=[sp["proj"], sp["trig"], sp["trig"], sp["mask"], sp["dec"], sp["dec"], sp["cdec"], sp["gn"]],
                 out_specs=[sp["val"], sp["val"], sp["state"]],
                 out_shape=[jax.ShapeDtypeStruct((T, RET_VW), BF16), jax.ShapeDtypeStruct((T, RET_VW), F32),
                            jax.ShapeDtypeStruct((B, RET_HEADS, n_sc, RET_DK, RET_DV), BF16)],
                 scratch_shapes=[pltpu.VMEM((RET_DK, RET_DV), F32)], args=(proj, *consts, gn), ride=ride)


def _ret_bwd(proj, consts, gn, o, states, dy, B, S, name, ride=None):
    T = B * S
    n_sc = S // RET_SC
    sp = _ret_specs(B, S, True)

    def body(p_ref, cos_ref, sin_ref, m_ref, qd_ref, kd_ref, cd_ref, gn_ref, o_ref, st_ref, dy_ref,
             dp_ref, dgn_ref, dstate):
        b, c = pl.program_id(1), pl.program_id(2)

        @pl.when(c == 0)
        def _():
            dstate[...] = jnp.zeros_like(dstate)

        @pl.when((b == 0) & (c == 0))
        def _():
            dgn_ref[...] = jnp.zeros_like(dgn_ref)

        cs, sn = cos_ref[...], sin_ref[...]
        q, k, v, gate = _ret_qkvg(p_ref, cs, sn)
        qb, kb = q.astype(BF16), k.astype(BF16)
        kt = (k * kd_ref[...]).astype(BF16)
        sb = st_ref[...]
        m = m_ref[...]
        ohat, rstd = _group_norm(o_ref[...])
        gnv = gn_ref[...]
        dyv = dy_ref[...].astype(F32)
        sg = jax.nn.sigmoid(gate)
        don = dyv * (gate * sg)
        dgate = dyv * (ohat * gnv) * (sg * (1.0 + gate * (1.0 - sg)))
        dgn_ref[...] += jnp.sum(don * ohat, axis=0, keepdims=True)
        dohat = don * gnv
        do = rstd * (dohat - jnp.mean(dohat, axis=-1, keepdims=True)
                     - ohat * jnp.mean(dohat * ohat, axis=-1, keepdims=True))
        dob = do.astype(BF16)
        doq = (do * qd_ref[...]).astype(BF16)
        dsb = dstate[...].astype(BF16)
        p = (_dot_nt(qb, kb) * m).astype(BF16)
        dv = _dot_tn(p, dob) + _dot(kt, dsb)
        dkt = _dot_nt(v, dsb)
        dsc = (_dot_nt(dob, v) * m).astype(BF16)
        dq = _dot(dsc, kb) + _dot_nt(doq, sb)
        dk = (_dot_tn(dsc, qb) + dkt * kd_ref[...]) * (RET_DK ** -0.5)
        dstate[...] = dstate[...] * cd_ref[...] + _dot_tn(qb, doq)
        dp_ref[:, 0:RET_DK] = _rope_bwd(dq, cs, sn).astype(BF16)
        dp_ref[:, RET_DK:2 * RET_DK] = _rope_bwd(dk, cs, sn).astype(BF16)
        dp_ref[:, 2 * RET_DK:2 * RET_DK + RET_DV] = dv.astype(BF16)
        dp_ref[:, 2 * RET_DK + RET_DV:RET_HEAD_COLS] = dgate.astype(BF16)

    return _call(body, name=name, grid=(RET_HEADS, B, n_sc),
                 in_specs=[sp["proj"], sp["trig"], sp["trig"], sp["mask"], sp["dec"], sp["dec"], sp["cdec"], sp["gn"],
                           sp["val"], sp["state"], sp["val"]],
                 out_specs=[sp["proj"], sp["gn"]],
                 out_shape=[jax.ShapeDtypeStruct((T, RET_HEADS * RET_HEAD_COLS), BF16), jax.ShapeDtypeStruct((1, RET_VW), F32)],
                 scratch_shapes=[pltpu.VMEM((RET_DK, RET_DV), F32)], args=(proj, *consts, gn, o, states, dy), ride=ride)


def _rel_index():
    t = np.arange(ATT_EXT)
    return np.clip(t - (ATT_KW - 1), -MAX_REL, CHUNK - 1) + MAX_REL


def _bias_matrices(rel):
    idx = _rel_index()
    n_lo = int(np.argmax(idx > 0))
    n_hi = ATT_EXT - (n_lo + REL_TABLE - 1)
    ext = jnp.concatenate([jnp.broadcast_to(rel[:, :1], (ATT_HEADS, n_lo - 1)), rel,
                           jnp.broadcast_to(rel[:, -1:], (ATT_HEADS, n_hi))], axis=1)
    e_pad = jnp.pad(ext, ((0, 0), (0, 1)))
    tiled = jnp.broadcast_to(e_pad[:, None, :], (ATT_HEADS, ATT_CQ, ATT_EXT + 1))
    skew = tiled.reshape(ATT_HEADS, -1)[:, :ATT_CQ * ATT_EXT].reshape(ATT_HEADS, ATT_CQ, ATT_EXT)
    toep = skew[:, :, ATT_CQ - 1:ATT_CQ - 1 + ATT_KW]
    i = np.arange(ATT_CQ)[:, None]
    j = np.arange(ATT_KW)[None, :]
    lo = CHUNK * (i // CHUNK)
    band = np.where((j >= lo) & (j < lo + BAND_PAST + CHUNK), 0.0, NEG)
    variants = np.stack([band + np.where(j + v * ATT_CQ >= BAND_PAST, 0.0, NEG) for v in range(ATT_VARIANTS)])
    return toep.reshape(ATT_PAIRS, 1, 2, ATT_CQ, ATT_KW) + variants.astype(np.float32)[None, :, None]


ATT_STRIP = 32
ATT_SCALE = ATT_DH ** -0.5


def _strips(fn):
    def strip(r, carry):
        fn(pl.ds(pl.multiple_of(r * ATT_STRIP, ATT_STRIP), ATT_STRIP))
        return carry

    lax.fori_loop(0, ATT_CQ // ATT_STRIP, strip, 0, unroll=True)


def _attn_prepare(kpad, vpad, qm, qkv_ref):
    kpad[0:BAND_PAST, :] = jnp.zeros((BAND_PAST, LANES), BF16)
    vpad[0:BAND_PAST, :] = jnp.zeros((BAND_PAST, LANES), BF16)
    kpad[BAND_PAST:, :] = qkv_ref[1]
    vpad[BAND_PAST:, :] = qkv_ref[2]
    lane = lax.broadcasted_iota(jnp.int32, (1, LANES), 1)
    q = qkv_ref[0] * ATT_SCALE
    for hh in range(2):
        qm[hh] = jnp.where((lane >= ATT_DH * hh) & (lane < ATT_DH * (hh + 1)), q, jnp.zeros_like(q))


def _attn_scores(s_ref, qm, kpad, b_ref, t, n_qb):
    t = jnp.minimum(t, n_qb - 1)
    qs = pl.multiple_of(t * ATT_CQ, ATT_CQ)
    variant = jnp.minimum(t, ATT_VARIANTS - 1)
    kw = kpad[pl.ds(qs, ATT_KW), :]
    for hh in range(2):
        s_ref[hh] = _dot_nt(qm[hh, pl.ds(qs, ATT_CQ), :], kw) + b_ref[variant, hh]


def _attn_softmax(s_ref, e_ref, linv_ref, m_ref=None):
    def strip(rows):
        s = s_ref[rows, :]
        m = jnp.max(s, axis=-1, keepdims=True)
        e = jnp.exp(s - m)
        e_ref[rows, :] = e.astype(BF16)
        linv_ref[rows, :] = jnp.broadcast_to(1.0 / jnp.sum(e, axis=-1, keepdims=True), (ATT_STRIP, LANES))
        if m_ref is not None:
            m_ref[rows, :] = jnp.broadcast_to(m, (ATT_STRIP, LANES))

    _strips(strip)


def _attn_fwd(qkv3, bias, B, S, name, ride=None):
    T = B * S
    n_qb = S // ATT_CQ

    def body(qkv_ref, b_ref, o_ref, kpad, vpad, qm, s_scr, e_scr, linv_scr):
        _attn_prepare(kpad, vpad, qm, qkv_ref)
        e_scr[...] = jnp.zeros_like(e_scr)
        linv_scr[...] = jnp.zeros_like(linv_scr)
        lane = lax.broadcasted_iota(jnp.int32, (1, LANES), 1)

        def softmax(slot):
            for hh in range(2):
                _attn_softmax(s_scr.at[slot, hh], e_scr.at[slot, hh], linv_scr.at[slot, hh])

        def output(t, slot):
            qs = pl.multiple_of(jnp.maximum(t, 0) * ATT_CQ, ATT_CQ)
            vw = vpad[pl.ds(qs, ATT_KW), :]
            outs = [_dot(e_scr[slot, hh], vw) * linv_scr[slot, hh] for hh in range(2)]
            o_ref[pl.ds(qs, ATT_CQ), :] = jnp.where(lane < ATT_DH, outs[0], outs[1]).astype(BF16)

        def pair(u, carry):
            t = 2 * u
            _attn_scores(s_scr.at[1], qm, kpad, b_ref, t + 1, n_qb)
            softmax(0)
            output(t - 1, 1)
            _attn_scores(s_scr.at[0], qm, kpad, b_ref, t + 2, n_qb)
            softmax(1)
            output(t, 0)
            return carry

        _attn_scores(s_scr.at[0], qm, kpad, b_ref, 0, n_qb)
        lax.fori_loop(0, n_qb // 2, pair, 0)
        output(n_qb - 1, 1)

    return _call(body, name=name, grid=(ATT_PAIRS, B),
                 in_specs=[pl.BlockSpec((3, S, LANES), lambda hp, b: (0, b, hp)),
                           pl.BlockSpec((None, ATT_VARIANTS, 2, ATT_CQ, ATT_KW), lambda hp, b: (hp, 0, 0, 0, 0))],
                 out_specs=[pl.BlockSpec((S, LANES), lambda hp, b: (b, hp))],
                 out_shape=[jax.ShapeDtypeStruct((T, D_MODEL), BF16)],
                 scratch_shapes=[pltpu.VMEM((S + BAND_PAST, LANES), BF16), pltpu.VMEM((S + BAND_PAST, LANES), BF16),
                                 pltpu.VMEM((2, S, LANES), BF16), pltpu.VMEM((2, 2, ATT_CQ, ATT_KW), F32),
                                 pltpu.VMEM((2, 2, ATT_CQ, ATT_KW), BF16), pltpu.VMEM((2, 2, ATT_CQ, LANES), F32)],
                 args=(qkv3, bias), ride=ride)


def _attn_bwd(qkv3, bias, do, B, S, name, ride=None):
    T = B * S
    n_qb = S // ATT_CQ

    def body(qkv_ref, b_ref, do_ref, dqkv_ref, db_ref, kpad, vpad, qm, dkacc, dvacc):
        @pl.when(pl.program_id(1) == 0)
        def _():
            db_ref[...] = jnp.zeros_like(db_ref)

        _attn_prepare(kpad, vpad, qm, qkv_ref)
        dkacc[...] = jnp.zeros_like(dkacc)
        dvacc[...] = jnp.zeros_like(dvacc)
        lane = lax.broadcasted_iota(jnp.int32, (1, LANES), 1)

        def step(qb, carry):
            qs = pl.multiple_of(qb * ATT_CQ, ATT_CQ)
            variant = jnp.minimum(qb, ATT_VARIANTS - 1)
            dov = do_ref[pl.ds(qs, ATT_CQ), :].astype(F32)
            kw = kpad[pl.ds(qs, ATT_KW), :]
            vw = vpad[pl.ds(qs, ATT_KW), :]
            heads = (0, 1)
            qmh = [qm[hh, pl.ds(qs, ATT_CQ), :] for hh in heads]
            s = [_dot_nt(qmh[hh], kw) + b_ref[variant, hh] for hh in heads]
            e, linv, dom, dp = [None, None], [None, None], [None, None], [None, None]
            for hh in heads:
                e[hh] = jnp.exp(s[hh] - jnp.max(s[hh], axis=-1, keepdims=True))
                linv[hh] = 1.0 / jnp.sum(e[hh], axis=-1, keepdims=True)
                sel = (lane >= ATT_DH * hh) & (lane < ATT_DH * (hh + 1))
                dom[hh] = jnp.where(sel, dov * linv[hh], 0.0).astype(BF16)
                dp[hh] = _dot_nt(dom[hh], vw)
            dqs, dk, dv = [], None, None
            for hh in heads:
                ds = e[hh] * (dp[hh] - jnp.sum(dp[hh] * e[hh], axis=-1, keepdims=True) * linv[hh])
                db_ref[hh] += ds
                dsb = ds.astype(BF16)
                dqs.append(_dot(dsb, kw) * ATT_SCALE)
                dk_h = _dot_tn(dsb, qmh[hh])
                dv_h = _dot_tn(e[hh].astype(BF16), dom[hh])
                dk = dk_h if dk is None else dk + dk_h
                dv = dv_h if dv is None else dv + dv_h
            dqkv_ref[0, pl.ds(qs, ATT_CQ), :] = jnp.where(lane < ATT_DH, dqs[0], dqs[1]).astype(BF16)
            dkacc[pl.ds(qs, ATT_KW), :] += dk
            dvacc[pl.ds(qs, ATT_KW), :] += dv
            return carry

        lax.fori_loop(0, n_qb, step, 0)
        dqkv_ref[1] = dkacc[BAND_PAST:, :].astype(BF16)
        dqkv_ref[2] = dvacc[BAND_PAST:, :].astype(BF16)

    bias_spec = pl.BlockSpec((None, ATT_VARIANTS, 2, ATT_CQ, ATT_KW), lambda hp, b: (hp, 0, 0, 0, 0))
    dbias_spec = pl.BlockSpec((None, 2, ATT_CQ, ATT_KW), lambda hp, b: (hp, 0, 0, 0))
    qkv_spec = pl.BlockSpec((3, S, LANES), lambda hp, b: (0, b, hp))
    return _call(body, name=name, grid=(ATT_PAIRS, B),
                 in_specs=[qkv_spec, bias_spec, pl.BlockSpec((S, LANES), lambda hp, b: (b, hp))],
                 out_specs=[qkv_spec, dbias_spec],
                 out_shape=[jax.ShapeDtypeStruct((3, T, D_MODEL), BF16),
                            jax.ShapeDtypeStruct((ATT_PAIRS, 2, ATT_CQ, ATT_KW), F32)],
                 scratch_shapes=[pltpu.VMEM((S + BAND_PAST, LANES), BF16), pltpu.VMEM((S + BAND_PAST, LANES), BF16),
                                 pltpu.VMEM((2, S, LANES), BF16),
                                 pltpu.VMEM((S + BAND_PAST, LANES), F32), pltpu.VMEM((S + BAND_PAST, LANES), F32)],
                 args=(qkv3, bias, do), ride=ride)


def _rel_bias_grad(dbias, name):
    d = dbias.reshape(ATT_HEADS, ATT_CQ, ATT_KW)
    d = jnp.pad(d, ((0, 0), (0, 0), (ATT_CQ - 1, 0))).reshape(ATT_HEADS, ATT_CQ * ATT_EXT)
    skew = jnp.pad(d, ((0, 0), (0, ATT_CQ))).reshape(ATT_HEADS, ATT_CQ, ATT_EXT + 1)
    fold = np.zeros((ATT_EXT + 1, REL_PAD), np.float32)
    fold[np.arange(ATT_EXT), _rel_index()] = 1.0
    fold = jnp.asarray(fold, BF16)

    def body(s_ref, f_ref, o_ref):
        x = jnp.sum(s_ref[...], axis=0, keepdims=True)
        x = jnp.broadcast_to(x, (8, ATT_EXT + 1))
        acc = jnp.zeros((8, REL_PAD), F32)
        for _ in range(3):
            part = x.astype(BF16)
            acc = acc + _dot(part, f_ref[...])
            x = x - part.astype(F32)
        o_ref[...] = acc[0:1, :]

    out = _call(body, name=name, grid=(ATT_HEADS,),
                in_specs=[pl.BlockSpec((None, ATT_CQ, ATT_EXT + 1), lambda h: (h, 0, 0)),
                          pl.BlockSpec((ATT_EXT + 1, REL_PAD), lambda h: (0, 0))],
                out_specs=[pl.BlockSpec((None, 1, REL_PAD), lambda h: (h, 0, 0))],
                out_shape=[jax.ShapeDtypeStruct((ATT_HEADS, 1, REL_PAD), F32)], args=(skew, fold))[0]
    return out.reshape(ATT_HEADS, REL_PAD)


def _adamw(w, g, m, v):
    m = ADAM_B1 * m + (1.0 - ADAM_B1) * g
    v = ADAM_B2 * v + (1.0 - ADAM_B2) * (g * g)
    m_hat = m / (1.0 - ADAM_B1 ** ADAM_STEP)
    v_hat = v / (1.0 - ADAM_B2 ** ADAM_STEP)
    delta = -ADAM_LR * (m_hat / (jnp.sqrt(v_hat) + ADAM_EPS) + ADAM_WD * w)
    return delta, m, v


def _sum_devices(ref):
    g = ref[0].astype(F32)
    for d in range(1, N_DEV):
        g = g + ref[d].astype(F32)
    return g


def _adamw_reduce(lands, w, m, v, name, tr=256):
    L, R, C = w.shape
    tr = min(tr, R)
    n_i = R // tr

    def body(*refs):
        l_refs = refs[:L]
        w_ref, m_ref, v_ref, g_out, d_out, m_out, v_out = refs[L:]
        layer = pl.program_id(0)
        for l in range(L):
            @pl.when(layer == l)
            def _(l=l):
                g = _sum_devices(l_refs[l])
                g_out[...] = g
                d_out[...], m_out[...], v_out[...] = _adamw(w_ref[...], g, m_ref[...], v_ref[...])

    def land_spec(l):
        return pl.BlockSpec((N_DEV, tr, C), lambda ly, i: (0, jnp.where(ly == l, i, jnp.where(ly < l, 0, n_i - 1)), 0))

    blk = pl.BlockSpec((None, tr, C), lambda ly, i: (ly, i, 0))
    return _call(body, name=name, grid=(L, n_i), in_specs=[land_spec(l) for l in range(L)] + [blk, blk, blk],
                 out_specs=[blk] * 4, out_shape=[jax.ShapeDtypeStruct((L, R, C), F32)] * 4, args=(*lands, w, m, v))


def _small_update(land_small, land_rel, w, m, v, name):
    def body(ls_ref, lr_ref, w_ref, m_ref, v_ref, g_out, d_out, m_out, v_out, rel_out):
        g = _sum_devices(ls_ref)
        g_out[...] = g
        d_out[...], m_out[...], v_out[...] = _adamw(w_ref[...], g, m_ref[...], v_ref[...])
        rel_out[...] = _sum_devices(lr_ref)

    return pl.pallas_call(
        body, name=name,
        out_shape=[jax.ShapeDtypeStruct(w.shape, F32)] * 4 + [jax.ShapeDtypeStruct(land_rel.shape[1:], F32)],
    )(land_small, land_rel, w, m, v)


def _adamw_plain(w, g, m, v, name):
    def body(w_ref, g_ref, m_ref, v_ref, d_out, m_out, v_out):
        d_out[...], m_out[...], v_out[...] = _adamw(w_ref[...], g_ref[...], m_ref[...], v_ref[...])

    return pl.pallas_call(body, name=name, out_shape=[jax.ShapeDtypeStruct(w.shape, F32)] * 3)(w, g, m, v)


def _ret_pieces():
    return [(RET_DK, lambda h: h, 0),
            (RET_DK, lambda h: RET_HEADS + h, RET_DK),
            (RET_DV, lambda h: RET_HEADS + h, 2 * RET_DK),
            (RET_DV, lambda h: 2 * RET_HEADS + h, 2 * RET_DK + RET_DV)]


def _ret_piece_of_column_block(p):
    per_head = RET_HEAD_COLS // RET_DK
    qk = jnp.where(p < RET_HEADS, per_head * p, per_head * (p - RET_HEADS) + 1)
    pv = p - 2 * RET_HEADS
    vv = per_head * (pv // 2) + 2 + pv % 2
    pg = p - 4 * RET_HEADS
    gg = per_head * (pg // 2) + 4 + pg % 2
    return jnp.where(p < 2 * RET_HEADS, qk, jnp.where(p < 4 * RET_HEADS, vv, gg))


def _gather(shard, full_shape, view):
    return _Exchange([shard], [(0, _whole, 0, view)], [jax.ShapeDtypeStruct(full_shape, shard.dtype)])


def _step(x, target, shards, mix_g, gn_g, rel_shard, mlp_g, final_g):
    s_rin, s_rout, s_ain, s_aout, s_w1, s_w2 = shards
    B, S, _ = x.shape
    T = B * S
    tk = min(T, 2048)
    n_k = T // tk
    h0 = x.reshape(T, D_MODEL)
    tgt = target.reshape(T, D_MODEL)
    consts = _ret_constants(S)
    att_pieces = [(D_MODEL, lambda j: j, 0)]
    w1_cols = w2_rows = D_FF // N_DEV

    def qkv_spec(tm):
        return pl.BlockSpec((None, tm, D_MODEL), lambda i, j: (j, i, 0))

    def head_spec(tm):
        return pl.BlockSpec((tm, RET_HEAD_COLS), lambda i, j: (i, j))

    rel_cols = REL_TABLE // N_DEV
    w_rin, rel_all = _all_gather(
        [s_rin, rel_shard],
        [jax.ShapeDtypeStruct((D_MODEL, RET_IN), BF16), jax.ShapeDtypeStruct((N_DEV, ATT_HEADS, rel_cols), F32)],
        [_cols(RET_IN // N_DEV), _slot], "gather_first")
    bias = _bias_matrices(rel_all.transpose(1, 0, 2).reshape(ATT_HEADS, REL_TABLE))

    ride = _Exchange([s_rout, s_w1[0]], [(0, _whole, 0, _rows(RET_VW // N_DEV)), (1, _whole, 1, _cols(w1_cols))],
                     [jax.ShapeDtypeStruct((RET_VW, D_MODEL), BF16), jax.ShapeDtypeStruct((D_MODEL, D_FF), BF16)])
    (proj, n0), (w_rout, w1_0) = _norm_proj(h0, mix_g[0:1], w_rin, _ret_pieces(), RET_HEADS,
                                            jax.ShapeDtypeStruct((T, RET_HEADS * RET_HEAD_COLS), BF16), head_spec,
                                            "ret_proj", ride=ride)
    ride = _Exchange([s_w2[0], s_ain], [(0, _whole, 0, _rows(w2_rows)), (1, _whole, 1, _cols(3 * D_MODEL // N_DEV))],
                     [jax.ShapeDtypeStruct((D_FF, D_MODEL), BF16), jax.ShapeDtypeStruct((D_MODEL, 3 * D_MODEL), BF16)])
    (y, o, states), (w2_0, w_ain) = _ret_fwd(proj, consts, gn_g, B, S, "ret_fwd", ride=ride)
    h1 = _matmul_res(y, w_rout, h0, "ret_out")
    ride = _Exchange([s_aout, s_w1[1]], [(0, _whole, 0, _rows(D_MODEL // N_DEV)), (1, _whole, 1, _cols(w1_cols))],
                     [jax.ShapeDtypeStruct((D_MODEL, D_MODEL), BF16), jax.ShapeDtypeStruct((D_MODEL, D_FF), BF16)])
    (h2, n1, u1), (w_aout, w1_1) = _mlp_fwd(h1, mlp_g[0:1], w1_0, w2_0, "mlp0_fwd", ride=ride)
    qkv3, n2 = _norm_proj(h2, mix_g[1:2], w_ain, att_pieces, 3, jax.ShapeDtypeStruct((3, T, D_MODEL), BF16), qkv_spec,
                          "att_proj")
    (o2,), (w2_1,) = _attn_fwd(qkv3, bias, B, S, "att_fwd", ride=_gather(s_w2[1], (D_FF, D_MODEL), _rows(w2_rows)))
    h3 = _matmul_res(o2, w_aout, h2, "att_out")
    h4, n3, u3 = _mlp_fwd(h3, mlp_g[1:2], w1_1, w2_1, "mlp1_fwd")
    dh4, dh4b, loss, dg_final = _final_loss(h4, final_g, tgt, "final_loss")

    def tok(width):
        return dict(a=pl.BlockSpec((tk, width), lambda i, j, k: (k, i)), b=pl.BlockSpec((tk, width), lambda i, j, k: (k, j)))

    def mlp_grads(dh, dhb, u, n, w1, w2, h, g, tag, ride=None):
        res = _mlp_bwd(dh, u, w1, w2, h, g, tag + "_bwd", ride=ride)
        (dhi, dhib, du, dg), lands = res if ride is not None else (res, None)
        gw2 = _wgrad(u, tok(1024)["a"], dhb, tok(1024)["b"], D_FF, D_MODEL, 1024, 1024, n_k, tag + "_dw2", square_a=True)
        gw1 = _wgrad(n, tok(1024)["a"], du, tok(1024)["b"], D_MODEL, D_FF, 1024, 1024, n_k, tag + "_dw1")
        return dhi, dhib, gw1, gw2, dg, lands

    dh3, dh3b, gw1_1, gw2_1, dg_mlp1, _ = mlp_grads(dh4, dh4b, u3, n3, w1_1, w2_1, h3, mlp_g[1:2], "mlp1")
    do2 = _matmul_nt(dh3b, w_aout, "att_out_bwd")
    g_aout = _wgrad(o2, tok(1024)["a"], dh3b, tok(1024)["b"], D_MODEL, D_MODEL, 1024, 1024, n_k, "att_out_dw")
    ride = _Exchange(
        [gw1_1, gw2_1, g_aout],
        [(0, _cols(w1_cols), 0, _slot), (1, _rows(w2_rows), 1, _slot), (2, _rows(D_MODEL // N_DEV), 2, _slot)],
        [jax.ShapeDtypeStruct((N_DEV, D_MODEL, w1_cols), BF16), jax.ShapeDtypeStruct((N_DEV, w2_rows, D_MODEL), BF16),
         jax.ShapeDtypeStruct((N_DEV, D_MODEL // N_DEV, D_MODEL), BF16)])
    (dqkv3, dbias), (l_w1_1, l_w2_1, l_aout) = _attn_bwd(qkv3, bias, do2, B, S, "att_bwd", ride=ride)
    g_rel = _rel_bias_grad(dbias, "rel_bias_grad")
    dh2, dh2b, dg_mix1 = _proj_bwd(dqkv3, qkv_spec, w_ain, att_pieces, 3, h2, mix_g[1:2], dh3, "att_proj_bwd")
    g_ain = _wgrad(n2, tok(1024)["a"], dqkv3, pl.BlockSpec((None, tk, D_MODEL), lambda i, j, k: (j, k, 0)),
                   D_MODEL, 3 * D_MODEL, 1024, 1024, n_k, "att_proj_dw")
    ride = _Exchange([g_ain], [(0, _cols(3 * D_MODEL // N_DEV), 0, _slot)],
                     [jax.ShapeDtypeStruct((N_DEV, D_MODEL, 3 * D_MODEL // N_DEV), BF16)])
    dh1, dh1b, gw1_0, gw2_0, dg_mlp0, (l_ain,) = mlp_grads(dh2, dh2b, u1, n1, w1_0, w2_0, h1, mlp_g[0:1], "mlp0", ride=ride)
    dy = _matmul_nt(dh1b, w_rout, "ret_out_bwd")
    g_rout = _wgrad(y, tok(1024)["a"], dh1b, tok(1024)["b"], RET_VW, D_MODEL, 1024, 1024, n_k, "ret_out_dw")
    ride = _Exchange(
        [gw1_0, gw2_0, g_rout],
        [(0, _cols(w1_cols), 0, _slot), (1, _rows(w2_rows), 1, _slot), (2, _rows(RET_VW // N_DEV), 2, _slot)],
        [jax.ShapeDtypeStruct((N_DEV, D_MODEL, w1_cols), BF16), jax.ShapeDtypeStruct((N_DEV, w2_rows, D_MODEL), BF16),
         jax.ShapeDtypeStruct((N_DEV, RET_VW // N_DEV, D_MODEL), BF16)])
    (dproj, dgn), (l_w1_0, l_w2_0, l_rout) = _ret_bwd(proj, consts, gn_g, o, states, dy, B, S, "ret_bwd", ride=ride)
    g_rin = _wgrad(n0, tok(1024)["a"], dproj,
                   pl.BlockSpec((tk, RET_DK), lambda i, j, k: (k, _ret_piece_of_column_block(j))),
                   D_MODEL, RET_IN, 1024, RET_DK, n_k, "ret_proj_dw")
    ride = _Exchange([g_rin], [(0, _cols(RET_IN // N_DEV), 0, _slot)],
                     [jax.ShapeDtypeStruct((N_DEV, D_MODEL, RET_IN // N_DEV), BF16)])
    (dx, _, dg_mix0), (l_rin,) = _proj_bwd(dproj, head_spec, w_rin, _ret_pieces(), RET_HEADS, h0, mix_g[0:1], dh1,
                                           "ret_proj_bwd", ride=ride)

    small = jnp.concatenate([dg_mix0, dg_mix1, dg_mlp0, dg_mlp1, dg_final, dgn.reshape(2, D_MODEL),
                             jnp.zeros((1, D_MODEL), F32)], axis=0)
    l_small, l_rel = _Exchange(
        [small, g_rel], [(0, _whole, 0, _slot), (1, _whole, 1, _slot)],
        [jax.ShapeDtypeStruct((N_DEV, 8, D_MODEL), F32), jax.ShapeDtypeStruct((N_DEV, ATT_HEADS, REL_PAD), F32)],
    ).run("scatter_small")
    lands = dict(rin=l_rin, rout=l_rout, ain=l_ain, aout=l_aout, w1=(l_w1_0, l_w1_1), w2=(l_w2_0, l_w2_1),
                 small=l_small, rel=l_rel)
    return loss[0, 0], dx.reshape(B, S, D_MODEL), lands


def kernel(x, mix_norm_g, ret_w_in, ret_gn_g, ret_w_out, att_w_in, att_rel_bias, att_w_out, mlp_norm_g, mlp_w1, mlp_w2, final_norm_g, loss_target, m_mix_norm_g, m_ret_w_in, m_ret_gn_g, m_ret_w_out, m_att_w_in, m_att_rel_bias, m_att_w_out, m_mlp_norm_g, m_mlp_w1, m_mlp_w2, m_final_norm_g, v_mix_norm_g, v_ret_w_in, v_ret_gn_g, v_ret_w_out, v_att_w_in, v_att_rel_bias, v_att_w_out, v_mlp_norm_g, v_mlp_w1, v_mlp_w2, v_final_norm_g):
    me = _lin(_place())
    rel_cols = REL_TABLE // N_DEV
    shards = (ret_w_in[0].astype(BF16), ret_w_out[0].astype(BF16), att_w_in[0].astype(BF16), att_w_out[0].astype(BF16),
              (mlp_w1[0].astype(BF16), mlp_w1[1].astype(BF16)), (mlp_w2[0].astype(BF16), mlp_w2[1].astype(BF16)))
    loss_part, grad_x, lands = _step(x, loss_target, shards, mix_norm_g, ret_gn_g, att_rel_bias[0], mlp_norm_g,
                                     final_norm_g.reshape(1, D_MODEL))
    loss = lax.psum(loss_part, ("x", "y", "c"))

    u_rin = _adamw_reduce([lands["rin"]], ret_w_in, m_ret_w_in, v_ret_w_in, "update_ret_w_in")
    u_rout = _adamw_reduce([lands["rout"]], ret_w_out, m_ret_w_out, v_ret_w_out, "update_ret_w_out")
    u_ain = _adamw_reduce([lands["ain"]], att_w_in, m_att_w_in, v_att_w_in, "update_att_w_in")
    u_aout = _adamw_reduce([lands["aout"]], att_w_out, m_att_w_out, v_att_w_out, "update_att_w_out")
    u_w1 = _adamw_reduce(lands["w1"], mlp_w1, m_mlp_w1, v_mlp_w1, "update_mlp_w1")
    u_w2 = _adamw_reduce(lands["w2"], mlp_w2, m_mlp_w2, v_mlp_w2, "update_mlp_w2")

    def pack(mix, mlp, fin, gn):
        return jnp.concatenate([mix, mlp, fin.reshape(1, D_MODEL), gn.reshape(2, D_MODEL), jnp.zeros((1, D_MODEL), F32)], axis=0)

    small_w = pack(mix_norm_g, mlp_norm_g, final_norm_g, ret_gn_g)
    small_m = pack(m_mix_norm_g, m_mlp_norm_g, m_final_norm_g, m_ret_gn_g)
    small_v = pack(v_mix_norm_g, v_mlp_norm_g, v_final_norm_g, v_ret_gn_g)
    sg, sd, sm, sv, rel_sum = _small_update(lands["small"], lands["rel"], small_w, small_m, small_v, "update_small")
    g_rel_mine = lax.dynamic_slice(rel_sum, (0, me * rel_cols), (ATT_HEADS, rel_cols))
    rel_d, rel_m, rel_v = _adamw_plain(att_rel_bias[0], g_rel_mine, m_att_rel_bias[0], v_att_rel_bias[0], "update_rel_bias")
    u_rel = [g_rel_mine[None], rel_d[None], rel_m[None], rel_v[None]]

    def unpack(t):
        return dict(mix=t[0:2], mlp=t[2:4], fin=t[4], gn=t[5:7].reshape(1, RET_VW))

    us = [unpack(t) for t in (sg, sd, sm, sv)]
    outs = [loss, grad_x]
    for k in range(4):
        outs += [us[k]["mix"], u_rin[k], us[k]["gn"], u_rout[k], u_ain[k], u_rel[k], u_aout[k], us[k]["mlp"],
                 u_w1[k], u_w2[k], us[k]["fin"]]
    return tuple(outs)
```

```python
import functools

import numpy as np
import jax
import jax.numpy as jnp
from jax import lax
from jax.experimental import pallas as pl
from jax.experimental.pallas import tpu as pltpu

F32, BF16 = jnp.float32, jnp.bfloat16

D_MODEL = 1024
CHUNK = 64
RET_HEADS, RET_DK, RET_DV = 4, 256, 512
RET_QK, RET_VW = RET_HEADS * RET_DK, RET_HEADS * RET_DV
RET_IN = 2 * RET_QK + 2 * RET_VW
RET_HEAD_COLS = 2 * RET_DK + 2 * RET_DV
RET_SC = 256
ROPE_BASE = 10000.0
ATT_HEADS, ATT_DH = 16, 64
LANES = 128
ATT_PAIRS = ATT_HEADS * ATT_DH // LANES
BAND_PAST = 8 * CHUNK
MAX_REL = 256
REL_TABLE = MAX_REL + CHUNK
REL_PAD = 384
ATT_CQ = 256
ATT_KW = ATT_CQ + BAND_PAST
ATT_EXT = ATT_CQ + ATT_KW - 1
ATT_VARIANTS = BAND_PAST // ATT_CQ + 1
D_FF = 4 * D_MODEL
EPS = 1e-6
EPILOGUE_ROWS = 256
NEG = -1e30
N_DEV = 8
N_PEERS = N_DEV - 1

ADAM_LR, ADAM_B1, ADAM_B2, ADAM_EPS, ADAM_WD, ADAM_STEP = 0.001, 0.9, 0.999, 1e-08, 0.01, 10

VMEM_LIMIT = 56 * 1024 * 1024
MESH = pl.DeviceIdType.MESH
ANY = pl.BlockSpec(memory_space=pl.ANY)


def _dot(a, b):
    return jnp.dot(a, b, preferred_element_type=F32)


def _dot_nt(a, b):
    return lax.dot_general(a, b, (((1,), (1,)), ((), ())), preferred_element_type=F32)


def _dot_tn(a, b):
    return lax.dot_general(a, b, (((0,), (0,)), ((), ())), preferred_element_type=F32)


def _rms(x, g):
    r = lax.rsqrt(jnp.mean(x * x, axis=-1, keepdims=True) + EPS)
    return x * r * g


def _rms_bwd(dn, x, g):
    r = lax.rsqrt(jnp.mean(x * x, axis=-1, keepdims=True) + EPS)
    xh = x * r
    dg = jnp.sum(dn * xh, axis=0, keepdims=True)
    dxh = dn * g
    dx = r * (dxh - xh * jnp.mean(dxh * xh, axis=-1, keepdims=True))
    return dx, dg


def _rms_bwd_epilogue(x_ref, g_ref, dres_ref, dx_ref, dxb_ref, dg_ref):
    for r in range(0, dx_ref.shape[0], EPILOGUE_ROWS):
        rows = slice(r, r + EPILOGUE_ROWS)
        dx, dg = _rms_bwd(dx_ref[rows, :], x_ref[rows, :], g_ref[...])
        dx = dres_ref[rows, :] + dx
        dx_ref[rows, :] = dx
        dxb_ref[rows, :] = dx.astype(BF16)
        dg_ref[...] += dg


def _place():
    return lax.axis_index("x"), lax.axis_index("y"), lax.axis_index("c")


def _lin(p):
    return 4 * p[0] + 2 * p[1] + p[2]


def _cols(width):
    return lambda ref, blk: ref.at[:, pl.ds(pl.multiple_of(blk * width, LANES), width)]


def _rows(height):
    return lambda ref, blk: ref.at[pl.ds(pl.multiple_of(blk * height, 8), height), :]


def _slot(ref, blk):
    return ref.at[blk]


def _whole(ref, blk):
    return ref


class _Exchange:
    def __init__(self, sources, flows, land_shapes):
        self.sources, self.flows, self.land_shapes = list(sources), flows, list(land_shapes)
        n_f = len(flows)
        self.sem_shapes = [pltpu.SemaphoreType.DMA((N_PEERS * n_f,)), pltpu.SemaphoreType.DMA((N_PEERS * n_f,)),
                           pltpu.SemaphoreType.DMA((n_f,))]

    def _copies(self, srcs, lands, sems):
        send_sems, recv_sems, local_sems = sems
        x, y, c = _place()
        me = (x, y, c)
        peers = [(x ^ dx, y ^ dy, c ^ dc) for dx in (0, 1) for dy in (0, 1) for dc in (0, 1)][1:]

        def copy(f, k, sender, to):
            si, sview, li, lview = self.flows[f]
            return pltpu.make_async_remote_copy(
                src_ref=sview(srcs[si], _lin(to)), dst_ref=lview(lands[li], _lin(sender)),
                send_sem=send_sems.at[f * N_PEERS + k], recv_sem=recv_sems.at[f * N_PEERS + k],
                device_id=to, device_id_type=MESH)

        mine, sends, recvs = [], [], []
        for f, (si, sview, li, lview) in enumerate(self.flows):
            mine.append(pltpu.make_async_copy(sview(srcs[si], _lin(me)), lview(lands[li], _lin(me)), local_sems.at[f]))
            for k, peer in enumerate(peers):
                sends.append(copy(f, k, me, peer))
                recvs.append(copy(f, k, peer, me))
        return mine, sends, recvs

    def start(self, srcs, lands, sems):
        mine, sends, _ = self._copies(srcs, lands, sems)
        for cp in mine + sends:
            cp.start()

    def finish(self, srcs, lands, sems):
        mine, sends, recvs = self._copies(srcs, lands, sems)
        for cp in recvs:
            cp.wait_recv()
        for cp in sends:
            cp.wait_send()
        for cp in mine:
            cp.wait()

    def run(self, name):
        n_src, n_land = len(self.sources), len(self.land_shapes)

        def body(*refs):
            srcs, lands, sems = refs[:n_src], refs[n_src:n_src + n_land], refs[n_src + n_land:]
            self.start(srcs, lands, sems)
            self.finish(srcs, lands, sems)

        return pl.pallas_call(body, name=name, in_specs=[ANY] * n_src, out_specs=[ANY] * n_land,
                              out_shape=self.land_shapes, scratch_shapes=self.sem_shapes)(*self.sources)


def _call(body, *, name, grid, in_specs, out_specs, out_shape, args, scratch_shapes=(), ride=None):
    params = pltpu.CompilerParams(dimension_semantics=("arbitrary",) * len(grid), vmem_limit_bytes=VMEM_LIMIT)
    in_specs, out_specs, out_shape, scratch_shapes = list(in_specs), list(out_specs), list(out_shape), list(scratch_shapes)
    if ride is None:
        return pl.pallas_call(body, name=name, grid=grid, in_specs=in_specs, out_specs=out_specs, out_shape=out_shape,
                              scratch_shapes=scratch_shapes, compiler_params=params)(*args)
    n_in, n_out, n_scr = len(in_specs), len(out_specs), len(scratch_shapes)
    n_src, n_land = len(ride.sources), len(ride.land_shapes)

    def riding(*refs):
        bounds = np.cumsum([n_in, n_src, n_out, n_land, n_scr])
        ins, srcs, outs, lands, scr, sems = (refs[a:b] for a, b in zip([0, *bounds], [*bounds, len(refs)]))
        ids = [pl.program_id(d) for d in range(len(grid))]
        first = functools.reduce(lambda a, b: a & b, [i == 0 for i in ids])
        last = functools.reduce(lambda a, b: a & b, [i == n - 1 for i, n in zip(ids, grid)])

        @pl.when(first)
        def _():
            ride.start(srcs, lands, sems)

        body(*ins, *outs, *scr)

        @pl.when(last)
        def _():
            ride.finish(srcs, lands, sems)

    res = pl.pallas_call(
        riding, name=name, grid=grid, in_specs=in_specs + [ANY] * n_src, out_specs=out_specs + [ANY] * n_land,
        out_shape=out_shape + ride.land_shapes, scratch_shapes=scratch_shapes + ride.sem_shapes,
        compiler_params=params)(*args, *ride.sources)
    return res[:n_out], res[n_out:]


def _all_gather(shards, out_shapes, views, name):
    n = len(shards)

    def body(*refs):
        ins, outs = refs[:n], refs[n:2 * n]
        send_sems, recv_sems, local_sems = refs[2 * n:]
        x, y, c = _place()
        me, sibling = (x, y, c), (x, y, 1 - c)
        chips = [(1 - x, y), (x, 1 - y), (1 - x, 1 - y)]

        def copy(a, k, block, to, src=None):
            dst = views[a](outs[a], _lin(block))
            return pltpu.make_async_remote_copy(
                src_ref=dst if src is None else src, dst_ref=dst,
                send_sem=send_sems.at[a * N_PEERS + k], recv_sem=recv_sems.at[a * N_PEERS + k],
                device_id=to, device_id_type=MESH)

        mine = [pltpu.make_async_copy(ins[a], views[a](outs[a], _lin(me)), local_sems.at[a]) for a in range(n)]
        for cp in mine:
            cp.start()
        first = []
        for a in range(n):
            first.append(copy(a, 0, me, sibling, src=ins[a]))
            first += [copy(a, 1 + j, me, (*chip, c), src=ins[a]) for j, chip in enumerate(chips)]
        for cp in first:
            cp.start()
        passed = []
        for j, chip in enumerate(chips):
            for a in range(n):
                copy(a, 1 + j, (*chip, c), me).wait_recv()
                fwd = copy(a, 4 + j, (*chip, c), sibling)
                fwd.start()
                passed.append(fwd)
        for a in range(n):
            copy(a, 0, sibling, me).wait_recv()
            for j, chip in enumerate(chips):
                copy(a, 4 + j, (*chip, 1 - c), me).wait_recv()
        for cp in first + passed:
            cp.wait_send()
        for cp in mine:
            cp.wait()

    return pl.pallas_call(
        body, name=name,
        in_specs=[ANY] * n, out_specs=[ANY] * n, out_shape=out_shapes,
        scratch_shapes=[pltpu.SemaphoreType.DMA((N_PEERS * n,)), pltpu.SemaphoreType.DMA((N_PEERS * n,)),
                        pltpu.SemaphoreType.DMA((n,))],
    )(*shards)


def _norm_proj(h, g, w, pieces, n_steps, out_shape, out_spec, name, tm=1024, ride=None):
    T = h.shape[0]
    n_p = len(pieces)

    def body(h_ref, g_ref, *rest):
        w_refs = rest[:n_p]
        o_ref, n_ref = rest[n_p:]

        @pl.when(pl.program_id(1) == 0)
        def _():
            n_ref[...] = _rms(h_ref[...], g_ref[...]).astype(BF16)

        n = n_ref[...]
        for (width, _, start), w_ref in zip(pieces, w_refs):
            o_ref[:, start:start + width] = _dot(n, w_ref[...]).astype(o_ref.dtype)

    w_specs = [pl.BlockSpec((D_MODEL, width), functools.partial(lambda i, j, f: (0, f(j)), f=col))
               for (width, col, _) in pieces]
    row = pl.BlockSpec((tm, D_MODEL), lambda i, j: (i, 0))
    return _call(body, name=name, grid=(T // tm, n_steps),
                 in_specs=[row, pl.BlockSpec((1, D_MODEL), lambda i, j: (0, 0))] + w_specs,
                 out_specs=[out_spec(tm), row], out_shape=[out_shape, jax.ShapeDtypeStruct((T, D_MODEL), BF16)],
                 args=(h, g, *([w] * n_p)), ride=ride)


def _proj_bwd(dy, dy_spec, w, pieces, n_steps, x, g, dres, name, tm=1024, ride=None):
    T = x.shape[0]
    n_p = len(pieces)

    def body(dy_ref, *rest):
        w_refs = rest[:n_p]
        x_ref, g_ref, dres_ref, dx_ref, dxb_ref, dg_ref = rest[n_p:]
        acc = dx_ref
        i, j = pl.program_id(0), pl.program_id(1)

        @pl.when(j == 0)
        def _():
            acc[...] = jnp.zeros_like(acc)

        @pl.when((i == 0) & (j == 0))
        def _():
            dg_ref[...] = jnp.zeros_like(dg_ref)

        for (width, _, start), w_ref in zip(pieces, w_refs):
            acc[...] += _dot_nt(dy_ref[:, start:start + width], w_ref[...])

        @pl.when(j == n_steps - 1)
        def _():
            _rms_bwd_epilogue(x_ref, g_ref, dres_ref, dx_ref, dxb_ref, dg_ref)

    w_specs = [pl.BlockSpec((D_MODEL, width), functools.partial(lambda i, j, f: (0, f(j)), f=col))
               for (width, col, _) in pieces]
    row = pl.BlockSpec((tm, D_MODEL), lambda i, j: (i, 0))
    vec = pl.BlockSpec((1, D_MODEL), lambda i, j: (0, 0))
    return _call(body, name=name, grid=(T // tm, n_steps),
                 in_specs=[dy_spec(tm)] + w_specs + [row, vec, row], out_specs=[row, row, vec],
                 out_shape=[jax.ShapeDtypeStruct((T, D_MODEL), F32), jax.ShapeDtypeStruct((T, D_MODEL), BF16),
                            jax.ShapeDtypeStruct((1, D_MODEL), F32)],
                 args=(dy, *([w] * n_p), x, g, dres), ride=ride)


def _matmul_res(a, w, res, name, tm=1024):
    T, K = a.shape

    def body(a_ref, w_ref, r_ref, o_ref):
        o_ref[...] = r_ref[...] + _dot(a_ref[...], w_ref[...])

    row = pl.BlockSpec((tm, D_MODEL), lambda i: (i, 0))
    return _call(body, name=name, grid=(T // tm,),
                 in_specs=[pl.BlockSpec((tm, K), lambda i: (i, 0)), pl.BlockSpec((K, D_MODEL), lambda i: (0, 0)), row],
                 out_specs=[row], out_shape=[jax.ShapeDtypeStruct((T, D_MODEL), F32)], args=(a, w, res))[0]


def _matmul_nt(dy, w, name, tm=1024):
    T, N = dy.shape
    K = w.shape[0]

    def body(dy_ref, w_ref, o_ref):
        o_ref[...] = _dot_nt(dy_ref[...], w_ref[...]).astype(BF16)

    return _call(body, name=name, grid=(T // tm,),
                 in_specs=[pl.BlockSpec((tm, N), lambda i: (i, 0)), pl.BlockSpec((K, N), lambda i: (0, 0))],
                 out_specs=[pl.BlockSpec((tm, K), lambda i: (i, 0))], out_shape=[jax.ShapeDtypeStruct((T, K), BF16)],
                 args=(dy, w))[0]


def _wgrad(a, a_spec, b, b_specs, m, n, bm, bn, n_k, name, square_a=False):
    b_specs = b_specs if isinstance(b_specs, (list, tuple)) else [b_specs]
    n_b = len(b_specs)

    def body(a_ref, *rest):
        b_refs = rest[:n_b]
        o_ref, acc = rest[n_b:]
        k = pl.program_id(2)

        @pl.when(k == 0)
        def _():
            acc[...] = jnp.zeros_like(acc)

        av = a_ref[...]
        if square_a:
            af = av.astype(F32)
            av = (af * af).astype(BF16)
        bv = b_refs[0][...] if n_b == 1 else jnp.concatenate([r[...] for r in b_refs], axis=1)
        acc[...] += _dot_tn(av, bv)

        @pl.when(k == n_k - 1)
        def _():
            o_ref[...] = acc[...].astype(BF16)

    return _call(body, name=name, grid=(m // bm, n // bn, n_k), in_specs=[a_spec, *b_specs],
                 out_specs=[pl.BlockSpec((bm, bn), lambda i, j, k: (i, j))], out_shape=[jax.ShapeDtypeStruct((m, n), BF16)],
                 scratch_shapes=[pltpu.VMEM((bm, bn), F32)], args=(a, *([b] * n_b)))[0]


def _mlp_fwd(h, g, w1, w2, name, tm=1024, tf=512, ride=None):
    T = h.shape[0]

    def body(h_ref, g_ref, w1_ref, w2_ref, ho_ref, n_ref, u_ref):
        @pl.when(pl.program_id(1) == 0)
        def _():
            n_ref[...] = _rms(h_ref[...], g_ref[...]).astype(BF16)
            ho_ref[...] = h_ref[...]

        u = jnp.maximum(_dot(n_ref[...], w1_ref[...]), 0.0)
        u_ref[...] = u.astype(BF16)
        ho_ref[...] += _dot((u * u).astype(BF16), w2_ref[...])

    row = pl.BlockSpec((tm, D_MODEL), lambda i, j: (i, 0))
    return _call(body, name=name, grid=(T // tm, D_FF // tf),
                 in_specs=[row, pl.BlockSpec((1, D_MODEL), lambda i, j: (0, 0)),
                           pl.BlockSpec((D_MODEL, tf), lambda i, j: (0, j)), pl.BlockSpec((tf, D_MODEL), lambda i, j: (j, 0))],
                 out_specs=[row, row, pl.BlockSpec((tm, tf), lambda i, j: (i, j))],
                 out_shape=[jax.ShapeDtypeStruct((T, D_MODEL), F32), jax.ShapeDtypeStruct((T, D_MODEL), BF16),
                            jax.ShapeDtypeStruct((T, D_FF), BF16)],
                 args=(h, g, w1, w2), ride=ride)


def _mlp_bwd(dh, u, w1, w2, h, g, name, tm=1024, tf=512, ride=None):
    T = h.shape[0]
    n_j = D_FF // tf

    def body(dh_ref, u_ref, w1_ref, w2_ref, h_ref, g_ref, dx_ref, dxb_ref, du_ref, dg_ref, dhb):
        acc = dx_ref
        i, j = pl.program_id(0), pl.program_id(1)

        @pl.when(j == 0)
        def _():
            dhb[...] = dh_ref[...].astype(BF16)
            acc[...] = jnp.zeros_like(acc)

        @pl.when((i == 0) & (j == 0))
        def _():
            dg_ref[...] = jnp.zeros_like(dg_ref)

        da = _dot_nt(dhb[...], w2_ref[...])
        du = (da * (2.0 * u_ref[...].astype(F32))).astype(BF16)
        du_ref[...] = du
        acc[...] += _dot_nt(du, w1_ref[...])

        @pl.when(j == n_j - 1)
        def _():
            _rms_bwd_epilogue(h_ref, g_ref, dh_ref, dx_ref, dxb_ref, dg_ref)

    row = pl.BlockSpec((tm, D_MODEL), lambda i, j: (i, 0))
    vec = pl.BlockSpec((1, D_MODEL), lambda i, j: (0, 0))
    hid = pl.BlockSpec((tm, tf), lambda i, j: (i, j))
    return _call(body, name=name, grid=(T // tm, n_j),
                 in_specs=[row, hid, pl.BlockSpec((D_MODEL, tf), lambda i, j: (0, j)),
                           pl.BlockSpec((tf, D_MODEL), lambda i, j: (j, 0)), row, vec],
                 out_specs=[row, row, hid, vec],
                 out_shape=[jax.ShapeDtypeStruct((T, D_MODEL), F32), jax.ShapeDtypeStruct((T, D_MODEL), BF16),
                            jax.ShapeDtypeStruct((T, D_FF), BF16), jax.ShapeDtypeStruct((1, D_MODEL), F32)],
                 scratch_shapes=[pltpu.VMEM((tm, D_MODEL), BF16)], args=(dh, u, w1, w2, h, g), ride=ride)


def _final_loss(h, g, target, name, tm=512):
    T = h.shape[0]

    def body(h_ref, g_ref, t_ref, dh_ref, dhb_ref, loss_ref, dg_ref):
        @pl.when(pl.program_id(0) == 0)
        def _():
            loss_ref[...] = jnp.zeros_like(loss_ref)
            dg_ref[...] = jnp.zeros_like(dg_ref)

        x = h_ref[...]
        gg = g_ref[...]
        r = lax.rsqrt(jnp.mean(x * x, axis=-1, keepdims=True) + EPS)
        xh = x * r
        e = xh * gg - t_ref[...]
        per_tok = jnp.mean(e * e, axis=-1, keepdims=True)
        loss_ref[...] += 0.5 * jnp.sum(per_tok, axis=0, keepdims=True)
        dy = e * (1.0 / D_MODEL)
        dg_ref[...] += jnp.sum(dy * xh, axis=0, keepdims=True)
        dxh = dy * gg
        dx = r * (dxh - xh * jnp.mean(dxh * xh, axis=-1, keepdims=True))
        dh_ref[...] = dx
        dhb_ref[...] = dx.astype(BF16)

    row = pl.BlockSpec((tm, D_MODEL), lambda i: (i, 0))
    vec = pl.BlockSpec((1, D_MODEL), lambda i: (0, 0))
    return _call(body, name=name, grid=(T // tm,), in_specs=[row, vec, row],
                 out_specs=[row, row, pl.BlockSpec((8, LANES), lambda i: (0, 0)), vec],
                 out_shape=[jax.ShapeDtypeStruct((T, D_MODEL), F32), jax.ShapeDtypeStruct((T, D_MODEL), BF16),
                            jax.ShapeDtypeStruct((8, LANES), F32), jax.ShapeDtypeStruct((1, D_MODEL), F32)],
                 args=(h, g, target))


def _ret_constants(S):
    log_gamma = jnp.log1p(-jnp.exp2(-5.0 - jnp.arange(RET_HEADS, dtype=F32)))
    idx = jnp.arange(RET_SC, dtype=F32)
    i, j = idx[:, None], idx[None, :]
    same_chunk = jnp.floor(i / CHUNK) == jnp.floor(j / CHUNK)
    mask = jnp.where((j <= i) | same_chunk, jnp.exp(log_gamma[:, None, None] * jnp.abs(i - j)[None]), 0.0)
    qdec = jnp.exp(log_gamma[:, None] * (idx + 1.0)[None, :])[:, :, None]
    kdec = jnp.exp(log_gamma[:, None] * (RET_SC - 1 - idx)[None, :])[:, :, None]
    cdec = jnp.exp(log_gamma * RET_SC)[:, None, None]
    half = RET_DK // 2
    inv = jnp.exp(-jnp.log(ROPE_BASE) * jnp.arange(half, dtype=F32) / half)
    ang = jnp.arange(S, dtype=F32)[:, None] * inv[None, :]
    return jnp.cos(ang), jnp.sin(ang), mask.astype(F32), qdec, kdec, cdec


def _rope(t, cs, sn):
    t1, t2 = t[:, :RET_DK // 2], t[:, RET_DK // 2:]
    return jnp.concatenate([t1 * cs - t2 * sn, t1 * sn + t2 * cs], axis=-1)


def _rope_bwd(d, cs, sn):
    d1, d2 = d[:, :RET_DK // 2], d[:, RET_DK // 2:]
    return jnp.concatenate([d1 * cs + d2 * sn, d2 * cs - d1 * sn], axis=-1)


def _ret_specs(B, S, reverse):
    n_sc = S // RET_SC

    def cc(c):
        return n_sc - 1 - c if reverse else c

    return dict(
        proj=pl.BlockSpec((RET_SC, RET_HEAD_COLS), lambda h, b, c: (b * n_sc + cc(c), h)),
        trig=pl.BlockSpec((RET_SC, RET_DK // 2), lambda h, b, c: (cc(c), 0)),
        mask=pl.BlockSpec((None, RET_SC, RET_SC), lambda h, b, c: (h, 0, 0)),
        dec=pl.BlockSpec((None, RET_SC, 1), lambda h, b, c: (h, 0, 0)),
        cdec=pl.BlockSpec((None, 1, 1), lambda h, b, c: (h, 0, 0)),
        gn=pl.BlockSpec((1, RET_DV), lambda h, b, c: (0, h)),
        val=pl.BlockSpec((RET_SC, RET_DV), lambda h, b, c: (b * n_sc + cc(c), h)),
        state=pl.BlockSpec((None, None, None, RET_DK, RET_DV), lambda h, b, c: (b, h, cc(c), 0, 0)),
    )


def _ret_qkvg(p_ref, cs, sn):
    q = _rope(p_ref[:, 0:RET_DK].astype(F32), cs, sn)
    k = _rope(p_ref[:, RET_DK:2 * RET_DK].astype(F32), cs, sn) * (RET_DK ** -0.5)
    v = p_ref[:, 2 * RET_DK:2 * RET_DK + RET_DV]
    gate = p_ref[:, 2 * RET_DK + RET_DV:RET_HEAD_COLS].astype(F32)
    return q, k, v, gate


def _group_norm(o):
    mu = jnp.mean(o, axis=-1, keepdims=True)
    oc = o - mu
    rstd = lax.rsqrt(jnp.mean(oc * oc, axis=-1, keepdims=True) + EPS)
    return oc * rstd, rstd


def _ret_fwd(proj, consts, gn, B, S, name, ride=None):
    T = B * S
    n_sc = S // RET_SC
    sp = _ret_specs(B, S, False)

    def body(p_ref, cos_ref, sin_ref, m_ref, qd_ref, kd_ref, cd_ref, gn_ref, y_ref, o_ref, st_ref, state):
        @pl.when(pl.program_id(2) == 0)
        def _():
            state[...] = jnp.zeros_like(state)

        q, k, v, gate = _ret_qkvg(p_ref, cos_ref[...], sin_ref[...])
        qb, kb = q.astype(BF16), k.astype(BF16)
        p = (_dot_nt(qb, kb) * m_ref[...]).astype(BF16)
        sb = state[...].astype(BF16)
        st_ref[...] = sb
        o = _dot(p, v) + _dot(qb, sb) * qd_ref[...]
        kt = (k * kd_ref[...]).astype(BF16)
        state[...] = state[...] * cd_ref[...] + _dot_tn(kt, v)
        o_ref[...] = o
        ohat, _ = _group_norm(o)
        y_ref[...] = (gate * jax.nn.sigmoid(gate) * (ohat * gn_ref[...])).astype(BF16)

    return _call(body, name=name, grid=(RET_HEADS, B, n_sc),
                 in_specs=[sp["proj"], sp["trig"], sp["trig"], sp["mask"], sp["dec"], sp["dec"], sp["cdec"], sp["gn"]],
                 out_specs=[sp["val"], sp["val"], sp["state"]],
                 out_shape=[jax.ShapeDtypeStruct((T, RET_VW), BF16), jax.ShapeDtypeStruct((T, RET_VW), F32),
                            jax.ShapeDtypeStruct((B, RET_HEADS, n_sc, RET_DK, RET_DV), BF16)],
                 scratch_shapes=[pltpu.VMEM((RET_DK, RET_DV), F32)], args=(proj, *consts, gn), ride=ride)


def _ret_bwd(proj, consts, gn, o, states, dy, B, S, name, ride=None):
    T = B * S
    n_sc = S // RET_SC
    sp = _ret_specs(B, S, True)

    def body(p_ref, cos_ref, sin_ref, m_ref, qd_ref, kd_ref, cd_ref, gn_ref, o_ref, st_ref, dy_ref,
             dp_ref, dgn_ref, dstate):
        b, c = pl.program_id(1), pl.program_id(2)

        @pl.when(c == 0)
        def _():
            dstate[...] = jnp.zeros_like(dstate)

        @pl.when((b == 0) & (c == 0))
        def _():
            dgn_ref[...] = jnp.zeros_like(dgn_ref)

        cs, sn = cos_ref[...], sin_ref[...]
        q, k, v, gate = _ret_qkvg(p_ref, cs, sn)
        qb, kb = q.astype(BF16), k.astype(BF16)
        kt = (k * kd_ref[...]).astype(BF16)
        sb = st_ref[...]
        m = m_ref[...]
        ohat, rstd = _group_norm(o_ref[...])
        gnv = gn_ref[...]
        dyv = dy_ref[...].astype(F32)
        sg = jax.nn.sigmoid(gate)
        don = dyv * (gate * sg)
        dgate = dyv * (ohat * gnv) * (sg * (1.0 + gate * (1.0 - sg)))
        dgn_ref[...] += jnp.sum(don * ohat, axis=0, keepdims=True)
        dohat = don * gnv
        do = rstd * (dohat - jnp.mean(dohat, axis=-1, keepdims=True)
                     - ohat * jnp.mean(dohat * ohat, axis=-1, keepdims=True))
        dob = do.astype(BF16)
        doq = (do * qd_ref[...]).astype(BF16)
        dsb = dstate[...].astype(BF16)
        p = (_dot_nt(qb, kb) * m).astype(BF16)
        dv = _dot_tn(p, dob) + _dot(kt, dsb)
        dkt = _dot_nt(v, dsb)
        dsc = (_dot_nt(dob, v) * m).astype(BF16)
        dq = _dot(dsc, kb) + _dot_nt(doq, sb)
        dk = (_dot_tn(dsc, qb) + dkt * kd_ref[...]) * (RET_DK ** -0.5)
        dstate[...] = dstate[...] * cd_ref[...] + _dot_tn(qb, doq)
        dp_ref[:, 0:RET_DK] = _rope_bwd(dq, cs, sn).astype(BF16)
        dp_ref[:, RET_DK:2 * RET_DK] = _rope_bwd(dk, cs, sn).astype(BF16)
        dp_ref[:, 2 * RET_DK:2 * RET_DK + RET_DV] = dv.astype(BF16)
        dp_ref[:, 2 * RET_DK + RET_DV:RET_HEAD_COLS] = dgate.astype(BF16)

    return _call(body, name=name, grid=(RET_HEADS, B, n_sc),
                 in_specs=[sp["proj"], sp["trig"], sp["trig"], sp["mask"], sp["dec"], sp["dec"], sp["cdec"], sp["gn"],
                           sp["val"], sp["state"], sp["val"]],
                 out_specs=[sp["proj"], sp["gn"]],
                 out_shape=[jax.ShapeDtypeStruct((T, RET_HEADS * RET_HEAD_COLS), BF16), jax.ShapeDtypeStruct((1, RET_VW), F32)],
                 scratch_shapes=[pltpu.VMEM((RET_DK, RET_DV), F32)], args=(proj, *consts, gn, o, states, dy), ride=ride)


def _rel_index():
    t = np.arange(ATT_EXT)
    return np.clip(t - (ATT_KW - 1), -MAX_REL, CHUNK - 1) + MAX_REL


def _bias_matrices(rel):
    idx = _rel_index()
    n_lo = int(np.argmax(idx > 0))
    n_hi = ATT_EXT - (n_lo + REL_TABLE - 1)
    ext = jnp.concatenate([jnp.broadcast_to(rel[:, :1], (ATT_HEADS, n_lo - 1)), rel,
                           jnp.broadcast_to(rel[:, -1:], (ATT_HEADS, n_hi))], axis=1)
    e_pad = jnp.pad(ext, ((0, 0), (0, 1)))
    tiled = jnp.broadcast_to(e_pad[:, None, :], (ATT_HEADS, ATT_CQ, ATT_EXT + 1))
    skew = tiled.reshape(ATT_HEADS, -1)[:, :ATT_CQ * ATT_EXT].reshape(ATT_HEADS, ATT_CQ, ATT_EXT)
    toep = skew[:, :, ATT_CQ - 1:ATT_CQ - 1 + ATT_KW]
    i = np.arange(ATT_CQ)[:, None]
    j = np.arange(ATT_KW)[None, :]
    lo = CHUNK * (i // CHUNK)
    band = np.where((j >= lo) & (j < lo + BAND_PAST + CHUNK), 0.0, NEG)
    variants = np.stack([band + np.where(j + v * ATT_CQ >= BAND_PAST, 0.0, NEG) for v in range(ATT_VARIANTS)])
    return toep.reshape(ATT_PAIRS, 1, 2, ATT_CQ, ATT_KW) + variants.astype(np.float32)[None, :, None]


ATT_STRIP = 32
ATT_SCALE = ATT_DH ** -0.5


def _strips(fn):
    def strip(r, carry):
        fn(pl.ds(pl.multiple_of(r * ATT_STRIP, ATT_STRIP), ATT_STRIP))
        return carry

    lax.fori_loop(0, ATT_CQ // ATT_STRIP, strip, 0, unroll=True)


def _attn_prepare(kpad, vpad, qm, qkv_ref):
    kpad[0:BAND_PAST, :] = jnp.zeros((BAND_PAST, LANES), BF16)
    vpad[0:BAND_PAST, :] = jnp.zeros((BAND_PAST, LANES), BF16)
    kpad[BAND_PAST:, :] = qkv_ref[1]
    vpad[BAND_PAST:, :] = qkv_ref[2]
    lane = lax.broadcasted_iota(jnp.int32, (1, LANES), 1)
    q = qkv_ref[0] * ATT_SCALE
    for hh in range(2):
        qm[hh] = jnp.where((lane >= ATT_DH * hh) & (lane < ATT_DH * (hh + 1)), q, jnp.zeros_like(q))


def _attn_scores(s_ref, qm, kpad, b_ref, t, n_qb):
    t = jnp.minimum(t, n_qb - 1)
    qs = pl.multiple_of(t * ATT_CQ, ATT_CQ)
    variant = jnp.minimum(t, ATT_VARIANTS - 1)
    kw = kpad[pl.ds(qs, ATT_KW), :]
    for hh in range(2):
        s_ref[hh] = _dot_nt(qm[hh, pl.ds(qs, ATT_CQ), :], kw) + b_ref[variant, hh]


def _attn_softmax(s_ref, e_ref, linv_ref, m_ref=None):
    def strip(rows):
        s = s_ref[rows, :]
        m = jnp.max(s, axis=-1, keepdims=True)
        e = jnp.exp(s - m)
        e_ref[rows, :] = e.astype(BF16)
        linv_ref[rows, :] = jnp.broadcast_to(1.0 / jnp.sum(e, axis=-1, keepdims=True), (ATT_STRIP, LANES))
        if m_ref is not None:
            m_ref[rows, :] = jnp.broadcast_to(m, (ATT_STRIP, LANES))

    _strips(strip)


def _attn_fwd(qkv3, bias, B, S, name, ride=None):
    T = B * S
    n_qb = S // ATT_CQ

    def body(qkv_ref, b_ref, o_ref, kpad, vpad, qm, s_scr, e_scr, linv_scr):
        _attn_prepare(kpad, vpad, qm, qkv_ref)
        e_scr[...] = jnp.zeros_like(e_scr)
        linv_scr[...] = jnp.zeros_like(linv_scr)
        lane = lax.broadcasted_iota(jnp.int32, (1, LANES), 1)

        def softmax(slot):
            for hh in range(2):
                _attn_softmax(s_scr.at[slot, hh], e_scr.at[slot, hh], linv_scr.at[slot, hh])

        def output(t, slot):
            qs = pl.multiple_of(jnp.maximum(t, 0) * ATT_CQ, ATT_CQ)
            vw = vpad[pl.ds(qs, ATT_KW), :]
            outs = [_dot(e_scr[slot, hh], vw) * linv_scr[slot, hh] for hh in range(2)]
            o_ref[pl.ds(qs, ATT_CQ), :] = jnp.where(lane < ATT_DH, outs[0], outs[1]).astype(BF16)

        def pair(u, carry):
            t = 2 * u
            _attn_scores(s_scr.at[1], qm, kpad, b_ref, t + 1, n_qb)
            softmax(0)
            output(t - 1, 1)
            _attn_scores(s_scr.at[0], qm, kpad, b_ref, t + 2, n_qb)
            softmax(1)
            output(t, 0)
            return carry

        _attn_scores(s_scr.at[0], qm, kpad, b_ref, 0, n_qb)
        lax.fori_loop(0, n_qb // 2, pair, 0)
        output(n_qb - 1, 1)

    return _call(body, name=name, grid=(ATT_PAIRS, B),
                 in_specs=[pl.BlockSpec((3, S, LANES), lambda hp, b: (0, b, hp)),
                           pl.BlockSpec((None, ATT_VARIANTS, 2, ATT_CQ, ATT_KW), lambda hp, b: (hp, 0, 0, 0, 0))],
                 out_specs=[pl.BlockSpec((S, LANES), lambda hp, b: (b, hp))],
                 out_shape=[jax.ShapeDtypeStruct((T, D_MODEL), BF16)],
                 scratch_shapes=[pltpu.VMEM((S + BAND_PAST, LANES), BF16), pltpu.VMEM((S + BAND_PAST, LANES), BF16),
                                 pltpu.VMEM((2, S, LANES), BF16), pltpu.VMEM((2, 2, ATT_CQ, ATT_KW), F32),
                                 pltpu.VMEM((2, 2, ATT_CQ, ATT_KW), BF16), pltpu.VMEM((2, 2, ATT_CQ, LANES), F32)],
                 args=(qkv3, bias), ride=ride)


def _attn_bwd(qkv3, bias, do, B, S, name, ride=None):
    T = B * S
    n_qb = S // ATT_CQ

    def body(qkv_ref, b_ref, do_ref, dqkv_ref, db_ref, kpad, vpad, qm, dkacc, dvacc):
        @pl.when(pl.program_id(1) == 0)
        def _():
            db_ref[...] = jnp.zeros_like(db_ref)

        _attn_prepare(kpad, vpad, qm, qkv_ref)
        dkacc[...] = jnp.zeros_like(dkacc)
        dvacc[...] = jnp.zeros_like(dvacc)
        lane = lax.broadcasted_iota(jnp.int32, (1, LANES), 1)

        def step(qb, carry):
            qs = pl.multiple_of(qb * ATT_CQ, ATT_CQ)
            variant = jnp.minimum(qb, ATT_VARIANTS - 1)
            dov = do_ref[pl.ds(qs, ATT_CQ), :].astype(F32)
            kw = kpad[pl.ds(qs, ATT_KW), :]
            vw = vpad[pl.ds(qs, ATT_KW), :]
            heads = (0, 1)
            qmh = [qm[hh, pl.ds(qs, ATT_CQ), :] for hh in heads]
            s = [_dot_nt(qmh[hh], kw) + b_ref[variant, hh] for hh in heads]
            e, linv, dom, dp = [None, None], [None, None], [None, None], [None, None]
            for hh in heads:
                e[hh] = jnp.exp(s[hh] - jnp.max(s[hh], axis=-1, keepdims=True))
                linv[hh] = 1.0 / jnp.sum(e[hh], axis=-1, keepdims=True)
                sel = (lane >= ATT_DH * hh) & (lane < ATT_DH * (hh + 1))
                dom[hh] = jnp.where(sel, dov * linv[hh], 0.0).astype(BF16)
                dp[hh] = _dot_nt(dom[hh], vw)
            dqs, dk, dv = [], None, None
            for hh in heads:
                ds = e[hh] * (dp[hh] - jnp.sum(dp[hh] * e[hh], axis=-1, keepdims=True) * linv[hh])
                db_ref[hh] += ds
                dsb = ds.astype(BF16)
                dqs.append(_dot(dsb, kw) * ATT_SCALE)
                dk_h = _dot_tn(dsb, qmh[hh])
                dv_h = _dot_tn(e[hh].astype(BF16), dom[hh])
                dk = dk_h if dk is None else dk + dk_h
                dv = dv_h if dv is None else dv + dv_h
            dqkv_ref[0, pl.ds(qs, ATT_CQ), :] = jnp.where(lane < ATT_DH, dqs[0], dqs[1]).astype(BF16)
            dkacc[pl.ds(qs, ATT_KW), :] += dk
            dvacc[pl.ds(qs, ATT_KW), :] += dv
            return carry

        lax.fori_loop(0, n_qb, step, 0)
        dqkv_ref[1] = dkacc[BAND_PAST:, :].astype(BF16)
        dqkv_ref[2] = dvacc[BAND_PAST:, :].astype(BF16)

    bias_spec = pl.BlockSpec((None, ATT_VARIANTS, 2, ATT_CQ, ATT_KW), lambda hp, b: (hp, 0, 0, 0, 0))
    dbias_spec = pl.BlockSpec((None, 2, ATT_CQ, ATT_KW), lambda hp, b: (hp, 0, 0, 0))
    qkv_spec = pl.BlockSpec((3, S, LANES), lambda hp, b: (0, b, hp))
    return _call(body, name=name, grid=(ATT_PAIRS, B),
                 in_specs=[qkv_spec, bias_spec, pl.BlockSpec((S, LANES), lambda hp, b: (b, hp))],
                 out_specs=[qkv_spec, dbias_spec],
                 out_shape=[jax.ShapeDtypeStruct((3, T, D_MODEL), BF16),
                            jax.ShapeDtypeStruct((ATT_PAIRS, 2, ATT_CQ, ATT_KW), F32)],
                 scratch_shapes=[pltpu.VMEM((S + BAND_PAST, LANES), BF16), pltpu.VMEM((S + BAND_PAST, LANES), BF16),
                                 pltpu.VMEM((2, S, LANES), BF16),
                                 pltpu.VMEM((S + BAND_PAST, LANES), F32), pltpu.VMEM((S + BAND_PAST, LANES), F32)],
                 args=(qkv3, bias, do), ride=ride)


def _rel_bias_grad(dbias, name):
    d = dbias.reshape(ATT_HEADS, ATT_CQ, ATT_KW)
    d = jnp.pad(d, ((0, 0), (0, 0), (ATT_CQ - 1, 0))).reshape(ATT_HEADS, ATT_CQ * ATT_EXT)
    skew = jnp.pad(d, ((0, 0), (0, ATT_CQ))).reshape(ATT_HEADS, ATT_CQ, ATT_EXT + 1)
    fold = np.zeros((ATT_EXT + 1, REL_PAD), np.float32)
    fold[np.arange(ATT_EXT), _rel_index()] = 1.0
    fold = jnp.asarray(fold, BF16)

    def body(s_ref, f_ref, o_ref):
        x = jnp.sum(s_ref[...], axis=0, keepdims=True)
        x = jnp.broadcast_to(x, (8, ATT_EXT + 1))
        acc = jnp.zeros((8, REL_PAD), F32)
        for _ in range(3):
            part = x.astype(BF16)
            acc = acc + _dot(part, f_ref[...])
            x = x - part.astype(F32)
        o_ref[...] = acc[0:1, :]

    out = _call(body, name=name, grid=(ATT_HEADS,),
                in_specs=[pl.BlockSpec((None, ATT_CQ, ATT_EXT + 1), lambda h: (h, 0, 0)),
                          pl.BlockSpec((ATT_EXT + 1, REL_PAD), lambda h: (0, 0))],
                out_specs=[pl.BlockSpec((None, 1, REL_PAD), lambda h: (h, 0, 0))],
                out_shape=[jax.ShapeDtypeStruct((ATT_HEADS, 1, REL_PAD), F32)], args=(skew, fold))[0]
    return out.reshape(ATT_HEADS, REL_PAD)


def _adamw(w, g, m, v):
    m = ADAM_B1 * m + (1.0 - ADAM_B1) * g
    v = ADAM_B2 * v + (1.0 - ADAM_B2) * (g * g)
    m_hat = m / (1.0 - ADAM_B1 ** ADAM_STEP)
    v_hat = v / (1.0 - ADAM_B2 ** ADAM_STEP)
    delta = -ADAM_LR * (m_hat / (jnp.sqrt(v_hat) + ADAM_EPS) + ADAM_WD * w)
    return delta, m, v


def _sum_devices(ref):
    g = ref[0].astype(F32)
    for d in range(1, N_DEV):
        g = g + ref[d].astype(F32)
    return g


def _adamw_reduce(lands, w, m, v, name, tr=256):
    L, R, C = w.shape
    tr = min(tr, R)
    n_i = R // tr

    def body(*refs):
        l_refs = refs[:L]
        w_ref, m_ref, v_ref, g_out, d_out, m_out, v_out = refs[L:]
        layer = pl.program_id(0)
        for l in range(L):
            @pl.when(layer == l)
            def _(l=l):
                g = _sum_devices(l_refs[l])
                g_out[...] = g
                d_out[...], m_out[...], v_out[...] = _adamw(w_ref[...], g, m_ref[...], v_ref[...])

    def land_spec(l):
        return pl.BlockSpec((N_DEV, tr, C), lambda ly, i: (0, jnp.where(ly == l, i, jnp.where(ly < l, 0, n_i - 1)), 0))

    blk = pl.BlockSpec((None, tr, C), lambda ly, i: (ly, i, 0))
    return _call(body, name=name, grid=(L, n_i), in_specs=[land_spec(l) for l in range(L)] + [blk, blk, blk],
                 out_specs=[blk] * 4, out_shape=[jax.ShapeDtypeStruct((L, R, C), F32)] * 4, args=(*lands, w, m, v))


def _small_update(land_small, land_rel, w, m, v, name):
    def body(ls_ref, lr_ref, w_ref, m_ref, v_ref, g_out, d_out, m_out, v_out, rel_out):
        g = _sum_devices(ls_ref)
        g_out[...] = g
        d_out[...], m_out[...], v_out[...] = _adamw(w_ref[...], g, m_ref[...], v_ref[...])
        rel_out[...] = _sum_devices(lr_ref)

    return pl.pallas_call(
        body, name=name,
        out_shape=[jax.ShapeDtypeStruct(w.shape, F32)] * 4 + [jax.ShapeDtypeStruct(land_rel.shape[1:], F32)],
    )(land_small, land_rel, w, m, v)


def _adamw_plain(w, g, m, v, name):
    def body(w_ref, g_ref, m_ref, v_ref, d_out, m_out, v_out):
        d_out[...], m_out[...], v_out[...] = _adamw(w_ref[...], g_ref[...], m_ref[...], v_ref[...])

    return pl.pallas_call(body, name=name, out_shape=[jax.ShapeDtypeStruct(w.shape, F32)] * 3)(w, g, m, v)


def _ret_pieces():
    return [(RET_DK, lambda h: h, 0),
            (RET_DK, lambda h: RET_HEADS + h, RET_DK),
            (RET_DV, lambda h: RET_HEADS + h, 2 * RET_DK),
            (RET_DV, lambda h: 2 * RET_HEADS + h, 2 * RET_DK + RET_DV)]


def _ret_piece_of_column_block(p):
    per_head = RET_HEAD_COLS // RET_DK
    qk = jnp.where(p < RET_HEADS, per_head * p, per_head * (p - RET_HEADS) + 1)
    pv = p - 2 * RET_HEADS
    vv = per_head * (pv // 2) + 2 + pv % 2
    pg = p - 4 * RET_HEADS
    gg = per_head * (pg // 2) + 4 + pg % 2
    return jnp.where(p < 2 * RET_HEADS, qk, jnp.where(p < 4 * RET_HEADS, vv, gg))


def _gather(shard, full_shape, view):
    return _Exchange([shard], [(0, _whole, 0, view)], [jax.ShapeDtypeStruct(full_shape, shard.dtype)])


def _step(x, target, shards, mix_g, gn_g, rel_shard, mlp_g, final_g):
    s_rin, s_rout, s_ain, s_aout, s_w1, s_w2 = shards
    B, S, _ = x.shape
    T = B * S
    tk = min(T, 2048)
    n_k = T // tk
    h0 = x.reshape(T, D_MODEL)
    tgt = target.reshape(T, D_MODEL)
    consts = _ret_constants(S)
    att_pieces = [(D_MODEL, lambda j: j, 0)]
    w1_cols = w2_rows = D_FF // N_DEV

    def qkv_spec(tm):
        return pl.BlockSpec((None, tm, D_MODEL), lambda i, j: (j, i, 0))

    def head_spec(tm):
        return pl.BlockSpec((tm, RET_HEAD_COLS), lambda i, j: (i, j))

    rel_cols = REL_TABLE // N_DEV
    w_rin, rel_all = _all_gather(
        [s_rin, rel_shard],
        [jax.ShapeDtypeStruct((D_MODEL, RET_IN), BF16), jax.ShapeDtypeStruct((N_DEV, ATT_HEADS, rel_cols), F32)],
        [_cols(RET_IN // N_DEV), _slot], "gather_first")
    bias = _bias_matrices(rel_all.transpose(1, 0, 2).reshape(ATT_HEADS, REL_TABLE))

    ride = _Exchange([s_rout, s_w1[0]], [(0, _whole, 0, _rows(RET_VW // N_DEV)), (1, _whole, 1, _cols(w1_cols))],
                     [jax.ShapeDtypeStruct((RET_VW, D_MODEL), BF16), jax.ShapeDtypeStruct((D_MODEL, D_FF), BF16)])
    (proj, n0), (w_rout, w1_0) = _norm_proj(h0, mix_g[0:1], w_rin, _ret_pieces(), RET_HEADS,
                                            jax.ShapeDtypeStruct((T, RET_HEADS * RET_HEAD_COLS), BF16), head_spec,
                                            "ret_proj", ride=ride)
    ride = _Exchange([s_w2[0], s_ain], [(0, _whole, 0, _rows(w2_rows)), (1, _whole, 1, _cols(3 * D_MODEL // N_DEV))],
                     [jax.ShapeDtypeStruct((D_FF, D_MODEL), BF16), jax.ShapeDtypeStruct((D_MODEL, 3 * D_MODEL), BF16)])
    (y, o, states), (w2_0, w_ain) = _ret_fwd(proj, consts, gn_g, B, S, "ret_fwd", ride=ride)
    h1 = _matmul_res(y, w_rout, h0, "ret_out")
    ride = _Exchange([s_aout, s_w1[1]], [(0, _whole, 0, _rows(D_MODEL // N_DEV)), (1, _whole, 1, _cols(w1_cols))],
                     [jax.ShapeDtypeStruct((D_MODEL, D_MODEL), BF16), jax.ShapeDtypeStruct((D_MODEL, D_FF), BF16)])
    (h2, n1, u1), (w_aout, w1_1) = _mlp_fwd(h1, mlp_g[0:1], w1_0, w2_0, "mlp0_fwd", ride=ride)
    qkv3, n2 = _norm_proj(h2, mix_g[1:2], w_ain, att_pieces, 3, jax.ShapeDtypeStruct((3, T, D_MODEL), BF16), qkv_spec,
                          "att_proj")
    (o2,), (w2_1,) = _attn_fwd(qkv3, bias, B, S, "att_fwd", ride=_gather(s_w2[1], (D_FF, D_MODEL), _rows(w2_rows)))
    h3 = _matmul_res(o2, w_aout, h2, "att_out")
    h4, n3, u3 = _mlp_fwd(h3, mlp_g[1:2], w1_1, w2_1, "mlp1_fwd")
    dh4, dh4b, loss, dg_final = _final_loss(h4, final_g, tgt, "final_loss")

    def tok(width):
        return dict(a=pl.BlockSpec((tk, width), lambda i, j, k: (k, i)), b=pl.BlockSpec((tk, width), lambda i, j, k: (k, j)))

    def mlp_grads(dh, dhb, u, n, w1, w2, h, g, tag, ride=None):
        res = _mlp_bwd(dh, u, w1, w2, h, g, tag + "_bwd", ride=ride)
        (dhi, dhib, du, dg), lands = res if ride is not None else (res, None)
        gw2 = _wgrad(u, tok(1024)["a"], dhb, tok(1024)["b"], D_FF, D_MODEL, 1024, 1024, n_k, tag + "_dw2", square_a=True)
        gw1 = _wgrad(n, tok(1024)["a"], du, tok(1024)["b"], D_MODEL, D_FF, 1024, 1024, n_k, tag + "_dw1")
        return dhi, dhib, gw1, gw2, dg, lands

    dh3, dh3b, gw1_1, gw2_1, dg_mlp1, _ = mlp_grads(dh4, dh4b, u3, n3, w1_1, w2_1, h3, mlp_g[1:2], "mlp1")
    do2 = _matmul_nt(dh3b, w_aout, "att_out_bwd")
    g_aout = _wgrad(o2, tok(1024)["a"], dh3b, tok(1024)["b"], D_MODEL, D_MODEL, 1024, 1024, n_k, "att_out_dw")
    ride = _Exchange(
        [gw1_1, gw2_1, g_aout],
        [(0, _cols(w1_cols), 0, _slot), (1, _rows(w2_rows), 1, _slot), (2, _rows(D_MODEL // N_DEV), 2, _slot)],
        [jax.ShapeDtypeStruct((N_DEV, D_MODEL, w1_cols), BF16), jax.ShapeDtypeStruct((N_DEV, w2_rows, D_MODEL), BF16),
         jax.ShapeDtypeStruct((N_DEV, D_MODEL // N_DEV, D_MODEL), BF16)])
    (dqkv3, dbias), (l_w1_1, l_w2_1, l_aout) = _attn_bwd(qkv3, bias, do2, B, S, "att_bwd", ride=ride)
    g_rel = _rel_bias_grad(dbias, "rel_bias_grad")
    dh2, dh2b, dg_mix1 = _proj_bwd(dqkv3, qkv_spec, w_ain, att_pieces, 3, h2, mix_g[1:2], dh3, "att_proj_bwd")
    g_ain = _wgrad(n2, tok(1024)["a"], dqkv3, pl.BlockSpec((None, tk, D_MODEL), lambda i, j, k: (j, k, 0)),
                   D_MODEL, 3 * D_MODEL, 1024, 1024, n_k, "att_proj_dw")
    ride = _Exchange([g_ain], [(0, _cols(3 * D_MODEL // N_DEV), 0, _slot)],
                     [jax.ShapeDtypeStruct((N_DEV, D_MODEL, 3 * D_MODEL // N_DEV), BF16)])
    dh1, dh1b, gw1_0, gw2_0, dg_mlp0, (l_ain,) = mlp_grads(dh2, dh2b, u1, n1, w1_0, w2_0, h1, mlp_g[0:1], "mlp0", ride=ride)
    dy = _matmul_nt(dh1b, w_rout, "ret_out_bwd")
    g_rout = _wgrad(y, tok(1024)["a"], dh1b, tok(1024)["b"], RET_VW, D_MODEL, 1024, 1024, n_k, "ret_out_dw")
    ride = _Exchange(
        [gw1_0, gw2_0, g_rout],
        [(0, _cols(w1_cols), 0, _slot), (1, _rows(w2_rows), 1, _slot), (2, _rows(RET_VW // N_DEV), 2, _slot)],
        [jax.ShapeDtypeStruct((N_DEV, D_MODEL, w1_cols), BF16), jax.ShapeDtypeStruct((N_DEV, w2_rows, D_MODEL), BF16),
         jax.ShapeDtypeStruct((N_DEV, RET_VW // N_DEV, D_MODEL), BF16)])
    (dproj, dgn), (l_w1_0, l_w2_0, l_rout) = _ret_bwd(proj, consts, gn_g, o, states, dy, B, S, "ret_bwd", ride=ride)
    per_shard = RET_IN // N_DEV // RET_DK
    g_rin = _wgrad(n0, tok(1024)["a"], dproj,
                   [pl.BlockSpec((tk, RET_DK), functools.partial(
                       lambda i, j, k, r: (k, _ret_piece_of_column_block(per_shard * j + r)), r=r)) for r in range(per_shard)],
                   D_MODEL, RET_IN, 1024, per_shard * RET_DK, n_k, "ret_proj_dw")
    ride = _Exchange([g_rin], [(0, _cols(RET_IN // N_DEV), 0, _slot)],
                     [jax.ShapeDtypeStruct((N_DEV, D_MODEL, RET_IN // N_DEV), BF16)])
    (dx, _, dg_mix0), (l_rin,) = _proj_bwd(dproj, head_spec, w_rin, _ret_pieces(), RET_HEADS, h0, mix_g[0:1], dh1,
                                           "ret_proj_bwd", ride=ride)

    small = jnp.concatenate([dg_mix0, dg_mix1, dg_mlp0, dg_mlp1, dg_final, dgn.reshape(2, D_MODEL),
                             jnp.zeros((1, D_MODEL), F32)], axis=0)
    l_small, l_rel = _Exchange(
        [small, g_rel], [(0, _whole, 0, _slot), (1, _whole, 1, _slot)],
        [jax.ShapeDtypeStruct((N_DEV, 8, D_MODEL), F32), jax.ShapeDtypeStruct((N_DEV, ATT_HEADS, REL_PAD), F32)],
    ).run("scatter_small")
    lands = dict(rin=l_rin, rout=l_rout, ain=l_ain, aout=l_aout, w1=(l_w1_0, l_w1_1), w2=(l_w2_0, l_w2_1),
                 small=l_small, rel=l_rel)
    return loss[0, 0], dx.reshape(B, S, D_MODEL), lands


def kernel(x, mix_norm_g, ret_w_in, ret_gn_g, ret_w_out, att_w_in, att_rel_bias, att_w_out, mlp_norm_g, mlp_w1, mlp_w2, final_norm_g, loss_target, m_mix_norm_g, m_ret_w_in, m_ret_gn_g, m_ret_w_out, m_att_w_in, m_att_rel_bias, m_att_w_out, m_mlp_norm_g, m_mlp_w1, m_mlp_w2, m_final_norm_g, v_mix_norm_g, v_ret_w_in, v_ret_gn_g, v_ret_w_out, v_att_w_in, v_att_rel_bias, v_att_w_out, v_mlp_norm_g, v_mlp_w1, v_mlp_w2, v_final_norm_g):
    me = _lin(_place())
    rel_cols = REL_TABLE // N_DEV
    shards = (ret_w_in[0].astype(BF16), ret_w_out[0].astype(BF16), att_w_in[0].astype(BF16), att_w_out[0].astype(BF16),
              (mlp_w1[0].astype(BF16), mlp_w1[1].astype(BF16)), (mlp_w2[0].astype(BF16), mlp_w2[1].astype(BF16)))
    loss_part, grad_x, lands = _step(x, loss_target, shards, mix_norm_g, ret_gn_g, att_rel_bias[0], mlp_norm_g,
                                     final_norm_g.reshape(1, D_MODEL))
    loss = lax.psum(loss_part, ("x", "y", "c"))

    u_rin = _adamw_reduce([lands["rin"]], ret_w_in, m_ret_w_in, v_ret_w_in, "update_ret_w_in")
    u_rout = _adamw_reduce([lands["rout"]], ret_w_out, m_ret_w_out, v_ret_w_out, "update_ret_w_out")
    u_ain = _adamw_reduce([lands["ain"]], att_w_in, m_att_w_in, v_att_w_in, "update_att_w_in")
    u_aout = _adamw_reduce([lands["aout"]], att_w_out, m_att_w_out, v_att_w_out, "update_att_w_out")
    u_w1 = _adamw_reduce(lands["w1"], mlp_w1, m_mlp_w1, v_mlp_w1, "update_mlp_w1")
    u_w2 = _adamw_reduce(lands["w2"], mlp_w2, m_mlp_w2, v_mlp_w2, "update_mlp_w2")

    def pack(mix, mlp, fin, gn):
        return jnp.concatenate([mix, mlp, fin.reshape(1, D_MODEL), gn.reshape(2, D_MODEL), jnp.zeros((1, D_MODEL), F32)], axis=0)

    small_w = pack(mix_norm_g, mlp_norm_g, final_norm_g, ret_gn_g)
    small_m = pack(m_mix_norm_g, m_mlp_norm_g, m_final_norm_g, m_ret_gn_g)
    small_v = pack(v_mix_norm_g, v_mlp_norm_g, v_final_norm_g, v_ret_gn_g)
    sg, sd, sm, sv, rel_sum = _small_update(lands["small"], lands["rel"], small_w, small_m, small_v, "update_small")
    g_rel_mine = lax.dynamic_slice(rel_sum, (0, me * rel_cols), (ATT_HEADS, rel_cols))
    rel_d, rel_m, rel_v = _adamw_plain(att_rel_bias[0], g_rel_mine, m_att_rel_bias[0], v_att_rel_bias[0], "update_rel_bias")
    u_rel = [g_rel_mine[None], rel_d[None], rel_m[None], rel_v[None]]

    def unpack(t):
        return dict(mix=t[0:2], mlp=t[2:4], fin=t[4], gn=t[5:7].reshape(1, RET_VW))

    us = [unpack(t) for t in (sg, sd, sm, sv)]
    outs = [loss, grad_x]
    for k in range(4):
        outs += [us[k]["mix"], u_rin[k], us[k]["gn"], u_rout[k], u_ain[k], u_rel[k], u_aout[k], us[k]["mlp"],
                 u_w1[k], u_w2[k], us[k]["fin"]]
    return tuple(outs)
```

```python
import functools

import numpy as np
import jax
import jax.numpy as jnp
from jax import lax
from jax.experimental import pallas as pl
from jax.experimental.pallas import tpu as pltpu

F32, BF16 = jnp.float32, jnp.bfloat16

D_MODEL = 1024
CHUNK = 64
RET_HEADS, RET_DK, RET_DV = 4, 256, 512
RET_QK, RET_VW = RET_HEADS * RET_DK, RET_HEADS * RET_DV
RET_IN = 2 * RET_QK + 2 * RET_VW
RET_HEAD_COLS = 2 * RET_DK + 2 * RET_DV
RET_SC = 256
ROPE_BASE = 10000.0
ATT_HEADS, ATT_DH = 16, 64
LANES = 128
ATT_PAIRS = ATT_HEADS * ATT_DH // LANES
BAND_PAST = 8 * CHUNK
MAX_REL = 256
REL_TABLE = MAX_REL + CHUNK
REL_PAD = 384
ATT_CQ = 256
ATT_KW = ATT_CQ + BAND_PAST
ATT_EXT = ATT_CQ + ATT_KW - 1
ATT_VARIANTS = BAND_PAST // ATT_CQ + 1
D_FF = 4 * D_MODEL
EPS = 1e-6
EPILOGUE_ROWS = 256
NEG = -1e30
N_DEV = 8
N_PEERS = N_DEV - 1

ADAM_LR, ADAM_B1, ADAM_B2, ADAM_EPS, ADAM_WD, ADAM_STEP = 0.001, 0.9, 0.999, 1e-08, 0.01, 10

VMEM_LIMIT = 56 * 1024 * 1024
MESH = pl.DeviceIdType.MESH
ANY = pl.BlockSpec(memory_space=pl.ANY)


def _dot(a, b):
    return jnp.dot(a, b, preferred_element_type=F32)


def _dot_nt(a, b):
    return lax.dot_general(a, b, (((1,), (1,)), ((), ())), preferred_element_type=F32)


def _dot_tn(a, b):
    return lax.dot_general(a, b, (((0,), (0,)), ((), ())), preferred_element_type=F32)


def _rms(x, g):
    r = lax.rsqrt(jnp.mean(x * x, axis=-1, keepdims=True) + EPS)
    return x * r * g


def _rms_bwd(dn, x, g):
    r = lax.rsqrt(jnp.mean(x * x, axis=-1, keepdims=True) + EPS)
    xh = x * r
    dg = jnp.sum(dn * xh, axis=0, keepdims=True)
    dxh = dn * g
    dx = r * (dxh - xh * jnp.mean(dxh * xh, axis=-1, keepdims=True))
    return dx, dg


def _rms_bwd_epilogue(x_ref, g_ref, dres_ref, dx_ref, dxb_ref, dg_ref):
    for r in range(0, dx_ref.shape[0], EPILOGUE_ROWS):
        rows = slice(r, r + EPILOGUE_ROWS)
        dx, dg = _rms_bwd(dx_ref[rows, :], x_ref[rows, :], g_ref[...])
        dx = dres_ref[rows, :] + dx
        dx_ref[rows, :] = dx
        dxb_ref[rows, :] = dx.astype(BF16)
        dg_ref[...] += dg


def _place():
    return lax.axis_index("x"), lax.axis_index("y"), lax.axis_index("c")


def _lin(p):
    return 4 * p[0] + 2 * p[1] + p[2]


def _cols(width):
    return lambda ref, blk: ref.at[:, pl.ds(pl.multiple_of(blk * width, LANES), width)]


def _rows(height):
    return lambda ref, blk: ref.at[pl.ds(pl.multiple_of(blk * height, 8), height), :]


def _slot(ref, blk):
    return ref.at[blk]


def _whole(ref, blk):
    return ref


class _Exchange:
    def __init__(self, sources, flows, land_shapes):
        self.sources, self.flows, self.land_shapes = list(sources), flows, list(land_shapes)
        n_f = len(flows)
        self.sem_shapes = [pltpu.SemaphoreType.DMA((N_PEERS * n_f,)), pltpu.SemaphoreType.DMA((N_PEERS * n_f,)),
                           pltpu.SemaphoreType.DMA((n_f,))]

    def _copies(self, srcs, lands, sems):
        send_sems, recv_sems, local_sems = sems
        x, y, c = _place()
        me = (x, y, c)
        peers = [(x ^ dx, y ^ dy, c ^ dc) for dx in (0, 1) for dy in (0, 1) for dc in (0, 1)][1:]

        def copy(f, k, sender, to):
            si, sview, li, lview = self.flows[f]
            return pltpu.make_async_remote_copy(
                src_ref=sview(srcs[si], _lin(to)), dst_ref=lview(lands[li], _lin(sender)),
                send_sem=send_sems.at[f * N_PEERS + k], recv_sem=recv_sems.at[f * N_PEERS + k],
                device_id=to, device_id_type=MESH)

        mine, sends, recvs = [], [], []
        for f, (si, sview, li, lview) in enumerate(self.flows):
            mine.append(pltpu.make_async_copy(sview(srcs[si], _lin(me)), lview(lands[li], _lin(me)), local_sems.at[f]))
            for k, peer in enumerate(peers):
                sends.append(copy(f, k, me, peer))
                recvs.append(copy(f, k, peer, me))
        return mine, sends, recvs

    def start(self, srcs, lands, sems):
        mine, sends, _ = self._copies(srcs, lands, sems)
        for cp in mine + sends:
            cp.start()

    def finish(self, srcs, lands, sems):
        mine, sends, recvs = self._copies(srcs, lands, sems)
        for cp in recvs:
            cp.wait_recv()
        for cp in sends:
            cp.wait_send()
        for cp in mine:
            cp.wait()

    def run(self, name):
        n_src, n_land = len(self.sources), len(self.land_shapes)

        def body(*refs):
            srcs, lands, sems = refs[:n_src], refs[n_src:n_src + n_land], refs[n_src + n_land:]
            self.start(srcs, lands, sems)
            self.finish(srcs, lands, sems)

        return pl.pallas_call(body, name=name, in_specs=[ANY] * n_src, out_specs=[ANY] * n_land,
                              out_shape=self.land_shapes, scratch_shapes=self.sem_shapes)(*self.sources)


def _call(body, *, name, grid, in_specs, out_specs, out_shape, args, scratch_shapes=(), ride=None):
    params = pltpu.CompilerParams(dimension_semantics=("arbitrary",) * len(grid), vmem_limit_bytes=VMEM_LIMIT)
    in_specs, out_specs, out_shape, scratch_shapes = list(in_specs), list(out_specs), list(out_shape), list(scratch_shapes)
    if ride is None:
        return pl.pallas_call(body, name=name, grid=grid, in_specs=in_specs, out_specs=out_specs, out_shape=out_shape,
                              scratch_shapes=scratch_shapes, compiler_params=params)(*args)
    n_in, n_out, n_scr = len(in_specs), len(out_specs), len(scratch_shapes)
    n_src, n_land = len(ride.sources), len(ride.land_shapes)

    def riding(*refs):
        bounds = np.cumsum([n_in, n_src, n_out, n_land, n_scr])
        ins, srcs, outs, lands, scr, sems = (refs[a:b] for a, b in zip([0, *bounds], [*bounds, len(refs)]))
        ids = [pl.program_id(d) for d in range(len(grid))]
        first = functools.reduce(lambda a, b: a & b, [i == 0 for i in ids])
        last = functools.reduce(lambda a, b: a & b, [i == n - 1 for i, n in zip(ids, grid)])

        @pl.when(first)
        def _():
            ride.start(srcs, lands, sems)

        body(*ins, *outs, *scr)

        @pl.when(last)
        def _():
            ride.finish(srcs, lands, sems)

    res = pl.pallas_call(
        riding, name=name, grid=grid, in_specs=in_specs + [ANY] * n_src, out_specs=out_specs + [ANY] * n_land,
        out_shape=out_shape + ride.land_shapes, scratch_shapes=scratch_shapes + ride.sem_shapes,
        compiler_params=params)(*args, *ride.sources)
    return res[:n_out], res[n_out:]


def _all_gather(shards, out_shapes, views, name):
    n = len(shards)

    def body(*refs):
        ins, outs = refs[:n], refs[n:2 * n]
        send_sems, recv_sems, local_sems = refs[2 * n:]
        x, y, c = _place()
        me, sibling = (x, y, c), (x, y, 1 - c)
        chips = [(1 - x, y), (x, 1 - y), (1 - x, 1 - y)]

        def copy(a, k, block, to, src=None):
            dst = views[a](outs[a], _lin(block))
            return pltpu.make_async_remote_copy(
                src_ref=dst if src is None else src, dst_ref=dst,
                send_sem=send_sems.at[a * N_PEERS + k], recv_sem=recv_sems.at[a * N_PEERS + k],
                device_id=to, device_id_type=MESH)

        mine = [pltpu.make_async_copy(ins[a], views[a](outs[a], _lin(me)), local_sems.at[a]) for a in range(n)]
        for cp in mine:
            cp.start()
        first = []
        for a in range(n):
            first.append(copy(a, 0, me, sibling, src=ins[a]))
            first += [copy(a, 1 + j, me, (*chip, c), src=ins[a]) for j, chip in enumerate(chips)]
        for cp in first:
            cp.start()
        passed = []
        for j, chip in enumerate(chips):
            for a in range(n):
                copy(a, 1 + j, (*chip, c), me).wait_recv()
                fwd = copy(a, 4 + j, (*chip, c), sibling)
                fwd.start()
                passed.append(fwd)
        for a in range(n):
            copy(a, 0, sibling, me).wait_recv()
            for j, chip in enumerate(chips):
                copy(a, 4 + j, (*chip, 1 - c), me).wait_recv()
        for cp in first + passed:
            cp.wait_send()
        for cp in mine:
            cp.wait()

    return pl.pallas_call(
        body, name=name,
        in_specs=[ANY] * n, out_specs=[ANY] * n, out_shape=out_shapes,
        scratch_shapes=[pltpu.SemaphoreType.DMA((N_PEERS * n,)), pltpu.SemaphoreType.DMA((N_PEERS * n,)),
                        pltpu.SemaphoreType.DMA((n,))],
    )(*shards)


def _norm_proj(h, g, w, pieces, n_steps, out_shape, out_spec, name, tm=1024, ride=None):
    T = h.shape[0]
    n_p = len(pieces)

    def body(h_ref, g_ref, *rest):
        w_refs = rest[:n_p]
        o_ref, n_ref = rest[n_p:]

        @pl.when(pl.program_id(1) == 0)
        def _():
            n_ref[...] = _rms(h_ref[...], g_ref[...]).astype(BF16)

        n = n_ref[...]
        for (width, _, start), w_ref in zip(pieces, w_refs):
            o_ref[:, start:start + width] = _dot(n, w_ref[...]).astype(o_ref.dtype)

    w_specs = [pl.BlockSpec((D_MODEL, width), functools.partial(lambda i, j, f: (0, f(j)), f=col))
               for (width, col, _) in pieces]
    row = pl.BlockSpec((tm, D_MODEL), lambda i, j: (i, 0))
    return _call(body, name=name, grid=(T // tm, n_steps),
                 in_specs=[row, pl.BlockSpec((1, D_MODEL), lambda i, j: (0, 0))] + w_specs,
                 out_specs=[out_spec(tm), row], out_shape=[out_shape, jax.ShapeDtypeStruct((T, D_MODEL), BF16)],
                 args=(h, g, *([w] * n_p)), ride=ride)


def _proj_bwd(dy, dy_spec, w, pieces, n_steps, x, g, dres, name, tm=1024, ride=None):
    T = x.shape[0]
    n_p = len(pieces)

    def body(dy_ref, *rest):
        w_refs = rest[:n_p]
        x_ref, g_ref, dres_ref, dx_ref, dxb_ref, dg_ref = rest[n_p:]
        acc = dx_ref
        i, j = pl.program_id(0), pl.program_id(1)

        @pl.when(j == 0)
        def _():
            acc[...] = jnp.zeros_like(acc)

        @pl.when((i == 0) & (j == 0))
        def _():
            dg_ref[...] = jnp.zeros_like(dg_ref)

        for (width, _, start), w_ref in zip(pieces, w_refs):
            acc[...] += _dot_nt(dy_ref[:, start:start + width], w_ref[...])

        @pl.when(j == n_steps - 1)
        def _():
            _rms_bwd_epilogue(x_ref, g_ref, dres_ref, dx_ref, dxb_ref, dg_ref)

    w_specs = [pl.BlockSpec((D_MODEL, width), functools.partial(lambda i, j, f: (0, f(j)), f=col))
               for (width, col, _) in pieces]
    row = pl.BlockSpec((tm, D_MODEL), lambda i, j: (i, 0))
    vec = pl.BlockSpec((1, D_MODEL), lambda i, j: (0, 0))
    return _call(body, name=name, grid=(T // tm, n_steps),
                 in_specs=[dy_spec(tm)] + w_specs + [row, vec, row], out_specs=[row, row, vec],
                 out_shape=[jax.ShapeDtypeStruct((T, D_MODEL), F32), jax.ShapeDtypeStruct((T, D_MODEL), BF16),
                            jax.ShapeDtypeStruct((1, D_MODEL), F32)],
                 args=(dy, *([w] * n_p), x, g, dres), ride=ride)


def _matmul_res(a, w, res, name, tm=1024):
    T, K = a.shape

    def body(a_ref, w_ref, r_ref, o_ref):
        o_ref[...] = r_ref[...] + _dot(a_ref[...], w_ref[...])

    row = pl.BlockSpec((tm, D_MODEL), lambda i: (i, 0))
    return _call(body, name=name, grid=(T // tm,),
                 in_specs=[pl.BlockSpec((tm, K), lambda i: (i, 0)), pl.BlockSpec((K, D_MODEL), lambda i: (0, 0)), row],
                 out_specs=[row], out_shape=[jax.ShapeDtypeStruct((T, D_MODEL), F32)], args=(a, w, res))[0]


def _matmul_nt(dy, w, name, tm=1024):
    T, N = dy.shape
    K = w.shape[0]

    def body(dy_ref, w_ref, o_ref):
        o_ref[...] = _dot_nt(dy_ref[...], w_ref[...]).astype(BF16)

    return _call(body, name=name, grid=(T // tm,),
                 in_specs=[pl.BlockSpec((tm, N), lambda i: (i, 0)), pl.BlockSpec((K, N), lambda i: (0, 0))],
                 out_specs=[pl.BlockSpec((tm, K), lambda i: (i, 0))], out_shape=[jax.ShapeDtypeStruct((T, K), BF16)],
                 args=(dy, w))[0]


def _wgrad(a, a_spec, b, b_specs, m, n, bm, bn, n_k, name, square_a=False):
    b_specs = b_specs if isinstance(b_specs, (list, tuple)) else [b_specs]
    n_b = len(b_specs)

    def body(a_ref, *rest):
        b_refs = rest[:n_b]
        o_ref, acc = rest[n_b:]
        k = pl.program_id(2)

        @pl.when(k == 0)
        def _():
            acc[...] = jnp.zeros_like(acc)

        av = a_ref[...]
        if square_a:
            af = av.astype(F32)
            av = (af * af).astype(BF16)
        bv = b_refs[0][...] if n_b == 1 else jnp.concatenate([r[...] for r in b_refs], axis=1)
        acc[...] += _dot_tn(av, bv)

        @pl.when(k == n_k - 1)
        def _():
            o_ref[...] = acc[...].astype(BF16)

    return _call(body, name=name, grid=(m // bm, n // bn, n_k), in_specs=[a_spec, *b_specs],
                 out_specs=[pl.BlockSpec((bm, bn), lambda i, j, k: (i, j))], out_shape=[jax.ShapeDtypeStruct((m, n), BF16)],
                 scratch_shapes=[pltpu.VMEM((bm, bn), F32)], args=(a, *([b] * n_b)))[0]


def _mlp_fwd(h, g, w1, w2, name, tm=1024, tf=512, ride=None):
    T = h.shape[0]

    def body(h_ref, g_ref, w1_ref, w2_ref, ho_ref, n_ref, u_ref):
        @pl.when(pl.program_id(1) == 0)
        def _():
            n_ref[...] = _rms(h_ref[...], g_ref[...]).astype(BF16)
            ho_ref[...] = h_ref[...]

        u = jnp.maximum(_dot(n_ref[...], w1_ref[...]), 0.0)
        u_ref[...] = u.astype(BF16)
        ho_ref[...] += _dot((u * u).astype(BF16), w2_ref[...])

    row = pl.BlockSpec((tm, D_MODEL), lambda i, j: (i, 0))
    return _call(body, name=name, grid=(T // tm, D_FF // tf),
                 in_specs=[row, pl.BlockSpec((1, D_MODEL), lambda i, j: (0, 0)),
                           pl.BlockSpec((D_MODEL, tf), lambda i, j: (0, j)), pl.BlockSpec((tf, D_MODEL), lambda i, j: (j, 0))],
                 out_specs=[row, row, pl.BlockSpec((tm, tf), lambda i, j: (i, j))],
                 out_shape=[jax.ShapeDtypeStruct((T, D_MODEL), F32), jax.ShapeDtypeStruct((T, D_MODEL), BF16),
                            jax.ShapeDtypeStruct((T, D_FF), BF16)],
                 args=(h, g, w1, w2), ride=ride)


def _mlp_bwd(dh, u, w1, w2, h, g, name, tm=1024, tf=512, ride=None):
    T = h.shape[0]
    n_j = D_FF // tf

    def body(dh_ref, u_ref, w1_ref, w2_ref, h_ref, g_ref, dx_ref, dxb_ref, du_ref, dg_ref, dhb):
        acc = dx_ref
        i, j = pl.program_id(0), pl.program_id(1)

        @pl.when(j == 0)
        def _():
            dhb[...] = dh_ref[...].astype(BF16)
            acc[...] = jnp.zeros_like(acc)

        @pl.when((i == 0) & (j == 0))
        def _():
            dg_ref[...] = jnp.zeros_like(dg_ref)

        da = _dot_nt(dhb[...], w2_ref[...])
        du = (da * (2.0 * u_ref[...].astype(F32))).astype(BF16)
        du_ref[...] = du
        acc[...] += _dot_nt(du, w1_ref[...])

        @pl.when(j == n_j - 1)
        def _():
            _rms_bwd_epilogue(h_ref, g_ref, dh_ref, dx_ref, dxb_ref, dg_ref)

    row = pl.BlockSpec((tm, D_MODEL), lambda i, j: (i, 0))
    vec = pl.BlockSpec((1, D_MODEL), lambda i, j: (0, 0))
    hid = pl.BlockSpec((tm, tf), lambda i, j: (i, j))
    return _call(body, name=name, grid=(T // tm, n_j),
                 in_specs=[row, hid, pl.BlockSpec((D_MODEL, tf), lambda i, j: (0, j)),
                           pl.BlockSpec((tf, D_MODEL), lambda i, j: (j, 0)), row, vec],
                 out_specs=[row, row, hid, vec],
                 out_shape=[jax.ShapeDtypeStruct((T, D_MODEL), F32), jax.ShapeDtypeStruct((T, D_MODEL), BF16),
                            jax.ShapeDtypeStruct((T, D_FF), BF16), jax.ShapeDtypeStruct((1, D_MODEL), F32)],
                 scratch_shapes=[pltpu.VMEM((tm, D_MODEL), BF16)], args=(dh, u, w1, w2, h, g), ride=ride)


def _final_loss(h, g, target, name, tm=512):
    T = h.shape[0]

    def body(h_ref, g_ref, t_ref, dh_ref, dhb_ref, loss_ref, dg_ref):
        @pl.when(pl.program_id(0) == 0)
        def _():
            loss_ref[...] = jnp.zeros_like(loss_ref)
            dg_ref[...] = jnp.zeros_like(dg_ref)

        x = h_ref[...]
        gg = g_ref[...]
        r = lax.rsqrt(jnp.mean(x * x, axis=-1, keepdims=True) + EPS)
        xh = x * r
        e = xh * gg - t_ref[...]
        per_tok = jnp.mean(e * e, axis=-1, keepdims=True)
        loss_ref[...] += 0.5 * jnp.sum(per_tok, axis=0, keepdims=True)
        dy = e * (1.0 / D_MODEL)
        dg_ref[...] += jnp.sum(dy * xh, axis=0, keepdims=True)
        dxh = dy * gg
        dx = r * (dxh - xh * jnp.mean(dxh * xh, axis=-1, keepdims=True))
        dh_ref[...] = dx
        dhb_ref[...] = dx.astype(BF16)

    row = pl.BlockSpec((tm, D_MODEL), lambda i: (i, 0))
    vec = pl.BlockSpec((1, D_MODEL), lambda i: (0, 0))
    return _call(body, name=name, grid=(T // tm,), in_specs=[row, vec, row],
                 out_specs=[row, row, pl.BlockSpec((8, LANES), lambda i: (0, 0)), vec],
                 out_shape=[jax.ShapeDtypeStruct((T, D_MODEL), F32), jax.ShapeDtypeStruct((T, D_MODEL), BF16),
                            jax.ShapeDtypeStruct((8, LANES), F32), jax.ShapeDtypeStruct((1, D_MODEL), F32)],
                 args=(h, g, target))


def _ret_constants(S):
    log_gamma = jnp.log1p(-jnp.exp2(-5.0 - jnp.arange(RET_HEADS, dtype=F32)))
    idx = jnp.arange(RET_SC, dtype=F32)
    i, j = idx[:, None], idx[None, :]
    same_chunk = jnp.floor(i / CHUNK) == jnp.floor(j / CHUNK)
    mask = jnp.where((j <= i) | same_chunk, jnp.exp(log_gamma[:, None, None] * jnp.abs(i - j)[None]), 0.0)
    qdec = jnp.exp(log_gamma[:, None] * (idx + 1.0)[None, :])[:, :, None]
    kdec = jnp.exp(log_gamma[:, None] * (RET_SC - 1 - idx)[None, :])[:, :, None]
    cdec = jnp.exp(log_gamma * RET_SC)[:, None, None]
    half = RET_DK // 2
    inv = jnp.exp(-jnp.log(ROPE_BASE) * jnp.arange(half, dtype=F32) / half)
    ang = jnp.arange(S, dtype=F32)[:, None] * inv[None, :]
    return jnp.cos(ang), jnp.sin(ang), mask.astype(F32), qdec, kdec, cdec


def _rope(t, cs, sn):
    t1, t2 = t[:, :RET_DK // 2], t[:, RET_DK // 2:]
    return jnp.concatenate([t1 * cs - t2 * sn, t1 * sn + t2 * cs], axis=-1)


def _rope_bwd(d, cs, sn):
    d1, d2 = d[:, :RET_DK // 2], d[:, RET_DK // 2:]
    return jnp.concatenate([d1 * cs + d2 * sn, d2 * cs - d1 * sn], axis=-1)


def _ret_specs(B, S, reverse):
    n_sc = S // RET_SC

    def cc(c):
        return n_sc - 1 - c if reverse else c

    return dict(
        proj=pl.BlockSpec((B, RET_SC, RET_HEAD_COLS), lambda h, c: (0, cc(c), h)),
        trig=pl.BlockSpec((RET_SC, RET_DK // 2), lambda h, c: (cc(c), 0)),
        mask=pl.BlockSpec((None, RET_SC, RET_SC), lambda h, c: (h, 0, 0)),
        dec=pl.BlockSpec((None, RET_SC, 1), lambda h, c: (h, 0, 0)),
        cdec=pl.BlockSpec((None, 1, 1), lambda h, c: (h, 0, 0)),
        gn=pl.BlockSpec((1, RET_DV), lambda h, c: (0, h)),
        val=pl.BlockSpec((B, RET_SC, RET_DV), lambda h, c: (0, cc(c), h)),
        state=pl.BlockSpec((B, None, None, RET_DK, RET_DV), lambda h, c: (0, h, cc(c), 0, 0)),
    )


def _ret_qkvg(p_ref, cs, sn):
    q = _rope(p_ref[:, 0:RET_DK].astype(F32), cs, sn)
    k = _rope(p_ref[:, RET_DK:2 * RET_DK].astype(F32), cs, sn) * (RET_DK ** -0.5)
    v = p_ref[:, 2 * RET_DK:2 * RET_DK + RET_DV]
    gate = p_ref[:, 2 * RET_DK + RET_DV:RET_HEAD_COLS].astype(F32)
    return q, k, v, gate


def _group_norm(o):
    mu = jnp.mean(o, axis=-1, keepdims=True)
    oc = o - mu
    rstd = lax.rsqrt(jnp.mean(oc * oc, axis=-1, keepdims=True) + EPS)
    return oc * rstd, rstd


def _ret_fwd(proj, consts, gn, B, S, name, ride=None):
    T = B * S
    n_sc = S // RET_SC
    sp = _ret_specs(B, S, False)

    def body(p_ref, cos_ref, sin_ref, m_ref, qd_ref, kd_ref, cd_ref, gn_ref, y_ref, o_ref, st_ref, state):
        @pl.when(pl.program_id(1) == 0)
        def _():
            state[...] = jnp.zeros_like(state)

        seqs = range(B)
        qkvg = [_ret_qkvg(p_ref.at[b], cos_ref[...], sin_ref[...]) for b in seqs]
        qb = [qkvg[b][0].astype(BF16) for b in seqs]
        kb = [qkvg[b][1].astype(BF16) for b in seqs]
        kt = [(qkvg[b][1] * kd_ref[...]).astype(BF16) for b in seqs]
        v = [qkvg[b][2] for b in seqs]
        sc = [_dot_nt(qb[b], kb[b]) for b in seqs]
        sb = [state[b].astype(BF16) for b in seqs]
        cross = [_dot(qb[b], sb[b]) for b in seqs]
        for b in seqs:
            st_ref[b] = sb[b]
        p = [(sc[b] * m_ref[...]).astype(BF16) for b in seqs]
        o = [_dot(p[b], v[b]) + cross[b] * qd_ref[...] for b in seqs]
        upd = [_dot_tn(kt[b], v[b]) for b in seqs]
        for b in seqs:
            o_ref[b] = o[b]
            ohat, _ = _group_norm(o[b])
            gate = qkvg[b][3]
            y_ref[b] = (gate * jax.nn.sigmoid(gate) * (ohat * gn_ref[...])).astype(BF16)
            state[b] = state[b] * cd_ref[...] + upd[b]

    res = _call(
        body, name=name, grid=(RET_HEADS, n_sc),
        in_specs=[sp["proj"], sp["trig"], sp["trig"], sp["mask"], sp["dec"], sp["dec"], sp["cdec"], sp["gn"]],
        out_specs=[sp["val"], sp["val"], sp["state"]],
        out_shape=[jax.ShapeDtypeStruct((B, S, RET_VW), BF16), jax.ShapeDtypeStruct((B, S, RET_VW), F32),
                   jax.ShapeDtypeStruct((B, RET_HEADS, n_sc, RET_DK, RET_DV), BF16)],
        scratch_shapes=[pltpu.VMEM((B, RET_DK, RET_DV), F32)],
        args=(proj.reshape(B, S, -1), *consts, gn), ride=ride)
    (y, o, states), lands = res if ride is not None else (res, None)
    outs = [y.reshape(T, RET_VW), o.reshape(T, RET_VW), states]
    return outs if ride is None else (outs, lands)


def _ret_bwd(proj, consts, gn, o, states, dy, B, S, name, ride=None):
    T = B * S
    n_sc = S // RET_SC
    sp = _ret_specs(B, S, True)

    def body(p_ref, cos_ref, sin_ref, m_ref, qd_ref, kd_ref, cd_ref, gn_ref, o_ref, st_ref, dy_ref,
             dp_ref, dgn_ref, dstate):
        @pl.when(pl.program_id(1) == 0)
        def _():
            dstate[...] = jnp.zeros_like(dstate)
            dgn_ref[...] = jnp.zeros_like(dgn_ref)

        seqs = range(B)
        cs, sn = cos_ref[...], sin_ref[...]
        m, gnv = m_ref[...], gn_ref[...]
        qkvg = [_ret_qkvg(p_ref.at[b], cs, sn) for b in seqs]
        qb = [qkvg[b][0].astype(BF16) for b in seqs]
        kb = [qkvg[b][1].astype(BF16) for b in seqs]
        kt = [(qkvg[b][1] * kd_ref[...]).astype(BF16) for b in seqs]
        v = [qkvg[b][2] for b in seqs]
        sc = [_dot_nt(qb[b], kb[b]) for b in seqs]
        dsb = [dstate[b].astype(BF16) for b in seqs]
        dv_state = [_dot(kt[b], dsb[b]) for b in seqs]
        dkt = [_dot_nt(v[b], dsb[b]) for b in seqs]
        do, dgate = [], []
        for b in seqs:
            gate = qkvg[b][3]
            ohat, rstd = _group_norm(o_ref[b])
            dyv = dy_ref[b].astype(F32)
            sg = jax.nn.sigmoid(gate)
            don = dyv * (gate * sg)
            dgate.append(dyv * (ohat * gnv) * (sg * (1.0 + gate * (1.0 - sg))))
            dgn_ref[...] += jnp.sum(don * ohat, axis=0, keepdims=True)
            dohat = don * gnv
            do.append(rstd * (dohat - jnp.mean(dohat, axis=-1, keepdims=True)
                              - ohat * jnp.mean(dohat * ohat, axis=-1, keepdims=True)))
        dob = [do[b].astype(BF16) for b in seqs]
        doq = [(do[b] * qd_ref[...]).astype(BF16) for b in seqs]
        dsc_f = [_dot_nt(dob[b], v[b]) for b in seqs]
        dq_state = [_dot_nt(doq[b], st_ref[b]) for b in seqs]
        dstate_upd = [_dot_tn(qb[b], doq[b]) for b in seqs]
        p = [(sc[b] * m).astype(BF16) for b in seqs]
        dsc = [(dsc_f[b] * m).astype(BF16) for b in seqs]
        dv = [_dot_tn(p[b], dob[b]) + dv_state[b] for b in seqs]
        dq = [_dot(dsc[b], kb[b]) + dq_state[b] for b in seqs]
        dk = [(_dot_tn(dsc[b], qb[b]) + dkt[b] * kd_ref[...]) * (RET_DK ** -0.5) for b in seqs]
        for b in seqs:
            dstate[b] = dstate[b] * cd_ref[...] + dstate_upd[b]
            dp_ref[b, :, 0:RET_DK] = _rope_bwd(dq[b], cs, sn).astype(BF16)
            dp_ref[b, :, RET_DK:2 * RET_DK] = _rope_bwd(dk[b], cs, sn).astype(BF16)
            dp_ref[b, :, 2 * RET_DK:2 * RET_DK + RET_DV] = dv[b].astype(BF16)
            dp_ref[b, :, 2 * RET_DK + RET_DV:RET_HEAD_COLS] = dgate[b].astype(BF16)

    res = _call(
        body, name=name, grid=(RET_HEADS, n_sc),
        in_specs=[sp["proj"], sp["trig"], sp["trig"], sp["mask"], sp["dec"], sp["dec"], sp["cdec"], sp["gn"],
                  sp["val"], sp["state"], sp["val"]],
        out_specs=[sp["proj"], sp["gn"]],
        out_shape=[jax.ShapeDtypeStruct((B, S, RET_HEADS * RET_HEAD_COLS), BF16), jax.ShapeDtypeStruct((1, RET_VW), F32)],
        scratch_shapes=[pltpu.VMEM((B, RET_DK, RET_DV), F32)],
        args=(proj.reshape(B, S, -1), *consts, gn, o.reshape(B, S, -1), states, dy.reshape(B, S, -1)), ride=ride)
    (dproj, dgn), lands = res if ride is not None else (res, None)
    outs = [dproj.reshape(T, -1), dgn]
    return outs if ride is None else (outs, lands)


def _rel_index():
    t = np.arange(ATT_EXT)
    return np.clip(t - (ATT_KW - 1), -MAX_REL, CHUNK - 1) + MAX_REL


def _bias_matrices(rel):
    idx = _rel_index()
    n_lo = int(np.argmax(idx > 0))
    n_hi = ATT_EXT - (n_lo + REL_TABLE - 1)
    ext = jnp.concatenate([jnp.broadcast_to(rel[:, :1], (ATT_HEADS, n_lo - 1)), rel,
                           jnp.broadcast_to(rel[:, -1:], (ATT_HEADS, n_hi))], axis=1)
    e_pad = jnp.pad(ext, ((0, 0), (0, 1)))
    tiled = jnp.broadcast_to(e_pad[:, None, :], (ATT_HEADS, ATT_CQ, ATT_EXT + 1))
    skew = tiled.reshape(ATT_HEADS, -1)[:, :ATT_CQ * ATT_EXT].reshape(ATT_HEADS, ATT_CQ, ATT_EXT)
    toep = skew[:, :, ATT_CQ - 1:ATT_CQ - 1 + ATT_KW]
    i = np.arange(ATT_CQ)[:, None]
    j = np.arange(ATT_KW)[None, :]
    lo = CHUNK * (i // CHUNK)
    band = np.where((j >= lo) & (j < lo + BAND_PAST + CHUNK), 0.0, NEG)
    variants = np.stack([band + np.where(j + v * ATT_CQ >= BAND_PAST, 0.0, NEG) for v in range(ATT_VARIANTS)])
    return toep.reshape(ATT_PAIRS, 1, 2, ATT_CQ, ATT_KW) + variants.astype(np.float32)[None, :, None]


ATT_STRIP = 32
ATT_SCALE = ATT_DH ** -0.5


def _strips(fn):
    def strip(r, carry):
        fn(pl.ds(pl.multiple_of(r * ATT_STRIP, ATT_STRIP), ATT_STRIP))
        return carry

    lax.fori_loop(0, ATT_CQ // ATT_STRIP, strip, 0, unroll=True)


def _attn_prepare(kpad, vpad, qm, qkv_ref):
    kpad[0:BAND_PAST, :] = jnp.zeros((BAND_PAST, LANES), BF16)
    vpad[0:BAND_PAST, :] = jnp.zeros((BAND_PAST, LANES), BF16)
    kpad[BAND_PAST:, :] = qkv_ref[1]
    vpad[BAND_PAST:, :] = qkv_ref[2]
    lane = lax.broadcasted_iota(jnp.int32, (1, LANES), 1)
    q = qkv_ref[0] * ATT_SCALE
    for hh in range(2):
        qm[hh] = jnp.where((lane >= ATT_DH * hh) & (lane < ATT_DH * (hh + 1)), q, jnp.zeros_like(q))


def _attn_scores(s_ref, qm, kpad, b_ref, t, n_qb):
    t = jnp.minimum(t, n_qb - 1)
    qs = pl.multiple_of(t * ATT_CQ, ATT_CQ)
    variant = jnp.minimum(t, ATT_VARIANTS - 1)
    kw = kpad[pl.ds(qs, ATT_KW), :]
    for hh in range(2):
        s_ref[hh] = _dot_nt(qm[hh, pl.ds(qs, ATT_CQ), :], kw) + b_ref[variant, hh]


def _attn_softmax(s_ref, e_ref, linv_ref, m_ref=None):
    def strip(rows):
        s = s_ref[rows, :]
        m = jnp.max(s, axis=-1, keepdims=True)
        e = jnp.exp(s - m)
        e_ref[rows, :] = e.astype(BF16)
        linv_ref[rows, :] = jnp.broadcast_to(1.0 / jnp.sum(e, axis=-1, keepdims=True), (ATT_STRIP, LANES))
        if m_ref is not None:
            m_ref[rows, :] = jnp.broadcast_to(m, (ATT_STRIP, LANES))

    _strips(strip)


def _attn_fwd(qkv3, bias, B, S, name, ride=None):
    T = B * S
    n_qb = S // ATT_CQ

    def body(qkv_ref, b_ref, o_ref, kpad, vpad, qm, s_scr, e_scr, linv_scr):
        _attn_prepare(kpad, vpad, qm, qkv_ref)
        e_scr[...] = jnp.zeros_like(e_scr)
        linv_scr[...] = jnp.zeros_like(linv_scr)
        lane = lax.broadcasted_iota(jnp.int32, (1, LANES), 1)

        def softmax(slot):
            for hh in range(2):
                _attn_softmax(s_scr.at[slot, hh], e_scr.at[slot, hh], linv_scr.at[slot, hh])

        def output(t, slot):
            qs = pl.multiple_of(jnp.maximum(t, 0) * ATT_CQ, ATT_CQ)
            vw = vpad[pl.ds(qs, ATT_KW), :]
            outs = [_dot(e_scr[slot, hh], vw) * linv_scr[slot, hh] for hh in range(2)]
            o_ref[pl.ds(qs, ATT_CQ), :] = jnp.where(lane < ATT_DH, outs[0], outs[1]).astype(BF16)

        def pair(u, carry):
            t = 2 * u
            _attn_scores(s_scr.at[1], qm, kpad, b_ref, t + 1, n_qb)
            softmax(0)
            output(t - 1, 1)
            _attn_scores(s_scr.at[0], qm, kpad, b_ref, t + 2, n_qb)
            softmax(1)
            output(t, 0)
            return carry

        _attn_scores(s_scr.at[0], qm, kpad, b_ref, 0, n_qb)
        lax.fori_loop(0, n_qb // 2, pair, 0)
        output(n_qb - 1, 1)

    return _call(body, name=name, grid=(ATT_PAIRS, B),
                 in_specs=[pl.BlockSpec((3, S, LANES), lambda hp, b: (0, b, hp)),
                           pl.BlockSpec((None, ATT_VARIANTS, 2, ATT_CQ, ATT_KW), lambda hp, b: (hp, 0, 0, 0, 0))],
                 out_specs=[pl.BlockSpec((S, LANES), lambda hp, b: (b, hp))],
                 out_shape=[jax.ShapeDtypeStruct((T, D_MODEL), BF16)],
                 scratch_shapes=[pltpu.VMEM((S + BAND_PAST, LANES), BF16), pltpu.VMEM((S + BAND_PAST, LANES), BF16),
                                 pltpu.VMEM((2, S, LANES), BF16), pltpu.VMEM((2, 2, ATT_CQ, ATT_KW), F32),
                                 pltpu.VMEM((2, 2, ATT_CQ, ATT_KW), BF16), pltpu.VMEM((2, 2, ATT_CQ, LANES), F32)],
                 args=(qkv3, bias), ride=ride)


def _attn_bwd(qkv3, bias, do, B, S, name, ride=None):
    T = B * S
    n_qb = S // ATT_CQ

    def body(qkv_ref, b_ref, do_ref, dqkv_ref, db_ref, kpad, vpad, qm, dkacc, dvacc):
        @pl.when(pl.program_id(1) == 0)
        def _():
            db_ref[...] = jnp.zeros_like(db_ref)

        _attn_prepare(kpad, vpad, qm, qkv_ref)
        dkacc[...] = jnp.zeros_like(dkacc)
        dvacc[...] = jnp.zeros_like(dvacc)
        lane = lax.broadcasted_iota(jnp.int32, (1, LANES), 1)

        def step(qb, carry):
            qs = pl.multiple_of(qb * ATT_CQ, ATT_CQ)
            variant = jnp.minimum(qb, ATT_VARIANTS - 1)
            dov = do_ref[pl.ds(qs, ATT_CQ), :].astype(F32)
            kw = kpad[pl.ds(qs, ATT_KW), :]
            vw = vpad[pl.ds(qs, ATT_KW), :]
            heads = (0, 1)
            qmh = [qm[hh, pl.ds(qs, ATT_CQ), :] for hh in heads]
            s = [_dot_nt(qmh[hh], kw) + b_ref[variant, hh] for hh in heads]
            e, linv, dom, dp = [None, None], [None, None], [None, None], [None, None]
            for hh in heads:
                e[hh] = jnp.exp(s[hh] - jnp.max(s[hh], axis=-1, keepdims=True))
                linv[hh] = 1.0 / jnp.sum(e[hh], axis=-1, keepdims=True)
                sel = (lane >= ATT_DH * hh) & (lane < ATT_DH * (hh + 1))
                dom[hh] = jnp.where(sel, dov * linv[hh], 0.0).astype(BF16)
                dp[hh] = _dot_nt(dom[hh], vw)
            dqs, dk, dv = [], None, None
            for hh in heads:
                ds = e[hh] * (dp[hh] - jnp.sum(dp[hh] * e[hh], axis=-1, keepdims=True) * linv[hh])
                db_ref[hh] += ds
                dsb = ds.astype(BF16)
                dqs.append(_dot(dsb, kw) * ATT_SCALE)
                dk_h = _dot_tn(qmh[hh], dsb)
                dv_h = _dot_tn(dom[hh], e[hh].astype(BF16))
                dk = dk_h if dk is None else dk + dk_h
                dv = dv_h if dv is None else dv + dv_h
            dqkv_ref[0, pl.ds(qs, ATT_CQ), :] = jnp.where(lane < ATT_DH, dqs[0], dqs[1]).astype(BF16)
            dkacc[:, pl.ds(qs, ATT_KW)] += dk
            dvacc[:, pl.ds(qs, ATT_KW)] += dv
            return carry

        lax.fori_loop(0, n_qb, step, 0)
        dqkv_ref[1] = dkacc[:, BAND_PAST:].T.astype(BF16)
        dqkv_ref[2] = dvacc[:, BAND_PAST:].T.astype(BF16)

    bias_spec = pl.BlockSpec((None, ATT_VARIANTS, 2, ATT_CQ, ATT_KW), lambda hp, b: (hp, 0, 0, 0, 0))
    dbias_spec = pl.BlockSpec((None, 2, ATT_CQ, ATT_KW), lambda hp, b: (hp, 0, 0, 0))
    qkv_spec = pl.BlockSpec((3, S, LANES), lambda hp, b: (0, b, hp))
    return _call(body, name=name, grid=(ATT_PAIRS, B),
                 in_specs=[qkv_spec, bias_spec, pl.BlockSpec((S, LANES), lambda hp, b: (b, hp))],
                 out_specs=[qkv_spec, dbias_spec],
                 out_shape=[jax.ShapeDtypeStruct((3, T, D_MODEL), BF16),
                            jax.ShapeDtypeStruct((ATT_PAIRS, 2, ATT_CQ, ATT_KW), F32)],
                 scratch_shapes=[pltpu.VMEM((S + BAND_PAST, LANES), BF16), pltpu.VMEM((S + BAND_PAST, LANES), BF16),
                                 pltpu.VMEM((2, S, LANES), BF16),
                                 pltpu.VMEM((LANES, S + BAND_PAST), F32), pltpu.VMEM((LANES, S + BAND_PAST), F32)],
                 args=(qkv3, bias, do), ride=ride)


def _rel_bias_grad(dbias, name):
    d = dbias.reshape(ATT_HEADS, ATT_CQ, ATT_KW)
    d = jnp.pad(d, ((0, 0), (0, 0), (ATT_CQ - 1, 0))).reshape(ATT_HEADS, ATT_CQ * ATT_EXT)
    skew = jnp.pad(d, ((0, 0), (0, ATT_CQ))).reshape(ATT_HEADS, ATT_CQ, ATT_EXT + 1)
    fold = np.zeros((ATT_EXT + 1, REL_PAD), np.float32)
    fold[np.arange(ATT_EXT), _rel_index()] = 1.0
    fold = jnp.asarray(fold, BF16)

    def body(s_ref, f_ref, o_ref):
        x = jnp.sum(s_ref[...], axis=0, keepdims=True)
        x = jnp.broadcast_to(x, (8, ATT_EXT + 1))
        acc = jnp.zeros((8, REL_PAD), F32)
        for _ in range(3):
            part = x.astype(BF16)
            acc = acc + _dot(part, f_ref[...])
            x = x - part.astype(F32)
        o_ref[...] = acc[0:1, :]

    out = _call(body, name=name, grid=(ATT_HEADS,),
                in_specs=[pl.BlockSpec((None, ATT_CQ, ATT_EXT + 1), lambda h: (h, 0, 0)),
                          pl.BlockSpec((ATT_EXT + 1, REL_PAD), lambda h: (0, 0))],
                out_specs=[pl.BlockSpec((None, 1, REL_PAD), lambda h: (h, 0, 0))],
                out_shape=[jax.ShapeDtypeStruct((ATT_HEADS, 1, REL_PAD), F32)], args=(skew, fold))[0]
    return out.reshape(ATT_HEADS, REL_PAD)


def _adamw(w, g, m, v):
    m = ADAM_B1 * m + (1.0 - ADAM_B1) * g
    v = ADAM_B2 * v + (1.0 - ADAM_B2) * (g * g)
    m_hat = m / (1.0 - ADAM_B1 ** ADAM_STEP)
    v_hat = v / (1.0 - ADAM_B2 ** ADAM_STEP)
    delta = -ADAM_LR * (m_hat / (jnp.sqrt(v_hat) + ADAM_EPS) + ADAM_WD * w)
    return delta, m, v


def _sum_devices(ref):
    g = ref[0].astype(F32)
    for d in range(1, N_DEV):
        g = g + ref[d].astype(F32)
    return g


def _adamw_reduce(lands, w, m, v, name, tr=256):
    L, R, C = w.shape
    tr = min(tr, R)
    n_i = R // tr

    def body(*refs):
        l_refs = refs[:L]
        w_ref, m_ref, v_ref, g_out, d_out, m_out, v_out = refs[L:]
        layer = pl.program_id(0)
        for l in range(L):
            @pl.when(layer == l)
            def _(l=l):
                g = _sum_devices(l_refs[l])
                g_out[...] = g
                d_out[...], m_out[...], v_out[...] = _adamw(w_ref[...], g, m_ref[...], v_ref[...])

    def land_spec(l):
        return pl.BlockSpec((N_DEV, tr, C), lambda ly, i: (0, jnp.where(ly == l, i, jnp.where(ly < l, 0, n_i - 1)), 0))

    blk = pl.BlockSpec((None, tr, C), lambda ly, i: (ly, i, 0))
    return _call(body, name=name, grid=(L, n_i), in_specs=[land_spec(l) for l in range(L)] + [blk, blk, blk],
                 out_specs=[blk] * 4, out_shape=[jax.ShapeDtypeStruct((L, R, C), F32)] * 4, args=(*lands, w, m, v))


def _small_update(land_small, land_rel, w, m, v, name):
    def body(ls_ref, lr_ref, w_ref, m_ref, v_ref, g_out, d_out, m_out, v_out, rel_out):
        g = _sum_devices(ls_ref)
        g_out[...] = g
        d_out[...], m_out[...], v_out[...] = _adamw(w_ref[...], g, m_ref[...], v_ref[...])
        rel_out[...] = _sum_devices(lr_ref)

    return pl.pallas_call(
        body, name=name,
        out_shape=[jax.ShapeDtypeStruct(w.shape, F32)] * 4 + [jax.ShapeDtypeStruct(land_rel.shape[1:], F32)],
    )(land_small, land_rel, w, m, v)


def _adamw_plain(w, g, m, v, name):
    def body(w_ref, g_ref, m_ref, v_ref, d_out, m_out, v_out):
        d_out[...], m_out[...], v_out[...] = _adamw(w_ref[...], g_ref[...], m_ref[...], v_ref[...])

    return pl.pallas_call(body, name=name, out_shape=[jax.ShapeDtypeStruct(w.shape, F32)] * 3)(w, g, m, v)


def _ret_pieces():
    return [(RET_DK, lambda h: h, 0),
            (RET_DK, lambda h: RET_HEADS + h, RET_DK),
            (RET_DV, lambda h: RET_HEADS + h, 2 * RET_DK),
            (RET_DV, lambda h: 2 * RET_HEADS + h, 2 * RET_DK + RET_DV)]


def _ret_piece_of_column_block(p):
    per_head = RET_HEAD_COLS // RET_DK
    qk = jnp.where(p < RET_HEADS, per_head * p, per_head * (p - RET_HEADS) + 1)
    pv = p - 2 * RET_HEADS
    vv = per_head * (pv // 2) + 2 + pv % 2
    pg = p - 4 * RET_HEADS
    gg = per_head * (pg // 2) + 4 + pg % 2
    return jnp.where(p < 2 * RET_HEADS, qk, jnp.where(p < 4 * RET_HEADS, vv, gg))


def _gather(shard, full_shape, view):
    return _Exchange([shard], [(0, _whole, 0, view)], [jax.ShapeDtypeStruct(full_shape, shard.dtype)])


def _step(x, target, shards, mix_g, gn_g, rel_shard, mlp_g, final_g):
    s_rin, s_rout, s_ain, s_aout, s_w1, s_w2 = shards
    B, S, _ = x.shape
    T = B * S
    tk = min(T, 2048)
    n_k = T // tk
    h0 = x.reshape(T, D_MODEL)
    tgt = target.reshape(T, D_MODEL)
    consts = _ret_constants(S)
    att_pieces = [(D_MODEL, lambda j: j, 0)]
    w1_cols = w2_rows = D_FF // N_DEV

    def qkv_spec(tm):
        return pl.BlockSpec((None, tm, D_MODEL), lambda i, j: (j, i, 0))

    def head_spec(tm):
        return pl.BlockSpec((tm, RET_HEAD_COLS), lambda i, j: (i, j))

    rel_cols = REL_TABLE // N_DEV
    w_rin, rel_all = _all_gather(
        [s_rin, rel_shard],
        [jax.ShapeDtypeStruct((D_MODEL, RET_IN), BF16), jax.ShapeDtypeStruct((N_DEV, ATT_HEADS, rel_cols), F32)],
        [_cols(RET_IN // N_DEV), _slot], "gather_first")
    bias = _bias_matrices(rel_all.transpose(1, 0, 2).reshape(ATT_HEADS, REL_TABLE))

    ride = _Exchange([s_rout, s_w1[0]], [(0, _whole, 0, _rows(RET_VW // N_DEV)), (1, _whole, 1, _cols(w1_cols))],
                     [jax.ShapeDtypeStruct((RET_VW, D_MODEL), BF16), jax.ShapeDtypeStruct((D_MODEL, D_FF), BF16)])
    (proj, n0), (w_rout, w1_0) = _norm_proj(h0, mix_g[0:1], w_rin, _ret_pieces(), RET_HEADS,
                                            jax.ShapeDtypeStruct((T, RET_HEADS * RET_HEAD_COLS), BF16), head_spec,
                                            "ret_proj", ride=ride)
    ride = _Exchange([s_w2[0], s_ain], [(0, _whole, 0, _rows(w2_rows)), (1, _whole, 1, _cols(3 * D_MODEL // N_DEV))],
                     [jax.ShapeDtypeStruct((D_FF, D_MODEL), BF16), jax.ShapeDtypeStruct((D_MODEL, 3 * D_MODEL), BF16)])
    (y, o, states), (w2_0, w_ain) = _ret_fwd(proj, consts, gn_g, B, S, "ret_fwd", ride=ride)
    h1 = _matmul_res(y, w_rout, h0, "ret_out")
    ride = _Exchange([s_aout, s_w1[1]], [(0, _whole, 0, _rows(D_MODEL // N_DEV)), (1, _whole, 1, _cols(w1_cols))],
                     [jax.ShapeDtypeStruct((D_MODEL, D_MODEL), BF16), jax.ShapeDtypeStruct((D_MODEL, D_FF), BF16)])
    (h2, n1, u1), (w_aout, w1_1) = _mlp_fwd(h1, mlp_g[0:1], w1_0, w2_0, "mlp0_fwd", ride=ride)
    qkv3, n2 = _norm_proj(h2, mix_g[1:2], w_ain, att_pieces, 3, jax.ShapeDtypeStruct((3, T, D_MODEL), BF16), qkv_spec,
                          "att_proj")
    (o2,), (w2_1,) = _attn_fwd(qkv3, bias, B, S, "att_fwd", ride=_gather(s_w2[1], (D_FF, D_MODEL), _rows(w2_rows)))
    h3 = _matmul_res(o2, w_aout, h2, "att_out")
    h4, n3, u3 = _mlp_fwd(h3, mlp_g[1:2], w1_1, w2_1, "mlp1_fwd")
    dh4, dh4b, loss, dg_final = _final_loss(h4, final_g, tgt, "final_loss")

    def tok(width):
        return dict(a=pl.BlockSpec((tk, width), lambda i, j, k: (k, i)), b=pl.BlockSpec((tk, width), lambda i, j, k: (k, j)))

    def mlp_grads(dh, dhb, u, n, w1, w2, h, g, tag, ride=None):
        res = _mlp_bwd(dh, u, w1, w2, h, g, tag + "_bwd", ride=ride)
        (dhi, dhib, du, dg), lands = res if ride is not None else (res, None)
        gw2 = _wgrad(u, tok(1024)["a"], dhb, tok(1024)["b"], D_FF, D_MODEL, 1024, 1024, n_k, tag + "_dw2", square_a=True)
        gw1 = _wgrad(n, tok(1024)["a"], du, tok(1024)["b"], D_MODEL, D_FF, 1024, 1024, n_k, tag + "_dw1")
        return dhi, dhib, gw1, gw2, dg, lands

    dh3, dh3b, gw1_1, gw2_1, dg_mlp1, _ = mlp_grads(dh4, dh4b, u3, n3, w1_1, w2_1, h3, mlp_g[1:2], "mlp1")
    do2 = _matmul_nt(dh3b, w_aout, "att_out_bwd")
    g_aout = _wgrad(o2, tok(1024)["a"], dh3b, tok(1024)["b"], D_MODEL, D_MODEL, 1024, 1024, n_k, "att_out_dw")
    ride = _Exchange(
        [gw1_1, gw2_1, g_aout],
        [(0, _cols(w1_cols), 0, _slot), (1, _rows(w2_rows), 1, _slot), (2, _rows(D_MODEL // N_DEV), 2, _slot)],
        [jax.ShapeDtypeStruct((N_DEV, D_MODEL, w1_cols), BF16), jax.ShapeDtypeStruct((N_DEV, w2_rows, D_MODEL), BF16),
         jax.ShapeDtypeStruct((N_DEV, D_MODEL // N_DEV, D_MODEL), BF16)])
    (dqkv3, dbias), (l_w1_1, l_w2_1, l_aout) = _attn_bwd(qkv3, bias, do2, B, S, "att_bwd", ride=ride)
    g_rel = _rel_bias_grad(dbias, "rel_bias_grad")
    dh2, dh2b, dg_mix1 = _proj_bwd(dqkv3, qkv_spec, w_ain, att_pieces, 3, h2, mix_g[1:2], dh3, "att_proj_bwd")
    g_ain = _wgrad(n2, tok(1024)["a"], dqkv3, pl.BlockSpec((None, tk, D_MODEL), lambda i, j, k: (j, k, 0)),
                   D_MODEL, 3 * D_MODEL, 1024, 1024, n_k, "att_proj_dw")
    ride = _Exchange([g_ain], [(0, _cols(3 * D_MODEL // N_DEV), 0, _slot)],
                     [jax.ShapeDtypeStruct((N_DEV, D_MODEL, 3 * D_MODEL // N_DEV), BF16)])
    dh1, dh1b, gw1_0, gw2_0, dg_mlp0, (l_ain,) = mlp_grads(dh2, dh2b, u1, n1, w1_0, w2_0, h1, mlp_g[0:1], "mlp0", ride=ride)
    dy = _matmul_nt(dh1b, w_rout, "ret_out_bwd")
    g_rout = _wgrad(y, tok(1024)["a"], dh1b, tok(1024)["b"], RET_VW, D_MODEL, 1024, 1024, n_k, "ret_out_dw")
    ride = _Exchange(
        [gw1_0, gw2_0, g_rout],
        [(0, _cols(w1_cols), 0, _slot), (1, _rows(w2_rows), 1, _slot), (2, _rows(RET_VW // N_DEV), 2, _slot)],
        [jax.ShapeDtypeStruct((N_DEV, D_MODEL, w1_cols), BF16), jax.ShapeDtypeStruct((N_DEV, w2_rows, D_MODEL), BF16),
         jax.ShapeDtypeStruct((N_DEV, RET_VW // N_DEV, D_MODEL), BF16)])
    (dproj, dgn), (l_w1_0, l_w2_0, l_rout) = _ret_bwd(proj, consts, gn_g, o, states, dy, B, S, "ret_bwd", ride=ride)
    per_shard = RET_IN // N_DEV // RET_DK
    g_rin = _wgrad(n0, tok(1024)["a"], dproj,
                   [pl.BlockSpec((tk, RET_DK), functools.partial(
                       lambda i, j, k, r: (k, _ret_piece_of_column_block(per_shard * j + r)), r=r)) for r in range(per_shard)],
                   D_MODEL, RET_IN, 1024, per_shard * RET_DK, n_k, "ret_proj_dw")
    ride = _Exchange([g_rin], [(0, _cols(RET_IN // N_DEV), 0, _slot)],
                     [jax.ShapeDtypeStruct((N_DEV, D_MODEL, RET_IN // N_DEV), BF16)])
    (dx, _, dg_mix0), (l_rin,) = _proj_bwd(dproj, head_spec, w_rin, _ret_pieces(), RET_HEADS, h0, mix_g[0:1], dh1,
                                           "ret_proj_bwd", ride=ride)

    small = jnp.concatenate([dg_mix0, dg_mix1, dg_mlp0, dg_mlp1, dg_final, dgn.reshape(2, D_MODEL),
                             jnp.zeros((1, D_MODEL), F32)], axis=0)
    l_small, l_rel = _Exchange(
        [small, g_rel], [(0, _whole, 0, _slot), (1, _whole, 1, _slot)],
        [jax.ShapeDtypeStruct((N_DEV, 8, D_MODEL), F32), jax.ShapeDtypeStruct((N_DEV, ATT_HEADS, REL_PAD), F32)],
    ).run("scatter_small")
    lands = dict(rin=l_rin, rout=l_rout, ain=l_ain, aout=l_aout, w1=(l_w1_0, l_w1_1), w2=(l_w2_0, l_w2_1),
                 small=l_small, rel=l_rel)
    return loss[0, 0], dx.reshape(B, S, D_MODEL), lands


def kernel(x, mix_norm_g, ret_w_in, ret_gn_g, ret_w_out, att_w_in, att_rel_bias, att_w_out, mlp_norm_g, mlp_w1, mlp_w2, final_norm_g, loss_target, m_mix_norm_g, m_ret_w_in, m_ret_gn_g, m_ret_w_out, m_att_w_in, m_att_rel_bias, m_att_w_out, m_mlp_norm_g, m_mlp_w1, m_mlp_w2, m_final_norm_g, v_mix_norm_g, v_ret_w_in, v_ret_gn_g, v_ret_w_out, v_att_w_in, v_att_rel_bias, v_att_w_out, v_mlp_norm_g, v_mlp_w1, v_mlp_w2, v_final_norm_g):
    me = _lin(_place())
    rel_cols = REL_TABLE // N_DEV
    shards = (ret_w_in[0].astype(BF16), ret_w_out[0].astype(BF16), att_w_in[0].astype(BF16), att_w_out[0].astype(BF16),
              (mlp_w1[0].astype(BF16), mlp_w1[1].astype(BF16)), (mlp_w2[0].astype(BF16), mlp_w2[1].astype(BF16)))
    loss_part, grad_x, lands = _step(x, loss_target, shards, mix_norm_g, ret_gn_g, att_rel_bias[0], mlp_norm_g,
                                     final_norm_g.reshape(1, D_MODEL))
    loss = lax.psum(loss_part, ("x", "y", "c"))

    u_rin = _adamw_reduce([lands["rin"]], ret_w_in, m_ret_w_in, v_ret_w_in, "update_ret_w_in")
    u_rout = _adamw_reduce([lands["rout"]], ret_w_out, m_ret_w_out, v_ret_w_out, "update_ret_w_out")
    u_ain = _adamw_reduce([lands["ain"]], att_w_in, m_att_w_in, v_att_w_in, "update_att_w_in")
    u_aout = _adamw_reduce([lands["aout"]], att_w_out, m_att_w_out, v_att_w_out, "update_att_w_out")
    u_w1 = _adamw_reduce(lands["w1"], mlp_w1, m_mlp_w1, v_mlp_w1, "update_mlp_w1")
    u_w2 = _adamw_reduce(lands["w2"], mlp_w2, m_mlp_w2, v_mlp_w2, "update_mlp_w2")

    def pack(mix, mlp, fin, gn):
        return jnp.concatenate([mix, mlp, fin.reshape(1, D_MODEL), gn.reshape(2, D_MODEL), jnp.zeros((1, D_MODEL), F32)], axis=0)

    small_w = pack(mix_norm_g, mlp_norm_g, final_norm_g, ret_gn_g)
    small_m = pack(m_mix_norm_g, m_mlp_norm_g, m_final_norm_g, m_ret_gn_g)
    small_v = pack(v_mix_norm_g, v_mlp_norm_g, v_final_norm_g, v_ret_gn_g)
    sg, sd, sm, sv, rel_sum = _small_update(lands["small"], lands["rel"], small_w, small_m, small_v, "update_small")
    g_rel_mine = lax.dynamic_slice(rel_sum, (0, me * rel_cols), (ATT_HEADS, rel_cols))
    rel_d, rel_m, rel_v = _adamw_plain(att_rel_bias[0], g_rel_mine, m_att_rel_bias[0], v_att_rel_bias[0], "update_rel_bias")
    u_rel = [g_rel_mine[None], rel_d[None], rel_m[None], rel_v[None]]

    def unpack(t):
        return dict(mix=t[0:2], mlp=t[2:4], fin=t[4], gn=t[5:7].reshape(1, RET_VW))

    us = [unpack(t) for t in (sg, sd, sm, sv)]
    outs = [loss, grad_x]
    for k in range(4):
        outs += [us[k]["mix"], u_rin[k], us[k]["gn"], u_rout[k], u_ain[k], u_rel[k], u_aout[k], us[k]["mlp"],
                 u_w1[k], u_w2[k], us[k]["fin"]]
    return tuple(outs)
```

```python
import functools

import numpy as np
import jax
import jax.numpy as jnp
from jax import lax
from jax.experimental import pallas as pl
from jax.experimental.pallas import tpu as pltpu

F32, BF16 = jnp.float32, jnp.bfloat16

D_MODEL = 1024
CHUNK = 64
RET_HEADS, RET_DK, RET_DV = 4, 256, 512
RET_QK, RET_VW = RET_HEADS * RET_DK, RET_HEADS * RET_DV
RET_IN = 2 * RET_QK + 2 * RET_VW
RET_HEAD_COLS = 2 * RET_DK + 2 * RET_DV
RET_SC = 256
ROPE_BASE = 10000.0
ATT_HEADS, ATT_DH = 16, 64
LANES = 128
ATT_PAIRS = ATT_HEADS * ATT_DH // LANES
BAND_PAST = 8 * CHUNK
MAX_REL = 256
REL_TABLE = MAX_REL + CHUNK
REL_PAD = 384
ATT_CQ = 256
ATT_KW = ATT_CQ + BAND_PAST
ATT_RING = ATT_CQ + ATT_KW
ATT_VARIANTS = BAND_PAST // ATT_CQ + 1
D_FF = 4 * D_MODEL
EPS = 1e-6
EPILOGUE_ROWS = 256
NEG = -1e30
N_DEV = 8
N_PEERS = N_DEV - 1

ADAM_LR, ADAM_B1, ADAM_B2, ADAM_EPS, ADAM_WD, ADAM_STEP = 0.001, 0.9, 0.999, 1e-08, 0.01, 10

VMEM_LIMIT = 56 * 1024 * 1024
MESH = pl.DeviceIdType.MESH
ANY = pl.BlockSpec(memory_space=pl.ANY)


def _dot(a, b):
    return jnp.dot(a, b, preferred_element_type=F32)


def _dot_nt(a, b):
    return lax.dot_general(a, b, (((1,), (1,)), ((), ())), preferred_element_type=F32)


def _dot_tn(a, b):
    return lax.dot_general(a, b, (((0,), (0,)), ((), ())), preferred_element_type=F32)


def _rms(x, g):
    r = lax.rsqrt(jnp.mean(x * x, axis=-1, keepdims=True) + EPS)
    return x * r * g


def _rms_bwd(dn, x, g):
    r = lax.rsqrt(jnp.mean(x * x, axis=-1, keepdims=True) + EPS)
    xh = x * r
    dg = jnp.sum(dn * xh, axis=0, keepdims=True)
    dxh = dn * g
    dx = r * (dxh - xh * jnp.mean(dxh * xh, axis=-1, keepdims=True))
    return dx, dg


def _rms_bwd_epilogue(x_ref, g_ref, dres_ref, dx_ref, dxb_ref, dg_ref):
    for r in range(0, dx_ref.shape[0], EPILOGUE_ROWS):
        rows = slice(r, r + EPILOGUE_ROWS)
        dx, dg = _rms_bwd(dx_ref[rows, :], x_ref[rows, :], g_ref[...])
        dx = dres_ref[rows, :] + dx
        dx_ref[rows, :] = dx
        dxb_ref[rows, :] = dx.astype(BF16)
        dg_ref[...] += dg


def _place():
    return lax.axis_index("x"), lax.axis_index("y"), lax.axis_index("c")


def _lin(p):
    return 4 * p[0] + 2 * p[1] + p[2]


def _cols(width):
    return lambda ref, blk: ref.at[:, pl.ds(pl.multiple_of(blk * width, LANES), width)]


def _rows(height):
    return lambda ref, blk: ref.at[pl.ds(pl.multiple_of(blk * height, 8), height), :]


def _slot(ref, blk):
    return ref.at[blk]


def _whole(ref, blk):
    return ref


class _Exchange:
    def __init__(self, sources, flows, land_shapes):
        self.sources, self.flows, self.land_shapes = list(sources), flows, list(land_shapes)
        n_f = len(flows)
        self.sem_shapes = [pltpu.SemaphoreType.DMA((N_PEERS * n_f,)), pltpu.SemaphoreType.DMA((N_PEERS * n_f,)),
                           pltpu.SemaphoreType.DMA((n_f,))]

    def _copies(self, srcs, lands, sems):
        send_sems, recv_sems, local_sems = sems
        x, y, c = _place()
        me = (x, y, c)
        peers = [(x ^ dx, y ^ dy, c ^ dc) for dx in (0, 1) for dy in (0, 1) for dc in (0, 1)][1:]

        def copy(f, k, sender, to):
            si, sview, li, lview = self.flows[f]
            return pltpu.make_async_remote_copy(
                src_ref=sview(srcs[si], _lin(to)), dst_ref=lview(lands[li], _lin(sender)),
                send_sem=send_sems.at[f * N_PEERS + k], recv_sem=recv_sems.at[f * N_PEERS + k],
                device_id=to, device_id_type=MESH)

        mine, sends, recvs = [], [], []
        for f, (si, sview, li, lview) in enumerate(self.flows):
            mine.append(pltpu.make_async_copy(sview(srcs[si], _lin(me)), lview(lands[li], _lin(me)), local_sems.at[f]))
            for k, peer in enumerate(peers):
                sends.append(copy(f, k, me, peer))
                recvs.append(copy(f, k, peer, me))
        return mine, sends, recvs

    def start(self, srcs, lands, sems):
        mine, sends, _ = self._copies(srcs, lands, sems)
        for cp in mine + sends:
            cp.start()

    def finish(self, srcs, lands, sems):
        mine, sends, recvs = self._copies(srcs, lands, sems)
        for cp in recvs:
            cp.wait_recv()
        for cp in sends:
            cp.wait_send()
        for cp in mine:
            cp.wait()

    def run(self, name):
        n_src, n_land = len(self.sources), len(self.land_shapes)

        def body(*refs):
            srcs, lands, sems = refs[:n_src], refs[n_src:n_src + n_land], refs[n_src + n_land:]
            self.start(srcs, lands, sems)
            self.finish(srcs, lands, sems)

        return pl.pallas_call(body, name=name, in_specs=[ANY] * n_src, out_specs=[ANY] * n_land,
                              out_shape=self.land_shapes, scratch_shapes=self.sem_shapes)(*self.sources)


def _call(body, *, name, grid, in_specs, out_specs, out_shape, args, scratch_shapes=(), ride=None):
    params = pltpu.CompilerParams(dimension_semantics=("arbitrary",) * len(grid), vmem_limit_bytes=VMEM_LIMIT)
    in_specs, out_specs, out_shape, scratch_shapes = list(in_specs), list(out_specs), list(out_shape), list(scratch_shapes)
    if ride is None:
        return pl.pallas_call(body, name=name, grid=grid, in_specs=in_specs, out_specs=out_specs, out_shape=out_shape,
                              scratch_shapes=scratch_shapes, compiler_params=params)(*args)
    n_in, n_out, n_scr = len(in_specs), len(out_specs), len(scratch_shapes)
    n_src, n_land = len(ride.sources), len(ride.land_shapes)

    def riding(*refs):
        bounds = np.cumsum([n_in, n_src, n_out, n_land, n_scr])
        ins, srcs, outs, lands, scr, sems = (refs[a:b] for a, b in zip([0, *bounds], [*bounds, len(refs)]))
        ids = [pl.program_id(d) for d in range(len(grid))]
        first = functools.reduce(lambda a, b: a & b, [i == 0 for i in ids])
        last = functools.reduce(lambda a, b: a & b, [i == n - 1 for i, n in zip(ids, grid)])

        @pl.when(first)
        def _():
            ride.start(srcs, lands, sems)

        body(*ins, *outs, *scr)

        @pl.when(last)
        def _():
            ride.finish(srcs, lands, sems)

    res = pl.pallas_call(
        riding, name=name, grid=grid, in_specs=in_specs + [ANY] * n_src, out_specs=out_specs + [ANY] * n_land,
        out_shape=out_shape + ride.land_shapes, scratch_shapes=scratch_shapes + ride.sem_shapes,
        compiler_params=params)(*args, *ride.sources)
    return res[:n_out], res[n_out:]


def _all_gather(shards, out_shapes, views, name):
    n = len(shards)

    def body(*refs):
        ins, outs = refs[:n], refs[n:2 * n]
        send_sems, recv_sems, local_sems = refs[2 * n:]
        x, y, c = _place()
        me, sibling = (x, y, c), (x, y, 1 - c)
        chips = [(1 - x, y), (x, 1 - y), (1 - x, 1 - y)]

        def copy(a, k, block, to, src=None):
            dst = views[a](outs[a], _lin(block))
            return pltpu.make_async_remote_copy(
                src_ref=dst if src is None else src, dst_ref=dst,
                send_sem=send_sems.at[a * N_PEERS + k], recv_sem=recv_sems.at[a * N_PEERS + k],
                device_id=to, device_id_type=MESH)

        mine = [pltpu.make_async_copy(ins[a], views[a](outs[a], _lin(me)), local_sems.at[a]) for a in range(n)]
        for cp in mine:
            cp.start()
        first = []
        for a in range(n):
            first.append(copy(a, 0, me, sibling, src=ins[a]))
            first += [copy(a, 1 + j, me, (*chip, c), src=ins[a]) for j, chip in enumerate(chips)]
        for cp in first:
            cp.start()
        passed = []
        for j, chip in enumerate(chips):
            for a in range(n):
                copy(a, 1 + j, (*chip, c), me).wait_recv()
                fwd = copy(a, 4 + j, (*chip, c), sibling)
                fwd.start()
                passed.append(fwd)
        for a in range(n):
            copy(a, 0, sibling, me).wait_recv()
            for j, chip in enumerate(chips):
                copy(a, 4 + j, (*chip, 1 - c), me).wait_recv()
        for cp in first + passed:
            cp.wait_send()
        for cp in mine:
            cp.wait()

    return pl.pallas_call(
        body, name=name,
        in_specs=[ANY] * n, out_specs=[ANY] * n, out_shape=out_shapes,
        scratch_shapes=[pltpu.SemaphoreType.DMA((N_PEERS * n,)), pltpu.SemaphoreType.DMA((N_PEERS * n,)),
                        pltpu.SemaphoreType.DMA((n,))],
    )(*shards)


def _norm_proj(h, g, w, pieces, n_steps, out_shape, out_spec, name, tm=1024, ride=None, stacked=False):
    T = h.shape[0]
    n_p = len(pieces)

    def body(h_ref, g_ref, *rest):
        w_refs = rest[:n_p]
        o_ref, n_ref = rest[n_p:]

        @pl.when(pl.program_id(1) == 0)
        def _():
            n_ref[...] = _rms(h_ref[...], g_ref[...]).astype(BF16)

        n = n_ref[...]
        for p, ((width, _, start), w_ref) in enumerate(zip(pieces, w_refs)):
            out = _dot(n, w_ref[...]).astype(o_ref.dtype)
            if stacked:
                o_ref[p] = out
            else:
                o_ref[:, start:start + width] = out

    w_specs = [pl.BlockSpec((D_MODEL, width), functools.partial(lambda i, j, f: (0, f(j)), f=col))
               for (width, col, _) in pieces]
    row = pl.BlockSpec((tm, D_MODEL), lambda i, j: (i, 0))
    return _call(body, name=name, grid=(T // tm, n_steps),
                 in_specs=[row, pl.BlockSpec((1, D_MODEL), lambda i, j: (0, 0))] + w_specs,
                 out_specs=[out_spec(tm), row], out_shape=[out_shape, jax.ShapeDtypeStruct((T, D_MODEL), BF16)],
                 args=(h, g, *([w] * n_p)), ride=ride)


def _proj_bwd(dy, dy_spec, w, pieces, n_steps, x, g, dres, name, tm=1024, ride=None):
    T = x.shape[0]
    n_p = len(pieces)

    def body(dy_ref, *rest):
        w_refs = rest[:n_p]
        x_ref, g_ref, dres_ref, dx_ref, dxb_ref, dg_ref = rest[n_p:]
        acc = dx_ref
        i, j = pl.program_id(0), pl.program_id(1)

        @pl.when(j == 0)
        def _():
            acc[...] = jnp.zeros_like(acc)

        @pl.when((i == 0) & (j == 0))
        def _():
            dg_ref[...] = jnp.zeros_like(dg_ref)

        for (width, _, start), w_ref in zip(pieces, w_refs):
            acc[...] += _dot_nt(dy_ref[:, start:start + width], w_ref[...])

        @pl.when(j == n_steps - 1)
        def _():
            _rms_bwd_epilogue(x_ref, g_ref, dres_ref, dx_ref, dxb_ref, dg_ref)

    w_specs = [pl.BlockSpec((D_MODEL, width), functools.partial(lambda i, j, f: (0, f(j)), f=col))
               for (width, col, _) in pieces]
    row = pl.BlockSpec((tm, D_MODEL), lambda i, j: (i, 0))
    vec = pl.BlockSpec((1, D_MODEL), lambda i, j: (0, 0))
    return _call(body, name=name, grid=(T // tm, n_steps),
                 in_specs=[dy_spec(tm)] + w_specs + [row, vec, row], out_specs=[row, row, vec],
                 out_shape=[jax.ShapeDtypeStruct((T, D_MODEL), F32), jax.ShapeDtypeStruct((T, D_MODEL), BF16),
                            jax.ShapeDtypeStruct((1, D_MODEL), F32)],
                 args=(dy, *([w] * n_p), x, g, dres), ride=ride)


def _matmul_res(a, w, res, name, tm=1024):
    T, K = a.shape

    def body(a_ref, w_ref, r_ref, o_ref):
        o_ref[...] = r_ref[...] + _dot(a_ref[...], w_ref[...])

    row = pl.BlockSpec((tm, D_MODEL), lambda i: (i, 0))
    return _call(body, name=name, grid=(T // tm,),
                 in_specs=[pl.BlockSpec((tm, K), lambda i: (i, 0)), pl.BlockSpec((K, D_MODEL), lambda i: (0, 0)), row],
                 out_specs=[row], out_shape=[jax.ShapeDtypeStruct((T, D_MODEL), F32)], args=(a, w, res))[0]


def _matmul_nt(dy, w, name, tm=1024):
    T, N = dy.shape
    K = w.shape[0]

    def body(dy_ref, w_ref, o_ref):
        o_ref[...] = _dot_nt(dy_ref[...], w_ref[...]).astype(BF16)

    return _call(body, name=name, grid=(T // tm,),
                 in_specs=[pl.BlockSpec((tm, N), lambda i: (i, 0)), pl.BlockSpec((K, N), lambda i: (0, 0))],
                 out_specs=[pl.BlockSpec((tm, K), lambda i: (i, 0))], out_shape=[jax.ShapeDtypeStruct((T, K), BF16)],
                 args=(dy, w))[0]


def _wgrad(a, a_spec, b, b_specs, m, n, bm, bn, n_k, name, square_a=False):
    b_specs = b_specs if isinstance(b_specs, (list, tuple)) else [b_specs]
    n_b = len(b_specs)

    def body(a_ref, *rest):
        b_refs = rest[:n_b]
        o_ref, acc = rest[n_b:]
        k = pl.program_id(2)

        @pl.when(k == 0)
        def _():
            acc[...] = jnp.zeros_like(acc)

        av = a_ref[...]
        if square_a:
            af = av.astype(F32)
            av = (af * af).astype(BF16)
        bv = b_refs[0][...] if n_b == 1 else jnp.concatenate([r[...] for r in b_refs], axis=1)
        acc[...] += _dot_tn(av, bv)

        @pl.when(k == n_k - 1)
        def _():
            o_ref[...] = acc[...].astype(BF16)

    return _call(body, name=name, grid=(m // bm, n // bn, n_k), in_specs=[a_spec, *b_specs],
                 out_specs=[pl.BlockSpec((bm, bn), lambda i, j, k: (i, j))], out_shape=[jax.ShapeDtypeStruct((m, n), BF16)],
                 scratch_shapes=[pltpu.VMEM((bm, bn), F32)], args=(a, *([b] * n_b)))[0]


def _mlp_fwd(h, g, w1, w2, name, tm=1024, tf=1024, ride=None):
    T = h.shape[0]

    def body(h_ref, g_ref, w1_ref, w2_ref, ho_ref, n_ref, u_ref):
        @pl.when(pl.program_id(1) == 0)
        def _():
            n_ref[...] = _rms(h_ref[...], g_ref[...]).astype(BF16)
            ho_ref[...] = h_ref[...]

        u = jnp.maximum(_dot(n_ref[...], w1_ref[...]), 0.0)
        u_ref[...] = u.astype(BF16)
        ho_ref[...] += _dot((u * u).astype(BF16), w2_ref[...])

    row = pl.BlockSpec((tm, D_MODEL), lambda i, j: (i, 0))
    return _call(body, name=name, grid=(T // tm, D_FF // tf),
                 in_specs=[row, pl.BlockSpec((1, D_MODEL), lambda i, j: (0, 0)),
                           pl.BlockSpec((D_MODEL, tf), lambda i, j: (0, j)), pl.BlockSpec((tf, D_MODEL), lambda i, j: (j, 0))],
                 out_specs=[row, row, pl.BlockSpec((tm, tf), lambda i, j: (i, j))],
                 out_shape=[jax.ShapeDtypeStruct((T, D_MODEL), F32), jax.ShapeDtypeStruct((T, D_MODEL), BF16),
                            jax.ShapeDtypeStruct((T, D_FF), BF16)],
                 args=(h, g, w1, w2), ride=ride)


def _mlp_bwd(dh, u, w1, w2, h, g, name, tm=1024, tf=1024, ride=None):
    T = h.shape[0]
    n_j = D_FF // tf

    def body(dh_ref, u_ref, w1_ref, w2_ref, h_ref, g_ref, dx_ref, dxb_ref, du_ref, dg_ref, dhb):
        acc = dx_ref
        i, j = pl.program_id(0), pl.program_id(1)

        @pl.when(j == 0)
        def _():
            dhb[...] = dh_ref[...].astype(BF16)
            acc[...] = jnp.zeros_like(acc)

        @pl.when((i == 0) & (j == 0))
        def _():
            dg_ref[...] = jnp.zeros_like(dg_ref)

        da = _dot_nt(dhb[...], w2_ref[...])
        du = (da * (2.0 * u_ref[...].astype(F32))).astype(BF16)
        du_ref[...] = du
        acc[...] += _dot_nt(du, w1_ref[...])

        @pl.when(j == n_j - 1)
        def _():
            _rms_bwd_epilogue(h_ref, g_ref, dh_ref, dx_ref, dxb_ref, dg_ref)

    row = pl.BlockSpec((tm, D_MODEL), lambda i, j: (i, 0))
    vec = pl.BlockSpec((1, D_MODEL), lambda i, j: (0, 0))
    hid = pl.BlockSpec((tm, tf), lambda i, j: (i, j))
    return _call(body, name=name, grid=(T // tm, n_j),
                 in_specs=[row, hid, pl.BlockSpec((D_MODEL, tf), lambda i, j: (0, j)),
                           pl.BlockSpec((tf, D_MODEL), lambda i, j: (j, 0)), row, vec],
                 out_specs=[row, row, hid, vec],
                 out_shape=[jax.ShapeDtypeStruct((T, D_MODEL), F32), jax.ShapeDtypeStruct((T, D_MODEL), BF16),
                            jax.ShapeDtypeStruct((T, D_FF), BF16), jax.ShapeDtypeStruct((1, D_MODEL), F32)],
                 scratch_shapes=[pltpu.VMEM((tm, D_MODEL), BF16)], args=(dh, u, w1, w2, h, g), ride=ride)


def _final_loss(h, g, target, name, tm=512):
    T = h.shape[0]

    def body(h_ref, g_ref, t_ref, dh_ref, dhb_ref, loss_ref, dg_ref):
        @pl.when(pl.program_id(0) == 0)
        def _():
            loss_ref[...] = jnp.zeros_like(loss_ref)
            dg_ref[...] = jnp.zeros_like(dg_ref)

        x = h_ref[...]
        gg = g_ref[...]
        r = lax.rsqrt(jnp.mean(x * x, axis=-1, keepdims=True) + EPS)
        xh = x * r
        e = xh * gg - t_ref[...]
        per_tok = jnp.mean(e * e, axis=-1, keepdims=True)
        loss_ref[...] += 0.5 * jnp.sum(per_tok, axis=0, keepdims=True)
        dy = e * (1.0 / D_MODEL)
        dg_ref[...] += jnp.sum(dy * xh, axis=0, keepdims=True)
        dxh = dy * gg
        dx = r * (dxh - xh * jnp.mean(dxh * xh, axis=-1, keepdims=True))
        dh_ref[...] = dx
        dhb_ref[...] = dx.astype(BF16)

    row = pl.BlockSpec((tm, D_MODEL), lambda i: (i, 0))
    vec = pl.BlockSpec((1, D_MODEL), lambda i: (0, 0))
    return _call(body, name=name, grid=(T // tm,), in_specs=[row, vec, row],
                 out_specs=[row, row, pl.BlockSpec((8, LANES), lambda i: (0, 0)), vec],
                 out_shape=[jax.ShapeDtypeStruct((T, D_MODEL), F32), jax.ShapeDtypeStruct((T, D_MODEL), BF16),
                            jax.ShapeDtypeStruct((8, LANES), F32), jax.ShapeDtypeStruct((1, D_MODEL), F32)],
                 args=(h, g, target))


def _ret_constants(S):
    log_gamma = jnp.log1p(-jnp.exp2(-5.0 - jnp.arange(RET_HEADS, dtype=F32)))
    idx = jnp.arange(RET_SC, dtype=F32)
    i, j = idx[:, None], idx[None, :]
    same_chunk = jnp.floor(i / CHUNK) == jnp.floor(j / CHUNK)
    mask = jnp.where((j <= i) | same_chunk, jnp.exp(log_gamma[:, None, None] * jnp.abs(i - j)[None]), 0.0)
    qdec = jnp.exp(log_gamma[:, None] * (idx + 1.0)[None, :])[:, :, None]
    kdec = jnp.exp(log_gamma[:, None] * (RET_SC - 1 - idx)[None, :])[:, :, None]
    cdec = jnp.exp(log_gamma * RET_SC)[:, None, None]
    half = RET_DK // 2
    inv = jnp.exp(-jnp.log(ROPE_BASE) * jnp.arange(half, dtype=F32) / half)
    ang = jnp.arange(S, dtype=F32)[:, None] * inv[None, :]
    return jnp.cos(ang), jnp.sin(ang), mask.astype(F32), qdec, kdec, cdec


def _rope(t, cs, sn):
    t1, t2 = t[:, :RET_DK // 2], t[:, RET_DK // 2:]
    return jnp.concatenate([t1 * cs - t2 * sn, t1 * sn + t2 * cs], axis=-1)


def _rope_bwd(d, cs, sn):
    d1, d2 = d[:, :RET_DK // 2], d[:, RET_DK // 2:]
    return jnp.concatenate([d1 * cs + d2 * sn, d2 * cs - d1 * sn], axis=-1)


def _ret_specs(B, S, reverse):
    n_sc = S // RET_SC

    def cc(c):
        return n_sc - 1 - c if reverse else c

    return dict(
        proj=pl.BlockSpec((B, RET_SC, RET_HEAD_COLS), lambda h, c: (0, cc(c), h)),
        trig=pl.BlockSpec((RET_SC, RET_DK // 2), lambda h, c: (cc(c), 0)),
        mask=pl.BlockSpec((None, RET_SC, RET_SC), lambda h, c: (h, 0, 0)),
        dec=pl.BlockSpec((None, RET_SC, 1), lambda h, c: (h, 0, 0)),
        cdec=pl.BlockSpec((None, 1, 1), lambda h, c: (h, 0, 0)),
        gn=pl.BlockSpec((1, RET_DV), lambda h, c: (0, h)),
        val=pl.BlockSpec((B, RET_SC, RET_DV), lambda h, c: (0, cc(c), h)),
        state=pl.BlockSpec((B, None, None, RET_DK, RET_DV), lambda h, c: (0, h, cc(c), 0, 0)),
    )


def _ret_qkvg(p_ref, cs, sn):
    q = _rope(p_ref[:, 0:RET_DK].astype(F32), cs, sn)
    k = _rope(p_ref[:, RET_DK:2 * RET_DK].astype(F32), cs, sn) * (RET_DK ** -0.5)
    v = p_ref[:, 2 * RET_DK:2 * RET_DK + RET_DV]
    gate = p_ref[:, 2 * RET_DK + RET_DV:RET_HEAD_COLS].astype(F32)
    return q, k, v, gate


def _group_norm(o):
    mu = jnp.mean(o, axis=-1, keepdims=True)
    oc = o - mu
    rstd = lax.rsqrt(jnp.mean(oc * oc, axis=-1, keepdims=True) + EPS)
    return oc * rstd, rstd


def _ret_fwd(proj, consts, gn, B, S, name, ride=None):
    T = B * S
    n_sc = S // RET_SC
    sp = _ret_specs(B, S, False)

    def body(p_ref, cos_ref, sin_ref, m_ref, qd_ref, kd_ref, cd_ref, gn_ref, y_ref, o_ref, st_ref, state):
        @pl.when(pl.program_id(1) == 0)
        def _():
            state[...] = jnp.zeros_like(state)

        seqs = range(B)
        qkvg = [_ret_qkvg(p_ref.at[b], cos_ref[...], sin_ref[...]) for b in seqs]
        qb = [qkvg[b][0].astype(BF16) for b in seqs]
        kb = [qkvg[b][1].astype(BF16) for b in seqs]
        kt = [(qkvg[b][1] * kd_ref[...]).astype(BF16) for b in seqs]
        v = [qkvg[b][2] for b in seqs]
        sc = [_dot_nt(qb[b], kb[b]) for b in seqs]
        sb = [state[b].astype(BF16) for b in seqs]
        cross = [_dot(qb[b], sb[b]) for b in seqs]
        for b in seqs:
            st_ref[b] = sb[b]
        p = [(sc[b] * m_ref[...]).astype(BF16) for b in seqs]
        o = [_dot(p[b], v[b]) + cross[b] * qd_ref[...] for b in seqs]
        upd = [_dot_tn(kt[b], v[b]) for b in seqs]
        for b in seqs:
            o_ref[b] = o[b]
            ohat, _ = _group_norm(o[b])
            gate = qkvg[b][3]
            y_ref[b] = (gate * jax.nn.sigmoid(gate) * (ohat * gn_ref[...])).astype(BF16)
            state[b] = state[b] * cd_ref[...] + upd[b]

    res = _call(
        body, name=name, grid=(RET_HEADS, n_sc),
        in_specs=[sp["proj"], sp["trig"], sp["trig"], sp["mask"], sp["dec"], sp["dec"], sp["cdec"], sp["gn"]],
        out_specs=[sp["val"], sp["val"], sp["state"]],
        out_shape=[jax.ShapeDtypeStruct((B, S, RET_VW), BF16), jax.ShapeDtypeStruct((B, S, RET_VW), F32),
                   jax.ShapeDtypeStruct((B, RET_HEADS, n_sc, RET_DK, RET_DV), BF16)],
        scratch_shapes=[pltpu.VMEM((B, RET_DK, RET_DV), F32)],
        args=(proj.reshape(B, S, -1), *consts, gn), ride=ride)
    (y, o, states), lands = res if ride is not None else (res, None)
    outs = [y.reshape(T, RET_VW), o.reshape(T, RET_VW), states]
    return outs if ride is None else (outs, lands)


def _ret_bwd(proj, consts, gn, o, states, dy, B, S, name, ride=None):
    T = B * S
    n_sc = S // RET_SC
    sp = _ret_specs(B, S, True)

    def body(p_ref, cos_ref, sin_ref, m_ref, qd_ref, kd_ref, cd_ref, gn_ref, o_ref, st_ref, dy_ref,
             dp_ref, dgn_ref, dstate):
        @pl.when(pl.program_id(1) == 0)
        def _():
            dstate[...] = jnp.zeros_like(dstate)
            dgn_ref[...] = jnp.zeros_like(dgn_ref)

        seqs = range(B)
        cs, sn = cos_ref[...], sin_ref[...]
        m, gnv = m_ref[...], gn_ref[...]
        qkvg = [_ret_qkvg(p_ref.at[b], cs, sn) for b in seqs]
        qb = [qkvg[b][0].astype(BF16) for b in seqs]
        kb = [qkvg[b][1].astype(BF16) for b in seqs]
        kt = [(qkvg[b][1] * kd_ref[...]).astype(BF16) for b in seqs]
        v = [qkvg[b][2] for b in seqs]
        sc = [_dot_nt(qb[b], kb[b]) for b in seqs]
        dsb = [dstate[b].astype(BF16) for b in seqs]
        dv_state = [_dot(kt[b], dsb[b]) for b in seqs]
        dkt = [_dot_nt(v[b], dsb[b]) for b in seqs]
        do, dgate = [], []
        for b in seqs:
            gate = qkvg[b][3]
            ohat, rstd = _group_norm(o_ref[b])
            dyv = dy_ref[b].astype(F32)
            sg = jax.nn.sigmoid(gate)
            don = dyv * (gate * sg)
            dgate.append(dyv * (ohat * gnv) * (sg * (1.0 + gate * (1.0 - sg))))
            dgn_ref[...] += jnp.sum(don * ohat, axis=0, keepdims=True)
            dohat = don * gnv
            do.append(rstd * (dohat - jnp.mean(dohat, axis=-1, keepdims=True)
                              - ohat * jnp.mean(dohat * ohat, axis=-1, keepdims=True)))
        dob = [do[b].astype(BF16) for b in seqs]
        doq = [(do[b] * qd_ref[...]).astype(BF16) for b in seqs]
        dsc_f = [_dot_nt(dob[b], v[b]) for b in seqs]
        dq_state = [_dot_nt(doq[b], st_ref[b]) for b in seqs]
        dstate_upd = [_dot_tn(qb[b], doq[b]) for b in seqs]
        p = [(sc[b] * m).astype(BF16) for b in seqs]
        dsc = [(dsc_f[b] * m).astype(BF16) for b in seqs]
        dv = [_dot_tn(p[b], dob[b]) + dv_state[b] for b in seqs]
        dq = [_dot(dsc[b], kb[b]) + dq_state[b] for b in seqs]
        dk = [(_dot_tn(dsc[b], qb[b]) + dkt[b] * kd_ref[...]) * (RET_DK ** -0.5) for b in seqs]
        for b in seqs:
            dstate[b] = dstate[b] * cd_ref[...] + dstate_upd[b]
            dp_ref[b, :, 0:RET_DK] = _rope_bwd(dq[b], cs, sn).astype(BF16)
            dp_ref[b, :, RET_DK:2 * RET_DK] = _rope_bwd(dk[b], cs, sn).astype(BF16)
            dp_ref[b, :, 2 * RET_DK:2 * RET_DK + RET_DV] = dv[b].astype(BF16)
            dp_ref[b, :, 2 * RET_DK + RET_DV:RET_HEAD_COLS] = dgate[b].astype(BF16)

    res = _call(
        body, name=name, grid=(RET_HEADS, n_sc),
        in_specs=[sp["proj"], sp["trig"], sp["trig"], sp["mask"], sp["dec"], sp["dec"], sp["cdec"], sp["gn"],
                  sp["val"], sp["state"], sp["val"]],
        out_specs=[sp["proj"], sp["gn"]],
        out_shape=[jax.ShapeDtypeStruct((B, S, RET_HEADS * RET_HEAD_COLS), BF16), jax.ShapeDtypeStruct((1, RET_VW), F32)],
        scratch_shapes=[pltpu.VMEM((B, RET_DK, RET_DV), F32)],
        args=(proj.reshape(B, S, -1), *consts, gn, o.reshape(B, S, -1), states, dy.reshape(B, S, -1)), ride=ride)
    (dproj, dgn), lands = res if ride is not None else (res, None)
    outs = [dproj.reshape(T, -1), dgn]
    return outs if ride is None else (outs, lands)


def _ring_index():
    u = np.arange(ATT_RING)
    offset = np.where(u < ATT_KW, u, u - ATT_RING)
    return np.clip(offset - BAND_PAST, -MAX_REL, CHUNK - 1) + MAX_REL


def _bias_ring(rel):
    n_clip = BAND_PAST - MAX_REL
    n_hi = ATT_KW - n_clip - REL_TABLE
    ring = jnp.concatenate([jnp.broadcast_to(rel[:, :1], (ATT_HEADS, n_clip)), rel,
                            jnp.broadcast_to(rel[:, -1:], (ATT_HEADS, n_hi)),
                            jnp.broadcast_to(rel[:, :1], (ATT_HEADS, ATT_CQ))], axis=1)
    return jnp.broadcast_to(ring.reshape(ATT_PAIRS, 2, 1, ATT_RING), (ATT_PAIRS, 2, 8, ATT_RING))


def _band_masks():
    i = np.arange(ATT_CQ)[:, None]
    j = np.arange(ATT_KW)[None, :]
    lo = CHUNK * (i // CHUNK)
    band = np.where((j >= lo) & (j < lo + BAND_PAST + CHUNK), 0.0, NEG)
    return np.stack([band + np.where(j + v * ATT_CQ >= BAND_PAST, 0.0, NEG)
                     for v in range(ATT_VARIANTS)]).astype(np.float32)


def _attn_bias(bias_scr, ring_ref, band_ref):
    for hh in range(2):
        rows = jnp.broadcast_to(ring_ref[hh, 0:1, :], (ATT_CQ, ATT_RING))
        toeplitz = pltpu.roll(rows, 0, 1, stride=1, stride_axis=0)[:, :ATT_KW]
        for v in range(ATT_VARIANTS):
            bias_scr[v, hh] = toeplitz + band_ref[v]


ATT_STRIP = 32
ATT_SCALE = ATT_DH ** -0.5


def _strips(fn):
    def strip(r, carry):
        fn(pl.ds(pl.multiple_of(r * ATT_STRIP, ATT_STRIP), ATT_STRIP))
        return carry

    lax.fori_loop(0, ATT_CQ // ATT_STRIP, strip, 0, unroll=True)


def _attn_prepare(kpad, vpad, qm, qkv_ref):
    kpad[0:BAND_PAST, :] = jnp.zeros((BAND_PAST, LANES), BF16)
    vpad[0:BAND_PAST, :] = jnp.zeros((BAND_PAST, LANES), BF16)
    kpad[BAND_PAST:, :] = qkv_ref[1]
    vpad[BAND_PAST:, :] = qkv_ref[2]
    lane = lax.broadcasted_iota(jnp.int32, (1, LANES), 1)
    q = qkv_ref[0] * ATT_SCALE
    for hh in range(2):
        qm[hh] = jnp.where((lane >= ATT_DH * hh) & (lane < ATT_DH * (hh + 1)), q, jnp.zeros_like(q))


def _attn_scores(s_ref, qm, kpad, b_ref, t, n_qb):
    t = jnp.minimum(t, n_qb - 1)
    qs = pl.multiple_of(t * ATT_CQ, ATT_CQ)
    variant = jnp.minimum(t, ATT_VARIANTS - 1)
    kw = kpad[pl.ds(qs, ATT_KW), :]
    for hh in range(2):
        s_ref[hh] = _dot_nt(qm[hh, pl.ds(qs, ATT_CQ), :], kw) + b_ref[variant, hh]


def _attn_softmax(s_ref, e_ref, linv_ref, m_ref=None):
    def strip(rows):
        s = s_ref[rows, :]
        m = jnp.max(s, axis=-1, keepdims=True)
        e = jnp.exp(s - m)
        e_ref[rows, :] = e.astype(BF16)
        linv_ref[rows, :] = jnp.broadcast_to(1.0 / jnp.sum(e, axis=-1, keepdims=True), (ATT_STRIP, LANES))
        if m_ref is not None:
            m_ref[rows, :] = jnp.broadcast_to(m, (ATT_STRIP, LANES))

    _strips(strip)


_RING_SPEC = pl.BlockSpec((None, 2, 8, ATT_RING), lambda hp, b: (hp, 0, 0, 0))
_BAND_SPEC = pl.BlockSpec((ATT_VARIANTS, ATT_CQ, ATT_KW), lambda hp, b: (0, 0, 0))


def _attn_fwd(qkv3, ring, B, S, name, ride=None):
    T = B * S
    n_qb = S // ATT_CQ

    def body(qkv_ref, ring_ref, band_ref, o_ref, kpad, vpad, qm, b_ref, s_scr, e_scr, linv_scr):
        @pl.when(pl.program_id(1) == 0)
        def _():
            _attn_bias(b_ref, ring_ref, band_ref)

        _attn_prepare(kpad, vpad, qm, qkv_ref)
        e_scr[...] = jnp.zeros_like(e_scr)
        linv_scr[...] = jnp.zeros_like(linv_scr)
        lane = lax.broadcasted_iota(jnp.int32, (1, LANES), 1)

        def softmax(slot):
            for hh in range(2):
                _attn_softmax(s_scr.at[slot, hh], e_scr.at[slot, hh], linv_scr.at[slot, hh])

        def output(t, slot):
            qs = pl.multiple_of(jnp.maximum(t, 0) * ATT_CQ, ATT_CQ)
            vw = vpad[pl.ds(qs, ATT_KW), :]
            outs = [_dot(e_scr[slot, hh], vw) * linv_scr[slot, hh] for hh in range(2)]
            o_ref[pl.ds(qs, ATT_CQ), :] = jnp.where(lane < ATT_DH, outs[0], outs[1]).astype(BF16)

        def pair(u, carry):
            t = 2 * u
            _attn_scores(s_scr.at[1], qm, kpad, b_ref, t + 1, n_qb)
            softmax(0)
            output(t - 1, 1)
            _attn_scores(s_scr.at[0], qm, kpad, b_ref, t + 2, n_qb)
            softmax(1)
            output(t, 0)
            return carry

        _attn_scores(s_scr.at[0], qm, kpad, b_ref, 0, n_qb)
        lax.fori_loop(0, n_qb // 2, pair, 0)
        output(n_qb - 1, 1)

    return _call(body, name=name, grid=(ATT_PAIRS, B),
                 in_specs=[pl.BlockSpec((3, S, LANES), lambda hp, b: (0, b, hp)), _RING_SPEC, _BAND_SPEC],
                 out_specs=[pl.BlockSpec((S, LANES), lambda hp, b: (b, hp))],
                 out_shape=[jax.ShapeDtypeStruct((T, D_MODEL), BF16)],
                 scratch_shapes=[pltpu.VMEM((S + BAND_PAST, LANES), BF16), pltpu.VMEM((S + BAND_PAST, LANES), BF16),
                                 pltpu.VMEM((2, S, LANES), BF16), pltpu.VMEM((ATT_VARIANTS, 2, ATT_CQ, ATT_KW), F32),
                                 pltpu.VMEM((2, 2, ATT_CQ, ATT_KW), F32),
                                 pltpu.VMEM((2, 2, ATT_CQ, ATT_KW), BF16), pltpu.VMEM((2, 2, ATT_CQ, LANES), F32)],
                 args=(qkv3, ring, jnp.asarray(_band_masks())), ride=ride)


def _attn_bwd(qkv3, ring, do, B, S, name, ride=None):
    T = B * S
    n_qb = S // ATT_CQ

    def body(qkv_ref, ring_ref, band_ref, do_ref, dqkv_ref, dring_ref, kpad, vpad, qm, dkacc, dvacc, b_ref, db_ref):
        @pl.when(pl.program_id(1) == 0)
        def _():
            _attn_bias(b_ref, ring_ref, band_ref)
            db_ref[...] = jnp.zeros_like(db_ref)

        _attn_prepare(kpad, vpad, qm, qkv_ref)
        dkacc[...] = jnp.zeros_like(dkacc)
        dvacc[...] = jnp.zeros_like(dvacc)
        lane = lax.broadcasted_iota(jnp.int32, (1, LANES), 1)

        def step(qb, carry):
            qs = pl.multiple_of(qb * ATT_CQ, ATT_CQ)
            variant = jnp.minimum(qb, ATT_VARIANTS - 1)
            dov = do_ref[pl.ds(qs, ATT_CQ), :].astype(F32)
            kw = kpad[pl.ds(qs, ATT_KW), :]
            vw = vpad[pl.ds(qs, ATT_KW), :]
            heads = (0, 1)
            qmh = [qm[hh, pl.ds(qs, ATT_CQ), :] for hh in heads]
            s = [_dot_nt(qmh[hh], kw) + b_ref[variant, hh] for hh in heads]
            e, linv, dom, dp = [None, None], [None, None], [None, None], [None, None]
            for hh in heads:
                e[hh] = jnp.exp(s[hh] - jnp.max(s[hh], axis=-1, keepdims=True))
                linv[hh] = 1.0 / jnp.sum(e[hh], axis=-1, keepdims=True)
                sel = (lane >= ATT_DH * hh) & (lane < ATT_DH * (hh + 1))
                dom[hh] = jnp.where(sel, dov * linv[hh], 0.0).astype(BF16)
                dp[hh] = _dot_nt(dom[hh], vw)
            dqs, dk, dv = [], None, None
            for hh in heads:
                ds = e[hh] * (dp[hh] - jnp.sum(dp[hh] * e[hh], axis=-1, keepdims=True) * linv[hh])
                db_ref[hh] += ds
                dsb = ds.astype(BF16)
                dqs.append(_dot(dsb, kw) * ATT_SCALE)
                dk_h = _dot_tn(qmh[hh], dsb)
                dv_h = _dot_tn(dom[hh], e[hh].astype(BF16))
                dk = dk_h if dk is None else dk + dk_h
                dv = dv_h if dv is None else dv + dv_h
            dqkv_ref[0, pl.ds(qs, ATT_CQ), :] = jnp.where(lane < ATT_DH, dqs[0], dqs[1]).astype(BF16)
            dkacc[:, pl.ds(qs, ATT_KW)] += dk
            dvacc[:, pl.ds(qs, ATT_KW)] += dv
            return carry

        lax.fori_loop(0, n_qb, step, 0)
        dqkv_ref[1] = dkacc[:, BAND_PAST:].T.astype(BF16)
        dqkv_ref[2] = dvacc[:, BAND_PAST:].T.astype(BF16)

        @pl.when(pl.program_id(1) == B - 1)
        def _():
            r = lax.broadcasted_iota(jnp.int32, (ATT_CQ, ATT_CQ), 0)
            c = lax.broadcasted_iota(jnp.int32, (ATT_CQ, ATT_CQ), 1)
            reverse = jnp.where(r + c == ATT_CQ - 1, 1.0, 0.0).astype(BF16)
            for hh in range(2):
                x = jnp.concatenate([db_ref[hh], jnp.zeros((ATT_CQ, ATT_RING - ATT_KW), F32)], axis=1)
                flipped = jnp.zeros((ATT_CQ, ATT_RING), F32)
                for _ in range(3):
                    part = x.astype(BF16)
                    flipped = flipped + _dot(reverse, part)
                    x = x - part.astype(F32)
                aligned = pltpu.roll(flipped, ATT_KW + 1, 1, stride=1, stride_axis=0)
                dring_ref[hh] = jnp.sum(aligned.reshape(ATT_CQ // 8, 8, ATT_RING), axis=0)

    qkv_spec = pl.BlockSpec((3, S, LANES), lambda hp, b: (0, b, hp))
    return _call(body, name=name, grid=(ATT_PAIRS, B),
                 in_specs=[qkv_spec, _RING_SPEC, _BAND_SPEC, pl.BlockSpec((S, LANES), lambda hp, b: (b, hp))],
                 out_specs=[qkv_spec, _RING_SPEC],
                 out_shape=[jax.ShapeDtypeStruct((3, T, D_MODEL), BF16),
                            jax.ShapeDtypeStruct((ATT_PAIRS, 2, 8, ATT_RING), F32)],
                 scratch_shapes=[pltpu.VMEM((S + BAND_PAST, LANES), BF16), pltpu.VMEM((S + BAND_PAST, LANES), BF16),
                                 pltpu.VMEM((2, S, LANES), BF16),
                                 pltpu.VMEM((LANES, S + BAND_PAST), F32), pltpu.VMEM((LANES, S + BAND_PAST), F32),
                                 pltpu.VMEM((ATT_VARIANTS, 2, ATT_CQ, ATT_KW), F32), pltpu.VMEM((2, ATT_CQ, ATT_KW), F32)],
                 args=(qkv3, ring, jnp.asarray(_band_masks()), do), ride=ride)


def _rel_bias_grad(dring, name):
    fold = np.zeros((ATT_RING, REL_PAD), np.float32)
    fold[np.arange(ATT_RING), _ring_index()] = 1.0
    fold = jnp.asarray(fold, BF16)

    def body(d_ref, f_ref, o_ref):
        x = jnp.sum(d_ref[...], axis=0, keepdims=True)
        x = jnp.broadcast_to(x, (8, ATT_RING))
        acc = jnp.zeros((8, REL_PAD), F32)
        for _ in range(3):
            part = x.astype(BF16)
            acc = acc + _dot(part, f_ref[...])
            x = x - part.astype(F32)
        o_ref[...] = acc[0:1, :]

    out = _call(body, name=name, grid=(ATT_HEADS,),
                in_specs=[pl.BlockSpec((None, 8, ATT_RING), lambda h: (h, 0, 0)),
                          pl.BlockSpec((ATT_RING, REL_PAD), lambda h: (0, 0))],
                out_specs=[pl.BlockSpec((None, 1, REL_PAD), lambda h: (h, 0, 0))],
                out_shape=[jax.ShapeDtypeStruct((ATT_HEADS, 1, REL_PAD), F32)],
                args=(dring.reshape(ATT_HEADS, 8, ATT_RING), fold))[0]
    return out.reshape(ATT_HEADS, REL_PAD)


def _adamw(w, g, m, v):
    m = ADAM_B1 * m + (1.0 - ADAM_B1) * g
    v = ADAM_B2 * v + (1.0 - ADAM_B2) * (g * g)
    m_hat = m / (1.0 - ADAM_B1 ** ADAM_STEP)
    v_hat = v / (1.0 - ADAM_B2 ** ADAM_STEP)
    delta = -ADAM_LR * (m_hat / (jnp.sqrt(v_hat) + ADAM_EPS) + ADAM_WD * w)
    return delta, m, v


def _sum_devices(ref):
    g = ref[0].astype(F32)
    for d in range(1, N_DEV):
        g = g + ref[d].astype(F32)
    return g


def _adamw_reduce(lands, w, m, v, name, tr=256):
    L, R, C = w.shape
    tr = min(tr, R)
    n_i = R // tr

    def body(*refs):
        l_refs = refs[:L]
        w_ref, m_ref, v_ref, g_out, d_out, m_out, v_out = refs[L:]
        layer = pl.program_id(0)
        for l in range(L):
            @pl.when(layer == l)
            def _(l=l):
                g = _sum_devices(l_refs[l])
                g_out[...] = g
                d_out[...], m_out[...], v_out[...] = _adamw(w_ref[...], g, m_ref[...], v_ref[...])

    def land_spec(l):
        return pl.BlockSpec((N_DEV, tr, C), lambda ly, i: (0, jnp.where(ly == l, i, jnp.where(ly < l, 0, n_i - 1)), 0))

    blk = pl.BlockSpec((None, tr, C), lambda ly, i: (ly, i, 0))
    return _call(body, name=name, grid=(L, n_i), in_specs=[land_spec(l) for l in range(L)] + [blk, blk, blk],
                 out_specs=[blk] * 4, out_shape=[jax.ShapeDtypeStruct((L, R, C), F32)] * 4, args=(*lands, w, m, v))


def _small_update(land_small, land_rel, w, m, v, name):
    def body(ls_ref, lr_ref, w_ref, m_ref, v_ref, g_out, d_out, m_out, v_out, rel_out):
        g = _sum_devices(ls_ref)
        g_out[...] = g
        d_out[...], m_out[...], v_out[...] = _adamw(w_ref[...], g, m_ref[...], v_ref[...])
        rel_out[...] = _sum_devices(lr_ref)

    return pl.pallas_call(
        body, name=name,
        out_shape=[jax.ShapeDtypeStruct(w.shape, F32)] * 4 + [jax.ShapeDtypeStruct(land_rel.shape[1:], F32)],
    )(land_small, land_rel, w, m, v)


def _adamw_plain(w, g, m, v, name):
    def body(w_ref, g_ref, m_ref, v_ref, d_out, m_out, v_out):
        d_out[...], m_out[...], v_out[...] = _adamw(w_ref[...], g_ref[...], m_ref[...], v_ref[...])

    return pl.pallas_call(body, name=name, out_shape=[jax.ShapeDtypeStruct(w.shape, F32)] * 3)(w, g, m, v)


def _ret_pieces():
    return [(RET_DK, lambda h: h, 0),
            (RET_DK, lambda h: RET_HEADS + h, RET_DK),
            (RET_DV, lambda h: RET_HEADS + h, 2 * RET_DK),
            (RET_DV, lambda h: 2 * RET_HEADS + h, 2 * RET_DK + RET_DV)]


def _ret_piece_of_column_block(p):
    per_head = RET_HEAD_COLS // RET_DK
    qk = jnp.where(p < RET_HEADS, per_head * p, per_head * (p - RET_HEADS) + 1)
    pv = p - 2 * RET_HEADS
    vv = per_head * (pv // 2) + 2 + pv % 2
    pg = p - 4 * RET_HEADS
    gg = per_head * (pg // 2) + 4 + pg % 2
    return jnp.where(p < 2 * RET_HEADS, qk, jnp.where(p < 4 * RET_HEADS, vv, gg))


def _gather(shard, full_shape, view):
    return _Exchange([shard], [(0, _whole, 0, view)], [jax.ShapeDtypeStruct(full_shape, shard.dtype)])


def _step(x, target, shards, mix_g, gn_g, rel_shard, mlp_g, final_g):
    s_rin, s_rout, s_ain, s_aout, s_w1, s_w2 = shards
    B, S, _ = x.shape
    T = B * S
    tk = min(T, 2048)
    n_k = T // tk
    h0 = x.reshape(T, D_MODEL)
    tgt = target.reshape(T, D_MODEL)
    consts = _ret_constants(S)
    att_pieces = [(D_MODEL, lambda j: j, 0)]
    w1_cols = w2_rows = D_FF // N_DEV

    def qkv_spec(tm):
        return pl.BlockSpec((None, tm, D_MODEL), lambda i, j: (j, i, 0))

    def head_spec(tm):
        return pl.BlockSpec((tm, RET_HEAD_COLS), lambda i, j: (i, j))

    rel_cols = REL_TABLE // N_DEV
    w_rin, rel_all = _all_gather(
        [s_rin, rel_shard],
        [jax.ShapeDtypeStruct((D_MODEL, RET_IN), BF16), jax.ShapeDtypeStruct((N_DEV, ATT_HEADS, rel_cols), F32)],
        [_cols(RET_IN // N_DEV), _slot], "gather_first")
    ring = _bias_ring(rel_all.transpose(1, 0, 2).reshape(ATT_HEADS, REL_TABLE))

    ride = _Exchange([s_rout, s_w1[0]], [(0, _whole, 0, _rows(RET_VW // N_DEV)), (1, _whole, 1, _cols(w1_cols))],
                     [jax.ShapeDtypeStruct((RET_VW, D_MODEL), BF16), jax.ShapeDtypeStruct((D_MODEL, D_FF), BF16)])
    (proj, n0), (w_rout, w1_0) = _norm_proj(h0, mix_g[0:1], w_rin, _ret_pieces(), RET_HEADS,
                                            jax.ShapeDtypeStruct((T, RET_HEADS * RET_HEAD_COLS), BF16), head_spec,
                                            "ret_proj", ride=ride)
    ride = _Exchange([s_w2[0], s_ain], [(0, _whole, 0, _rows(w2_rows)), (1, _whole, 1, _cols(3 * D_MODEL // N_DEV))],
                     [jax.ShapeDtypeStruct((D_FF, D_MODEL), BF16), jax.ShapeDtypeStruct((D_MODEL, 3 * D_MODEL), BF16)])
    (y, o, states), (w2_0, w_ain) = _ret_fwd(proj, consts, gn_g, B, S, "ret_fwd", ride=ride)
    h1 = _matmul_res(y, w_rout, h0, "ret_out")
    ride = _Exchange([s_aout, s_w1[1]], [(0, _whole, 0, _rows(D_MODEL // N_DEV)), (1, _whole, 1, _cols(w1_cols))],
                     [jax.ShapeDtypeStruct((D_MODEL, D_MODEL), BF16), jax.ShapeDtypeStruct((D_MODEL, D_FF), BF16)])
    (h2, n1, u1), (w_aout, w1_1) = _mlp_fwd(h1, mlp_g[0:1], w1_0, w2_0, "mlp0_fwd", ride=ride)
    qkv3, n2 = _norm_proj(h2, mix_g[1:2], w_ain, [(D_MODEL, functools.partial(lambda j, p: p, p=p), 0) for p in range(3)], 1,
                          jax.ShapeDtypeStruct((3, T, D_MODEL), BF16),
                          lambda tm: pl.BlockSpec((3, tm, D_MODEL), lambda i, j: (0, i, 0)), "att_proj", stacked=True)
    (o2,), (w2_1,) = _attn_fwd(qkv3, ring, B, S, "att_fwd", ride=_gather(s_w2[1], (D_FF, D_MODEL), _rows(w2_rows)))
    h3 = _matmul_res(o2, w_aout, h2, "att_out")
    h4, n3, u3 = _mlp_fwd(h3, mlp_g[1:2], w1_1, w2_1, "mlp1_fwd")
    dh4, dh4b, loss, dg_final = _final_loss(h4, final_g, tgt, "final_loss")

    def tok(width):
        return dict(a=pl.BlockSpec((tk, width), lambda i, j, k: (k, i)), b=pl.BlockSpec((tk, width), lambda i, j, k: (k, j)))

    def mlp_grads(dh, dhb, u, n, w1, w2, h, g, tag, ride=None):
        res = _mlp_bwd(dh, u, w1, w2, h, g, tag + "_bwd", ride=ride)
        (dhi, dhib, du, dg), lands = res if ride is not None else (res, None)
        gw2 = _wgrad(u, tok(1024)["a"], dhb, tok(1024)["b"], D_FF, D_MODEL, 1024, 1024, n_k, tag + "_dw2", square_a=True)
        gw1 = _wgrad(n, tok(1024)["a"], du, tok(1024)["b"], D_MODEL, D_FF, 1024, 1024, n_k, tag + "_dw1")
        return dhi, dhib, gw1, gw2, dg, lands

    dh3, dh3b, gw1_1, gw2_1, dg_mlp1, _ = mlp_grads(dh4, dh4b, u3, n3, w1_1, w2_1, h3, mlp_g[1:2], "mlp1")
    do2 = _matmul_nt(dh3b, w_aout, "att_out_bwd")
    g_aout = _wgrad(o2, tok(1024)["a"], dh3b, tok(1024)["b"], D_MODEL, D_MODEL, 1024, 1024, n_k, "att_out_dw")
    ride = _Exchange(
        [gw1_1, gw2_1, g_aout],
        [(0, _cols(w1_cols), 0, _slot), (1, _rows(w2_rows), 1, _slot), (2, _rows(D_MODEL // N_DEV), 2, _slot)],
        [jax.ShapeDtypeStruct((N_DEV, D_MODEL, w1_cols), BF16), jax.ShapeDtypeStruct((N_DEV, w2_rows, D_MODEL), BF16),
         jax.ShapeDtypeStruct((N_DEV, D_MODEL // N_DEV, D_MODEL), BF16)])
    (dqkv3, dring), (l_w1_1, l_w2_1, l_aout) = _attn_bwd(qkv3, ring, do2, B, S, "att_bwd", ride=ride)
    g_rel = _rel_bias_grad(dring, "rel_bias_grad")
    dh2, dh2b, dg_mix1 = _proj_bwd(dqkv3, qkv_spec, w_ain, att_pieces, 3, h2, mix_g[1:2], dh3, "att_proj_bwd")
    g_ain = _wgrad(n2, tok(1024)["a"], dqkv3, pl.BlockSpec((None, tk, D_MODEL), lambda i, j, k: (j, k, 0)),
                   D_MODEL, 3 * D_MODEL, 1024, 1024, n_k, "att_proj_dw")
    ride = _Exchange([g_ain], [(0, _cols(3 * D_MODEL // N_DEV), 0, _slot)],
                     [jax.ShapeDtypeStruct((N_DEV, D_MODEL, 3 * D_MODEL // N_DEV), BF16)])
    dh1, dh1b, gw1_0, gw2_0, dg_mlp0, (l_ain,) = mlp_grads(dh2, dh2b, u1, n1, w1_0, w2_0, h1, mlp_g[0:1], "mlp0", ride=ride)
    dy = _matmul_nt(dh1b, w_rout, "ret_out_bwd")
    g_rout = _wgrad(y, tok(1024)["a"], dh1b, tok(1024)["b"], RET_VW, D_MODEL, 1024, 1024, n_k, "ret_out_dw")
    ride = _Exchange(
        [gw1_0, gw2_0, g_rout],
        [(0, _cols(w1_cols), 0, _slot), (1, _rows(w2_rows), 1, _slot), (2, _rows(RET_VW // N_DEV), 2, _slot)],
        [jax.ShapeDtypeStruct((N_DEV, D_MODEL, w1_cols), BF16), jax.ShapeDtypeStruct((N_DEV, w2_rows, D_MODEL), BF16),
         jax.ShapeDtypeStruct((N_DEV, RET_VW // N_DEV, D_MODEL), BF16)])
    (dproj, dgn), (l_w1_0, l_w2_0, l_rout) = _ret_bwd(proj, consts, gn_g, o, states, dy, B, S, "ret_bwd", ride=ride)
    per_shard = RET_IN // N_DEV // RET_DK
    g_rin = _wgrad(n0, tok(1024)["a"], dproj,
                   [pl.BlockSpec((tk, RET_DK), functools.partial(
                       lambda i, j, k, r: (k, _ret_piece_of_column_block(per_shard * j + r)), r=r)) for r in range(per_shard)],
                   D_MODEL, RET_IN, 1024, per_shard * RET_DK, n_k, "ret_proj_dw")
    ride = _Exchange([g_rin], [(0, _cols(RET_IN // N_DEV), 0, _slot)],
                     [jax.ShapeDtypeStruct((N_DEV, D_MODEL, RET_IN // N_DEV), BF16)])
    (dx, _, dg_mix0), (l_rin,) = _proj_bwd(dproj, head_spec, w_rin, _ret_pieces(), RET_HEADS, h0, mix_g[0:1], dh1,
                                           "ret_proj_bwd", ride=ride)

    small = jnp.concatenate([dg_mix0, dg_mix1, dg_mlp0, dg_mlp1, dg_final, dgn.reshape(2, D_MODEL),
                             jnp.zeros((1, D_MODEL), F32)], axis=0)
    l_small, l_rel = _Exchange(
        [small, g_rel], [(0, _whole, 0, _slot), (1, _whole, 1, _slot)],
        [jax.ShapeDtypeStruct((N_DEV, 8, D_MODEL), F32), jax.ShapeDtypeStruct((N_DEV, ATT_HEADS, REL_PAD), F32)],
    ).run("scatter_small")
    lands = dict(rin=l_rin, rout=l_rout, ain=l_ain, aout=l_aout, w1=(l_w1_0, l_w1_1), w2=(l_w2_0, l_w2_1),
                 small=l_small, rel=l_rel)
    return loss[0, 0], dx.reshape(B, S, D_MODEL), lands


def kernel(x, mix_norm_g, ret_w_in, ret_gn_g, ret_w_out, att_w_in, att_rel_bias, att_w_out, mlp_norm_g, mlp_w1, mlp_w2, final_norm_g, loss_target, m_mix_norm_g, m_ret_w_in, m_ret_gn_g, m_ret_w_out, m_att_w_in, m_att_rel_bias, m_att_w_out, m_mlp_norm_g, m_mlp_w1, m_mlp_w2, m_final_norm_g, v_mix_norm_g, v_ret_w_in, v_ret_gn_g, v_ret_w_out, v_att_w_in, v_att_rel_bias, v_att_w_out, v_mlp_norm_g, v_mlp_w1, v_mlp_w2, v_final_norm_g):
    me = _lin(_place())
    rel_cols = REL_TABLE // N_DEV
    shards = (ret_w_in[0].astype(BF16), ret_w_out[0].astype(BF16), att_w_in[0].astype(BF16), att_w_out[0].astype(BF16),
              (mlp_w1[0].astype(BF16), mlp_w1[1].astype(BF16)), (mlp_w2[0].astype(BF16), mlp_w2[1].astype(BF16)))
    loss_part, grad_x, lands = _step(x, loss_target, shards, mix_norm_g, ret_gn_g, att_rel_bias[0], mlp_norm_g,
                                     final_norm_g.reshape(1, D_MODEL))
    loss = lax.psum(loss_part, ("x", "y", "c"))

    u_rin = _adamw_reduce([lands["rin"]], ret_w_in, m_ret_w_in, v_ret_w_in, "update_ret_w_in")
    u_rout = _adamw_reduce([lands["rout"]], ret_w_out, m_ret_w_out, v_ret_w_out, "update_ret_w_out")
    u_ain = _adamw_reduce([lands["ain"]], att_w_in, m_att_w_in, v_att_w_in, "update_att_w_in")
    u_aout = _adamw_reduce([lands["aout"]], att_w_out, m_att_w_out, v_att_w_out, "update_att_w_out")
    u_w1 = _adamw_reduce(lands["w1"], mlp_w1, m_mlp_w1, v_mlp_w1, "update_mlp_w1")
    u_w2 = _adamw_reduce(lands["w2"], mlp_w2, m_mlp_w2, v_mlp_w2, "update_mlp_w2")

    def pack(mix, mlp, fin, gn):
        return jnp.concatenate([mix, mlp, fin.reshape(1, D_MODEL), gn.reshape(2, D_MODEL), jnp.zeros((1, D_MODEL), F32)], axis=0)

    small_w = pack(mix_norm_g, mlp_norm_g, final_norm_g, ret_gn_g)
    small_m = pack(m_mix_norm_g, m_mlp_norm_g, m_final_norm_g, m_ret_gn_g)
    small_v = pack(v_mix_norm_g, v_mlp_norm_g, v_final_norm_g, v_ret_gn_g)
    sg, sd, sm, sv, rel_sum = _small_update(lands["small"], lands["rel"], small_w, small_m, small_v, "update_small")
    g_rel_mine = lax.dynamic_slice(rel_sum, (0, me * rel_cols), (ATT_HEADS, rel_cols))
    rel_d, rel_m, rel_v = _adamw_plain(att_rel_bias[0], g_rel_mine, m_att_rel_bias[0], v_att_rel_bias[0], "update_rel_bias")
    u_rel = [g_rel_mine[None], rel_d[None], rel_m[None], rel_v[None]]

    def unpack(t):
        return dict(mix=t[0:2], mlp=t[2:4], fin=t[4], gn=t[5:7].reshape(1, RET_VW))

    us = [unpack(t) for t in (sg, sd, sm, sv)]
    outs = [loss, grad_x]
    for k in range(4):
        outs += [us[k]["mix"], u_rin[k], us[k]["gn"], u_rout[k], u_ain[k], u_rel[k], u_aout[k], us[k]["mlp"],
                 u_w1[k], u_w2[k], us[k]["fin"]]
    return tuple(outs)
```

```python
import functools

import numpy as np
import jax
import jax.numpy as jnp
from jax import lax
from jax.experimental import pallas as pl
from jax.experimental.pallas import tpu as pltpu

F32, BF16 = jnp.float32, jnp.bfloat16

D_MODEL = 1024
CHUNK = 64
RET_HEADS, RET_DK, RET_DV = 4, 256, 512
RET_QK, RET_VW = RET_HEADS * RET_DK, RET_HEADS * RET_DV
RET_IN = 2 * RET_QK + 2 * RET_VW
RET_HEAD_COLS = 2 * RET_DK + 2 * RET_DV
RET_SC = 256
ROPE_BASE = 10000.0
ATT_HEADS, ATT_DH = 16, 64
LANES = 128
ATT_PAIRS = ATT_HEADS * ATT_DH // LANES
BAND_PAST = 8 * CHUNK
MAX_REL = 256
REL_TABLE = MAX_REL + CHUNK
REL_PAD = 384
ATT_CQ = 256
ATT_KW = ATT_CQ + BAND_PAST
ATT_RING = ATT_CQ + ATT_KW
ATT_VARIANTS = BAND_PAST // ATT_CQ + 1
D_FF = 4 * D_MODEL
EPS = 1e-6
EPILOGUE_ROWS = 256
NEG = -1e30
N_DEV = 8
N_PEERS = N_DEV - 1

ADAM_LR, ADAM_B1, ADAM_B2, ADAM_EPS, ADAM_WD, ADAM_STEP = 0.001, 0.9, 0.999, 1e-08, 0.01, 10

VMEM_LIMIT = 56 * 1024 * 1024
MESH = pl.DeviceIdType.MESH
ANY = pl.BlockSpec(memory_space=pl.ANY)


def _dot(a, b):
    return jnp.dot(a, b, preferred_element_type=F32)


def _dot_nt(a, b):
    return lax.dot_general(a, b, (((1,), (1,)), ((), ())), preferred_element_type=F32)


def _dot_tn(a, b):
    return lax.dot_general(a, b, (((0,), (0,)), ((), ())), preferred_element_type=F32)


def _rms(x, g):
    r = lax.rsqrt(jnp.mean(x * x, axis=-1, keepdims=True) + EPS)
    return x * r * g


def _rms_bwd(dn, x, g):
    r = lax.rsqrt(jnp.mean(x * x, axis=-1, keepdims=True) + EPS)
    xh = x * r
    dg = jnp.sum(dn * xh, axis=0, keepdims=True)
    dxh = dn * g
    dx = r * (dxh - xh * jnp.mean(dxh * xh, axis=-1, keepdims=True))
    return dx, dg


def _rms_bwd_epilogue(x_ref, g_ref, dres_ref, dx_ref, dxb_ref, dg_ref):
    for r in range(0, dx_ref.shape[0], EPILOGUE_ROWS):
        rows = slice(r, r + EPILOGUE_ROWS)
        dx, dg = _rms_bwd(dx_ref[rows, :], x_ref[rows, :], g_ref[...])
        dx = dres_ref[rows, :] + dx
        dx_ref[rows, :] = dx
        dxb_ref[rows, :] = dx.astype(BF16)
        dg_ref[...] += dg


def _place():
    return lax.axis_index("x"), lax.axis_index("y"), lax.axis_index("c")


def _lin(p):
    return 4 * p[0] + 2 * p[1] + p[2]


def _cols(width):
    return lambda ref, blk: ref.at[:, pl.ds(pl.multiple_of(blk * width, LANES), width)]


def _rows(height):
    return lambda ref, blk: ref.at[pl.ds(pl.multiple_of(blk * height, 8), height), :]


def _slot(ref, blk):
    return ref.at[blk]


def _whole(ref, blk):
    return ref


class _Exchange:
    def __init__(self, sources, flows, land_shapes):
        self.sources, self.flows, self.land_shapes = list(sources), flows, list(land_shapes)
        n_f = len(flows)
        self.sem_shapes = [pltpu.SemaphoreType.DMA((N_PEERS * n_f,)), pltpu.SemaphoreType.DMA((N_PEERS * n_f,)),
                           pltpu.SemaphoreType.DMA((n_f,))]

    def _copies(self, srcs, lands, sems):
        send_sems, recv_sems, local_sems = sems
        x, y, c = _place()
        me = (x, y, c)
        peers = [(x ^ dx, y ^ dy, c ^ dc) for dx in (0, 1) for dy in (0, 1) for dc in (0, 1)][1:]

        def copy(f, k, sender, to):
            si, sview, li, lview = self.flows[f]
            return pltpu.make_async_remote_copy(
                src_ref=sview(srcs[si], _lin(to)), dst_ref=lview(lands[li], _lin(sender)),
                send_sem=send_sems.at[f * N_PEERS + k], recv_sem=recv_sems.at[f * N_PEERS + k],
                device_id=to, device_id_type=MESH)

        mine, sends, recvs = [], [], []
        for f, (si, sview, li, lview) in enumerate(self.flows):
            mine.append(pltpu.make_async_copy(sview(srcs[si], _lin(me)), lview(lands[li], _lin(me)), local_sems.at[f]))
            for k, peer in enumerate(peers):
                sends.append(copy(f, k, me, peer))
                recvs.append(copy(f, k, peer, me))
        return mine, sends, recvs

    def start(self, srcs, lands, sems):
        mine, sends, _ = self._copies(srcs, lands, sems)
        for cp in mine + sends:
            cp.start()

    def finish(self, srcs, lands, sems):
        mine, sends, recvs = self._copies(srcs, lands, sems)
        for cp in recvs:
            cp.wait_recv()
        for cp in sends:
            cp.wait_send()
        for cp in mine:
            cp.wait()

    def run(self, name):
        n_src, n_land = len(self.sources), len(self.land_shapes)

        def body(*refs):
            srcs, lands, sems = refs[:n_src], refs[n_src:n_src + n_land], refs[n_src + n_land:]
            self.start(srcs, lands, sems)
            self.finish(srcs, lands, sems)

        return pl.pallas_call(body, name=name, in_specs=[ANY] * n_src, out_specs=[ANY] * n_land,
                              out_shape=self.land_shapes, scratch_shapes=self.sem_shapes)(*self.sources)


def _call(body, *, name, grid, in_specs, out_specs, out_shape, args, scratch_shapes=(), ride=None):
    params = pltpu.CompilerParams(dimension_semantics=("arbitrary",) * len(grid), vmem_limit_bytes=VMEM_LIMIT)
    in_specs, out_specs, out_shape, scratch_shapes = list(in_specs), list(out_specs), list(out_shape), list(scratch_shapes)
    if ride is None:
        return pl.pallas_call(body, name=name, grid=grid, in_specs=in_specs, out_specs=out_specs, out_shape=out_shape,
                              scratch_shapes=scratch_shapes, compiler_params=params)(*args)
    n_in, n_out, n_scr = len(in_specs), len(out_specs), len(scratch_shapes)
    n_src, n_land = len(ride.sources), len(ride.land_shapes)

    def riding(*refs):
        bounds = np.cumsum([n_in, n_src, n_out, n_land, n_scr])
        ins, srcs, outs, lands, scr, sems = (refs[a:b] for a, b in zip([0, *bounds], [*bounds, len(refs)]))
        ids = [pl.program_id(d) for d in range(len(grid))]
        first = functools.reduce(lambda a, b: a & b, [i == 0 for i in ids])
        last = functools.reduce(lambda a, b: a & b, [i == n - 1 for i, n in zip(ids, grid)])

        @pl.when(first)
        def _():
            ride.start(srcs, lands, sems)

        body(*ins, *outs, *scr)

        @pl.when(last)
        def _():
            ride.finish(srcs, lands, sems)

    res = pl.pallas_call(
        riding, name=name, grid=grid, in_specs=in_specs + [ANY] * n_src, out_specs=out_specs + [ANY] * n_land,
        out_shape=out_shape + ride.land_shapes, scratch_shapes=scratch_shapes + ride.sem_shapes,
        compiler_params=params)(*args, *ride.sources)
    return res[:n_out], res[n_out:]


def _all_gather(shards, out_shapes, views, name):
    n = len(shards)

    def body(*refs):
        ins, outs = refs[:n], refs[n:2 * n]
        send_sems, recv_sems, local_sems = refs[2 * n:]
        x, y, c = _place()
        me, sibling = (x, y, c), (x, y, 1 - c)
        chips = [(1 - x, y), (x, 1 - y), (1 - x, 1 - y)]

        def copy(a, k, block, to, src=None):
            dst = views[a](outs[a], _lin(block))
            return pltpu.make_async_remote_copy(
                src_ref=dst if src is None else src, dst_ref=dst,
                send_sem=send_sems.at[a * N_PEERS + k], recv_sem=recv_sems.at[a * N_PEERS + k],
                device_id=to, device_id_type=MESH)

        mine = [pltpu.make_async_copy(ins[a], views[a](outs[a], _lin(me)), local_sems.at[a]) for a in range(n)]
        for cp in mine:
            cp.start()
        first = []
        for a in range(n):
            first.append(copy(a, 0, me, sibling, src=ins[a]))
            first += [copy(a, 1 + j, me, (*chip, c), src=ins[a]) for j, chip in enumerate(chips)]
        for cp in first:
            cp.start()
        passed = []
        for j, chip in enumerate(chips):
            for a in range(n):
                copy(a, 1 + j, (*chip, c), me).wait_recv()
                fwd = copy(a, 4 + j, (*chip, c), sibling)
                fwd.start()
                passed.append(fwd)
        for a in range(n):
            copy(a, 0, sibling, me).wait_recv()
            for j, chip in enumerate(chips):
                copy(a, 4 + j, (*chip, 1 - c), me).wait_recv()
        for cp in first + passed:
            cp.wait_send()
        for cp in mine:
            cp.wait()

    return pl.pallas_call(
        body, name=name,
        in_specs=[ANY] * n, out_specs=[ANY] * n, out_shape=out_shapes,
        scratch_shapes=[pltpu.SemaphoreType.DMA((N_PEERS * n,)), pltpu.SemaphoreType.DMA((N_PEERS * n,)),
                        pltpu.SemaphoreType.DMA((n,))],
    )(*shards)


def _resident(shape):
    return pl.BlockSpec(shape, lambda i: (0,) * len(shape), pipeline_mode=pl.Buffered(1))


def _norm_proj(h, g, w, groups, name, tm, ride=None):
    T = h.shape[0]
    N = w.shape[1]
    width = N // groups

    def body(h_ref, g_ref, w_ref, o_ref, n_ref):
        n = _rms(h_ref[...], g_ref[...]).astype(BF16)
        n_ref[...] = n
        if groups == 1:
            o_ref[...] = _dot(n, w_ref[...]).astype(BF16)
        else:
            for p in range(groups):
                o_ref[p] = _dot(n, w_ref[:, p * width:(p + 1) * width]).astype(BF16)

    row = pl.BlockSpec((tm, D_MODEL), lambda i: (i, 0))
    if groups == 1:
        out_spec, out_shape = pl.BlockSpec((tm, N), lambda i: (i, 0)), jax.ShapeDtypeStruct((T, N), BF16)
    else:
        out_spec = pl.BlockSpec((groups, tm, width), lambda i: (0, i, 0))
        out_shape = jax.ShapeDtypeStruct((groups, T, width), BF16)
    return _call(body, name=name, grid=(T // tm,),
                 in_specs=[row, pl.BlockSpec((1, D_MODEL), lambda i: (0, 0)), _resident(w.shape)],
                 out_specs=[out_spec, row], out_shape=[out_shape, jax.ShapeDtypeStruct((T, D_MODEL), BF16)],
                 args=(h, g, w), ride=ride)


def _proj_bwd(dy, w, x, g, dres, name, tm, ride=None):
    T = x.shape[0]
    groups = dy.shape[0] if dy.ndim == 3 else 1
    width = w.shape[1] // groups

    def body(dy_ref, w_ref, x_ref, g_ref, dres_ref, dx_ref, dxb_ref, dg_ref):
        @pl.when(pl.program_id(0) == 0)
        def _():
            dg_ref[...] = jnp.zeros_like(dg_ref)

        if groups == 1:
            dn = _dot_nt(dy_ref[...], w_ref[...])
        else:
            dn = sum(_dot_nt(dy_ref[p], w_ref[:, p * width:(p + 1) * width]) for p in range(groups))
        dx_ref[...] = dn
        _rms_bwd_epilogue(x_ref, g_ref, dres_ref, dx_ref, dxb_ref, dg_ref)

    row = pl.BlockSpec((tm, D_MODEL), lambda i: (i, 0))
    vec = pl.BlockSpec((1, D_MODEL), lambda i: (0, 0))
    dy_spec = (pl.BlockSpec((tm, w.shape[1]), lambda i: (i, 0)) if groups == 1
               else pl.BlockSpec((groups, tm, width), lambda i: (0, i, 0)))
    return _call(body, name=name, grid=(T // tm,),
                 in_specs=[dy_spec, _resident(w.shape), row, vec, row], out_specs=[row, row, vec],
                 out_shape=[jax.ShapeDtypeStruct((T, D_MODEL), F32), jax.ShapeDtypeStruct((T, D_MODEL), BF16),
                            jax.ShapeDtypeStruct((1, D_MODEL), F32)],
                 args=(dy, w, x, g, dres), ride=ride)


def _matmul_res(a, w, res, name, tm=1024):
    T, K = a.shape

    def body(a_ref, w_ref, r_ref, o_ref):
        o_ref[...] = r_ref[...] + _dot(a_ref[...], w_ref[...])

    row = pl.BlockSpec((tm, D_MODEL), lambda i: (i, 0))
    return _call(body, name=name, grid=(T // tm,),
                 in_specs=[pl.BlockSpec((tm, K), lambda i: (i, 0)), pl.BlockSpec((K, D_MODEL), lambda i: (0, 0)), row],
                 out_specs=[row], out_shape=[jax.ShapeDtypeStruct((T, D_MODEL), F32)], args=(a, w, res))[0]


def _matmul_nt(dy, w, name, tm=1024):
    T, N = dy.shape
    K = w.shape[0]

    def body(dy_ref, w_ref, o_ref):
        o_ref[...] = _dot_nt(dy_ref[...], w_ref[...]).astype(BF16)

    return _call(body, name=name, grid=(T // tm,),
                 in_specs=[pl.BlockSpec((tm, N), lambda i: (i, 0)), pl.BlockSpec((K, N), lambda i: (0, 0))],
                 out_specs=[pl.BlockSpec((tm, K), lambda i: (i, 0))], out_shape=[jax.ShapeDtypeStruct((T, K), BF16)],
                 args=(dy, w))[0]


def _wgrad(a, a_spec, b, b_specs, m, n, bm, bn, n_k, name, square_a=False):
    b_specs = b_specs if isinstance(b_specs, (list, tuple)) else [b_specs]
    n_b = len(b_specs)

    def body(a_ref, *rest):
        b_refs = rest[:n_b]
        o_ref, acc = rest[n_b:]
        k = pl.program_id(2)

        @pl.when(k == 0)
        def _():
            acc[...] = jnp.zeros_like(acc)

        av = a_ref[...]
        if square_a:
            af = av.astype(F32)
            av = (af * af).astype(BF16)
        bv = b_refs[0][...] if n_b == 1 else jnp.concatenate([r[...] for r in b_refs], axis=1)
        acc[...] += _dot_tn(av, bv)

        @pl.when(k == n_k - 1)
        def _():
            o_ref[...] = acc[...].astype(BF16)

    return _call(body, name=name, grid=(m // bm, n // bn, n_k), in_specs=[a_spec, *b_specs],
                 out_specs=[pl.BlockSpec((bm, bn), lambda i, j, k: (i, j))], out_shape=[jax.ShapeDtypeStruct((m, n), BF16)],
                 scratch_shapes=[pltpu.VMEM((bm, bn), F32)], args=(a, *([b] * n_b)))[0]


def _mlp_fwd(h, g, w1, w2, name, tm=1024, tf=1024, ride=None):
    T = h.shape[0]

    def body(h_ref, g_ref, w1_ref, w2_ref, ho_ref, n_ref, u_ref):
        @pl.when(pl.program_id(1) == 0)
        def _():
            n_ref[...] = _rms(h_ref[...], g_ref[...]).astype(BF16)
            ho_ref[...] = h_ref[...]

        u = jnp.maximum(_dot(n_ref[...], w1_ref[...]), 0.0)
        u_ref[...] = u.astype(BF16)
        ho_ref[...] += _dot((u * u).astype(BF16), w2_ref[...])

    row = pl.BlockSpec((tm, D_MODEL), lambda i, j: (i, 0))
    return _call(body, name=name, grid=(T // tm, D_FF // tf),
                 in_specs=[row, pl.BlockSpec((1, D_MODEL), lambda i, j: (0, 0)),
                           pl.BlockSpec((D_MODEL, tf), lambda i, j: (0, j)), pl.BlockSpec((tf, D_MODEL), lambda i, j: (j, 0))],
                 out_specs=[row, row, pl.BlockSpec((tm, tf), lambda i, j: (i, j))],
                 out_shape=[jax.ShapeDtypeStruct((T, D_MODEL), F32), jax.ShapeDtypeStruct((T, D_MODEL), BF16),
                            jax.ShapeDtypeStruct((T, D_FF), BF16)],
                 args=(h, g, w1, w2), ride=ride)


def _mlp_bwd(dh, u, w1, w2, h, g, name, tm=1024, tf=1024, ride=None):
    T = h.shape[0]
    n_j = D_FF // tf

    def body(dh_ref, u_ref, w1_ref, w2_ref, h_ref, g_ref, dx_ref, dxb_ref, du_ref, dg_ref, dhb):
        acc = dx_ref
        i, j = pl.program_id(0), pl.program_id(1)

        @pl.when(j == 0)
        def _():
            dhb[...] = dh_ref[...].astype(BF16)
            acc[...] = jnp.zeros_like(acc)

        @pl.when((i == 0) & (j == 0))
        def _():
            dg_ref[...] = jnp.zeros_like(dg_ref)

        da = _dot_nt(dhb[...], w2_ref[...])
        du = (da * (2.0 * u_ref[...].astype(F32))).astype(BF16)
        du_ref[...] = du
        acc[...] += _dot_nt(du, w1_ref[...])

        @pl.when(j == n_j - 1)
        def _():
            _rms_bwd_epilogue(h_ref, g_ref, dh_ref, dx_ref, dxb_ref, dg_ref)

    row = pl.BlockSpec((tm, D_MODEL), lambda i, j: (i, 0))
    vec = pl.BlockSpec((1, D_MODEL), lambda i, j: (0, 0))
    hid = pl.BlockSpec((tm, tf), lambda i, j: (i, j))
    return _call(body, name=name, grid=(T // tm, n_j),
                 in_specs=[row, hid, pl.BlockSpec((D_MODEL, tf), lambda i, j: (0, j)),
                           pl.BlockSpec((tf, D_MODEL), lambda i, j: (j, 0)), row, vec],
                 out_specs=[row, row, hid, vec],
                 out_shape=[jax.ShapeDtypeStruct((T, D_MODEL), F32), jax.ShapeDtypeStruct((T, D_MODEL), BF16),
                            jax.ShapeDtypeStruct((T, D_FF), BF16), jax.ShapeDtypeStruct((1, D_MODEL), F32)],
                 scratch_shapes=[pltpu.VMEM((tm, D_MODEL), BF16)], args=(dh, u, w1, w2, h, g), ride=ride)


def _final_loss(h, g, target, name, tm=512):
    T = h.shape[0]

    def body(h_ref, g_ref, t_ref, dh_ref, dhb_ref, loss_ref, dg_ref):
        @pl.when(pl.program_id(0) == 0)
        def _():
            loss_ref[...] = jnp.zeros_like(loss_ref)
            dg_ref[...] = jnp.zeros_like(dg_ref)

        x = h_ref[...]
        gg = g_ref[...]
        r = lax.rsqrt(jnp.mean(x * x, axis=-1, keepdims=True) + EPS)
        xh = x * r
        e = xh * gg - t_ref[...]
        per_tok = jnp.mean(e * e, axis=-1, keepdims=True)
        loss_ref[...] += 0.5 * jnp.sum(per_tok, axis=0, keepdims=True)
        dy = e * (1.0 / D_MODEL)
        dg_ref[...] += jnp.sum(dy * xh, axis=0, keepdims=True)
        dxh = dy * gg
        dx = r * (dxh - xh * jnp.mean(dxh * xh, axis=-1, keepdims=True))
        dh_ref[...] = dx
        dhb_ref[...] = dx.astype(BF16)

    row = pl.BlockSpec((tm, D_MODEL), lambda i: (i, 0))
    vec = pl.BlockSpec((1, D_MODEL), lambda i: (0, 0))
    return _call(body, name=name, grid=(T // tm,), in_specs=[row, vec, row],
                 out_specs=[row, row, pl.BlockSpec((8, LANES), lambda i: (0, 0)), vec],
                 out_shape=[jax.ShapeDtypeStruct((T, D_MODEL), F32), jax.ShapeDtypeStruct((T, D_MODEL), BF16),
                            jax.ShapeDtypeStruct((8, LANES), F32), jax.ShapeDtypeStruct((1, D_MODEL), F32)],
                 args=(h, g, target))


def _ret_constants(S):
    log_gamma = jnp.log1p(-jnp.exp2(-5.0 - jnp.arange(RET_HEADS, dtype=F32)))
    idx = jnp.arange(RET_SC, dtype=F32)
    i, j = idx[:, None], idx[None, :]
    same_chunk = jnp.floor(i / CHUNK) == jnp.floor(j / CHUNK)
    mask = jnp.where((j <= i) | same_chunk, jnp.exp(log_gamma[:, None, None] * jnp.abs(i - j)[None]), 0.0)
    qdec = jnp.exp(log_gamma[:, None] * (idx + 1.0)[None, :])[:, :, None]
    kdec = jnp.exp(log_gamma[:, None] * (RET_SC - 1 - idx)[None, :])[:, :, None]
    cdec = jnp.exp(log_gamma * RET_SC)[:, None, None]
    half = RET_DK // 2
    inv = jnp.exp(-jnp.log(ROPE_BASE) * jnp.arange(half, dtype=F32) / half)
    ang = jnp.arange(S, dtype=F32)[:, None] * inv[None, :]
    return jnp.cos(ang), jnp.sin(ang), mask.astype(F32), qdec, kdec, cdec


def _rope(t, cs, sn):
    t1, t2 = t[:, :RET_DK // 2], t[:, RET_DK // 2:]
    return jnp.concatenate([t1 * cs - t2 * sn, t1 * sn + t2 * cs], axis=-1)


def _rope_bwd(d, cs, sn):
    d1, d2 = d[:, :RET_DK // 2], d[:, RET_DK // 2:]
    return jnp.concatenate([d1 * cs + d2 * sn, d2 * cs - d1 * sn], axis=-1)


def _ret_specs(B, S, reverse):
    n_sc = S // RET_SC

    def cc(c):
        return n_sc - 1 - c if reverse else c

    return dict(
        proj=pl.BlockSpec((B, RET_SC, RET_HEAD_COLS), lambda h, c: (0, cc(c), h)),
        trig=pl.BlockSpec((RET_SC, RET_DK // 2), lambda h, c: (cc(c), 0)),
        mask=pl.BlockSpec((None, RET_SC, RET_SC), lambda h, c: (h, 0, 0)),
        dec=pl.BlockSpec((None, RET_SC, 1), lambda h, c: (h, 0, 0)),
        cdec=pl.BlockSpec((None, 1, 1), lambda h, c: (h, 0, 0)),
        gn=pl.BlockSpec((1, RET_DV), lambda h, c: (0, h)),
        val=pl.BlockSpec((B, RET_SC, RET_DV), lambda h, c: (0, cc(c), h)),
        state=pl.BlockSpec((B, None, None, RET_DK, RET_DV), lambda h, c: (0, h, cc(c), 0, 0)),
    )


def _ret_qkvg(p_ref, cs, sn):
    q = _rope(p_ref[:, 0:RET_DK].astype(F32), cs, sn)
    k = _rope(p_ref[:, RET_DK:2 * RET_DK].astype(F32), cs, sn) * (RET_DK ** -0.5)
    v = p_ref[:, 2 * RET_DK:2 * RET_DK + RET_DV]
    gate = p_ref[:, 2 * RET_DK + RET_DV:RET_HEAD_COLS].astype(F32)
    return q, k, v, gate


def _group_norm(o):
    mu = jnp.mean(o, axis=-1, keepdims=True)
    oc = o - mu
    rstd = lax.rsqrt(jnp.mean(oc * oc, axis=-1, keepdims=True) + EPS)
    return oc * rstd, rstd


def _ret_fwd(proj, consts, gn, B, S, name, ride=None):
    T = B * S
    n_sc = S // RET_SC
    sp = _ret_specs(B, S, False)

    def body(p_ref, cos_ref, sin_ref, m_ref, qd_ref, kd_ref, cd_ref, gn_ref, y_ref, o_ref, st_ref, state):
        @pl.when(pl.program_id(1) == 0)
        def _():
            state[...] = jnp.zeros_like(state)

        seqs = range(B)
        qkvg = [_ret_qkvg(p_ref.at[b], cos_ref[...], sin_ref[...]) for b in seqs]
        qb = [qkvg[b][0].astype(BF16) for b in seqs]
        kb = [qkvg[b][1].astype(BF16) for b in seqs]
        kt = [(qkvg[b][1] * kd_ref[...]).astype(BF16) for b in seqs]
        v = [qkvg[b][2] for b in seqs]
        sc = [_dot_nt(qb[b], kb[b]) for b in seqs]
        sb = [state[b].astype(BF16) for b in seqs]
        cross = [_dot(qb[b], sb[b]) for b in seqs]
        for b in seqs:
            st_ref[b] = sb[b]
        p = [(sc[b] * m_ref[...]).astype(BF16) for b in seqs]
        o = [_dot(p[b], v[b]) + cross[b] * qd_ref[...] for b in seqs]
        upd = [_dot_tn(kt[b], v[b]) for b in seqs]
        for b in seqs:
            o_ref[b] = o[b].astype(BF16)
            ohat, _ = _group_norm(o[b])
            gate = qkvg[b][3]
            y_ref[b] = (gate * jax.nn.sigmoid(gate) * (ohat * gn_ref[...])).astype(BF16)
            state[b] = state[b] * cd_ref[...] + upd[b]

    res = _call(
        body, name=name, grid=(RET_HEADS, n_sc),
        in_specs=[sp["proj"], sp["trig"], sp["trig"], sp["mask"], sp["dec"], sp["dec"], sp["cdec"], sp["gn"]],
        out_specs=[sp["val"], sp["val"], sp["state"]],
        out_shape=[jax.ShapeDtypeStruct((B, S, RET_VW), BF16), jax.ShapeDtypeStruct((B, S, RET_VW), BF16),
                   jax.ShapeDtypeStruct((B, RET_HEADS, n_sc, RET_DK, RET_DV), BF16)],
        scratch_shapes=[pltpu.VMEM((B, RET_DK, RET_DV), F32)],
        args=(proj.reshape(B, S, -1), *consts, gn), ride=ride)
    (y, o, states), lands = res if ride is not None else (res, None)
    outs = [y.reshape(T, RET_VW), o.reshape(T, RET_VW), states]
    return outs if ride is None else (outs, lands)


def _ret_bwd(proj, consts, gn, o, states, dy, B, S, name, ride=None):
    T = B * S
    n_sc = S // RET_SC
    sp = _ret_specs(B, S, True)

    def body(p_ref, cos_ref, sin_ref, m_ref, qd_ref, kd_ref, cd_ref, gn_ref, o_ref, st_ref, dy_ref,
             dp_ref, dgn_ref, dstate):
        @pl.when(pl.program_id(1) == 0)
        def _():
            dstate[...] = jnp.zeros_like(dstate)
            dgn_ref[...] = jnp.zeros_like(dgn_ref)

        seqs = range(B)
        cs, sn = cos_ref[...], sin_ref[...]
        m, gnv = m_ref[...], gn_ref[...]
        qkvg = [_ret_qkvg(p_ref.at[b], cs, sn) for b in seqs]
        qb = [qkvg[b][0].astype(BF16) for b in seqs]
        kb = [qkvg[b][1].astype(BF16) for b in seqs]
        kt = [(qkvg[b][1] * kd_ref[...]).astype(BF16) for b in seqs]
        v = [qkvg[b][2] for b in seqs]
        sc = [_dot_nt(qb[b], kb[b]) for b in seqs]
        dsb = [dstate[b].astype(BF16) for b in seqs]
        dv_state = [_dot(kt[b], dsb[b]) for b in seqs]
        dkt = [_dot_nt(v[b], dsb[b]) for b in seqs]
        do, dgate = [], []
        for b in seqs:
            gate = qkvg[b][3]
            ohat, rstd = _group_norm(o_ref[b].astype(F32))
            dyv = dy_ref[b].astype(F32)
            sg = jax.nn.sigmoid(gate)
            don = dyv * (gate * sg)
            dgate.append(dyv * (ohat * gnv) * (sg * (1.0 + gate * (1.0 - sg))))
            dgn_ref[...] += jnp.sum(don * ohat, axis=0, keepdims=True)
            dohat = don * gnv
            do.append(rstd * (dohat - jnp.mean(dohat, axis=-1, keepdims=True)
                              - ohat * jnp.mean(dohat * ohat, axis=-1, keepdims=True)))
        dob = [do[b].astype(BF16) for b in seqs]
        doq = [(do[b] * qd_ref[...]).astype(BF16) for b in seqs]
        dsc_f = [_dot_nt(dob[b], v[b]) for b in seqs]
        dq_state = [_dot_nt(doq[b], st_ref[b]) for b in seqs]
        dstate_upd = [_dot_tn(qb[b], doq[b]) for b in seqs]
        p = [(sc[b] * m).astype(BF16) for b in seqs]
        dsc = [(dsc_f[b] * m).astype(BF16) for b in seqs]
        dv = [_dot_tn(p[b], dob[b]) + dv_state[b] for b in seqs]
        dq = [_dot(dsc[b], kb[b]) + dq_state[b] for b in seqs]
        dk = [(_dot_tn(dsc[b], qb[b]) + dkt[b] * kd_ref[...]) * (RET_DK ** -0.5) for b in seqs]
        for b in seqs:
            dstate[b] = dstate[b] * cd_ref[...] + dstate_upd[b]
            dp_ref[b, :, 0:RET_DK] = _rope_bwd(dq[b], cs, sn).astype(BF16)
            dp_ref[b, :, RET_DK:2 * RET_DK] = _rope_bwd(dk[b], cs, sn).astype(BF16)
            dp_ref[b, :, 2 * RET_DK:2 * RET_DK + RET_DV] = dv[b].astype(BF16)
            dp_ref[b, :, 2 * RET_DK + RET_DV:RET_HEAD_COLS] = dgate[b].astype(BF16)

    res = _call(
        body, name=name, grid=(RET_HEADS, n_sc),
        in_specs=[sp["proj"], sp["trig"], sp["trig"], sp["mask"], sp["dec"], sp["dec"], sp["cdec"], sp["gn"],
                  sp["val"], sp["state"], sp["val"]],
        out_specs=[sp["proj"], sp["gn"]],
        out_shape=[jax.ShapeDtypeStruct((B, S, RET_HEADS * RET_HEAD_COLS), BF16), jax.ShapeDtypeStruct((1, RET_VW), F32)],
        scratch_shapes=[pltpu.VMEM((B, RET_DK, RET_DV), F32)],
        args=(proj.reshape(B, S, -1), *consts, gn, o.reshape(B, S, -1), states, dy.reshape(B, S, -1)), ride=ride)
    (dproj, dgn), lands = res if ride is not None else (res, None)
    outs = [dproj.reshape(T, -1), dgn]
    return outs if ride is None else (outs, lands)


def _ring_index():
    u = np.arange(ATT_RING)
    offset = np.where(u < ATT_KW, u, u - ATT_RING)
    return np.clip(offset - BAND_PAST, -MAX_REL, CHUNK - 1) + MAX_REL


def _bias_ring(rel):
    n_clip = BAND_PAST - MAX_REL
    n_hi = ATT_KW - n_clip - REL_TABLE
    ring = jnp.concatenate([jnp.broadcast_to(rel[:, :1], (ATT_HEADS, n_clip)), rel,
                            jnp.broadcast_to(rel[:, -1:], (ATT_HEADS, n_hi)),
                            jnp.broadcast_to(rel[:, :1], (ATT_HEADS, ATT_CQ))], axis=1)
    return jnp.broadcast_to(ring.reshape(ATT_PAIRS, 2, 1, ATT_RING), (ATT_PAIRS, 2, 8, ATT_RING))


def _band_masks():
    i = np.arange(ATT_CQ)[:, None]
    j = np.arange(ATT_KW)[None, :]
    lo = CHUNK * (i // CHUNK)
    band = np.where((j >= lo) & (j < lo + BAND_PAST + CHUNK), 0.0, NEG)
    return np.stack([band + np.where(j + v * ATT_CQ >= BAND_PAST, 0.0, NEG)
                     for v in range(ATT_VARIANTS)]).astype(np.float32)


def _attn_bias(bias_scr, ring_ref, band_ref):
    for hh in range(2):
        rows = jnp.broadcast_to(ring_ref[hh, 0:1, :], (ATT_CQ, ATT_RING))
        toeplitz = pltpu.roll(rows, 0, 1, stride=1, stride_axis=0)[:, :ATT_KW]
        for v in range(ATT_VARIANTS):
            bias_scr[v, hh] = toeplitz + band_ref[v]


ATT_STRIP = 32
ATT_SCALE = ATT_DH ** -0.5


def _strips(fn):
    def strip(r, carry):
        fn(pl.ds(pl.multiple_of(r * ATT_STRIP, ATT_STRIP), ATT_STRIP))
        return carry

    lax.fori_loop(0, ATT_CQ // ATT_STRIP, strip, 0, unroll=True)


def _attn_prepare(kpad, vpad, qm, qkv_ref):
    kpad[0:BAND_PAST, :] = jnp.zeros((BAND_PAST, LANES), BF16)
    vpad[0:BAND_PAST, :] = jnp.zeros((BAND_PAST, LANES), BF16)
    kpad[BAND_PAST:, :] = qkv_ref[1]
    vpad[BAND_PAST:, :] = qkv_ref[2]
    lane = lax.broadcasted_iota(jnp.int32, (1, LANES), 1)
    q = qkv_ref[0] * ATT_SCALE
    for hh in range(2):
        qm[hh] = jnp.where((lane >= ATT_DH * hh) & (lane < ATT_DH * (hh + 1)), q, jnp.zeros_like(q))


def _attn_scores(s_ref, qm, kpad, b_ref, t, n_qb):
    t = jnp.minimum(t, n_qb - 1)
    qs = pl.multiple_of(t * ATT_CQ, ATT_CQ)
    variant = jnp.minimum(t, ATT_VARIANTS - 1)
    kw = kpad[pl.ds(qs, ATT_KW), :]
    for hh in range(2):
        s_ref[hh] = _dot_nt(qm[hh, pl.ds(qs, ATT_CQ), :], kw) + b_ref[variant, hh]


def _attn_softmax(s_ref, e_ref, linv_ref, m_ref=None):
    def strip(rows):
        s = s_ref[rows, :]
        m = jnp.max(s, axis=-1, keepdims=True)
        e = jnp.exp(s - m)
        e_ref[rows, :] = e.astype(BF16)
        linv_ref[rows, :] = jnp.broadcast_to(1.0 / jnp.sum(e, axis=-1, keepdims=True), (ATT_STRIP, LANES))
        if m_ref is not None:
            m_ref[rows, :] = jnp.broadcast_to(m, (ATT_STRIP, LANES))

    _strips(strip)


_RING_SPEC = pl.BlockSpec((None, 2, 8, ATT_RING), lambda hp, b: (hp, 0, 0, 0))
_BAND_SPEC = pl.BlockSpec((ATT_VARIANTS, ATT_CQ, ATT_KW), lambda hp, b: (0, 0, 0))


def _attn_fwd(qkv3, ring, B, S, name, ride=None):
    T = B * S
    n_qb = S // ATT_CQ

    def body(qkv_ref, ring_ref, band_ref, o_ref, kpad, vpad, qm, b_ref, s_scr, e_scr, linv_scr):
        @pl.when(pl.program_id(1) == 0)
        def _():
            _attn_bias(b_ref, ring_ref, band_ref)

        _attn_prepare(kpad, vpad, qm, qkv_ref)
        e_scr[...] = jnp.zeros_like(e_scr)
        linv_scr[...] = jnp.zeros_like(linv_scr)
        lane = lax.broadcasted_iota(jnp.int32, (1, LANES), 1)

        def softmax(slot):
            for hh in range(2):
                _attn_softmax(s_scr.at[slot, hh], e_scr.at[slot, hh], linv_scr.at[slot, hh])

        def output(t, slot):
            qs = pl.multiple_of(jnp.maximum(t, 0) * ATT_CQ, ATT_CQ)
            vw = vpad[pl.ds(qs, ATT_KW), :]
            outs = [_dot(e_scr[slot, hh], vw) * linv_scr[slot, hh] for hh in range(2)]
            o_ref[pl.ds(qs, ATT_CQ), :] = jnp.where(lane < ATT_DH, outs[0], outs[1]).astype(BF16)

        def pair(u, carry):
            t = 2 * u
            _attn_scores(s_scr.at[1], qm, kpad, b_ref, t + 1, n_qb)
            softmax(0)
            output(t - 1, 1)
            _attn_scores(s_scr.at[0], qm, kpad, b_ref, t + 2, n_qb)
            softmax(1)
            output(t, 0)
            return carry

        _attn_scores(s_scr.at[0], qm, kpad, b_ref, 0, n_qb)
        lax.fori_loop(0, n_qb // 2, pair, 0)
        output(n_qb - 1, 1)

    return _call(body, name=name, grid=(ATT_PAIRS, B),
                 in_specs=[pl.BlockSpec((3, S, LANES), lambda hp, b: (0, b, hp)), _RING_SPEC, _BAND_SPEC],
                 out_specs=[pl.BlockSpec((S, LANES), lambda hp, b: (b, hp))],
                 out_shape=[jax.ShapeDtypeStruct((T, D_MODEL), BF16)],
                 scratch_shapes=[pltpu.VMEM((S + BAND_PAST, LANES), BF16), pltpu.VMEM((S + BAND_PAST, LANES), BF16),
                                 pltpu.VMEM((2, S, LANES), BF16), pltpu.VMEM((ATT_VARIANTS, 2, ATT_CQ, ATT_KW), F32),
                                 pltpu.VMEM((2, 2, ATT_CQ, ATT_KW), F32),
                                 pltpu.VMEM((2, 2, ATT_CQ, ATT_KW), BF16), pltpu.VMEM((2, 2, ATT_CQ, LANES), F32)],
                 args=(qkv3, ring, jnp.asarray(_band_masks())), ride=ride)


def _attn_bwd(qkv3, ring, do, B, S, name, ride=None):
    T = B * S
    n_qb = S // ATT_CQ

    def body(qkv_ref, ring_ref, band_ref, do_ref, dqkv_ref, dring_ref, kpad, vpad, qm, dkacc, dvacc, b_ref, db_ref):
        @pl.when(pl.program_id(1) == 0)
        def _():
            _attn_bias(b_ref, ring_ref, band_ref)
            db_ref[...] = jnp.zeros_like(db_ref)

        _attn_prepare(kpad, vpad, qm, qkv_ref)
        dkacc[...] = jnp.zeros_like(dkacc)
        dvacc[...] = jnp.zeros_like(dvacc)
        lane = lax.broadcasted_iota(jnp.int32, (1, LANES), 1)

        def step(qb, carry):
            qs = pl.multiple_of(qb * ATT_CQ, ATT_CQ)
            variant = jnp.minimum(qb, ATT_VARIANTS - 1)
            dov = do_ref[pl.ds(qs, ATT_CQ), :].astype(F32)
            kw = kpad[pl.ds(qs, ATT_KW), :]
            vw = vpad[pl.ds(qs, ATT_KW), :]
            heads = (0, 1)
            qmh = [qm[hh, pl.ds(qs, ATT_CQ), :] for hh in heads]
            s = [_dot_nt(qmh[hh], kw) + b_ref[variant, hh] for hh in heads]
            e, linv, dom, dp = [None, None], [None, None], [None, None], [None, None]
            for hh in heads:
                e[hh] = jnp.exp(s[hh] - jnp.max(s[hh], axis=-1, keepdims=True))
                linv[hh] = 1.0 / jnp.sum(e[hh], axis=-1, keepdims=True)
                sel = (lane >= ATT_DH * hh) & (lane < ATT_DH * (hh + 1))
                dom[hh] = jnp.where(sel, dov * linv[hh], 0.0).astype(BF16)
                dp[hh] = _dot_nt(dom[hh], vw)
            dqs, dk, dv = [], None, None
            for hh in heads:
                ds = e[hh] * (dp[hh] - jnp.sum(dp[hh] * e[hh], axis=-1, keepdims=True) * linv[hh])
                db_ref[hh] += ds
                dsb = ds.astype(BF16)
                dqs.append(_dot(dsb, kw) * ATT_SCALE)
                dk_h = _dot_tn(qmh[hh], dsb)
                dv_h = _dot_tn(dom[hh], e[hh].astype(BF16))
                dk = dk_h if dk is None else dk + dk_h
                dv = dv_h if dv is None else dv + dv_h
            dqkv_ref[0, pl.ds(qs, ATT_CQ), :] = jnp.where(lane < ATT_DH, dqs[0], dqs[1]).astype(BF16)
            dkacc[:, pl.ds(qs, ATT_KW)] += dk
            dvacc[:, pl.ds(qs, ATT_KW)] += dv
            return carry

        lax.fori_loop(0, n_qb, step, 0)
        dqkv_ref[1] = dkacc[:, BAND_PAST:].T.astype(BF16)
        dqkv_ref[2] = dvacc[:, BAND_PAST:].T.astype(BF16)

        @pl.when(pl.program_id(1) == B - 1)
        def _():
            r = lax.broadcasted_iota(jnp.int32, (ATT_CQ, ATT_CQ), 0)
            c = lax.broadcasted_iota(jnp.int32, (ATT_CQ, ATT_CQ), 1)
            reverse = jnp.where(r + c == ATT_CQ - 1, 1.0, 0.0).astype(BF16)
            for hh in range(2):
                x = jnp.concatenate([db_ref[hh], jnp.zeros((ATT_CQ, ATT_RING - ATT_KW), F32)], axis=1)
                flipped = jnp.zeros((ATT_CQ, ATT_RING), F32)
                for _ in range(3):
                    part = x.astype(BF16)
                    flipped = flipped + _dot(reverse, part)
                    x = x - part.astype(F32)
                aligned = pltpu.roll(flipped, ATT_KW + 1, 1, stride=1, stride_axis=0)
                dring_ref[hh] = jnp.sum(aligned.reshape(ATT_CQ // 8, 8, ATT_RING), axis=0)

    qkv_spec = pl.BlockSpec((3, S, LANES), lambda hp, b: (0, b, hp))
    return _call(body, name=name, grid=(ATT_PAIRS, B),
                 in_specs=[qkv_spec, _RING_SPEC, _BAND_SPEC, pl.BlockSpec((S, LANES), lambda hp, b: (b, hp))],
                 out_specs=[qkv_spec, _RING_SPEC],
                 out_shape=[jax.ShapeDtypeStruct((3, T, D_MODEL), BF16),
                            jax.ShapeDtypeStruct((ATT_PAIRS, 2, 8, ATT_RING), F32)],
                 scratch_shapes=[pltpu.VMEM((S + BAND_PAST, LANES), BF16), pltpu.VMEM((S + BAND_PAST, LANES), BF16),
                                 pltpu.VMEM((2, S, LANES), BF16),
                                 pltpu.VMEM((LANES, S + BAND_PAST), F32), pltpu.VMEM((LANES, S + BAND_PAST), F32),
                                 pltpu.VMEM((ATT_VARIANTS, 2, ATT_CQ, ATT_KW), F32), pltpu.VMEM((2, ATT_CQ, ATT_KW), F32)],
                 args=(qkv3, ring, jnp.asarray(_band_masks()), do), ride=ride)


def _rel_bias_grad(dring, name):
    fold = np.zeros((ATT_RING, REL_PAD), np.float32)
    fold[np.arange(ATT_RING), _ring_index()] = 1.0
    fold = jnp.asarray(fold, BF16)

    def body(d_ref, f_ref, o_ref):
        x = jnp.sum(d_ref[...], axis=0, keepdims=True)
        x = jnp.broadcast_to(x, (8, ATT_RING))
        acc = jnp.zeros((8, REL_PAD), F32)
        for _ in range(3):
            part = x.astype(BF16)
            acc = acc + _dot(part, f_ref[...])
            x = x - part.astype(F32)
        o_ref[...] = acc[0:1, :]

    out = _call(body, name=name, grid=(ATT_HEADS,),
                in_specs=[pl.BlockSpec((None, 8, ATT_RING), lambda h: (h, 0, 0)),
                          pl.BlockSpec((ATT_RING, REL_PAD), lambda h: (0, 0))],
                out_specs=[pl.BlockSpec((None, 1, REL_PAD), lambda h: (h, 0, 0))],
                out_shape=[jax.ShapeDtypeStruct((ATT_HEADS, 1, REL_PAD), F32)],
                args=(dring.reshape(ATT_HEADS, 8, ATT_RING), fold))[0]
    return out.reshape(ATT_HEADS, REL_PAD)


def _adamw(w, g, m, v):
    m = ADAM_B1 * m + (1.0 - ADAM_B1) * g
    v = ADAM_B2 * v + (1.0 - ADAM_B2) * (g * g)
    m_hat = m / (1.0 - ADAM_B1 ** ADAM_STEP)
    v_hat = v / (1.0 - ADAM_B2 ** ADAM_STEP)
    delta = -ADAM_LR * (m_hat / (jnp.sqrt(v_hat) + ADAM_EPS) + ADAM_WD * w)
    return delta, m, v


def _sum_devices(ref):
    g = ref[0].astype(F32)
    for d in range(1, N_DEV):
        g = g + ref[d].astype(F32)
    return g


def _adamw_reduce(lands, w, m, v, name, tr=256):
    L, R, C = w.shape
    tr = min(tr, R)
    n_i = R // tr

    def body(*refs):
        l_refs = refs[:L]
        w_ref, m_ref, v_ref, g_out, d_out, m_out, v_out = refs[L:]
        layer = pl.program_id(0)
        for l in range(L):
            @pl.when(layer == l)
            def _(l=l):
                g = _sum_devices(l_refs[l])
                g_out[...] = g
                d_out[...], m_out[...], v_out[...] = _adamw(w_ref[...], g, m_ref[...], v_ref[...])

    def land_spec(l):
        return pl.BlockSpec((N_DEV, tr, C), lambda ly, i: (0, jnp.where(ly == l, i, jnp.where(ly < l, 0, n_i - 1)), 0))

    blk = pl.BlockSpec((None, tr, C), lambda ly, i: (ly, i, 0))
    return _call(body, name=name, grid=(L, n_i), in_specs=[land_spec(l) for l in range(L)] + [blk, blk, blk],
                 out_specs=[blk] * 4, out_shape=[jax.ShapeDtypeStruct((L, R, C), F32)] * 4, args=(*lands, w, m, v))


def _small_update(land_small, land_rel, w, m, v, name):
    def body(ls_ref, lr_ref, w_ref, m_ref, v_ref, g_out, d_out, m_out, v_out, rel_out):
        g = _sum_devices(ls_ref)
        g_out[...] = g
        d_out[...], m_out[...], v_out[...] = _adamw(w_ref[...], g, m_ref[...], v_ref[...])
        rel_out[...] = _sum_devices(lr_ref)

    return pl.pallas_call(
        body, name=name,
        out_shape=[jax.ShapeDtypeStruct(w.shape, F32)] * 4 + [jax.ShapeDtypeStruct(land_rel.shape[1:], F32)],
    )(land_small, land_rel, w, m, v)


def _adamw_plain(w, g, m, v, name):
    def body(w_ref, g_ref, m_ref, v_ref, d_out, m_out, v_out):
        d_out[...], m_out[...], v_out[...] = _adamw(w_ref[...], g_ref[...], m_ref[...], v_ref[...])

    return pl.pallas_call(body, name=name, out_shape=[jax.ShapeDtypeStruct(w.shape, F32)] * 3)(w, g, m, v)


def _per_head_columns(w):
    q, k, v, gate = w[:, :RET_QK], w[:, RET_QK:2 * RET_QK], w[:, 2 * RET_QK:2 * RET_QK + RET_VW], w[:, 2 * RET_QK + RET_VW:]
    return jnp.concatenate(
        [t[:, h * d:(h + 1) * d] for h in range(RET_HEADS) for t, d in ((q, RET_DK), (k, RET_DK), (v, RET_DV), (gate, RET_DV))],
        axis=1)


def _ret_piece_of_column_block(p):
    per_head = RET_HEAD_COLS // RET_DK
    qk = jnp.where(p < RET_HEADS, per_head * p, per_head * (p - RET_HEADS) + 1)
    pv = p - 2 * RET_HEADS
    vv = per_head * (pv // 2) + 2 + pv % 2
    pg = p - 4 * RET_HEADS
    gg = per_head * (pg // 2) + 4 + pg % 2
    return jnp.where(p < 2 * RET_HEADS, qk, jnp.where(p < 4 * RET_HEADS, vv, gg))


def _gather(shard, full_shape, view):
    return _Exchange([shard], [(0, _whole, 0, view)], [jax.ShapeDtypeStruct(full_shape, shard.dtype)])


def _step(x, target, shards, mix_g, gn_g, rel_shard, mlp_g, final_g):
    s_rin, s_rout, s_ain, s_aout, s_w1, s_w2 = shards
    B, S, _ = x.shape
    T = B * S
    tk = min(T, 2048)
    n_k = T // tk
    h0 = x.reshape(T, D_MODEL)
    tgt = target.reshape(T, D_MODEL)
    consts = _ret_constants(S)
    w1_cols = w2_rows = D_FF // N_DEV

    rel_cols = REL_TABLE // N_DEV
    w_rin, rel_all = _all_gather(
        [s_rin, rel_shard],
        [jax.ShapeDtypeStruct((D_MODEL, RET_IN), BF16), jax.ShapeDtypeStruct((N_DEV, ATT_HEADS, rel_cols), F32)],
        [_cols(RET_IN // N_DEV), _slot], "gather_first")
    ring = _bias_ring(rel_all.transpose(1, 0, 2).reshape(ATT_HEADS, REL_TABLE))
    w_rin_heads = _per_head_columns(w_rin)

    ride = _Exchange([s_rout, s_w1[0]], [(0, _whole, 0, _rows(RET_VW // N_DEV)), (1, _whole, 1, _cols(w1_cols))],
                     [jax.ShapeDtypeStruct((RET_VW, D_MODEL), BF16), jax.ShapeDtypeStruct((D_MODEL, D_FF), BF16)])
    (proj, n0), (w_rout, w1_0) = _norm_proj(h0, mix_g[0:1], w_rin_heads, 1, "ret_proj", 512, ride=ride)
    ride = _Exchange([s_w2[0], s_ain], [(0, _whole, 0, _rows(w2_rows)), (1, _whole, 1, _cols(3 * D_MODEL // N_DEV))],
                     [jax.ShapeDtypeStruct((D_FF, D_MODEL), BF16), jax.ShapeDtypeStruct((D_MODEL, 3 * D_MODEL), BF16)])
    (y, o, states), (w2_0, w_ain) = _ret_fwd(proj, consts, gn_g, B, S, "ret_fwd", ride=ride)
    h1 = _matmul_res(y, w_rout, h0, "ret_out")
    ride = _Exchange([s_aout, s_w1[1]], [(0, _whole, 0, _rows(D_MODEL // N_DEV)), (1, _whole, 1, _cols(w1_cols))],
                     [jax.ShapeDtypeStruct((D_MODEL, D_MODEL), BF16), jax.ShapeDtypeStruct((D_MODEL, D_FF), BF16)])
    (h2, n1, u1), (w_aout, w1_1) = _mlp_fwd(h1, mlp_g[0:1], w1_0, w2_0, "mlp0_fwd", ride=ride)
    qkv3, n2 = _norm_proj(h2, mix_g[1:2], w_ain, 3, "att_proj", 1024)
    (o2,), (w2_1,) = _attn_fwd(qkv3, ring, B, S, "att_fwd", ride=_gather(s_w2[1], (D_FF, D_MODEL), _rows(w2_rows)))
    h3 = _matmul_res(o2, w_aout, h2, "att_out")
    h4, n3, u3 = _mlp_fwd(h3, mlp_g[1:2], w1_1, w2_1, "mlp1_fwd")
    dh4, dh4b, loss, dg_final = _final_loss(h4, final_g, tgt, "final_loss")

    def tok(width):
        return dict(a=pl.BlockSpec((tk, width), lambda i, j, k: (k, i)), b=pl.BlockSpec((tk, width), lambda i, j, k: (k, j)))

    def mlp_grads(dh, dhb, u, n, w1, w2, h, g, tag, ride=None):
        res = _mlp_bwd(dh, u, w1, w2, h, g, tag + "_bwd", ride=ride)
        (dhi, dhib, du, dg), lands = res if ride is not None else (res, None)
        gw2 = _wgrad(u, tok(1024)["a"], dhb, tok(1024)["b"], D_FF, D_MODEL, 1024, 1024, n_k, tag + "_dw2", square_a=True)
        gw1 = _wgrad(n, tok(1024)["a"], du, tok(1024)["b"], D_MODEL, D_FF, 1024, 1024, n_k, tag + "_dw1")
        return dhi, dhib, gw1, gw2, dg, lands

    dh3, dh3b, gw1_1, gw2_1, dg_mlp1, _ = mlp_grads(dh4, dh4b, u3, n3, w1_1, w2_1, h3, mlp_g[1:2], "mlp1")
    do2 = _matmul_nt(dh3b, w_aout, "att_out_bwd")
    g_aout = _wgrad(o2, tok(1024)["a"], dh3b, tok(1024)["b"], D_MODEL, D_MODEL, 1024, 1024, n_k, "att_out_dw")
    ride = _Exchange(
        [gw1_1, gw2_1, g_aout],
        [(0, _cols(w1_cols), 0, _slot), (1, _rows(w2_rows), 1, _slot), (2, _rows(D_MODEL // N_DEV), 2, _slot)],
        [jax.ShapeDtypeStruct((N_DEV, D_MODEL, w1_cols), BF16), jax.ShapeDtypeStruct((N_DEV, w2_rows, D_MODEL), BF16),
         jax.ShapeDtypeStruct((N_DEV, D_MODEL // N_DEV, D_MODEL), BF16)])
    (dqkv3, dring), (l_w1_1, l_w2_1, l_aout) = _attn_bwd(qkv3, ring, do2, B, S, "att_bwd", ride=ride)
    g_rel = _rel_bias_grad(dring, "rel_bias_grad")
    dh2, dh2b, dg_mix1 = _proj_bwd(dqkv3, w_ain, h2, mix_g[1:2], dh3, "att_proj_bwd", 1024)
    g_ain = _wgrad(n2, tok(1024)["a"], dqkv3, pl.BlockSpec((None, tk, D_MODEL), lambda i, j, k: (j, k, 0)),
                   D_MODEL, 3 * D_MODEL, 1024, 1024, n_k, "att_proj_dw")
    ride = _Exchange([g_ain], [(0, _cols(3 * D_MODEL // N_DEV), 0, _slot)],
                     [jax.ShapeDtypeStruct((N_DEV, D_MODEL, 3 * D_MODEL // N_DEV), BF16)])
    dh1, dh1b, gw1_0, gw2_0, dg_mlp0, (l_ain,) = mlp_grads(dh2, dh2b, u1, n1, w1_0, w2_0, h1, mlp_g[0:1], "mlp0", ride=ride)
    dy = _matmul_nt(dh1b, w_rout, "ret_out_bwd")
    g_rout = _wgrad(y, tok(1024)["a"], dh1b, tok(1024)["b"], RET_VW, D_MODEL, 1024, 1024, n_k, "ret_out_dw")
    ride = _Exchange(
        [gw1_0, gw2_0, g_rout],
        [(0, _cols(w1_cols), 0, _slot), (1, _rows(w2_rows), 1, _slot), (2, _rows(RET_VW // N_DEV), 2, _slot)],
        [jax.ShapeDtypeStruct((N_DEV, D_MODEL, w1_cols), BF16), jax.ShapeDtypeStruct((N_DEV, w2_rows, D_MODEL), BF16),
         jax.ShapeDtypeStruct((N_DEV, RET_VW // N_DEV, D_MODEL), BF16)])
    (dproj, dgn), (l_w1_0, l_w2_0, l_rout) = _ret_bwd(proj, consts, gn_g, o, states, dy, B, S, "ret_bwd", ride=ride)
    per_shard = RET_IN // N_DEV // RET_DK
    g_rin = _wgrad(n0, tok(1024)["a"], dproj,
                   [pl.BlockSpec((tk, RET_DK), functools.partial(
                       lambda i, j, k, r: (k, _ret_piece_of_column_block(per_shard * j + r)), r=r)) for r in range(per_shard)],
                   D_MODEL, RET_IN, 1024, per_shard * RET_DK, n_k, "ret_proj_dw")
    ride = _Exchange([g_rin], [(0, _cols(RET_IN // N_DEV), 0, _slot)],
                     [jax.ShapeDtypeStruct((N_DEV, D_MODEL, RET_IN // N_DEV), BF16)])
    (dx, _, dg_mix0), (l_rin,) = _proj_bwd(dproj, w_rin_heads, h0, mix_g[0:1], dh1, "ret_proj_bwd", 512, ride=ride)

    small = jnp.concatenate([dg_mix0, dg_mix1, dg_mlp0, dg_mlp1, dg_final, dgn.reshape(2, D_MODEL),
                             jnp.zeros((1, D_MODEL), F32)], axis=0)
    l_small, l_rel = _Exchange(
        [small, g_rel], [(0, _whole, 0, _slot), (1, _whole, 1, _slot)],
        [jax.ShapeDtypeStruct((N_DEV, 8, D_MODEL), F32), jax.ShapeDtypeStruct((N_DEV, ATT_HEADS, REL_PAD), F32)],
    ).run("scatter_small")
    lands = dict(rin=l_rin, rout=l_rout, ain=l_ain, aout=l_aout, w1=(l_w1_0, l_w1_1), w2=(l_w2_0, l_w2_1),
                 small=l_small, rel=l_rel)
    return loss[0, 0], dx.reshape(B, S, D_MODEL), lands


def kernel(x, mix_norm_g, ret_w_in, ret_gn_g, ret_w_out, att_w_in, att_rel_bias, att_w_out, mlp_norm_g, mlp_w1, mlp_w2, final_norm_g, loss_target, m_mix_norm_g, m_ret_w_in, m_ret_gn_g, m_ret_w_out, m_att_w_in, m_att_rel_bias, m_att_w_out, m_mlp_norm_g, m_mlp_w1, m_mlp_w2, m_final_norm_g, v_mix_norm_g, v_ret_w_in, v_ret_gn_g, v_ret_w_out, v_att_w_in, v_att_rel_bias, v_att_w_out, v_mlp_norm_g, v_mlp_w1, v_mlp_w2, v_final_norm_g):
    me = _lin(_place())
    rel_cols = REL_TABLE // N_DEV
    shards = (ret_w_in[0].astype(BF16), ret_w_out[0].astype(BF16), att_w_in[0].astype(BF16), att_w_out[0].astype(BF16),
              (mlp_w1[0].astype(BF16), mlp_w1[1].astype(BF16)), (mlp_w2[0].astype(BF16), mlp_w2[1].astype(BF16)))
    loss_part, grad_x, lands = _step(x, loss_target, shards, mix_norm_g, ret_gn_g, att_rel_bias[0], mlp_norm_g,
                                     final_norm_g.reshape(1, D_MODEL))
    loss = lax.psum(loss_part, ("x", "y", "c"))

    u_rin = _adamw_reduce([lands["rin"]], ret_w_in, m_ret_w_in, v_ret_w_in, "update_ret_w_in")
    u_rout = _adamw_reduce([lands["rout"]], ret_w_out, m_ret_w_out, v_ret_w_out, "update_ret_w_out")
    u_ain = _adamw_reduce([lands["ain"]], att_w_in, m_att_w_in, v_att_w_in, "update_att_w_in")
    u_aout = _adamw_reduce([lands["aout"]], att_w_out, m_att_w_out, v_att_w_out, "update_att_w_out")
    u_w1 = _adamw_reduce(lands["w1"], mlp_w1, m_mlp_w1, v_mlp_w1, "update_mlp_w1")
    u_w2 = _adamw_reduce(lands["w2"], mlp_w2, m_mlp_w2, v_mlp_w2, "update_mlp_w2")

    def pack(mix, mlp, fin, gn):
        return jnp.concatenate([mix, mlp, fin.reshape(1, D_MODEL), gn.reshape(2, D_MODEL), jnp.zeros((1, D_MODEL), F32)], axis=0)

    small_w = pack(mix_norm_g, mlp_norm_g, final_norm_g, ret_gn_g)
    small_m = pack(m_mix_norm_g, m_mlp_norm_g, m_final_norm_g, m_ret_gn_g)
    small_v = pack(v_mix_norm_g, v_mlp_norm_g, v_final_norm_g, v_ret_gn_g)
    sg, sd, sm, sv, rel_sum = _small_update(lands["small"], lands["rel"], small_w, small_m, small_v, "update_small")
    g_rel_mine = lax.dynamic_slice(rel_sum, (0, me * rel_cols), (ATT_HEADS, rel_cols))
    rel_d, rel_m, rel_v = _adamw_plain(att_rel_bias[0], g_rel_mine, m_att_rel_bias[0], v_att_rel_bias[0], "update_rel_bias")
    u_rel = [g_rel_mine[None], rel_d[None], rel_m[None], rel_v[None]]

    def unpack(t):
        return dict(mix=t[0:2], mlp=t[2:4], fin=t[4], gn=t[5:7].reshape(1, RET_VW))

    us = [unpack(t) for t in (sg, sd, sm, sv)]
    outs = [loss, grad_x]
    for k in range(4):
        outs += [us[k]["mix"], u_rin[k], us[k]["gn"], u_rout[k], u_ain[k], u_rel[k], u_aout[k], us[k]["mlp"],
                 u_w1[k], u_w2[k], us[k]["fin"]]
    return tuple(outs)
```

```python
import functools

import numpy as np
import jax
import jax.numpy as jnp
from jax import lax
from jax.experimental import pallas as pl
from jax.experimental.pallas import tpu as pltpu

F32, BF16 = jnp.float32, jnp.bfloat16

D_MODEL = 1024
CHUNK = 64
RET_HEADS, RET_DK, RET_DV = 4, 256, 512
RET_QK, RET_VW = RET_HEADS * RET_DK, RET_HEADS * RET_DV
RET_IN = 2 * RET_QK + 2 * RET_VW
RET_HEAD_COLS = 2 * RET_DK + 2 * RET_DV
RET_SC = 256
ROPE_BASE = 10000.0
ATT_HEADS, ATT_DH = 16, 64
LANES = 128
ATT_PAIRS = ATT_HEADS * ATT_DH // LANES
BAND_PAST = 8 * CHUNK
MAX_REL = 256
REL_TABLE = MAX_REL + CHUNK
REL_PAD = 384
ATT_CQ = 256
ATT_KW = ATT_CQ + BAND_PAST
ATT_RING = ATT_CQ + ATT_KW
ATT_VARIANTS = BAND_PAST // ATT_CQ + 1
D_FF = 4 * D_MODEL
EPS = 1e-6
EPILOGUE_ROWS = 256
NEG = -1e30
N_DEV = 8
N_PEERS = N_DEV - 1

ADAM_LR, ADAM_B1, ADAM_B2, ADAM_EPS, ADAM_WD, ADAM_STEP = 0.001, 0.9, 0.999, 1e-08, 0.01, 10

VMEM_LIMIT = 56 * 1024 * 1024
MESH = pl.DeviceIdType.MESH
ANY = pl.BlockSpec(memory_space=pl.ANY)


def _dot(a, b):
    return jnp.dot(a, b, preferred_element_type=F32)


def _dot_nt(a, b):
    return lax.dot_general(a, b, (((1,), (1,)), ((), ())), preferred_element_type=F32)


def _dot_tn(a, b):
    return lax.dot_general(a, b, (((0,), (0,)), ((), ())), preferred_element_type=F32)


def _rms(x, g):
    r = lax.rsqrt(jnp.mean(x * x, axis=-1, keepdims=True) + EPS)
    return x * r * g


def _rms_bwd(dn, x, g):
    r = lax.rsqrt(jnp.mean(x * x, axis=-1, keepdims=True) + EPS)
    xh = x * r
    dg = jnp.sum(dn * xh, axis=0, keepdims=True)
    dxh = dn * g
    dx = r * (dxh - xh * jnp.mean(dxh * xh, axis=-1, keepdims=True))
    return dx, dg


def _rms_bwd_epilogue(x_ref, g_ref, dres_ref, dx_ref, dxb_ref, dg_ref):
    for r in range(0, dx_ref.shape[0], EPILOGUE_ROWS):
        rows = slice(r, r + EPILOGUE_ROWS)
        dx, dg = _rms_bwd(dx_ref[rows, :], x_ref[rows, :], g_ref[...])
        dx = dres_ref[rows, :] + dx
        dx_ref[rows, :] = dx
        dxb_ref[rows, :] = dx.astype(BF16)
        dg_ref[...] += dg


def _place():
    return lax.axis_index("x"), lax.axis_index("y"), lax.axis_index("c")


def _lin(p):
    return 4 * p[0] + 2 * p[1] + p[2]


def _cols(width):
    return lambda ref, blk: ref.at[:, pl.ds(pl.multiple_of(blk * width, LANES), width)]


def _rows(height):
    return lambda ref, blk: ref.at[pl.ds(pl.multiple_of(blk * height, 8), height), :]


def _slot(ref, blk):
    return ref.at[blk]


def _whole(ref, blk):
    return ref


class _Exchange:
    def __init__(self, sources, flows, land_shapes):
        self.sources, self.flows, self.land_shapes = list(sources), flows, list(land_shapes)
        n_f = len(flows)
        self.sem_shapes = [pltpu.SemaphoreType.DMA((N_PEERS * n_f,)), pltpu.SemaphoreType.DMA((N_PEERS * n_f,)),
                           pltpu.SemaphoreType.DMA((n_f,))]

    def _copies(self, srcs, lands, sems):
        send_sems, recv_sems, local_sems = sems
        x, y, c = _place()
        me = (x, y, c)
        peers = [(x ^ dx, y ^ dy, c ^ dc) for dx in (0, 1) for dy in (0, 1) for dc in (0, 1)][1:]

        def copy(f, k, sender, to):
            si, sview, li, lview = self.flows[f]
            return pltpu.make_async_remote_copy(
                src_ref=sview(srcs[si], _lin(to)), dst_ref=lview(lands[li], _lin(sender)),
                send_sem=send_sems.at[f * N_PEERS + k], recv_sem=recv_sems.at[f * N_PEERS + k],
                device_id=to, device_id_type=MESH)

        mine, sends, recvs = [], [], []
        for f, (si, sview, li, lview) in enumerate(self.flows):
            mine.append(pltpu.make_async_copy(sview(srcs[si], _lin(me)), lview(lands[li], _lin(me)), local_sems.at[f]))
            for k, peer in enumerate(peers):
                sends.append(copy(f, k, me, peer))
                recvs.append(copy(f, k, peer, me))
        return mine, sends, recvs

    def start(self, srcs, lands, sems):
        mine, sends, _ = self._copies(srcs, lands, sems)
        for cp in mine + sends:
            cp.start()

    def finish(self, srcs, lands, sems):
        mine, sends, recvs = self._copies(srcs, lands, sems)
        for cp in recvs:
            cp.wait_recv()
        for cp in sends:
            cp.wait_send()
        for cp in mine:
            cp.wait()

    def run(self, name):
        n_src, n_land = len(self.sources), len(self.land_shapes)

        def body(*refs):
            srcs, lands, sems = refs[:n_src], refs[n_src:n_src + n_land], refs[n_src + n_land:]
            self.start(srcs, lands, sems)
            self.finish(srcs, lands, sems)

        return pl.pallas_call(body, name=name, in_specs=[ANY] * n_src, out_specs=[ANY] * n_land,
                              out_shape=self.land_shapes, scratch_shapes=self.sem_shapes)(*self.sources)


def _call(body, *, name, grid, in_specs, out_specs, out_shape, args, scratch_shapes=(), ride=None):
    params = pltpu.CompilerParams(dimension_semantics=("arbitrary",) * len(grid), vmem_limit_bytes=VMEM_LIMIT)
    in_specs, out_specs, out_shape, scratch_shapes = list(in_specs), list(out_specs), list(out_shape), list(scratch_shapes)
    if ride is None:
        return pl.pallas_call(body, name=name, grid=grid, in_specs=in_specs, out_specs=out_specs, out_shape=out_shape,
                              scratch_shapes=scratch_shapes, compiler_params=params)(*args)
    n_in, n_out, n_scr = len(in_specs), len(out_specs), len(scratch_shapes)
    n_src, n_land = len(ride.sources), len(ride.land_shapes)

    def riding(*refs):
        bounds = np.cumsum([n_in, n_src, n_out, n_land, n_scr])
        ins, srcs, outs, lands, scr, sems = (refs[a:b] for a, b in zip([0, *bounds], [*bounds, len(refs)]))
        ids = [pl.program_id(d) for d in range(len(grid))]
        first = functools.reduce(lambda a, b: a & b, [i == 0 for i in ids])
        last = functools.reduce(lambda a, b: a & b, [i == n - 1 for i, n in zip(ids, grid)])

        @pl.when(first)
        def _():
            ride.start(srcs, lands, sems)

        body(*ins, *outs, *scr)

        @pl.when(last)
        def _():
            ride.finish(srcs, lands, sems)

    res = pl.pallas_call(
        riding, name=name, grid=grid, in_specs=in_specs + [ANY] * n_src, out_specs=out_specs + [ANY] * n_land,
        out_shape=out_shape + ride.land_shapes, scratch_shapes=scratch_shapes + ride.sem_shapes,
        compiler_params=params)(*args, *ride.sources)
    return res[:n_out], res[n_out:]


def _all_gather(shards, out_shapes, views, name):
    n = len(shards)

    def body(*refs):
        ins, outs = refs[:n], refs[n:2 * n]
        send_sems, recv_sems, local_sems = refs[2 * n:]
        x, y, c = _place()
        me, sibling = (x, y, c), (x, y, 1 - c)
        chips = [(1 - x, y), (x, 1 - y), (1 - x, 1 - y)]

        def copy(a, k, block, to, src=None):
            dst = views[a](outs[a], _lin(block))
            return pltpu.make_async_remote_copy(
                src_ref=dst if src is None else src, dst_ref=dst,
                send_sem=send_sems.at[a * N_PEERS + k], recv_sem=recv_sems.at[a * N_PEERS + k],
                device_id=to, device_id_type=MESH)

        mine = [pltpu.make_async_copy(ins[a], views[a](outs[a], _lin(me)), local_sems.at[a]) for a in range(n)]
        for cp in mine:
            cp.start()
        first = []
        for a in range(n):
            first.append(copy(a, 0, me, sibling, src=ins[a]))
            first += [copy(a, 1 + j, me, (*chip, c), src=ins[a]) for j, chip in enumerate(chips)]
        for cp in first:
            cp.start()
        passed = []
        for j, chip in enumerate(chips):
            for a in range(n):
                copy(a, 1 + j, (*chip, c), me).wait_recv()
                fwd = copy(a, 4 + j, (*chip, c), sibling)
                fwd.start()
                passed.append(fwd)
        for a in range(n):
            copy(a, 0, sibling, me).wait_recv()
            for j, chip in enumerate(chips):
                copy(a, 4 + j, (*chip, 1 - c), me).wait_recv()
        for cp in first + passed:
            cp.wait_send()
        for cp in mine:
            cp.wait()

    return pl.pallas_call(
        body, name=name,
        in_specs=[ANY] * n, out_specs=[ANY] * n, out_shape=out_shapes,
        scratch_shapes=[pltpu.SemaphoreType.DMA((N_PEERS * n,)), pltpu.SemaphoreType.DMA((N_PEERS * n,)),
                        pltpu.SemaphoreType.DMA((n,))],
    )(*shards)


def _resident(shape):
    return pl.BlockSpec(shape, lambda i: (0,) * len(shape), pipeline_mode=pl.Buffered(1))


def _norm_proj(h, g, w, groups, name, tm, ride=None):
    T = h.shape[0]
    N = w.shape[1]
    width = N // groups

    def body(h_ref, g_ref, w_ref, o_ref, n_ref):
        n = _rms(h_ref[...], g_ref[...]).astype(BF16)
        n_ref[...] = n
        if groups == 1:
            o_ref[...] = _dot(n, w_ref[...]).astype(BF16)
        else:
            for p in range(groups):
                o_ref[p] = _dot(n, w_ref[:, p * width:(p + 1) * width]).astype(BF16)

    row = pl.BlockSpec((tm, D_MODEL), lambda i: (i, 0))
    if groups == 1:
        out_spec, out_shape = pl.BlockSpec((tm, N), lambda i: (i, 0)), jax.ShapeDtypeStruct((T, N), BF16)
    else:
        out_spec = pl.BlockSpec((groups, tm, width), lambda i: (0, i, 0))
        out_shape = jax.ShapeDtypeStruct((groups, T, width), BF16)
    return _call(body, name=name, grid=(T // tm,),
                 in_specs=[row, pl.BlockSpec((1, D_MODEL), lambda i: (0, 0)), _resident(w.shape)],
                 out_specs=[out_spec, row], out_shape=[out_shape, jax.ShapeDtypeStruct((T, D_MODEL), BF16)],
                 args=(h, g, w), ride=ride)


def _proj_bwd(dy, w, x, g, dres, name, tm, ride=None):
    T = x.shape[0]
    groups = dy.shape[0] if dy.ndim == 3 else 1
    width = w.shape[1] // groups

    def body(dy_ref, w_ref, x_ref, g_ref, dres_ref, dx_ref, dxb_ref, dg_ref):
        @pl.when(pl.program_id(0) == 0)
        def _():
            dg_ref[...] = jnp.zeros_like(dg_ref)

        if groups == 1:
            dn = _dot_nt(dy_ref[...], w_ref[...])
        else:
            dn = sum(_dot_nt(dy_ref[p], w_ref[:, p * width:(p + 1) * width]) for p in range(groups))
        dx_ref[...] = dn
        _rms_bwd_epilogue(x_ref, g_ref, dres_ref, dx_ref, dxb_ref, dg_ref)

    row = pl.BlockSpec((tm, D_MODEL), lambda i: (i, 0))
    vec = pl.BlockSpec((1, D_MODEL), lambda i: (0, 0))
    dy_spec = (pl.BlockSpec((tm, w.shape[1]), lambda i: (i, 0)) if groups == 1
               else pl.BlockSpec((groups, tm, width), lambda i: (0, i, 0)))
    return _call(body, name=name, grid=(T // tm,),
                 in_specs=[dy_spec, _resident(w.shape), row, vec, row], out_specs=[row, row, vec],
                 out_shape=[jax.ShapeDtypeStruct((T, D_MODEL), F32), jax.ShapeDtypeStruct((T, D_MODEL), BF16),
                            jax.ShapeDtypeStruct((1, D_MODEL), F32)],
                 args=(dy, w, x, g, dres), ride=ride)


def _matmul_res(a, w, res, name, tm=1024):
    T, K = a.shape

    def body(a_ref, w_ref, r_ref, o_ref):
        o_ref[...] = r_ref[...] + _dot(a_ref[...], w_ref[...])

    row = pl.BlockSpec((tm, D_MODEL), lambda i: (i, 0))
    return _call(body, name=name, grid=(T // tm,),
                 in_specs=[pl.BlockSpec((tm, K), lambda i: (i, 0)), pl.BlockSpec((K, D_MODEL), lambda i: (0, 0)), row],
                 out_specs=[row], out_shape=[jax.ShapeDtypeStruct((T, D_MODEL), F32)], args=(a, w, res))[0]


def _matmul_nt(dy, w, name, tm=1024):
    T, N = dy.shape
    K = w.shape[0]

    def body(dy_ref, w_ref, o_ref):
        o_ref[...] = _dot_nt(dy_ref[...], w_ref[...]).astype(BF16)

    return _call(body, name=name, grid=(T // tm,),
                 in_specs=[pl.BlockSpec((tm, N), lambda i: (i, 0)), pl.BlockSpec((K, N), lambda i: (0, 0))],
                 out_specs=[pl.BlockSpec((tm, K), lambda i: (i, 0))], out_shape=[jax.ShapeDtypeStruct((T, K), BF16)],
                 args=(dy, w))[0]


def _wgrad(a, a_spec, b, b_specs, m, n, bm, bn, n_k, name, square_a=False, ride=None):
    b_specs = b_specs if isinstance(b_specs, (list, tuple)) else [b_specs]
    n_b = len(b_specs)

    def body(a_ref, *rest):
        b_refs = rest[:n_b]
        o_ref, acc = rest[n_b:]
        k = pl.program_id(2)

        @pl.when(k == 0)
        def _():
            acc[...] = jnp.zeros_like(acc)

        av = a_ref[...]
        if square_a:
            af = av.astype(F32)
            av = (af * af).astype(BF16)
        bv = b_refs[0][...] if n_b == 1 else jnp.concatenate([r[...] for r in b_refs], axis=1)
        acc[...] += _dot_tn(av, bv)

        @pl.when(k == n_k - 1)
        def _():
            o_ref[...] = acc[...].astype(BF16)

    res = _call(body, name=name, grid=(m // bm, n // bn, n_k), in_specs=[a_spec, *b_specs],
                out_specs=[pl.BlockSpec((bm, bn), lambda i, j, k: (i, j))], out_shape=[jax.ShapeDtypeStruct((m, n), BF16)],
                scratch_shapes=[pltpu.VMEM((bm, bn), F32)], args=(a, *([b] * n_b)), ride=ride)
    return res[0] if ride is None else (res[0][0], res[1])


def _mlp_fwd(h, g, w1, w2, name, tm=1024, tf=1024, ride=None):
    T = h.shape[0]

    def body(h_ref, g_ref, w1_ref, w2_ref, ho_ref, n_ref, u_ref):
        @pl.when(pl.program_id(1) == 0)
        def _():
            n_ref[...] = _rms(h_ref[...], g_ref[...]).astype(BF16)
            ho_ref[...] = h_ref[...]

        u = jnp.maximum(_dot(n_ref[...], w1_ref[...]), 0.0)
        u_ref[...] = u.astype(BF16)
        ho_ref[...] += _dot((u * u).astype(BF16), w2_ref[...])

    row = pl.BlockSpec((tm, D_MODEL), lambda i, j: (i, 0))
    return _call(body, name=name, grid=(T // tm, D_FF // tf),
                 in_specs=[row, pl.BlockSpec((1, D_MODEL), lambda i, j: (0, 0)),
                           pl.BlockSpec((D_MODEL, tf), lambda i, j: (0, j)), pl.BlockSpec((tf, D_MODEL), lambda i, j: (j, 0))],
                 out_specs=[row, row, pl.BlockSpec((tm, tf), lambda i, j: (i, j))],
                 out_shape=[jax.ShapeDtypeStruct((T, D_MODEL), F32), jax.ShapeDtypeStruct((T, D_MODEL), BF16),
                            jax.ShapeDtypeStruct((T, D_FF), BF16)],
                 args=(h, g, w1, w2), ride=ride)


def _mlp_bwd(dh, u, w1, w2, h, g, name, tm=1024, tf=1024, ride=None):
    T = h.shape[0]
    n_j = D_FF // tf

    def body(dh_ref, u_ref, w1_ref, w2_ref, h_ref, g_ref, dx_ref, dxb_ref, du_ref, dg_ref, dhb):
        acc = dx_ref
        i, j = pl.program_id(0), pl.program_id(1)

        @pl.when(j == 0)
        def _():
            dhb[...] = dh_ref[...].astype(BF16)
            acc[...] = jnp.zeros_like(acc)

        @pl.when((i == 0) & (j == 0))
        def _():
            dg_ref[...] = jnp.zeros_like(dg_ref)

        da = _dot_nt(dhb[...], w2_ref[...])
        du = (da * (2.0 * u_ref[...].astype(F32))).astype(BF16)
        du_ref[...] = du
        acc[...] += _dot_nt(du, w1_ref[...])

        @pl.when(j == n_j - 1)
        def _():
            _rms_bwd_epilogue(h_ref, g_ref, dh_ref, dx_ref, dxb_ref, dg_ref)

    row = pl.BlockSpec((tm, D_MODEL), lambda i, j: (i, 0))
    vec = pl.BlockSpec((1, D_MODEL), lambda i, j: (0, 0))
    hid = pl.BlockSpec((tm, tf), lambda i, j: (i, j))
    return _call(body, name=name, grid=(T // tm, n_j),
                 in_specs=[row, hid, pl.BlockSpec((D_MODEL, tf), lambda i, j: (0, j)),
                           pl.BlockSpec((tf, D_MODEL), lambda i, j: (j, 0)), row, vec],
                 out_specs=[row, row, hid, vec],
                 out_shape=[jax.ShapeDtypeStruct((T, D_MODEL), F32), jax.ShapeDtypeStruct((T, D_MODEL), BF16),
                            jax.ShapeDtypeStruct((T, D_FF), BF16), jax.ShapeDtypeStruct((1, D_MODEL), F32)],
                 scratch_shapes=[pltpu.VMEM((tm, D_MODEL), BF16)], args=(dh, u, w1, w2, h, g), ride=ride)


def _final_loss(h, g, target, name, tm=512):
    T = h.shape[0]

    def body(h_ref, g_ref, t_ref, dh_ref, dhb_ref, loss_ref, dg_ref):
        @pl.when(pl.program_id(0) == 0)
        def _():
            loss_ref[...] = jnp.zeros_like(loss_ref)
            dg_ref[...] = jnp.zeros_like(dg_ref)

        x = h_ref[...]
        gg = g_ref[...]
        r = lax.rsqrt(jnp.mean(x * x, axis=-1, keepdims=True) + EPS)
        xh = x * r
        e = xh * gg - t_ref[...]
        per_tok = jnp.mean(e * e, axis=-1, keepdims=True)
        loss_ref[...] += 0.5 * jnp.sum(per_tok, axis=0, keepdims=True)
        dy = e * (1.0 / D_MODEL)
        dg_ref[...] += jnp.sum(dy * xh, axis=0, keepdims=True)
        dxh = dy * gg
        dx = r * (dxh - xh * jnp.mean(dxh * xh, axis=-1, keepdims=True))
        dh_ref[...] = dx
        dhb_ref[...] = dx.astype(BF16)

    row = pl.BlockSpec((tm, D_MODEL), lambda i: (i, 0))
    vec = pl.BlockSpec((1, D_MODEL), lambda i: (0, 0))
    return _call(body, name=name, grid=(T // tm,), in_specs=[row, vec, row],
                 out_specs=[row, row, pl.BlockSpec((8, LANES), lambda i: (0, 0)), vec],
                 out_shape=[jax.ShapeDtypeStruct((T, D_MODEL), F32), jax.ShapeDtypeStruct((T, D_MODEL), BF16),
                            jax.ShapeDtypeStruct((8, LANES), F32), jax.ShapeDtypeStruct((1, D_MODEL), F32)],
                 args=(h, g, target))


def _ret_constants(S):
    log_gamma = jnp.log1p(-jnp.exp2(-5.0 - jnp.arange(RET_HEADS, dtype=F32)))
    idx = jnp.arange(RET_SC, dtype=F32)
    i, j = idx[:, None], idx[None, :]
    same_chunk = jnp.floor(i / CHUNK) == jnp.floor(j / CHUNK)
    mask = jnp.where((j <= i) | same_chunk, jnp.exp(log_gamma[:, None, None] * jnp.abs(i - j)[None]), 0.0)
    qdec = jnp.exp(log_gamma[:, None] * (idx + 1.0)[None, :])[:, :, None]
    kdec = jnp.exp(log_gamma[:, None] * (RET_SC - 1 - idx)[None, :])[:, :, None]
    cdec = jnp.exp(log_gamma * RET_SC)[:, None, None]
    half = RET_DK // 2
    inv = jnp.exp(-jnp.log(ROPE_BASE) * jnp.arange(half, dtype=F32) / half)
    ang = jnp.arange(S, dtype=F32)[:, None] * inv[None, :]
    return jnp.cos(ang), jnp.sin(ang), mask.astype(F32), qdec, kdec, cdec


def _rope(t, cs, sn):
    t1, t2 = t[:, :RET_DK // 2], t[:, RET_DK // 2:]
    return jnp.concatenate([t1 * cs - t2 * sn, t1 * sn + t2 * cs], axis=-1)


def _rope_bwd(d, cs, sn):
    d1, d2 = d[:, :RET_DK // 2], d[:, RET_DK // 2:]
    return jnp.concatenate([d1 * cs + d2 * sn, d2 * cs - d1 * sn], axis=-1)


def _ret_specs(B, S, reverse):
    n_sc = S // RET_SC

    def cc(c):
        return n_sc - 1 - c if reverse else c

    return dict(
        proj=pl.BlockSpec((B, RET_SC, RET_HEAD_COLS), lambda h, c: (0, cc(c), h)),
        trig=pl.BlockSpec((RET_SC, RET_DK // 2), lambda h, c: (cc(c), 0)),
        mask=pl.BlockSpec((None, RET_SC, RET_SC), lambda h, c: (h, 0, 0)),
        dec=pl.BlockSpec((None, RET_SC, 1), lambda h, c: (h, 0, 0)),
        cdec=pl.BlockSpec((None, 1, 1), lambda h, c: (h, 0, 0)),
        gn=pl.BlockSpec((1, RET_DV), lambda h, c: (0, h)),
        val=pl.BlockSpec((B, RET_SC, RET_DV), lambda h, c: (0, cc(c), h)),
        state=pl.BlockSpec((B, None, None, RET_DK, RET_DV), lambda h, c: (0, h, cc(c), 0, 0)),
    )


def _ret_qkvg(p_ref, cs, sn):
    q = _rope(p_ref[:, 0:RET_DK].astype(F32), cs, sn)
    k = _rope(p_ref[:, RET_DK:2 * RET_DK].astype(F32), cs, sn) * (RET_DK ** -0.5)
    v = p_ref[:, 2 * RET_DK:2 * RET_DK + RET_DV]
    gate = p_ref[:, 2 * RET_DK + RET_DV:RET_HEAD_COLS].astype(F32)
    return q, k, v, gate


def _group_norm(o):
    mu = jnp.mean(o, axis=-1, keepdims=True)
    oc = o - mu
    rstd = lax.rsqrt(jnp.mean(oc * oc, axis=-1, keepdims=True) + EPS)
    return oc * rstd, rstd


def _ret_fwd(proj, consts, gn, B, S, name, ride=None):
    T = B * S
    n_sc = S // RET_SC
    sp = _ret_specs(B, S, False)

    def body(p_ref, cos_ref, sin_ref, m_ref, qd_ref, kd_ref, cd_ref, gn_ref, y_ref, o_ref, st_ref, state):
        @pl.when(pl.program_id(1) == 0)
        def _():
            state[...] = jnp.zeros_like(state)

        seqs = range(B)
        qkvg = [_ret_qkvg(p_ref.at[b], cos_ref[...], sin_ref[...]) for b in seqs]
        qb = [qkvg[b][0].astype(BF16) for b in seqs]
        kb = [qkvg[b][1].astype(BF16) for b in seqs]
        kt = [(qkvg[b][1] * kd_ref[...]).astype(BF16) for b in seqs]
        v = [qkvg[b][2] for b in seqs]
        sc = [_dot_nt(qb[b], kb[b]) for b in seqs]
        sb = [state[b].astype(BF16) for b in seqs]
        cross = [_dot(qb[b], sb[b]) for b in seqs]
        for b in seqs:
            st_ref[b] = sb[b]
        p = [(sc[b] * m_ref[...]).astype(BF16) for b in seqs]
        o = [_dot(p[b], v[b]) + cross[b] * qd_ref[...] for b in seqs]
        upd = [_dot_tn(kt[b], v[b]) for b in seqs]
        for b in seqs:
            o_ref[b] = o[b].astype(BF16)
            ohat, _ = _group_norm(o[b])
            gate = qkvg[b][3]
            y_ref[b] = (gate * jax.nn.sigmoid(gate) * (ohat * gn_ref[...])).astype(BF16)
            state[b] = state[b] * cd_ref[...] + upd[b]

    res = _call(
        body, name=name, grid=(RET_HEADS, n_sc),
        in_specs=[sp["proj"], sp["trig"], sp["trig"], sp["mask"], sp["dec"], sp["dec"], sp["cdec"], sp["gn"]],
        out_specs=[sp["val"], sp["val"], sp["state"]],
        out_shape=[jax.ShapeDtypeStruct((B, S, RET_VW), BF16), jax.ShapeDtypeStruct((B, S, RET_VW), BF16),
                   jax.ShapeDtypeStruct((B, RET_HEADS, n_sc, RET_DK, RET_DV), BF16)],
        scratch_shapes=[pltpu.VMEM((B, RET_DK, RET_DV), F32)],
        args=(proj.reshape(B, S, -1), *consts, gn), ride=ride)
    (y, o, states), lands = res if ride is not None else (res, None)
    outs = [y.reshape(T, RET_VW), o.reshape(T, RET_VW), states]
    return outs if ride is None else (outs, lands)


def _ret_bwd(proj, consts, gn, o, states, dy, B, S, name, ride=None):
    T = B * S
    n_sc = S // RET_SC
    sp = _ret_specs(B, S, True)

    def body(p_ref, cos_ref, sin_ref, m_ref, qd_ref, kd_ref, cd_ref, gn_ref, o_ref, st_ref, dy_ref,
             dp_ref, dgn_ref, dstate):
        @pl.when(pl.program_id(1) == 0)
        def _():
            dstate[...] = jnp.zeros_like(dstate)
            dgn_ref[...] = jnp.zeros_like(dgn_ref)

        seqs = range(B)
        cs, sn = cos_ref[...], sin_ref[...]
        m, gnv = m_ref[...], gn_ref[...]
        qkvg = [_ret_qkvg(p_ref.at[b], cs, sn) for b in seqs]
        qb = [qkvg[b][0].astype(BF16) for b in seqs]
        kb = [qkvg[b][1].astype(BF16) for b in seqs]
        kt = [(qkvg[b][1] * kd_ref[...]).astype(BF16) for b in seqs]
        v = [qkvg[b][2] for b in seqs]
        sc = [_dot_nt(qb[b], kb[b]) for b in seqs]
        dsb = [dstate[b].astype(BF16) for b in seqs]
        dv_state = [_dot(kt[b], dsb[b]) for b in seqs]
        dkt = [_dot_nt(v[b], dsb[b]) for b in seqs]
        do, dgate = [], []
        for b in seqs:
            gate = qkvg[b][3]
            ohat, rstd = _group_norm(o_ref[b].astype(F32))
            dyv = dy_ref[b].astype(F32)
            sg = jax.nn.sigmoid(gate)
            don = dyv * (gate * sg)
            dgate.append(dyv * (ohat * gnv) * (sg * (1.0 + gate * (1.0 - sg))))
            dgn_ref[...] += jnp.sum(don * ohat, axis=0, keepdims=True)
            dohat = don * gnv
            do.append(rstd * (dohat - jnp.mean(dohat, axis=-1, keepdims=True)
                              - ohat * jnp.mean(dohat * ohat, axis=-1, keepdims=True)))
        dob = [do[b].astype(BF16) for b in seqs]
        doq = [(do[b] * qd_ref[...]).astype(BF16) for b in seqs]
        dsc_f = [_dot_nt(dob[b], v[b]) for b in seqs]
        dq_state = [_dot_nt(doq[b], st_ref[b]) for b in seqs]
        dstate_upd = [_dot_tn(qb[b], doq[b]) for b in seqs]
        p = [(sc[b] * m).astype(BF16) for b in seqs]
        dsc = [(dsc_f[b] * m).astype(BF16) for b in seqs]
        dv = [_dot_tn(p[b], dob[b]) + dv_state[b] for b in seqs]
        dq = [_dot(dsc[b], kb[b]) + dq_state[b] for b in seqs]
        dk = [(_dot_tn(dsc[b], qb[b]) + dkt[b] * kd_ref[...]) * (RET_DK ** -0.5) for b in seqs]
        for b in seqs:
            dstate[b] = dstate[b] * cd_ref[...] + dstate_upd[b]
            dp_ref[b, :, 0:RET_DK] = _rope_bwd(dq[b], cs, sn).astype(BF16)
            dp_ref[b, :, RET_DK:2 * RET_DK] = _rope_bwd(dk[b], cs, sn).astype(BF16)
            dp_ref[b, :, 2 * RET_DK:2 * RET_DK + RET_DV] = dv[b].astype(BF16)
            dp_ref[b, :, 2 * RET_DK + RET_DV:RET_HEAD_COLS] = dgate[b].astype(BF16)

    res = _call(
        body, name=name, grid=(RET_HEADS, n_sc),
        in_specs=[sp["proj"], sp["trig"], sp["trig"], sp["mask"], sp["dec"], sp["dec"], sp["cdec"], sp["gn"],
                  sp["val"], sp["state"], sp["val"]],
        out_specs=[sp["proj"], sp["gn"]],
        out_shape=[jax.ShapeDtypeStruct((B, S, RET_HEADS * RET_HEAD_COLS), BF16), jax.ShapeDtypeStruct((1, RET_VW), F32)],
        scratch_shapes=[pltpu.VMEM((B, RET_DK, RET_DV), F32)],
        args=(proj.reshape(B, S, -1), *consts, gn, o.reshape(B, S, -1), states, dy.reshape(B, S, -1)), ride=ride)
    (dproj, dgn), lands = res if ride is not None else (res, None)
    outs = [dproj.reshape(T, -1), dgn]
    return outs if ride is None else (outs, lands)


def _ring_index():
    u = np.arange(ATT_RING)
    offset = np.where(u < ATT_KW, u, u - ATT_RING)
    return np.clip(offset - BAND_PAST, -MAX_REL, CHUNK - 1) + MAX_REL


def _bias_ring(rel):
    n_clip = BAND_PAST - MAX_REL
    n_hi = ATT_KW - n_clip - REL_TABLE
    ring = jnp.concatenate([jnp.broadcast_to(rel[:, :1], (ATT_HEADS, n_clip)), rel,
                            jnp.broadcast_to(rel[:, -1:], (ATT_HEADS, n_hi)),
                            jnp.broadcast_to(rel[:, :1], (ATT_HEADS, ATT_CQ))], axis=1)
    return jnp.broadcast_to(ring.reshape(ATT_PAIRS, 2, 1, ATT_RING), (ATT_PAIRS, 2, 8, ATT_RING))


def _band_masks():
    i = np.arange(ATT_CQ)[:, None]
    j = np.arange(ATT_KW)[None, :]
    lo = CHUNK * (i // CHUNK)
    band = np.where((j >= lo) & (j < lo + BAND_PAST + CHUNK), 0.0, NEG)
    return np.stack([band + np.where(j + v * ATT_CQ >= BAND_PAST, 0.0, NEG)
                     for v in range(ATT_VARIANTS)]).astype(np.float32)


def _attn_bias(bias_scr, ring_ref, band_ref):
    for hh in range(2):
        rows = jnp.broadcast_to(ring_ref[hh, 0:1, :], (ATT_CQ, ATT_RING))
        toeplitz = pltpu.roll(rows, 0, 1, stride=1, stride_axis=0)[:, :ATT_KW]
        for v in range(ATT_VARIANTS):
            bias_scr[v, hh] = toeplitz + band_ref[v]


ATT_STRIP = 32
ATT_SCALE = ATT_DH ** -0.5


def _strips(fn):
    def strip(r, carry):
        fn(pl.ds(pl.multiple_of(r * ATT_STRIP, ATT_STRIP), ATT_STRIP))
        return carry

    lax.fori_loop(0, ATT_CQ // ATT_STRIP, strip, 0, unroll=True)


def _attn_prepare(kpad, vpad, qm, qkv_ref):
    kpad[0:BAND_PAST, :] = jnp.zeros((BAND_PAST, LANES), BF16)
    vpad[0:BAND_PAST, :] = jnp.zeros((BAND_PAST, LANES), BF16)
    kpad[BAND_PAST:, :] = qkv_ref[1]
    vpad[BAND_PAST:, :] = qkv_ref[2]
    lane = lax.broadcasted_iota(jnp.int32, (1, LANES), 1)
    q = qkv_ref[0] * ATT_SCALE
    for hh in range(2):
        qm[hh] = jnp.where((lane >= ATT_DH * hh) & (lane < ATT_DH * (hh + 1)), q, jnp.zeros_like(q))


def _attn_scores(s_ref, qm, kpad, b_ref, t, n_qb):
    t = jnp.minimum(t, n_qb - 1)
    qs = pl.multiple_of(t * ATT_CQ, ATT_CQ)
    variant = jnp.minimum(t, ATT_VARIANTS - 1)
    kw = kpad[pl.ds(qs, ATT_KW), :]
    for hh in range(2):
        s_ref[hh] = _dot_nt(qm[hh, pl.ds(qs, ATT_CQ), :], kw) + b_ref[variant, hh]


def _attn_softmax(s_ref, e_ref, linv_ref, m_ref=None):
    def strip(rows):
        s = s_ref[rows, :]
        m = jnp.max(s, axis=-1, keepdims=True)
        e = jnp.exp(s - m)
        e_ref[rows, :] = e.astype(BF16)
        linv_ref[rows, :] = jnp.broadcast_to(1.0 / jnp.sum(e, axis=-1, keepdims=True), (ATT_STRIP, LANES))
        if m_ref is not None:
            m_ref[rows, :] = jnp.broadcast_to(m, (ATT_STRIP, LANES))

    _strips(strip)


_RING_SPEC = pl.BlockSpec((None, 2, 8, ATT_RING), lambda hp, b: (hp, 0, 0, 0))
_BAND_SPEC = pl.BlockSpec((ATT_VARIANTS, ATT_CQ, ATT_KW), lambda hp, b: (0, 0, 0))


def _attn_fwd(qkv3, ring, B, S, name, ride=None):
    T = B * S
    n_qb = S // ATT_CQ

    def body(qkv_ref, ring_ref, band_ref, o_ref, kpad, vpad, qm, b_ref, s_scr, e_scr, linv_scr):
        @pl.when(pl.program_id(1) == 0)
        def _():
            _attn_bias(b_ref, ring_ref, band_ref)

        _attn_prepare(kpad, vpad, qm, qkv_ref)
        e_scr[...] = jnp.zeros_like(e_scr)
        linv_scr[...] = jnp.zeros_like(linv_scr)
        lane = lax.broadcasted_iota(jnp.int32, (1, LANES), 1)

        def softmax(slot):
            for hh in range(2):
                _attn_softmax(s_scr.at[slot, hh], e_scr.at[slot, hh], linv_scr.at[slot, hh])

        def output(t, slot):
            qs = pl.multiple_of(jnp.maximum(t, 0) * ATT_CQ, ATT_CQ)
            vw = vpad[pl.ds(qs, ATT_KW), :]
            outs = [_dot(e_scr[slot, hh], vw) * linv_scr[slot, hh] for hh in range(2)]
            o_ref[pl.ds(qs, ATT_CQ), :] = jnp.where(lane < ATT_DH, outs[0], outs[1]).astype(BF16)

        def pair(u, carry):
            t = 2 * u
            _attn_scores(s_scr.at[1], qm, kpad, b_ref, t + 1, n_qb)
            softmax(0)
            output(t - 1, 1)
            _attn_scores(s_scr.at[0], qm, kpad, b_ref, t + 2, n_qb)
            softmax(1)
            output(t, 0)
            return carry

        _attn_scores(s_scr.at[0], qm, kpad, b_ref, 0, n_qb)
        lax.fori_loop(0, n_qb // 2, pair, 0)
        output(n_qb - 1, 1)

    return _call(body, name=name, grid=(ATT_PAIRS, B),
                 in_specs=[pl.BlockSpec((3, S, LANES), lambda hp, b: (0, b, hp)), _RING_SPEC, _BAND_SPEC],
                 out_specs=[pl.BlockSpec((S, LANES), lambda hp, b: (b, hp))],
                 out_shape=[jax.ShapeDtypeStruct((T, D_MODEL), BF16)],
                 scratch_shapes=[pltpu.VMEM((S + BAND_PAST, LANES), BF16), pltpu.VMEM((S + BAND_PAST, LANES), BF16),
                                 pltpu.VMEM((2, S, LANES), BF16), pltpu.VMEM((ATT_VARIANTS, 2, ATT_CQ, ATT_KW), F32),
                                 pltpu.VMEM((2, 2, ATT_CQ, ATT_KW), F32),
                                 pltpu.VMEM((2, 2, ATT_CQ, ATT_KW), BF16), pltpu.VMEM((2, 2, ATT_CQ, LANES), F32)],
                 args=(qkv3, ring, jnp.asarray(_band_masks())), ride=ride)


def _attn_bwd(qkv3, ring, do, B, S, name, ride=None):
    T = B * S
    n_qb = S // ATT_CQ

    def body(qkv_ref, ring_ref, band_ref, do_ref, dqkv_ref, dring_ref, kpad, vpad, qm, dkacc, dvacc, b_ref, db_ref):
        @pl.when(pl.program_id(1) == 0)
        def _():
            _attn_bias(b_ref, ring_ref, band_ref)
            db_ref[...] = jnp.zeros_like(db_ref)

        _attn_prepare(kpad, vpad, qm, qkv_ref)
        dkacc[...] = jnp.zeros_like(dkacc)
        dvacc[...] = jnp.zeros_like(dvacc)
        lane = lax.broadcasted_iota(jnp.int32, (1, LANES), 1)

        def step(qb, carry):
            qs = pl.multiple_of(qb * ATT_CQ, ATT_CQ)
            variant = jnp.minimum(qb, ATT_VARIANTS - 1)
            dov = do_ref[pl.ds(qs, ATT_CQ), :].astype(F32)
            kw = kpad[pl.ds(qs, ATT_KW), :]
            vw = vpad[pl.ds(qs, ATT_KW), :]
            heads = (0, 1)
            qmh = [qm[hh, pl.ds(qs, ATT_CQ), :] for hh in heads]
            s = [_dot_nt(qmh[hh], kw) + b_ref[variant, hh] for hh in heads]
            e, linv, dom, dp = [None, None], [None, None], [None, None], [None, None]
            for hh in heads:
                e[hh] = jnp.exp(s[hh] - jnp.max(s[hh], axis=-1, keepdims=True))
                linv[hh] = 1.0 / jnp.sum(e[hh], axis=-1, keepdims=True)
                sel = (lane >= ATT_DH * hh) & (lane < ATT_DH * (hh + 1))
                dom[hh] = jnp.where(sel, dov * linv[hh], 0.0).astype(BF16)
                dp[hh] = _dot_nt(dom[hh], vw)
            dqs, dk, dv = [], None, None
            for hh in heads:
                ds = e[hh] * (dp[hh] - jnp.sum(dp[hh] * e[hh], axis=-1, keepdims=True) * linv[hh])
                db_ref[hh] += ds
                dsb = ds.astype(BF16)
                dqs.append(_dot(dsb, kw) * ATT_SCALE)
                dk_h = _dot_tn(qmh[hh], dsb)
                dv_h = _dot_tn(dom[hh], e[hh].astype(BF16))
                dk = dk_h if dk is None else dk + dk_h
                dv = dv_h if dv is None else dv + dv_h
            dqkv_ref[0, pl.ds(qs, ATT_CQ), :] = jnp.where(lane < ATT_DH, dqs[0], dqs[1]).astype(BF16)
            dkacc[:, pl.ds(qs, ATT_KW)] += dk
            dvacc[:, pl.ds(qs, ATT_KW)] += dv
            return carry

        lax.fori_loop(0, n_qb, step, 0)
        dqkv_ref[1] = dkacc[:, BAND_PAST:].T.astype(BF16)
        dqkv_ref[2] = dvacc[:, BAND_PAST:].T.astype(BF16)

        @pl.when(pl.program_id(1) == B - 1)
        def _():
            r = lax.broadcasted_iota(jnp.int32, (ATT_CQ, ATT_CQ), 0)
            c = lax.broadcasted_iota(jnp.int32, (ATT_CQ, ATT_CQ), 1)
            reverse = jnp.where(r + c == ATT_CQ - 1, 1.0, 0.0).astype(BF16)
            for hh in range(2):
                x = jnp.concatenate([db_ref[hh], jnp.zeros((ATT_CQ, ATT_RING - ATT_KW), F32)], axis=1)
                flipped = jnp.zeros((ATT_CQ, ATT_RING), F32)
                for _ in range(3):
                    part = x.astype(BF16)
                    flipped = flipped + _dot(reverse, part)
                    x = x - part.astype(F32)
                aligned = pltpu.roll(flipped, ATT_KW + 1, 1, stride=1, stride_axis=0)
                dring_ref[hh] = jnp.sum(aligned.reshape(ATT_CQ // 8, 8, ATT_RING), axis=0)

    qkv_spec = pl.BlockSpec((3, S, LANES), lambda hp, b: (0, b, hp))
    return _call(body, name=name, grid=(ATT_PAIRS, B),
                 in_specs=[qkv_spec, _RING_SPEC, _BAND_SPEC, pl.BlockSpec((S, LANES), lambda hp, b: (b, hp))],
                 out_specs=[qkv_spec, _RING_SPEC],
                 out_shape=[jax.ShapeDtypeStruct((3, T, D_MODEL), BF16),
                            jax.ShapeDtypeStruct((ATT_PAIRS, 2, 8, ATT_RING), F32)],
                 scratch_shapes=[pltpu.VMEM((S + BAND_PAST, LANES), BF16), pltpu.VMEM((S + BAND_PAST, LANES), BF16),
                                 pltpu.VMEM((2, S, LANES), BF16),
                                 pltpu.VMEM((LANES, S + BAND_PAST), F32), pltpu.VMEM((LANES, S + BAND_PAST), F32),
                                 pltpu.VMEM((ATT_VARIANTS, 2, ATT_CQ, ATT_KW), F32), pltpu.VMEM((2, ATT_CQ, ATT_KW), F32)],
                 args=(qkv3, ring, jnp.asarray(_band_masks()), do), ride=ride)


def _rel_bias_grad(dring, name):
    fold = np.zeros((ATT_RING, REL_PAD), np.float32)
    fold[np.arange(ATT_RING), _ring_index()] = 1.0
    fold = jnp.asarray(fold, BF16)

    def body(d_ref, f_ref, o_ref):
        x = jnp.sum(d_ref[...], axis=0, keepdims=True)
        x = jnp.broadcast_to(x, (8, ATT_RING))
        acc = jnp.zeros((8, REL_PAD), F32)
        for _ in range(3):
            part = x.astype(BF16)
            acc = acc + _dot(part, f_ref[...])
            x = x - part.astype(F32)
        o_ref[...] = acc[0:1, :]

    out = _call(body, name=name, grid=(ATT_HEADS,),
                in_specs=[pl.BlockSpec((None, 8, ATT_RING), lambda h: (h, 0, 0)),
                          pl.BlockSpec((ATT_RING, REL_PAD), lambda h: (0, 0))],
                out_specs=[pl.BlockSpec((None, 1, REL_PAD), lambda h: (h, 0, 0))],
                out_shape=[jax.ShapeDtypeStruct((ATT_HEADS, 1, REL_PAD), F32)],
                args=(dring.reshape(ATT_HEADS, 8, ATT_RING), fold))[0]
    return out.reshape(ATT_HEADS, REL_PAD)


def _adamw(w, g, m, v):
    m = ADAM_B1 * m + (1.0 - ADAM_B1) * g
    v = ADAM_B2 * v + (1.0 - ADAM_B2) * (g * g)
    m_hat = m / (1.0 - ADAM_B1 ** ADAM_STEP)
    v_hat = v / (1.0 - ADAM_B2 ** ADAM_STEP)
    delta = -ADAM_LR * (m_hat / (jnp.sqrt(v_hat) + ADAM_EPS) + ADAM_WD * w)
    return delta, m, v


def _sum_devices(ref):
    g = ref[0].astype(F32)
    for d in range(1, N_DEV):
        g = g + ref[d].astype(F32)
    return g


def _adamw_reduce(lands, w, m, v, name, tr=256):
    L, R, C = w.shape
    tr = min(tr, R)
    n_i = R // tr

    def body(*refs):
        l_refs = refs[:L]
        w_ref, m_ref, v_ref, g_out, d_out, m_out, v_out = refs[L:]
        layer = pl.program_id(0)
        for l in range(L):
            @pl.when(layer == l)
            def _(l=l):
                g = _sum_devices(l_refs[l])
                g_out[...] = g
                d_out[...], m_out[...], v_out[...] = _adamw(w_ref[...], g, m_ref[...], v_ref[...])

    def land_spec(l):
        return pl.BlockSpec((N_DEV, tr, C), lambda ly, i: (0, jnp.where(ly == l, i, jnp.where(ly < l, 0, n_i - 1)), 0))

    blk = pl.BlockSpec((None, tr, C), lambda ly, i: (ly, i, 0))
    return _call(body, name=name, grid=(L, n_i), in_specs=[land_spec(l) for l in range(L)] + [blk, blk, blk],
                 out_specs=[blk] * 4, out_shape=[jax.ShapeDtypeStruct((L, R, C), F32)] * 4, args=(*lands, w, m, v))


def _small_update(land_small, land_rel, w, m, v, name):
    def body(ls_ref, lr_ref, w_ref, m_ref, v_ref, g_out, d_out, m_out, v_out, rel_out):
        g = _sum_devices(ls_ref)
        g_out[...] = g
        d_out[...], m_out[...], v_out[...] = _adamw(w_ref[...], g, m_ref[...], v_ref[...])
        rel_out[...] = _sum_devices(lr_ref)

    return pl.pallas_call(
        body, name=name,
        out_shape=[jax.ShapeDtypeStruct(w.shape, F32)] * 4 + [jax.ShapeDtypeStruct(land_rel.shape[1:], F32)],
    )(land_small, land_rel, w, m, v)


def _adamw_plain(w, g, m, v, name):
    def body(w_ref, g_ref, m_ref, v_ref, d_out, m_out, v_out):
        d_out[...], m_out[...], v_out[...] = _adamw(w_ref[...], g_ref[...], m_ref[...], v_ref[...])

    return pl.pallas_call(body, name=name, out_shape=[jax.ShapeDtypeStruct(w.shape, F32)] * 3)(w, g, m, v)


def _per_head_columns(w):
    rows = w.shape[0]
    bounds = (0, RET_QK, 2 * RET_QK, 2 * RET_QK + RET_VW, RET_IN)
    parts = [w[:, a:b].reshape(rows, RET_HEADS, (b - a) // RET_HEADS) for a, b in zip(bounds[:-1], bounds[1:])]
    return jnp.concatenate(parts, axis=2).reshape(rows, RET_IN)


def _ret_piece_of_column_block(p):
    per_head = RET_HEAD_COLS // RET_DK
    qk = jnp.where(p < RET_HEADS, per_head * p, per_head * (p - RET_HEADS) + 1)
    pv = p - 2 * RET_HEADS
    vv = per_head * (pv // 2) + 2 + pv % 2
    pg = p - 4 * RET_HEADS
    gg = per_head * (pg // 2) + 4 + pg % 2
    return jnp.where(p < 2 * RET_HEADS, qk, jnp.where(p < 4 * RET_HEADS, vv, gg))


def _step(x, target, shards, mix_g, gn_g, rel_shard, mlp_g, final_g):
    s_rin, s_rout, s_ain, s_aout, s_w1, s_w2 = shards
    B, S, _ = x.shape
    T = B * S
    tk = min(T, 2048)
    n_k = T // tk
    h0 = x.reshape(T, D_MODEL)
    tgt = target.reshape(T, D_MODEL)
    consts = _ret_constants(S)
    w1_cols = w2_rows = D_FF // N_DEV

    rel_cols = REL_TABLE // N_DEV
    w_rin, rel_all = _all_gather(
        [s_rin, rel_shard],
        [jax.ShapeDtypeStruct((D_MODEL, RET_IN), BF16), jax.ShapeDtypeStruct((N_DEV, ATT_HEADS, rel_cols), F32)],
        [_cols(RET_IN // N_DEV), _slot], "gather_first")
    ring = _bias_ring(rel_all.transpose(1, 0, 2).reshape(ATT_HEADS, REL_TABLE))
    w_rin_heads = _per_head_columns(w_rin)

    def gather(*items):
        return _Exchange([s for s, _, _ in items], [(a, _whole, a, view) for a, (_, view, _) in enumerate(items)],
                         [jax.ShapeDtypeStruct(full, BF16) for _, _, full in items])

    def scatter(*items):
        return _Exchange([p for p, _, _ in items], [(a, view, a, _slot) for a, (_, view, _) in enumerate(items)],
                         [jax.ShapeDtypeStruct((N_DEV, *shard), BF16) for _, _, shard in items])

    w1_full, w2_full = (D_MODEL, D_FF), (D_FF, D_MODEL)
    w1_shard, w2_shard = (D_MODEL, w1_cols), (w2_rows, D_MODEL)

    ride = gather((s_rout, _rows(RET_VW // N_DEV), (RET_VW, D_MODEL)), (s_w1[0], _cols(w1_cols), w1_full))
    (proj, n0), (w_rout, w1_0) = _norm_proj(h0, mix_g[0:1], w_rin_heads, 1, "ret_proj", 512, ride=ride)
    (y, o, states), (w2_0,) = _ret_fwd(proj, consts, gn_g, B, S, "ret_fwd", ride=gather((s_w2[0], _rows(w2_rows), w2_full)))
    h1 = _matmul_res(y, w_rout, h0, "ret_out")
    ride = gather((s_ain, _cols(3 * D_MODEL // N_DEV), (D_MODEL, 3 * D_MODEL)),
                  (s_aout, _rows(D_MODEL // N_DEV), (D_MODEL, D_MODEL)))
    (h2, n1, u1), (w_ain, w_aout) = _mlp_fwd(h1, mlp_g[0:1], w1_0, w2_0, "mlp0_fwd", ride=ride)
    qkv3, n2 = _norm_proj(h2, mix_g[1:2], w_ain, 3, "att_proj", 1024)
    ride = gather((s_w1[1], _cols(w1_cols), w1_full), (s_w2[1], _rows(w2_rows), w2_full))
    (o2,), (w1_1, w2_1) = _attn_fwd(qkv3, ring, B, S, "att_fwd", ride=ride)
    h3 = _matmul_res(o2, w_aout, h2, "att_out")
    h4, n3, u3 = _mlp_fwd(h3, mlp_g[1:2], w1_1, w2_1, "mlp1_fwd")
    dh4, dh4b, loss, dg_final = _final_loss(h4, final_g, tgt, "final_loss")

    def tok(width):
        return dict(a=pl.BlockSpec((tk, width), lambda i, j, k: (k, i)), b=pl.BlockSpec((tk, width), lambda i, j, k: (k, j)))

    def mlp_grads(dh, dhb, u, n, w1, w2, h, g, tag, ride=None, w2_rides_on_dw1=False):
        res = _mlp_bwd(dh, u, w1, w2, h, g, tag + "_bwd", ride=ride)
        (dhi, dhib, du, dg), lands = res if ride is not None else (res, None)
        gw2 = _wgrad(u, tok(1024)["a"], dhb, tok(1024)["b"], D_FF, D_MODEL, 1024, 1024, n_k, tag + "_dw2", square_a=True)
        ride_w2 = scatter((gw2, _rows(w2_rows), w2_shard)) if w2_rides_on_dw1 else None
        gw1 = _wgrad(n, tok(1024)["a"], du, tok(1024)["b"], D_MODEL, D_FF, 1024, 1024, n_k, tag + "_dw1", ride=ride_w2)
        return dhi, dhib, gw1, gw2, dg, lands

    dh3, dh3b, gw1_1, gw2_1, dg_mlp1, _ = mlp_grads(dh4, dh4b, u3, n3, w1_1, w2_1, h3, mlp_g[1:2], "mlp1")
    do2 = _matmul_nt(dh3b, w_aout, "att_out_bwd")
    g_aout = _wgrad(o2, tok(1024)["a"], dh3b, tok(1024)["b"], D_MODEL, D_MODEL, 1024, 1024, n_k, "att_out_dw")
    ride = scatter((gw1_1, _cols(w1_cols), w1_shard), (gw2_1, _rows(w2_rows), w2_shard),
                   (g_aout, _rows(D_MODEL // N_DEV), (D_MODEL // N_DEV, D_MODEL)))
    (dqkv3, dring), (l_w1_1, l_w2_1, l_aout) = _attn_bwd(qkv3, ring, do2, B, S, "att_bwd", ride=ride)
    g_rel = _rel_bias_grad(dring, "rel_bias_grad")
    dh2, dh2b, dg_mix1 = _proj_bwd(dqkv3, w_ain, h2, mix_g[1:2], dh3, "att_proj_bwd", 1024)
    g_ain = _wgrad(n2, tok(1024)["a"], dqkv3, pl.BlockSpec((None, tk, D_MODEL), lambda i, j, k: (j, k, 0)),
                   D_MODEL, 3 * D_MODEL, 1024, 1024, n_k, "att_proj_dw")
    ride = scatter((g_ain, _cols(3 * D_MODEL // N_DEV), (D_MODEL, 3 * D_MODEL // N_DEV)))
    dh1, dh1b, (gw1_0, (l_w2_0,)), _, dg_mlp0, (l_ain,) = mlp_grads(
        dh2, dh2b, u1, n1, w1_0, w2_0, h1, mlp_g[0:1], "mlp0", ride=ride, w2_rides_on_dw1=True)
    dy = _matmul_nt(dh1b, w_rout, "ret_out_bwd")
    g_rout = _wgrad(y, tok(1024)["a"], dh1b, tok(1024)["b"], RET_VW, D_MODEL, 1024, 1024, n_k, "ret_out_dw")
    ride = scatter((gw1_0, _cols(w1_cols), w1_shard), (g_rout, _rows(RET_VW // N_DEV), (RET_VW // N_DEV, D_MODEL)))
    (dproj, dgn), (l_w1_0, l_rout) = _ret_bwd(proj, consts, gn_g, o, states, dy, B, S, "ret_bwd", ride=ride)
    per_shard = RET_IN // N_DEV // RET_DK
    g_rin = _wgrad(n0, tok(1024)["a"], dproj,
                   [pl.BlockSpec((tk, RET_DK), functools.partial(
                       lambda i, j, k, r: (k, _ret_piece_of_column_block(per_shard * j + r)), r=r)) for r in range(per_shard)],
                   D_MODEL, RET_IN, 1024, per_shard * RET_DK, n_k, "ret_proj_dw")
    ride = scatter((g_rin, _cols(RET_IN // N_DEV), (D_MODEL, RET_IN // N_DEV)))
    (dx, _, dg_mix0), (l_rin,) = _proj_bwd(dproj, w_rin_heads, h0, mix_g[0:1], dh1, "ret_proj_bwd", 512, ride=ride)

    small = jnp.concatenate([dg_mix0, dg_mix1, dg_mlp0, dg_mlp1, dg_final, dgn.reshape(2, D_MODEL),
                             jnp.zeros((1, D_MODEL), F32)], axis=0)
    l_small, l_rel = _Exchange(
        [small, g_rel], [(0, _whole, 0, _slot), (1, _whole, 1, _slot)],
        [jax.ShapeDtypeStruct((N_DEV, 8, D_MODEL), F32), jax.ShapeDtypeStruct((N_DEV, ATT_HEADS, REL_PAD), F32)],
    ).run("scatter_small")
    lands = dict(rin=l_rin, rout=l_rout, ain=l_ain, aout=l_aout, w1=(l_w1_0, l_w1_1), w2=(l_w2_0, l_w2_1),
                 small=l_small, rel=l_rel)
    return loss[0, 0], dx.reshape(B, S, D_MODEL), lands


def kernel(x, mix_norm_g, ret_w_in, ret_gn_g, ret_w_out, att_w_in, att_rel_bias, att_w_out, mlp_norm_g, mlp_w1, mlp_w2, final_norm_g, loss_target, m_mix_norm_g, m_ret_w_in, m_ret_gn_g, m_ret_w_out, m_att_w_in, m_att_rel_bias, m_att_w_out, m_mlp_norm_g, m_mlp_w1, m_mlp_w2, m_final_norm_g, v_mix_norm_g, v_ret_w_in, v_ret_gn_g, v_ret_w_out, v_att_w_in, v_att_rel_bias, v_att_w_out, v_mlp_norm_g, v_mlp_w1, v_mlp_w2, v_final_norm_g):
    me = _lin(_place())
    rel_cols = REL_TABLE // N_DEV
    shards = (ret_w_in[0].astype(BF16), ret_w_out[0].astype(BF16), att_w_in[0].astype(BF16), att_w_out[0].astype(BF16),
              (mlp_w1[0].astype(BF16), mlp_w1[1].astype(BF16)), (mlp_w2[0].astype(BF16), mlp_w2[1].astype(BF16)))
    loss_part, grad_x, lands = _step(x, loss_target, shards, mix_norm_g, ret_gn_g, att_rel_bias[0], mlp_norm_g,
                                     final_norm_g.reshape(1, D_MODEL))
    loss = lax.psum(loss_part, ("x", "y", "c"))

    u_rin = _adamw_reduce([lands["rin"]], ret_w_in, m_ret_w_in, v_ret_w_in, "update_ret_w_in")
    u_rout = _adamw_reduce([lands["rout"]], ret_w_out, m_ret_w_out, v_ret_w_out, "update_ret_w_out")
    u_ain = _adamw_reduce([lands["ain"]], att_w_in, m_att_w_in, v_att_w_in, "update_att_w_in")
    u_aout = _adamw_reduce([lands["aout"]], att_w_out, m_att_w_out, v_att_w_out, "update_att_w_out")
    u_w1 = _adamw_reduce(lands["w1"], mlp_w1, m_mlp_w1, v_mlp_w1, "update_mlp_w1")
    u_w2 = _adamw_reduce(lands["w2"], mlp_w2, m_mlp_w2, v_mlp_w2, "update_mlp_w2")

    def pack(mix, mlp, fin, gn):
        return jnp.concatenate([mix, mlp, fin.reshape(1, D_MODEL), gn.reshape(2, D_MODEL), jnp.zeros((1, D_MODEL), F32)], axis=0)

    small_w = pack(mix_norm_g, mlp_norm_g, final_norm_g, ret_gn_g)
    small_m = pack(m_mix_norm_g, m_mlp_norm_g, m_final_norm_g, m_ret_gn_g)
    small_v = pack(v_mix_norm_g, v_mlp_norm_g, v_final_norm_g, v_ret_gn_g)
    sg, sd, sm, sv, rel_sum = _small_update(lands["small"], lands["rel"], small_w, small_m, small_v, "update_small")
    g_rel_mine = lax.dynamic_slice(rel_sum, (0, me * rel_cols), (ATT_HEADS, rel_cols))
    rel_d, rel_m, rel_v = _adamw_plain(att_rel_bias[0], g_rel_mine, m_att_rel_bias[0], v_att_rel_bias[0], "update_rel_bias")
    u_rel = [g_rel_mine[None], rel_d[None], rel_m[None], rel_v[None]]

    def unpack(t):
        return dict(mix=t[0:2], mlp=t[2:4], fin=t[4], gn=t[5:7].reshape(1, RET_VW))

    us = [unpack(t) for t in (sg, sd, sm, sv)]
    outs = [loss, grad_x]
    for k in range(4):
        outs += [us[k]["mix"], u_rin[k], us[k]["gn"], u_rout[k], u_ain[k], u_rel[k], u_aout[k], us[k]["mlp"],
                 u_w1[k], u_w2[k], us[k]["fin"]]
    return tuple(outs)
```

```python
import functools

import numpy as np
import jax
import jax.numpy as jnp
from jax import lax
from jax.experimental import pallas as pl
from jax.experimental.pallas import tpu as pltpu

F32, BF16 = jnp.float32, jnp.bfloat16

D_MODEL = 1024
CHUNK = 64
RET_HEADS, RET_DK, RET_DV = 4, 256, 512
RET_QK, RET_VW = RET_HEADS * RET_DK, RET_HEADS * RET_DV
RET_IN = 2 * RET_QK + 2 * RET_VW
RET_HEAD_COLS = 2 * RET_DK + 2 * RET_DV
RET_SC = 256
ROPE_BASE = 10000.0
ATT_HEADS, ATT_DH = 16, 64
LANES = 128
ATT_PAIRS = ATT_HEADS * ATT_DH // LANES
BAND_PAST = 8 * CHUNK
MAX_REL = 256
REL_TABLE = MAX_REL + CHUNK
REL_PAD = 384
ATT_CQ = 256
ATT_KW = ATT_CQ + BAND_PAST
ATT_RING = ATT_CQ + ATT_KW
ATT_VARIANTS = BAND_PAST // ATT_CQ + 1
D_FF = 4 * D_MODEL
EPS = 1e-6
EPILOGUE_ROWS = 256
NEG = -1e30
N_DEV = 8
N_PEERS = N_DEV - 1

ADAM_LR, ADAM_B1, ADAM_B2, ADAM_EPS, ADAM_WD, ADAM_STEP = 0.001, 0.9, 0.999, 1e-08, 0.01, 10

VMEM_LIMIT = 56 * 1024 * 1024
MESH = pl.DeviceIdType.MESH
ANY = pl.BlockSpec(memory_space=pl.ANY)


def _dot(a, b):
    return jnp.dot(a, b, preferred_element_type=F32)


def _dot_nt(a, b):
    return lax.dot_general(a, b, (((1,), (1,)), ((), ())), preferred_element_type=F32)


def _dot_tn(a, b):
    return lax.dot_general(a, b, (((0,), (0,)), ((), ())), preferred_element_type=F32)


def _rms(x, g):
    r = lax.rsqrt(jnp.mean(x * x, axis=-1, keepdims=True) + EPS)
    return x * r * g


def _rms_bwd(dn, x, g):
    r = lax.rsqrt(jnp.mean(x * x, axis=-1, keepdims=True) + EPS)
    xh = x * r
    dg = jnp.sum(dn * xh, axis=0, keepdims=True)
    dxh = dn * g
    dx = r * (dxh - xh * jnp.mean(dxh * xh, axis=-1, keepdims=True))
    return dx, dg


def _rms_bwd_epilogue(x_ref, g_ref, dres_ref, dx_ref, dxb_ref, dg_ref):
    for r in range(0, dx_ref.shape[0], EPILOGUE_ROWS):
        rows = slice(r, r + EPILOGUE_ROWS)
        dx, dg = _rms_bwd(dx_ref[rows, :], x_ref[rows, :], g_ref[...])
        dx = dres_ref[rows, :] + dx
        dx_ref[rows, :] = dx
        dxb_ref[rows, :] = dx.astype(BF16)
        dg_ref[...] += dg


def _place():
    return lax.axis_index("x"), lax.axis_index("y"), lax.axis_index("c")


def _lin(p):
    return 4 * p[0] + 2 * p[1] + p[2]


def _cols(width):
    return lambda ref, blk: ref.at[:, pl.ds(pl.multiple_of(blk * width, LANES), width)]


def _rows(height):
    return lambda ref, blk: ref.at[pl.ds(pl.multiple_of(blk * height, 8), height), :]


def _slot(ref, blk):
    return ref.at[blk]


def _whole(ref, blk):
    return ref


class _Exchange:
    def __init__(self, sources, flows, land_shapes):
        self.sources, self.flows, self.land_shapes = list(sources), flows, list(land_shapes)
        n_f = len(flows)
        self.sem_shapes = [pltpu.SemaphoreType.DMA((N_PEERS * n_f,)), pltpu.SemaphoreType.DMA((N_PEERS * n_f,)),
                           pltpu.SemaphoreType.DMA((n_f,))]

    def _copies(self, srcs, lands, sems):
        send_sems, recv_sems, local_sems = sems
        x, y, c = _place()
        me = (x, y, c)
        peers = [(x ^ dx, y ^ dy, c ^ dc) for dx in (0, 1) for dy in (0, 1) for dc in (0, 1)][1:]

        def copy(f, k, sender, to):
            si, sview, li, lview = self.flows[f]
            return pltpu.make_async_remote_copy(
                src_ref=sview(srcs[si], _lin(to)), dst_ref=lview(lands[li], _lin(sender)),
                send_sem=send_sems.at[f * N_PEERS + k], recv_sem=recv_sems.at[f * N_PEERS + k],
                device_id=to, device_id_type=MESH)

        mine, sends, recvs = [], [], []
        for f, (si, sview, li, lview) in enumerate(self.flows):
            mine.append(pltpu.make_async_copy(sview(srcs[si], _lin(me)), lview(lands[li], _lin(me)), local_sems.at[f]))
            for k, peer in enumerate(peers):
                sends.append(copy(f, k, me, peer))
                recvs.append(copy(f, k, peer, me))
        return mine, sends, recvs

    def start(self, srcs, lands, sems):
        mine, sends, _ = self._copies(srcs, lands, sems)
        for cp in mine + sends:
            cp.start()

    def finish(self, srcs, lands, sems):
        mine, sends, recvs = self._copies(srcs, lands, sems)
        for cp in recvs:
            cp.wait_recv()
        for cp in sends:
            cp.wait_send()
        for cp in mine:
            cp.wait()

    def run(self, name):
        n_src, n_land = len(self.sources), len(self.land_shapes)

        def body(*refs):
            srcs, lands, sems = refs[:n_src], refs[n_src:n_src + n_land], refs[n_src + n_land:]
            self.start(srcs, lands, sems)
            self.finish(srcs, lands, sems)

        return pl.pallas_call(body, name=name, in_specs=[ANY] * n_src, out_specs=[ANY] * n_land,
                              out_shape=self.land_shapes, scratch_shapes=self.sem_shapes)(*self.sources)


def _call(body, *, name, grid, in_specs, out_specs, out_shape, args, scratch_shapes=(), ride=None):
    params = pltpu.CompilerParams(dimension_semantics=("arbitrary",) * len(grid), vmem_limit_bytes=VMEM_LIMIT)
    in_specs, out_specs, out_shape, scratch_shapes = list(in_specs), list(out_specs), list(out_shape), list(scratch_shapes)
    if ride is None:
        return pl.pallas_call(body, name=name, grid=grid, in_specs=in_specs, out_specs=out_specs, out_shape=out_shape,
                              scratch_shapes=scratch_shapes, compiler_params=params)(*args)
    n_in, n_out, n_scr = len(in_specs), len(out_specs), len(scratch_shapes)
    n_src, n_land = len(ride.sources), len(ride.land_shapes)

    def riding(*refs):
        bounds = np.cumsum([n_in, n_src, n_out, n_land, n_scr])
        ins, srcs, outs, lands, scr, sems = (refs[a:b] for a, b in zip([0, *bounds], [*bounds, len(refs)]))
        ids = [pl.program_id(d) for d in range(len(grid))]
        first = functools.reduce(lambda a, b: a & b, [i == 0 for i in ids])
        last = functools.reduce(lambda a, b: a & b, [i == n - 1 for i, n in zip(ids, grid)])

        @pl.when(first)
        def _():
            ride.start(srcs, lands, sems)

        body(*ins, *outs, *scr)

        @pl.when(last)
        def _():
            ride.finish(srcs, lands, sems)

    res = pl.pallas_call(
        riding, name=name, grid=grid, in_specs=in_specs + [ANY] * n_src, out_specs=out_specs + [ANY] * n_land,
        out_shape=out_shape + ride.land_shapes, scratch_shapes=scratch_shapes + ride.sem_shapes,
        compiler_params=params)(*args, *ride.sources)
    return res[:n_out], res[n_out:]


def _all_gather(shards, out_shapes, out_of, views, name):
    n, n_out = len(shards), len(out_shapes)

    def body(*refs):
        ins = refs[:n]
        outs = [refs[n + out_of[a]] for a in range(n)]
        send_sems, recv_sems, local_sems = refs[n + n_out:]
        x, y, c = _place()
        me, sibling = (x, y, c), (x, y, 1 - c)
        chips = [(1 - x, y), (x, 1 - y), (1 - x, 1 - y)]

        def copy(a, k, block, to, src=None):
            dst = views[a](outs[a], _lin(block))
            return pltpu.make_async_remote_copy(
                src_ref=dst if src is None else src, dst_ref=dst,
                send_sem=send_sems.at[a * N_PEERS + k], recv_sem=recv_sems.at[a * N_PEERS + k],
                device_id=to, device_id_type=MESH)

        mine = [pltpu.make_async_copy(ins[a], views[a](outs[a], _lin(me)), local_sems.at[a]) for a in range(n)]
        for cp in mine:
            cp.start()
        first = []
        for a in range(n):
            first.append(copy(a, 0, me, sibling, src=ins[a]))
            first += [copy(a, 1 + j, me, (*chip, c), src=ins[a]) for j, chip in enumerate(chips)]
        for cp in first:
            cp.start()
        passed = []
        for j, chip in enumerate(chips):
            for a in range(n):
                copy(a, 1 + j, (*chip, c), me).wait_recv()
                fwd = copy(a, 4 + j, (*chip, c), sibling)
                fwd.start()
                passed.append(fwd)
        for a in range(n):
            copy(a, 0, sibling, me).wait_recv()
            for j, chip in enumerate(chips):
                copy(a, 4 + j, (*chip, 1 - c), me).wait_recv()
        for cp in first + passed:
            cp.wait_send()
        for cp in mine:
            cp.wait()

    return pl.pallas_call(
        body, name=name,
        in_specs=[ANY] * n, out_specs=[ANY] * n_out, out_shape=out_shapes,
        scratch_shapes=[pltpu.SemaphoreType.DMA((N_PEERS * n,)), pltpu.SemaphoreType.DMA((N_PEERS * n,)),
                        pltpu.SemaphoreType.DMA((n,))],
    )(*shards)


def _resident(shape):
    return pl.BlockSpec(shape, lambda i: (0,) * len(shape), pipeline_mode=pl.Buffered(1))


def _norm_proj(h, g, w, groups, name, tm, ride=None):
    T = h.shape[0]
    N = w.shape[1]
    width = N // groups

    def body(h_ref, g_ref, w_ref, o_ref, n_ref):
        n = _rms(h_ref[...], g_ref[...]).astype(BF16)
        n_ref[...] = n
        if groups == 1:
            o_ref[...] = _dot(n, w_ref[...]).astype(BF16)
        else:
            for p in range(groups):
                o_ref[p] = _dot(n, w_ref[:, p * width:(p + 1) * width]).astype(BF16)

    row = pl.BlockSpec((tm, D_MODEL), lambda i: (i, 0))
    if groups == 1:
        out_spec, out_shape = pl.BlockSpec((tm, N), lambda i: (i, 0)), jax.ShapeDtypeStruct((T, N), BF16)
    else:
        out_spec = pl.BlockSpec((groups, tm, width), lambda i: (0, i, 0))
        out_shape = jax.ShapeDtypeStruct((groups, T, width), BF16)
    return _call(body, name=name, grid=(T // tm,),
                 in_specs=[row, pl.BlockSpec((1, D_MODEL), lambda i: (0, 0)), _resident(w.shape)],
                 out_specs=[out_spec, row], out_shape=[out_shape, jax.ShapeDtypeStruct((T, D_MODEL), BF16)],
                 args=(h, g, w), ride=ride)


def _proj_bwd(dy, w, x, g, dres, name, tm, ride=None):
    T = x.shape[0]
    groups = dy.shape[0] if dy.ndim == 3 else 1
    width = w.shape[1] // groups

    def body(dy_ref, w_ref, x_ref, g_ref, dres_ref, dx_ref, dxb_ref, dg_ref):
        @pl.when(pl.program_id(0) == 0)
        def _():
            dg_ref[...] = jnp.zeros_like(dg_ref)

        if groups == 1:
            dn = _dot_nt(dy_ref[...], w_ref[...])
        else:
            dn = sum(_dot_nt(dy_ref[p], w_ref[:, p * width:(p + 1) * width]) for p in range(groups))
        dx_ref[...] = dn
        _rms_bwd_epilogue(x_ref, g_ref, dres_ref, dx_ref, dxb_ref, dg_ref)

    row = pl.BlockSpec((tm, D_MODEL), lambda i: (i, 0))
    vec = pl.BlockSpec((1, D_MODEL), lambda i: (0, 0))
    dy_spec = (pl.BlockSpec((tm, w.shape[1]), lambda i: (i, 0)) if groups == 1
               else pl.BlockSpec((groups, tm, width), lambda i: (0, i, 0)))
    return _call(body, name=name, grid=(T // tm,),
                 in_specs=[dy_spec, _resident(w.shape), row, vec, row], out_specs=[row, row, vec],
                 out_shape=[jax.ShapeDtypeStruct((T, D_MODEL), F32), jax.ShapeDtypeStruct((T, D_MODEL), BF16),
                            jax.ShapeDtypeStruct((1, D_MODEL), F32)],
                 args=(dy, w, x, g, dres), ride=ride)


def _matmul_res(a, w, res, name, tm=1024):
    T, K = a.shape

    def body(a_ref, w_ref, r_ref, o_ref):
        o_ref[...] = r_ref[...] + _dot(a_ref[...], w_ref[...])

    row = pl.BlockSpec((tm, D_MODEL), lambda i: (i, 0))
    return _call(body, name=name, grid=(T // tm,),
                 in_specs=[pl.BlockSpec((tm, K), lambda i: (i, 0)), pl.BlockSpec((K, D_MODEL), lambda i: (0, 0)), row],
                 out_specs=[row], out_shape=[jax.ShapeDtypeStruct((T, D_MODEL), F32)], args=(a, w, res))[0]


def _matmul_nt(dy, w, name, tm=1024):
    T, N = dy.shape
    K = w.shape[0]

    def body(dy_ref, w_ref, o_ref):
        o_ref[...] = _dot_nt(dy_ref[...], w_ref[...]).astype(BF16)

    return _call(body, name=name, grid=(T // tm,),
                 in_specs=[pl.BlockSpec((tm, N), lambda i: (i, 0)), pl.BlockSpec((K, N), lambda i: (0, 0))],
                 out_specs=[pl.BlockSpec((tm, K), lambda i: (i, 0))], out_shape=[jax.ShapeDtypeStruct((T, K), BF16)],
                 args=(dy, w))[0]


def _wgrad(a, a_spec, b, b_specs, m, n, bm, bn, n_k, name, square_a=False, ride=None):
    b_specs = b_specs if isinstance(b_specs, (list, tuple)) else [b_specs]
    n_b = len(b_specs)

    def body(a_ref, *rest):
        b_refs = rest[:n_b]
        o_ref, acc = rest[n_b:]
        k = pl.program_id(2)

        @pl.when(k == 0)
        def _():
            acc[...] = jnp.zeros_like(acc)

        av = a_ref[...]
        if square_a:
            af = av.astype(F32)
            av = (af * af).astype(BF16)
        bv = b_refs[0][...] if n_b == 1 else jnp.concatenate([r[...] for r in b_refs], axis=1)
        acc[...] += _dot_tn(av, bv)

        @pl.when(k == n_k - 1)
        def _():
            o_ref[...] = acc[...].astype(BF16)

    res = _call(body, name=name, grid=(m // bm, n // bn, n_k), in_specs=[a_spec, *b_specs],
                out_specs=[pl.BlockSpec((bm, bn), lambda i, j, k: (i, j))], out_shape=[jax.ShapeDtypeStruct((m, n), BF16)],
                scratch_shapes=[pltpu.VMEM((bm, bn), F32)], args=(a, *([b] * n_b)), ride=ride)
    return res[0] if ride is None else (res[0][0], res[1])


def _mlp_fwd(h, g, w1, w2, name, tm=1024, tf=1024, ride=None):
    T = h.shape[0]

    def body(h_ref, g_ref, w1_ref, w2_ref, ho_ref, n_ref, u_ref):
        @pl.when(pl.program_id(1) == 0)
        def _():
            n_ref[...] = _rms(h_ref[...], g_ref[...]).astype(BF16)
            ho_ref[...] = h_ref[...]

        u = jnp.maximum(_dot(n_ref[...], w1_ref[...]), 0.0)
        u_ref[...] = u.astype(BF16)
        ho_ref[...] += _dot((u * u).astype(BF16), w2_ref[...])

    row = pl.BlockSpec((tm, D_MODEL), lambda i, j: (i, 0))
    return _call(body, name=name, grid=(T // tm, D_FF // tf),
                 in_specs=[row, pl.BlockSpec((1, D_MODEL), lambda i, j: (0, 0)),
                           pl.BlockSpec((D_MODEL, tf), lambda i, j: (0, j)), pl.BlockSpec((tf, D_MODEL), lambda i, j: (j, 0))],
                 out_specs=[row, row, pl.BlockSpec((tm, tf), lambda i, j: (i, j))],
                 out_shape=[jax.ShapeDtypeStruct((T, D_MODEL), F32), jax.ShapeDtypeStruct((T, D_MODEL), BF16),
                            jax.ShapeDtypeStruct((T, D_FF), BF16)],
                 args=(h, g, w1, w2), ride=ride)


def _mlp_bwd(dh, u, w1, w2, h, g, name, tm=1024, tf=1024, ride=None):
    T = h.shape[0]
    n_j = D_FF // tf

    def body(dh_ref, u_ref, w1_ref, w2_ref, h_ref, g_ref, dx_ref, dxb_ref, du_ref, dg_ref, dhb):
        acc = dx_ref
        i, j = pl.program_id(0), pl.program_id(1)

        @pl.when(j == 0)
        def _():
            dhb[...] = dh_ref[...].astype(BF16)
            acc[...] = jnp.zeros_like(acc)

        @pl.when((i == 0) & (j == 0))
        def _():
            dg_ref[...] = jnp.zeros_like(dg_ref)

        da = _dot_nt(dhb[...], w2_ref[...])
        du = (da * (2.0 * u_ref[...].astype(F32))).astype(BF16)
        du_ref[...] = du
        acc[...] += _dot_nt(du, w1_ref[...])

        @pl.when(j == n_j - 1)
        def _():
            _rms_bwd_epilogue(h_ref, g_ref, dh_ref, dx_ref, dxb_ref, dg_ref)

    row = pl.BlockSpec((tm, D_MODEL), lambda i, j: (i, 0))
    vec = pl.BlockSpec((1, D_MODEL), lambda i, j: (0, 0))
    hid = pl.BlockSpec((tm, tf), lambda i, j: (i, j))
    return _call(body, name=name, grid=(T // tm, n_j),
                 in_specs=[row, hid, pl.BlockSpec((D_MODEL, tf), lambda i, j: (0, j)),
                           pl.BlockSpec((tf, D_MODEL), lambda i, j: (j, 0)), row, vec],
                 out_specs=[row, row, hid, vec],
                 out_shape=[jax.ShapeDtypeStruct((T, D_MODEL), F32), jax.ShapeDtypeStruct((T, D_MODEL), BF16),
                            jax.ShapeDtypeStruct((T, D_FF), BF16), jax.ShapeDtypeStruct((1, D_MODEL), F32)],
                 scratch_shapes=[pltpu.VMEM((tm, D_MODEL), BF16)], args=(dh, u, w1, w2, h, g), ride=ride)


def _final_loss(h, g, target, name, tm=512):
    T = h.shape[0]

    def body(h_ref, g_ref, t_ref, dh_ref, dhb_ref, loss_ref, dg_ref):
        @pl.when(pl.program_id(0) == 0)
        def _():
            loss_ref[...] = jnp.zeros_like(loss_ref)
            dg_ref[...] = jnp.zeros_like(dg_ref)

        x = h_ref[...]
        gg = g_ref[...]
        r = lax.rsqrt(jnp.mean(x * x, axis=-1, keepdims=True) + EPS)
        xh = x * r
        e = xh * gg - t_ref[...]
        per_tok = jnp.mean(e * e, axis=-1, keepdims=True)
        loss_ref[...] += 0.5 * jnp.sum(per_tok, axis=0, keepdims=True)
        dy = e * (1.0 / D_MODEL)
        dg_ref[...] += jnp.sum(dy * xh, axis=0, keepdims=True)
        dxh = dy * gg
        dx = r * (dxh - xh * jnp.mean(dxh * xh, axis=-1, keepdims=True))
        dh_ref[...] = dx
        dhb_ref[...] = dx.astype(BF16)

    row = pl.BlockSpec((tm, D_MODEL), lambda i: (i, 0))
    vec = pl.BlockSpec((1, D_MODEL), lambda i: (0, 0))
    return _call(body, name=name, grid=(T // tm,), in_specs=[row, vec, row],
                 out_specs=[row, row, pl.BlockSpec((8, LANES), lambda i: (0, 0)), vec],
                 out_shape=[jax.ShapeDtypeStruct((T, D_MODEL), F32), jax.ShapeDtypeStruct((T, D_MODEL), BF16),
                            jax.ShapeDtypeStruct((8, LANES), F32), jax.ShapeDtypeStruct((1, D_MODEL), F32)],
                 args=(h, g, target))


def _ret_constants(S):
    log_gamma = jnp.log1p(-jnp.exp2(-5.0 - jnp.arange(RET_HEADS, dtype=F32)))
    idx = jnp.arange(RET_SC, dtype=F32)
    i, j = idx[:, None], idx[None, :]
    same_chunk = jnp.floor(i / CHUNK) == jnp.floor(j / CHUNK)
    mask = jnp.where((j <= i) | same_chunk, jnp.exp(log_gamma[:, None, None] * jnp.abs(i - j)[None]), 0.0)
    qdec = jnp.exp(log_gamma[:, None] * (idx + 1.0)[None, :])[:, :, None]
    kdec = jnp.exp(log_gamma[:, None] * (RET_SC - 1 - idx)[None, :])[:, :, None]
    cdec = jnp.exp(log_gamma * RET_SC)[:, None, None]
    half = RET_DK // 2
    inv = jnp.exp(-jnp.log(ROPE_BASE) * jnp.arange(half, dtype=F32) / half)
    ang = jnp.arange(S, dtype=F32)[:, None] * inv[None, :]
    return jnp.cos(ang), jnp.sin(ang), mask.astype(F32), qdec, kdec, cdec


def _rope(t, cs, sn):
    t1, t2 = t[:, :RET_DK // 2], t[:, RET_DK // 2:]
    return jnp.concatenate([t1 * cs - t2 * sn, t1 * sn + t2 * cs], axis=-1)


def _rope_bwd(d, cs, sn):
    d1, d2 = d[:, :RET_DK // 2], d[:, RET_DK // 2:]
    return jnp.concatenate([d1 * cs + d2 * sn, d2 * cs - d1 * sn], axis=-1)


def _ret_specs(B, S, reverse):
    n_sc = S // RET_SC

    def cc(c):
        return n_sc - 1 - c if reverse else c

    return dict(
        proj=pl.BlockSpec((B, RET_SC, RET_HEAD_COLS), lambda h, c: (0, cc(c), h)),
        trig=pl.BlockSpec((RET_SC, RET_DK // 2), lambda h, c: (cc(c), 0)),
        mask=pl.BlockSpec((None, RET_SC, RET_SC), lambda h, c: (h, 0, 0)),
        dec=pl.BlockSpec((None, RET_SC, 1), lambda h, c: (h, 0, 0)),
        cdec=pl.BlockSpec((None, 1, 1), lambda h, c: (h, 0, 0)),
        gn=pl.BlockSpec((1, RET_DV), lambda h, c: (0, h)),
        val=pl.BlockSpec((B, RET_SC, RET_DV), lambda h, c: (0, cc(c), h)),
        state=pl.BlockSpec((B, None, None, RET_DK, RET_DV), lambda h, c: (0, h, cc(c), 0, 0)),
    )


def _ret_qkvg(p_ref, cs, sn):
    q = _rope(p_ref[:, 0:RET_DK].astype(F32), cs, sn)
    k = _rope(p_ref[:, RET_DK:2 * RET_DK].astype(F32), cs, sn) * (RET_DK ** -0.5)
    v = p_ref[:, 2 * RET_DK:2 * RET_DK + RET_DV]
    gate = p_ref[:, 2 * RET_DK + RET_DV:RET_HEAD_COLS].astype(F32)
    return q, k, v, gate


def _group_norm(o):
    mu = jnp.mean(o, axis=-1, keepdims=True)
    oc = o - mu
    rstd = lax.rsqrt(jnp.mean(oc * oc, axis=-1, keepdims=True) + EPS)
    return oc * rstd, rstd


def _ret_fwd(proj, consts, gn, B, S, name, ride=None):
    T = B * S
    n_sc = S // RET_SC
    sp = _ret_specs(B, S, False)

    def body(p_ref, cos_ref, sin_ref, m_ref, qd_ref, kd_ref, cd_ref, gn_ref, y_ref, o_ref, st_ref, state):
        @pl.when(pl.program_id(1) == 0)
        def _():
            state[...] = jnp.zeros_like(state)

        seqs = range(B)
        qkvg = [_ret_qkvg(p_ref.at[b], cos_ref[...], sin_ref[...]) for b in seqs]
        qb = [qkvg[b][0].astype(BF16) for b in seqs]
        kb = [qkvg[b][1].astype(BF16) for b in seqs]
        kt = [(qkvg[b][1] * kd_ref[...]).astype(BF16) for b in seqs]
        v = [qkvg[b][2] for b in seqs]
        sc = [_dot_nt(qb[b], kb[b]) for b in seqs]
        sb = [state[b].astype(BF16) for b in seqs]
        cross = [_dot(qb[b], sb[b]) for b in seqs]
        for b in seqs:
            st_ref[b] = sb[b]
        p = [(sc[b] * m_ref[...]).astype(BF16) for b in seqs]
        o = [_dot(p[b], v[b]) + cross[b] * qd_ref[...] for b in seqs]
        upd = [_dot_tn(kt[b], v[b]) for b in seqs]
        for b in seqs:
            o_ref[b] = o[b].astype(BF16)
            ohat, _ = _group_norm(o[b])
            gate = qkvg[b][3]
            y_ref[b] = (gate * jax.nn.sigmoid(gate) * (ohat * gn_ref[...])).astype(BF16)
            state[b] = state[b] * cd_ref[...] + upd[b]

    res = _call(
        body, name=name, grid=(RET_HEADS, n_sc),
        in_specs=[sp["proj"], sp["trig"], sp["trig"], sp["mask"], sp["dec"], sp["dec"], sp["cdec"], sp["gn"]],
        out_specs=[sp["val"], sp["val"], sp["state"]],
        out_shape=[jax.ShapeDtypeStruct((B, S, RET_VW), BF16), jax.ShapeDtypeStruct((B, S, RET_VW), BF16),
                   jax.ShapeDtypeStruct((B, RET_HEADS, n_sc, RET_DK, RET_DV), BF16)],
        scratch_shapes=[pltpu.VMEM((B, RET_DK, RET_DV), F32)],
        args=(proj.reshape(B, S, -1), *consts, gn), ride=ride)
    (y, o, states), lands = res if ride is not None else (res, None)
    outs = [y.reshape(T, RET_VW), o.reshape(T, RET_VW), states]
    return outs if ride is None else (outs, lands)


def _ret_bwd(proj, consts, gn, o, states, dy, B, S, name, ride=None):
    T = B * S
    n_sc = S // RET_SC
    sp = _ret_specs(B, S, True)

    def body(p_ref, cos_ref, sin_ref, m_ref, qd_ref, kd_ref, cd_ref, gn_ref, o_ref, st_ref, dy_ref,
             dp_ref, dgn_ref, dstate):
        @pl.when(pl.program_id(1) == 0)
        def _():
            dstate[...] = jnp.zeros_like(dstate)
            dgn_ref[...] = jnp.zeros_like(dgn_ref)

        seqs = range(B)
        cs, sn = cos_ref[...], sin_ref[...]
        m, gnv = m_ref[...], gn_ref[...]
        qkvg = [_ret_qkvg(p_ref.at[b], cs, sn) for b in seqs]
        qb = [qkvg[b][0].astype(BF16) for b in seqs]
        kb = [qkvg[b][1].astype(BF16) for b in seqs]
        kt = [(qkvg[b][1] * kd_ref[...]).astype(BF16) for b in seqs]
        v = [qkvg[b][2] for b in seqs]
        sc = [_dot_nt(qb[b], kb[b]) for b in seqs]
        dsb = [dstate[b].astype(BF16) for b in seqs]
        dv_state = [_dot(kt[b], dsb[b]) for b in seqs]
        dkt = [_dot_nt(v[b], dsb[b]) for b in seqs]
        do, dgate = [], []
        for b in seqs:
            gate = qkvg[b][3]
            ohat, rstd = _group_norm(o_ref[b].astype(F32))
            dyv = dy_ref[b].astype(F32)
            sg = jax.nn.sigmoid(gate)
            don = dyv * (gate * sg)
            dgate.append(dyv * (ohat * gnv) * (sg * (1.0 + gate * (1.0 - sg))))
            dgn_ref[...] += jnp.sum(don * ohat, axis=0, keepdims=True)
            dohat = don * gnv
            do.append(rstd * (dohat - jnp.mean(dohat, axis=-1, keepdims=True)
                              - ohat * jnp.mean(dohat * ohat, axis=-1, keepdims=True)))
        dob = [do[b].astype(BF16) for b in seqs]
        doq = [(do[b] * qd_ref[...]).astype(BF16) for b in seqs]
        dsc_f = [_dot_nt(dob[b], v[b]) for b in seqs]
        dq_state = [_dot_nt(doq[b], st_ref[b]) for b in seqs]
        dstate_upd = [_dot_tn(qb[b], doq[b]) for b in seqs]
        p = [(sc[b] * m).astype(BF16) for b in seqs]
        dsc = [(dsc_f[b] * m).astype(BF16) for b in seqs]
        dv = [_dot_tn(p[b], dob[b]) + dv_state[b] for b in seqs]
        dq = [_dot(dsc[b], kb[b]) + dq_state[b] for b in seqs]
        dk = [(_dot_tn(dsc[b], qb[b]) + dkt[b] * kd_ref[...]) * (RET_DK ** -0.5) for b in seqs]
        for b in seqs:
            dstate[b] = dstate[b] * cd_ref[...] + dstate_upd[b]
            dp_ref[b, :, 0:RET_DK] = _rope_bwd(dq[b], cs, sn).astype(BF16)
            dp_ref[b, :, RET_DK:2 * RET_DK] = _rope_bwd(dk[b], cs, sn).astype(BF16)
            dp_ref[b, :, 2 * RET_DK:2 * RET_DK + RET_DV] = dv[b].astype(BF16)
            dp_ref[b, :, 2 * RET_DK + RET_DV:RET_HEAD_COLS] = dgate[b].astype(BF16)

    res = _call(
        body, name=name, grid=(RET_HEADS, n_sc),
        in_specs=[sp["proj"], sp["trig"], sp["trig"], sp["mask"], sp["dec"], sp["dec"], sp["cdec"], sp["gn"],
                  sp["val"], sp["state"], sp["val"]],
        out_specs=[sp["proj"], sp["gn"]],
        out_shape=[jax.ShapeDtypeStruct((B, S, RET_HEADS * RET_HEAD_COLS), BF16), jax.ShapeDtypeStruct((1, RET_VW), F32)],
        scratch_shapes=[pltpu.VMEM((B, RET_DK, RET_DV), F32)],
        args=(proj.reshape(B, S, -1), *consts, gn, o.reshape(B, S, -1), states, dy.reshape(B, S, -1)), ride=ride)
    (dproj, dgn), lands = res if ride is not None else (res, None)
    outs = [dproj.reshape(T, -1), dgn]
    return outs if ride is None else (outs, lands)


def _ring_index():
    u = np.arange(ATT_RING)
    offset = np.where(u < ATT_KW, u, u - ATT_RING)
    return np.clip(offset - BAND_PAST, -MAX_REL, CHUNK - 1) + MAX_REL


def _bias_ring(rel):
    n_clip = BAND_PAST - MAX_REL
    n_hi = ATT_KW - n_clip - REL_TABLE
    ring = jnp.concatenate([jnp.broadcast_to(rel[:, :1], (ATT_HEADS, n_clip)), rel,
                            jnp.broadcast_to(rel[:, -1:], (ATT_HEADS, n_hi)),
                            jnp.broadcast_to(rel[:, :1], (ATT_HEADS, ATT_CQ))], axis=1)
    return jnp.broadcast_to(ring.reshape(ATT_PAIRS, 2, 1, ATT_RING), (ATT_PAIRS, 2, 8, ATT_RING))


def _band_masks():
    i = np.arange(ATT_CQ)[:, None]
    j = np.arange(ATT_KW)[None, :]
    lo = CHUNK * (i // CHUNK)
    band = np.where((j >= lo) & (j < lo + BAND_PAST + CHUNK), 0.0, NEG)
    return np.stack([band + np.where(j + v * ATT_CQ >= BAND_PAST, 0.0, NEG)
                     for v in range(ATT_VARIANTS)]).astype(np.float32)


def _attn_bias(bias_scr, ring_ref, band_ref):
    for hh in range(2):
        rows = jnp.broadcast_to(ring_ref[hh, 0:1, :], (ATT_CQ, ATT_RING))
        toeplitz = pltpu.roll(rows, 0, 1, stride=1, stride_axis=0)[:, :ATT_KW]
        for v in range(ATT_VARIANTS):
            bias_scr[v, hh] = toeplitz + band_ref[v]


ATT_STRIP = 32
ATT_SCALE = ATT_DH ** -0.5


def _strips(fn):
    def strip(r, carry):
        fn(pl.ds(pl.multiple_of(r * ATT_STRIP, ATT_STRIP), ATT_STRIP))
        return carry

    lax.fori_loop(0, ATT_CQ // ATT_STRIP, strip, 0, unroll=True)


def _attn_prepare(kpad, vpad, qm, qkv_ref):
    kpad[0:BAND_PAST, :] = jnp.zeros((BAND_PAST, LANES), BF16)
    vpad[0:BAND_PAST, :] = jnp.zeros((BAND_PAST, LANES), BF16)
    kpad[BAND_PAST:, :] = qkv_ref[1]
    vpad[BAND_PAST:, :] = qkv_ref[2]
    lane = lax.broadcasted_iota(jnp.int32, (1, LANES), 1)
    q = qkv_ref[0] * ATT_SCALE
    for hh in range(2):
        qm[hh] = jnp.where((lane >= ATT_DH * hh) & (lane < ATT_DH * (hh + 1)), q, jnp.zeros_like(q))


def _attn_scores(s_ref, qm, kpad, b_ref, t, n_qb):
    t = jnp.minimum(t, n_qb - 1)
    qs = pl.multiple_of(t * ATT_CQ, ATT_CQ)
    variant = jnp.minimum(t, ATT_VARIANTS - 1)
    kw = kpad[pl.ds(qs, ATT_KW), :]
    for hh in range(2):
        s_ref[hh] = _dot_nt(qm[hh, pl.ds(qs, ATT_CQ), :], kw) + b_ref[variant, hh]


def _attn_softmax(s_ref, e_ref, linv_ref, m_ref=None):
    def strip(rows):
        s = s_ref[rows, :]
        m = jnp.max(s, axis=-1, keepdims=True)
        e = jnp.exp(s - m)
        e_ref[rows, :] = e.astype(BF16)
        linv_ref[rows, :] = jnp.broadcast_to(1.0 / jnp.sum(e, axis=-1, keepdims=True), (ATT_STRIP, LANES))
        if m_ref is not None:
            m_ref[rows, :] = jnp.broadcast_to(m, (ATT_STRIP, LANES))

    _strips(strip)


_RING_SPEC = pl.BlockSpec((None, 2, 8, ATT_RING), lambda hp, b: (hp, 0, 0, 0))
_BAND_SPEC = pl.BlockSpec((ATT_VARIANTS, ATT_CQ, ATT_KW), lambda hp, b: (0, 0, 0))


def _attn_fwd(qkv3, ring, B, S, name, ride=None):
    T = B * S
    n_qb = S // ATT_CQ

    def body(qkv_ref, ring_ref, band_ref, o_ref, kpad, vpad, qm, b_ref, s_scr, e_scr, linv_scr):
        @pl.when(pl.program_id(1) == 0)
        def _():
            _attn_bias(b_ref, ring_ref, band_ref)

        _attn_prepare(kpad, vpad, qm, qkv_ref)
        e_scr[...] = jnp.zeros_like(e_scr)
        linv_scr[...] = jnp.zeros_like(linv_scr)
        lane = lax.broadcasted_iota(jnp.int32, (1, LANES), 1)

        def softmax(slot):
            for hh in range(2):
                _attn_softmax(s_scr.at[slot, hh], e_scr.at[slot, hh], linv_scr.at[slot, hh])

        def output(t, slot):
            qs = pl.multiple_of(jnp.maximum(t, 0) * ATT_CQ, ATT_CQ)
            vw = vpad[pl.ds(qs, ATT_KW), :]
            outs = [_dot(e_scr[slot, hh], vw) * linv_scr[slot, hh] for hh in range(2)]
            o_ref[pl.ds(qs, ATT_CQ), :] = jnp.where(lane < ATT_DH, outs[0], outs[1]).astype(BF16)

        def pair(u, carry):
            t = 2 * u
            _attn_scores(s_scr.at[1], qm, kpad, b_ref, t + 1, n_qb)
            softmax(0)
            output(t - 1, 1)
            _attn_scores(s_scr.at[0], qm, kpad, b_ref, t + 2, n_qb)
            softmax(1)
            output(t, 0)
            return carry

        _attn_scores(s_scr.at[0], qm, kpad, b_ref, 0, n_qb)
        lax.fori_loop(0, n_qb // 2, pair, 0)
        output(n_qb - 1, 1)

    return _call(body, name=name, grid=(ATT_PAIRS, B),
                 in_specs=[pl.BlockSpec((3, S, LANES), lambda hp, b: (0, b, hp)), _RING_SPEC, _BAND_SPEC],
                 out_specs=[pl.BlockSpec((S, LANES), lambda hp, b: (b, hp))],
                 out_shape=[jax.ShapeDtypeStruct((T, D_MODEL), BF16)],
                 scratch_shapes=[pltpu.VMEM((S + BAND_PAST, LANES), BF16), pltpu.VMEM((S + BAND_PAST, LANES), BF16),
                                 pltpu.VMEM((2, S, LANES), BF16), pltpu.VMEM((ATT_VARIANTS, 2, ATT_CQ, ATT_KW), F32),
                                 pltpu.VMEM((2, 2, ATT_CQ, ATT_KW), F32),
                                 pltpu.VMEM((2, 2, ATT_CQ, ATT_KW), BF16), pltpu.VMEM((2, 2, ATT_CQ, LANES), F32)],
                 args=(qkv3, ring, jnp.asarray(_band_masks())), ride=ride)


def _attn_bwd(qkv3, ring, do, B, S, name, ride=None):
    T = B * S
    n_qb = S // ATT_CQ

    def body(qkv_ref, ring_ref, band_ref, do_ref, dqkv_ref, dring_ref, kpad, vpad, qm, dkacc, dvacc, b_ref, db_ref):
        @pl.when(pl.program_id(1) == 0)
        def _():
            _attn_bias(b_ref, ring_ref, band_ref)
            db_ref[...] = jnp.zeros_like(db_ref)

        _attn_prepare(kpad, vpad, qm, qkv_ref)
        dkacc[...] = jnp.zeros_like(dkacc)
        dvacc[...] = jnp.zeros_like(dvacc)
        lane = lax.broadcasted_iota(jnp.int32, (1, LANES), 1)

        def step(qb, carry):
            qs = pl.multiple_of(qb * ATT_CQ, ATT_CQ)
            variant = jnp.minimum(qb, ATT_VARIANTS - 1)
            dov = do_ref[pl.ds(qs, ATT_CQ), :].astype(F32)
            kw = kpad[pl.ds(qs, ATT_KW), :]
            vw = vpad[pl.ds(qs, ATT_KW), :]
            heads = (0, 1)
            qmh = [qm[hh, pl.ds(qs, ATT_CQ), :] for hh in heads]
            s = [_dot_nt(qmh[hh], kw) + b_ref[variant, hh] for hh in heads]
            e, linv, dom, dp = [None, None], [None, None], [None, None], [None, None]
            for hh in heads:
                e[hh] = jnp.exp(s[hh] - jnp.max(s[hh], axis=-1, keepdims=True))
                linv[hh] = 1.0 / jnp.sum(e[hh], axis=-1, keepdims=True)
                sel = (lane >= ATT_DH * hh) & (lane < ATT_DH * (hh + 1))
                dom[hh] = jnp.where(sel, dov * linv[hh], 0.0).astype(BF16)
                dp[hh] = _dot_nt(dom[hh], vw)
            dqs, dk, dv = [], None, None
            for hh in heads:
                ds = e[hh] * (dp[hh] - jnp.sum(dp[hh] * e[hh], axis=-1, keepdims=True) * linv[hh])
                db_ref[hh] += ds
                dsb = ds.astype(BF16)
                dqs.append(_dot(dsb, kw) * ATT_SCALE)
                dk_h = _dot_tn(qmh[hh], dsb)
                dv_h = _dot_tn(dom[hh], e[hh].astype(BF16))
                dk = dk_h if dk is None else dk + dk_h
                dv = dv_h if dv is None else dv + dv_h
            dqkv_ref[0, pl.ds(qs, ATT_CQ), :] = jnp.where(lane < ATT_DH, dqs[0], dqs[1]).astype(BF16)
            dkacc[:, pl.ds(qs, ATT_KW)] += dk
            dvacc[:, pl.ds(qs, ATT_KW)] += dv
            return carry

        lax.fori_loop(0, n_qb, step, 0)
        dqkv_ref[1] = dkacc[:, BAND_PAST:].T.astype(BF16)
        dqkv_ref[2] = dvacc[:, BAND_PAST:].T.astype(BF16)

        @pl.when(pl.program_id(1) == B - 1)
        def _():
            r = lax.broadcasted_iota(jnp.int32, (ATT_CQ, ATT_CQ), 0)
            c = lax.broadcasted_iota(jnp.int32, (ATT_CQ, ATT_CQ), 1)
            reverse = jnp.where(r + c == ATT_CQ - 1, 1.0, 0.0).astype(BF16)
            for hh in range(2):
                x = jnp.concatenate([db_ref[hh], jnp.zeros((ATT_CQ, ATT_RING - ATT_KW), F32)], axis=1)
                flipped = jnp.zeros((ATT_CQ, ATT_RING), F32)
                for _ in range(3):
                    part = x.astype(BF16)
                    flipped = flipped + _dot(reverse, part)
                    x = x - part.astype(F32)
                aligned = pltpu.roll(flipped, ATT_KW + 1, 1, stride=1, stride_axis=0)
                dring_ref[hh] = jnp.sum(aligned.reshape(ATT_CQ // 8, 8, ATT_RING), axis=0)

    qkv_spec = pl.BlockSpec((3, S, LANES), lambda hp, b: (0, b, hp))
    return _call(body, name=name, grid=(ATT_PAIRS, B),
                 in_specs=[qkv_spec, _RING_SPEC, _BAND_SPEC, pl.BlockSpec((S, LANES), lambda hp, b: (b, hp))],
                 out_specs=[qkv_spec, _RING_SPEC],
                 out_shape=[jax.ShapeDtypeStruct((3, T, D_MODEL), BF16),
                            jax.ShapeDtypeStruct((ATT_PAIRS, 2, 8, ATT_RING), F32)],
                 scratch_shapes=[pltpu.VMEM((S + BAND_PAST, LANES), BF16), pltpu.VMEM((S + BAND_PAST, LANES), BF16),
                                 pltpu.VMEM((2, S, LANES), BF16),
                                 pltpu.VMEM((LANES, S + BAND_PAST), F32), pltpu.VMEM((LANES, S + BAND_PAST), F32),
                                 pltpu.VMEM((ATT_VARIANTS, 2, ATT_CQ, ATT_KW), F32), pltpu.VMEM((2, ATT_CQ, ATT_KW), F32)],
                 args=(qkv3, ring, jnp.asarray(_band_masks()), do), ride=ride)


def _rel_bias_grad(dring, name):
    fold = np.zeros((ATT_RING, REL_PAD), np.float32)
    fold[np.arange(ATT_RING), _ring_index()] = 1.0
    fold = jnp.asarray(fold, BF16)

    def body(d_ref, f_ref, o_ref):
        x = jnp.sum(d_ref[...], axis=0, keepdims=True)
        x = jnp.broadcast_to(x, (8, ATT_RING))
        acc = jnp.zeros((8, REL_PAD), F32)
        for _ in range(3):
            part = x.astype(BF16)
            acc = acc + _dot(part, f_ref[...])
            x = x - part.astype(F32)
        o_ref[...] = acc[0:1, :]

    out = _call(body, name=name, grid=(ATT_HEADS,),
                in_specs=[pl.BlockSpec((None, 8, ATT_RING), lambda h: (h, 0, 0)),
                          pl.BlockSpec((ATT_RING, REL_PAD), lambda h: (0, 0))],
                out_specs=[pl.BlockSpec((None, 1, REL_PAD), lambda h: (h, 0, 0))],
                out_shape=[jax.ShapeDtypeStruct((ATT_HEADS, 1, REL_PAD), F32)],
                args=(dring.reshape(ATT_HEADS, 8, ATT_RING), fold))[0]
    return out.reshape(ATT_HEADS, REL_PAD)


def _adamw(w, g, m, v):
    m = ADAM_B1 * m + (1.0 - ADAM_B1) * g
    v = ADAM_B2 * v + (1.0 - ADAM_B2) * (g * g)
    m_hat = m / (1.0 - ADAM_B1 ** ADAM_STEP)
    v_hat = v / (1.0 - ADAM_B2 ** ADAM_STEP)
    delta = -ADAM_LR * (m_hat / (jnp.sqrt(v_hat) + ADAM_EPS) + ADAM_WD * w)
    return delta, m, v


def _sum_devices(ref):
    g = ref[0].astype(F32)
    for d in range(1, N_DEV):
        g = g + ref[d].astype(F32)
    return g


def _adamw_reduce(lands, w, m, v, name, tr=256):
    L, R, C = w.shape
    tr = min(tr, R)
    n_i = R // tr

    def body(*refs):
        l_refs = refs[:L]
        w_ref, m_ref, v_ref, g_out, d_out, m_out, v_out = refs[L:]
        layer = pl.program_id(0)
        for l in range(L):
            @pl.when(layer == l)
            def _(l=l):
                g = _sum_devices(l_refs[l])
                g_out[...] = g
                d_out[...], m_out[...], v_out[...] = _adamw(w_ref[...], g, m_ref[...], v_ref[...])

    def land_spec(l):
        return pl.BlockSpec((N_DEV, tr, C), lambda ly, i: (0, jnp.where(ly == l, i, jnp.where(ly < l, 0, n_i - 1)), 0))

    blk = pl.BlockSpec((None, tr, C), lambda ly, i: (ly, i, 0))
    return _call(body, name=name, grid=(L, n_i), in_specs=[land_spec(l) for l in range(L)] + [blk, blk, blk],
                 out_specs=[blk] * 4, out_shape=[jax.ShapeDtypeStruct((L, R, C), F32)] * 4, args=(*lands, w, m, v))


def _small_update(land_small, land_rel, w, m, v, name):
    def body(ls_ref, lr_ref, w_ref, m_ref, v_ref, g_out, d_out, m_out, v_out, rel_out):
        g = _sum_devices(ls_ref)
        g_out[...] = g
        d_out[...], m_out[...], v_out[...] = _adamw(w_ref[...], g, m_ref[...], v_ref[...])
        rel_out[...] = _sum_devices(lr_ref)

    return pl.pallas_call(
        body, name=name,
        out_shape=[jax.ShapeDtypeStruct(w.shape, F32)] * 4 + [jax.ShapeDtypeStruct(land_rel.shape[1:], F32)],
    )(land_small, land_rel, w, m, v)


def _adamw_plain(w, g, m, v, name):
    def body(w_ref, g_ref, m_ref, v_ref, d_out, m_out, v_out):
        d_out[...], m_out[...], v_out[...] = _adamw(w_ref[...], g_ref[...], m_ref[...], v_ref[...])

    return pl.pallas_call(body, name=name, out_shape=[jax.ShapeDtypeStruct(w.shape, F32)] * 3)(w, g, m, v)


def _ret_piece_of_column_block(p):
    per_head = RET_HEAD_COLS // RET_DK
    qk = jnp.where(p < RET_HEADS, per_head * p, per_head * (p - RET_HEADS) + 1)
    pv = p - 2 * RET_HEADS
    vv = per_head * (pv // 2) + 2 + pv % 2
    pg = p - 4 * RET_HEADS
    gg = per_head * (pg // 2) + 4 + pg % 2
    return jnp.where(p < 2 * RET_HEADS, qk, jnp.where(p < 4 * RET_HEADS, vv, gg))


def _step(x, target, shards, mix_g, gn_g, rel_shard, mlp_g, final_g):
    s_rin, s_rout, s_ain, s_aout, s_w1, s_w2 = shards
    B, S, _ = x.shape
    T = B * S
    tk = min(T, 2048)
    n_k = T // tk
    h0 = x.reshape(T, D_MODEL)
    tgt = target.reshape(T, D_MODEL)
    consts = _ret_constants(S)
    w1_cols = w2_rows = D_FF // N_DEV

    rel_cols = REL_TABLE // N_DEV
    per_shard = RET_IN // N_DEV // RET_DK

    def head_layout(r):
        return lambda ref, blk: ref.at[:, pl.ds(pl.multiple_of(
            _ret_piece_of_column_block(per_shard * blk + r) * RET_DK, LANES), RET_DK)]

    w_rin_heads, rel_all = _all_gather(
        [s_rin[:, r * RET_DK:(r + 1) * RET_DK] for r in range(per_shard)] + [rel_shard],
        [jax.ShapeDtypeStruct((D_MODEL, RET_IN), BF16), jax.ShapeDtypeStruct((N_DEV, ATT_HEADS, rel_cols), F32)],
        [0] * per_shard + [1], [head_layout(r) for r in range(per_shard)] + [_slot], "gather_first")
    ring = _bias_ring(rel_all.transpose(1, 0, 2).reshape(ATT_HEADS, REL_TABLE))

    def gather(*items):
        return _Exchange([s for s, _, _ in items], [(a, _whole, a, view) for a, (_, view, _) in enumerate(items)],
                         [jax.ShapeDtypeStruct(full, BF16) for _, _, full in items])

    def scatter(*items):
        return _Exchange([p for p, _, _ in items], [(a, view, a, _slot) for a, (_, view, _) in enumerate(items)],
                         [jax.ShapeDtypeStruct((N_DEV, *shard), BF16) for _, _, shard in items])

    w1_full, w2_full = (D_MODEL, D_FF), (D_FF, D_MODEL)
    w1_shard, w2_shard = (D_MODEL, w1_cols), (w2_rows, D_MODEL)

    ride = gather((s_rout, _rows(RET_VW // N_DEV), (RET_VW, D_MODEL)), (s_w1[0], _cols(w1_cols), w1_full))
    (proj, n0), (w_rout, w1_0) = _norm_proj(h0, mix_g[0:1], w_rin_heads, 1, "ret_proj", 512, ride=ride)
    (y, o, states), (w2_0,) = _ret_fwd(proj, consts, gn_g, B, S, "ret_fwd", ride=gather((s_w2[0], _rows(w2_rows), w2_full)))
    h1 = _matmul_res(y, w_rout, h0, "ret_out")
    ride = gather((s_ain, _cols(3 * D_MODEL // N_DEV), (D_MODEL, 3 * D_MODEL)),
                  (s_aout, _rows(D_MODEL // N_DEV), (D_MODEL, D_MODEL)))
    (h2, n1, u1), (w_ain, w_aout) = _mlp_fwd(h1, mlp_g[0:1], w1_0, w2_0, "mlp0_fwd", ride=ride)
    qkv3, n2 = _norm_proj(h2, mix_g[1:2], w_ain, 3, "att_proj", 1024)
    ride = gather((s_w1[1], _cols(w1_cols), w1_full), (s_w2[1], _rows(w2_rows), w2_full))
    (o2,), (w1_1, w2_1) = _attn_fwd(qkv3, ring, B, S, "att_fwd", ride=ride)
    h3 = _matmul_res(o2, w_aout, h2, "att_out")
    h4, n3, u3 = _mlp_fwd(h3, mlp_g[1:2], w1_1, w2_1, "mlp1_fwd")
    dh4, dh4b, loss, dg_final = _final_loss(h4, final_g, tgt, "final_loss")

    def tok(width):
        return dict(a=pl.BlockSpec((tk, width), lambda i, j, k: (k, i)), b=pl.BlockSpec((tk, width), lambda i, j, k: (k, j)))

    def mlp_grads(dh, dhb, u, n, w1, w2, h, g, tag, ride=None, w2_rides_on_dw1=False):
        res = _mlp_bwd(dh, u, w1, w2, h, g, tag + "_bwd", ride=ride)
        (dhi, dhib, du, dg), lands = res if ride is not None else (res, None)
        gw2 = _wgrad(u, tok(1024)["a"], dhb, tok(1024)["b"], D_FF, D_MODEL, 1024, 1024, n_k, tag + "_dw2", square_a=True)
        ride_w2 = scatter((gw2, _rows(w2_rows), w2_shard)) if w2_rides_on_dw1 else None
        gw1 = _wgrad(n, tok(1024)["a"], du, tok(1024)["b"], D_MODEL, D_FF, 1024, 1024, n_k, tag + "_dw1", ride=ride_w2)
        return dhi, dhib, gw1, gw2, dg, lands

    dh3, dh3b, gw1_1, gw2_1, dg_mlp1, _ = mlp_grads(dh4, dh4b, u3, n3, w1_1, w2_1, h3, mlp_g[1:2], "mlp1")
    do2 = _matmul_nt(dh3b, w_aout, "att_out_bwd")
    g_aout = _wgrad(o2, tok(1024)["a"], dh3b, tok(1024)["b"], D_MODEL, D_MODEL, 1024, 1024, n_k, "att_out_dw")
    ride = scatter((gw1_1, _cols(w1_cols), w1_shard), (gw2_1, _rows(w2_rows), w2_shard),
                   (g_aout, _rows(D_MODEL // N_DEV), (D_MODEL // N_DEV, D_MODEL)))
    (dqkv3, dring), (l_w1_1, l_w2_1, l_aout) = _attn_bwd(qkv3, ring, do2, B, S, "att_bwd", ride=ride)
    g_rel = _rel_bias_grad(dring, "rel_bias_grad")
    dh2, dh2b, dg_mix1 = _proj_bwd(dqkv3, w_ain, h2, mix_g[1:2], dh3, "att_proj_bwd", 1024)
    g_ain = _wgrad(n2, tok(1024)["a"], dqkv3, pl.BlockSpec((None, tk, D_MODEL), lambda i, j, k: (j, k, 0)),
                   D_MODEL, 3 * D_MODEL, 1024, 1024, n_k, "att_proj_dw")
    ride = scatter((g_ain, _cols(3 * D_MODEL // N_DEV), (D_MODEL, 3 * D_MODEL // N_DEV)))
    dh1, dh1b, (gw1_0, (l_w2_0,)), _, dg_mlp0, (l_ain,) = mlp_grads(
        dh2, dh2b, u1, n1, w1_0, w2_0, h1, mlp_g[0:1], "mlp0", ride=ride, w2_rides_on_dw1=True)
    dy = _matmul_nt(dh1b, w_rout, "ret_out_bwd")
    g_rout = _wgrad(y, tok(1024)["a"], dh1b, tok(1024)["b"], RET_VW, D_MODEL, 1024, 1024, n_k, "ret_out_dw")
    ride = scatter((gw1_0, _cols(w1_cols), w1_shard), (g_rout, _rows(RET_VW // N_DEV), (RET_VW // N_DEV, D_MODEL)))
    (dproj, dgn), (l_w1_0, l_rout) = _ret_bwd(proj, consts, gn_g, o, states, dy, B, S, "ret_bwd", ride=ride)
    g_rin =_wgrad(n0, tok(1024)["a"], dproj,
                   [pl.BlockSpec((tk, RET_DK), functools.partial(
                       lambda i, j, k, r: (k, _ret_piece_of_column_block(per_shard * j + r)), r=r)) for r in range(per_shard)],
                   D_MODEL, RET_IN, 1024, per_shard * RET_DK, n_k, "ret_proj_dw")
    ride = scatter((g_rin, _cols(RET_IN // N_DEV), (D_MODEL, RET_IN // N_DEV)))
    (dx, _, dg_mix0), (l_rin,) = _proj_bwd(dproj, w_rin_heads, h0, mix_g[0:1], dh1, "ret_proj_bwd", 512, ride=ride)

    small = jnp.concatenate([dg_mix0, dg_mix1, dg_mlp0, dg_mlp1, dg_final, dgn.reshape(2, D_MODEL),
                             jnp.zeros((1, D_MODEL), F32)], axis=0)
    l_small, l_rel = _Exchange(
        [small, g_rel], [(0, _whole, 0, _slot), (1, _whole, 1, _slot)],
        [jax.ShapeDtypeStruct((N_DEV, 8, D_MODEL), F32), jax.ShapeDtypeStruct((N_DEV, ATT_HEADS, REL_PAD), F32)],
    ).run("scatter_small")
    lands = dict(rin=l_rin, rout=l_rout, ain=l_ain, aout=l_aout, w1=(l_w1_0, l_w1_1), w2=(l_w2_0, l_w2_1),
                 small=l_small, rel=l_rel)
    return loss[0, 0], dx.reshape(B, S, D_MODEL), lands


def kernel(x, mix_norm_g, ret_w_in, ret_gn_g, ret_w_out, att_w_in, att_rel_bias, att_w_out, mlp_norm_g, mlp_w1, mlp_w2, final_norm_g, loss_target, m_mix_norm_g, m_ret_w_in, m_ret_gn_g, m_ret_w_out, m_att_w_in, m_att_rel_bias, m_att_w_out, m_mlp_norm_g, m_mlp_w1, m_mlp_w2, m_final_norm_g, v_mix_norm_g, v_ret_w_in, v_ret_gn_g, v_ret_w_out, v_att_w_in, v_att_rel_bias, v_att_w_out, v_mlp_norm_g, v_mlp_w1, v_mlp_w2, v_final_norm_g):
    me = _lin(_place())
    rel_cols = REL_TABLE // N_DEV
    shards = (ret_w_in[0].astype(BF16), ret_w_out[0].astype(BF16), att_w_in[0].astype(BF16), att_w_out[0].astype(BF16),
              (mlp_w1[0].astype(BF16), mlp_w1[1].astype(BF16)), (mlp_w2[0].astype(BF16), mlp_w2[1].astype(BF16)))
    loss_part, grad_x, lands = _step(x, loss_target, shards, mix_norm_g, ret_gn_g, att_rel_bias[0], mlp_norm_g,
                                     final_norm_g.reshape(1, D_MODEL))
    loss = lax.psum(loss_part, ("x", "y", "c"))

    u_rin = _adamw_reduce([lands["rin"]], ret_w_in, m_ret_w_in, v_ret_w_in, "update_ret_w_in")
    u_rout = _adamw_reduce([lands["rout"]], ret_w_out, m_ret_w_out, v_ret_w_out, "update_ret_w_out")
    u_ain = _adamw_reduce([lands["ain"]], att_w_in, m_att_w_in, v_att_w_in, "update_att_w_in")
    u_aout = _adamw_reduce([lands["aout"]], att_w_out, m_att_w_out, v_att_w_out, "update_att_w_out")
    u_w1 = _adamw_reduce(lands["w1"], mlp_w1, m_mlp_w1, v_mlp_w1, "update_mlp_w1")
    u_w2 = _adamw_reduce(lands["w2"], mlp_w2, m_mlp_w2, v_mlp_w2, "update_mlp_w2")

    def pack(mix, mlp, fin, gn):
        return jnp.concatenate([mix, mlp, fin.reshape(1, D_MODEL), gn.reshape(2, D_MODEL), jnp.zeros((1, D_MODEL), F32)], axis=0)

    small_w = pack(mix_norm_g, mlp_norm_g, final_norm_g, ret_gn_g)
    small_m = pack(m_mix_norm_g, m_mlp_norm_g, m_final_norm_g, m_ret_gn_g)
    small_v = pack(v_mix_norm_g, v_mlp_norm_g, v_final_norm_g, v_ret_gn_g)
    sg, sd, sm, sv, rel_sum = _small_update(lands["small"], lands["rel"], small_w, small_m, small_v, "update_small")
    g_rel_mine = lax.dynamic_slice(rel_sum, (0, me * rel_cols), (ATT_HEADS, rel_cols))
    rel_d, rel_m, rel_v = _adamw_plain(att_rel_bias[0], g_rel_mine, m_att_rel_bias[0], v_att_rel_bias[0], "update_rel_bias")
    u_rel = [g_rel_mine[None], rel_d[None], rel_m[None], rel_v[None]]

    def unpack(t):
        return dict(mix=t[0:2], mlp=t[2:4], fin=t[4], gn=t[5:7].reshape(1, RET_VW))

    us = [unpack(t) for t in (sg, sd, sm, sv)]
    outs = [loss, grad_x]
    for k in range(4):
        outs += [us[k]["mix"], u_rin[k], us[k]["gn"], u_rout[k], u_ain[k], u_rel[k], u_aout[k], us[k]["mlp"],
                 u_w1[k], u_w2[k], us[k]["fin"]]
    return tuple(outs)
```

```python
import functools

import numpy as np
import jax
import jax.numpy as jnp
from jax import lax
from jax.experimental import pallas as pl
from jax.experimental.pallas import tpu as pltpu

F32, BF16 = jnp.float32, jnp.bfloat16

D_MODEL = 1024
CHUNK = 64
RET_HEADS, RET_DK, RET_DV = 4, 256, 512
RET_QK, RET_VW = RET_HEADS * RET_DK, RET_HEADS * RET_DV
RET_IN = 2 * RET_QK + 2 * RET_VW
RET_HEAD_COLS = 2 * RET_DK + 2 * RET_DV
RET_SC = 256
ROPE_BASE = 10000.0
ATT_HEADS, ATT_DH = 16, 64
LANES = 128
ATT_PAIRS = ATT_HEADS * ATT_DH // LANES
BAND_PAST = 8 * CHUNK
MAX_REL = 256
REL_TABLE = MAX_REL + CHUNK
REL_PAD = 384
ATT_CQ = 256
ATT_KW = ATT_CQ + BAND_PAST
ATT_RING = ATT_CQ + ATT_KW
ATT_VARIANTS = BAND_PAST // ATT_CQ + 1
D_FF = 4 * D_MODEL
EPS = 1e-6
EPILOGUE_ROWS = 256
NEG = -1e30
N_DEV = 8
N_PEERS = N_DEV - 1

ADAM_LR, ADAM_B1, ADAM_B2, ADAM_EPS, ADAM_WD, ADAM_STEP = 0.001, 0.9, 0.999, 1e-08, 0.01, 10

VMEM_LIMIT = 56 * 1024 * 1024
MESH = pl.DeviceIdType.MESH
ANY = pl.BlockSpec(memory_space=pl.ANY)


def _dot(a, b):
    return jnp.dot(a, b, preferred_element_type=F32)


def _dot_nt(a, b):
    return lax.dot_general(a, b, (((1,), (1,)), ((), ())), preferred_element_type=F32)


def _dot_tn(a, b):
    return lax.dot_general(a, b, (((0,), (0,)), ((), ())), preferred_element_type=F32)


def _rms(x, g):
    r = lax.rsqrt(jnp.mean(x * x, axis=-1, keepdims=True) + EPS)
    return x * r * g


def _rms_bwd(dn, x, g):
    r = lax.rsqrt(jnp.mean(x * x, axis=-1, keepdims=True) + EPS)
    xh = x * r
    dg = jnp.sum(dn * xh, axis=0, keepdims=True)
    dxh = dn * g
    dx = r * (dxh - xh * jnp.mean(dxh * xh, axis=-1, keepdims=True))
    return dx, dg


def _rms_bwd_epilogue(x_ref, g_ref, dres_ref, dx_ref, dxb_ref, dg_ref):
    for r in range(0, dx_ref.shape[0], EPILOGUE_ROWS):
        rows = slice(r, r + EPILOGUE_ROWS)
        dx, dg = _rms_bwd(dx_ref[rows, :], x_ref[rows, :], g_ref[...])
        dx = dres_ref[rows, :] + dx
        dx_ref[rows, :] = dx
        dxb_ref[rows, :] = dx.astype(BF16)
        dg_ref[...] += dg


def _place():
    return lax.axis_index("x"), lax.axis_index("y"), lax.axis_index("c")


def _lin(p):
    return 4 * p[0] + 2 * p[1] + p[2]


def _cols(width):
    return lambda ref, blk: ref.at[:, pl.ds(pl.multiple_of(blk * width, LANES), width)]


def _rows(height):
    return lambda ref, blk: ref.at[pl.ds(pl.multiple_of(blk * height, 8), height), :]


def _slot(ref, blk):
    return ref.at[blk]


def _whole(ref, blk):
    return ref


class _Exchange:
    def __init__(self, sources, flows, land_shapes):
        self.sources, self.flows, self.land_shapes = list(sources), flows, list(land_shapes)
        n_f = len(flows)
        self.sem_shapes = [pltpu.SemaphoreType.DMA((N_PEERS * n_f,)), pltpu.SemaphoreType.DMA((N_PEERS * n_f,)),
                           pltpu.SemaphoreType.DMA((n_f,))]

    def _copies(self, srcs, lands, sems):
        send_sems, recv_sems, local_sems = sems
        x, y, c = _place()
        me = (x, y, c)
        peers = [(x ^ dx, y ^ dy, c ^ dc) for dx in (0, 1) for dy in (0, 1) for dc in (0, 1)][1:]

        def copy(f, k, sender, to):
            si, sview, li, lview = self.flows[f]
            return pltpu.make_async_remote_copy(
                src_ref=sview(srcs[si], _lin(to)), dst_ref=lview(lands[li], _lin(sender)),
                send_sem=send_sems.at[f * N_PEERS + k], recv_sem=recv_sems.at[f * N_PEERS + k],
                device_id=to, device_id_type=MESH)

        mine, sends, recvs = [], [], []
        for f, (si, sview, li, lview) in enumerate(self.flows):
            mine.append(pltpu.make_async_copy(sview(srcs[si], _lin(me)), lview(lands[li], _lin(me)), local_sems.at[f]))
            for k, peer in enumerate(peers):
                sends.append(copy(f, k, me, peer))
                recvs.append(copy(f, k, peer, me))
        return mine, sends, recvs

    def start(self, srcs, lands, sems):
        mine, sends, _ = self._copies(srcs, lands, sems)
        for cp in mine + sends:
            cp.start()

    def finish(self, srcs, lands, sems):
        mine, sends, recvs = self._copies(srcs, lands, sems)
        for cp in recvs:
            cp.wait_recv()
        for cp in sends:
            cp.wait_send()
        for cp in mine:
            cp.wait()

    def run(self, name):
        n_src, n_land = len(self.sources), len(self.land_shapes)

        def body(*refs):
            srcs, lands, sems = refs[:n_src], refs[n_src:n_src + n_land], refs[n_src + n_land:]
            self.start(srcs, lands, sems)
            self.finish(srcs, lands, sems)

        return pl.pallas_call(body, name=name, in_specs=[ANY] * n_src, out_specs=[ANY] * n_land,
                              out_shape=self.land_shapes, scratch_shapes=self.sem_shapes)(*self.sources)


def _call(body, *, name, grid, in_specs, out_specs, out_shape, args, scratch_shapes=(), ride=None):
    params = pltpu.CompilerParams(dimension_semantics=("arbitrary",) * len(grid), vmem_limit_bytes=VMEM_LIMIT)
    in_specs, out_specs, out_shape, scratch_shapes = list(in_specs), list(out_specs), list(out_shape), list(scratch_shapes)
    if ride is None:
        return pl.pallas_call(body, name=name, grid=grid, in_specs=in_specs, out_specs=out_specs, out_shape=out_shape,
                              scratch_shapes=scratch_shapes, compiler_params=params)(*args)
    n_in, n_out, n_scr = len(in_specs), len(out_specs), len(scratch_shapes)
    n_src, n_land = len(ride.sources), len(ride.land_shapes)

    def riding(*refs):
        bounds = np.cumsum([n_in, n_src, n_out, n_land, n_scr])
        ins, srcs, outs, lands, scr, sems = (refs[a:b] for a, b in zip([0, *bounds], [*bounds, len(refs)]))
        step = functools.reduce(lambda a, d: a * grid[d] + pl.program_id(d), range(len(grid)), 0)
        n_steps = int(np.prod(grid))

        @pl.when(step == 0)
        def _():
            ride.start(srcs, lands, sems)

        if hasattr(ride, "middle"):
            @pl.when(step == (5 * n_steps) // 8)
            def _():
                ride.middle(srcs, lands, sems)

        body(*ins, *outs, *scr)

        @pl.when(step == n_steps - 1)
        def _():
            ride.finish(srcs, lands, sems)

    res = pl.pallas_call(
        riding, name=name, grid=grid, in_specs=in_specs + [ANY] * n_src, out_specs=out_specs + [ANY] * n_land,
        out_shape=out_shape + ride.land_shapes, scratch_shapes=scratch_shapes + ride.sem_shapes,
        compiler_params=params)(*args, *ride.sources)
    return res[:n_out], res[n_out:]


class _ChipGather:
    def __init__(self, shards, out_shapes, out_of, views):
        self.sources, self.land_shapes, self.out_of, self.views = list(shards), list(out_shapes), out_of, views
        n = len(shards)
        self.sem_shapes = [pltpu.SemaphoreType.DMA((N_PEERS * n,)), pltpu.SemaphoreType.DMA((N_PEERS * n,)),
                           pltpu.SemaphoreType.DMA((n,))]

    def _copies(self, srcs, lands, sems):
        send_sems, recv_sems, local_sems = sems
        n = len(srcs)
        outs = [lands[self.out_of[a]] for a in range(n)]
        x, y, c = _place()
        me, sibling = (x, y, c), (x, y, 1 - c)
        chips = [(1 - x, y), (x, 1 - y), (1 - x, 1 - y)]

        def copy(a, k, block, to, own=False):
            dst = self.views[a](outs[a], _lin(block))
            return pltpu.make_async_remote_copy(
                src_ref=srcs[a] if own else dst, dst_ref=dst,
                send_sem=send_sems.at[a * N_PEERS + k], recv_sem=recv_sems.at[a * N_PEERS + k],
                device_id=to, device_id_type=MESH)

        mine = lambda: [pltpu.make_async_copy(srcs[a], self.views[a](outs[a], _lin(me)), local_sems.at[a]) for a in range(n)]
        first = lambda: [cp for a in range(n) for cp in
                         [copy(a, 0, me, sibling, own=True)] + [copy(a, 1 + j, me, (*chip, c), own=True) for j, chip in enumerate(chips)]]
        landed = lambda a, j: copy(a, 1 + j, (*chips[j], c), me)
        passed = lambda a, j: copy(a, 4 + j, (*chips[j], c), sibling)
        from_sibling = lambda a: [copy(a, 0, sibling, me)] + [copy(a, 4 + j, (*chips[j], 1 - c), me) for j in range(3)]
        return n, mine, first, landed, passed, from_sibling

    def start(self, srcs, lands, sems):
        _, mine, first, _, _, _ = self._copies(srcs, lands, sems)
        for cp in mine() + first():
            cp.start()

    def middle(self, srcs, lands, sems):
        n, _, _, landed, passed, _ = self._copies(srcs, lands, sems)
        for j in range(3):
            for a in range(n):
                landed(a, j).wait_recv()
                passed(a, j).start()

    def finish(self, srcs, lands, sems):
        n, mine, first, _, passed, from_sibling = self._copies(srcs, lands, sems)
        for a in range(n):
            for cp in from_sibling(a):
                cp.wait_recv()
        for cp in first() + [passed(a, j) for j in range(3) for a in range(n)]:
            cp.wait_send()
        for cp in mine():
            cp.wait()

    def run(self, name):
        n, n_out = len(self.sources), len(self.land_shapes)

        def body(*refs):
            parts = refs[:n], refs[n:n + n_out], refs[n + n_out:]
            self.start(*parts)
            self.middle(*parts)
            self.finish(*parts)

        return pl.pallas_call(body, name=name, in_specs=[ANY] * n, out_specs=[ANY] * n_out,
                              out_shape=self.land_shapes, scratch_shapes=self.sem_shapes)(*self.sources)


def _resident(shape):
    return pl.BlockSpec(shape, lambda i: (0,) * len(shape), pipeline_mode=pl.Buffered(1))


def _norm_proj(h, g, w, groups, name, tm, ride=None):
    T = h.shape[0]
    N = w.shape[1]
    width = N // groups

    def body(h_ref, g_ref, w_ref, o_ref, n_ref):
        n = _rms(h_ref[...], g_ref[...]).astype(BF16)
        n_ref[...] = n
        if groups == 1:
            o_ref[...] = _dot(n, w_ref[...]).astype(BF16)
        else:
            for p in range(groups):
                o_ref[p] = _dot(n, w_ref[:, p * width:(p + 1) * width]).astype(BF16)

    row = pl.BlockSpec((tm, D_MODEL), lambda i: (i, 0))
    if groups == 1:
        out_spec, out_shape = pl.BlockSpec((tm, N), lambda i: (i, 0)), jax.ShapeDtypeStruct((T, N), BF16)
    else:
        out_spec = pl.BlockSpec((groups, tm, width), lambda i: (0, i, 0))
        out_shape = jax.ShapeDtypeStruct((groups, T, width), BF16)
    return _call(body, name=name, grid=(T // tm,),
                 in_specs=[row, pl.BlockSpec((1, D_MODEL), lambda i: (0, 0)), _resident(w.shape)],
                 out_specs=[out_spec, row], out_shape=[out_shape, jax.ShapeDtypeStruct((T, D_MODEL), BF16)],
                 args=(h, g, w), ride=ride)


def _proj_bwd(dy, w, x, g, dres, name, tm, ride=None):
    T = x.shape[0]
    groups = dy.shape[0] if dy.ndim == 3 else 1
    width = w.shape[1] // groups

    def body(dy_ref, w_ref, x_ref, g_ref, dres_ref, dx_ref, dxb_ref, dg_ref):
        @pl.when(pl.program_id(0) == 0)
        def _():
            dg_ref[...] = jnp.zeros_like(dg_ref)

        if groups == 1:
            dn = _dot_nt(dy_ref[...], w_ref[...])
        else:
            dn = sum(_dot_nt(dy_ref[p], w_ref[:, p * width:(p + 1) * width]) for p in range(groups))
        dx_ref[...] = dn
        _rms_bwd_epilogue(x_ref, g_ref, dres_ref, dx_ref, dxb_ref, dg_ref)

    row = pl.BlockSpec((tm, D_MODEL), lambda i: (i, 0))
    vec = pl.BlockSpec((1, D_MODEL), lambda i: (0, 0))
    dy_spec = (pl.BlockSpec((tm, w.shape[1]), lambda i: (i, 0)) if groups == 1
               else pl.BlockSpec((groups, tm, width), lambda i: (0, i, 0)))
    return _call(body, name=name, grid=(T // tm,),
                 in_specs=[dy_spec, _resident(w.shape), row, vec, row], out_specs=[row, row, vec],
                 out_shape=[jax.ShapeDtypeStruct((T, D_MODEL), F32), jax.ShapeDtypeStruct((T, D_MODEL), BF16),
                            jax.ShapeDtypeStruct((1, D_MODEL), F32)],
                 args=(dy, w, x, g, dres), ride=ride)


def _matmul_res(a, w, res, name, tm=1024):
    T, K = a.shape

    def body(a_ref, w_ref, r_ref, o_ref):
        o_ref[...] = r_ref[...] + _dot(a_ref[...], w_ref[...])

    row = pl.BlockSpec((tm, D_MODEL), lambda i: (i, 0))
    return _call(body, name=name, grid=(T // tm,),
                 in_specs=[pl.BlockSpec((tm, K), lambda i: (i, 0)), pl.BlockSpec((K, D_MODEL), lambda i: (0, 0)), row],
                 out_specs=[row], out_shape=[jax.ShapeDtypeStruct((T, D_MODEL), F32)], args=(a, w, res))[0]


def _matmul_nt(dy, w, name, tm=1024):
    T, N = dy.shape
    K = w.shape[0]

    def body(dy_ref, w_ref, o_ref):
        o_ref[...] = _dot_nt(dy_ref[...], w_ref[...]).astype(BF16)

    return _call(body, name=name, grid=(T // tm,),
                 in_specs=[pl.BlockSpec((tm, N), lambda i: (i, 0)), pl.BlockSpec((K, N), lambda i: (0, 0))],
                 out_specs=[pl.BlockSpec((tm, K), lambda i: (i, 0))], out_shape=[jax.ShapeDtypeStruct((T, K), BF16)],
                 args=(dy, w))[0]


def _wgrad(a, a_spec, b, b_specs, m, n, bm, bn, n_k, name, square_a=False, ride=None):
    b_specs = b_specs if isinstance(b_specs, (list, tuple)) else [b_specs]
    n_b = len(b_specs)

    def body(a_ref, *rest):
        b_refs = rest[:n_b]
        o_ref, acc = rest[n_b:]
        k = pl.program_id(2)

        @pl.when(k == 0)
        def _():
            acc[...] = jnp.zeros_like(acc)

        av = a_ref[...]
        if square_a:
            af = av.astype(F32)
            av = (af * af).astype(BF16)
        bv = b_refs[0][...] if n_b == 1 else jnp.concatenate([r[...] for r in b_refs], axis=1)
        acc[...] += _dot_tn(av, bv)

        @pl.when(k == n_k - 1)
        def _():
            o_ref[...] = acc[...].astype(BF16)

    res = _call(body, name=name, grid=(m // bm, n // bn, n_k), in_specs=[a_spec, *b_specs],
                out_specs=[pl.BlockSpec((bm, bn), lambda i, j, k: (i, j))], out_shape=[jax.ShapeDtypeStruct((m, n), BF16)],
                scratch_shapes=[pltpu.VMEM((bm, bn), F32)], args=(a, *([b] * n_b)), ride=ride)
    return res[0] if ride is None else (res[0][0], res[1])


def _mlp_fwd(h, g, w1, w2, name, tm=1024, tf=1024, ride=None):
    T = h.shape[0]

    def body(h_ref, g_ref, w1_ref, w2_ref, ho_ref, n_ref, u_ref):
        @pl.when(pl.program_id(1) == 0)
        def _():
            n_ref[...] = _rms(h_ref[...], g_ref[...]).astype(BF16)
            ho_ref[...] = h_ref[...]

        u = jnp.maximum(_dot(n_ref[...], w1_ref[...]), 0.0)
        u_ref[...] = u.astype(BF16)
        ho_ref[...] += _dot((u * u).astype(BF16), w2_ref[...])

    row = pl.BlockSpec((tm, D_MODEL), lambda i, j: (i, 0))
    return _call(body, name=name, grid=(T // tm, D_FF // tf),
                 in_specs=[row, pl.BlockSpec((1, D_MODEL), lambda i, j: (0, 0)),
                           pl.BlockSpec((D_MODEL, tf), lambda i, j: (0, j)), pl.BlockSpec((tf, D_MODEL), lambda i, j: (j, 0))],
                 out_specs=[row, row, pl.BlockSpec((tm, tf), lambda i, j: (i, j))],
                 out_shape=[jax.ShapeDtypeStruct((T, D_MODEL), F32), jax.ShapeDtypeStruct((T, D_MODEL), BF16),
                            jax.ShapeDtypeStruct((T, D_FF), BF16)],
                 args=(h, g, w1, w2), ride=ride)


def _mlp_bwd(dh, u, w1, w2, h, g, name, tm=1024, tf=1024, ride=None):
    T = h.shape[0]
    n_j = D_FF // tf

    def body(dh_ref, u_ref, w1_ref, w2_ref, h_ref, g_ref, dx_ref, dxb_ref, du_ref, dg_ref, dhb):
        acc = dx_ref
        i, j = pl.program_id(0), pl.program_id(1)

        @pl.when(j == 0)
        def _():
            dhb[...] = dh_ref[...].astype(BF16)
            acc[...] = jnp.zeros_like(acc)

        @pl.when((i == 0) & (j == 0))
        def _():
            dg_ref[...] = jnp.zeros_like(dg_ref)

        da = _dot_nt(dhb[...], w2_ref[...])
        du = (da * (2.0 * u_ref[...].astype(F32))).astype(BF16)
        du_ref[...] = du
        acc[...] += _dot_nt(du, w1_ref[...])

        @pl.when(j == n_j - 1)
        def _():
            _rms_bwd_epilogue(h_ref, g_ref, dh_ref, dx_ref, dxb_ref, dg_ref)

    row = pl.BlockSpec((tm, D_MODEL), lambda i, j: (i, 0))
    vec = pl.BlockSpec((1, D_MODEL), lambda i, j: (0, 0))
    hid = pl.BlockSpec((tm, tf), lambda i, j: (i, j))
    return _call(body, name=name, grid=(T // tm, n_j),
                 in_specs=[row, hid, pl.BlockSpec((D_MODEL, tf), lambda i, j: (0, j)),
                           pl.BlockSpec((tf, D_MODEL), lambda i, j: (j, 0)), row, vec],
                 out_specs=[row, row, hid, vec],
                 out_shape=[jax.ShapeDtypeStruct((T, D_MODEL), F32), jax.ShapeDtypeStruct((T, D_MODEL), BF16),
                            jax.ShapeDtypeStruct((T, D_FF), BF16), jax.ShapeDtypeStruct((1, D_MODEL), F32)],
                 scratch_shapes=[pltpu.VMEM((tm, D_MODEL), BF16)], args=(dh, u, w1, w2, h, g), ride=ride)


def _final_loss(h, g, target, name, tm=512):
    T = h.shape[0]

    def body(h_ref, g_ref, t_ref, dh_ref, dhb_ref, loss_ref, dg_ref):
        @pl.when(pl.program_id(0) == 0)
        def _():
            loss_ref[...] = jnp.zeros_like(loss_ref)
            dg_ref[...] = jnp.zeros_like(dg_ref)

        x = h_ref[...]
        gg = g_ref[...]
        r = lax.rsqrt(jnp.mean(x * x, axis=-1, keepdims=True) + EPS)
        xh = x * r
        e = xh * gg - t_ref[...]
        per_tok = jnp.mean(e * e, axis=-1, keepdims=True)
        loss_ref[...] += 0.5 * jnp.sum(per_tok, axis=0, keepdims=True)
        dy = e * (1.0 / D_MODEL)
        dg_ref[...] += jnp.sum(dy * xh, axis=0, keepdims=True)
        dxh = dy * gg
        dx = r * (dxh - xh * jnp.mean(dxh * xh, axis=-1, keepdims=True))
        dh_ref[...] = dx
        dhb_ref[...] = dx.astype(BF16)

    row = pl.BlockSpec((tm, D_MODEL), lambda i: (i, 0))
    vec = pl.BlockSpec((1, D_MODEL), lambda i: (0, 0))
    return _call(body, name=name, grid=(T // tm,), in_specs=[row, vec, row],
                 out_specs=[row, row, pl.BlockSpec((8, LANES), lambda i: (0, 0)), vec],
                 out_shape=[jax.ShapeDtypeStruct((T, D_MODEL), F32), jax.ShapeDtypeStruct((T, D_MODEL), BF16),
                            jax.ShapeDtypeStruct((8, LANES), F32), jax.ShapeDtypeStruct((1, D_MODEL), F32)],
                 args=(h, g, target))


def _ret_constants(S):
    log_gamma = jnp.log1p(-jnp.exp2(-5.0 - jnp.arange(RET_HEADS, dtype=F32)))
    idx = jnp.arange(RET_SC, dtype=F32)
    i, j = idx[:, None], idx[None, :]
    same_chunk = jnp.floor(i / CHUNK) == jnp.floor(j / CHUNK)
    mask = jnp.where((j <= i) | same_chunk, jnp.exp(log_gamma[:, None, None] * jnp.abs(i - j)[None]), 0.0)
    qdec = jnp.exp(log_gamma[:, None] * (idx + 1.0)[None, :])[:, :, None]
    kdec = jnp.exp(log_gamma[:, None] * (RET_SC - 1 - idx)[None, :])[:, :, None]
    cdec = jnp.exp(log_gamma * RET_SC)[:, None, None]
    half = RET_DK // 2
    inv = jnp.exp(-jnp.log(ROPE_BASE) * jnp.arange(half, dtype=F32) / half)
    ang = jnp.arange(S, dtype=F32)[:, None] * inv[None, :]
    return jnp.cos(ang), jnp.sin(ang), mask.astype(F32), qdec, kdec, cdec


def _rope(t, cs, sn):
    t1, t2 = t[:, :RET_DK // 2], t[:, RET_DK // 2:]
    return jnp.concatenate([t1 * cs - t2 * sn, t1 * sn + t2 * cs], axis=-1)


def _rope_bwd(d, cs, sn):
    d1, d2 = d[:, :RET_DK // 2], d[:, RET_DK // 2:]
    return jnp.concatenate([d1 * cs + d2 * sn, d2 * cs - d1 * sn], axis=-1)


def _ret_specs(B, S, reverse):
    n_sc = S // RET_SC

    def cc(c):
        return n_sc - 1 - c if reverse else c

    return dict(
        proj=pl.BlockSpec((B, RET_SC, RET_HEAD_COLS), lambda h, c: (0, cc(c), h)),
        trig=pl.BlockSpec((RET_SC, RET_DK // 2), lambda h, c: (cc(c), 0)),
        mask=pl.BlockSpec((None, RET_SC, RET_SC), lambda h, c: (h, 0, 0)),
        dec=pl.BlockSpec((None, RET_SC, 1), lambda h, c: (h, 0, 0)),
        cdec=pl.BlockSpec((None, 1, 1), lambda h, c: (h, 0, 0)),
        gn=pl.BlockSpec((1, RET_DV), lambda h, c: (0, h)),
        val=pl.BlockSpec((B, RET_SC, RET_DV), lambda h, c: (0, cc(c), h)),
        state=pl.BlockSpec((B, None, None, RET_DK, RET_DV), lambda h, c: (0, h, cc(c), 0, 0)),
    )


def _ret_qkvg(p_ref, cs, sn):
    q = _rope(p_ref[:, 0:RET_DK].astype(F32), cs, sn)
    k = _rope(p_ref[:, RET_DK:2 * RET_DK].astype(F32), cs, sn) * (RET_DK ** -0.5)
    v = p_ref[:, 2 * RET_DK:2 * RET_DK + RET_DV]
    gate = p_ref[:, 2 * RET_DK + RET_DV:RET_HEAD_COLS].astype(F32)
    return q, k, v, gate


def _group_norm(o):
    mu = jnp.mean(o, axis=-1, keepdims=True)
    oc = o - mu
    rstd = lax.rsqrt(jnp.mean(oc * oc, axis=-1, keepdims=True) + EPS)
    return oc * rstd, rstd


def _ret_fwd(proj, consts, gn, B, S, name, ride=None):
    T = B * S
    n_sc = S // RET_SC
    sp = _ret_specs(B, S, False)

    def body(p_ref, cos_ref, sin_ref, m_ref, qd_ref, kd_ref, cd_ref, gn_ref, y_ref, o_ref, st_ref, state):
        @pl.when(pl.program_id(1) == 0)
        def _():
            state[...] = jnp.zeros_like(state)

        seqs = range(B)
        qkvg = [_ret_qkvg(p_ref.at[b], cos_ref[...], sin_ref[...]) for b in seqs]
        qb = [qkvg[b][0].astype(BF16) for b in seqs]
        kb = [qkvg[b][1].astype(BF16) for b in seqs]
        kt = [(qkvg[b][1] * kd_ref[...]).astype(BF16) for b in seqs]
        v = [qkvg[b][2] for b in seqs]
        sc = [_dot_nt(qb[b], kb[b]) for b in seqs]
        sb = [state[b].astype(BF16) for b in seqs]
        cross = [_dot(qb[b], sb[b]) for b in seqs]
        for b in seqs:
            st_ref[b] = sb[b]
        p = [(sc[b] * m_ref[...]).astype(BF16) for b in seqs]
        o = [_dot(p[b], v[b]) + cross[b] * qd_ref[...] for b in seqs]
        upd = [_dot_tn(kt[b], v[b]) for b in seqs]
        for b in seqs:
            o_ref[b] = o[b].astype(BF16)
            ohat, _ = _group_norm(o[b])
            gate = qkvg[b][3]
            y_ref[b] = (gate * jax.nn.sigmoid(gate) * (ohat * gn_ref[...])).astype(BF16)
            state[b] = state[b] * cd_ref[...] + upd[b]

    res = _call(
        body, name=name, grid=(RET_HEADS, n_sc),
        in_specs=[sp["proj"], sp["trig"], sp["trig"], sp["mask"], sp["dec"], sp["dec"], sp["cdec"], sp["gn"]],
        out_specs=[sp["val"], sp["val"], sp["state"]],
        out_shape=[jax.ShapeDtypeStruct((B, S, RET_VW), BF16), jax.ShapeDtypeStruct((B, S, RET_VW), BF16),
                   jax.ShapeDtypeStruct((B, RET_HEADS, n_sc, RET_DK, RET_DV), BF16)],
        scratch_shapes=[pltpu.VMEM((B, RET_DK, RET_DV), F32)],
        args=(proj.reshape(B, S, -1), *consts, gn), ride=ride)
    (y, o, states), lands = res if ride is not None else (res, None)
    outs = [y.reshape(T, RET_VW), o.reshape(T, RET_VW), states]
    return outs if ride is None else (outs, lands)


def _ret_bwd(proj, consts, gn, o, states, dy, B, S, name, ride=None):
    T = B * S
    n_sc = S // RET_SC
    sp = _ret_specs(B, S, True)

    def body(p_ref, cos_ref, sin_ref, m_ref, qd_ref, kd_ref, cd_ref, gn_ref, o_ref, st_ref, dy_ref,
             dp_ref, dgn_ref, dstate):
        @pl.when(pl.program_id(1) == 0)
        def _():
            dstate[...] = jnp.zeros_like(dstate)
            dgn_ref[...] = jnp.zeros_like(dgn_ref)

        seqs = range(B)
        cs, sn = cos_ref[...], sin_ref[...]
        m, gnv = m_ref[...], gn_ref[...]
        qkvg = [_ret_qkvg(p_ref.at[b], cs, sn) for b in seqs]
        qb = [qkvg[b][0].astype(BF16) for b in seqs]
        kb = [qkvg[b][1].astype(BF16) for b in seqs]
        kt = [(qkvg[b][1] * kd_ref[...]).astype(BF16) for b in seqs]
        v = [qkvg[b][2] for b in seqs]
        sc = [_dot_nt(qb[b], kb[b]) for b in seqs]
        dsb = [dstate[b].astype(BF16) for b in seqs]
        dv_state = [_dot(kt[b], dsb[b]) for b in seqs]
        dkt = [_dot_nt(v[b], dsb[b]) for b in seqs]
        do, dgate = [], []
        for b in seqs:
            gate = qkvg[b][3]
            ohat, rstd = _group_norm(o_ref[b].astype(F32))
            dyv = dy_ref[b].astype(F32)
            sg = jax.nn.sigmoid(gate)
            don = dyv * (gate * sg)
            dgate.append(dyv * (ohat * gnv) * (sg * (1.0 + gate * (1.0 - sg))))
            dgn_ref[...] += jnp.sum(don * ohat, axis=0, keepdims=True)
            dohat = don * gnv
            do.append(rstd * (dohat - jnp.mean(dohat, axis=-1, keepdims=True)
                              - ohat * jnp.mean(dohat * ohat, axis=-1, keepdims=True)))
        dob = [do[b].astype(BF16) for b in seqs]
        doq = [(do[b] * qd_ref[...]).astype(BF16) for b in seqs]
        dsc_f = [_dot_nt(dob[b], v[b]) for b in seqs]
        dq_state = [_dot_nt(doq[b], st_ref[b]) for b in seqs]
        dstate_upd = [_dot_tn(qb[b], doq[b]) for b in seqs]
        p = [(sc[b] * m).astype(BF16) for b in seqs]
        dsc = [(dsc_f[b] * m).astype(BF16) for b in seqs]
        dv = [_dot_tn(p[b], dob[b]) + dv_state[b] for b in seqs]
        dq = [_dot(dsc[b], kb[b]) + dq_state[b] for b in seqs]
        dk = [(_dot_tn(dsc[b], qb[b]) + dkt[b] * kd_ref[...]) * (RET_DK ** -0.5) for b in seqs]
        for b in seqs:
            dstate[b] = dstate[b] * cd_ref[...] + dstate_upd[b]
            dp_ref[b, :, 0:RET_DK] = _rope_bwd(dq[b], cs, sn).astype(BF16)
            dp_ref[b, :, RET_DK:2 * RET_DK] = _rope_bwd(dk[b], cs, sn).astype(BF16)
            dp_ref[b, :, 2 * RET_DK:2 * RET_DK + RET_DV] = dv[b].astype(BF16)
            dp_ref[b, :, 2 * RET_DK + RET_DV:RET_HEAD_COLS] = dgate[b].astype(BF16)

    res = _call(
        body, name=name, grid=(RET_HEADS, n_sc),
        in_specs=[sp["proj"], sp["trig"], sp["trig"], sp["mask"], sp["dec"], sp["dec"], sp["cdec"], sp["gn"],
                  sp["val"], sp["state"], sp["val"]],
        out_specs=[sp["proj"], sp["gn"]],
        out_shape=[jax.ShapeDtypeStruct((B, S, RET_HEADS * RET_HEAD_COLS), BF16), jax.ShapeDtypeStruct((1, RET_VW), F32)],
        scratch_shapes=[pltpu.VMEM((B, RET_DK, RET_DV), F32)],
        args=(proj.reshape(B, S, -1), *consts, gn, o.reshape(B, S, -1), states, dy.reshape(B, S, -1)), ride=ride)
    (dproj, dgn), lands = res if ride is not None else (res, None)
    outs = [dproj.reshape(T, -1), dgn]
    return outs if ride is None else (outs, lands)


def _ring_index():
    u = np.arange(ATT_RING)
    offset = np.where(u < ATT_KW, u, u - ATT_RING)
    return np.clip(offset - BAND_PAST, -MAX_REL, CHUNK - 1) + MAX_REL


def _bias_ring(rel):
    n_clip = BAND_PAST - MAX_REL
    n_hi = ATT_KW - n_clip - REL_TABLE
    ring = jnp.concatenate([jnp.broadcast_to(rel[:, :1], (ATT_HEADS, n_clip)), rel,
                            jnp.broadcast_to(rel[:, -1:], (ATT_HEADS, n_hi)),
                            jnp.broadcast_to(rel[:, :1], (ATT_HEADS, ATT_CQ))], axis=1)
    return jnp.broadcast_to(ring.reshape(ATT_PAIRS, 2, 1, ATT_RING), (ATT_PAIRS, 2, 8, ATT_RING))


def _band_masks():
    i = np.arange(ATT_CQ)[:, None]
    j = np.arange(ATT_KW)[None, :]
    lo = CHUNK * (i // CHUNK)
    band = np.where((j >= lo) & (j < lo + BAND_PAST + CHUNK), 0.0, NEG)
    return np.stack([band + np.where(j + v * ATT_CQ >= BAND_PAST, 0.0, NEG)
                     for v in range(ATT_VARIANTS)]).astype(np.float32)


def _attn_bias(bias_scr, ring_ref, band_ref):
    for hh in range(2):
        rows = jnp.broadcast_to(ring_ref[hh, 0:1, :], (ATT_CQ, ATT_RING))
        toeplitz = pltpu.roll(rows, 0, 1, stride=1, stride_axis=0)[:, :ATT_KW]
        for v in range(ATT_VARIANTS):
            bias_scr[v, hh] = toeplitz + band_ref[v]


ATT_STRIP = 32
ATT_SCALE = ATT_DH ** -0.5


def _strips(fn):
    def strip(r, carry):
        fn(pl.ds(pl.multiple_of(r * ATT_STRIP, ATT_STRIP), ATT_STRIP))
        return carry

    lax.fori_loop(0, ATT_CQ // ATT_STRIP, strip, 0, unroll=True)


def _attn_prepare(kpad, vpad, qm, qkv_ref):
    kpad[0:BAND_PAST, :] = jnp.zeros((BAND_PAST, LANES), BF16)
    vpad[0:BAND_PAST, :] = jnp.zeros((BAND_PAST, LANES), BF16)
    kpad[BAND_PAST:, :] = qkv_ref[1]
    vpad[BAND_PAST:, :] = qkv_ref[2]
    lane = lax.broadcasted_iota(jnp.int32, (1, LANES), 1)
    q = qkv_ref[0] * ATT_SCALE
    for hh in range(2):
        qm[hh] = jnp.where((lane >= ATT_DH * hh) & (lane < ATT_DH * (hh + 1)), q, jnp.zeros_like(q))


def _attn_scores(s_ref, qm, kpad, b_ref, t, n_qb):
    t = jnp.minimum(t, n_qb - 1)
    qs = pl.multiple_of(t * ATT_CQ, ATT_CQ)
    variant = jnp.minimum(t, ATT_VARIANTS - 1)
    kw = kpad[pl.ds(qs, ATT_KW), :]
    for hh in range(2):
        s_ref[hh] = _dot_nt(qm[hh, pl.ds(qs, ATT_CQ), :], kw) + b_ref[variant, hh]


def _attn_softmax(s_ref, e_ref, linv_ref, m_ref=None):
    def strip(rows):
        s = s_ref[rows, :]
        m = jnp.max(s, axis=-1, keepdims=True)
        e = jnp.exp(s - m)
        e_ref[rows, :] = e.astype(BF16)
        linv_ref[rows, :] = jnp.broadcast_to(1.0 / jnp.sum(e, axis=-1, keepdims=True), (ATT_STRIP, LANES))
        if m_ref is not None:
            m_ref[rows, :] = jnp.broadcast_to(m, (ATT_STRIP, LANES))

    _strips(strip)


_RING_SPEC = pl.BlockSpec((None, 2, 8, ATT_RING), lambda hp, b: (hp, 0, 0, 0))
_BAND_SPEC = pl.BlockSpec((ATT_VARIANTS, ATT_CQ, ATT_KW), lambda hp, b: (0, 0, 0))


def _attn_fwd(qkv3, ring, B, S, name, ride=None):
    T = B * S
    n_qb = S // ATT_CQ

    def body(qkv_ref, ring_ref, band_ref, o_ref, kpad, vpad, qm, b_ref, s_scr, e_scr, linv_scr):
        @pl.when(pl.program_id(1) == 0)
        def _():
            _attn_bias(b_ref, ring_ref, band_ref)

        _attn_prepare(kpad, vpad, qm, qkv_ref)
        e_scr[...] = jnp.zeros_like(e_scr)
        linv_scr[...] = jnp.zeros_like(linv_scr)
        lane = lax.broadcasted_iota(jnp.int32, (1, LANES), 1)

        def softmax(slot):
            for hh in range(2):
                _attn_softmax(s_scr.at[slot, hh], e_scr.at[slot, hh], linv_scr.at[slot, hh])

        def output(t, slot):
            qs = pl.multiple_of(jnp.maximum(t, 0) * ATT_CQ, ATT_CQ)
            vw = vpad[pl.ds(qs, ATT_KW), :]
            outs = [_dot(e_scr[slot, hh], vw) * linv_scr[slot, hh] for hh in range(2)]
            o_ref[pl.ds(qs, ATT_CQ), :] = jnp.where(lane < ATT_DH, outs[0], outs[1]).astype(BF16)

        def pair(u, carry):
            t = 2 * u
            _attn_scores(s_scr.at[1], qm, kpad, b_ref, t + 1, n_qb)
            softmax(0)
            output(t - 1, 1)
            _attn_scores(s_scr.at[0], qm, kpad, b_ref, t + 2, n_qb)
            softmax(1)
            output(t, 0)
            return carry

        _attn_scores(s_scr.at[0], qm, kpad, b_ref, 0, n_qb)
        lax.fori_loop(0, n_qb // 2, pair, 0)
        output(n_qb - 1, 1)

    return _call(body, name=name, grid=(ATT_PAIRS, B),
                 in_specs=[pl.BlockSpec((3, S, LANES), lambda hp, b: (0, b, hp)), _RING_SPEC, _BAND_SPEC],
                 out_specs=[pl.BlockSpec((S, LANES), lambda hp, b: (b, hp))],
                 out_shape=[jax.ShapeDtypeStruct((T, D_MODEL), BF16)],
                 scratch_shapes=[pltpu.VMEM((S + BAND_PAST, LANES), BF16), pltpu.VMEM((S + BAND_PAST, LANES), BF16),
                                 pltpu.VMEM((2, S, LANES), BF16), pltpu.VMEM((ATT_VARIANTS, 2, ATT_CQ, ATT_KW), F32),
                                 pltpu.VMEM((2, 2, ATT_CQ, ATT_KW), F32),
                                 pltpu.VMEM((2, 2, ATT_CQ, ATT_KW), BF16), pltpu.VMEM((2, 2, ATT_CQ, LANES), F32)],
                 args=(qkv3, ring, jnp.asarray(_band_masks())), ride=ride)


def _attn_bwd(qkv3, ring, do, B, S, name, ride=None):
    T = B * S
    n_qb = S // ATT_CQ

    def body(qkv_ref, ring_ref, band_ref, do_ref, dqkv_ref, dring_ref, kpad, vpad, qm, dkacc, dvacc, b_ref, db_ref):
        @pl.when(pl.program_id(1) == 0)
        def _():
            _attn_bias(b_ref, ring_ref, band_ref)
            db_ref[...] = jnp.zeros_like(db_ref)

        _attn_prepare(kpad, vpad, qm, qkv_ref)
        dkacc[...] = jnp.zeros_like(dkacc)
        dvacc[...] = jnp.zeros_like(dvacc)
        lane = lax.broadcasted_iota(jnp.int32, (1, LANES), 1)

        def step(qb, carry):
            qs = pl.multiple_of(qb * ATT_CQ, ATT_CQ)
            variant = jnp.minimum(qb, ATT_VARIANTS - 1)
            dov = do_ref[pl.ds(qs, ATT_CQ), :].astype(F32)
            kw = kpad[pl.ds(qs, ATT_KW), :]
            vw = vpad[pl.ds(qs, ATT_KW), :]
            heads = (0, 1)
            qmh = [qm[hh, pl.ds(qs, ATT_CQ), :] for hh in heads]
            s = [_dot_nt(qmh[hh], kw) + b_ref[variant, hh] for hh in heads]
            e, linv, dom, dp = [None, None], [None, None], [None, None], [None, None]
            for hh in heads:
                e[hh] = jnp.exp(s[hh] - jnp.max(s[hh], axis=-1, keepdims=True))
                linv[hh] = 1.0 / jnp.sum(e[hh], axis=-1, keepdims=True)
                sel = (lane >= ATT_DH * hh) & (lane < ATT_DH * (hh + 1))
                dom[hh] = jnp.where(sel, dov * linv[hh], 0.0).astype(BF16)
                dp[hh] = _dot_nt(dom[hh], vw)
            dqs, dk, dv = [], None, None
            for hh in heads:
                ds = e[hh] * (dp[hh] - jnp.sum(dp[hh] * e[hh], axis=-1, keepdims=True) * linv[hh])
                db_ref[hh] += ds
                dsb = ds.astype(BF16)
                dqs.append(_dot(dsb, kw) * ATT_SCALE)
                dk_h = _dot_tn(qmh[hh], dsb)
                dv_h = _dot_tn(dom[hh], e[hh].astype(BF16))
                dk = dk_h if dk is None else dk + dk_h
                dv = dv_h if dv is None else dv + dv_h
            dqkv_ref[0, pl.ds(qs, ATT_CQ), :] = jnp.where(lane < ATT_DH, dqs[0], dqs[1]).astype(BF16)
            dkacc[:, pl.ds(qs, ATT_KW)] += dk
            dvacc[:, pl.ds(qs, ATT_KW)] += dv
            return carry

        lax.fori_loop(0, n_qb, step, 0, unroll=4)
        dqkv_ref[1] = dkacc[:, BAND_PAST:].T.astype(BF16)
        dqkv_ref[2] = dvacc[:, BAND_PAST:].T.astype(BF16)

        @pl.when(pl.program_id(1) == B - 1)
        def _():
            r = lax.broadcasted_iota(jnp.int32, (ATT_CQ, ATT_CQ), 0)
            c = lax.broadcasted_iota(jnp.int32, (ATT_CQ, ATT_CQ), 1)
            reverse = jnp.where(r + c == ATT_CQ - 1, 1.0, 0.0).astype(BF16)
            for hh in range(2):
                x = jnp.concatenate([db_ref[hh], jnp.zeros((ATT_CQ, ATT_RING - ATT_KW), F32)], axis=1)
                flipped = jnp.zeros((ATT_CQ, ATT_RING), F32)
                for _ in range(3):
                    part = x.astype(BF16)
                    flipped = flipped + _dot(reverse, part)
                    x = x - part.astype(F32)
                aligned = pltpu.roll(flipped, ATT_KW + 1, 1, stride=1, stride_axis=0)
                dring_ref[hh] = jnp.sum(aligned.reshape(ATT_CQ // 8, 8, ATT_RING), axis=0)

    qkv_spec = pl.BlockSpec((3, S, LANES), lambda hp, b: (0, b, hp))
    return _call(body, name=name, grid=(ATT_PAIRS, B),
                 in_specs=[qkv_spec, _RING_SPEC, _BAND_SPEC, pl.BlockSpec((S, LANES), lambda hp, b: (b, hp))],
                 out_specs=[qkv_spec, _RING_SPEC],
                 out_shape=[jax.ShapeDtypeStruct((3, T, D_MODEL), BF16),
                            jax.ShapeDtypeStruct((ATT_PAIRS, 2, 8, ATT_RING), F32)],
                 scratch_shapes=[pltpu.VMEM((S + BAND_PAST, LANES), BF16), pltpu.VMEM((S + BAND_PAST, LANES), BF16),
                                 pltpu.VMEM((2, S, LANES), BF16),
                                 pltpu.VMEM((LANES, S + BAND_PAST), F32), pltpu.VMEM((LANES, S + BAND_PAST), F32),
                                 pltpu.VMEM((ATT_VARIANTS, 2, ATT_CQ, ATT_KW), F32), pltpu.VMEM((2, ATT_CQ, ATT_KW), F32)],
                 args=(qkv3, ring, jnp.asarray(_band_masks()), do), ride=ride)


def _rel_bias_grad(dring, name):
    fold = np.zeros((ATT_RING, REL_PAD), np.float32)
    fold[np.arange(ATT_RING), _ring_index()] = 1.0
    fold = jnp.asarray(fold, BF16)

    def body(d_ref, f_ref, o_ref):
        x = jnp.sum(d_ref[...], axis=0, keepdims=True)
        x = jnp.broadcast_to(x, (8, ATT_RING))
        acc = jnp.zeros((8, REL_PAD), F32)
        for _ in range(3):
            part = x.astype(BF16)
            acc = acc + _dot(part, f_ref[...])
            x = x - part.astype(F32)
        o_ref[...] = acc[0:1, :]

    out = _call(body, name=name, grid=(ATT_HEADS,),
                in_specs=[pl.BlockSpec((None, 8, ATT_RING), lambda h: (h, 0, 0)),
                          pl.BlockSpec((ATT_RING, REL_PAD), lambda h: (0, 0))],
                out_specs=[pl.BlockSpec((None, 1, REL_PAD), lambda h: (h, 0, 0))],
                out_shape=[jax.ShapeDtypeStruct((ATT_HEADS, 1, REL_PAD), F32)],
                args=(dring.reshape(ATT_HEADS, 8, ATT_RING), fold))[0]
    return out.reshape(ATT_HEADS, REL_PAD)


def _adamw(w, g, m, v):
    m = ADAM_B1 * m + (1.0 - ADAM_B1) * g
    v = ADAM_B2 * v + (1.0 - ADAM_B2) * (g * g)
    m_hat = m / (1.0 - ADAM_B1 ** ADAM_STEP)
    v_hat = v / (1.0 - ADAM_B2 ** ADAM_STEP)
    delta = -ADAM_LR * (m_hat / (jnp.sqrt(v_hat) + ADAM_EPS) + ADAM_WD * w)
    return delta, m, v


def _sum_devices(ref):
    g = ref[0].astype(F32)
    for d in range(1, N_DEV):
        g = g + ref[d].astype(F32)
    return g


def _adamw_reduce(lands, w, m, v, name, tr=256):
    L, R, C = w.shape
    tr = min(tr, R)
    n_i = R // tr

    def body(*refs):
        l_refs = refs[:L]
        w_ref, m_ref, v_ref, g_out, d_out, m_out, v_out = refs[L:]
        layer = pl.program_id(0)
        for l in range(L):
            @pl.when(layer == l)
            def _(l=l):
                g = _sum_devices(l_refs[l])
                g_out[...] = g
                d_out[...], m_out[...], v_out[...] = _adamw(w_ref[...], g, m_ref[...], v_ref[...])

    def land_spec(l):
        return pl.BlockSpec((N_DEV, tr, C), lambda ly, i: (0, jnp.where(ly == l, i, jnp.where(ly < l, 0, n_i - 1)), 0))

    blk = pl.BlockSpec((None, tr, C), lambda ly, i: (ly, i, 0))
    return _call(body, name=name, grid=(L, n_i), in_specs=[land_spec(l) for l in range(L)] + [blk, blk, blk],
                 out_specs=[blk] * 4, out_shape=[jax.ShapeDtypeStruct((L, R, C), F32)] * 4, args=(*lands, w, m, v))


def _small_update(land_small, land_rel, w, m, v, name):
    def body(ls_ref, lr_ref, w_ref, m_ref, v_ref, g_out, d_out, m_out, v_out, rel_out):
        g = _sum_devices(ls_ref)
        g_out[...] = g
        d_out[...], m_out[...], v_out[...] = _adamw(w_ref[...], g, m_ref[...], v_ref[...])
        rel_out[...] = _sum_devices(lr_ref)

    return pl.pallas_call(
        body, name=name,
        out_shape=[jax.ShapeDtypeStruct(w.shape, F32)] * 4 + [jax.ShapeDtypeStruct(land_rel.shape[1:], F32)],
    )(land_small, land_rel, w, m, v)


def _adamw_plain(w, g, m, v, name):
    def body(w_ref, g_ref, m_ref, v_ref, d_out, m_out, v_out):
        d_out[...], m_out[...], v_out[...] = _adamw(w_ref[...], g_ref[...], m_ref[...], v_ref[...])

    return pl.pallas_call(body, name=name, out_shape=[jax.ShapeDtypeStruct(w.shape, F32)] * 3)(w, g, m, v)


def _ret_piece_of_column_block(p):
    per_head = RET_HEAD_COLS // RET_DK
    qk = jnp.where(p < RET_HEADS, per_head * p, per_head * (p - RET_HEADS) + 1)
    pv = p - 2 * RET_HEADS
    vv = per_head * (pv // 2) + 2 + pv % 2
    pg = p - 4 * RET_HEADS
    gg = per_head * (pg // 2) + 4 + pg % 2
    return jnp.where(p < 2 * RET_HEADS, qk, jnp.where(p < 4 * RET_HEADS, vv, gg))


def _step(x, target, shards, mix_g, gn_g, rel_shard, mlp_g, final_g):
    s_rin, s_rout, s_ain, s_aout, s_w1, s_w2 = shards
    B, S, _ = x.shape
    T = B * S
    tk = min(T, 2048)
    n_k = T // tk
    h0 = x.reshape(T, D_MODEL)
    tgt = target.reshape(T, D_MODEL)
    consts = _ret_constants(S)
    w1_cols = w2_rows = D_FF // N_DEV

    rel_cols = REL_TABLE // N_DEV
    per_shard = RET_IN // N_DEV // RET_DK

    def head_layout(r):
        return lambda ref, blk: ref.at[:, pl.ds(pl.multiple_of(
            _ret_piece_of_column_block(per_shard * blk + r) * RET_DK, LANES), RET_DK)]

    w_rin_heads, rel_all = _ChipGather(
        [s_rin[:, r * RET_DK:(r + 1) * RET_DK] for r in range(per_shard)] + [rel_shard],
        [jax.ShapeDtypeStruct((D_MODEL, RET_IN), BF16), jax.ShapeDtypeStruct((N_DEV, ATT_HEADS, rel_cols), F32)],
        [0] * per_shard + [1], [head_layout(r) for r in range(per_shard)] + [_slot]).run("gather_first")
    ring = _bias_ring(rel_all.transpose(1, 0, 2).reshape(ATT_HEADS, REL_TABLE))

    def gather(*items):
        return _ChipGather([s for s, _, _ in items], [jax.ShapeDtypeStruct(full, BF16) for _, _, full in items],
                           list(range(len(items))), [view for _, view, _ in items])

    def scatter(*items):
        return _Exchange([p for p, _, _ in items], [(a, view, a, _slot) for a, (_, view, _) in enumerate(items)],
                         [jax.ShapeDtypeStruct((N_DEV, *shard), BF16) for _, _, shard in items])

    w1_full, w2_full = (D_MODEL, D_FF), (D_FF, D_MODEL)
    w1_shard, w2_shard = (D_MODEL, w1_cols), (w2_rows, D_MODEL)

    ride = gather((s_rout, _rows(RET_VW // N_DEV), (RET_VW, D_MODEL)), (s_w1[0], _cols(w1_cols), w1_full))
    (proj, n0), (w_rout, w1_0) = _norm_proj(h0, mix_g[0:1], w_rin_heads, 1, "ret_proj", 512, ride=ride)
    (y, o, states), (w2_0,) = _ret_fwd(proj, consts, gn_g, B, S, "ret_fwd", ride=gather((s_w2[0], _rows(w2_rows), w2_full)))
    h1 = _matmul_res(y, w_rout, h0, "ret_out")
    ride = gather((s_ain, _cols(3 * D_MODEL // N_DEV), (D_MODEL, 3 * D_MODEL)),
                  (s_aout, _rows(D_MODEL // N_DEV), (D_MODEL, D_MODEL)))
    (h2, n1, u1), (w_ain, w_aout) = _mlp_fwd(h1, mlp_g[0:1], w1_0, w2_0, "mlp0_fwd", ride=ride)
    qkv3, n2 = _norm_proj(h2, mix_g[1:2], w_ain, 3, "att_proj", 1024)
    ride = gather((s_w1[1], _cols(w1_cols), w1_full), (s_w2[1], _rows(w2_rows), w2_full))
    (o2,), (w1_1, w2_1) = _attn_fwd(qkv3, ring, B, S, "att_fwd", ride=ride)
    h3 = _matmul_res(o2, w_aout, h2, "att_out")
    h4, n3, u3 = _mlp_fwd(h3, mlp_g[1:2], w1_1, w2_1, "mlp1_fwd")
    dh4, dh4b, loss, dg_final = _final_loss(h4, final_g, tgt, "final_loss")

    def tok(width):
        return dict(a=pl.BlockSpec((tk, width), lambda i, j, k: (k, i)), b=pl.BlockSpec((tk, width), lambda i, j, k: (k, j)))

    def mlp_grads(dh, dhb, u, n, w1, w2, h, g, tag, ride=None, w2_rides_on_dw1=False):
        res = _mlp_bwd(dh, u, w1, w2, h, g, tag + "_bwd", ride=ride)
        (dhi, dhib, du, dg), lands = res if ride is not None else (res, None)
        gw2 = _wgrad(u, tok(1024)["a"], dhb, tok(1024)["b"], D_FF, D_MODEL, 1024, 1024, n_k, tag + "_dw2", square_a=True)
        ride_w2 = scatter((gw2, _rows(w2_rows), w2_shard)) if w2_rides_on_dw1 else None
        gw1 = _wgrad(n, tok(1024)["a"], du, tok(1024)["b"], D_MODEL, D_FF, 1024, 1024, n_k, tag + "_dw1", ride=ride_w2)
        return dhi, dhib, gw1, gw2, dg, lands

    dh3, dh3b, gw1_1, gw2_1, dg_mlp1, _ = mlp_grads(dh4, dh4b, u3, n3, w1_1, w2_1, h3, mlp_g[1:2], "mlp1")
    do2 = _matmul_nt(dh3b, w_aout, "att_out_bwd")
    g_aout = _wgrad(o2, tok(1024)["a"], dh3b, tok(1024)["b"], D_MODEL, D_MODEL, 1024, 1024, n_k, "att_out_dw")
    ride = scatter((gw1_1, _cols(w1_cols), w1_shard), (gw2_1, _rows(w2_rows), w2_shard),
                   (g_aout, _rows(D_MODEL // N_DEV), (D_MODEL // N_DEV, D_MODEL)))
    (dqkv3, dring), (l_w1_1, l_w2_1, l_aout) = _attn_bwd(qkv3, ring, do2, B, S, "att_bwd", ride=ride)
    g_rel = _rel_bias_grad(dring, "rel_bias_grad")
    dh2, dh2b, dg_mix1 = _proj_bwd(dqkv3, w_ain, h2, mix_g[1:2], dh3, "att_proj_bwd", 1024)
    g_ain = _wgrad(n2, tok(1024)["a"], dqkv3, pl.BlockSpec((None, tk, D_MODEL), lambda i, j, k: (j, k, 0)),
                   D_MODEL, 3 * D_MODEL, 1024, 1024, n_k, "att_proj_dw")
    ride = scatter((g_ain, _cols(3 * D_MODEL // N_DEV), (D_MODEL, 3 * D_MODEL // N_DEV)))
    dh1, dh1b, (gw1_0, (l_w2_0,)), _, dg_mlp0, (l_ain,) = mlp_grads(
        dh2, dh2b, u1, n1, w1_0, w2_0, h1, mlp_g[0:1], "mlp0", ride=ride, w2_rides_on_dw1=True)
    dy = _matmul_nt(dh1b, w_rout, "ret_out_bwd")
    g_rout = _wgrad(y, tok(1024)["a"], dh1b, tok(1024)["b"], RET_VW, D_MODEL, 1024, 1024, n_k, "ret_out_dw")
    ride = scatter((gw1_0, _cols(w1_cols), w1_shard), (g_rout, _rows(RET_VW // N_DEV), (RET_VW // N_DEV, D_MODEL)))
    (dproj, dgn), (l_w1_0, l_rout) = _ret_bwd(proj, consts, gn_g, o, states, dy, B, S, "ret_bwd", ride=ride)
    g_rin =_wgrad(n0, tok(1024)["a"], dproj,
                   [pl.BlockSpec((tk, RET_DK), functools.partial(
                       lambda i, j, k, r: (k, _ret_piece_of_column_block(per_shard * j + r)), r=r)) for r in range(per_shard)],
                   D_MODEL, RET_IN, 1024, per_shard * RET_DK, n_k, "ret_proj_dw")
    ride = scatter((g_rin, _cols(RET_IN // N_DEV), (D_MODEL, RET_IN // N_DEV)))
    (dx, _, dg_mix0), (l_rin,) = _proj_bwd(dproj, w_rin_heads, h0, mix_g[0:1], dh1, "ret_proj_bwd", 512, ride=ride)

    small = jnp.concatenate([dg_mix0, dg_mix1, dg_mlp0, dg_mlp1, dg_final, dgn.reshape(2, D_MODEL),
                             jnp.zeros((1, D_MODEL), F32)], axis=0)
    l_small, l_rel = _Exchange(
        [small, g_rel], [(0, _whole, 0, _slot), (1, _whole, 1, _slot)],
        [jax.ShapeDtypeStruct((N_DEV, 8, D_MODEL), F32), jax.ShapeDtypeStruct((N_DEV, ATT_HEADS, REL_PAD), F32)],
    ).run("scatter_small")
    lands = dict(rin=l_rin, rout=l_rout, ain=l_ain, aout=l_aout, w1=(l_w1_0, l_w1_1), w2=(l_w2_0, l_w2_1),
                 small=l_small, rel=l_rel)
    return loss[0, 0], dx.reshape(B, S, D_MODEL), lands


def kernel(x, mix_norm_g, ret_w_in, ret_gn_g, ret_w_out, att_w_in, att_rel_bias, att_w_out, mlp_norm_g, mlp_w1, mlp_w2, final_norm_g, loss_target, m_mix_norm_g, m_ret_w_in, m_ret_gn_g, m_ret_w_out, m_att_w_in, m_att_rel_bias, m_att_w_out, m_mlp_norm_g, m_mlp_w1, m_mlp_w2, m_final_norm_g, v_mix_norm_g, v_ret_w_in, v_ret_gn_g, v_ret_w_out, v_att_w_in, v_att_rel_bias, v_att_w_out, v_mlp_norm_g, v_mlp_w1, v_mlp_w2, v_final_norm_g):
    me = _lin(_place())
    rel_cols = REL_TABLE // N_DEV
    shards = (ret_w_in[0].astype(BF16), ret_w_out[0].astype(BF16), att_w_in[0].astype(BF16), att_w_out[0].astype(BF16),
              (mlp_w1[0].astype(BF16), mlp_w1[1].astype(BF16)), (mlp_w2[0].astype(BF16), mlp_w2[1].astype(BF16)))
    loss_part, grad_x, lands = _step(x, loss_target, shards, mix_norm_g, ret_gn_g, att_rel_bias[0], mlp_norm_g,
                                     final_norm_g.reshape(1, D_MODEL))
    loss = lax.psum(loss_part, ("x", "y", "c"))

    u_rin = _adamw_reduce([lands["rin"]], ret_w_in, m_ret_w_in, v_ret_w_in, "update_ret_w_in")
    u_rout = _adamw_reduce([lands["rout"]], ret_w_out, m_ret_w_out, v_ret_w_out, "update_ret_w_out")
    u_ain = _adamw_reduce([lands["ain"]], att_w_in, m_att_w_in, v_att_w_in, "update_att_w_in")
    u_aout = _adamw_reduce([lands["aout"]], att_w_out, m_att_w_out, v_att_w_out, "update_att_w_out")
    u_w1 = _adamw_reduce(lands["w1"], mlp_w1, m_mlp_w1, v_mlp_w1, "update_mlp_w1")
    u_w2 = _adamw_reduce(lands["w2"], mlp_w2, m_mlp_w2, v_mlp_w2, "update_mlp_w2")

    def pack(mix, mlp, fin, gn):
        return jnp.concatenate([mix, mlp, fin.reshape(1, D_MODEL), gn.reshape(2, D_MODEL), jnp.zeros((1, D_MODEL), F32)], axis=0)

    small_w = pack(mix_norm_g, mlp_norm_g, final_norm_g, ret_gn_g)
    small_m = pack(m_mix_norm_g, m_mlp_norm_g, m_final_norm_g, m_ret_gn_g)
    small_v = pack(v_mix_norm_g, v_mlp_norm_g, v_final_norm_g, v_ret_gn_g)
    sg, sd, sm, sv, rel_sum = _small_update(lands["small"], lands["rel"], small_w, small_m, small_v, "update_small")
    g_rel_mine = lax.dynamic_slice(rel_sum, (0, me * rel_cols), (ATT_HEADS, rel_cols))
    rel_d, rel_m, rel_v = _adamw_plain(att_rel_bias[0], g_rel_mine, m_att_rel_bias[0], v_att_rel_bias[0], "update_rel_bias")
    u_rel = [g_rel_mine[None], rel_d[None], rel_m[None], rel_v[None]]

    def unpack(t):
        return dict(mix=t[0:2], mlp=t[2:4], fin=t[4], gn=t[5:7].reshape(1, RET_VW))

    us = [unpack(t) for t in (sg, sd, sm, sv)]
    outs = [loss, grad_x]
    for k in range(4):
        outs += [us[k]["mix"], u_rin[k], us[k]["gn"], u_rout[k], u_ain[k], u_rel[k], u_aout[k], us[k]["mlp"],
                 u_w1[k], u_w2[k], us[k]["fin"]]
    return tuple(outs)
```

```python
import functools

import numpy as np
import jax
import jax.numpy as jnp
from jax import lax
from jax.experimental import pallas as pl
from jax.experimental.pallas import tpu as pltpu

F32, BF16 = jnp.float32, jnp.bfloat16

D_MODEL = 1024
CHUNK = 64
RET_HEADS, RET_DK, RET_DV = 4, 256, 512
RET_QK, RET_VW = RET_HEADS * RET_DK, RET_HEADS * RET_DV
RET_IN = 2 * RET_QK + 2 * RET_VW
RET_HEAD_COLS = 2 * RET_DK + 2 * RET_DV
RET_SC = 256
ROPE_BASE = 10000.0
ATT_HEADS, ATT_DH = 16, 64
LANES = 128
ATT_PAIRS = ATT_HEADS * ATT_DH // LANES
BAND_PAST = 8 * CHUNK
MAX_REL = 256
REL_TABLE = MAX_REL + CHUNK
REL_PAD = 384
ATT_CQ = 256
ATT_KW = ATT_CQ + BAND_PAST
ATT_RING = ATT_CQ + ATT_KW
ATT_VARIANTS = BAND_PAST // ATT_CQ + 1
D_FF = 4 * D_MODEL
EPS = 1e-6
EPILOGUE_ROWS = 256
NEG = -1e30
N_DEV = 8
N_PEERS = N_DEV - 1

ADAM_LR, ADAM_B1, ADAM_B2, ADAM_EPS, ADAM_WD, ADAM_STEP = 0.001, 0.9, 0.999, 1e-08, 0.01, 10

VMEM_LIMIT = 56 * 1024 * 1024
MESH = pl.DeviceIdType.MESH
ANY = pl.BlockSpec(memory_space=pl.ANY)


def _dot(a, b):
    return jnp.dot(a, b, preferred_element_type=F32)


def _dot_nt(a, b):
    return lax.dot_general(a, b, (((1,), (1,)), ((), ())), preferred_element_type=F32)


def _dot_tn(a, b):
    return lax.dot_general(a, b, (((0,), (0,)), ((), ())), preferred_element_type=F32)


def _rms(x, g):
    r = lax.rsqrt(jnp.mean(x * x, axis=-1, keepdims=True) + EPS)
    return x * r * g


def _rms_bwd(dn, x, g):
    r = lax.rsqrt(jnp.mean(x * x, axis=-1, keepdims=True) + EPS)
    xh = x * r
    dg = jnp.sum(dn * xh, axis=0, keepdims=True)
    dxh = dn * g
    dx = r * (dxh - xh * jnp.mean(dxh * xh, axis=-1, keepdims=True))
    return dx, dg


def _rms_bwd_epilogue(x_ref, g_ref, dres_ref, dx_ref, dxb_ref, dg_ref):
    for r in range(0, dx_ref.shape[0], EPILOGUE_ROWS):
        rows = slice(r, r + EPILOGUE_ROWS)
        dx, dg = _rms_bwd(dx_ref[rows, :], x_ref[rows, :], g_ref[...])
        dx = dres_ref[rows, :] + dx
        dx_ref[rows, :] = dx
        dxb_ref[rows, :] = dx.astype(BF16)
        dg_ref[...] += dg


def _place():
    return lax.axis_index("x"), lax.axis_index("y"), lax.axis_index("c")


def _lin(p):
    return 4 * p[0] + 2 * p[1] + p[2]


def _cols(width):
    return lambda ref, blk: ref.at[:, pl.ds(pl.multiple_of(blk * width, LANES), width)]


def _rows(height):
    return lambda ref, blk: ref.at[pl.ds(pl.multiple_of(blk * height, 8), height), :]


def _slot(ref, blk):
    return ref.at[blk]


def _whole(ref, blk):
    return ref


class _Exchange:
    def __init__(self, sources, flows, land_shapes):
        self.sources, self.flows, self.land_shapes = list(sources), flows, list(land_shapes)
        n_f = len(flows)
        self.sem_shapes = [pltpu.SemaphoreType.DMA((N_PEERS * n_f,)), pltpu.SemaphoreType.DMA((N_PEERS * n_f,)),
                           pltpu.SemaphoreType.DMA((n_f,))]

    def _copies(self, srcs, lands, sems):
        send_sems, recv_sems, local_sems = sems
        x, y, c = _place()
        me = (x, y, c)
        peers = [(x ^ dx, y ^ dy, c ^ dc) for dx in (0, 1) for dy in (0, 1) for dc in (0, 1)][1:]

        def copy(f, k, sender, to):
            si, sview, li, lview = self.flows[f]
            return pltpu.make_async_remote_copy(
                src_ref=sview(srcs[si], _lin(to)), dst_ref=lview(lands[li], _lin(sender)),
                send_sem=send_sems.at[f * N_PEERS + k], recv_sem=recv_sems.at[f * N_PEERS + k],
                device_id=to, device_id_type=MESH)

        mine, sends, recvs = [], [], []
        for f, (si, sview, li, lview) in enumerate(self.flows):
            mine.append(pltpu.make_async_copy(sview(srcs[si], _lin(me)), lview(lands[li], _lin(me)), local_sems.at[f]))
            for k, peer in enumerate(peers):
                sends.append(copy(f, k, me, peer))
                recvs.append(copy(f, k, peer, me))
        return mine, sends, recvs

    def start(self, srcs, lands, sems):
        mine, sends, _ = self._copies(srcs, lands, sems)
        for cp in mine + sends:
            cp.start()

    def finish(self, srcs, lands, sems):
        mine, sends, recvs = self._copies(srcs, lands, sems)
        for cp in recvs:
            cp.wait_recv()
        for cp in sends:
            cp.wait_send()
        for cp in mine:
            cp.wait()

    def run(self, name):
        n_src, n_land = len(self.sources), len(self.land_shapes)

        def body(*refs):
            srcs, lands, sems = refs[:n_src], refs[n_src:n_src + n_land], refs[n_src + n_land:]
            self.start(srcs, lands, sems)
            self.finish(srcs, lands, sems)

        return pl.pallas_call(body, name=name, in_specs=[ANY] * n_src, out_specs=[ANY] * n_land,
                              out_shape=self.land_shapes, scratch_shapes=self.sem_shapes)(*self.sources)


def _call(body, *, name, grid, in_specs, out_specs, out_shape, args, scratch_shapes=(), ride=None):
    params = pltpu.CompilerParams(dimension_semantics=("arbitrary",) * len(grid), vmem_limit_bytes=VMEM_LIMIT)
    in_specs, out_specs, out_shape, scratch_shapes = list(in_specs), list(out_specs), list(out_shape), list(scratch_shapes)
    if ride is None:
        return pl.pallas_call(body, name=name, grid=grid, in_specs=in_specs, out_specs=out_specs, out_shape=out_shape,
                              scratch_shapes=scratch_shapes, compiler_params=params)(*args)
    n_in, n_out, n_scr = len(in_specs), len(out_specs), len(scratch_shapes)
    n_src, n_land = len(ride.sources), len(ride.land_shapes)

    def riding(*refs):
        bounds = np.cumsum([n_in, n_src, n_out, n_land, n_scr])
        ins, srcs, outs, lands, scr, sems = (refs[a:b] for a, b in zip([0, *bounds], [*bounds, len(refs)]))
        step = functools.reduce(lambda a, d: a * grid[d] + pl.program_id(d), range(len(grid)), 0)
        n_steps = int(np.prod(grid))

        @pl.when(step == 0)
        def _():
            ride.start(srcs, lands, sems)

        if hasattr(ride, "middle"):
            @pl.when(step == (5 * n_steps) // 8)
            def _():
                ride.middle(srcs, lands, sems)

        body(*ins, *outs, *scr)

        @pl.when(step == n_steps - 1)
        def _():
            ride.finish(srcs, lands, sems)

    res = pl.pallas_call(
        riding, name=name, grid=grid, in_specs=in_specs + [ANY] * n_src, out_specs=out_specs + [ANY] * n_land,
        out_shape=out_shape + ride.land_shapes, scratch_shapes=scratch_shapes + ride.sem_shapes,
        compiler_params=params)(*args, *ride.sources)
    return res[:n_out], res[n_out:]


class _ChipGather:
    def __init__(self, shards, out_shapes, out_of, views):
        self.sources, self.land_shapes, self.out_of, self.views = list(shards), list(out_shapes), out_of, views
        n = len(shards)
        self.sem_shapes = [pltpu.SemaphoreType.DMA((N_PEERS * n,)), pltpu.SemaphoreType.DMA((N_PEERS * n,)),
                           pltpu.SemaphoreType.DMA((n,))]

    def _copies(self, srcs, lands, sems):
        send_sems, recv_sems, local_sems = sems
        n = len(srcs)
        outs = [lands[self.out_of[a]] for a in range(n)]
        x, y, c = _place()
        me, sibling = (x, y, c), (x, y, 1 - c)
        chips = [(1 - x, y), (x, 1 - y), (1 - x, 1 - y)]

        def copy(a, k, block, to, own=False):
            dst = self.views[a](outs[a], _lin(block))
            return pltpu.make_async_remote_copy(
                src_ref=srcs[a] if own else dst, dst_ref=dst,
                send_sem=send_sems.at[a * N_PEERS + k], recv_sem=recv_sems.at[a * N_PEERS + k],
                device_id=to, device_id_type=MESH)

        mine = lambda: [pltpu.make_async_copy(srcs[a], self.views[a](outs[a], _lin(me)), local_sems.at[a]) for a in range(n)]
        first = lambda: [cp for a in range(n) for cp in
                         [copy(a, 0, me, sibling, own=True)] + [copy(a, 1 + j, me, (*chip, c), own=True) for j, chip in enumerate(chips)]]
        landed = lambda a, j: copy(a, 1 + j, (*chips[j], c), me)
        passed = lambda a, j: copy(a, 4 + j, (*chips[j], c), sibling)
        from_sibling = lambda a: [copy(a, 0, sibling, me)] + [copy(a, 4 + j, (*chips[j], 1 - c), me) for j in range(3)]
        return n, mine, first, landed, passed, from_sibling

    def start(self, srcs, lands, sems):
        _, mine, first, _, _, _ = self._copies(srcs, lands, sems)
        for cp in mine() + first():
            cp.start()

    def middle(self, srcs, lands, sems):
        n, _, _, landed, passed, _ = self._copies(srcs, lands, sems)
        for j in range(3):
            for a in range(n):
                landed(a, j).wait_recv()
                passed(a, j).start()

    def finish(self, srcs, lands, sems):
        n, mine, first, _, passed, from_sibling = self._copies(srcs, lands, sems)
        for a in range(n):
            for cp in from_sibling(a):
                cp.wait_recv()
        for cp in first() + [passed(a, j) for j in range(3) for a in range(n)]:
            cp.wait_send()
        for cp in mine():
            cp.wait()

    def run(self, name):
        n, n_out = len(self.sources), len(self.land_shapes)

        def body(*refs):
            parts = refs[:n], refs[n:n + n_out], refs[n + n_out:]
            self.start(*parts)
            self.middle(*parts)
            self.finish(*parts)

        return pl.pallas_call(body, name=name, in_specs=[ANY] * n, out_specs=[ANY] * n_out,
                              out_shape=self.land_shapes, scratch_shapes=self.sem_shapes)(*self.sources)


def _resident(shape):
    return pl.BlockSpec(shape, lambda i: (0,) * len(shape), pipeline_mode=pl.Buffered(1))


def _norm_proj(h, g, w, groups, name, tm, ride=None):
    T = h.shape[0]
    N = w.shape[1]
    width = N // groups

    def body(h_ref, g_ref, w_ref, o_ref, n_ref):
        n = _rms(h_ref[...], g_ref[...]).astype(BF16)
        n_ref[...] = n
        if groups == 1:
            o_ref[...] = _dot(n, w_ref[...]).astype(BF16)
        else:
            for p in range(groups):
                o_ref[p] = _dot(n, w_ref[:, p * width:(p + 1) * width]).astype(BF16)

    row = pl.BlockSpec((tm, D_MODEL), lambda i: (i, 0))
    if groups == 1:
        out_spec, out_shape = pl.BlockSpec((tm, N), lambda i: (i, 0)), jax.ShapeDtypeStruct((T, N), BF16)
    else:
        out_spec = pl.BlockSpec((groups, tm, width), lambda i: (0, i, 0))
        out_shape = jax.ShapeDtypeStruct((groups, T, width), BF16)
    return _call(body, name=name, grid=(T // tm,),
                 in_specs=[row, pl.BlockSpec((1, D_MODEL), lambda i: (0, 0)), _resident(w.shape)],
                 out_specs=[out_spec, row], out_shape=[out_shape, jax.ShapeDtypeStruct((T, D_MODEL), BF16)],
                 args=(h, g, w), ride=ride)


def _ret_proj(h, g, w, cos, sin, S, name, tm, ride=None):
    T = h.shape[0]
    per_seq = S // tm

    def body(h_ref, g_ref, w_ref, cos_ref, sin_ref, o_ref, n_ref):
        n = _rms(h_ref[...], g_ref[...]).astype(BF16)
        n_ref[...] = n
        cs, sn = cos_ref[...], sin_ref[...]
        for hd in range(RET_HEADS):
            c0 = hd * RET_HEAD_COLS
            out = _dot(n, w_ref[:, c0:c0 + RET_HEAD_COLS])
            o_ref[:, c0:c0 + RET_DK] = _rope(out[:, 0:RET_DK], cs, sn).astype(BF16)
            o_ref[:, c0 + RET_DK:c0 + 2 * RET_DK] = (_rope(out[:, RET_DK:2 * RET_DK], cs, sn) * (RET_DK ** -0.5)).astype(BF16)
            o_ref[:, c0 + 2 * RET_DK:c0 + RET_HEAD_COLS] = out[:, 2 * RET_DK:].astype(BF16)

    row = pl.BlockSpec((tm, D_MODEL), lambda i: (i, 0))
    trig = pl.BlockSpec((tm, RET_DK // 2), lambda i: (i % per_seq, 0))
    return _call(body, name=name, grid=(T // tm,),
                 in_specs=[row, pl.BlockSpec((1, D_MODEL), lambda i: (0, 0)), _resident(w.shape), trig, trig],
                 out_specs=[pl.BlockSpec((tm, w.shape[1]), lambda i: (i, 0)), row],
                 out_shape=[jax.ShapeDtypeStruct((T, w.shape[1]), BF16), jax.ShapeDtypeStruct((T, D_MODEL), BF16)],
                 args=(h, g, w, cos, sin), ride=ride)


def _proj_bwd(dy, w, x, g, dres, name, tm, ride=None):
    T = x.shape[0]
    groups = dy.shape[0] if dy.ndim == 3 else 1
    width = w.shape[1] // groups

    def body(dy_ref, w_ref, x_ref, g_ref, dres_ref, dx_ref, dxb_ref, dg_ref):
        @pl.when(pl.program_id(0) == 0)
        def _():
            dg_ref[...] = jnp.zeros_like(dg_ref)

        if groups == 1:
            dn = _dot_nt(dy_ref[...], w_ref[...])
        else:
            dn = sum(_dot_nt(dy_ref[p], w_ref[:, p * width:(p + 1) * width]) for p in range(groups))
        dx_ref[...] = dn
        _rms_bwd_epilogue(x_ref, g_ref, dres_ref, dx_ref, dxb_ref, dg_ref)

    row = pl.BlockSpec((tm, D_MODEL), lambda i: (i, 0))
    vec = pl.BlockSpec((1, D_MODEL), lambda i: (0, 0))
    dy_spec = (pl.BlockSpec((tm, w.shape[1]), lambda i: (i, 0)) if groups == 1
               else pl.BlockSpec((groups, tm, width), lambda i: (0, i, 0)))
    return _call(body, name=name, grid=(T // tm,),
                 in_specs=[dy_spec, _resident(w.shape), row, vec, row], out_specs=[row, row, vec],
                 out_shape=[jax.ShapeDtypeStruct((T, D_MODEL), F32), jax.ShapeDtypeStruct((T, D_MODEL), BF16),
                            jax.ShapeDtypeStruct((1, D_MODEL), F32)],
                 args=(dy, w, x, g, dres), ride=ride)


def _matmul_res(a, w, res, name, tm=1024):
    T, K = a.shape

    def body(a_ref, w_ref, r_ref, o_ref):
        o_ref[...] = r_ref[...] + _dot(a_ref[...], w_ref[...])

    row = pl.BlockSpec((tm, D_MODEL), lambda i: (i, 0))
    return _call(body, name=name, grid=(T // tm,),
                 in_specs=[pl.BlockSpec((tm, K), lambda i: (i, 0)), pl.BlockSpec((K, D_MODEL), lambda i: (0, 0)), row],
                 out_specs=[row], out_shape=[jax.ShapeDtypeStruct((T, D_MODEL), F32)], args=(a, w, res))[0]


def _matmul_nt(dy, w, name, tm=1024):
    T, N = dy.shape
    K = w.shape[0]

    def body(dy_ref, w_ref, o_ref):
        o_ref[...] = _dot_nt(dy_ref[...], w_ref[...]).astype(BF16)

    return _call(body, name=name, grid=(T // tm,),
                 in_specs=[pl.BlockSpec((tm, N), lambda i: (i, 0)), pl.BlockSpec((K, N), lambda i: (0, 0))],
                 out_specs=[pl.BlockSpec((tm, K), lambda i: (i, 0))], out_shape=[jax.ShapeDtypeStruct((T, K), BF16)],
                 args=(dy, w))[0]


def _wgrad(a, a_spec, b, b_specs, m, n, bm, bn, n_k, name, square_a=False, ride=None):
    b_specs = b_specs if isinstance(b_specs, (list, tuple)) else [b_specs]
    n_b = len(b_specs)

    def body(a_ref, *rest):
        b_refs = rest[:n_b]
        o_ref, acc = rest[n_b:]
        k = pl.program_id(2)

        @pl.when(k == 0)
        def _():
            acc[...] = jnp.zeros_like(acc)

        av = a_ref[...]
        if square_a:
            af = av.astype(F32)
            av = (af * af).astype(BF16)
        bv = b_refs[0][...] if n_b == 1 else jnp.concatenate([r[...] for r in b_refs], axis=1)
        acc[...] += _dot_tn(av, bv)

        @pl.when(k == n_k - 1)
        def _():
            o_ref[...] = acc[...].astype(BF16)

    res = _call(body, name=name, grid=(m // bm, n // bn, n_k), in_specs=[a_spec, *b_specs],
                out_specs=[pl.BlockSpec((bm, bn), lambda i, j, k: (i, j))], out_shape=[jax.ShapeDtypeStruct((m, n), BF16)],
                scratch_shapes=[pltpu.VMEM((bm, bn), F32)], args=(a, *([b] * n_b)), ride=ride)
    return res[0] if ride is None else (res[0][0], res[1])


def _mlp_fwd(h, g, w1, w2, name, tm=1024, tf=1024, ride=None):
    T = h.shape[0]

    def body(h_ref, g_ref, w1_ref, w2_ref, ho_ref, n_ref, u_ref):
        @pl.when(pl.program_id(1) == 0)
        def _():
            n_ref[...] = _rms(h_ref[...], g_ref[...]).astype(BF16)
            ho_ref[...] = h_ref[...]

        u = jnp.maximum(_dot(n_ref[...], w1_ref[...]), 0.0)
        u_ref[...] = u.astype(BF16)
        ho_ref[...] += _dot((u * u).astype(BF16), w2_ref[...])

    row = pl.BlockSpec((tm, D_MODEL), lambda i, j: (i, 0))
    return _call(body, name=name, grid=(T // tm, D_FF // tf),
                 in_specs=[row, pl.BlockSpec((1, D_MODEL), lambda i, j: (0, 0)),
                           pl.BlockSpec((D_MODEL, tf), lambda i, j: (0, j)), pl.BlockSpec((tf, D_MODEL), lambda i, j: (j, 0))],
                 out_specs=[row, row, pl.BlockSpec((tm, tf), lambda i, j: (i, j))],
                 out_shape=[jax.ShapeDtypeStruct((T, D_MODEL), F32), jax.ShapeDtypeStruct((T, D_MODEL), BF16),
                            jax.ShapeDtypeStruct((T, D_FF), BF16)],
                 args=(h, g, w1, w2), ride=ride)


def _mlp_bwd(dh, u, w1, w2, h, g, name, tm=1024, tf=1024, ride=None):
    T = h.shape[0]
    n_j = D_FF // tf

    def body(dh_ref, u_ref, w1_ref, w2_ref, h_ref, g_ref, dx_ref, dxb_ref, du_ref, dg_ref, dhb):
        acc = dx_ref
        i, j = pl.program_id(0), pl.program_id(1)

        @pl.when(j == 0)
        def _():
            dhb[...] = dh_ref[...].astype(BF16)
            acc[...] = jnp.zeros_like(acc)

        @pl.when((i == 0) & (j == 0))
        def _():
            dg_ref[...] = jnp.zeros_like(dg_ref)

        da = _dot_nt(dhb[...], w2_ref[...])
        du = (da * (2.0 * u_ref[...].astype(F32))).astype(BF16)
        du_ref[...] = du
        acc[...] += _dot_nt(du, w1_ref[...])

        @pl.when(j == n_j - 1)
        def _():
            _rms_bwd_epilogue(h_ref, g_ref, dh_ref, dx_ref, dxb_ref, dg_ref)

    row = pl.BlockSpec((tm, D_MODEL), lambda i, j: (i, 0))
    vec = pl.BlockSpec((1, D_MODEL), lambda i, j: (0, 0))
    hid = pl.BlockSpec((tm, tf), lambda i, j: (i, j))
    return _call(body, name=name, grid=(T // tm, n_j),
                 in_specs=[row, hid, pl.BlockSpec((D_MODEL, tf), lambda i, j: (0, j)),
                           pl.BlockSpec((tf, D_MODEL), lambda i, j: (j, 0)), row, vec],
                 out_specs=[row, row, hid, vec],
                 out_shape=[jax.ShapeDtypeStruct((T, D_MODEL), F32), jax.ShapeDtypeStruct((T, D_MODEL), BF16),
                            jax.ShapeDtypeStruct((T, D_FF), BF16), jax.ShapeDtypeStruct((1, D_MODEL), F32)],
                 scratch_shapes=[pltpu.VMEM((tm, D_MODEL), BF16)], args=(dh, u, w1, w2, h, g), ride=ride)


def _final_loss(h, g, target, name, tm=512):
    T = h.shape[0]

    def body(h_ref, g_ref, t_ref, dh_ref, dhb_ref, loss_ref, dg_ref):
        @pl.when(pl.program_id(0) == 0)
        def _():
            loss_ref[...] = jnp.zeros_like(loss_ref)
            dg_ref[...] = jnp.zeros_like(dg_ref)

        x = h_ref[...]
        gg = g_ref[...]
        r = lax.rsqrt(jnp.mean(x * x, axis=-1, keepdims=True) + EPS)
        xh = x * r
        e = xh * gg - t_ref[...]
        per_tok = jnp.mean(e * e, axis=-1, keepdims=True)
        loss_ref[...] += 0.5 * jnp.sum(per_tok, axis=0, keepdims=True)
        dy = e * (1.0 / D_MODEL)
        dg_ref[...] += jnp.sum(dy * xh, axis=0, keepdims=True)
        dxh = dy * gg
        dx = r * (dxh - xh * jnp.mean(dxh * xh, axis=-1, keepdims=True))
        dh_ref[...] = dx
        dhb_ref[...] = dx.astype(BF16)

    row = pl.BlockSpec((tm, D_MODEL), lambda i: (i, 0))
    vec = pl.BlockSpec((1, D_MODEL), lambda i: (0, 0))
    return _call(body, name=name, grid=(T // tm,), in_specs=[row, vec, row],
                 out_specs=[row, row, pl.BlockSpec((8, LANES), lambda i: (0, 0)), vec],
                 out_shape=[jax.ShapeDtypeStruct((T, D_MODEL), F32), jax.ShapeDtypeStruct((T, D_MODEL), BF16),
                            jax.ShapeDtypeStruct((8, LANES), F32), jax.ShapeDtypeStruct((1, D_MODEL), F32)],
                 args=(h, g, target))


def _ret_constants(S):
    log_gamma = jnp.log1p(-jnp.exp2(-5.0 - jnp.arange(RET_HEADS, dtype=F32)))
    idx = jnp.arange(RET_SC, dtype=F32)
    i, j = idx[:, None], idx[None, :]
    same_chunk = jnp.floor(i / CHUNK) == jnp.floor(j / CHUNK)
    mask = jnp.where((j <= i) | same_chunk, jnp.exp(log_gamma[:, None, None] * jnp.abs(i - j)[None]), 0.0)
    qdec = jnp.exp(log_gamma[:, None] * (idx + 1.0)[None, :])[:, :, None]
    kdec = jnp.exp(log_gamma[:, None] * (RET_SC - 1 - idx)[None, :])[:, :, None]
    cdec = jnp.exp(log_gamma * RET_SC)[:, None, None]
    half = RET_DK // 2
    inv = jnp.exp(-jnp.log(ROPE_BASE) * jnp.arange(half, dtype=F32) / half)
    ang = jnp.arange(S, dtype=F32)[:, None] * inv[None, :]
    return jnp.cos(ang), jnp.sin(ang), mask.astype(F32), qdec, kdec, cdec


def _rope(t, cs, sn):
    t1, t2 = t[:, :RET_DK // 2], t[:, RET_DK // 2:]
    return jnp.concatenate([t1 * cs - t2 * sn, t1 * sn + t2 * cs], axis=-1)


def _rope_bwd(d, cs, sn):
    d1, d2 = d[:, :RET_DK // 2], d[:, RET_DK // 2:]
    return jnp.concatenate([d1 * cs + d2 * sn, d2 * cs - d1 * sn], axis=-1)


def _ret_specs(B, S, reverse):
    n_sc = S // RET_SC

    def cc(c):
        return n_sc - 1 - c if reverse else c

    return dict(
        proj=pl.BlockSpec((B, RET_SC, RET_HEAD_COLS), lambda h, c: (0, cc(c), h)),
        trig=pl.BlockSpec((RET_SC, RET_DK // 2), lambda h, c: (cc(c), 0)),
        mask=pl.BlockSpec((None, RET_SC, RET_SC), lambda h, c: (h, 0, 0)),
        dec=pl.BlockSpec((None, RET_SC, 1), lambda h, c: (h, 0, 0)),
        cdec=pl.BlockSpec((None, 1, 1), lambda h, c: (h, 0, 0)),
        gn=pl.BlockSpec((1, RET_DV), lambda h, c: (0, h)),
        val=pl.BlockSpec((B, RET_SC, RET_DV), lambda h, c: (0, cc(c), h)),
        state=pl.BlockSpec((B, None, None, RET_DK, RET_DV), lambda h, c: (0, h, cc(c), 0, 0)),
    )


def _ret_qkvg(p_ref, kdec):
    qb = p_ref[:, 0:RET_DK]
    kb = p_ref[:, RET_DK:2 * RET_DK]
    kt = (kb.astype(F32) * kdec).astype(BF16)
    v = p_ref[:, 2 * RET_DK:2 * RET_DK + RET_DV]
    gate = p_ref[:, 2 * RET_DK + RET_DV:RET_HEAD_COLS].astype(F32)
    return qb, kb, kt, v, gate


def _group_norm(o):
    mu = jnp.mean(o, axis=-1, keepdims=True)
    oc = o - mu
    rstd = lax.rsqrt(jnp.mean(oc * oc, axis=-1, keepdims=True) + EPS)
    return oc * rstd, rstd


def _ret_fwd(proj, consts, gn, B, S, name, ride=None):
    T = B * S
    n_sc = S // RET_SC
    sp = _ret_specs(B, S, False)

    def body(p_ref, m_ref, qd_ref, kd_ref, cd_ref, gn_ref, y_ref, o_ref, st_ref, state):
        @pl.when(pl.program_id(1) == 0)
        def _():
            state[...] = jnp.zeros_like(state)

        seqs = range(B)
        qkvg = [_ret_qkvg(p_ref.at[b], kd_ref[...]) for b in seqs]
        qb, kb, kt, v = ([qkvg[b][i] for b in seqs] for i in range(4))
        sc = [_dot_nt(qb[b], kb[b]) for b in seqs]
        sb = [state[b].astype(BF16) for b in seqs]
        cross = [_dot(qb[b], sb[b]) for b in seqs]
        for b in seqs:
            st_ref[b] = sb[b]
        p = [(sc[b] * m_ref[...]).astype(BF16) for b in seqs]
        o = [_dot(p[b], v[b]) + cross[b] * qd_ref[...] for b in seqs]
        upd = [_dot_tn(kt[b], v[b]) for b in seqs]
        for b in seqs:
            o_ref[b] = o[b].astype(BF16)
            ohat, _ = _group_norm(o[b])
            gate = qkvg[b][4]
            y_ref[b] =(gate * jax.nn.sigmoid(gate) * (ohat * gn_ref[...])).astype(BF16)
            state[b] = state[b] * cd_ref[...] + upd[b]

    res = _call(
        body, name=name, grid=(RET_HEADS, n_sc),
        in_specs=[sp["proj"], sp["mask"], sp["dec"], sp["dec"], sp["cdec"], sp["gn"]],
        out_specs=[sp["val"], sp["val"], sp["state"]],
        out_shape=[jax.ShapeDtypeStruct((B, S, RET_VW), BF16), jax.ShapeDtypeStruct((B, S, RET_VW), BF16),
                   jax.ShapeDtypeStruct((B, RET_HEADS, n_sc, RET_DK, RET_DV), BF16)],
        scratch_shapes=[pltpu.VMEM((B, RET_DK, RET_DV), F32)],
        args=(proj.reshape(B, S, -1), *consts[2:], gn), ride=ride)
    (y, o, states), lands = res if ride is not None else (res, None)
    outs = [y.reshape(T, RET_VW), o.reshape(T, RET_VW), states]
    return outs if ride is None else (outs, lands)


def _ret_bwd(proj, consts, gn, o, states, dy, B, S, name, ride=None):
    T = B * S
    n_sc = S // RET_SC
    sp = _ret_specs(B, S, True)

    def body(p_ref, cos_ref, sin_ref, m_ref, qd_ref, kd_ref, cd_ref, gn_ref, o_ref, st_ref, dy_ref,
             dp_ref, dgn_ref, dstate):
        @pl.when(pl.program_id(1) == 0)
        def _():
            dstate[...] = jnp.zeros_like(dstate)
            dgn_ref[...] = jnp.zeros_like(dgn_ref)

        seqs = range(B)
        cs, sn = cos_ref[...], sin_ref[...]
        m, gnv = m_ref[...], gn_ref[...]
        qkvg = [_ret_qkvg(p_ref.at[b], kd_ref[...]) for b in seqs]
        qb, kb, kt, v = ([qkvg[b][i] for b in seqs] for i in range(4))
        sc = [_dot_nt(qb[b], kb[b]) for b in seqs]
        dsb = [dstate[b].astype(BF16) for b in seqs]
        dv_state = [_dot(kt[b], dsb[b]) for b in seqs]
        dkt = [_dot_nt(v[b], dsb[b]) for b in seqs]
        do, dgate = [], []
        for b in seqs:
            gate = qkvg[b][4]
            ohat, rstd = _group_norm(o_ref[b].astype(F32))
            dyv = dy_ref[b].astype(F32)
            sg = jax.nn.sigmoid(gate)
            don = dyv * (gate * sg)
            dgate.append(dyv * (ohat * gnv) * (sg * (1.0 + gate * (1.0 - sg))))
            dgn_ref[...] += jnp.sum(don * ohat, axis=0, keepdims=True)
            dohat = don * gnv
            do.append(rstd * (dohat - jnp.mean(dohat, axis=-1, keepdims=True)
                              - ohat * jnp.mean(dohat * ohat, axis=-1, keepdims=True)))
        dob = [do[b].astype(BF16) for b in seqs]
        doq = [(do[b] * qd_ref[...]).astype(BF16) for b in seqs]
        dsc_f = [_dot_nt(dob[b], v[b]) for b in seqs]
        dq_state = [_dot_nt(doq[b], st_ref[b]) for b in seqs]
        dstate_upd = [_dot_tn(qb[b], doq[b]) for b in seqs]
        p = [(sc[b] * m).astype(BF16) for b in seqs]
        dsc = [(dsc_f[b] * m).astype(BF16) for b in seqs]
        dv = [_dot_tn(p[b], dob[b]) + dv_state[b] for b in seqs]
        dq = [_dot(dsc[b], kb[b]) + dq_state[b] for b in seqs]
        dk = [(_dot_tn(dsc[b], qb[b]) + dkt[b] * kd_ref[...]) * (RET_DK ** -0.5) for b in seqs]
        for b in seqs:
            dstate[b] = dstate[b] * cd_ref[...] + dstate_upd[b]
            dp_ref[b, :, 0:RET_DK] = _rope_bwd(dq[b], cs, sn).astype(BF16)
            dp_ref[b, :, RET_DK:2 * RET_DK] = _rope_bwd(dk[b], cs, sn).astype(BF16)
            dp_ref[b, :, 2 * RET_DK:2 * RET_DK + RET_DV] = dv[b].astype(BF16)
            dp_ref[b, :, 2 * RET_DK + RET_DV:RET_HEAD_COLS] = dgate[b].astype(BF16)

    res = _call(
        body, name=name, grid=(RET_HEADS, n_sc),
        in_specs=[sp["proj"], sp["trig"], sp["trig"], sp["mask"], sp["dec"], sp["dec"], sp["cdec"], sp["gn"],
                  sp["val"], sp["state"], sp["val"]],
        out_specs=[sp["proj"], sp["gn"]],
        out_shape=[jax.ShapeDtypeStruct((B, S, RET_HEADS * RET_HEAD_COLS), BF16), jax.ShapeDtypeStruct((1, RET_VW), F32)],
        scratch_shapes=[pltpu.VMEM((B, RET_DK, RET_DV), F32)],
        args=(proj.reshape(B, S, -1), *consts, gn, o.reshape(B, S, -1), states, dy.reshape(B, S, -1)), ride=ride)
    (dproj, dgn), lands = res if ride is not None else (res, None)
    outs = [dproj.reshape(T, -1), dgn]
    return outs if ride is None else (outs, lands)


def _ring_index():
    u = np.arange(ATT_RING)
    offset = np.where(u < ATT_KW, u, u - ATT_RING)
    return np.clip(offset - BAND_PAST, -MAX_REL, CHUNK - 1) + MAX_REL


def _bias_ring(rel):
    n_clip = BAND_PAST - MAX_REL
    n_hi = ATT_KW - n_clip - REL_TABLE
    ring = jnp.concatenate([jnp.broadcast_to(rel[:, :1], (ATT_HEADS, n_clip)), rel,
                            jnp.broadcast_to(rel[:, -1:], (ATT_HEADS, n_hi)),
                            jnp.broadcast_to(rel[:, :1], (ATT_HEADS, ATT_CQ))], axis=1)
    return jnp.broadcast_to(ring.reshape(ATT_PAIRS, 2, 1, ATT_RING), (ATT_PAIRS, 2, 8, ATT_RING))


def _band_masks():
    i = np.arange(ATT_CQ)[:, None]
    j = np.arange(ATT_KW)[None, :]
    lo = CHUNK * (i // CHUNK)
    band = np.where((j >= lo) & (j < lo + BAND_PAST + CHUNK), 0.0, NEG)
    return np.stack([band + np.where(j + v * ATT_CQ >= BAND_PAST, 0.0, NEG)
                     for v in range(ATT_VARIANTS)]).astype(np.float32)


def _attn_bias(bias_scr, ring_ref, band_ref):
    for hh in range(2):
        rows = jnp.broadcast_to(ring_ref[hh, 0:1, :], (ATT_CQ, ATT_RING))
        toeplitz = pltpu.roll(rows, 0, 1, stride=1, stride_axis=0)[:, :ATT_KW]
        for v in range(ATT_VARIANTS):
            bias_scr[v, hh] = toeplitz + band_ref[v]


ATT_STRIP = 32
ATT_SCALE = ATT_DH ** -0.5


def _strips(fn):
    def strip(r, carry):
        fn(pl.ds(pl.multiple_of(r * ATT_STRIP, ATT_STRIP), ATT_STRIP))
        return carry

    lax.fori_loop(0, ATT_CQ // ATT_STRIP, strip, 0, unroll=True)


def _attn_prepare(kpad, vpad, qm, qkv_ref):
    kpad[0:BAND_PAST, :] = jnp.zeros((BAND_PAST, LANES), BF16)
    vpad[0:BAND_PAST, :] = jnp.zeros((BAND_PAST, LANES), BF16)
    kpad[BAND_PAST:, :] = qkv_ref[1]
    vpad[BAND_PAST:, :] = qkv_ref[2]
    lane = lax.broadcasted_iota(jnp.int32, (1, LANES), 1)
    q = qkv_ref[0] * ATT_SCALE
    for hh in range(2):
        qm[hh] = jnp.where((lane >= ATT_DH * hh) & (lane < ATT_DH * (hh + 1)), q, jnp.zeros_like(q))


def _attn_scores(s_ref, qm, kpad, b_ref, t, n_qb):
    t = jnp.minimum(t, n_qb - 1)
    qs = pl.multiple_of(t * ATT_CQ, ATT_CQ)
    variant = jnp.minimum(t, ATT_VARIANTS - 1)
    kw = kpad[pl.ds(qs, ATT_KW), :]
    for hh in range(2):
        s_ref[hh] = _dot_nt(qm[hh, pl.ds(qs, ATT_CQ), :], kw) + b_ref[variant, hh]


def _attn_softmax(s_ref, e_ref, linv_ref, m_ref=None):
    def strip(rows):
        s = s_ref[rows, :]
        m = jnp.max(s, axis=-1, keepdims=True)
        e = jnp.exp(s - m)
        e_ref[rows, :] = e.astype(BF16)
        linv_ref[rows, :] = jnp.broadcast_to(1.0 / jnp.sum(e, axis=-1, keepdims=True), (ATT_STRIP, LANES))
        if m_ref is not None:
            m_ref[rows, :] = jnp.broadcast_to(m, (ATT_STRIP, LANES))

    _strips(strip)


_RING_SPEC = pl.BlockSpec((None, 2, 8, ATT_RING), lambda hp, b: (hp, 0, 0, 0))
_BAND_SPEC = pl.BlockSpec((ATT_VARIANTS, ATT_CQ, ATT_KW), lambda hp, b: (0, 0, 0))


def _attn_fwd(qkv3, ring, B, S, name, ride=None):
    T = B * S
    n_qb = S // ATT_CQ

    def body(qkv_ref, ring_ref, band_ref, o_ref, kpad, vpad, qm, b_ref, s_scr, e_scr, linv_scr):
        @pl.when(pl.program_id(1) == 0)
        def _():
            _attn_bias(b_ref, ring_ref, band_ref)

        _attn_prepare(kpad, vpad, qm, qkv_ref)
        e_scr[...] = jnp.zeros_like(e_scr)
        linv_scr[...] = jnp.zeros_like(linv_scr)
        lane = lax.broadcasted_iota(jnp.int32, (1, LANES), 1)

        def softmax(slot):
            for hh in range(2):
                _attn_softmax(s_scr.at[slot, hh], e_scr.at[slot, hh], linv_scr.at[slot, hh])

        def output(t, slot):
            qs = pl.multiple_of(jnp.maximum(t, 0) * ATT_CQ, ATT_CQ)
            vw = vpad[pl.ds(qs, ATT_KW), :]
            outs = [_dot(e_scr[slot, hh], vw) * linv_scr[slot, hh] for hh in range(2)]
            o_ref[pl.ds(qs, ATT_CQ), :] = jnp.where(lane < ATT_DH, outs[0], outs[1]).astype(BF16)

        def pair(u, carry):
            t = 2 * u
            _attn_scores(s_scr.at[1], qm, kpad, b_ref, t + 1, n_qb)
            softmax(0)
            output(t - 1, 1)
            _attn_scores(s_scr.at[0], qm, kpad, b_ref, t + 2, n_qb)
            softmax(1)
            output(t, 0)
            return carry

        _attn_scores(s_scr.at[0], qm, kpad, b_ref, 0, n_qb)
        lax.fori_loop(0, n_qb // 2, pair, 0)
        output(n_qb - 1, 1)

    return _call(body, name=name, grid=(ATT_PAIRS, B),
                 in_specs=[pl.BlockSpec((3, S, LANES), lambda hp, b: (0, b, hp)), _RING_SPEC, _BAND_SPEC],
                 out_specs=[pl.BlockSpec((S, LANES), lambda hp, b: (b, hp))],
                 out_shape=[jax.ShapeDtypeStruct((T, D_MODEL), BF16)],
                 scratch_shapes=[pltpu.VMEM((S + BAND_PAST, LANES), BF16), pltpu.VMEM((S + BAND_PAST, LANES), BF16),
                                 pltpu.VMEM((2, S, LANES), BF16), pltpu.VMEM((ATT_VARIANTS, 2, ATT_CQ, ATT_KW), F32),
                                 pltpu.VMEM((2, 2, ATT_CQ, ATT_KW), F32),
                                 pltpu.VMEM((2, 2, ATT_CQ, ATT_KW), BF16), pltpu.VMEM((2, 2, ATT_CQ, LANES), F32)],
                 args=(qkv3, ring, jnp.asarray(_band_masks())), ride=ride)


def _attn_bwd(qkv3, ring, do, B, S, name, ride=None):
    T = B * S
    n_qb = S // ATT_CQ

    def body(qkv_ref, ring_ref, band_ref, do_ref, dqkv_ref, dring_ref, kpad, vpad, qm, dkacc, dvacc, b_ref, db_ref):
        @pl.when(pl.program_id(1) == 0)
        def _():
            _attn_bias(b_ref, ring_ref, band_ref)
            db_ref[...] = jnp.zeros_like(db_ref)

        _attn_prepare(kpad, vpad, qm, qkv_ref)
        dkacc[...] = jnp.zeros_like(dkacc)
        dvacc[...] = jnp.zeros_like(dvacc)
        lane = lax.broadcasted_iota(jnp.int32, (1, LANES), 1)

        def step(qb, carry):
            qs = pl.multiple_of(qb * ATT_CQ, ATT_CQ)
            variant = jnp.minimum(qb, ATT_VARIANTS - 1)
            dov = do_ref[pl.ds(qs, ATT_CQ), :].astype(F32)
            kw = kpad[pl.ds(qs, ATT_KW), :]
            vw = vpad[pl.ds(qs, ATT_KW), :]
            heads = (0, 1)
            qmh = [qm[hh, pl.ds(qs, ATT_CQ), :] for hh in heads]
            s = [_dot_nt(qmh[hh], kw) + b_ref[variant, hh] for hh in heads]
            e, linv, dom, dp = [None, None], [None, None], [None, None], [None, None]
            for hh in heads:
                e[hh] = jnp.exp(s[hh] - jnp.max(s[hh], axis=-1, keepdims=True))
                linv[hh] = 1.0 / jnp.sum(e[hh], axis=-1, keepdims=True)
                sel = (lane >= ATT_DH * hh) & (lane < ATT_DH * (hh + 1))
                dom[hh] = jnp.where(sel, dov * linv[hh], 0.0).astype(BF16)
                dp[hh] = _dot_nt(dom[hh], vw)
            dqs, dk, dv = [], None, None
            for hh in heads:
                ds = e[hh] * (dp[hh] - jnp.sum(dp[hh] * e[hh], axis=-1, keepdims=True) * linv[hh])
                db_ref[hh] += ds
                dsb = ds.astype(BF16)
                dqs.append(_dot(dsb, kw) * ATT_SCALE)
                dk_h = _dot_tn(qmh[hh], dsb)
                dv_h = _dot_tn(dom[hh], e[hh].astype(BF16))
                dk = dk_h if dk is None else dk + dk_h
                dv = dv_h if dv is None else dv + dv_h
            dqkv_ref[0, pl.ds(qs, ATT_CQ), :] = jnp.where(lane < ATT_DH, dqs[0], dqs[1]).astype(BF16)
            dkacc[:, pl.ds(qs, ATT_KW)] += dk
            dvacc[:, pl.ds(qs, ATT_KW)] += dv
            return carry

        lax.fori_loop(0, n_qb, step, 0, unroll=4)
        dqkv_ref[1] = dkacc[:, BAND_PAST:].T.astype(BF16)
        dqkv_ref[2] = dvacc[:, BAND_PAST:].T.astype(BF16)

        @pl.when(pl.program_id(1) == B - 1)
        def _():
            r = lax.broadcasted_iota(jnp.int32, (ATT_CQ, ATT_CQ), 0)
            c = lax.broadcasted_iota(jnp.int32, (ATT_CQ, ATT_CQ), 1)
            reverse = jnp.where(r + c == ATT_CQ - 1, 1.0, 0.0).astype(BF16)
            for hh in range(2):
                x = jnp.concatenate([db_ref[hh], jnp.zeros((ATT_CQ, ATT_RING - ATT_KW), F32)], axis=1)
                flipped = jnp.zeros((ATT_CQ, ATT_RING), F32)
                for _ in range(3):
                    part = x.astype(BF16)
                    flipped = flipped + _dot(reverse, part)
                    x = x - part.astype(F32)
                aligned = pltpu.roll(flipped, ATT_KW + 1, 1, stride=1, stride_axis=0)
                dring_ref[hh] = jnp.sum(aligned.reshape(ATT_CQ // 8, 8, ATT_RING), axis=0)

    qkv_spec = pl.BlockSpec((3, S, LANES), lambda hp, b: (0, b, hp))
    return _call(body, name=name, grid=(ATT_PAIRS, B),
                 in_specs=[qkv_spec, _RING_SPEC, _BAND_SPEC, pl.BlockSpec((S, LANES), lambda hp, b: (b, hp))],
                 out_specs=[qkv_spec, _RING_SPEC],
                 out_shape=[jax.ShapeDtypeStruct((3, T, D_MODEL), BF16),
                            jax.ShapeDtypeStruct((ATT_PAIRS, 2, 8, ATT_RING), F32)],
                 scratch_shapes=[pltpu.VMEM((S + BAND_PAST, LANES), BF16), pltpu.VMEM((S + BAND_PAST, LANES), BF16),
                                 pltpu.VMEM((2, S, LANES), BF16),
                                 pltpu.VMEM((LANES, S + BAND_PAST), F32), pltpu.VMEM((LANES, S + BAND_PAST), F32),
                                 pltpu.VMEM((ATT_VARIANTS, 2, ATT_CQ, ATT_KW), F32), pltpu.VMEM((2, ATT_CQ, ATT_KW), F32)],
                 args=(qkv3, ring, jnp.asarray(_band_masks()), do), ride=ride)


def _rel_bias_grad(dring, name):
    fold = np.zeros((ATT_RING, REL_PAD), np.float32)
    fold[np.arange(ATT_RING), _ring_index()] = 1.0
    fold = jnp.asarray(fold, BF16)

    def body(d_ref, f_ref, o_ref):
        x = jnp.sum(d_ref[...], axis=0, keepdims=True)
        x = jnp.broadcast_to(x, (8, ATT_RING))
        acc = jnp.zeros((8, REL_PAD), F32)
        for _ in range(3):
            part = x.astype(BF16)
            acc = acc + _dot(part, f_ref[...])
            x = x - part.astype(F32)
        o_ref[...] = acc[0:1, :]

    out = _call(body, name=name, grid=(ATT_HEADS,),
                in_specs=[pl.BlockSpec((None, 8, ATT_RING), lambda h: (h, 0, 0)),
                          pl.BlockSpec((ATT_RING, REL_PAD), lambda h: (0, 0))],
                out_specs=[pl.BlockSpec((None, 1, REL_PAD), lambda h: (h, 0, 0))],
                out_shape=[jax.ShapeDtypeStruct((ATT_HEADS, 1, REL_PAD), F32)],
                args=(dring.reshape(ATT_HEADS, 8, ATT_RING), fold))[0]
    return out.reshape(ATT_HEADS, REL_PAD)


def _adamw(w, g, m, v):
    m = ADAM_B1 * m + (1.0 - ADAM_B1) * g
    v = ADAM_B2 * v + (1.0 - ADAM_B2) * (g * g)
    m_hat = m / (1.0 - ADAM_B1 ** ADAM_STEP)
    v_hat = v / (1.0 - ADAM_B2 ** ADAM_STEP)
    delta = -ADAM_LR * (m_hat / (jnp.sqrt(v_hat) + ADAM_EPS) + ADAM_WD * w)
    return delta, m, v


def _sum_devices(ref):
    g = ref[0].astype(F32)
    for d in range(1, N_DEV):
        g = g + ref[d].astype(F32)
    return g


def _adamw_reduce(lands, w, m, v, name, tr=256):
    L, R, C = w.shape
    tr = min(tr, R)
    n_i = R // tr

    def body(*refs):
        l_refs = refs[:L]
        w_ref, m_ref, v_ref, g_out, d_out, m_out, v_out = refs[L:]
        layer = pl.program_id(0)
        for l in range(L):
            @pl.when(layer == l)
            def _(l=l):
                g = _sum_devices(l_refs[l])
                g_out[...] = g
                d_out[...], m_out[...], v_out[...] = _adamw(w_ref[...], g, m_ref[...], v_ref[...])

    def land_spec(l):
        return pl.BlockSpec((N_DEV, tr, C), lambda ly, i: (0, jnp.where(ly == l, i, jnp.where(ly < l, 0, n_i - 1)), 0))

    blk = pl.BlockSpec((None, tr, C), lambda ly, i: (ly, i, 0))
    return _call(body, name=name, grid=(L, n_i), in_specs=[land_spec(l) for l in range(L)] + [blk, blk, blk],
                 out_specs=[blk] * 4, out_shape=[jax.ShapeDtypeStruct((L, R, C), F32)] * 4, args=(*lands, w, m, v))


def _small_update(land_small, land_rel, w, m, v, name):
    def body(ls_ref, lr_ref, w_ref, m_ref, v_ref, g_out, d_out, m_out, v_out, rel_out):
        g = _sum_devices(ls_ref)
        g_out[...] = g
        d_out[...], m_out[...], v_out[...] = _adamw(w_ref[...], g, m_ref[...], v_ref[...])
        rel_out[...] = _sum_devices(lr_ref)

    return pl.pallas_call(
        body, name=name,
        out_shape=[jax.ShapeDtypeStruct(w.shape, F32)] * 4 + [jax.ShapeDtypeStruct(land_rel.shape[1:], F32)],
    )(land_small, land_rel, w, m, v)


def _adamw_plain(w, g, m, v, name):
    def body(w_ref, g_ref, m_ref, v_ref, d_out, m_out, v_out):
        d_out[...], m_out[...], v_out[...] = _adamw(w_ref[...], g_ref[...], m_ref[...], v_ref[...])

    return pl.pallas_call(body, name=name, out_shape=[jax.ShapeDtypeStruct(w.shape, F32)] * 3)(w, g, m, v)


def _ret_piece_of_column_block(p):
    per_head = RET_HEAD_COLS // RET_DK
    qk = jnp.where(p < RET_HEADS, per_head * p, per_head * (p - RET_HEADS) + 1)
    pv = p - 2 * RET_HEADS
    vv = per_head * (pv // 2) + 2 + pv % 2
    pg = p - 4 * RET_HEADS
    gg = per_head * (pg // 2) + 4 + pg % 2
    return jnp.where(p < 2 * RET_HEADS, qk, jnp.where(p < 4 * RET_HEADS, vv, gg))


def _step(x, target, shards, mix_g, gn_g, rel_shard, mlp_g, final_g):
    s_rin, s_rout, s_ain, s_aout, s_w1, s_w2 = shards
    B, S, _ = x.shape
    T = B * S
    tk = min(T, 2048)
    n_k = T // tk
    h0 = x.reshape(T, D_MODEL)
    tgt = target.reshape(T, D_MODEL)
    consts = _ret_constants(S)
    w1_cols = w2_rows = D_FF // N_DEV

    rel_cols = REL_TABLE // N_DEV
    per_shard = RET_IN // N_DEV // RET_DK

    def head_layout(r):
        return lambda ref, blk: ref.at[:, pl.ds(pl.multiple_of(
            _ret_piece_of_column_block(per_shard * blk + r) * RET_DK, LANES), RET_DK)]

    w_rin_heads, rel_all = _ChipGather(
        [s_rin[:, r * RET_DK:(r + 1) * RET_DK] for r in range(per_shard)] + [rel_shard],
        [jax.ShapeDtypeStruct((D_MODEL, RET_IN), BF16), jax.ShapeDtypeStruct((N_DEV, ATT_HEADS, rel_cols), F32)],
        [0] * per_shard + [1], [head_layout(r) for r in range(per_shard)] + [_slot]).run("gather_first")
    ring = _bias_ring(rel_all.transpose(1, 0, 2).reshape(ATT_HEADS, REL_TABLE))

    def gather(*items):
        return _ChipGather([s for s, _, _ in items], [jax.ShapeDtypeStruct(full, BF16) for _, _, full in items],
                           list(range(len(items))), [view for _, view, _ in items])

    def scatter(*items):
        return _Exchange([p for p, _, _ in items], [(a, view, a, _slot) for a, (_, view, _) in enumerate(items)],
                         [jax.ShapeDtypeStruct((N_DEV, *shard), BF16) for _, _, shard in items])

    w1_full, w2_full = (D_MODEL, D_FF), (D_FF, D_MODEL)
    w1_shard, w2_shard = (D_MODEL, w1_cols), (w2_rows, D_MODEL)

    ride = gather((s_rout, _rows(RET_VW // N_DEV), (RET_VW, D_MODEL)), (s_w1[0], _cols(w1_cols), w1_full))
    (proj, n0), (w_rout, w1_0) = _ret_proj(h0, mix_g[0:1], w_rin_heads, consts[0], consts[1], S, "ret_proj", 512, ride=ride)
    (y, o, states), (w2_0,) = _ret_fwd(proj, consts, gn_g, B, S, "ret_fwd", ride=gather((s_w2[0], _rows(w2_rows), w2_full)))
    h1 = _matmul_res(y, w_rout, h0, "ret_out")
    ride = gather((s_ain, _cols(3 * D_MODEL // N_DEV), (D_MODEL, 3 * D_MODEL)),
                  (s_aout, _rows(D_MODEL // N_DEV), (D_MODEL, D_MODEL)))
    (h2, n1, u1), (w_ain, w_aout) = _mlp_fwd(h1, mlp_g[0:1], w1_0, w2_0, "mlp0_fwd", ride=ride)
    qkv3, n2 = _norm_proj(h2, mix_g[1:2], w_ain, 3, "att_proj", 1024)
    ride = gather((s_w1[1], _cols(w1_cols), w1_full), (s_w2[1], _rows(w2_rows), w2_full))
    (o2,), (w1_1, w2_1) = _attn_fwd(qkv3, ring, B, S, "att_fwd", ride=ride)
    h3 = _matmul_res(o2, w_aout, h2, "att_out")
    h4, n3, u3 = _mlp_fwd(h3, mlp_g[1:2], w1_1, w2_1, "mlp1_fwd")
    dh4, dh4b, loss, dg_final = _final_loss(h4, final_g, tgt, "final_loss")

    def tok(width):
        return dict(a=pl.BlockSpec((tk, width), lambda i, j, k: (k, i)), b=pl.BlockSpec((tk, width), lambda i, j, k: (k, j)))

    def mlp_grads(dh, dhb, u, n, w1, w2, h, g, tag, ride=None, w2_rides_on_dw1=False):
        res = _mlp_bwd(dh, u, w1, w2, h, g, tag + "_bwd", ride=ride)
        (dhi, dhib, du, dg), lands = res if ride is not None else (res, None)
        gw2 = _wgrad(u, tok(1024)["a"], dhb, tok(1024)["b"], D_FF, D_MODEL, 1024, 1024, n_k, tag + "_dw2", square_a=True)
        ride_w2 = scatter((gw2, _rows(w2_rows), w2_shard)) if w2_rides_on_dw1 else None
        gw1 = _wgrad(n, tok(1024)["a"], du, tok(1024)["b"], D_MODEL, D_FF, 1024, 1024, n_k, tag + "_dw1", ride=ride_w2)
        return dhi, dhib, gw1, gw2, dg, lands

    dh3, dh3b, gw1_1, gw2_1, dg_mlp1, _ = mlp_grads(dh4, dh4b, u3, n3, w1_1, w2_1, h3, mlp_g[1:2], "mlp1")
    do2 = _matmul_nt(dh3b, w_aout, "att_out_bwd")
    g_aout = _wgrad(o2, tok(1024)["a"], dh3b, tok(1024)["b"], D_MODEL, D_MODEL, 1024, 1024, n_k, "att_out_dw")
    ride = scatter((gw1_1, _cols(w1_cols), w1_shard), (gw2_1, _rows(w2_rows), w2_shard),
                   (g_aout, _rows(D_MODEL // N_DEV), (D_MODEL // N_DEV, D_MODEL)))
    (dqkv3, dring), (l_w1_1, l_w2_1, l_aout) = _attn_bwd(qkv3, ring, do2, B, S, "att_bwd", ride=ride)
    g_rel = _rel_bias_grad(dring, "rel_bias_grad")
    dh2, dh2b, dg_mix1 = _proj_bwd(dqkv3, w_ain, h2, mix_g[1:2], dh3, "att_proj_bwd", 1024)
    g_ain = _wgrad(n2, tok(1024)["a"], dqkv3, pl.BlockSpec((None, tk, D_MODEL), lambda i, j, k: (j, k, 0)),
                   D_MODEL, 3 * D_MODEL, 1024, 1024, n_k, "att_proj_dw")
    ride = scatter((g_ain, _cols(3 * D_MODEL // N_DEV), (D_MODEL, 3 * D_MODEL // N_DEV)))
    dh1, dh1b, (gw1_0, (l_w2_0,)), _, dg_mlp0, (l_ain,) = mlp_grads(
        dh2, dh2b, u1, n1, w1_0, w2_0, h1, mlp_g[0:1], "mlp0", ride=ride, w2_rides_on_dw1=True)
    dy = _matmul_nt(dh1b, w_rout, "ret_out_bwd")
    g_rout = _wgrad(y, tok(1024)["a"], dh1b, tok(1024)["b"], RET_VW, D_MODEL, 1024, 1024, n_k, "ret_out_dw")
    ride = scatter((gw1_0, _cols(w1_cols), w1_shard), (g_rout, _rows(RET_VW // N_DEV), (RET_VW // N_DEV, D_MODEL)))
    (dproj, dgn), (l_w1_0, l_rout) = _ret_bwd(proj, consts, gn_g, o, states, dy, B, S, "ret_bwd", ride=ride)
    g_rin =_wgrad(n0, tok(1024)["a"], dproj,
                   [pl.BlockSpec((tk, RET_DK), functools.partial(
                       lambda i, j, k, r: (k, _ret_piece_of_column_block(per_shard * j + r)), r=r)) for r in range(per_shard)],
                   D_MODEL, RET_IN, 1024, per_shard * RET_DK, n_k, "ret_proj_dw")
    ride = scatter((g_rin, _cols(RET_IN // N_DEV), (D_MODEL, RET_IN // N_DEV)))
    (dx, _, dg_mix0), (l_rin,) = _proj_bwd(dproj, w_rin_heads, h0, mix_g[0:1], dh1, "ret_proj_bwd", 512, ride=ride)

    small = jnp.concatenate([dg_mix0, dg_mix1, dg_mlp0, dg_mlp1, dg_final, dgn.reshape(2, D_MODEL),
                             jnp.zeros((1, D_MODEL), F32)], axis=0)
    l_small, l_rel = _Exchange(
        [small, g_rel], [(0, _whole, 0, _slot), (1, _whole, 1, _slot)],
        [jax.ShapeDtypeStruct((N_DEV, 8, D_MODEL), F32), jax.ShapeDtypeStruct((N_DEV, ATT_HEADS, REL_PAD), F32)],
    ).run("scatter_small")
    lands = dict(rin=l_rin, rout=l_rout, ain=l_ain, aout=l_aout, w1=(l_w1_0, l_w1_1), w2=(l_w2_0, l_w2_1),
                 small=l_small, rel=l_rel)
    return loss[0, 0], dx.reshape(B, S, D_MODEL), lands


def kernel(x, mix_norm_g, ret_w_in, ret_gn_g, ret_w_out, att_w_in, att_rel_bias, att_w_out, mlp_norm_g, mlp_w1, mlp_w2, final_norm_g, loss_target, m_mix_norm_g, m_ret_w_in, m_ret_gn_g, m_ret_w_out, m_att_w_in, m_att_rel_bias, m_att_w_out, m_mlp_norm_g, m_mlp_w1, m_mlp_w2, m_final_norm_g, v_mix_norm_g, v_ret_w_in, v_ret_gn_g, v_ret_w_out, v_att_w_in, v_att_rel_bias, v_att_w_out, v_mlp_norm_g, v_mlp_w1, v_mlp_w2, v_final_norm_g):
    me = _lin(_place())
    rel_cols = REL_TABLE // N_DEV
    shards = (ret_w_in[0].astype(BF16), ret_w_out[0].astype(BF16), att_w_in[0].astype(BF16), att_w_out[0].astype(BF16),
              (mlp_w1[0].astype(BF16), mlp_w1[1].astype(BF16)), (mlp_w2[0].astype(BF16), mlp_w2[1].astype(BF16)))
    loss_part, grad_x, lands = _step(x, loss_target, shards, mix_norm_g, ret_gn_g, att_rel_bias[0], mlp_norm_g,
                                     final_norm_g.reshape(1, D_MODEL))
    loss = lax.psum(loss_part, ("x", "y", "c"))

    u_rin = _adamw_reduce([lands["rin"]], ret_w_in, m_ret_w_in, v_ret_w_in, "update_ret_w_in")
    u_rout = _adamw_reduce([lands["rout"]], ret_w_out, m_ret_w_out, v_ret_w_out, "update_ret_w_out")
    u_ain = _adamw_reduce([lands["ain"]], att_w_in, m_att_w_in, v_att_w_in, "update_att_w_in")
    u_aout = _adamw_reduce([lands["aout"]], att_w_out, m_att_w_out, v_att_w_out, "update_att_w_out")
    u_w1 = _adamw_reduce(lands["w1"], mlp_w1, m_mlp_w1, v_mlp_w1, "update_mlp_w1")
    u_w2 = _adamw_reduce(lands["w2"], mlp_w2, m_mlp_w2, v_mlp_w2, "update_mlp_w2")

    def pack(mix, mlp, fin, gn):
        return jnp.concatenate([mix, mlp, fin.reshape(1, D_MODEL), gn.reshape(2, D_MODEL), jnp.zeros((1, D_MODEL), F32)], axis=0)

    small_w = pack(mix_norm_g, mlp_norm_g, final_norm_g, ret_gn_g)
    small_m = pack(m_mix_norm_g, m_mlp_norm_g, m_final_norm_g, m_ret_gn_g)
    small_v = pack(v_mix_norm_g, v_mlp_norm_g, v_final_norm_g, v_ret_gn_g)
    sg, sd, sm, sv, rel_sum = _small_update(lands["small"], lands["rel"], small_w, small_m, small_v, "update_small")
    g_rel_mine = lax.dynamic_slice(rel_sum, (0, me * rel_cols), (ATT_HEADS, rel_cols))
    rel_d, rel_m, rel_v = _adamw_plain(att_rel_bias[0], g_rel_mine, m_att_rel_bias[0], v_att_rel_bias[0], "update_rel_bias")
    u_rel = [g_rel_mine[None], rel_d[None], rel_m[None], rel_v[None]]

    def unpack(t):
        return dict(mix=t[0:2], mlp=t[2:4], fin=t[4], gn=t[5:7].reshape(1, RET_VW))

    us = [unpack(t) for t in (sg, sd, sm, sv)]
    outs = [loss, grad_x]
    for k in range(4):
        outs += [us[k]["mix"], u_rin[k], us[k]["gn"], u_rout[k], u_ain[k], u_rel[k], u_aout[k], us[k]["mlp"],
                 u_w1[k], u_w2[k], us[k]["fin"]]
    return tuple(outs)
```

```python
import functools

import numpy as np
import jax
import jax.numpy as jnp
from jax import lax
from jax.experimental import pallas as pl
from jax.experimental.pallas import tpu as pltpu

F32, BF16 = jnp.float32, jnp.bfloat16

D_MODEL = 1024
CHUNK = 64
RET_HEADS, RET_DK, RET_DV = 4, 256, 512
RET_QK, RET_VW = RET_HEADS * RET_DK, RET_HEADS * RET_DV
RET_IN = 2 * RET_QK + 2 * RET_VW
RET_HEAD_COLS = 2 * RET_DK + 2 * RET_DV
RET_SC = 256
ROPE_BASE = 10000.0
ATT_HEADS, ATT_DH = 16, 64
LANES = 128
ATT_PAIRS = ATT_HEADS * ATT_DH // LANES
BAND_PAST = 8 * CHUNK
MAX_REL = 256
REL_TABLE = MAX_REL + CHUNK
REL_PAD = 384
ATT_CQ = 256
ATT_KW = ATT_CQ + BAND_PAST
ATT_RING = ATT_CQ + ATT_KW
ATT_VARIANTS = BAND_PAST // ATT_CQ + 1
D_FF = 4 * D_MODEL
EPS = 1e-6
EPILOGUE_ROWS = 256
NEG = -1e30
N_DEV = 8
N_PEERS = N_DEV - 1

ADAM_LR, ADAM_B1, ADAM_B2, ADAM_EPS, ADAM_WD, ADAM_STEP = 0.001, 0.9, 0.999, 1e-08, 0.01, 10

VMEM_LIMIT = 56 * 1024 * 1024
MESH = pl.DeviceIdType.MESH
ANY = pl.BlockSpec(memory_space=pl.ANY)


def _dot(a, b):
    return jnp.dot(a, b, preferred_element_type=F32)


def _dot_nt(a, b):
    return lax.dot_general(a, b, (((1,), (1,)), ((), ())), preferred_element_type=F32)


def _dot_tn(a, b):
    return lax.dot_general(a, b, (((0,), (0,)), ((), ())), preferred_element_type=F32)


def _rms(x, g):
    r = lax.rsqrt(jnp.mean(x * x, axis=-1, keepdims=True) + EPS)
    return x * r * g


def _rms_bwd(dn, x, g):
    r = lax.rsqrt(jnp.mean(x * x, axis=-1, keepdims=True) + EPS)
    xh = x * r
    dg = jnp.sum(dn * xh, axis=0, keepdims=True)
    dxh = dn * g
    dx = r * (dxh - xh * jnp.mean(dxh * xh, axis=-1, keepdims=True))
    return dx, dg


def _rms_bwd_epilogue(x_ref, g_ref, dres_ref, dx_ref, dxb_ref, dg_ref):
    for r in range(0, dx_ref.shape[0], EPILOGUE_ROWS):
        rows = slice(r, r + EPILOGUE_ROWS)
        dx, dg = _rms_bwd(dx_ref[rows, :], x_ref[rows, :], g_ref[...])
        dx = dres_ref[rows, :] + dx
        dx_ref[rows, :] = dx
        dxb_ref[rows, :] = dx.astype(BF16)
        dg_ref[...] += dg


def _place():
    return lax.axis_index("x"), lax.axis_index("y"), lax.axis_index("c")


def _lin(p):
    return 4 * p[0] + 2 * p[1] + p[2]


def _cols(width):
    return lambda ref, blk: ref.at[:, pl.ds(pl.multiple_of(blk * width, LANES), width)]


def _rows(height):
    return lambda ref, blk: ref.at[pl.ds(pl.multiple_of(blk * height, 8), height), :]


def _slot(ref, blk):
    return ref.at[blk]


def _whole(ref, blk):
    return ref


class _Exchange:
    def __init__(self, sources, flows, land_shapes):
        self.sources, self.flows, self.land_shapes = list(sources), flows, list(land_shapes)
        n_f = len(flows)
        self.sem_shapes = [pltpu.SemaphoreType.DMA((N_PEERS * n_f,)), pltpu.SemaphoreType.DMA((N_PEERS * n_f,)),
                           pltpu.SemaphoreType.DMA((n_f,))]

    def _copies(self, srcs, lands, sems):
        send_sems, recv_sems, local_sems = sems
        x, y, c = _place()
        me = (x, y, c)
        peers = [(x ^ dx, y ^ dy, c ^ dc) for dx in (0, 1) for dy in (0, 1) for dc in (0, 1)][1:]

        def copy(f, k, sender, to):
            si, sview, li, lview = self.flows[f]
            return pltpu.make_async_remote_copy(
                src_ref=sview(srcs[si], _lin(to)), dst_ref=lview(lands[li], _lin(sender)),
                send_sem=send_sems.at[f * N_PEERS + k], recv_sem=recv_sems.at[f * N_PEERS + k],
                device_id=to, device_id_type=MESH)

        mine, sends, recvs = [], [], []
        for f, (si, sview, li, lview) in enumerate(self.flows):
            mine.append(pltpu.make_async_copy(sview(srcs[si], _lin(me)), lview(lands[li], _lin(me)), local_sems.at[f]))
            for k, peer in enumerate(peers):
                sends.append(copy(f, k, me, peer))
                recvs.append(copy(f, k, peer, me))
        return mine, sends, recvs

    def start(self, srcs, lands, sems):
        mine, sends, _ = self._copies(srcs, lands, sems)
        for cp in mine + sends:
            cp.start()

    def finish(self, srcs, lands, sems):
        mine, sends, recvs = self._copies(srcs, lands, sems)
        for cp in recvs:
            cp.wait_recv()
        for cp in sends:
            cp.wait_send()
        for cp in mine:
            cp.wait()

    def run(self, name):
        n_src, n_land = len(self.sources), len(self.land_shapes)

        def body(*refs):
            srcs, lands, sems = refs[:n_src], refs[n_src:n_src + n_land], refs[n_src + n_land:]
            self.start(srcs, lands, sems)
            self.finish(srcs, lands, sems)

        return pl.pallas_call(body, name=name, in_specs=[ANY] * n_src, out_specs=[ANY] * n_land,
                              out_shape=self.land_shapes, scratch_shapes=self.sem_shapes)(*self.sources)


def _call(body, *, name, grid, in_specs, out_specs, out_shape, args, scratch_shapes=(), ride=None):
    params = pltpu.CompilerParams(dimension_semantics=("arbitrary",) * len(grid), vmem_limit_bytes=VMEM_LIMIT)
    in_specs, out_specs, out_shape, scratch_shapes = list(in_specs), list(out_specs), list(out_shape), list(scratch_shapes)
    if ride is None:
        return pl.pallas_call(body, name=name, grid=grid, in_specs=in_specs, out_specs=out_specs, out_shape=out_shape,
                              scratch_shapes=scratch_shapes, compiler_params=params)(*args)
    n_in, n_out, n_scr = len(in_specs), len(out_specs), len(scratch_shapes)
    n_src, n_land = len(ride.sources), len(ride.land_shapes)

    def riding(*refs):
        bounds = np.cumsum([n_in, n_src, n_out, n_land, n_scr])
        ins, srcs, outs, lands, scr, sems = (refs[a:b] for a, b in zip([0, *bounds], [*bounds, len(refs)]))
        step = functools.reduce(lambda a, d: a * grid[d] + pl.program_id(d), range(len(grid)), 0)
        n_steps = int(np.prod(grid))

        @pl.when(step == 0)
        def _():
            ride.start(srcs, lands, sems)

        if hasattr(ride, "middle"):
            @pl.when(step == (5 * n_steps) // 8)
            def _():
                ride.middle(srcs, lands, sems)

        body(*ins, *outs, *scr)

        @pl.when(step == n_steps - 1)
        def _():
            ride.finish(srcs, lands, sems)

    res = pl.pallas_call(
        riding, name=name, grid=grid, in_specs=in_specs + [ANY] * n_src, out_specs=out_specs + [ANY] * n_land,
        out_shape=out_shape + ride.land_shapes, scratch_shapes=scratch_shapes + ride.sem_shapes,
        compiler_params=params)(*args, *ride.sources)
    return res[:n_out], res[n_out:]


class _ChipGather:
    def __init__(self, shards, out_shapes, out_of, views):
        self.sources, self.land_shapes, self.out_of, self.views = list(shards), list(out_shapes), out_of, views
        n = len(shards)
        self.sem_shapes = [pltpu.SemaphoreType.DMA((N_PEERS * n,)), pltpu.SemaphoreType.DMA((N_PEERS * n,)),
                           pltpu.SemaphoreType.DMA((n,))]

    def _copies(self, srcs, lands, sems):
        send_sems, recv_sems, local_sems = sems
        n = len(srcs)
        outs = [lands[self.out_of[a]] for a in range(n)]
        x, y, c = _place()
        me, sibling = (x, y, c), (x, y, 1 - c)
        chips = [(1 - x, y), (x, 1 - y), (1 - x, 1 - y)]

        def copy(a, k, block, to, own=False):
            dst = self.views[a](outs[a], _lin(block))
            return pltpu.make_async_remote_copy(
                src_ref=srcs[a] if own else dst, dst_ref=dst,
                send_sem=send_sems.at[a * N_PEERS + k], recv_sem=recv_sems.at[a * N_PEERS + k],
                device_id=to, device_id_type=MESH)

        mine = lambda: [pltpu.make_async_copy(srcs[a], self.views[a](outs[a], _lin(me)), local_sems.at[a]) for a in range(n)]
        first = lambda: [cp for a in range(n) for cp in
                         [copy(a, 0, me, sibling, own=True)] + [copy(a, 1 + j, me, (*chip, c), own=True) for j, chip in enumerate(chips)]]
        landed = lambda a, j: copy(a, 1 + j, (*chips[j], c), me)
        passed = lambda a, j: copy(a, 4 + j, (*chips[j], c), sibling)
        from_sibling = lambda a: [copy(a, 0, sibling, me)] + [copy(a, 4 + j, (*chips[j], 1 - c), me) for j in range(3)]
        return n, mine, first, landed, passed, from_sibling

    def start(self, srcs, lands, sems):
        _, mine, first, _, _, _ = self._copies(srcs, lands, sems)
        for cp in mine() + first():
            cp.start()

    def middle(self, srcs, lands, sems):
        n, _, _, landed, passed, _ = self._copies(srcs, lands, sems)
        for j in range(3):
            for a in range(n):
                landed(a, j).wait_recv()
                passed(a, j).start()

    def finish(self, srcs, lands, sems):
        n, mine, first, _, passed, from_sibling = self._copies(srcs, lands, sems)
        for a in range(n):
            for cp in from_sibling(a):
                cp.wait_recv()
        for cp in first() + [passed(a, j) for j in range(3) for a in range(n)]:
            cp.wait_send()
        for cp in mine():
            cp.wait()

    def run(self, name):
        n, n_out = len(self.sources), len(self.land_shapes)

        def body(*refs):
            parts = refs[:n], refs[n:n + n_out], refs[n + n_out:]
            self.start(*parts)
            self.middle(*parts)
            self.finish(*parts)

        return pl.pallas_call(body, name=name, in_specs=[ANY] * n, out_specs=[ANY] * n_out,
                              out_shape=self.land_shapes, scratch_shapes=self.sem_shapes)(*self.sources)


def _resident(shape):
    return pl.BlockSpec(shape, lambda i: (0,) * len(shape), pipeline_mode=pl.Buffered(1))


def _norm_proj(h, g, w, groups, name, tm, ride=None):
    T = h.shape[0]
    N = w.shape[1]
    width = N // groups

    def body(h_ref, g_ref, w_ref, o_ref, n_ref):
        n = _rms(h_ref[...], g_ref[...]).astype(BF16)
        n_ref[...] = n
        if groups == 1:
            o_ref[...] = _dot(n, w_ref[...]).astype(BF16)
        else:
            for p in range(groups):
                o_ref[p] = _dot(n, w_ref[:, p * width:(p + 1) * width]).astype(BF16)

    row = pl.BlockSpec((tm, D_MODEL), lambda i: (i, 0))
    if groups == 1:
        out_spec, out_shape = pl.BlockSpec((tm, N), lambda i: (i, 0)), jax.ShapeDtypeStruct((T, N), BF16)
    else:
        out_spec = pl.BlockSpec((groups, tm, width), lambda i: (0, i, 0))
        out_shape = jax.ShapeDtypeStruct((groups, T, width), BF16)
    return _call(body, name=name, grid=(T // tm,),
                 in_specs=[row, pl.BlockSpec((1, D_MODEL), lambda i: (0, 0)), _resident(w.shape)],
                 out_specs=[out_spec, row], out_shape=[out_shape, jax.ShapeDtypeStruct((T, D_MODEL), BF16)],
                 args=(h, g, w), ride=ride)


def _ret_proj(h, g, w, cos, sin, S, name, tm, ride=None):
    T = h.shape[0]
    per_seq = S // tm

    def body(h_ref, g_ref, w_ref, cos_ref, sin_ref, o_ref, n_ref):
        n = _rms(h_ref[...], g_ref[...]).astype(BF16)
        n_ref[...] = n
        cs, sn = cos_ref[...], sin_ref[...]
        for hd in range(RET_HEADS):
            c0 = hd * RET_HEAD_COLS
            out = _dot(n, w_ref[:, c0:c0 + RET_HEAD_COLS])
            o_ref[:, c0:c0 + RET_DK] = _rope(out[:, 0:RET_DK], cs, sn).astype(BF16)
            o_ref[:, c0 + RET_DK:c0 + 2 * RET_DK] = (_rope(out[:, RET_DK:2 * RET_DK], cs, sn) * (RET_DK ** -0.5)).astype(BF16)
            o_ref[:, c0 + 2 * RET_DK:c0 + RET_HEAD_COLS] = out[:, 2 * RET_DK:].astype(BF16)

    row = pl.BlockSpec((tm, D_MODEL), lambda i: (i, 0))
    trig = pl.BlockSpec((tm, RET_DK // 2), lambda i: (i % per_seq, 0))
    return _call(body, name=name, grid=(T // tm,),
                 in_specs=[row, pl.BlockSpec((1, D_MODEL), lambda i: (0, 0)), _resident(w.shape), trig, trig],
                 out_specs=[pl.BlockSpec((tm, w.shape[1]), lambda i: (i, 0)), row],
                 out_shape=[jax.ShapeDtypeStruct((T, w.shape[1]), BF16), jax.ShapeDtypeStruct((T, D_MODEL), BF16)],
                 args=(h, g, w, cos, sin), ride=ride)


def _proj_bwd(dy, w, x, g, dres, name, tm, ride=None):
    T = x.shape[0]
    groups = dy.shape[0] if dy.ndim == 3 else 1
    width = w.shape[1] // groups

    def body(dy_ref, w_ref, x_ref, g_ref, dres_ref, dx_ref, dxb_ref, dg_ref):
        @pl.when(pl.program_id(0) == 0)
        def _():
            dg_ref[...] = jnp.zeros_like(dg_ref)

        if groups == 1:
            dn = _dot_nt(dy_ref[...], w_ref[...])
        else:
            dn = sum(_dot_nt(dy_ref[p], w_ref[:, p * width:(p + 1) * width]) for p in range(groups))
        dx_ref[...] = dn
        _rms_bwd_epilogue(x_ref, g_ref, dres_ref, dx_ref, dxb_ref, dg_ref)

    row = pl.BlockSpec((tm, D_MODEL), lambda i: (i, 0))
    vec = pl.BlockSpec((1, D_MODEL), lambda i: (0, 0))
    dy_spec = (pl.BlockSpec((tm, w.shape[1]), lambda i: (i, 0)) if groups == 1
               else pl.BlockSpec((groups, tm, width), lambda i: (0, i, 0)))
    return _call(body, name=name, grid=(T // tm,),
                 in_specs=[dy_spec, _resident(w.shape), row, vec, row], out_specs=[row, row, vec],
                 out_shape=[jax.ShapeDtypeStruct((T, D_MODEL), F32), jax.ShapeDtypeStruct((T, D_MODEL), BF16),
                            jax.ShapeDtypeStruct((1, D_MODEL), F32)],
                 args=(dy, w, x, g, dres), ride=ride)


def _matmul_res(a, w, res, name, tm=1024):
    T, K = a.shape

    def body(a_ref, w_ref, r_ref, o_ref):
        o_ref[...] = r_ref[...] + _dot(a_ref[...], w_ref[...])

    row = pl.BlockSpec((tm, D_MODEL), lambda i: (i, 0))
    return _call(body, name=name, grid=(T // tm,),
                 in_specs=[pl.BlockSpec((tm, K), lambda i: (i, 0)), pl.BlockSpec((K, D_MODEL), lambda i: (0, 0)), row],
                 out_specs=[row], out_shape=[jax.ShapeDtypeStruct((T, D_MODEL), F32)], args=(a, w, res))[0]


def _matmul_nt(dy, w, name, tm=1024):
    T, N = dy.shape
    K = w.shape[0]

    def body(dy_ref, w_ref, o_ref):
        o_ref[...] = _dot_nt(dy_ref[...], w_ref[...]).astype(BF16)

    return _call(body, name=name, grid=(T // tm,),
                 in_specs=[pl.BlockSpec((tm, N), lambda i: (i, 0)), pl.BlockSpec((K, N), lambda i: (0, 0))],
                 out_specs=[pl.BlockSpec((tm, K), lambda i: (i, 0))], out_shape=[jax.ShapeDtypeStruct((T, K), BF16)],
                 args=(dy, w))[0]


def _wgrad(a, a_spec, b, b_specs, m, n, bm, bn, n_k, name, square_a=False, ride=None):
    b_specs = b_specs if isinstance(b_specs, (list, tuple)) else [b_specs]
    n_b = len(b_specs)

    def body(a_ref, *rest):
        b_refs = rest[:n_b]
        o_ref, acc = rest[n_b:]
        k = pl.program_id(2)

        @pl.when(k == 0)
        def _():
            acc[...] = jnp.zeros_like(acc)

        av = a_ref[...]
        if square_a:
            af = av.astype(F32)
            av = (af * af).astype(BF16)
        bv = b_refs[0][...] if n_b == 1 else jnp.concatenate([r[...] for r in b_refs], axis=1)
        acc[...] += _dot_tn(av, bv)

        @pl.when(k == n_k - 1)
        def _():
            o_ref[...] = acc[...].astype(BF16)

    res = _call(body, name=name, grid=(m // bm, n // bn, n_k), in_specs=[a_spec, *b_specs],
                out_specs=[pl.BlockSpec((bm, bn), lambda i, j, k: (i, j))], out_shape=[jax.ShapeDtypeStruct((m, n), BF16)],
                scratch_shapes=[pltpu.VMEM((bm, bn), F32)], args=(a, *([b] * n_b)), ride=ride)
    return res[0] if ride is None else (res[0][0], res[1])


def _mlp_fwd(h, g, w1, w2, name, tm=1024, tf=1024, ride=None):
    T = h.shape[0]

    def body(h_ref, g_ref, w1_ref, w2_ref, ho_ref, n_ref, u_ref):
        @pl.when(pl.program_id(1) == 0)
        def _():
            n_ref[...] = _rms(h_ref[...], g_ref[...]).astype(BF16)
            ho_ref[...] = h_ref[...]

        u = jnp.maximum(_dot(n_ref[...], w1_ref[...]), 0.0)
        u_ref[...] = u.astype(BF16)
        ho_ref[...] += _dot((u * u).astype(BF16), w2_ref[...])

    row = pl.BlockSpec((tm, D_MODEL), lambda i, j: (i, 0))
    return _call(body, name=name, grid=(T // tm, D_FF // tf),
                 in_specs=[row, pl.BlockSpec((1, D_MODEL), lambda i, j: (0, 0)),
                           pl.BlockSpec((D_MODEL, tf), lambda i, j: (0, j)), pl.BlockSpec((tf, D_MODEL), lambda i, j: (j, 0))],
                 out_specs=[row, row, pl.BlockSpec((tm, tf), lambda i, j: (i, j))],
                 out_shape=[jax.ShapeDtypeStruct((T, D_MODEL), F32), jax.ShapeDtypeStruct((T, D_MODEL), BF16),
                            jax.ShapeDtypeStruct((T, D_FF), BF16)],
                 args=(h, g, w1, w2), ride=ride)


def _mlp_bwd(dh, u, w1, w2, h, g, name, tm=1024, tf=1024, ride=None):
    T = h.shape[0]
    n_j = D_FF // tf

    def body(dh_ref, u_ref, w1_ref, w2_ref, h_ref, g_ref, dx_ref, dxb_ref, du_ref, dg_ref, dhb):
        acc = dx_ref
        i, j = pl.program_id(0), pl.program_id(1)

        @pl.when(j == 0)
        def _():
            dhb[...] = dh_ref[...].astype(BF16)
            acc[...] = jnp.zeros_like(acc)

        @pl.when((i == 0) & (j == 0))
        def _():
            dg_ref[...] = jnp.zeros_like(dg_ref)

        da = _dot_nt(dhb[...], w2_ref[...])
        du = (da * (2.0 * u_ref[...].astype(F32))).astype(BF16)
        du_ref[...] = du
        acc[...] += _dot_nt(du, w1_ref[...])

        @pl.when(j == n_j - 1)
        def _():
            _rms_bwd_epilogue(h_ref, g_ref, dh_ref, dx_ref, dxb_ref, dg_ref)

    row = pl.BlockSpec((tm, D_MODEL), lambda i, j: (i, 0))
    vec = pl.BlockSpec((1, D_MODEL), lambda i, j: (0, 0))
    hid = pl.BlockSpec((tm, tf), lambda i, j: (i, j))
    return _call(body, name=name, grid=(T // tm, n_j),
                 in_specs=[row, hid, pl.BlockSpec((D_MODEL, tf), lambda i, j: (0, j)),
                           pl.BlockSpec((tf, D_MODEL), lambda i, j: (j, 0)), row, vec],
                 out_specs=[row, row, hid, vec],
                 out_shape=[jax.ShapeDtypeStruct((T, D_MODEL), F32), jax.ShapeDtypeStruct((T, D_MODEL), BF16),
                            jax.ShapeDtypeStruct((T, D_FF), BF16), jax.ShapeDtypeStruct((1, D_MODEL), F32)],
                 scratch_shapes=[pltpu.VMEM((tm, D_MODEL), BF16)], args=(dh, u, w1, w2, h, g), ride=ride)


def _final_loss(h, g, target, name, tm=512):
    T = h.shape[0]

    def body(h_ref, g_ref, t_ref, dh_ref, dhb_ref, loss_ref, dg_ref):
        @pl.when(pl.program_id(0) == 0)
        def _():
            loss_ref[...] = jnp.zeros_like(loss_ref)
            dg_ref[...] = jnp.zeros_like(dg_ref)

        x = h_ref[...]
        gg = g_ref[...]
        r = lax.rsqrt(jnp.mean(x * x, axis=-1, keepdims=True) + EPS)
        xh = x * r
        e = xh * gg - t_ref[...]
        per_tok = jnp.mean(e * e, axis=-1, keepdims=True)
        loss_ref[...] += 0.5 * jnp.sum(per_tok, axis=0, keepdims=True)
        dy = e * (1.0 / D_MODEL)
        dg_ref[...] += jnp.sum(dy * xh, axis=0, keepdims=True)
        dxh = dy * gg
        dx = r * (dxh - xh * jnp.mean(dxh * xh, axis=-1, keepdims=True))
        dh_ref[...] = dx
        dhb_ref[...] = dx.astype(BF16)

    row = pl.BlockSpec((tm, D_MODEL), lambda i: (i, 0))
    vec = pl.BlockSpec((1, D_MODEL), lambda i: (0, 0))
    return _call(body, name=name, grid=(T // tm,), in_specs=[row, vec, row],
                 out_specs=[row, row, pl.BlockSpec((8, LANES), lambda i: (0, 0)), vec],
                 out_shape=[jax.ShapeDtypeStruct((T, D_MODEL), F32), jax.ShapeDtypeStruct((T, D_MODEL), BF16),
                            jax.ShapeDtypeStruct((8, LANES), F32), jax.ShapeDtypeStruct((1, D_MODEL), F32)],
                 args=(h, g, target))


def _ret_constants(S):
    log_gamma = jnp.log1p(-jnp.exp2(-5.0 - jnp.arange(RET_HEADS, dtype=F32)))
    idx = jnp.arange(RET_SC, dtype=F32)
    i, j = idx[:, None], idx[None, :]
    same_chunk = jnp.floor(i / CHUNK) == jnp.floor(j / CHUNK)
    mask = jnp.where((j <= i) | same_chunk, jnp.exp(log_gamma[:, None, None] * jnp.abs(i - j)[None]), 0.0)
    qdec = jnp.exp(log_gamma[:, None] * (idx + 1.0)[None, :])[:, :, None]
    kdec = jnp.exp(log_gamma[:, None] * (RET_SC - 1 - idx)[None, :])[:, :, None]
    cdec = jnp.exp(log_gamma * RET_SC)[:, None, None]
    half = RET_DK // 2
    inv = jnp.exp(-jnp.log(ROPE_BASE) * jnp.arange(half, dtype=F32) / half)
    ang = jnp.arange(S, dtype=F32)[:, None] * inv[None, :]
    return jnp.cos(ang), jnp.sin(ang), mask.astype(F32), qdec, kdec, cdec


def _rope(t, cs, sn):
    t1, t2 = t[:, :RET_DK // 2], t[:, RET_DK // 2:]
    return jnp.concatenate([t1 * cs - t2 * sn, t1 * sn + t2 * cs], axis=-1)


def _rope_bwd(d, cs, sn):
    d1, d2 = d[:, :RET_DK // 2], d[:, RET_DK // 2:]
    return jnp.concatenate([d1 * cs + d2 * sn, d2 * cs - d1 * sn], axis=-1)


def _ret_specs(B, S, reverse):
    n_sc = S // RET_SC

    def cc(c):
        return n_sc - 1 - c if reverse else c

    return dict(
        proj=pl.BlockSpec((B, RET_SC, RET_HEAD_COLS), lambda h, c: (0, cc(c), h)),
        trig=pl.BlockSpec((RET_SC, RET_DK // 2), lambda h, c: (cc(c), 0)),
        mask=pl.BlockSpec((None, RET_SC, RET_SC), lambda h, c: (h, 0, 0)),
        dec=pl.BlockSpec((None, RET_SC, 1), lambda h, c: (h, 0, 0)),
        cdec=pl.BlockSpec((None, 1, 1), lambda h, c: (h, 0, 0)),
        gn=pl.BlockSpec((1, RET_DV), lambda h, c: (0, h)),
        val=pl.BlockSpec((B, RET_SC, RET_DV), lambda h, c: (0, cc(c), h)),
        state=pl.BlockSpec((B, None, None, RET_DK, RET_DV), lambda h, c: (0, h, cc(c), 0, 0)),
    )


def _ret_qkvg(p_ref, kdec):
    qb = p_ref[:, 0:RET_DK]
    kb = p_ref[:, RET_DK:2 * RET_DK]
    kt = (kb.astype(F32) * kdec).astype(BF16)
    v = p_ref[:, 2 * RET_DK:2 * RET_DK + RET_DV]
    gate = p_ref[:, 2 * RET_DK + RET_DV:RET_HEAD_COLS].astype(F32)
    return qb, kb, kt, v, gate


def _group_norm(o):
    mu = jnp.mean(o, axis=-1, keepdims=True)
    oc = o - mu
    rstd = lax.rsqrt(jnp.mean(oc * oc, axis=-1, keepdims=True) + EPS)
    return oc * rstd, rstd


def _ret_fwd(proj, consts, gn, B, S, name, ride=None):
    T = B * S
    n_sc = S // RET_SC
    sp = _ret_specs(B, S, False)

    def body(p_ref, m_ref, qd_ref, kd_ref, cd_ref, gn_ref, y_ref, o_ref, st_ref, state):
        @pl.when(pl.program_id(1) == 0)
        def _():
            state[...] = jnp.zeros_like(state)

        seqs = range(B)
        qkvg = [_ret_qkvg(p_ref.at[b], kd_ref[...]) for b in seqs]
        qb, kb, kt, v = ([qkvg[b][i] for b in seqs] for i in range(4))
        sc = [_dot_nt(qb[b], kb[b]) for b in seqs]
        sb = [state[b].astype(BF16) for b in seqs]
        cross = [_dot(qb[b], sb[b]) for b in seqs]
        for b in seqs:
            st_ref[b] = sb[b]
        p = [(sc[b] * m_ref[...]).astype(BF16) for b in seqs]
        o = [_dot(p[b], v[b]) + cross[b] * qd_ref[...] for b in seqs]
        upd = [_dot_tn(kt[b], v[b]) for b in seqs]
        for b in seqs:
            o_ref[b] = o[b].astype(BF16)
            ohat, _ = _group_norm(o[b])
            gate = qkvg[b][4]
            y_ref[b] =(gate * jax.nn.sigmoid(gate) * (ohat * gn_ref[...])).astype(BF16)
            state[b] = state[b] * cd_ref[...] + upd[b]

    res = _call(
        body, name=name, grid=(RET_HEADS, n_sc),
        in_specs=[sp["proj"], sp["mask"], sp["dec"], sp["dec"], sp["cdec"], sp["gn"]],
        out_specs=[sp["val"], sp["val"], sp["state"]],
        out_shape=[jax.ShapeDtypeStruct((B, S, RET_VW), BF16), jax.ShapeDtypeStruct((B, S, RET_VW), BF16),
                   jax.ShapeDtypeStruct((B, RET_HEADS, n_sc, RET_DK, RET_DV), BF16)],
        scratch_shapes=[pltpu.VMEM((B, RET_DK, RET_DV), F32)],
        args=(proj.reshape(B, S, -1), *consts[2:], gn), ride=ride)
    (y, o, states), lands = res if ride is not None else (res, None)
    outs = [y.reshape(T, RET_VW), o.reshape(T, RET_VW), states]
    return outs if ride is None else (outs, lands)


def _ret_bwd(proj, consts, gn, o, states, dy, B, S, name, ride=None):
    T = B * S
    n_sc = S // RET_SC
    sp = _ret_specs(B, S, True)

    def body(p_ref, cos_ref, sin_ref, m_ref, qd_ref, kd_ref, cd_ref, gn_ref, o_ref, st_ref, dy_ref,
             dp_ref, dgn_ref, dstate):
        @pl.when(pl.program_id(1) == 0)
        def _():
            dstate[...] = jnp.zeros_like(dstate)
            dgn_ref[...] = jnp.zeros_like(dgn_ref)

        seqs = range(B)
        cs, sn = cos_ref[...], sin_ref[...]
        m, gnv = m_ref[...], gn_ref[...]
        qkvg = [_ret_qkvg(p_ref.at[b], kd_ref[...]) for b in seqs]
        qb, kb, kt, v = ([qkvg[b][i] for b in seqs] for i in range(4))
        sc = [_dot_nt(qb[b], kb[b]) for b in seqs]
        dsb = [dstate[b].astype(BF16) for b in seqs]
        dv_state = [_dot(kt[b], dsb[b]) for b in seqs]
        dkt = [_dot_nt(v[b], dsb[b]) for b in seqs]
        do, dgate = [], []
        for b in seqs:
            gate = qkvg[b][4]
            ohat, rstd = _group_norm(o_ref[b].astype(F32))
            dyv = dy_ref[b].astype(F32)
            sg = jax.nn.sigmoid(gate)
            don = dyv * (gate * sg)
            dgate.append(dyv * (ohat * gnv) * (sg * (1.0 + gate * (1.0 - sg))))
            dgn_ref[...] += jnp.sum(don * ohat, axis=0, keepdims=True)
            dohat = don * gnv
            do.append(rstd * (dohat - jnp.mean(dohat, axis=-1, keepdims=True)
                              - ohat * jnp.mean(dohat * ohat, axis=-1, keepdims=True)))
        dob = [do[b].astype(BF16) for b in seqs]
        doq = [(do[b] * qd_ref[...]).astype(BF16) for b in seqs]
        dsc_f = [_dot_nt(dob[b], v[b]) for b in seqs]
        dq_state = [_dot_nt(doq[b], st_ref[b]) for b in seqs]
        dstate_upd = [_dot_tn(qb[b], doq[b]) for b in seqs]
        p = [(sc[b] * m).astype(BF16) for b in seqs]
        dsc = [(dsc_f[b] * m).astype(BF16) for b in seqs]
        dv = [_dot_tn(p[b], dob[b]) + dv_state[b] for b in seqs]
        dq = [_dot(dsc[b], kb[b]) + dq_state[b] for b in seqs]
        dk = [(_dot_tn(dsc[b], qb[b]) + dkt[b] * kd_ref[...]) * (RET_DK ** -0.5) for b in seqs]
        for b in seqs:
            dstate[b] = dstate[b] * cd_ref[...] + dstate_upd[b]
            dp_ref[b, :, 0:RET_DK] = _rope_bwd(dq[b], cs, sn).astype(BF16)
            dp_ref[b, :, RET_DK:2 * RET_DK] = _rope_bwd(dk[b], cs, sn).astype(BF16)
            dp_ref[b, :, 2 * RET_DK:2 * RET_DK + RET_DV] = dv[b].astype(BF16)
            dp_ref[b, :, 2 * RET_DK + RET_DV:RET_HEAD_COLS] = dgate[b].astype(BF16)

    res = _call(
        body, name=name, grid=(RET_HEADS, n_sc),
        in_specs=[sp["proj"], sp["trig"], sp["trig"], sp["mask"], sp["dec"], sp["dec"], sp["cdec"], sp["gn"],
                  sp["val"], sp["state"], sp["val"]],
        out_specs=[sp["proj"], sp["gn"]],
        out_shape=[jax.ShapeDtypeStruct((B, S, RET_HEADS * RET_HEAD_COLS), BF16), jax.ShapeDtypeStruct((1, RET_VW), F32)],
        scratch_shapes=[pltpu.VMEM((B, RET_DK, RET_DV), F32)],
        args=(proj.reshape(B, S, -1), *consts, gn, o.reshape(B, S, -1), states, dy.reshape(B, S, -1)), ride=ride)
    (dproj, dgn), lands = res if ride is not None else (res, None)
    outs = [dproj.reshape(T, -1), dgn]
    return outs if ride is None else (outs, lands)


def _ring_index():
    u = np.arange(ATT_RING)
    offset = np.where(u < ATT_KW, u, u - ATT_RING)
    return np.clip(offset - BAND_PAST, -MAX_REL, CHUNK - 1) + MAX_REL


def _bias_ring(rel):
    n_clip = BAND_PAST - MAX_REL
    n_hi = ATT_KW - n_clip - REL_TABLE
    ring = jnp.concatenate([jnp.broadcast_to(rel[:, :1], (ATT_HEADS, n_clip)), rel,
                            jnp.broadcast_to(rel[:, -1:], (ATT_HEADS, n_hi)),
                            jnp.broadcast_to(rel[:, :1], (ATT_HEADS, ATT_CQ))], axis=1)
    return jnp.broadcast_to(ring.reshape(ATT_PAIRS, 2, 1, ATT_RING), (ATT_PAIRS, 2, 8, ATT_RING))


def _band_masks():
    i = np.arange(ATT_CQ)[:, None]
    j = np.arange(ATT_KW)[None, :]
    lo = CHUNK * (i // CHUNK)
    band = np.where((j >= lo) & (j < lo + BAND_PAST + CHUNK), 0.0, NEG)
    return np.stack([band + np.where(j + v * ATT_CQ >= BAND_PAST, 0.0, NEG)
                     for v in range(ATT_VARIANTS)]).astype(np.float32)


def _attn_bias(bias_scr, ring_ref, band_ref):
    for hh in range(2):
        rows = jnp.broadcast_to(ring_ref[hh, 0:1, :], (ATT_CQ, ATT_RING))
        toeplitz = pltpu.roll(rows, 0, 1, stride=1, stride_axis=0)[:, :ATT_KW]
        for v in range(ATT_VARIANTS):
            bias_scr[v, hh] = toeplitz + band_ref[v]


ATT_STRIP = 32
ATT_SCALE = ATT_DH ** -0.5


def _strips(fn):
    def strip(r, carry):
        fn(pl.ds(pl.multiple_of(r * ATT_STRIP, ATT_STRIP), ATT_STRIP))
        return carry

    lax.fori_loop(0, ATT_CQ // ATT_STRIP, strip, 0, unroll=True)


def _attn_prepare(kpad, vpad, qm, qkv_ref):
    kpad[0:BAND_PAST, :] = jnp.zeros((BAND_PAST, LANES), BF16)
    vpad[0:BAND_PAST, :] = jnp.zeros((BAND_PAST, LANES), BF16)
    kpad[BAND_PAST:, :] = qkv_ref[1]
    vpad[BAND_PAST:, :] = qkv_ref[2]
    lane = lax.broadcasted_iota(jnp.int32, (1, LANES), 1)
    q = qkv_ref[0] * ATT_SCALE
    for hh in range(2):
        qm[hh] = jnp.where((lane >= ATT_DH * hh) & (lane < ATT_DH * (hh + 1)), q, jnp.zeros_like(q))


def _attn_scores(s_ref, qm, kpad, b_ref, t, n_qb):
    t = jnp.minimum(t, n_qb - 1)
    qs = pl.multiple_of(t * ATT_CQ, ATT_CQ)
    variant = jnp.minimum(t, ATT_VARIANTS - 1)
    kw = kpad[pl.ds(qs, ATT_KW), :]
    for hh in range(2):
        s_ref[hh] = _dot_nt(qm[hh, pl.ds(qs, ATT_CQ), :], kw) + b_ref[variant, hh]


def _attn_softmax(s_ref, e_ref, linv_ref, m_ref=None):
    def strip(rows):
        s = s_ref[rows, :]
        m = jnp.max(s, axis=-1, keepdims=True)
        e = jnp.exp(s - m)
        e_ref[rows, :] = e.astype(BF16)
        linv_ref[rows, :] = jnp.broadcast_to(1.0 / jnp.sum(e, axis=-1, keepdims=True), (ATT_STRIP, LANES))
        if m_ref is not None:
            m_ref[rows, :] = jnp.broadcast_to(m, (ATT_STRIP, LANES))

    _strips(strip)


_RING_SPEC = pl.BlockSpec((None, 2, 8, ATT_RING), lambda hp, b: (hp, 0, 0, 0))
_BAND_SPEC = pl.BlockSpec((ATT_VARIANTS, ATT_CQ, ATT_KW), lambda hp, b: (0, 0, 0))


def _attn_fwd(qkv3, ring, B, S, name, ride=None):
    T = B * S
    n_qb = S // ATT_CQ

    def body(qkv_ref, ring_ref, band_ref, o_ref, kpad, vpad, qm, b_ref, s_scr, e_scr, linv_scr):
        @pl.when(pl.program_id(1) == 0)
        def _():
            _attn_bias(b_ref, ring_ref, band_ref)

        _attn_prepare(kpad, vpad, qm, qkv_ref)
        e_scr[...] = jnp.zeros_like(e_scr)
        linv_scr[...] = jnp.zeros_like(linv_scr)
        lane = lax.broadcasted_iota(jnp.int32, (1, LANES), 1)

        def softmax(slot):
            for hh in range(2):
                _attn_softmax(s_scr.at[slot, hh], e_scr.at[slot, hh], linv_scr.at[slot, hh])

        def output(t, slot):
            qs = pl.multiple_of(jnp.maximum(t, 0) * ATT_CQ, ATT_CQ)
            vw = vpad[pl.ds(qs, ATT_KW), :]
            outs = [_dot(e_scr[slot, hh], vw) * linv_scr[slot, hh] for hh in range(2)]
            o_ref[pl.ds(qs, ATT_CQ), :] = jnp.where(lane < ATT_DH, outs[0], outs[1]).astype(BF16)

        def pair(u, carry):
            t = 2 * u
            _attn_scores(s_scr.at[1], qm, kpad, b_ref, t + 1, n_qb)
            softmax(0)
            output(t - 1, 1)
            _attn_scores(s_scr.at[0], qm, kpad, b_ref, t + 2, n_qb)
            softmax(1)
            output(t, 0)
            return carry

        _attn_scores(s_scr.at[0], qm, kpad, b_ref, 0, n_qb)
        lax.fori_loop(0, n_qb // 2, pair, 0)
        output(n_qb - 1, 1)

    return _call(body, name=name, grid=(ATT_PAIRS, B),
                 in_specs=[pl.BlockSpec((3, S, LANES), lambda hp, b: (0, b, hp)), _RING_SPEC, _BAND_SPEC],
                 out_specs=[pl.BlockSpec((S, LANES), lambda hp, b: (b, hp))],
                 out_shape=[jax.ShapeDtypeStruct((T, D_MODEL), BF16)],
                 scratch_shapes=[pltpu.VMEM((S + BAND_PAST, LANES), BF16), pltpu.VMEM((S + BAND_PAST, LANES), BF16),
                                 pltpu.VMEM((2, S, LANES), BF16), pltpu.VMEM((ATT_VARIANTS, 2, ATT_CQ, ATT_KW), F32),
                                 pltpu.VMEM((2, 2, ATT_CQ, ATT_KW), F32),
                                 pltpu.VMEM((2, 2, ATT_CQ, ATT_KW), BF16), pltpu.VMEM((2, 2, ATT_CQ, LANES), F32)],
                 args=(qkv3, ring, jnp.asarray(_band_masks())), ride=ride)


def _attn_bwd(qkv3, ring, do, B, S, name, ride=None):
    T = B * S
    n_qb = S // ATT_CQ

    def body(qkv_ref, ring_ref, band_ref, do_ref, dqkv_ref, dring_ref, kpad, vpad, qm, dkacc, dvacc, b_ref, db_ref):
        @pl.when(pl.program_id(1) == 0)
        def _():
            _attn_bias(b_ref, ring_ref, band_ref)
            db_ref[...] = jnp.zeros_like(db_ref)

        _attn_prepare(kpad, vpad, qm, qkv_ref)
        dkacc[...] = jnp.zeros_like(dkacc)
        dvacc[...] = jnp.zeros_like(dvacc)
        lane = lax.broadcasted_iota(jnp.int32, (1, LANES), 1)

        def step(qb, carry):
            qs = pl.multiple_of(qb * ATT_CQ, ATT_CQ)
            variant = jnp.minimum(qb, ATT_VARIANTS - 1)
            dov = do_ref[pl.ds(qs, ATT_CQ), :].astype(F32)
            kw = kpad[pl.ds(qs, ATT_KW), :]
            vw = vpad[pl.ds(qs, ATT_KW), :]
            heads = (0, 1)
            qmh = [qm[hh, pl.ds(qs, ATT_CQ), :] for hh in heads]
            s = [_dot_nt(qmh[hh], kw) + b_ref[variant, hh] for hh in heads]
            e, linv, dom, dp = [None, None], [None, None], [None, None], [None, None]
            for hh in heads:
                e[hh] = jnp.exp(s[hh] - jnp.max(s[hh], axis=-1, keepdims=True))
                linv[hh] = 1.0 / jnp.sum(e[hh], axis=-1, keepdims=True)
                sel = (lane >= ATT_DH * hh) & (lane < ATT_DH * (hh + 1))
                dom[hh] = jnp.where(sel, dov * linv[hh], 0.0).astype(BF16)
                dp[hh] = _dot_nt(dom[hh], vw)
            dqs, dk, dv = [], None, None
            for hh in heads:
                ds = e[hh] * (dp[hh] - jnp.sum(dp[hh] * e[hh], axis=-1, keepdims=True) * linv[hh])
                db_ref[hh] += ds
                dsb = ds.astype(BF16)
                dqs.append(_dot(dsb, kw) * ATT_SCALE)
                dk_h = _dot_tn(qmh[hh], dsb)
                dv_h = _dot_tn(dom[hh], e[hh].astype(BF16))
                dk = dk_h if dk is None else dk + dk_h
                dv = dv_h if dv is None else dv + dv_h
            dqkv_ref[0, pl.ds(qs, ATT_CQ), :] = jnp.where(lane < ATT_DH, dqs[0], dqs[1]).astype(BF16)
            dkacc[:, pl.ds(qs, ATT_KW)] += dk
            dvacc[:, pl.ds(qs, ATT_KW)] += dv
            return carry

        lax.fori_loop(0, n_qb, step, 0, unroll=4)
        dqkv_ref[1] = dkacc[:, BAND_PAST:].T.astype(BF16)
        dqkv_ref[2] = dvacc[:, BAND_PAST:].T.astype(BF16)

        @pl.when(pl.program_id(1) == B - 1)
        def _():
            r = lax.broadcasted_iota(jnp.int32, (ATT_CQ, ATT_CQ), 0)
            c = lax.broadcasted_iota(jnp.int32, (ATT_CQ, ATT_CQ), 1)
            reverse = jnp.where(r + c == ATT_CQ - 1, 1.0, 0.0).astype(BF16)
            for hh in range(2):
                x = jnp.concatenate([db_ref[hh], jnp.zeros((ATT_CQ, ATT_RING - ATT_KW), F32)], axis=1)
                flipped = jnp.zeros((ATT_CQ, ATT_RING), F32)
                for _ in range(3):
                    part = x.astype(BF16)
                    flipped = flipped + _dot(reverse, part)
                    x = x - part.astype(F32)
                aligned = pltpu.roll(flipped, ATT_KW + 1, 1, stride=1, stride_axis=0)
                dring_ref[hh] = jnp.sum(aligned.reshape(ATT_CQ // 8, 8, ATT_RING), axis=0)

    qkv_spec = pl.BlockSpec((3, S, LANES), lambda hp, b: (0, b, hp))
    return _call(body, name=name, grid=(ATT_PAIRS, B),
                 in_specs=[qkv_spec, _RING_SPEC, _BAND_SPEC, pl.BlockSpec((S, LANES), lambda hp, b: (b, hp))],
                 out_specs=[qkv_spec, _RING_SPEC],
                 out_shape=[jax.ShapeDtypeStruct((3, T, D_MODEL), BF16),
                            jax.ShapeDtypeStruct((ATT_PAIRS, 2, 8, ATT_RING), F32)],
                 scratch_shapes=[pltpu.VMEM((S + BAND_PAST, LANES), BF16), pltpu.VMEM((S + BAND_PAST, LANES), BF16),
                                 pltpu.VMEM((2, S, LANES), BF16),
                                 pltpu.VMEM((LANES, S + BAND_PAST), F32), pltpu.VMEM((LANES, S + BAND_PAST), F32),
                                 pltpu.VMEM((ATT_VARIANTS, 2, ATT_CQ, ATT_KW), F32), pltpu.VMEM((2, ATT_CQ, ATT_KW), F32)],
                 args=(qkv3, ring, jnp.asarray(_band_masks()), do), ride=ride)


def _rel_bias_grad(dring, name):
    fold = np.zeros((ATT_RING, REL_PAD), np.float32)
    fold[np.arange(ATT_RING), _ring_index()] = 1.0
    fold = jnp.asarray(fold, BF16)

    def body(d_ref, f_ref, o_ref):
        x = jnp.sum(d_ref[...], axis=0, keepdims=True)
        x = jnp.broadcast_to(x, (8, ATT_RING))
        acc = jnp.zeros((8, REL_PAD), F32)
        for _ in range(3):
            part = x.astype(BF16)
            acc = acc + _dot(part, f_ref[...])
            x = x - part.astype(F32)
        o_ref[...] = acc[0:1, :]

    out = _call(body, name=name, grid=(ATT_HEADS,),
                in_specs=[pl.BlockSpec((None, 8, ATT_RING), lambda h: (h, 0, 0)),
                          pl.BlockSpec((ATT_RING, REL_PAD), lambda h: (0, 0))],
                out_specs=[pl.BlockSpec((None, 1, REL_PAD), lambda h: (h, 0, 0))],
                out_shape=[jax.ShapeDtypeStruct((ATT_HEADS, 1, REL_PAD), F32)],
                args=(dring.reshape(ATT_HEADS, 8, ATT_RING), fold))[0]
    return out.reshape(ATT_HEADS, REL_PAD)


def _adamw(w, g, m, v):
    m = ADAM_B1 * m + (1.0 - ADAM_B1) * g
    v = ADAM_B2 * v + (1.0 - ADAM_B2) * (g * g)
    m_hat = m / (1.0 - ADAM_B1 ** ADAM_STEP)
    v_hat = v / (1.0 - ADAM_B2 ** ADAM_STEP)
    delta = -ADAM_LR * (m_hat / (jnp.sqrt(v_hat) + ADAM_EPS) + ADAM_WD * w)
    return delta, m, v


def _sum_devices(ref):
    g = ref[0].astype(F32)
    for d in range(1, N_DEV):
        g = g + ref[d].astype(F32)
    return g


def _adamw_reduce(lands, w, m, v, name, tr=256):
    L, R, C = w.shape
    tr = min(tr, R)
    n_i = R // tr

    def body(*refs):
        l_refs = refs[:L]
        w_ref, m_ref, v_ref, g_out, d_out, m_out, v_out = refs[L:]
        layer = pl.program_id(0)
        for l in range(L):
            @pl.when(layer == l)
            def _(l=l):
                g = _sum_devices(l_refs[l])
                g_out[...] = g
                d_out[...], m_out[...], v_out[...] = _adamw(w_ref[...], g, m_ref[...], v_ref[...])

    def land_spec(l):
        return pl.BlockSpec((N_DEV, tr, C), lambda ly, i: (0, jnp.where(ly == l, i, jnp.where(ly < l, 0, n_i - 1)), 0))

    blk = pl.BlockSpec((None, tr, C), lambda ly, i: (ly, i, 0))
    return _call(body, name=name, grid=(L, n_i), in_specs=[land_spec(l) for l in range(L)] + [blk, blk, blk],
                 out_specs=[blk] * 4, out_shape=[jax.ShapeDtypeStruct((L, R, C), F32)] * 4, args=(*lands, w, m, v))


def _small_update(land_small, land_rel, w, m, v, name):
    def body(ls_ref, lr_ref, w_ref, m_ref, v_ref, g_out, d_out, m_out, v_out, rel_out):
        g = _sum_devices(ls_ref)
        g_out[...] = g
        d_out[...], m_out[...], v_out[...] = _adamw(w_ref[...], g, m_ref[...], v_ref[...])
        rel_out[...] = _sum_devices(lr_ref)

    return pl.pallas_call(
        body, name=name,
        out_shape=[jax.ShapeDtypeStruct(w.shape, F32)] * 4 + [jax.ShapeDtypeStruct(land_rel.shape[1:], F32)],
    )(land_small, land_rel, w, m, v)


def _adamw_plain(w, g, m, v, name):
    def body(w_ref, g_ref, m_ref, v_ref, d_out, m_out, v_out):
        d_out[...], m_out[...], v_out[...] = _adamw(w_ref[...], g_ref[...], m_ref[...], v_ref[...])

    return pl.pallas_call(body, name=name, out_shape=[jax.ShapeDtypeStruct(w.shape, F32)] * 3)(w, g, m, v)


def _ret_piece_of_column_block(p):
    per_head = RET_HEAD_COLS // RET_DK
    qk = jnp.where(p < RET_HEADS, per_head * p, per_head * (p - RET_HEADS) + 1)
    pv = p - 2 * RET_HEADS
    vv = per_head * (pv // 2) + 2 + pv % 2
    pg = p - 4 * RET_HEADS
    gg = per_head * (pg // 2) + 4 + pg % 2
    return jnp.where(p < 2 * RET_HEADS, qk, jnp.where(p < 4 * RET_HEADS, vv, gg))


def _step(x, target, shards, mix_g, gn_g, rel_shard, mlp_g, final_g):
    s_rin, s_rout, s_ain, s_aout, s_w1, s_w2 = shards
    B, S, _ = x.shape
    T = B * S
    tk = min(T, 2048)
    n_k = T // tk
    h0 = x.reshape(T, D_MODEL)
    tgt = target.reshape(T, D_MODEL)
    consts = _ret_constants(S)
    w1_cols = w2_rows = D_FF // N_DEV

    rel_cols = REL_TABLE // N_DEV
    per_shard = RET_IN // N_DEV // RET_DK

    def head_layout(r):
        return lambda ref, blk: ref.at[:, pl.ds(pl.multiple_of(
            _ret_piece_of_column_block(per_shard * blk + r) * RET_DK, LANES), RET_DK)]

    w_rin_heads, rel_all = _ChipGather(
        [s_rin[:, r * RET_DK:(r + 1) * RET_DK] for r in range(per_shard)] + [rel_shard],
        [jax.ShapeDtypeStruct((D_MODEL, RET_IN), BF16), jax.ShapeDtypeStruct((N_DEV, ATT_HEADS, rel_cols), F32)],
        [0] * per_shard + [1], [head_layout(r) for r in range(per_shard)] + [_slot]).run("gather_first")
    ring = _bias_ring(rel_all.transpose(1, 0, 2).reshape(ATT_HEADS, REL_TABLE))

    def gather(*items):
        return _ChipGather([s for s, _, _ in items], [jax.ShapeDtypeStruct(full, BF16) for _, _, full in items],
                           list(range(len(items))), [view for _, view, _ in items])

    def scatter(*items):
        return _Exchange([p for p, _, _ in items], [(a, view, a, _slot) for a, (_, view, _) in enumerate(items)],
                         [jax.ShapeDtypeStruct((N_DEV, *shard), BF16) for _, _, shard in items])

    w1_full, w2_full = (D_MODEL, D_FF), (D_FF, D_MODEL)
    w1_shard, w2_shard = (D_MODEL, w1_cols), (w2_rows, D_MODEL)

    ride = gather((s_rout, _rows(RET_VW // N_DEV), (RET_VW, D_MODEL)), (s_w1[0], _cols(w1_cols), w1_full))
    (proj, n0), (w_rout, w1_0) = _ret_proj(h0, mix_g[0:1], w_rin_heads, consts[0], consts[1], S, "ret_proj", 512, ride=ride)
    (y, o, states), (w2_0,) = _ret_fwd(proj, consts, gn_g, B, S, "ret_fwd", ride=gather((s_w2[0], _rows(w2_rows), w2_full)))
    h1 = _matmul_res(y, w_rout, h0, "ret_out")
    ride = gather((s_ain, _cols(3 * D_MODEL // N_DEV), (D_MODEL, 3 * D_MODEL)),
                  (s_aout, _rows(D_MODEL // N_DEV), (D_MODEL, D_MODEL)))
    (h2, n1, u1), (w_ain, w_aout) = _mlp_fwd(h1, mlp_g[0:1], w1_0, w2_0, "mlp0_fwd", ride=ride)
    qkv3, n2 = _norm_proj(h2, mix_g[1:2], w_ain, 3, "att_proj", 1024)
    ride = gather((s_w1[1], _cols(w1_cols), w1_full), (s_w2[1], _rows(w2_rows), w2_full))
    (o2,), (w1_1, w2_1) = _attn_fwd(qkv3, ring, B, S, "att_fwd", ride=ride)
    h3 = _matmul_res(o2, w_aout, h2, "att_out")
    h4, n3, u3 = _mlp_fwd(h3, mlp_g[1:2], w1_1, w2_1, "mlp1_fwd")
    dh4, dh4b, loss, dg_final = _final_loss(h4, final_g, tgt, "final_loss")

    def tok(width):
        return dict(a=pl.BlockSpec((tk, width), lambda i, j, k: (k, i)), b=pl.BlockSpec((tk, width), lambda i, j, k: (k, j)))

    def mlp_grads(dh, dhb, u, n, w1, w2, h, g, tag, ride=None, w2_rides_on_dw1=False):
        res = _mlp_bwd(dh, u, w1, w2, h, g, tag + "_bwd", ride=ride)
        (dhi, dhib, du, dg), lands = res if ride is not None else (res, None)
        gw2 = _wgrad(u, tok(1024)["a"], dhb, tok(1024)["b"], D_FF, D_MODEL, 1024, 1024, n_k, tag + "_dw2", square_a=True)
        ride_w2 = scatter((gw2, _rows(w2_rows), w2_shard)) if w2_rides_on_dw1 else None
        gw1 = _wgrad(n, tok(1024)["a"], du, tok(1024)["b"], D_MODEL, D_FF, 1024, 1024, n_k, tag + "_dw1", ride=ride_w2)
        return dhi, dhib, gw1, gw2, dg, lands

    dh3, dh3b, gw1_1, gw2_1, dg_mlp1, _ = mlp_grads(dh4, dh4b, u3, n3, w1_1, w2_1, h3, mlp_g[1:2], "mlp1")
    do2 = _matmul_nt(dh3b, w_aout, "att_out_bwd")
    g_aout = _wgrad(o2, tok(1024)["a"], dh3b, tok(1024)["b"], D_MODEL, D_MODEL, 1024, 1024, n_k, "att_out_dw")
    ride = scatter((gw1_1, _cols(w1_cols), w1_shard), (gw2_1, _rows(w2_rows), w2_shard),
                   (g_aout, _rows(D_MODEL // N_DEV), (D_MODEL // N_DEV, D_MODEL)))
    (dqkv3, dring), (l_w1_1, l_w2_1, l_aout) = _attn_bwd(qkv3, ring, do2, B, S, "att_bwd", ride=ride)
    g_rel = _rel_bias_grad(dring, "rel_bias_grad")
    dh2, dh2b, dg_mix1 = _proj_bwd(dqkv3, w_ain, h2, mix_g[1:2], dh3, "att_proj_bwd", 1024)
    g_ain = _wgrad(n2, tok(1024)["a"], dqkv3, pl.BlockSpec((None, tk, D_MODEL), lambda i, j, k: (j, k, 0)),
                   D_MODEL, 3 * D_MODEL, 1024, 1024, n_k, "att_proj_dw")
    ride = scatter((g_ain, _cols(3 * D_MODEL // N_DEV), (D_MODEL, 3 * D_MODEL // N_DEV)))
    dh1, dh1b, (gw1_0, (l_w2_0,)), _, dg_mlp0, (l_ain,) = mlp_grads(
        dh2, dh2b, u1, n1, w1_0, w2_0, h1, mlp_g[0:1], "mlp0", ride=ride, w2_rides_on_dw1=True)
    dy = _matmul_nt(dh1b, w_rout, "ret_out_bwd")
    g_rout = _wgrad(y, tok(1024)["a"], dh1b, tok(1024)["b"], RET_VW, D_MODEL, 1024, 1024, n_k, "ret_out_dw")
    ride = scatter((gw1_0, _cols(w1_cols), w1_shard), (g_rout, _rows(RET_VW // N_DEV), (RET_VW // N_DEV, D_MODEL)))
    (dproj, dgn), (l_w1_0, l_rout) = _ret_bwd(proj, consts, gn_g, o, states, dy, B, S, "ret_bwd", ride=ride)
    half = D_MODEL // 2

    def rin_rows(hf, ride=None):
        return _wgrad(n0, pl.BlockSpec((tk, half), functools.partial(lambda i, j, k, hf: (k, hf), hf=hf)), dproj,
                      [pl.BlockSpec((tk, RET_DK), functools.partial(
                          lambda i, j, k, r: (k, _ret_piece_of_column_block(per_shard * j + r)), r=r)) for r in range(per_shard)],
                      half, RET_IN, half, per_shard * RET_DK, n_k, f"ret_proj_dw{hf}", ride=ride)

    rin_shard = (half, RET_IN // N_DEV)
    g_rin_top = rin_rows(0)
    g_rin_bottom, (l_rin_top,) = rin_rows(1, ride=scatter((g_rin_top, _cols(RET_IN // N_DEV), rin_shard)))
    ride = scatter((g_rin_bottom, _cols(RET_IN // N_DEV), rin_shard))
    (dx, _, dg_mix0), (l_rin_bottom,) = _proj_bwd(dproj, w_rin_heads, h0, mix_g[0:1], dh1, "ret_proj_bwd", 512, ride=ride)

    small = jnp.concatenate([dg_mix0, dg_mix1, dg_mlp0, dg_mlp1, dg_final, dgn.reshape(2, D_MODEL),
                             jnp.zeros((1, D_MODEL), F32)], axis=0)
    l_small, l_rel = _Exchange(
        [small, g_rel], [(0, _whole, 0, _slot), (1, _whole, 1, _slot)],
        [jax.ShapeDtypeStruct((N_DEV, 8, D_MODEL), F32), jax.ShapeDtypeStruct((N_DEV, ATT_HEADS, REL_PAD), F32)],
    ).run("scatter_small")
    lands = dict(rin=(l_rin_top, l_rin_bottom), rout=l_rout, ain=l_ain, aout=l_aout, w1=(l_w1_0, l_w1_1), w2=(l_w2_0, l_w2_1),
                 small=l_small, rel=l_rel)
    return loss[0, 0], dx.reshape(B, S, D_MODEL), lands


def kernel(x, mix_norm_g, ret_w_in, ret_gn_g, ret_w_out, att_w_in, att_rel_bias, att_w_out, mlp_norm_g, mlp_w1, mlp_w2, final_norm_g, loss_target, m_mix_norm_g, m_ret_w_in, m_ret_gn_g, m_ret_w_out, m_att_w_in, m_att_rel_bias, m_att_w_out, m_mlp_norm_g, m_mlp_w1, m_mlp_w2, m_final_norm_g, v_mix_norm_g, v_ret_w_in, v_ret_gn_g, v_ret_w_out, v_att_w_in, v_att_rel_bias, v_att_w_out, v_mlp_norm_g, v_mlp_w1, v_mlp_w2, v_final_norm_g):
    me = _lin(_place())
    rel_cols = REL_TABLE // N_DEV
    shards = (ret_w_in[0].astype(BF16), ret_w_out[0].astype(BF16), att_w_in[0].astype(BF16), att_w_out[0].astype(BF16),
              (mlp_w1[0].astype(BF16), mlp_w1[1].astype(BF16)), (mlp_w2[0].astype(BF16), mlp_w2[1].astype(BF16)))
    loss_part, grad_x, lands = _step(x, loss_target, shards, mix_norm_g, ret_gn_g, att_rel_bias[0], mlp_norm_g,
                                     final_norm_g.reshape(1, D_MODEL))
    loss = lax.psum(loss_part, ("x", "y", "c"))

    halves = (2, D_MODEL // 2, RET_IN // N_DEV)
    u_rin = [t.reshape(ret_w_in.shape) for t in _adamw_reduce(
        lands["rin"], ret_w_in.reshape(halves), m_ret_w_in.reshape(halves), v_ret_w_in.reshape(halves), "update_ret_w_in")]
    u_rout = _adamw_reduce([lands["rout"]], ret_w_out, m_ret_w_out, v_ret_w_out, "update_ret_w_out")
    u_ain = _adamw_reduce([lands["ain"]], att_w_in, m_att_w_in, v_att_w_in, "update_att_w_in")
    u_aout = _adamw_reduce([lands["aout"]], att_w_out, m_att_w_out, v_att_w_out, "update_att_w_out")
    u_w1 = _adamw_reduce(lands["w1"], mlp_w1, m_mlp_w1, v_mlp_w1, "update_mlp_w1")
    u_w2 = _adamw_reduce(lands["w2"], mlp_w2, m_mlp_w2, v_mlp_w2, "update_mlp_w2")

    def pack(mix, mlp, fin, gn):
        return jnp.concatenate([mix, mlp, fin.reshape(1, D_MODEL), gn.reshape(2, D_MODEL), jnp.zeros((1, D_MODEL), F32)], axis=0)

    small_w = pack(mix_norm_g, mlp_norm_g, final_norm_g, ret_gn_g)
    small_m = pack(m_mix_norm_g, m_mlp_norm_g, m_final_norm_g, m_ret_gn_g)
    small_v = pack(v_mix_norm_g, v_mlp_norm_g, v_final_norm_g, v_ret_gn_g)
    sg, sd, sm, sv, rel_sum = _small_update(lands["small"], lands["rel"], small_w, small_m, small_v, "update_small")
    g_rel_mine = lax.dynamic_slice(rel_sum, (0, me * rel_cols), (ATT_HEADS, rel_cols))
    rel_d, rel_m, rel_v = _adamw_plain(att_rel_bias[0], g_rel_mine, m_att_rel_bias[0], v_att_rel_bias[0], "update_rel_bias")
    u_rel = [g_rel_mine[None], rel_d[None], rel_m[None], rel_v[None]]

    def unpack(t):
        return dict(mix=t[0:2], mlp=t[2:4], fin=t[4], gn=t[5:7].reshape(1, RET_VW))

    us = [unpack(t) for t in (sg, sd, sm, sv)]
    outs = [loss, grad_x]
    for k in range(4):
        outs += [us[k]["mix"], u_rin[k], us[k]["gn"], u_rout[k], u_ain[k], u_rel[k], u_aout[k], us[k]["mlp"],
                 u_w1[k], u_w2[k], us[k]["fin"]]
    return tuple(outs)
```

```python
import functools

import numpy as np
import jax
import jax.numpy as jnp
from jax import lax
from jax.experimental import pallas as pl
from jax.experimental.pallas import tpu as pltpu

F32, BF16 = jnp.float32, jnp.bfloat16

D_MODEL = 1024
CHUNK = 64
RET_HEADS, RET_DK, RET_DV = 4, 256, 512
RET_QK, RET_VW = RET_HEADS * RET_DK, RET_HEADS * RET_DV
RET_IN = 2 * RET_QK + 2 * RET_VW
RET_HEAD_COLS = 2 * RET_DK + 2 * RET_DV
RET_SC = 256
ROPE_BASE = 10000.0
ATT_HEADS, ATT_DH = 16, 64
LANES = 128
ATT_PAIRS = ATT_HEADS * ATT_DH // LANES
BAND_PAST = 8 * CHUNK
MAX_REL = 256
REL_TABLE = MAX_REL + CHUNK
REL_PAD = 384
ATT_CQ = 256
ATT_KW = ATT_CQ + BAND_PAST
ATT_RING = ATT_CQ + ATT_KW
ATT_VARIANTS = BAND_PAST // ATT_CQ + 1
D_FF = 4 * D_MODEL
EPS = 1e-6
EPILOGUE_ROWS = 256
NEG = -1e30
N_DEV = 8
N_PEERS = N_DEV - 1

ADAM_LR, ADAM_B1, ADAM_B2, ADAM_EPS, ADAM_WD, ADAM_STEP = 0.001, 0.9, 0.999, 1e-08, 0.01, 10

VMEM_LIMIT = 56 * 1024 * 1024
MESH = pl.DeviceIdType.MESH
ANY = pl.BlockSpec(memory_space=pl.ANY)


def _dot(a, b):
    return jnp.dot(a, b, preferred_element_type=F32)


def _dot_nt(a, b):
    return lax.dot_general(a, b, (((1,), (1,)), ((), ())), preferred_element_type=F32)


def _dot_tn(a, b):
    return lax.dot_general(a, b, (((0,), (0,)), ((), ())), preferred_element_type=F32)


def _rms(x, g):
    r = lax.rsqrt(jnp.mean(x * x, axis=-1, keepdims=True) + EPS)
    return x * r * g


def _rms_bwd(dn, x, g):
    r = lax.rsqrt(jnp.mean(x * x, axis=-1, keepdims=True) + EPS)
    xh = x * r
    dg = jnp.sum(dn * xh, axis=0, keepdims=True)
    dxh = dn * g
    dx = r * (dxh - xh * jnp.mean(dxh * xh, axis=-1, keepdims=True))
    return dx, dg


def _rms_bwd_epilogue(x_ref, g_ref, dres_ref, dx_ref, dxb_ref, dg_ref):
    for r in range(0, dx_ref.shape[0], EPILOGUE_ROWS):
        rows = slice(r, r + EPILOGUE_ROWS)
        dx, dg = _rms_bwd(dx_ref[rows, :], x_ref[rows, :], g_ref[...])
        dx = dres_ref[rows, :] + dx
        dx_ref[rows, :] = dx
        dxb_ref[rows, :] = dx.astype(BF16)
        dg_ref[...] += dg


def _place():
    return lax.axis_index("x"), lax.axis_index("y"), lax.axis_index("c")


def _lin(p):
    return 4 * p[0] + 2 * p[1] + p[2]


def _cols(width):
    return lambda ref, blk: ref.at[:, pl.ds(pl.multiple_of(blk * width, LANES), width)]


def _rows(height):
    return lambda ref, blk: ref.at[pl.ds(pl.multiple_of(blk * height, 8), height), :]


def _slot(ref, blk):
    return ref.at[blk]


def _whole(ref, blk):
    return ref


class _Exchange:
    def __init__(self, sources, flows, land_shapes):
        self.sources, self.flows, self.land_shapes = list(sources), flows, list(land_shapes)
        n_f = len(flows)
        self.sem_shapes = [pltpu.SemaphoreType.DMA((N_PEERS * n_f,)), pltpu.SemaphoreType.DMA((N_PEERS * n_f,)),
                           pltpu.SemaphoreType.DMA((n_f,))]

    def _copies(self, srcs, lands, sems):
        send_sems, recv_sems, local_sems = sems
        x, y, c = _place()
        me = (x, y, c)
        peers = [(x ^ dx, y ^ dy, c ^ dc) for dx in (0, 1) for dy in (0, 1) for dc in (0, 1)][1:]

        def copy(f, k, sender, to):
            si, sview, li, lview = self.flows[f]
            return pltpu.make_async_remote_copy(
                src_ref=sview(srcs[si], _lin(to)), dst_ref=lview(lands[li], _lin(sender)),
                send_sem=send_sems.at[f * N_PEERS + k], recv_sem=recv_sems.at[f * N_PEERS + k],
                device_id=to, device_id_type=MESH)

        mine, sends, recvs = [], [], []
        for f, (si, sview, li, lview) in enumerate(self.flows):
            mine.append(pltpu.make_async_copy(sview(srcs[si], _lin(me)), lview(lands[li], _lin(me)), local_sems.at[f]))
            for k, peer in enumerate(peers):
                sends.append(copy(f, k, me, peer))
                recvs.append(copy(f, k, peer, me))
        return mine, sends, recvs

    def start(self, srcs, lands, sems):
        mine, sends, _ = self._copies(srcs, lands, sems)
        for cp in mine + sends:
            cp.start()

    def finish(self, srcs, lands, sems):
        mine, sends, recvs = self._copies(srcs, lands, sems)
        for cp in recvs:
            cp.wait_recv()
        for cp in sends:
            cp.wait_send()
        for cp in mine:
            cp.wait()

    def run(self, name):
        n_src, n_land = len(self.sources), len(self.land_shapes)

        def body(*refs):
            srcs, lands, sems = refs[:n_src], refs[n_src:n_src + n_land], refs[n_src + n_land:]
            self.start(srcs, lands, sems)
            self.finish(srcs, lands, sems)

        return pl.pallas_call(body, name=name, in_specs=[ANY] * n_src, out_specs=[ANY] * n_land,
                              out_shape=self.land_shapes, scratch_shapes=self.sem_shapes)(*self.sources)


def _call(body, *, name, grid, in_specs, out_specs, out_shape, args, scratch_shapes=(), ride=None):
    params = pltpu.CompilerParams(dimension_semantics=("arbitrary",) * len(grid), vmem_limit_bytes=VMEM_LIMIT)
    in_specs, out_specs, out_shape, scratch_shapes = list(in_specs), list(out_specs), list(out_shape), list(scratch_shapes)
    if ride is None:
        return pl.pallas_call(body, name=name, grid=grid, in_specs=in_specs, out_specs=out_specs, out_shape=out_shape,
                              scratch_shapes=scratch_shapes, compiler_params=params)(*args)
    n_in, n_out, n_scr = len(in_specs), len(out_specs), len(scratch_shapes)
    n_src, n_land = len(ride.sources), len(ride.land_shapes)

    def riding(*refs):
        bounds = np.cumsum([n_in, n_src, n_out, n_land, n_scr])
        ins, srcs, outs, lands, scr, sems = (refs[a:b] for a, b in zip([0, *bounds], [*bounds, len(refs)]))
        step = functools.reduce(lambda a, d: a * grid[d] + pl.program_id(d), range(len(grid)), 0)
        n_steps = int(np.prod(grid))

        @pl.when(step == 0)
        def _():
            ride.start(srcs, lands, sems)

        if hasattr(ride, "middle"):
            @pl.when(step == (5 * n_steps) // 8)
            def _():
                ride.middle(srcs, lands, sems)

        body(*ins, *outs, *scr)

        @pl.when(step == n_steps - 1)
        def _():
            ride.finish(srcs, lands, sems)

    res = pl.pallas_call(
        riding, name=name, grid=grid, in_specs=in_specs + [ANY] * n_src, out_specs=out_specs + [ANY] * n_land,
        out_shape=out_shape + ride.land_shapes, scratch_shapes=scratch_shapes + ride.sem_shapes,
        compiler_params=params)(*args, *ride.sources)
    return res[:n_out], res[n_out:]


class _ChipGather:
    def __init__(self, shards, out_shapes, out_of, views):
        self.sources, self.land_shapes, self.out_of, self.views = list(shards), list(out_shapes), out_of, views
        n = len(shards)
        self.sem_shapes = [pltpu.SemaphoreType.DMA((N_PEERS * n,)), pltpu.SemaphoreType.DMA((N_PEERS * n,)),
                           pltpu.SemaphoreType.DMA((n,))]

    def _copies(self, srcs, lands, sems):
        send_sems, recv_sems, local_sems = sems
        n = len(srcs)
        outs = [lands[self.out_of[a]] for a in range(n)]
        x, y, c = _place()
        me, sibling = (x, y, c), (x, y, 1 - c)
        chips = [(1 - x, y), (x, 1 - y), (1 - x, 1 - y)]

        def copy(a, k, block, to, own=False):
            dst = self.views[a](outs[a], _lin(block))
            return pltpu.make_async_remote_copy(
                src_ref=srcs[a] if own else dst, dst_ref=dst,
                send_sem=send_sems.at[a * N_PEERS + k], recv_sem=recv_sems.at[a * N_PEERS + k],
                device_id=to, device_id_type=MESH)

        mine = lambda: [pltpu.make_async_copy(srcs[a], self.views[a](outs[a], _lin(me)), local_sems.at[a]) for a in range(n)]
        first = lambda: [cp for a in range(n) for cp in
                         [copy(a, 0, me, sibling, own=True)] + [copy(a, 1 + j, me, (*chip, c), own=True) for j, chip in enumerate(chips)]]
        landed = lambda a, j: copy(a, 1 + j, (*chips[j], c), me)
        passed = lambda a, j: copy(a, 4 + j, (*chips[j], c), sibling)
        from_sibling = lambda a: [copy(a, 0, sibling, me)] + [copy(a, 4 + j, (*chips[j], 1 - c), me) for j in range(3)]
        return n, mine, first, landed, passed, from_sibling

    def start(self, srcs, lands, sems):
        _, mine, first, _, _, _ = self._copies(srcs, lands, sems)
        for cp in mine() + first():
            cp.start()

    def middle(self, srcs, lands, sems):
        n, _, _, landed, passed, _ = self._copies(srcs, lands, sems)
        for j in range(3):
            for a in range(n):
                landed(a, j).wait_recv()
                passed(a, j).start()

    def finish(self, srcs, lands, sems):
        n, mine, first, _, passed, from_sibling = self._copies(srcs, lands, sems)
        for a in range(n):
            for cp in from_sibling(a):
                cp.wait_recv()
        for cp in first() + [passed(a, j) for j in range(3) for a in range(n)]:
            cp.wait_send()
        for cp in mine():
            cp.wait()

    def run(self, name):
        n, n_out = len(self.sources), len(self.land_shapes)

        def body(*refs):
            parts = refs[:n], refs[n:n + n_out], refs[n + n_out:]
            self.start(*parts)
            self.middle(*parts)
            self.finish(*parts)

        return pl.pallas_call(body, name=name, in_specs=[ANY] * n, out_specs=[ANY] * n_out,
                              out_shape=self.land_shapes, scratch_shapes=self.sem_shapes)(*self.sources)


def _resident(shape):
    return pl.BlockSpec(shape, lambda i: (0,) * len(shape), pipeline_mode=pl.Buffered(1))


def _norm_proj(h, g, w, groups, name, tm, ride=None):
    T = h.shape[0]
    N = w.shape[1]
    width = N // groups

    def body(h_ref, g_ref, w_ref, o_ref, n_ref):
        n = _rms(h_ref[...], g_ref[...]).astype(BF16)
        n_ref[...] = n
        if groups == 1:
            o_ref[...] = _dot(n, w_ref[...]).astype(BF16)
        else:
            for p in range(groups):
                o_ref[p] = _dot(n, w_ref[:, p * width:(p + 1) * width]).astype(BF16)

    row = pl.BlockSpec((tm, D_MODEL), lambda i: (i, 0))
    if groups == 1:
        out_spec, out_shape = pl.BlockSpec((tm, N), lambda i: (i, 0)), jax.ShapeDtypeStruct((T, N), BF16)
    else:
        out_spec = pl.BlockSpec((groups, tm, width), lambda i: (0, i, 0))
        out_shape = jax.ShapeDtypeStruct((groups, T, width), BF16)
    return _call(body, name=name, grid=(T // tm,),
                 in_specs=[row, pl.BlockSpec((1, D_MODEL), lambda i: (0, 0)), _resident(w.shape)],
                 out_specs=[out_spec, row], out_shape=[out_shape, jax.ShapeDtypeStruct((T, D_MODEL), BF16)],
                 args=(h, g, w), ride=ride)


def _ret_proj(h, g, w, cos, sin, S, name, tm, ride=None):
    T = h.shape[0]
    per_seq = S // tm

    def body(h_ref, g_ref, w_ref, cos_ref, sin_ref, o_ref, n_ref):
        n = _rms(h_ref[...], g_ref[...]).astype(BF16)
        n_ref[...] = n
        cs, sn = cos_ref[...], sin_ref[...]
        for hd in range(RET_HEADS):
            c0 = hd * RET_HEAD_COLS
            out = _dot(n, w_ref[:, c0:c0 + RET_HEAD_COLS])
            o_ref[:, c0:c0 + RET_DK] = _rope(out[:, 0:RET_DK], cs, sn).astype(BF16)
            o_ref[:, c0 + RET_DK:c0 + 2 * RET_DK] = (_rope(out[:, RET_DK:2 * RET_DK], cs, sn) * (RET_DK ** -0.5)).astype(BF16)
            o_ref[:, c0 + 2 * RET_DK:c0 + RET_HEAD_COLS] = out[:, 2 * RET_DK:].astype(BF16)

    row = pl.BlockSpec((tm, D_MODEL), lambda i: (i, 0))
    trig = pl.BlockSpec((tm, RET_DK // 2), lambda i: (i % per_seq, 0))
    return _call(body, name=name, grid=(T // tm,),
                 in_specs=[row, pl.BlockSpec((1, D_MODEL), lambda i: (0, 0)), _resident(w.shape), trig, trig],
                 out_specs=[pl.BlockSpec((tm, w.shape[1]), lambda i: (i, 0)), row],
                 out_shape=[jax.ShapeDtypeStruct((T, w.shape[1]), BF16), jax.ShapeDtypeStruct((T, D_MODEL), BF16)],
                 args=(h, g, w, cos, sin), ride=ride)


def _proj_bwd(dy, w, x, g, dres, name, tm, ride=None):
    T = x.shape[0]
    groups = dy.shape[0] if dy.ndim == 3 else 1
    width = w.shape[1] // groups

    def body(dy_ref, w_ref, x_ref, g_ref, dres_ref, dx_ref, dxb_ref, dg_ref):
        @pl.when(pl.program_id(0) == 0)
        def _():
            dg_ref[...] = jnp.zeros_like(dg_ref)

        if groups == 1:
            dn = _dot_nt(dy_ref[...], w_ref[...])
        else:
            dn = sum(_dot_nt(dy_ref[p], w_ref[:, p * width:(p + 1) * width]) for p in range(groups))
        dx_ref[...] = dn
        _rms_bwd_epilogue(x_ref, g_ref, dres_ref, dx_ref, dxb_ref, dg_ref)

    row = pl.BlockSpec((tm, D_MODEL), lambda i: (i, 0))
    vec = pl.BlockSpec((1, D_MODEL), lambda i: (0, 0))
    dy_spec = (pl.BlockSpec((tm, w.shape[1]), lambda i: (i, 0)) if groups == 1
               else pl.BlockSpec((groups, tm, width), lambda i: (0, i, 0)))
    return _call(body, name=name, grid=(T // tm,),
                 in_specs=[dy_spec, _resident(w.shape), row, vec, row], out_specs=[row, row, vec],
                 out_shape=[jax.ShapeDtypeStruct((T, D_MODEL), F32), jax.ShapeDtypeStruct((T, D_MODEL), BF16),
                            jax.ShapeDtypeStruct((1, D_MODEL), F32)],
                 args=(dy, w, x, g, dres), ride=ride)


def _matmul_res(a, w, res, name, tm=1024):
    T, K = a.shape

    def body(a_ref, w_ref, r_ref, o_ref):
        o_ref[...] = r_ref[...] + _dot(a_ref[...], w_ref[...])

    row = pl.BlockSpec((tm, D_MODEL), lambda i: (i, 0))
    return _call(body, name=name, grid=(T // tm,),
                 in_specs=[pl.BlockSpec((tm, K), lambda i: (i, 0)), pl.BlockSpec((K, D_MODEL), lambda i: (0, 0)), row],
                 out_specs=[row], out_shape=[jax.ShapeDtypeStruct((T, D_MODEL), F32)], args=(a, w, res))[0]


def _matmul_nt(dy, w, name, tm=1024):
    T, N = dy.shape
    K = w.shape[0]

    def body(dy_ref, w_ref, o_ref):
        o_ref[...] = _dot_nt(dy_ref[...], w_ref[...]).astype(BF16)

    return _call(body, name=name, grid=(T // tm,),
                 in_specs=[pl.BlockSpec((tm, N), lambda i: (i, 0)), pl.BlockSpec((K, N), lambda i: (0, 0))],
                 out_specs=[pl.BlockSpec((tm, K), lambda i: (i, 0))], out_shape=[jax.ShapeDtypeStruct((T, K), BF16)],
                 args=(dy, w))[0]


def _wgrad(a, a_spec, b, b_specs, m, n, bm, bn, n_k, name, square_a=False, ride=None):
    b_specs = b_specs if isinstance(b_specs, (list, tuple)) else [b_specs]
    n_b = len(b_specs)

    def body(a_ref, *rest):
        b_refs = rest[:n_b]
        o_ref, acc = rest[n_b:]
        k = pl.program_id(2)

        @pl.when(k == 0)
        def _():
            acc[...] = jnp.zeros_like(acc)

        av = a_ref[...]
        if square_a:
            af = av.astype(F32)
            av = (af * af).astype(BF16)
        bv = b_refs[0][...] if n_b == 1 else jnp.concatenate([r[...] for r in b_refs], axis=1)
        acc[...] += _dot_tn(av, bv)

        @pl.when(k == n_k - 1)
        def _():
            o_ref[...] = acc[...].astype(BF16)

    res = _call(body, name=name, grid=(m // bm, n // bn, n_k), in_specs=[a_spec, *b_specs],
                out_specs=[pl.BlockSpec((bm, bn), lambda i, j, k: (i, j))], out_shape=[jax.ShapeDtypeStruct((m, n), BF16)],
                scratch_shapes=[pltpu.VMEM((bm, bn), F32)], args=(a, *([b] * n_b)), ride=ride)
    return res[0] if ride is None else (res[0][0], res[1])


def _mlp_fwd(h, g, w1, w2, name, tm=1024, tf=1024, ride=None):
    T = h.shape[0]

    def body(h_ref, g_ref, w1_ref, w2_ref, ho_ref, n_ref, u_ref):
        @pl.when(pl.program_id(1) == 0)
        def _():
            n_ref[...] = _rms(h_ref[...], g_ref[...]).astype(BF16)
            ho_ref[...] = h_ref[...]

        u = jnp.maximum(_dot(n_ref[...], w1_ref[...]), 0.0)
        u_ref[...] = u.astype(BF16)
        ho_ref[...] += _dot((u * u).astype(BF16), w2_ref[...])

    row = pl.BlockSpec((tm, D_MODEL), lambda i, j: (i, 0))
    return _call(body, name=name, grid=(T // tm, D_FF // tf),
                 in_specs=[row, pl.BlockSpec((1, D_MODEL), lambda i, j: (0, 0)),
                           pl.BlockSpec((D_MODEL, tf), lambda i, j: (0, j)), pl.BlockSpec((tf, D_MODEL), lambda i, j: (j, 0))],
                 out_specs=[row, row, pl.BlockSpec((tm, tf), lambda i, j: (i, j))],
                 out_shape=[jax.ShapeDtypeStruct((T, D_MODEL), F32), jax.ShapeDtypeStruct((T, D_MODEL), BF16),
                            jax.ShapeDtypeStruct((T, D_FF), BF16)],
                 args=(h, g, w1, w2), ride=ride)


def _mlp_bwd(dh, u, w1, w2, h, g, name, tm=1024, tf=1024, ride=None):
    T = h.shape[0]
    n_j = D_FF // tf

    def body(dh_ref, u_ref, w1_ref, w2_ref, h_ref, g_ref, dx_ref, dxb_ref, du_ref, dg_ref, dhb):
        acc = dx_ref
        i, j = pl.program_id(0), pl.program_id(1)

        @pl.when(j == 0)
        def _():
            dhb[...] = dh_ref[...].astype(BF16)
            acc[...] = jnp.zeros_like(acc)

        @pl.when((i == 0) & (j == 0))
        def _():
            dg_ref[...] = jnp.zeros_like(dg_ref)

        da = _dot_nt(dhb[...], w2_ref[...])
        du = (da * (2.0 * u_ref[...].astype(F32))).astype(BF16)
        du_ref[...] = du
        acc[...] += _dot_nt(du, w1_ref[...])

        @pl.when(j == n_j - 1)
        def _():
            _rms_bwd_epilogue(h_ref, g_ref, dh_ref, dx_ref, dxb_ref, dg_ref)

    row = pl.BlockSpec((tm, D_MODEL), lambda i, j: (i, 0))
    vec = pl.BlockSpec((1, D_MODEL), lambda i, j: (0, 0))
    hid = pl.BlockSpec((tm, tf), lambda i, j: (i, j))
    return _call(body, name=name, grid=(T // tm, n_j),
                 in_specs=[row, hid, pl.BlockSpec((D_MODEL, tf), lambda i, j: (0, j)),
                           pl.BlockSpec((tf, D_MODEL), lambda i, j: (j, 0)), row, vec],
                 out_specs=[row, row, hid, vec],
                 out_shape=[jax.ShapeDtypeStruct((T, D_MODEL), F32), jax.ShapeDtypeStruct((T, D_MODEL), BF16),
                            jax.ShapeDtypeStruct((T, D_FF), BF16), jax.ShapeDtypeStruct((1, D_MODEL), F32)],
                 scratch_shapes=[pltpu.VMEM((tm, D_MODEL), BF16)], args=(dh, u, w1, w2, h, g), ride=ride)


def _final_loss(h, g, target, name, tm=512):
    T = h.shape[0]

    def body(h_ref, g_ref, t_ref, dh_ref, dhb_ref, loss_ref, dg_ref):
        @pl.when(pl.program_id(0) == 0)
        def _():
            loss_ref[...] = jnp.zeros_like(loss_ref)
            dg_ref[...] = jnp.zeros_like(dg_ref)

        x = h_ref[...]
        gg = g_ref[...]
        r = lax.rsqrt(jnp.mean(x * x, axis=-1, keepdims=True) + EPS)
        xh = x * r
        e = xh * gg - t_ref[...]
        per_tok = jnp.mean(e * e, axis=-1, keepdims=True)
        loss_ref[...] += 0.5 * jnp.sum(per_tok, axis=0, keepdims=True)
        dy = e * (1.0 / D_MODEL)
        dg_ref[...] += jnp.sum(dy * xh, axis=0, keepdims=True)
        dxh = dy * gg
        dx = r * (dxh - xh * jnp.mean(dxh * xh, axis=-1, keepdims=True))
        dh_ref[...] = dx
        dhb_ref[...] = dx.astype(BF16)

    row = pl.BlockSpec((tm, D_MODEL), lambda i: (i, 0))
    vec = pl.BlockSpec((1, D_MODEL), lambda i: (0, 0))
    return _call(body, name=name, grid=(T // tm,), in_specs=[row, vec, row],
                 out_specs=[row, row, pl.BlockSpec((8, LANES), lambda i: (0, 0)), vec],
                 out_shape=[jax.ShapeDtypeStruct((T, D_MODEL), F32), jax.ShapeDtypeStruct((T, D_MODEL), BF16),
                            jax.ShapeDtypeStruct((8, LANES), F32), jax.ShapeDtypeStruct((1, D_MODEL), F32)],
                 args=(h, g, target))


def _ret_constants(S):
    log_gamma = jnp.log1p(-jnp.exp2(-5.0 - jnp.arange(RET_HEADS, dtype=F32)))
    idx = jnp.arange(RET_SC, dtype=F32)
    i, j = idx[:, None], idx[None, :]
    same_chunk = jnp.floor(i / CHUNK) == jnp.floor(j / CHUNK)
    mask = jnp.where((j <= i) | same_chunk, jnp.exp(log_gamma[:, None, None] * jnp.abs(i - j)[None]), 0.0)
    qdec = jnp.exp(log_gamma[:, None] * (idx + 1.0)[None, :])[:, :, None]
    kdec = jnp.exp(log_gamma[:, None] * (RET_SC - 1 - idx)[None, :])[:, :, None]
    cdec = jnp.exp(log_gamma * RET_SC)[:, None, None]
    half = RET_DK // 2
    inv = jnp.exp(-jnp.log(ROPE_BASE) * jnp.arange(half, dtype=F32) / half)
    ang = jnp.arange(S, dtype=F32)[:, None] * inv[None, :]
    return jnp.cos(ang), jnp.sin(ang), mask.astype(F32), qdec, kdec, cdec


def _rope(t, cs, sn):
    t1, t2 = t[:, :RET_DK // 2], t[:, RET_DK // 2:]
    return jnp.concatenate([t1 * cs - t2 * sn, t1 * sn + t2 * cs], axis=-1)


def _rope_bwd(d, cs, sn):
    d1, d2 = d[:, :RET_DK // 2], d[:, RET_DK // 2:]
    return jnp.concatenate([d1 * cs + d2 * sn, d2 * cs - d1 * sn], axis=-1)


def _ret_specs(B, S, reverse):
    n_sc = S // RET_SC

    def cc(c):
        return n_sc - 1 - c if reverse else c

    return dict(
        proj=pl.BlockSpec((B, RET_SC, RET_HEAD_COLS), lambda h, c: (0, cc(c), h)),
        trig=pl.BlockSpec((RET_SC, RET_DK // 2), lambda h, c: (cc(c), 0)),
        mask=pl.BlockSpec((None, RET_SC, RET_SC), lambda h, c: (h, 0, 0)),
        dec=pl.BlockSpec((None, RET_SC, 1), lambda h, c: (h, 0, 0)),
        cdec=pl.BlockSpec((None, 1, 1), lambda h, c: (h, 0, 0)),
        gn=pl.BlockSpec((1, RET_DV), lambda h, c: (0, h)),
        val=pl.BlockSpec((B, RET_SC, RET_DV), lambda h, c: (0, cc(c), h)),
        state=pl.BlockSpec((B, None, None, RET_DK, RET_DV), lambda h, c: (0, h, cc(c), 0, 0)),
    )


def _ret_qkvg(p_ref, kdec):
    qb = p_ref[:, 0:RET_DK]
    kb = p_ref[:, RET_DK:2 * RET_DK]
    kt = (kb.astype(F32) * kdec).astype(BF16)
    v = p_ref[:, 2 * RET_DK:2 * RET_DK + RET_DV]
    gate = p_ref[:, 2 * RET_DK + RET_DV:RET_HEAD_COLS].astype(F32)
    return qb, kb, kt, v, gate


def _group_norm(o):
    mu = jnp.mean(o, axis=-1, keepdims=True)
    oc = o - mu
    rstd = lax.rsqrt(jnp.mean(oc * oc, axis=-1, keepdims=True) + EPS)
    return oc * rstd, rstd


def _ret_fwd(proj, consts, gn, B, S, name, ride=None):
    T = B * S
    n_sc = S // RET_SC
    sp = _ret_specs(B, S, False)

    def body(p_ref, m_ref, qd_ref, kd_ref, cd_ref, gn_ref, y_ref, o_ref, st_ref, state):
        @pl.when(pl.program_id(1) == 0)
        def _():
            state[...] = jnp.zeros_like(state)

        seqs = range(B)
        qkvg = [_ret_qkvg(p_ref.at[b], kd_ref[...]) for b in seqs]
        qb, kb, kt, v = ([qkvg[b][i] for b in seqs] for i in range(4))
        sc = [_dot_nt(qb[b], kb[b]) for b in seqs]
        sb = [state[b].astype(BF16) for b in seqs]
        cross = [_dot(qb[b], sb[b]) for b in seqs]
        for b in seqs:
            st_ref[b] = sb[b]
        p = [(sc[b] * m_ref[...]).astype(BF16) for b in seqs]
        o = [_dot(p[b], v[b]) + cross[b] * qd_ref[...] for b in seqs]
        upd = [_dot_tn(kt[b], v[b]) for b in seqs]
        for b in seqs:
            o_ref[b] = o[b].astype(BF16)
            ohat, _ = _group_norm(o[b])
            gate = qkvg[b][4]
            y_ref[b] =(gate * jax.nn.sigmoid(gate) * (ohat * gn_ref[...])).astype(BF16)
            state[b] = state[b] * cd_ref[...] + upd[b]

    res = _call(
        body, name=name, grid=(RET_HEADS, n_sc),
        in_specs=[sp["proj"], sp["mask"], sp["dec"], sp["dec"], sp["cdec"], sp["gn"]],
        out_specs=[sp["val"], sp["val"], sp["state"]],
        out_shape=[jax.ShapeDtypeStruct((B, S, RET_VW), BF16), jax.ShapeDtypeStruct((B, S, RET_VW), BF16),
                   jax.ShapeDtypeStruct((B, RET_HEADS, n_sc, RET_DK, RET_DV), BF16)],
        scratch_shapes=[pltpu.VMEM((B, RET_DK, RET_DV), F32)],
        args=(proj.reshape(B, S, -1), *consts[2:], gn), ride=ride)
    (y, o, states), lands = res if ride is not None else (res, None)
    outs = [y.reshape(T, RET_VW), o.reshape(T, RET_VW), states]
    return outs if ride is None else (outs, lands)


def _ret_bwd(proj, consts, gn, o, states, dy, B, S, name, ride=None):
    T = B * S
    n_sc = S // RET_SC
    sp = _ret_specs(B, S, True)

    def body(p_ref, cos_ref, sin_ref, m_ref, qd_ref, kd_ref, cd_ref, gn_ref, o_ref, st_ref, dy_ref,
             dp_ref, dgn_ref, dstate):
        @pl.when(pl.program_id(1) == 0)
        def _():
            dstate[...] = jnp.zeros_like(dstate)
            dgn_ref[...] = jnp.zeros_like(dgn_ref)

        seqs = range(B)
        cs, sn = cos_ref[...], sin_ref[...]
        m, gnv = m_ref[...], gn_ref[...]
        qkvg = [_ret_qkvg(p_ref.at[b], kd_ref[...]) for b in seqs]
        qb, kb, kt, v = ([qkvg[b][i] for b in seqs] for i in range(4))
        sc = [_dot_nt(qb[b], kb[b]) for b in seqs]
        dsb = [dstate[b].astype(BF16) for b in seqs]
        dv_state = [_dot(kt[b], dsb[b]) for b in seqs]
        dkt = [_dot_nt(v[b], dsb[b]) for b in seqs]
        do, dgate = [], []
        for b in seqs:
            gate = qkvg[b][4]
            ohat, rstd = _group_norm(o_ref[b].astype(F32))
            dyv = dy_ref[b].astype(F32)
            sg = jax.nn.sigmoid(gate)
            don = dyv * (gate * sg)
            dgate.append(dyv * (ohat * gnv) * (sg * (1.0 + gate * (1.0 - sg))))
            dgn_ref[...] += jnp.sum(don * ohat, axis=0, keepdims=True)
            dohat = don * gnv
            do.append(rstd * (dohat - jnp.mean(dohat, axis=-1, keepdims=True)
                              - ohat * jnp.mean(dohat * ohat, axis=-1, keepdims=True)))
        dob = [do[b].astype(BF16) for b in seqs]
        doq = [(do[b] * qd_ref[...]).astype(BF16) for b in seqs]
        dsc_f = [_dot_nt(dob[b], v[b]) for b in seqs]
        dq_state = [_dot_nt(doq[b], st_ref[b]) for b in seqs]
        dstate_upd = [_dot_tn(qb[b], doq[b]) for b in seqs]
        p = [(sc[b] * m).astype(BF16) for b in seqs]
        dsc = [(dsc_f[b] * m).astype(BF16) for b in seqs]
        dv = [_dot_tn(p[b], dob[b]) + dv_state[b] for b in seqs]
        dq = [_dot(dsc[b], kb[b]) + dq_state[b] for b in seqs]
        dk = [(_dot_tn(dsc[b], qb[b]) + dkt[b] * kd_ref[...]) * (RET_DK ** -0.5) for b in seqs]
        for b in seqs:
            dstate[b] = dstate[b] * cd_ref[...] + dstate_upd[b]
            dp_ref[b, :, 0:RET_DK] = _rope_bwd(dq[b], cs, sn).astype(BF16)
            dp_ref[b, :, RET_DK:2 * RET_DK] = _rope_bwd(dk[b], cs, sn).astype(BF16)
            dp_ref[b, :, 2 * RET_DK:2 * RET_DK + RET_DV] = dv[b].astype(BF16)
            dp_ref[b, :, 2 * RET_DK + RET_DV:RET_HEAD_COLS] = dgate[b].astype(BF16)

    res = _call(
        body, name=name, grid=(RET_HEADS, n_sc),
        in_specs=[sp["proj"], sp["trig"], sp["trig"], sp["mask"], sp["dec"], sp["dec"], sp["cdec"], sp["gn"],
                  sp["val"], sp["state"], sp["val"]],
        out_specs=[sp["proj"], sp["gn"]],
        out_shape=[jax.ShapeDtypeStruct((B, S, RET_HEADS * RET_HEAD_COLS), BF16), jax.ShapeDtypeStruct((1, RET_VW), F32)],
        scratch_shapes=[pltpu.VMEM((B, RET_DK, RET_DV), F32)],
        args=(proj.reshape(B, S, -1), *consts, gn, o.reshape(B, S, -1), states, dy.reshape(B, S, -1)), ride=ride)
    (dproj, dgn), lands = res if ride is not None else (res, None)
    outs = [dproj.reshape(T, -1), dgn]
    return outs if ride is None else (outs, lands)


def _ring_index():
    u = np.arange(ATT_RING)
    offset = np.where(u < ATT_KW, u, u - ATT_RING)
    return np.clip(offset - BAND_PAST, -MAX_REL, CHUNK - 1) + MAX_REL


def _bias_ring(rel):
    n_clip = BAND_PAST - MAX_REL
    n_hi = ATT_KW - n_clip - REL_TABLE
    ring = jnp.concatenate([jnp.broadcast_to(rel[:, :1], (ATT_HEADS, n_clip)), rel,
                            jnp.broadcast_to(rel[:, -1:], (ATT_HEADS, n_hi)),
                            jnp.broadcast_to(rel[:, :1], (ATT_HEADS, ATT_CQ))], axis=1)
    return jnp.broadcast_to(ring.reshape(ATT_PAIRS, 2, 1, ATT_RING), (ATT_PAIRS, 2, 8, ATT_RING))


def _band_masks():
    i = np.arange(ATT_CQ)[:, None]
    j = np.arange(ATT_KW)[None, :]
    lo = CHUNK * (i // CHUNK)
    band = np.where((j >= lo) & (j < lo + BAND_PAST + CHUNK), 0.0, NEG)
    return np.stack([band + np.where(j + v * ATT_CQ >= BAND_PAST, 0.0, NEG)
                     for v in range(ATT_VARIANTS)]).astype(np.float32)


def _attn_bias(bias_scr, ring_ref, band_ref):
    for hh in range(2):
        rows = jnp.broadcast_to(ring_ref[hh, 0:1, :], (ATT_CQ, ATT_RING))
        toeplitz = pltpu.roll(rows, 0, 1, stride=1, stride_axis=0)[:, :ATT_KW]
        for v in range(ATT_VARIANTS):
            bias_scr[v, hh] = toeplitz + band_ref[v]


ATT_STRIP = 32
ATT_SCALE = ATT_DH ** -0.5


def _strips(fn):
    def strip(r, carry):
        fn(pl.ds(pl.multiple_of(r * ATT_STRIP, ATT_STRIP), ATT_STRIP))
        return carry

    lax.fori_loop(0, ATT_CQ // ATT_STRIP, strip, 0, unroll=True)


def _attn_prepare(kpad, vpad, qm, qkv_ref):
    kpad[0:BAND_PAST, :] = jnp.zeros((BAND_PAST, LANES), BF16)
    vpad[0:BAND_PAST, :] = jnp.zeros((BAND_PAST, LANES), BF16)
    kpad[BAND_PAST:, :] = qkv_ref[1]
    vpad[BAND_PAST:, :] = qkv_ref[2]
    lane = lax.broadcasted_iota(jnp.int32, (1, LANES), 1)
    q = qkv_ref[0] * ATT_SCALE
    for hh in range(2):
        qm[hh] = jnp.where((lane >= ATT_DH * hh) & (lane < ATT_DH * (hh + 1)), q, jnp.zeros_like(q))


def _attn_scores(s_ref, qm, kpad, b_ref, t, n_qb):
    t = jnp.minimum(t, n_qb - 1)
    qs = pl.multiple_of(t * ATT_CQ, ATT_CQ)
    variant = jnp.minimum(t, ATT_VARIANTS - 1)
    kw = kpad[pl.ds(qs, ATT_KW), :]
    for hh in range(2):
        s_ref[hh] = _dot_nt(qm[hh, pl.ds(qs, ATT_CQ), :], kw) + b_ref[variant, hh]


def _attn_softmax(s_ref, e_ref, linv_ref, m_ref=None):
    def strip(rows):
        s = s_ref[rows, :]
        m = jnp.max(s, axis=-1, keepdims=True)
        e = jnp.exp(s - m)
        e_ref[rows, :] = e.astype(BF16)
        linv_ref[rows, :] = jnp.broadcast_to(1.0 / jnp.sum(e, axis=-1, keepdims=True), (ATT_STRIP, LANES))
        if m_ref is not None:
            m_ref[rows, :] = jnp.broadcast_to(m, (ATT_STRIP, LANES))

    _strips(strip)


_RING_SPEC = pl.BlockSpec((None, 2, 8, ATT_RING), lambda hp, b: (hp, 0, 0, 0))
_BAND_SPEC = pl.BlockSpec((ATT_VARIANTS, ATT_CQ, ATT_KW), lambda hp, b: (0, 0, 0))


def _attn_fwd(qkv3, ring, B, S, name, ride=None):
    T = B * S
    n_qb = S // ATT_CQ

    def body(qkv_ref, ring_ref, band_ref, o_ref, kpad, vpad, qm, b_ref, s_scr, e_scr, linv_scr):
        @pl.when(pl.program_id(1) == 0)
        def _():
            _attn_bias(b_ref, ring_ref, band_ref)

        _attn_prepare(kpad, vpad, qm, qkv_ref)
        e_scr[...] = jnp.zeros_like(e_scr)
        linv_scr[...] = jnp.zeros_like(linv_scr)
        lane = lax.broadcasted_iota(jnp.int32, (1, LANES), 1)

        def softmax(slot):
            for hh in range(2):
                _attn_softmax(s_scr.at[slot, hh], e_scr.at[slot, hh], linv_scr.at[slot, hh])

        def output(t, slot):
            qs = pl.multiple_of(jnp.maximum(t, 0) * ATT_CQ, ATT_CQ)
            vw = vpad[pl.ds(qs, ATT_KW), :]
            outs = [_dot(e_scr[slot, hh], vw) * linv_scr[slot, hh] for hh in range(2)]
            o_ref[pl.ds(qs, ATT_CQ), :] = jnp.where(lane < ATT_DH, outs[0], outs[1]).astype(BF16)

        def pair(u, carry):
            t = 2 * u
            _attn_scores(s_scr.at[1], qm, kpad, b_ref, t + 1, n_qb)
            softmax(0)
            output(t - 1, 1)
            _attn_scores(s_scr.at[0], qm, kpad, b_ref, t + 2, n_qb)
            softmax(1)
            output(t, 0)
            return carry

        _attn_scores(s_scr.at[0], qm, kpad, b_ref, 0, n_qb)
        lax.fori_loop(0, n_qb // 2, pair, 0)
        output(n_qb - 1, 1)

    return _call(body, name=name, grid=(ATT_PAIRS, B),
                 in_specs=[pl.BlockSpec((3, S, LANES), lambda hp, b: (0, b, hp)), _RING_SPEC, _BAND_SPEC],
                 out_specs=[pl.BlockSpec((S, LANES), lambda hp, b: (b, hp))],
                 out_shape=[jax.ShapeDtypeStruct((T, D_MODEL), BF16)],
                 scratch_shapes=[pltpu.VMEM((S + BAND_PAST, LANES), BF16), pltpu.VMEM((S + BAND_PAST, LANES), BF16),
                                 pltpu.VMEM((2, S, LANES), BF16), pltpu.VMEM((ATT_VARIANTS, 2, ATT_CQ, ATT_KW), F32),
                                 pltpu.VMEM((2, 2, ATT_CQ, ATT_KW), F32),
                                 pltpu.VMEM((2, 2, ATT_CQ, ATT_KW), BF16), pltpu.VMEM((2, 2, ATT_CQ, LANES), F32)],
                 args=(qkv3, ring, jnp.asarray(_band_masks())), ride=ride)


def _attn_bwd(qkv3, ring, do, B, S, name, ride=None):
    T = B * S
    n_qb = S // ATT_CQ

    def body(qkv_ref, ring_ref, band_ref, do_ref, dqkv_ref, dring_ref, kpad, vpad, qm, dkacc, dvacc, b_ref, db_ref):
        @pl.when(pl.program_id(1) == 0)
        def _():
            _attn_bias(b_ref, ring_ref, band_ref)
            db_ref[...] = jnp.zeros_like(db_ref)

        _attn_prepare(kpad, vpad, qm, qkv_ref)
        dkacc[...] = jnp.zeros_like(dkacc)
        dvacc[...] = jnp.zeros_like(dvacc)
        lane = lax.broadcasted_iota(jnp.int32, (1, LANES), 1)

        def step(qb, carry):
            qs = pl.multiple_of(qb * ATT_CQ, ATT_CQ)
            variant = jnp.minimum(qb, ATT_VARIANTS - 1)
            dov = do_ref[pl.ds(qs, ATT_CQ), :].astype(F32)
            kw = kpad[pl.ds(qs, ATT_KW), :]
            vw = vpad[pl.ds(qs, ATT_KW), :]
            heads = (0, 1)
            qmh = [qm[hh, pl.ds(qs, ATT_CQ), :] for hh in heads]
            s = [_dot_nt(qmh[hh], kw) + b_ref[variant, hh] for hh in heads]
            e, linv, dom, dp = [None, None], [None, None], [None, None], [None, None]
            for hh in heads:
                e[hh] = jnp.exp(s[hh] - jnp.max(s[hh], axis=-1, keepdims=True))
                linv[hh] = 1.0 / jnp.sum(e[hh], axis=-1, keepdims=True)
                sel = (lane >= ATT_DH * hh) & (lane < ATT_DH * (hh + 1))
                dom[hh] = jnp.where(sel, dov * linv[hh], 0.0).astype(BF16)
                dp[hh] = _dot_nt(dom[hh], vw)
            dqs, dk, dv = [], None, None
            for hh in heads:
                ds = e[hh] * (dp[hh] - jnp.sum(dp[hh] * e[hh], axis=-1, keepdims=True) * linv[hh])
                db_ref[hh] += ds
                dsb = ds.astype(BF16)
                dqs.append(_dot(dsb, kw) * ATT_SCALE)
                dk_h = _dot_tn(qmh[hh], dsb)
                dv_h = _dot_tn(dom[hh], e[hh].astype(BF16))
                dk = dk_h if dk is None else dk + dk_h
                dv = dv_h if dv is None else dv + dv_h
            dqkv_ref[0, pl.ds(qs, ATT_CQ), :] = jnp.where(lane < ATT_DH, dqs[0], dqs[1]).astype(BF16)
            dkacc[:, pl.ds(qs, ATT_KW)] += dk
            dvacc[:, pl.ds(qs, ATT_KW)] += dv
            return carry

        lax.fori_loop(0, n_qb, step, 0, unroll=4)
        dqkv_ref[1] = dkacc[:, BAND_PAST:].T.astype(BF16)
        dqkv_ref[2] = dvacc[:, BAND_PAST:].T.astype(BF16)

        @pl.when(pl.program_id(1) == B - 1)
        def _():
            r = lax.broadcasted_iota(jnp.int32, (ATT_CQ, ATT_CQ), 0)
            c = lax.broadcasted_iota(jnp.int32, (ATT_CQ, ATT_CQ), 1)
            reverse = jnp.where(r + c == ATT_CQ - 1, 1.0, 0.0).astype(BF16)
            for hh in range(2):
                x = jnp.concatenate([db_ref[hh], jnp.zeros((ATT_CQ, ATT_RING - ATT_KW), F32)], axis=1)
                flipped = jnp.zeros((ATT_CQ, ATT_RING), F32)
                for _ in range(3):
                    part = x.astype(BF16)
                    flipped = flipped + _dot(reverse, part)
                    x = x - part.astype(F32)
                aligned = pltpu.roll(flipped, ATT_KW + 1, 1, stride=1, stride_axis=0)
                dring_ref[hh] = jnp.sum(aligned.reshape(ATT_CQ // 8, 8, ATT_RING), axis=0)

    qkv_spec = pl.BlockSpec((3, S, LANES), lambda hp, b: (0, b, hp))
    return _call(body, name=name, grid=(ATT_PAIRS, B),
                 in_specs=[qkv_spec, _RING_SPEC, _BAND_SPEC, pl.BlockSpec((S, LANES), lambda hp, b: (b, hp))],
                 out_specs=[qkv_spec, _RING_SPEC],
                 out_shape=[jax.ShapeDtypeStruct((3, T, D_MODEL), BF16),
                            jax.ShapeDtypeStruct((ATT_PAIRS, 2, 8, ATT_RING), F32)],
                 scratch_shapes=[pltpu.VMEM((S + BAND_PAST, LANES), BF16), pltpu.VMEM((S + BAND_PAST, LANES), BF16),
                                 pltpu.VMEM((2, S, LANES), BF16),
                                 pltpu.VMEM((LANES, S + BAND_PAST), F32), pltpu.VMEM((LANES, S + BAND_PAST), F32),
                                 pltpu.VMEM((ATT_VARIANTS, 2, ATT_CQ, ATT_KW), F32), pltpu.VMEM((2, ATT_CQ, ATT_KW), F32)],
                 args=(qkv3, ring, jnp.asarray(_band_masks()), do), ride=ride)


def _rel_bias_grad(dring, name):
    fold = np.zeros((ATT_RING, REL_PAD), np.float32)
    fold[np.arange(ATT_RING), _ring_index()] = 1.0
    fold = jnp.asarray(fold, BF16)

    def body(d_ref, f_ref, o_ref):
        x = jnp.sum(d_ref[...], axis=0, keepdims=True)
        x = jnp.broadcast_to(x, (8, ATT_RING))
        acc = jnp.zeros((8, REL_PAD), F32)
        for _ in range(3):
            part = x.astype(BF16)
            acc = acc + _dot(part, f_ref[...])
            x = x - part.astype(F32)
        o_ref[...] = acc[0:1, :]

    out = _call(body, name=name, grid=(ATT_HEADS,),
                in_specs=[pl.BlockSpec((None, 8, ATT_RING), lambda h: (h, 0, 0)),
                          pl.BlockSpec((ATT_RING, REL_PAD), lambda h: (0, 0))],
                out_specs=[pl.BlockSpec((None, 1, REL_PAD), lambda h: (h, 0, 0))],
                out_shape=[jax.ShapeDtypeStruct((ATT_HEADS, 1, REL_PAD), F32)],
                args=(dring.reshape(ATT_HEADS, 8, ATT_RING), fold))[0]
    return out.reshape(ATT_HEADS, REL_PAD)


def _adamw(w, g, m, v):
    m = ADAM_B1 * m + (1.0 - ADAM_B1) * g
    v = ADAM_B2 * v + (1.0 - ADAM_B2) * (g * g)
    m_hat = m / (1.0 - ADAM_B1 ** ADAM_STEP)
    v_hat = v / (1.0 - ADAM_B2 ** ADAM_STEP)
    delta = -ADAM_LR * (m_hat / (jnp.sqrt(v_hat) + ADAM_EPS) + ADAM_WD * w)
    return delta, m, v


def _sum_devices(ref):
    g = ref[0].astype(F32)
    for d in range(1, N_DEV):
        g = g + ref[d].astype(F32)
    return g


def _adamw_reduce(lands, w, m, v, name, tr=256):
    L, R, C = w.shape
    tr = min(tr, R)
    n_i = R // tr

    def body(*refs):
        l_refs = refs[:L]
        w_ref, m_ref, v_ref, g_out, d_out, m_out, v_out = refs[L:]
        layer = pl.program_id(0)
        for l in range(L):
            @pl.when(layer == l)
            def _(l=l):
                g = _sum_devices(l_refs[l])
                g_out[...] = g
                d_out[...], m_out[...], v_out[...] = _adamw(w_ref[...], g, m_ref[...], v_ref[...])

    def land_spec(l):
        return pl.BlockSpec((N_DEV, tr, C), lambda ly, i: (0, jnp.where(ly == l, i, jnp.where(ly < l, 0, n_i - 1)), 0))

    blk = pl.BlockSpec((None, tr, C), lambda ly, i: (ly, i, 0))
    return _call(body, name=name, grid=(L, n_i), in_specs=[land_spec(l) for l in range(L)] + [blk, blk, blk],
                 out_specs=[blk] * 4, out_shape=[jax.ShapeDtypeStruct((L, R, C), F32)] * 4, args=(*lands, w, m, v))


def _small_update(land_small, land_rel, w, m, v, name):
    def body(ls_ref, lr_ref, w_ref, m_ref, v_ref, g_out, d_out, m_out, v_out, rel_out):
        g = _sum_devices(ls_ref)
        g_out[...] = g
        d_out[...], m_out[...], v_out[...] = _adamw(w_ref[...], g, m_ref[...], v_ref[...])
        rel_out[...] = _sum_devices(lr_ref)

    return pl.pallas_call(
        body, name=name,
        out_shape=[jax.ShapeDtypeStruct(w.shape, F32)] * 4 + [jax.ShapeDtypeStruct(land_rel.shape[1:], F32)],
    )(land_small, land_rel, w, m, v)


def _adamw_plain(w, g, m, v, name):
    def body(w_ref, g_ref, m_ref, v_ref, d_out, m_out, v_out):
        d_out[...], m_out[...], v_out[...] = _adamw(w_ref[...], g_ref[...], m_ref[...], v_ref[...])

    return pl.pallas_call(body, name=name, out_shape=[jax.ShapeDtypeStruct(w.shape, F32)] * 3)(w, g, m, v)


def _ret_piece_of_column_block(p):
    per_head = RET_HEAD_COLS // RET_DK
    qk = jnp.where(p < RET_HEADS, per_head * p, per_head * (p - RET_HEADS) + 1)
    pv = p - 2 * RET_HEADS
    vv = per_head * (pv // 2) + 2 + pv % 2
    pg = p - 4 * RET_HEADS
    gg = per_head * (pg // 2) + 4 + pg % 2
    return jnp.where(p < 2 * RET_HEADS, qk, jnp.where(p < 4 * RET_HEADS, vv, gg))


def _step(x, target, shards, mix_g, gn_g, rel_shard, mlp_g, final_g):
    s_rin, s_rout, s_ain, s_aout, s_w1, s_w2 = shards
    B, S, _ = x.shape
    T = B * S
    tk = min(T, 2048)
    n_k = T // tk
    h0 = x.reshape(T, D_MODEL)
    tgt = target.reshape(T, D_MODEL)
    consts = _ret_constants(S)
    w1_cols = w2_rows = D_FF // N_DEV

    rel_cols = REL_TABLE // N_DEV
    per_shard = RET_IN // N_DEV // RET_DK

    def head_layout(r):
        return lambda ref, blk: ref.at[:, pl.ds(pl.multiple_of(
            _ret_piece_of_column_block(per_shard * blk + r) * RET_DK, LANES), RET_DK)]

    w_rin_heads, rel_all = _ChipGather(
        [s_rin[:, r * RET_DK:(r + 1) * RET_DK] for r in range(per_shard)] + [rel_shard],
        [jax.ShapeDtypeStruct((D_MODEL, RET_IN), BF16), jax.ShapeDtypeStruct((N_DEV, ATT_HEADS, rel_cols), F32)],
        [0] * per_shard + [1], [head_layout(r) for r in range(per_shard)] + [_slot]).run("gather_first")
    ring = _bias_ring(rel_all.transpose(1, 0, 2).reshape(ATT_HEADS, REL_TABLE))

    def gather(*items):
        return _ChipGather([s for s, _, _ in items], [jax.ShapeDtypeStruct(full, BF16) for _, _, full in items],
                           list(range(len(items))), [view for _, view, _ in items])

    def scatter(*items):
        return _Exchange([p for p, _, _ in items], [(a, view, a, _slot) for a, (_, view, _) in enumerate(items)],
                         [jax.ShapeDtypeStruct((N_DEV, *shard), BF16) for _, _, shard in items])

    w1_full, w2_full = (D_MODEL, D_FF), (D_FF, D_MODEL)
    w1_shard, w2_shard = (D_MODEL, w1_cols), (w2_rows, D_MODEL)

    ride = gather((s_rout, _rows(RET_VW // N_DEV), (RET_VW, D_MODEL)), (s_w1[0], _cols(w1_cols), w1_full))
    (proj, n0), (w_rout, w1_0) = _ret_proj(h0, mix_g[0:1], w_rin_heads, consts[0], consts[1], S, "ret_proj", 512, ride=ride)
    (y, o, states), (w2_0,) = _ret_fwd(proj, consts, gn_g, B, S, "ret_fwd", ride=gather((s_w2[0], _rows(w2_rows), w2_full)))
    h1 = _matmul_res(y, w_rout, h0, "ret_out")
    ride = gather((s_ain, _cols(3 * D_MODEL // N_DEV), (D_MODEL, 3 * D_MODEL)),
                  (s_aout, _rows(D_MODEL // N_DEV), (D_MODEL, D_MODEL)))
    (h2, n1, u1), (w_ain, w_aout) = _mlp_fwd(h1, mlp_g[0:1], w1_0, w2_0, "mlp0_fwd", ride=ride)
    qkv3, n2 = _norm_proj(h2, mix_g[1:2], w_ain, 3, "att_proj", 1024)
    ride = gather((s_w1[1], _cols(w1_cols), w1_full), (s_w2[1], _rows(w2_rows), w2_full))
    (o2,), (w1_1, w2_1) = _attn_fwd(qkv3, ring, B, S, "att_fwd", ride=ride)
    h3 = _matmul_res(o2, w_aout, h2, "att_out")
    h4, n3, u3 = _mlp_fwd(h3, mlp_g[1:2], w1_1, w2_1, "mlp1_fwd")
    dh4, dh4b, loss, dg_final = _final_loss(h4, final_g, tgt, "final_loss")

    def tok(width):
        return dict(a=pl.BlockSpec((tk, width), lambda i, j, k: (k, i)), b=pl.BlockSpec((tk, width), lambda i, j, k: (k, j)))

    def mlp_dw2(u, dhb, tag, ride=None):
        return _wgrad(u, tok(1024)["a"], dhb, tok(1024)["b"], D_FF, D_MODEL, 1024, 1024, n_k, tag + "_dw2",
                      square_a=True, ride=ride)

    def mlp_dw1(n, du, tag):
        return _wgrad(n, tok(1024)["a"], du, tok(1024)["b"], D_MODEL, D_FF, 1024, 1024, n_k, tag + "_dw1")

    dh3, dh3b, du3, dg_mlp1 = _mlp_bwd(dh4, u3, w1_1, w2_1, h3, mlp_g[1:2], "mlp1_bwd")
    gw2_1, gw1_1 = mlp_dw2(u3, dh4b, "mlp1"), mlp_dw1(n3, du3, "mlp1")
    do2 = _matmul_nt(dh3b, w_aout, "att_out_bwd")
    g_aout = _wgrad(o2, tok(1024)["a"], dh3b, tok(1024)["b"], D_MODEL, D_MODEL, 1024, 1024, n_k, "att_out_dw")
    ride = scatter((gw1_1, _cols(w1_cols), w1_shard), (gw2_1, _rows(w2_rows), w2_shard),
                   (g_aout, _rows(D_MODEL // N_DEV), (D_MODEL // N_DEV, D_MODEL)))
    (dqkv3, dring), (l_w1_1, l_w2_1, l_aout) = _attn_bwd(qkv3, ring, do2, B, S, "att_bwd", ride=ride)
    g_rel = _rel_bias_grad(dring, "rel_bias_grad")
    dh2, dh2b, dg_mix1 = _proj_bwd(dqkv3, w_ain, h2, mix_g[1:2], dh3, "att_proj_bwd", 1024)
    g_ain = _wgrad(n2, tok(1024)["a"], dqkv3, pl.BlockSpec((None, tk, D_MODEL), lambda i, j, k: (j, k, 0)),
                   D_MODEL, 3 * D_MODEL, 1024, 1024, n_k, "att_proj_dw")
    ride = scatter((g_ain, _cols(3 * D_MODEL // N_DEV), (D_MODEL, 3 * D_MODEL // N_DEV)))
    (dh1, dh1b, du1, dg_mlp0), (l_ain,) = _mlp_bwd(dh2, u1, w1_0, w2_0, h1, mlp_g[0:1], "mlp0_bwd", ride=ride)
    dy = _matmul_nt(dh1b, w_rout, "ret_out_bwd")
    g_rout = _wgrad(y, tok(1024)["a"], dh1b, tok(1024)["b"], RET_VW, D_MODEL, 1024, 1024, n_k, "ret_out_dw")
    gw2_0, (l_rout,) = mlp_dw2(u1, dh2b, "mlp0", ride=scatter((g_rout, _rows(RET_VW // N_DEV), (RET_VW // N_DEV, D_MODEL))))
    gw1_0 = mlp_dw1(n1, du1, "mlp0")
    ride = scatter((gw2_0, _rows(w2_rows), w2_shard), (gw1_0, _cols(w1_cols), w1_shard))
    (dproj, dgn), (l_w2_0, l_w1_0) = _ret_bwd(proj, consts, gn_g, o, states, dy, B, S, "ret_bwd", ride=ride)
    g_rin = _wgrad(n0, tok(1024)["a"], dproj,
                   [pl.BlockSpec((tk, RET_DK), functools.partial(
                       lambda i, j, k, r: (k, _ret_piece_of_column_block(per_shard * j + r)), r=r)) for r in range(per_shard)],
                   D_MODEL, RET_IN, 1024, per_shard * RET_DK, n_k, "ret_proj_dw")
    ride = scatter((g_rin, _cols(RET_IN // N_DEV), (D_MODEL, RET_IN // N_DEV)))
    (dx, _, dg_mix0), (l_rin,) = _proj_bwd(dproj, w_rin_heads, h0, mix_g[0:1], dh1, "ret_proj_bwd", 512, ride=ride)

    small = jnp.concatenate([dg_mix0, dg_mix1, dg_mlp0, dg_mlp1, dg_final, dgn.reshape(2, D_MODEL),
                             jnp.broadcast_to(loss[0:1, 0:1], (1, D_MODEL))], axis=0)
    l_small, l_rel = _Exchange(
        [small, g_rel], [(0, _whole, 0, _slot), (1, _whole, 1, _slot)],
        [jax.ShapeDtypeStruct((N_DEV, 8, D_MODEL), F32), jax.ShapeDtypeStruct((N_DEV, ATT_HEADS, REL_PAD), F32)],
    ).run("scatter_small")
    lands = dict(rin=l_rin, rout=l_rout, ain=l_ain, aout=l_aout, w1=(l_w1_0, l_w1_1), w2=(l_w2_0, l_w2_1),
                 small=l_small, rel=l_rel)
    return dx.reshape(B, S, D_MODEL), lands


def kernel(x, mix_norm_g, ret_w_in, ret_gn_g, ret_w_out, att_w_in, att_rel_bias, att_w_out, mlp_norm_g, mlp_w1, mlp_w2, final_norm_g, loss_target, m_mix_norm_g, m_ret_w_in, m_ret_gn_g, m_ret_w_out, m_att_w_in, m_att_rel_bias, m_att_w_out, m_mlp_norm_g, m_mlp_w1, m_mlp_w2, m_final_norm_g, v_mix_norm_g, v_ret_w_in, v_ret_gn_g, v_ret_w_out, v_att_w_in, v_att_rel_bias, v_att_w_out, v_mlp_norm_g, v_mlp_w1, v_mlp_w2, v_final_norm_g):
    me = _lin(_place())
    rel_cols = REL_TABLE // N_DEV
    shards = (ret_w_in[0].astype(BF16), ret_w_out[0].astype(BF16), att_w_in[0].astype(BF16), att_w_out[0].astype(BF16),
              (mlp_w1[0].astype(BF16), mlp_w1[1].astype(BF16)), (mlp_w2[0].astype(BF16), mlp_w2[1].astype(BF16)))
    grad_x, lands = _step(x, loss_target, shards, mix_norm_g, ret_gn_g, att_rel_bias[0], mlp_norm_g,
                          final_norm_g.reshape(1, D_MODEL))

    u_rin = _adamw_reduce([lands["rin"]], ret_w_in, m_ret_w_in, v_ret_w_in, "update_ret_w_in")
    u_rout = _adamw_reduce([lands["rout"]], ret_w_out, m_ret_w_out, v_ret_w_out, "update_ret_w_out")
    u_ain = _adamw_reduce([lands["ain"]], att_w_in, m_att_w_in, v_att_w_in, "update_att_w_in")
    u_aout = _adamw_reduce([lands["aout"]], att_w_out, m_att_w_out, v_att_w_out, "update_att_w_out")
    u_w1 = _adamw_reduce(lands["w1"], mlp_w1, m_mlp_w1, v_mlp_w1, "update_mlp_w1")
    u_w2 = _adamw_reduce(lands["w2"], mlp_w2, m_mlp_w2, v_mlp_w2, "update_mlp_w2")

    def pack(mix, mlp, fin, gn):
        return jnp.concatenate([mix, mlp, fin.reshape(1, D_MODEL), gn.reshape(2, D_MODEL), jnp.zeros((1, D_MODEL), F32)], axis=0)

    small_w = pack(mix_norm_g, mlp_norm_g, final_norm_g, ret_gn_g)
    small_m = pack(m_mix_norm_g, m_mlp_norm_g, m_final_norm_g, m_ret_gn_g)
    small_v = pack(v_mix_norm_g, v_mlp_norm_g, v_final_norm_g, v_ret_gn_g)
    sg, sd, sm, sv, rel_sum = _small_update(lands["small"], lands["rel"], small_w, small_m, small_v, "update_small")
    g_rel_mine = lax.dynamic_slice(rel_sum, (0, me * rel_cols), (ATT_HEADS, rel_cols))
    rel_d, rel_m, rel_v = _adamw_plain(att_rel_bias[0], g_rel_mine, m_att_rel_bias[0], v_att_rel_bias[0], "update_rel_bias")
    u_rel = [g_rel_mine[None], rel_d[None], rel_m[None], rel_v[None]]

    def unpack(t):
        return dict(mix=t[0:2], mlp=t[2:4], fin=t[4], gn=t[5:7].reshape(1, RET_VW))

    us = [unpack(t) for t in (sg, sd, sm, sv)]
    outs = [sg[7, 0], grad_x]
    for k in range(4):
        outs += [us[k]["mix"], u_rin[k], us[k]["gn"], u_rout[k], u_ain[k], u_rel[k], u_aout[k], us[k]["mlp"],
                 u_w1[k], u_w2[k], us[k]["fin"]]
    return tuple(outs)
```

```python
import functools

import numpy as np
import jax
import jax.numpy as jnp
from jax import lax
from jax.experimental import pallas as pl
from jax.experimental.pallas import tpu as pltpu

F32, BF16 = jnp.float32, jnp.bfloat16

D_MODEL = 1024
CHUNK = 64
RET_HEADS, RET_DK, RET_DV = 4, 256, 512
RET_QK, RET_VW = RET_HEADS * RET_DK, RET_HEADS * RET_DV
RET_IN = 2 * RET_QK + 2 * RET_VW
RET_HEAD_COLS = 2 * RET_DK + 2 * RET_DV
RET_SC = 256
ROPE_BASE = 10000.0
ATT_HEADS, ATT_DH = 16, 64
LANES = 128
ATT_PAIRS = ATT_HEADS * ATT_DH // LANES
BAND_PAST = 8 * CHUNK
MAX_REL = 256
REL_TABLE = MAX_REL + CHUNK
REL_PAD = 384
ATT_CQ = 256
ATT_KW = ATT_CQ + BAND_PAST
ATT_RING = ATT_CQ + ATT_KW
ATT_VARIANTS = BAND_PAST // ATT_CQ + 1
D_FF = 4 * D_MODEL
EPS = 1e-6
EPILOGUE_ROWS = 256
NEG = -1e30
N_DEV = 8
N_PEERS = N_DEV - 1

ADAM_LR, ADAM_B1, ADAM_B2, ADAM_EPS, ADAM_WD, ADAM_STEP = 0.001, 0.9, 0.999, 1e-08, 0.01, 10

VMEM_LIMIT = 56 * 1024 * 1024
MESH = pl.DeviceIdType.MESH
ANY = pl.BlockSpec(memory_space=pl.ANY)


def _dot(a, b):
    return jnp.dot(a, b, preferred_element_type=F32)


def _dot_nt(a, b):
    return lax.dot_general(a, b, (((1,), (1,)), ((), ())), preferred_element_type=F32)


def _dot_tn(a, b):
    return lax.dot_general(a, b, (((0,), (0,)), ((), ())), preferred_element_type=F32)


def _rms(x, g):
    r = lax.rsqrt(jnp.mean(x * x, axis=-1, keepdims=True) + EPS)
    return x * r * g


def _rms_bwd(dn, x, g):
    r = lax.rsqrt(jnp.mean(x * x, axis=-1, keepdims=True) + EPS)
    xh = x * r
    dg = jnp.sum(dn * xh, axis=0, keepdims=True)
    dxh = dn * g
    dx = r * (dxh - xh * jnp.mean(dxh * xh, axis=-1, keepdims=True))
    return dx, dg


def _rms_bwd_epilogue(x_ref, g_ref, dres_ref, dx_ref, dxb_ref, dg_ref):
    for r in range(0, dx_ref.shape[0], EPILOGUE_ROWS):
        rows = slice(r, r + EPILOGUE_ROWS)
        dx, dg = _rms_bwd(dx_ref[rows, :], x_ref[rows, :], g_ref[...])
        dx = dres_ref[rows, :] + dx
        dx_ref[rows, :] = dx
        dxb_ref[rows, :] = dx.astype(BF16)
        dg_ref[...] += dg


def _place():
    return lax.axis_index("x"), lax.axis_index("y"), lax.axis_index("c")


def _lin(p):
    return 4 * p[0] + 2 * p[1] + p[2]


def _cols(width):
    return lambda ref, blk: ref.at[:, pl.ds(pl.multiple_of(blk * width, LANES), width)]


def _rows(height):
    return lambda ref, blk: ref.at[pl.ds(pl.multiple_of(blk * height, 8), height), :]


def _slot(ref, blk):
    return ref.at[blk]


def _whole(ref, blk):
    return ref


class _Exchange:
    def __init__(self, sources, flows, land_shapes):
        self.sources, self.flows, self.land_shapes = list(sources), flows, list(land_shapes)
        n_f = len(flows)
        self.sem_shapes = [pltpu.SemaphoreType.DMA((N_PEERS * n_f,)), pltpu.SemaphoreType.DMA((N_PEERS * n_f,)),
                           pltpu.SemaphoreType.DMA((n_f,))]

    def _copies(self, srcs, lands, sems):
        send_sems, recv_sems, local_sems = sems
        x, y, c = _place()
        me = (x, y, c)
        peers = [(x ^ dx, y ^ dy, c ^ dc) for dx in (0, 1) for dy in (0, 1) for dc in (0, 1)][1:]

        def copy(f, k, sender, to):
            si, sview, li, lview = self.flows[f]
            return pltpu.make_async_remote_copy(
                src_ref=sview(srcs[si], _lin(to)), dst_ref=lview(lands[li], _lin(sender)),
                send_sem=send_sems.at[f * N_PEERS + k], recv_sem=recv_sems.at[f * N_PEERS + k],
                device_id=to, device_id_type=MESH)

        mine, sends, recvs = [], [], []
        for f, (si, sview, li, lview) in enumerate(self.flows):
            mine.append(pltpu.make_async_copy(sview(srcs[si], _lin(me)), lview(lands[li], _lin(me)), local_sems.at[f]))
            for k, peer in enumerate(peers):
                sends.append(copy(f, k, me, peer))
                recvs.append(copy(f, k, peer, me))
        return mine, sends, recvs

    def start(self, srcs, lands, sems):
        mine, sends, _ = self._copies(srcs, lands, sems)
        for cp in mine + sends:
            cp.start()

    def finish(self, srcs, lands, sems):
        mine, sends, recvs = self._copies(srcs, lands, sems)
        for cp in recvs:
            cp.wait_recv()
        for cp in sends:
            cp.wait_send()
        for cp in mine:
            cp.wait()

    def run(self, name):
        n_src, n_land = len(self.sources), len(self.land_shapes)

        def body(*refs):
            srcs, lands, sems = refs[:n_src], refs[n_src:n_src + n_land], refs[n_src + n_land:]
            self.start(srcs, lands, sems)
            self.finish(srcs, lands, sems)

        return pl.pallas_call(body, name=name, in_specs=[ANY] * n_src, out_specs=[ANY] * n_land,
                              out_shape=self.land_shapes, scratch_shapes=self.sem_shapes)(*self.sources)


def _call(body, *, name, grid, in_specs, out_specs, out_shape, args, scratch_shapes=(), ride=None):
    params = pltpu.CompilerParams(dimension_semantics=("arbitrary",) * len(grid), vmem_limit_bytes=VMEM_LIMIT)
    in_specs, out_specs, out_shape, scratch_shapes = list(in_specs), list(out_specs), list(out_shape), list(scratch_shapes)
    if ride is None:
        return pl.pallas_call(body, name=name, grid=grid, in_specs=in_specs, out_specs=out_specs, out_shape=out_shape,
                              scratch_shapes=scratch_shapes, compiler_params=params)(*args)
    n_in, n_out, n_scr = len(in_specs), len(out_specs), len(scratch_shapes)
    n_src, n_land = len(ride.sources), len(ride.land_shapes)

    def riding(*refs):
        bounds = np.cumsum([n_in, n_src, n_out, n_land, n_scr])
        ins, srcs, outs, lands, scr, sems = (refs[a:b] for a, b in zip([0, *bounds], [*bounds, len(refs)]))
        step = functools.reduce(lambda a, d: a * grid[d] + pl.program_id(d), range(len(grid)), 0)
        n_steps = int(np.prod(grid))

        @pl.when(step == 0)
        def _():
            ride.start(srcs, lands, sems)

        if hasattr(ride, "middle"):
            @pl.when(step == (5 * n_steps) // 8)
            def _():
                ride.middle(srcs, lands, sems)

        body(*ins, *outs, *scr)

        @pl.when(step == n_steps - 1)
        def _():
            ride.finish(srcs, lands, sems)

    res = pl.pallas_call(
        riding, name=name, grid=grid, in_specs=in_specs + [ANY] * n_src, out_specs=out_specs + [ANY] * n_land,
        out_shape=out_shape + ride.land_shapes, scratch_shapes=scratch_shapes + ride.sem_shapes,
        compiler_params=params)(*args, *ride.sources)
    return res[:n_out], res[n_out:]


class _ChipGather:
    def __init__(self, shards, out_shapes, out_of, views):
        self.sources, self.land_shapes, self.out_of, self.views = list(shards), list(out_shapes), out_of, views
        n = len(shards)
        self.sem_shapes = [pltpu.SemaphoreType.DMA((N_PEERS * n,)), pltpu.SemaphoreType.DMA((N_PEERS * n,)),
                           pltpu.SemaphoreType.DMA((n,))]

    def _copies(self, srcs, lands, sems):
        send_sems, recv_sems, local_sems = sems
        n = len(srcs)
        outs = [lands[self.out_of[a]] for a in range(n)]
        x, y, c = _place()
        me, sibling = (x, y, c), (x, y, 1 - c)
        chips = [(1 - x, y), (x, 1 - y), (1 - x, 1 - y)]

        def copy(a, k, block, to, own=False):
            dst = self.views[a](outs[a], _lin(block))
            return pltpu.make_async_remote_copy(
                src_ref=srcs[a] if own else dst, dst_ref=dst,
                send_sem=send_sems.at[a * N_PEERS + k], recv_sem=recv_sems.at[a * N_PEERS + k],
                device_id=to, device_id_type=MESH)

        mine = lambda: [pltpu.make_async_copy(srcs[a], self.views[a](outs[a], _lin(me)), local_sems.at[a]) for a in range(n)]
        first = lambda: [cp for a in range(n) for cp in
                         [copy(a, 0, me, sibling, own=True)] + [copy(a, 1 + j, me, (*chip, c), own=True) for j, chip in enumerate(chips)]]
        landed = lambda a, j: copy(a, 1 + j, (*chips[j], c), me)
        passed = lambda a, j: copy(a, 4 + j, (*chips[j], c), sibling)
        from_sibling = lambda a: [copy(a, 0, sibling, me)] + [copy(a, 4 + j, (*chips[j], 1 - c), me) for j in range(3)]
        return n, mine, first, landed, passed, from_sibling

    def start(self, srcs, lands, sems):
        _, mine, first, _, _, _ = self._copies(srcs, lands, sems)
        for cp in mine() + first():
            cp.start()

    def middle(self, srcs, lands, sems):
        n, _, _, landed, passed, _ = self._copies(srcs, lands, sems)
        for j in range(3):
            for a in range(n):
                landed(a, j).wait_recv()
                passed(a, j).start()

    def finish(self, srcs, lands, sems):
        n, mine, first, _, passed, from_sibling = self._copies(srcs, lands, sems)
        for a in range(n):
            for cp in from_sibling(a):
                cp.wait_recv()
        for cp in first() + [passed(a, j) for j in range(3) for a in range(n)]:
            cp.wait_send()
        for cp in mine():
            cp.wait()

    def run(self, name):
        n, n_out = len(self.sources), len(self.land_shapes)

        def body(*refs):
            parts = refs[:n], refs[n:n + n_out], refs[n + n_out:]
            self.start(*parts)
            self.middle(*parts)
            self.finish(*parts)

        return pl.pallas_call(body, name=name, in_specs=[ANY] * n, out_specs=[ANY] * n_out,
                              out_shape=self.land_shapes, scratch_shapes=self.sem_shapes)(*self.sources)


def _resident(shape):
    return pl.BlockSpec(shape, lambda i: (0,) * len(shape), pipeline_mode=pl.Buffered(1))


def _norm_proj(h, g, w, groups, name, tm, ride=None):
    T = h.shape[0]
    N = w.shape[1]
    width = N // groups

    def body(h_ref, g_ref, w_ref, o_ref, n_ref):
        n = _rms(h_ref[...], g_ref[...]).astype(BF16)
        n_ref[...] = n
        if groups == 1:
            o_ref[...] = _dot(n, w_ref[...]).astype(BF16)
        else:
            for p in range(groups):
                o_ref[p] = _dot(n, w_ref[:, p * width:(p + 1) * width]).astype(BF16)

    row = pl.BlockSpec((tm, D_MODEL), lambda i: (i, 0))
    if groups == 1:
        out_spec, out_shape = pl.BlockSpec((tm, N), lambda i: (i, 0)), jax.ShapeDtypeStruct((T, N), BF16)
    else:
        out_spec = pl.BlockSpec((groups, tm, width), lambda i: (0, i, 0))
        out_shape = jax.ShapeDtypeStruct((groups, T, width), BF16)
    return _call(body, name=name, grid=(T // tm,),
                 in_specs=[row, pl.BlockSpec((1, D_MODEL), lambda i: (0, 0)), _resident(w.shape)],
                 out_specs=[out_spec, row], out_shape=[out_shape, jax.ShapeDtypeStruct((T, D_MODEL), BF16)],
                 args=(h, g, w), ride=ride)


def _ret_proj(h, g, w, cos, sin, S, name, tm, ride=None):
    T = h.shape[0]
    per_seq = S // tm

    def body(h_ref, g_ref, w_ref, cos_ref, sin_ref, o_ref, n_ref):
        n = _rms(h_ref[...], g_ref[...]).astype(BF16)
        n_ref[...] = n
        cs, sn = cos_ref[...], sin_ref[...]
        for hd in range(RET_HEADS):
            c0 = hd * RET_HEAD_COLS
            out = _dot(n, w_ref[:, c0:c0 + RET_HEAD_COLS])
            o_ref[:, c0:c0 + RET_DK] = _rope(out[:, 0:RET_DK], cs, sn).astype(BF16)
            o_ref[:, c0 + RET_DK:c0 + 2 * RET_DK] = (_rope(out[:, RET_DK:2 * RET_DK], cs, sn) * (RET_DK ** -0.5)).astype(BF16)
            o_ref[:, c0 + 2 * RET_DK:c0 + RET_HEAD_COLS] = out[:, 2 * RET_DK:].astype(BF16)

    row = pl.BlockSpec((tm, D_MODEL), lambda i: (i, 0))
    trig = pl.BlockSpec((tm, RET_DK // 2), lambda i: (i % per_seq, 0))
    return _call(body, name=name, grid=(T // tm,),
                 in_specs=[row, pl.BlockSpec((1, D_MODEL), lambda i: (0, 0)), _resident(w.shape), trig, trig],
                 out_specs=[pl.BlockSpec((tm, w.shape[1]), lambda i: (i, 0)), row],
                 out_shape=[jax.ShapeDtypeStruct((T, w.shape[1]), BF16), jax.ShapeDtypeStruct((T, D_MODEL), BF16)],
                 args=(h, g, w, cos, sin), ride=ride)


def _proj_bwd(dy, w, x, g, dres, name, tm, ride=None):
    T = x.shape[0]
    groups = dy.shape[0] if dy.ndim == 3 else 1
    width = w.shape[1] // groups

    def body(dy_ref, w_ref, x_ref, g_ref, dres_ref, dx_ref, dxb_ref, dg_ref):
        @pl.when(pl.program_id(0) == 0)
        def _():
            dg_ref[...] = jnp.zeros_like(dg_ref)

        if groups == 1:
            dn = _dot_nt(dy_ref[...], w_ref[...])
        else:
            dn = sum(_dot_nt(dy_ref[p], w_ref[:, p * width:(p + 1) * width]) for p in range(groups))
        dx_ref[...] = dn
        _rms_bwd_epilogue(x_ref, g_ref, dres_ref, dx_ref, dxb_ref, dg_ref)

    row = pl.BlockSpec((tm, D_MODEL), lambda i: (i, 0))
    vec = pl.BlockSpec((1, D_MODEL), lambda i: (0, 0))
    dy_spec = (pl.BlockSpec((tm, w.shape[1]), lambda i: (i, 0)) if groups == 1
               else pl.BlockSpec((groups, tm, width), lambda i: (0, i, 0)))
    return _call(body, name=name, grid=(T // tm,),
                 in_specs=[dy_spec, _resident(w.shape), row, vec, row], out_specs=[row, row, vec],
                 out_shape=[jax.ShapeDtypeStruct((T, D_MODEL), F32), jax.ShapeDtypeStruct((T, D_MODEL), BF16),
                            jax.ShapeDtypeStruct((1, D_MODEL), F32)],
                 args=(dy, w, x, g, dres), ride=ride)


def _matmul_res(a, w, res, name, tm=1024):
    T, K = a.shape

    def body(a_ref, w_ref, r_ref, o_ref):
        o_ref[...] = r_ref[...] + _dot(a_ref[...], w_ref[...])

    row = pl.BlockSpec((tm, D_MODEL), lambda i: (i, 0))
    return _call(body, name=name, grid=(T // tm,),
                 in_specs=[pl.BlockSpec((tm, K), lambda i: (i, 0)), pl.BlockSpec((K, D_MODEL), lambda i: (0, 0)), row],
                 out_specs=[row], out_shape=[jax.ShapeDtypeStruct((T, D_MODEL), F32)], args=(a, w, res))[0]


def _matmul_nt(dy, w, name, tm=1024):
    T, N = dy.shape
    K = w.shape[0]

    def body(dy_ref, w_ref, o_ref):
        o_ref[...] = _dot_nt(dy_ref[...], w_ref[...]).astype(BF16)

    return _call(body, name=name, grid=(T // tm,),
                 in_specs=[pl.BlockSpec((tm, N), lambda i: (i, 0)), pl.BlockSpec((K, N), lambda i: (0, 0))],
                 out_specs=[pl.BlockSpec((tm, K), lambda i: (i, 0))], out_shape=[jax.ShapeDtypeStruct((T, K), BF16)],
                 args=(dy, w))[0]


def _wgrad(a, a_spec, b, b_specs, m, n, bm, bn, n_k, name, square_a=False, ride=None):
    b_specs = b_specs if isinstance(b_specs, (list, tuple)) else [b_specs]
    n_b = len(b_specs)

    def body(a_ref, *rest):
        b_refs = rest[:n_b]
        o_ref, acc = rest[n_b:]
        k = pl.program_id(2)

        @pl.when(k == 0)
        def _():
            acc[...] = jnp.zeros_like(acc)

        av = a_ref[...]
        if square_a:
            af = av.astype(F32)
            av = (af * af).astype(BF16)
        bv = b_refs[0][...] if n_b == 1 else jnp.concatenate([r[...] for r in b_refs], axis=1)
        acc[...] += _dot_tn(av, bv)

        @pl.when(k == n_k - 1)
        def _():
            o_ref[...] = acc[...].astype(BF16)

    res = _call(body, name=name, grid=(m // bm, n // bn, n_k), in_specs=[a_spec, *b_specs],
                out_specs=[pl.BlockSpec((bm, bn), lambda i, j, k: (i, j))], out_shape=[jax.ShapeDtypeStruct((m, n), BF16)],
                scratch_shapes=[pltpu.VMEM((bm, bn), F32)], args=(a, *([b] * n_b)), ride=ride)
    return res[0] if ride is None else (res[0][0], res[1])


def _mlp_fwd(h, g, w1, w2, name, tm=1024, tf=1024, ride=None):
    T = h.shape[0]

    def body(h_ref, g_ref, w1_ref, w2_ref, ho_ref, n_ref, u_ref):
        @pl.when(pl.program_id(1) == 0)
        def _():
            n_ref[...] = _rms(h_ref[...], g_ref[...]).astype(BF16)
            ho_ref[...] = h_ref[...]

        u = jnp.maximum(_dot(n_ref[...], w1_ref[...]), 0.0)
        u_ref[...] = u.astype(BF16)
        ho_ref[...] += _dot((u * u).astype(BF16), w2_ref[...])

    row = pl.BlockSpec((tm, D_MODEL), lambda i, j: (i, 0))
    return _call(body, name=name, grid=(T // tm, D_FF // tf),
                 in_specs=[row, pl.BlockSpec((1, D_MODEL), lambda i, j: (0, 0)),
                           pl.BlockSpec((D_MODEL, tf), lambda i, j: (0, j)), pl.BlockSpec((tf, D_MODEL), lambda i, j: (j, 0))],
                 out_specs=[row, row, pl.BlockSpec((tm, tf), lambda i, j: (i, j))],
                 out_shape=[jax.ShapeDtypeStruct((T, D_MODEL), F32), jax.ShapeDtypeStruct((T, D_MODEL), BF16),
                            jax.ShapeDtypeStruct((T, D_FF), BF16)],
                 args=(h, g, w1, w2), ride=ride)


def _mlp_bwd(dh, u, w1, w2, h, g, name, tm=1024, tf=1024, ride=None):
    T = h.shape[0]
    n_j = D_FF // tf

    def body(dh_ref, u_ref, w1_ref, w2_ref, h_ref, g_ref, dx_ref, dxb_ref, du_ref, dg_ref, dhb):
        acc = dx_ref
        i, j = pl.program_id(0), pl.program_id(1)

        @pl.when(j == 0)
        def _():
            dhb[...] = dh_ref[...].astype(BF16)
            acc[...] = jnp.zeros_like(acc)

        @pl.when((i == 0) & (j == 0))
        def _():
            dg_ref[...] = jnp.zeros_like(dg_ref)

        da = _dot_nt(dhb[...], w2_ref[...])
        du = (da * (2.0 * u_ref[...].astype(F32))).astype(BF16)
        du_ref[...] = du
        acc[...] += _dot_nt(du, w1_ref[...])

        @pl.when(j == n_j - 1)
        def _():
            _rms_bwd_epilogue(h_ref, g_ref, dh_ref, dx_ref, dxb_ref, dg_ref)

    row = pl.BlockSpec((tm, D_MODEL), lambda i, j: (i, 0))
    vec = pl.BlockSpec((1, D_MODEL), lambda i, j: (0, 0))
    hid = pl.BlockSpec((tm, tf), lambda i, j: (i, j))
    return _call(body, name=name, grid=(T // tm, n_j),
                 in_specs=[row, hid, pl.BlockSpec((D_MODEL, tf), lambda i, j: (0, j)),
                           pl.BlockSpec((tf, D_MODEL), lambda i, j: (j, 0)), row, vec],
                 out_specs=[row, row, hid, vec],
                 out_shape=[jax.ShapeDtypeStruct((T, D_MODEL), F32), jax.ShapeDtypeStruct((T, D_MODEL), BF16),
                            jax.ShapeDtypeStruct((T, D_FF), BF16), jax.ShapeDtypeStruct((1, D_MODEL), F32)],
                 scratch_shapes=[pltpu.VMEM((tm, D_MODEL), BF16)], args=(dh, u, w1, w2, h, g), ride=ride)


def _final_loss(h, g, target, name, tm=512):
    T = h.shape[0]

    def body(h_ref, g_ref, t_ref, dh_ref, dhb_ref, loss_ref, dg_ref):
        @pl.when(pl.program_id(0) == 0)
        def _():
            loss_ref[...] = jnp.zeros_like(loss_ref)
            dg_ref[...] = jnp.zeros_like(dg_ref)

        x = h_ref[...]
        gg = g_ref[...]
        r = lax.rsqrt(jnp.mean(x * x, axis=-1, keepdims=True) + EPS)
        xh = x * r
        e = xh * gg - t_ref[...]
        per_tok = jnp.mean(e * e, axis=-1, keepdims=True)
        loss_ref[...] += 0.5 * jnp.sum(per_tok, axis=0, keepdims=True)
        dy = e * (1.0 / D_MODEL)
        dg_ref[...] += jnp.sum(dy * xh, axis=0, keepdims=True)
        dxh = dy * gg
        dx = r * (dxh - xh * jnp.mean(dxh * xh, axis=-1, keepdims=True))
        dh_ref[...] = dx
        dhb_ref[...] = dx.astype(BF16)

    row = pl.BlockSpec((tm, D_MODEL), lambda i: (i, 0))
    vec = pl.BlockSpec((1, D_MODEL), lambda i: (0, 0))
    return _call(body, name=name, grid=(T // tm,), in_specs=[row, vec, row],
                 out_specs=[row, row, pl.BlockSpec((8, LANES), lambda i: (0, 0)), vec],
                 out_shape=[jax.ShapeDtypeStruct((T, D_MODEL), F32), jax.ShapeDtypeStruct((T, D_MODEL), BF16),
                            jax.ShapeDtypeStruct((8, LANES), F32), jax.ShapeDtypeStruct((1, D_MODEL), F32)],
                 args=(h, g, target))


def _ret_constants(S):
    log_gamma = jnp.log1p(-jnp.exp2(-5.0 - jnp.arange(RET_HEADS, dtype=F32)))
    idx = jnp.arange(RET_SC, dtype=F32)
    i, j = idx[:, None], idx[None, :]
    same_chunk = jnp.floor(i / CHUNK) == jnp.floor(j / CHUNK)
    mask = jnp.where((j <= i) | same_chunk, jnp.exp(log_gamma[:, None, None] * jnp.abs(i - j)[None]), 0.0)
    qdec = jnp.exp(log_gamma[:, None] * (idx + 1.0)[None, :])[:, :, None]
    kdec = jnp.exp(log_gamma[:, None] * (RET_SC - 1 - idx)[None, :])[:, :, None]
    cdec = jnp.exp(log_gamma * RET_SC)[:, None, None]
    half = RET_DK // 2
    inv = jnp.exp(-jnp.log(ROPE_BASE) * jnp.arange(half, dtype=F32) / half)
    ang = jnp.arange(S, dtype=F32)[:, None] * inv[None, :]
    return jnp.cos(ang), jnp.sin(ang), mask.astype(F32), qdec, kdec, cdec


def _rope(t, cs, sn):
    t1, t2 = t[:, :RET_DK // 2], t[:, RET_DK // 2:]
    return jnp.concatenate([t1 * cs - t2 * sn, t1 * sn + t2 * cs], axis=-1)


def _rope_bwd(d, cs, sn):
    d1, d2 = d[:, :RET_DK // 2], d[:, RET_DK // 2:]
    return jnp.concatenate([d1 * cs + d2 * sn, d2 * cs - d1 * sn], axis=-1)


def _ret_specs(B, S, reverse):
    n_sc = S // RET_SC

    def cc(c):
        return n_sc - 1 - c if reverse else c

    return dict(
        proj=pl.BlockSpec((B, RET_SC, RET_HEAD_COLS), lambda h, c: (0, cc(c), h)),
        trig=pl.BlockSpec((RET_SC, RET_DK // 2), lambda h, c: (cc(c), 0)),
        mask=pl.BlockSpec((None, RET_SC, RET_SC), lambda h, c: (h, 0, 0)),
        dec=pl.BlockSpec((None, RET_SC, 1), lambda h, c: (h, 0, 0)),
        cdec=pl.BlockSpec((None, 1, 1), lambda h, c: (h, 0, 0)),
        gn=pl.BlockSpec((1, RET_DV), lambda h, c: (0, h)),
        val=pl.BlockSpec((B, RET_SC, RET_DV), lambda h, c: (0, cc(c), h)),
        state=pl.BlockSpec((B, None, None, RET_DK, RET_DV), lambda h, c: (0, h, cc(c), 0, 0)),
    )


def _ret_qkvg(p_ref, kdec):
    qb = p_ref[:, 0:RET_DK]
    kb = p_ref[:, RET_DK:2 * RET_DK]
    kt = (kb.astype(F32) * kdec).astype(BF16)
    v = p_ref[:, 2 * RET_DK:2 * RET_DK + RET_DV]
    gate = p_ref[:, 2 * RET_DK + RET_DV:RET_HEAD_COLS].astype(F32)
    return qb, kb, kt, v, gate


def _group_norm(o):
    mu = jnp.mean(o, axis=-1, keepdims=True)
    oc = o - mu
    rstd = lax.rsqrt(jnp.mean(oc * oc, axis=-1, keepdims=True) + EPS)
    return oc * rstd, rstd


def _ret_fwd(proj, consts, gn, B, S, name, ride=None):
    T = B * S
    n_sc = S // RET_SC
    sp = _ret_specs(B, S, False)

    def body(p_ref, m_ref, qd_ref, kd_ref, cd_ref, gn_ref, y_ref, o_ref, st_ref, state):
        @pl.when(pl.program_id(1) == 0)
        def _():
            state[...] = jnp.zeros_like(state)

        seqs = range(B)
        qkvg = [_ret_qkvg(p_ref.at[b], kd_ref[...]) for b in seqs]
        qb, kb, kt, v = ([qkvg[b][i] for b in seqs] for i in range(4))
        sc = [_dot_nt(qb[b], kb[b]) for b in seqs]
        sb = [state[b].astype(BF16) for b in seqs]
        cross = [_dot(qb[b], sb[b]) for b in seqs]
        for b in seqs:
            st_ref[b] = sb[b]
        p = [(sc[b] * m_ref[...]).astype(BF16) for b in seqs]
        o = [_dot(p[b], v[b]) + cross[b] * qd_ref[...] for b in seqs]
        upd = [_dot_tn(kt[b], v[b]) for b in seqs]
        for b in seqs:
            o_ref[b] = o[b].astype(BF16)
            ohat, _ = _group_norm(o[b])
            gate = qkvg[b][4]
            y_ref[b] =(gate * jax.nn.sigmoid(gate) * (ohat * gn_ref[...])).astype(BF16)
            state[b] = state[b] * cd_ref[...] + upd[b]

    res = _call(
        body, name=name, grid=(RET_HEADS, n_sc),
        in_specs=[sp["proj"], sp["mask"], sp["dec"], sp["dec"], sp["cdec"], sp["gn"]],
        out_specs=[sp["val"], sp["val"], sp["state"]],
        out_shape=[jax.ShapeDtypeStruct((B, S, RET_VW), BF16), jax.ShapeDtypeStruct((B, S, RET_VW), BF16),
                   jax.ShapeDtypeStruct((B, RET_HEADS, n_sc, RET_DK, RET_DV), BF16)],
        scratch_shapes=[pltpu.VMEM((B, RET_DK, RET_DV), F32)],
        args=(proj.reshape(B, S, -1), *consts[2:], gn), ride=ride)
    (y, o, states), lands = res if ride is not None else (res, None)
    outs = [y.reshape(T, RET_VW), o.reshape(T, RET_VW), states]
    return outs if ride is None else (outs, lands)


def _ret_bwd(proj, consts, gn, o, states, dy, B, S, name, ride=None):
    T = B * S
    n_sc = S // RET_SC
    sp = _ret_specs(B, S, True)

    def body(p_ref, cos_ref, sin_ref, m_ref, qd_ref, kd_ref, cd_ref, gn_ref, o_ref, st_ref, dy_ref,
             dp_ref, dgn_ref, dstate):
        @pl.when(pl.program_id(1) == 0)
        def _():
            dstate[...] = jnp.zeros_like(dstate)
            dgn_ref[...] = jnp.zeros_like(dgn_ref)

        seqs = range(B)
        cs, sn = cos_ref[...], sin_ref[...]
        m, gnv = m_ref[...], gn_ref[...]
        qkvg = [_ret_qkvg(p_ref.at[b], kd_ref[...]) for b in seqs]
        qb, kb, kt, v = ([qkvg[b][i] for b in seqs] for i in range(4))
        sc = [_dot_nt(qb[b], kb[b]) for b in seqs]
        dsb = [dstate[b].astype(BF16) for b in seqs]
        dv_state = [_dot(kt[b], dsb[b]) for b in seqs]
        dkt = [_dot_nt(v[b], dsb[b]) for b in seqs]
        do, dgate = [], []
        for b in seqs:
            gate = qkvg[b][4]
            ohat, rstd = _group_norm(o_ref[b].astype(F32))
            dyv = dy_ref[b].astype(F32)
            sg = jax.nn.sigmoid(gate)
            don = dyv * (gate * sg)
            dgate.append(dyv * (ohat * gnv) * (sg * (1.0 + gate * (1.0 - sg))))
            dgn_ref[...] += jnp.sum(don * ohat, axis=0, keepdims=True)
            dohat = don * gnv
            do.append(rstd * (dohat - jnp.mean(dohat, axis=-1, keepdims=True)
                              - ohat * jnp.mean(dohat * ohat, axis=-1, keepdims=True)))
        dob = [do[b].astype(BF16) for b in seqs]
        doq = [(do[b] * qd_ref[...]).astype(BF16) for b in seqs]
        dsc_f = [_dot_nt(dob[b], v[b]) for b in seqs]
        dq_state = [_dot_nt(doq[b], st_ref[b]) for b in seqs]
        dstate_upd = [_dot_tn(qb[b], doq[b]) for b in seqs]
        p = [(sc[b] * m).astype(BF16) for b in seqs]
        dsc = [(dsc_f[b] * m).astype(BF16) for b in seqs]
        dv = [_dot_tn(p[b], dob[b]) + dv_state[b] for b in seqs]
        dq = [_dot(dsc[b], kb[b]) + dq_state[b] for b in seqs]
        dk = [(_dot_tn(dsc[b], qb[b]) + dkt[b] * kd_ref[...]) * (RET_DK ** -0.5) for b in seqs]
        for b in seqs:
            dstate[b] = dstate[b] * cd_ref[...] + dstate_upd[b]
            dp_ref[b, :, 0:RET_DK] = _rope_bwd(dq[b], cs, sn).astype(BF16)
            dp_ref[b, :, RET_DK:2 * RET_DK] = _rope_bwd(dk[b], cs, sn).astype(BF16)
            dp_ref[b, :, 2 * RET_DK:2 * RET_DK + RET_DV] = dv[b].astype(BF16)
            dp_ref[b, :, 2 * RET_DK + RET_DV:RET_HEAD_COLS] = dgate[b].astype(BF16)

    res = _call(
        body, name=name, grid=(RET_HEADS, n_sc),
        in_specs=[sp["proj"], sp["trig"], sp["trig"], sp["mask"], sp["dec"], sp["dec"], sp["cdec"], sp["gn"],
                  sp["val"], sp["state"], sp["val"]],
        out_specs=[sp["proj"], sp["gn"]],
        out_shape=[jax.ShapeDtypeStruct((B, S, RET_HEADS * RET_HEAD_COLS), BF16), jax.ShapeDtypeStruct((1, RET_VW), F32)],
        scratch_shapes=[pltpu.VMEM((B, RET_DK, RET_DV), F32)],
        args=(proj.reshape(B, S, -1), *consts, gn, o.reshape(B, S, -1), states, dy.reshape(B, S, -1)), ride=ride)
    (dproj, dgn), lands = res if ride is not None else (res, None)
    outs = [dproj.reshape(T, -1), dgn]
    return outs if ride is None else (outs, lands)


def _ring_index():
    u = np.arange(ATT_RING)
    offset = np.where(u < ATT_KW, u, u - ATT_RING)
    return np.clip(offset - BAND_PAST, -MAX_REL, CHUNK - 1) + MAX_REL


def _bias_ring(rel):
    n_clip = BAND_PAST - MAX_REL
    n_hi = ATT_KW - n_clip - REL_TABLE
    ring = jnp.concatenate([jnp.broadcast_to(rel[:, :1], (ATT_HEADS, n_clip)), rel,
                            jnp.broadcast_to(rel[:, -1:], (ATT_HEADS, n_hi)),
                            jnp.broadcast_to(rel[:, :1], (ATT_HEADS, ATT_CQ))], axis=1)
    return jnp.broadcast_to(ring.reshape(ATT_PAIRS, 2, 1, ATT_RING), (ATT_PAIRS, 2, 8, ATT_RING))


def _band_masks():
    i = np.arange(ATT_CQ)[:, None]
    j = np.arange(ATT_KW)[None, :]
    lo = CHUNK * (i // CHUNK)
    band = np.where((j >= lo) & (j < lo + BAND_PAST + CHUNK), 0.0, NEG)
    return np.stack([band + np.where(j + v * ATT_CQ >= BAND_PAST, 0.0, NEG)
                     for v in range(ATT_VARIANTS)]).astype(np.float32)


def _attn_bias(bias_scr, ring_ref, band_ref):
    for hh in range(2):
        rows = jnp.broadcast_to(ring_ref[hh, 0:1, :], (ATT_CQ, ATT_RING))
        toeplitz = pltpu.roll(rows, 0, 1, stride=1, stride_axis=0)[:, :ATT_KW]
        for v in range(ATT_VARIANTS):
            bias_scr[v, hh] = toeplitz * LOG2E + band_ref[v]


ATT_STRIP = 32
ATT_SCALE = ATT_DH ** -0.5
LOG2E = 1.4426950408889634


def _strips(fn):
    def strip(r, carry):
        fn(pl.ds(pl.multiple_of(r * ATT_STRIP, ATT_STRIP), ATT_STRIP))
        return carry

    lax.fori_loop(0, ATT_CQ // ATT_STRIP, strip, 0, unroll=True)


def _attn_prepare(kpad, vpad, qm, qkv_ref):
    kpad[0:BAND_PAST, :] = jnp.zeros((BAND_PAST, LANES), BF16)
    vpad[0:BAND_PAST, :] = jnp.zeros((BAND_PAST, LANES), BF16)
    kpad[BAND_PAST:, :] = qkv_ref[1]
    vpad[BAND_PAST:, :] = qkv_ref[2]
    lane = lax.broadcasted_iota(jnp.int32, (1, LANES), 1)
    q = (qkv_ref[0].astype(F32) * (ATT_SCALE * LOG2E)).astype(BF16)
    for hh in range(2):
        qm[hh] = jnp.where((lane >= ATT_DH * hh) & (lane < ATT_DH * (hh + 1)), q, jnp.zeros_like(q))


def _attn_scores(s_ref, qm, kpad, b_ref, t, n_qb):
    t = jnp.minimum(t, n_qb - 1)
    qs = pl.multiple_of(t * ATT_CQ, ATT_CQ)
    variant = jnp.minimum(t, ATT_VARIANTS - 1)
    kw = kpad[pl.ds(qs, ATT_KW), :]
    for hh in range(2):
        s_ref[hh] = _dot_nt(qm[hh, pl.ds(qs, ATT_CQ), :], kw) + b_ref[variant, hh]


def _attn_softmax(s_ref, e_ref, linv_ref, m_ref=None):
    def strip(rows):
        s = s_ref[rows, :]
        m = jnp.max(s, axis=-1, keepdims=True)
        e = jnp.exp2(s - m)
        e_ref[rows, :] = e.astype(BF16)
        linv_ref[rows, :] = jnp.broadcast_to(1.0 / jnp.sum(e, axis=-1, keepdims=True), (ATT_STRIP, LANES))
        if m_ref is not None:
            m_ref[rows, :] = jnp.broadcast_to(m, (ATT_STRIP, LANES))

    _strips(strip)


_RING_SPEC = pl.BlockSpec((None, 2, 8, ATT_RING), lambda hp, b: (hp, 0, 0, 0))
_BAND_SPEC = pl.BlockSpec((ATT_VARIANTS, ATT_CQ, ATT_KW), lambda hp, b: (0, 0, 0))


def _attn_fwd(qkv3, ring, B, S, name, ride=None):
    T = B * S
    n_qb = S // ATT_CQ

    def body(qkv_ref, ring_ref, band_ref, o_ref, kpad, vpad, qm, b_ref, s_scr, e_scr, linv_scr):
        @pl.when(pl.program_id(1) == 0)
        def _():
            _attn_bias(b_ref, ring_ref, band_ref)

        _attn_prepare(kpad, vpad, qm, qkv_ref)
        e_scr[...] = jnp.zeros_like(e_scr)
        linv_scr[...] = jnp.zeros_like(linv_scr)
        lane = lax.broadcasted_iota(jnp.int32, (1, LANES), 1)

        def softmax(slot):
            for hh in range(2):
                _attn_softmax(s_scr.at[slot, hh], e_scr.at[slot, hh], linv_scr.at[slot, hh])

        def output(t, slot):
            qs = pl.multiple_of(jnp.maximum(t, 0) * ATT_CQ, ATT_CQ)
            vw = vpad[pl.ds(qs, ATT_KW), :]
            outs = [_dot(e_scr[slot, hh], vw) * linv_scr[slot, hh] for hh in range(2)]
            o_ref[pl.ds(qs, ATT_CQ), :] = jnp.where(lane < ATT_DH, outs[0], outs[1]).astype(BF16)

        def pair(u, carry):
            t = 2 * u
            _attn_scores(s_scr.at[1], qm, kpad, b_ref, t + 1, n_qb)
            softmax(0)
            output(t - 1, 1)
            _attn_scores(s_scr.at[0], qm, kpad, b_ref, t + 2, n_qb)
            softmax(1)
            output(t, 0)
            return carry

        _attn_scores(s_scr.at[0], qm, kpad, b_ref, 0, n_qb)
        lax.fori_loop(0, n_qb // 2, pair, 0)
        output(n_qb - 1, 1)

    return _call(body, name=name, grid=(ATT_PAIRS, B),
                 in_specs=[pl.BlockSpec((3, S, LANES), lambda hp, b: (0, b, hp)), _RING_SPEC, _BAND_SPEC],
                 out_specs=[pl.BlockSpec((S, LANES), lambda hp, b: (b, hp))],
                 out_shape=[jax.ShapeDtypeStruct((T, D_MODEL), BF16)],
                 scratch_shapes=[pltpu.VMEM((S + BAND_PAST, LANES), BF16), pltpu.VMEM((S + BAND_PAST, LANES), BF16),
                                 pltpu.VMEM((2, S, LANES), BF16), pltpu.VMEM((ATT_VARIANTS, 2, ATT_CQ, ATT_KW), F32),
                                 pltpu.VMEM((2, 2, ATT_CQ, ATT_KW), F32),
                                 pltpu.VMEM((2, 2, ATT_CQ, ATT_KW), BF16), pltpu.VMEM((2, 2, ATT_CQ, LANES), F32)],
                 args=(qkv3, ring, jnp.asarray(_band_masks())), ride=ride)


def _attn_bwd(qkv3, ring, do, B, S, name, ride=None):
    T = B * S
    n_qb = S // ATT_CQ

    def body(qkv_ref, ring_ref, band_ref, do_ref, dqkv_ref, dring_ref, kpad, vpad, qm, dkacc, dvacc, b_ref, db_ref):
        @pl.when(pl.program_id(1) == 0)
        def _():
            _attn_bias(b_ref, ring_ref, band_ref)
            db_ref[...] = jnp.zeros_like(db_ref)

        _attn_prepare(kpad, vpad, qm, qkv_ref)
        dkacc[...] = jnp.zeros_like(dkacc)
        dvacc[...] = jnp.zeros_like(dvacc)
        lane = lax.broadcasted_iota(jnp.int32, (1, LANES), 1)

        def step(qb, carry):
            qs = pl.multiple_of(qb * ATT_CQ, ATT_CQ)
            variant = jnp.minimum(qb, ATT_VARIANTS - 1)
            dov = do_ref[pl.ds(qs, ATT_CQ), :].astype(F32)
            kw = kpad[pl.ds(qs, ATT_KW), :]
            vw = vpad[pl.ds(qs, ATT_KW), :]
            heads = (0, 1)
            qmh = [qm[hh, pl.ds(qs, ATT_CQ), :] for hh in heads]
            s = [_dot_nt(qmh[hh], kw) + b_ref[variant, hh] for hh in heads]
            e, linv, dom, dp = [None, None], [None, None], [None, None], [None, None]
            for hh in heads:
                e[hh] = jnp.exp2(s[hh] - jnp.max(s[hh], axis=-1, keepdims=True))
                linv[hh] = 1.0 / jnp.sum(e[hh], axis=-1, keepdims=True)
                sel = (lane >= ATT_DH * hh) & (lane < ATT_DH * (hh + 1))
                dom[hh] = jnp.where(sel, dov * linv[hh], 0.0).astype(BF16)
                dp[hh] = _dot_nt(dom[hh], vw)
            dqs, dk, dv = [], None, None
            for hh in heads:
                ds = e[hh] * (dp[hh] - jnp.sum(dp[hh] * e[hh], axis=-1, keepdims=True) * linv[hh])
                db_ref[hh] += ds
                dsb = ds.astype(BF16)
                dqs.append(_dot(dsb, kw) * ATT_SCALE)
                dk_h = _dot_tn(qmh[hh], dsb) * (1.0 / LOG2E)
                dv_h = _dot_tn(dom[hh], e[hh].astype(BF16))
                dk = dk_h if dk is None else dk + dk_h
                dv = dv_h if dv is None else dv + dv_h
            dqkv_ref[0, pl.ds(qs, ATT_CQ), :] = jnp.where(lane < ATT_DH, dqs[0], dqs[1]).astype(BF16)
            dkacc[:, pl.ds(qs, ATT_KW)] += dk
            dvacc[:, pl.ds(qs, ATT_KW)] += dv
            return carry

        lax.fori_loop(0, n_qb, step, 0, unroll=4)
        dqkv_ref[1] = dkacc[:, BAND_PAST:].T.astype(BF16)
        dqkv_ref[2] = dvacc[:, BAND_PAST:].T.astype(BF16)

        @pl.when(pl.program_id(1) == B - 1)
        def _():
            r = lax.broadcasted_iota(jnp.int32, (ATT_CQ, ATT_CQ), 0)
            c = lax.broadcasted_iota(jnp.int32, (ATT_CQ, ATT_CQ), 1)
            reverse = jnp.where(r + c == ATT_CQ - 1, 1.0, 0.0).astype(BF16)
            for hh in range(2):
                x = jnp.concatenate([db_ref[hh], jnp.zeros((ATT_CQ, ATT_RING - ATT_KW), F32)], axis=1)
                flipped = jnp.zeros((ATT_CQ, ATT_RING), F32)
                for _ in range(3):
                    part = x.astype(BF16)
                    flipped = flipped + _dot(reverse, part)
                    x = x - part.astype(F32)
                aligned = pltpu.roll(flipped, ATT_KW + 1, 1, stride=1, stride_axis=0)
                dring_ref[hh] = jnp.sum(aligned.reshape(ATT_CQ // 8, 8, ATT_RING), axis=0)

    qkv_spec = pl.BlockSpec((3, S, LANES), lambda hp, b: (0, b, hp))
    return _call(body, name=name, grid=(ATT_PAIRS, B),
                 in_specs=[qkv_spec, _RING_SPEC, _BAND_SPEC, pl.BlockSpec((S, LANES), lambda hp, b: (b, hp))],
                 out_specs=[qkv_spec, _RING_SPEC],
                 out_shape=[jax.ShapeDtypeStruct((3, T, D_MODEL), BF16),
                            jax.ShapeDtypeStruct((ATT_PAIRS, 2, 8, ATT_RING), F32)],
                 scratch_shapes=[pltpu.VMEM((S + BAND_PAST, LANES), BF16), pltpu.VMEM((S + BAND_PAST, LANES), BF16),
                                 pltpu.VMEM((2, S, LANES), BF16),
                                 pltpu.VMEM((LANES, S + BAND_PAST), F32), pltpu.VMEM((LANES, S + BAND_PAST), F32),
                                 pltpu.VMEM((ATT_VARIANTS, 2, ATT_CQ, ATT_KW), F32), pltpu.VMEM((2, ATT_CQ, ATT_KW), F32)],
                 args=(qkv3, ring, jnp.asarray(_band_masks()), do), ride=ride)


def _rel_bias_grad(dring, name):
    fold = np.zeros((ATT_RING, REL_PAD), np.float32)
    fold[np.arange(ATT_RING), _ring_index()] = 1.0
    fold = jnp.asarray(fold, BF16)

    def body(d_ref, f_ref, o_ref):
        x = jnp.sum(d_ref[...], axis=0, keepdims=True)
        x = jnp.broadcast_to(x, (8, ATT_RING))
        acc = jnp.zeros((8, REL_PAD), F32)
        for _ in range(3):
            part = x.astype(BF16)
            acc = acc + _dot(part, f_ref[...])
            x = x - part.astype(F32)
        o_ref[...] = acc[0:1, :]

    out = _call(body, name=name, grid=(ATT_HEADS,),
                in_specs=[pl.BlockSpec((None, 8, ATT_RING), lambda h: (h, 0, 0)),
                          pl.BlockSpec((ATT_RING, REL_PAD), lambda h: (0, 0))],
                out_specs=[pl.BlockSpec((None, 1, REL_PAD), lambda h: (h, 0, 0))],
                out_shape=[jax.ShapeDtypeStruct((ATT_HEADS, 1, REL_PAD), F32)],
                args=(dring.reshape(ATT_HEADS, 8, ATT_RING), fold))[0]
    return out.reshape(ATT_HEADS, REL_PAD)


def _adamw(w, g, m, v):
    m = ADAM_B1 * m + (1.0 - ADAM_B1) * g
    v = ADAM_B2 * v + (1.0 - ADAM_B2) * (g * g)
    m_hat = m / (1.0 - ADAM_B1 ** ADAM_STEP)
    v_hat = v / (1.0 - ADAM_B2 ** ADAM_STEP)
    delta = -ADAM_LR * (m_hat / (jnp.sqrt(v_hat) + ADAM_EPS) + ADAM_WD * w)
    return delta, m, v


def _sum_devices(ref):
    g = ref[0].astype(F32)
    for d in range(1, N_DEV):
        g = g + ref[d].astype(F32)
    return g


def _adamw_reduce(lands, w, m, v, name, tr=256):
    L, R, C = w.shape
    tr = min(tr, R)
    n_i = R // tr

    def body(*refs):
        l_refs = refs[:L]
        w_ref, m_ref, v_ref, g_out, d_out, m_out, v_out = refs[L:]
        layer = pl.program_id(0)
        for l in range(L):
            @pl.when(layer == l)
            def _(l=l):
                g = _sum_devices(l_refs[l])
                g_out[...] = g
                d_out[...], m_out[...], v_out[...] = _adamw(w_ref[...], g, m_ref[...], v_ref[...])

    def land_spec(l):
        return pl.BlockSpec((N_DEV, tr, C), lambda ly, i: (0, jnp.where(ly == l, i, jnp.where(ly < l, 0, n_i - 1)), 0))

    blk = pl.BlockSpec((None, tr, C), lambda ly, i: (ly, i, 0))
    return _call(body, name=name, grid=(L, n_i), in_specs=[land_spec(l) for l in range(L)] + [blk, blk, blk],
                 out_specs=[blk] * 4, out_shape=[jax.ShapeDtypeStruct((L, R, C), F32)] * 4, args=(*lands, w, m, v))


def _small_update(land_small, land_rel, w, m, v, name):
    def body(ls_ref, lr_ref, w_ref, m_ref, v_ref, g_out, d_out, m_out, v_out, rel_out):
        g = _sum_devices(ls_ref)
        g_out[...] = g
        d_out[...], m_out[...], v_out[...] = _adamw(w_ref[...], g, m_ref[...], v_ref[...])
        rel_out[...] = _sum_devices(lr_ref)

    return pl.pallas_call(
        body, name=name,
        out_shape=[jax.ShapeDtypeStruct(w.shape, F32)] * 4 + [jax.ShapeDtypeStruct(land_rel.shape[1:], F32)],
    )(land_small, land_rel, w, m, v)


def _adamw_plain(w, g, m, v, name):
    def body(w_ref, g_ref, m_ref, v_ref, d_out, m_out, v_out):
        d_out[...], m_out[...], v_out[...] = _adamw(w_ref[...], g_ref[...], m_ref[...], v_ref[...])

    return pl.pallas_call(body, name=name, out_shape=[jax.ShapeDtypeStruct(w.shape, F32)] * 3)(w, g, m, v)


def _ret_piece_of_column_block(p):
    per_head = RET_HEAD_COLS // RET_DK
    qk = jnp.where(p < RET_HEADS, per_head * p, per_head * (p - RET_HEADS) + 1)
    pv = p - 2 * RET_HEADS
    vv = per_head * (pv // 2) + 2 + pv % 2
    pg = p - 4 * RET_HEADS
    gg = per_head * (pg // 2) + 4 + pg % 2
    return jnp.where(p < 2 * RET_HEADS, qk, jnp.where(p < 4 * RET_HEADS, vv, gg))


def _step(x, target, shards, mix_g, gn_g, rel_shard, mlp_g, final_g):
    s_rin, s_rout, s_ain, s_aout, s_w1, s_w2 = shards
    B, S, _ = x.shape
    T = B * S
    tk = min(T, 2048)
    n_k = T // tk
    h0 = x.reshape(T, D_MODEL)
    tgt = target.reshape(T, D_MODEL)
    consts = _ret_constants(S)
    w1_cols = w2_rows = D_FF // N_DEV

    rel_cols = REL_TABLE // N_DEV
    per_shard = RET_IN // N_DEV // RET_DK

    def head_layout(r):
        return lambda ref, blk: ref.at[:, pl.ds(pl.multiple_of(
            _ret_piece_of_column_block(per_shard * blk + r) * RET_DK, LANES), RET_DK)]

    w_rin_heads, rel_all = _ChipGather(
        [s_rin[:, r * RET_DK:(r + 1) * RET_DK] for r in range(per_shard)] + [rel_shard],
        [jax.ShapeDtypeStruct((D_MODEL, RET_IN), BF16), jax.ShapeDtypeStruct((N_DEV, ATT_HEADS, rel_cols), F32)],
        [0] * per_shard + [1], [head_layout(r) for r in range(per_shard)] + [_slot]).run("gather_first")
    ring = _bias_ring(rel_all.transpose(1, 0, 2).reshape(ATT_HEADS, REL_TABLE))

    def gather(*items):
        return _ChipGather([s for s, _, _ in items], [jax.ShapeDtypeStruct(full, BF16) for _, _, full in items],
                           list(range(len(items))), [view for _, view, _ in items])

    def scatter(*items):
        return _Exchange([p for p, _, _ in items], [(a, view, a, _slot) for a, (_, view, _) in enumerate(items)],
                         [jax.ShapeDtypeStruct((N_DEV, *shard), BF16) for _, _, shard in items])

    w1_full, w2_full = (D_MODEL, D_FF), (D_FF, D_MODEL)
    w1_shard, w2_shard = (D_MODEL, w1_cols), (w2_rows, D_MODEL)

    ride = gather((s_rout, _rows(RET_VW // N_DEV), (RET_VW, D_MODEL)), (s_w1[0], _cols(w1_cols), w1_full))
    (proj, n0), (w_rout, w1_0) = _ret_proj(h0, mix_g[0:1], w_rin_heads, consts[0], consts[1], S, "ret_proj", 512, ride=ride)
    (y, o, states), (w2_0,) = _ret_fwd(proj, consts, gn_g, B, S, "ret_fwd", ride=gather((s_w2[0], _rows(w2_rows), w2_full)))
    h1 = _matmul_res(y, w_rout, h0, "ret_out")
    ride = gather((s_ain, _cols(3 * D_MODEL // N_DEV), (D_MODEL, 3 * D_MODEL)),
                  (s_aout, _rows(D_MODEL // N_DEV), (D_MODEL, D_MODEL)))
    (h2, n1, u1), (w_ain, w_aout) = _mlp_fwd(h1, mlp_g[0:1], w1_0, w2_0, "mlp0_fwd", ride=ride)
    qkv3, n2 = _norm_proj(h2, mix_g[1:2], w_ain, 3, "att_proj", 1024)
    ride = gather((s_w1[1], _cols(w1_cols), w1_full), (s_w2[1], _rows(w2_rows), w2_full))
    (o2,), (w1_1, w2_1) = _attn_fwd(qkv3, ring, B, S, "att_fwd", ride=ride)
    h3 = _matmul_res(o2, w_aout, h2, "att_out")
    h4, n3, u3 = _mlp_fwd(h3, mlp_g[1:2], w1_1, w2_1, "mlp1_fwd")
    dh4, dh4b, loss, dg_final = _final_loss(h4, final_g, tgt, "final_loss")

    def tok(width):
        return dict(a=pl.BlockSpec((tk, width), lambda i, j, k: (k, i)), b=pl.BlockSpec((tk, width), lambda i, j, k: (k, j)))

    def mlp_dw2(u, dhb, tag, ride=None):
        return _wgrad(u, tok(1024)["a"], dhb, tok(1024)["b"], D_FF, D_MODEL, 1024, 1024, n_k, tag + "_dw2",
                      square_a=True, ride=ride)

    def mlp_dw1(n, du, tag):
        return _wgrad(n, tok(1024)["a"], du, tok(1024)["b"], D_MODEL, D_FF, 1024, 1024, n_k, tag + "_dw1")

    dh3, dh3b, du3, dg_mlp1 = _mlp_bwd(dh4, u3, w1_1, w2_1, h3, mlp_g[1:2], "mlp1_bwd")
    gw2_1, gw1_1 = mlp_dw2(u3, dh4b, "mlp1"), mlp_dw1(n3, du3, "mlp1")
    do2 = _matmul_nt(dh3b, w_aout, "att_out_bwd")
    g_aout = _wgrad(o2, tok(1024)["a"], dh3b, tok(1024)["b"], D_MODEL, D_MODEL, 1024, 1024, n_k, "att_out_dw")
    ride = scatter((gw1_1, _cols(w1_cols), w1_shard), (gw2_1, _rows(w2_rows), w2_shard),
                   (g_aout, _rows(D_MODEL // N_DEV), (D_MODEL // N_DEV, D_MODEL)))
    (dqkv3, dring), (l_w1_1, l_w2_1, l_aout) = _attn_bwd(qkv3, ring, do2, B, S, "att_bwd", ride=ride)
    g_rel = _rel_bias_grad(dring, "rel_bias_grad")
    dh2, dh2b, dg_mix1 = _proj_bwd(dqkv3, w_ain, h2, mix_g[1:2], dh3, "att_proj_bwd", 1024)
    g_ain = _wgrad(n2, tok(1024)["a"], dqkv3, pl.BlockSpec((None, tk, D_MODEL), lambda i, j, k: (j, k, 0)),
                   D_MODEL, 3 * D_MODEL, 1024, 1024, n_k, "att_proj_dw")
    ride = scatter((g_ain, _cols(3 * D_MODEL // N_DEV), (D_MODEL, 3 * D_MODEL // N_DEV)))
    (dh1, dh1b, du1, dg_mlp0), (l_ain,) = _mlp_bwd(dh2, u1, w1_0, w2_0, h1, mlp_g[0:1], "mlp0_bwd", ride=ride)
    dy = _matmul_nt(dh1b, w_rout, "ret_out_bwd")
    g_rout = _wgrad(y, tok(1024)["a"], dh1b, tok(1024)["b"], RET_VW, D_MODEL, 1024, 1024, n_k, "ret_out_dw")
    gw2_0, (l_rout,) = mlp_dw2(u1, dh2b, "mlp0", ride=scatter((g_rout, _rows(RET_VW // N_DEV), (RET_VW // N_DEV, D_MODEL))))
    gw1_0 = mlp_dw1(n1, du1, "mlp0")
    ride = scatter((gw2_0, _rows(w2_rows), w2_shard), (gw1_0, _cols(w1_cols), w1_shard))
    (dproj, dgn), (l_w2_0, l_w1_0) = _ret_bwd(proj, consts, gn_g, o, states, dy, B, S, "ret_bwd", ride=ride)
    g_rin = _wgrad(n0, tok(1024)["a"], dproj,
                   [pl.BlockSpec((tk, RET_DK), functools.partial(
                       lambda i, j, k, r: (k, _ret_piece_of_column_block(per_shard * j + r)), r=r)) for r in range(per_shard)],
                   D_MODEL, RET_IN, 1024, per_shard * RET_DK, n_k, "ret_proj_dw")
    ride = scatter((g_rin, _cols(RET_IN // N_DEV), (D_MODEL, RET_IN // N_DEV)))
    (dx, _, dg_mix0), (l_rin,) = _proj_bwd(dproj, w_rin_heads, h0, mix_g[0:1], dh1, "ret_proj_bwd", 512, ride=ride)

    small = jnp.concatenate([dg_mix0, dg_mix1, dg_mlp0, dg_mlp1, dg_final, dgn.reshape(2, D_MODEL),
                             jnp.broadcast_to(loss[0:1, 0:1], (1, D_MODEL))], axis=0)
    l_small, l_rel = _Exchange(
        [small, g_rel], [(0, _whole, 0, _slot), (1, _whole, 1, _slot)],
        [jax.ShapeDtypeStruct((N_DEV, 8, D_MODEL), F32), jax.ShapeDtypeStruct((N_DEV, ATT_HEADS, REL_PAD), F32)],
    ).run("scatter_small")
    lands = dict(rin=l_rin, rout=l_rout, ain=l_ain, aout=l_aout, w1=(l_w1_0, l_w1_1), w2=(l_w2_0, l_w2_1),
                 small=l_small, rel=l_rel)
    return dx.reshape(B, S, D_MODEL), lands


def kernel(x, mix_norm_g, ret_w_in, ret_gn_g, ret_w_out, att_w_in, att_rel_bias, att_w_out, mlp_norm_g, mlp_w1, mlp_w2, final_norm_g, loss_target, m_mix_norm_g, m_ret_w_in, m_ret_gn_g, m_ret_w_out, m_att_w_in, m_att_rel_bias, m_att_w_out, m_mlp_norm_g, m_mlp_w1, m_mlp_w2, m_final_norm_g, v_mix_norm_g, v_ret_w_in, v_ret_gn_g, v_ret_w_out, v_att_w_in, v_att_rel_bias, v_att_w_out, v_mlp_norm_g, v_mlp_w1, v_mlp_w2, v_final_norm_g):
    me = _lin(_place())
    rel_cols = REL_TABLE // N_DEV
    shards = (ret_w_in[0].astype(BF16), ret_w_out[0].astype(BF16), att_w_in[0].astype(BF16), att_w_out[0].astype(BF16),
              (mlp_w1[0].astype(BF16), mlp_w1[1].astype(BF16)), (mlp_w2[0].astype(BF16), mlp_w2[1].astype(BF16)))
    grad_x, lands = _step(x, loss_target, shards, mix_norm_g, ret_gn_g, att_rel_bias[0], mlp_norm_g,
                          final_norm_g.reshape(1, D_MODEL))

    u_rin = _adamw_reduce([lands["rin"]], ret_w_in, m_ret_w_in, v_ret_w_in, "update_ret_w_in")
    u_rout = _adamw_reduce([lands["rout"]], ret_w_out, m_ret_w_out, v_ret_w_out, "update_ret_w_out")
    u_ain = _adamw_reduce([lands["ain"]], att_w_in, m_att_w_in, v_att_w_in, "update_att_w_in")
    u_aout = _adamw_reduce([lands["aout"]], att_w_out, m_att_w_out, v_att_w_out, "update_att_w_out")
    u_w1 = _adamw_reduce(lands["w1"], mlp_w1, m_mlp_w1, v_mlp_w1, "update_mlp_w1")
    u_w2 = _adamw_reduce(lands["w2"], mlp_w2, m_mlp_w2, v_mlp_w2, "update_mlp_w2")

    def pack(mix, mlp, fin, gn):
        return jnp.concatenate([mix, mlp, fin.reshape(1, D_MODEL), gn.reshape(2, D_MODEL), jnp.zeros((1, D_MODEL), F32)], axis=0)

    small_w = pack(mix_norm_g, mlp_norm_g, final_norm_g, ret_gn_g)
    small_m = pack(m_mix_norm_g, m_mlp_norm_g, m_final_norm_g, m_ret_gn_g)
    small_v = pack(v_mix_norm_g, v_mlp_norm_g, v_final_norm_g, v_ret_gn_g)
    sg, sd, sm, sv, rel_sum = _small_update(lands["small"], lands["rel"], small_w, small_m, small_v, "update_small")
    g_rel_mine = lax.dynamic_slice(rel_sum, (0, me * rel_cols), (ATT_HEADS, rel_cols))
    rel_d, rel_m, rel_v = _adamw_plain(att_rel_bias[0], g_rel_mine, m_att_rel_bias[0], v_att_rel_bias[0], "update_rel_bias")
    u_rel = [g_rel_mine[None], rel_d[None], rel_m[None], rel_v[None]]

    def unpack(t):
        return dict(mix=t[0:2], mlp=t[2:4], fin=t[4], gn=t[5:7].reshape(1, RET_VW))

    us = [unpack(t) for t in (sg, sd, sm, sv)]
    outs = [sg[7, 0], grad_x]
    for k in range(4):
        outs += [us[k]["mix"], u_rin[k], us[k]["gn"], u_rout[k], u_ain[k], u_rel[k], u_aout[k], us[k]["mlp"],
                 u_w1[k], u_w2[k], us[k]["fin"]]
    return tuple(outs)
```

```python
import functools

import numpy as np
import jax
import jax.numpy as jnp
from jax import lax
from jax.experimental import pallas as pl
from jax.experimental.pallas import tpu as pltpu

F32, BF16 = jnp.float32, jnp.bfloat16

D_MODEL = 1024
CHUNK = 64
RET_HEADS, RET_DK, RET_DV = 4, 256, 512
RET_QK, RET_VW = RET_HEADS * RET_DK, RET_HEADS * RET_DV
RET_IN = 2 * RET_QK + 2 * RET_VW
RET_HEAD_COLS = 2 * RET_DK + 2 * RET_DV
RET_SC = 256
ROPE_BASE = 10000.0
ATT_HEADS, ATT_DH = 16, 64
LANES = 128
ATT_PAIRS = ATT_HEADS * ATT_DH // LANES
BAND_PAST = 8 * CHUNK
MAX_REL = 256
REL_TABLE = MAX_REL + CHUNK
REL_PAD = 384
ATT_CQ = 256
ATT_KW = ATT_CQ + BAND_PAST
ATT_RING = ATT_CQ + ATT_KW
ATT_VARIANTS = BAND_PAST // ATT_CQ + 1
D_FF = 4 * D_MODEL
EPS = 1e-6
EPILOGUE_ROWS = 256
NEG = -1e30
N_DEV = 8
N_PEERS = N_DEV - 1

ADAM_LR, ADAM_B1, ADAM_B2, ADAM_EPS, ADAM_WD, ADAM_STEP = 0.001, 0.9, 0.999, 1e-08, 0.01, 10

VMEM_LIMIT = 56 * 1024 * 1024
MESH = pl.DeviceIdType.MESH
ANY = pl.BlockSpec(memory_space=pl.ANY)


def _dot(a, b):
    return jnp.dot(a, b, preferred_element_type=F32)


def _dot_nt(a, b):
    return lax.dot_general(a, b, (((1,), (1,)), ((), ())), preferred_element_type=F32)


def _dot_tn(a, b):
    return lax.dot_general(a, b, (((0,), (0,)), ((), ())), preferred_element_type=F32)


def _rms(x, g):
    r = lax.rsqrt(jnp.mean(x * x, axis=-1, keepdims=True) + EPS)
    return x * r * g


def _rms_bwd(dn, x, g):
    r = lax.rsqrt(jnp.mean(x * x, axis=-1, keepdims=True) + EPS)
    xh = x * r
    dg = jnp.sum(dn * xh, axis=0, keepdims=True)
    dxh = dn * g
    dx = r * (dxh - xh * jnp.mean(dxh * xh, axis=-1, keepdims=True))
    return dx, dg


def _rms_bwd_epilogue(x_ref, g_ref, dres_ref, dx_ref, dxb_ref, dg_ref):
    for r in range(0, dx_ref.shape[0], EPILOGUE_ROWS):
        rows = slice(r, r + EPILOGUE_ROWS)
        dx, dg = _rms_bwd(dx_ref[rows, :], x_ref[rows, :], g_ref[...])
        dx = dres_ref[rows, :] + dx
        dx_ref[rows, :] = dx
        dxb_ref[rows, :] = dx.astype(BF16)
        dg_ref[...] += dg


def _place():
    return lax.axis_index("x"), lax.axis_index("y"), lax.axis_index("c")


def _lin(p):
    return 4 * p[0] + 2 * p[1] + p[2]


def _cols(width):
    return lambda ref, blk: ref.at[:, pl.ds(pl.multiple_of(blk * width, LANES), width)]


def _rows(height):
    return lambda ref, blk: ref.at[pl.ds(pl.multiple_of(blk * height, 8), height), :]


def _slot(ref, blk):
    return ref.at[blk]


def _whole(ref, blk):
    return ref


class _Exchange:
    def __init__(self, sources, flows, land_shapes):
        self.sources, self.flows, self.land_shapes = list(sources), flows, list(land_shapes)
        n_f = len(flows)
        self.sem_shapes = [pltpu.SemaphoreType.DMA((N_PEERS * n_f,)), pltpu.SemaphoreType.DMA((N_PEERS * n_f,)),
                           pltpu.SemaphoreType.DMA((n_f,))]

    def _copies(self, srcs, lands, sems):
        send_sems, recv_sems, local_sems = sems
        x, y, c = _place()
        me = (x, y, c)
        peers = [(x ^ dx, y ^ dy, c ^ dc) for dx in (0, 1) for dy in (0, 1) for dc in (0, 1)][1:]

        def copy(f, k, sender, to):
            si, sview, li, lview = self.flows[f]
            return pltpu.make_async_remote_copy(
                src_ref=sview(srcs[si], _lin(to)), dst_ref=lview(lands[li], _lin(sender)),
                send_sem=send_sems.at[f * N_PEERS + k], recv_sem=recv_sems.at[f * N_PEERS + k],
                device_id=to, device_id_type=MESH)

        mine, sends, recvs = [], [], []
        for f, (si, sview, li, lview) in enumerate(self.flows):
            mine.append(pltpu.make_async_copy(sview(srcs[si], _lin(me)), lview(lands[li], _lin(me)), local_sems.at[f]))
            for k, peer in enumerate(peers):
                sends.append(copy(f, k, me, peer))
                recvs.append(copy(f, k, peer, me))
        return mine, sends, recvs

    def start(self, srcs, lands, sems):
        mine, sends, _ = self._copies(srcs, lands, sems)
        for cp in mine + sends:
            cp.start()

    def finish(self, srcs, lands, sems):
        mine, sends, recvs = self._copies(srcs, lands, sems)
        for cp in recvs:
            cp.wait_recv()
        for cp in sends:
            cp.wait_send()
        for cp in mine:
            cp.wait()

    def run(self, name):
        n_src, n_land = len(self.sources), len(self.land_shapes)

        def body(*refs):
            srcs, lands, sems = refs[:n_src], refs[n_src:n_src + n_land], refs[n_src + n_land:]
            self.start(srcs, lands, sems)
            self.finish(srcs, lands, sems)

        return pl.pallas_call(body, name=name, in_specs=[ANY] * n_src, out_specs=[ANY] * n_land,
                              out_shape=self.land_shapes, scratch_shapes=self.sem_shapes)(*self.sources)


def _call(body, *, name, grid, in_specs, out_specs, out_shape, args, scratch_shapes=(), ride=None):
    params = pltpu.CompilerParams(dimension_semantics=("arbitrary",) * len(grid), vmem_limit_bytes=VMEM_LIMIT)
    in_specs, out_specs, out_shape, scratch_shapes = list(in_specs), list(out_specs), list(out_shape), list(scratch_shapes)
    if ride is None:
        return pl.pallas_call(body, name=name, grid=grid, in_specs=in_specs, out_specs=out_specs, out_shape=out_shape,
                              scratch_shapes=scratch_shapes, compiler_params=params)(*args)
    n_in, n_out, n_scr = len(in_specs), len(out_specs), len(scratch_shapes)
    n_src, n_land = len(ride.sources), len(ride.land_shapes)

    def riding(*refs):
        bounds = np.cumsum([n_in, n_src, n_out, n_land, n_scr])
        ins, srcs, outs, lands, scr, sems = (refs[a:b] for a, b in zip([0, *bounds], [*bounds, len(refs)]))
        step = functools.reduce(lambda a, d: a * grid[d] + pl.program_id(d), range(len(grid)), 0)
        n_steps = int(np.prod(grid))

        @pl.when(step == 0)
        def _():
            ride.start(srcs, lands, sems)

        if hasattr(ride, "middle"):
            @pl.when(step == (5 * n_steps) // 8)
            def _():
                ride.middle(srcs, lands, sems)

        body(*ins, *outs, *scr)

        @pl.when(step == n_steps - 1)
        def _():
            ride.finish(srcs, lands, sems)

    res = pl.pallas_call(
        riding, name=name, grid=grid, in_specs=in_specs + [ANY] * n_src, out_specs=out_specs + [ANY] * n_land,
        out_shape=out_shape + ride.land_shapes, scratch_shapes=scratch_shapes + ride.sem_shapes,
        compiler_params=params)(*args, *ride.sources)
    return res[:n_out], res[n_out:]


class _ChipGather:
    def __init__(self, shards, out_shapes, out_of, views):
        self.sources, self.land_shapes, self.out_of, self.views = list(shards), list(out_shapes), out_of, views
        n = len(shards)
        self.sem_shapes = [pltpu.SemaphoreType.DMA((N_PEERS * n,)), pltpu.SemaphoreType.DMA((N_PEERS * n,)),
                           pltpu.SemaphoreType.DMA((n,))]

    def _copies(self, srcs, lands, sems):
        send_sems, recv_sems, local_sems = sems
        n = len(srcs)
        outs = [lands[self.out_of[a]] for a in range(n)]
        x, y, c = _place()
        me, sibling = (x, y, c), (x, y, 1 - c)
        chips = [(1 - x, y), (x, 1 - y), (1 - x, 1 - y)]

        def copy(a, k, block, to, own=False):
            dst = self.views[a](outs[a], _lin(block))
            return pltpu.make_async_remote_copy(
                src_ref=srcs[a] if own else dst, dst_ref=dst,
                send_sem=send_sems.at[a * N_PEERS + k], recv_sem=recv_sems.at[a * N_PEERS + k],
                device_id=to, device_id_type=MESH)

        mine = lambda: [pltpu.make_async_copy(srcs[a], self.views[a](outs[a], _lin(me)), local_sems.at[a]) for a in range(n)]
        first = lambda: [cp for a in range(n) for cp in
                         [copy(a, 0, me, sibling, own=True)] + [copy(a, 1 + j, me, (*chip, c), own=True) for j, chip in enumerate(chips)]]
        landed = lambda a, j: copy(a, 1 + j, (*chips[j], c), me)
        passed = lambda a, j: copy(a, 4 + j, (*chips[j], c), sibling)
        from_sibling = lambda a: [copy(a, 0, sibling, me)] + [copy(a, 4 + j, (*chips[j], 1 - c), me) for j in range(3)]
        return n, mine, first, landed, passed, from_sibling

    def start(self, srcs, lands, sems):
        _, mine, first, _, _, _ = self._copies(srcs, lands, sems)
        for cp in mine() + first():
            cp.start()

    def middle(self, srcs, lands, sems):
        n, _, _, landed, passed, _ = self._copies(srcs, lands, sems)
        for j in range(3):
            for a in range(n):
                landed(a, j).wait_recv()
                passed(a, j).start()

    def finish(self, srcs, lands, sems):
        n, mine, first, _, passed, from_sibling = self._copies(srcs, lands, sems)
        for a in range(n):
            for cp in from_sibling(a):
                cp.wait_recv()
        for cp in first() + [passed(a, j) for j in range(3) for a in range(n)]:
            cp.wait_send()
        for cp in mine():
            cp.wait()

    def run(self, name):
        n, n_out = len(self.sources), len(self.land_shapes)

        def body(*refs):
            parts = refs[:n], refs[n:n + n_out], refs[n + n_out:]
            self.start(*parts)
            self.middle(*parts)
            self.finish(*parts)

        return pl.pallas_call(body, name=name, in_specs=[ANY] * n, out_specs=[ANY] * n_out,
                              out_shape=self.land_shapes, scratch_shapes=self.sem_shapes)(*self.sources)


def _resident(shape):
    return pl.BlockSpec(shape, lambda i: (0,) * len(shape), pipeline_mode=pl.Buffered(1))


def _norm_proj(h, g, w, groups, name, tm, ride=None):
    T = h.shape[0]
    N = w.shape[1]
    width = N // groups

    def body(h_ref, g_ref, w_ref, o_ref, n_ref):
        n = _rms(h_ref[...], g_ref[...]).astype(BF16)
        n_ref[...] = n
        if groups == 1:
            o_ref[...] = _dot(n, w_ref[...]).astype(BF16)
        else:
            for p in range(groups):
                o_ref[p] = _dot(n, w_ref[:, p * width:(p + 1) * width]).astype(BF16)

    row = pl.BlockSpec((tm, D_MODEL), lambda i: (i, 0))
    if groups == 1:
        out_spec, out_shape = pl.BlockSpec((tm, N), lambda i: (i, 0)), jax.ShapeDtypeStruct((T, N), BF16)
    else:
        out_spec = pl.BlockSpec((groups, tm, width), lambda i: (0, i, 0))
        out_shape = jax.ShapeDtypeStruct((groups, T, width), BF16)
    return _call(body, name=name, grid=(T // tm,),
                 in_specs=[row, pl.BlockSpec((1, D_MODEL), lambda i: (0, 0)), _resident(w.shape)],
                 out_specs=[out_spec, row], out_shape=[out_shape, jax.ShapeDtypeStruct((T, D_MODEL), BF16)],
                 args=(h, g, w), ride=ride)


def _ret_proj(h, g, w, cos, sin, S, name, tm, ride=None):
    T = h.shape[0]
    per_seq = S // tm

    def body(h_ref, g_ref, w_ref, cos_ref, sin_ref, o_ref, n_ref):
        n = _rms(h_ref[...], g_ref[...]).astype(BF16)
        n_ref[...] = n
        cs, sn = cos_ref[...], sin_ref[...]
        for hd in range(RET_HEADS):
            c0 = hd * RET_HEAD_COLS
            out = _dot(n, w_ref[:, c0:c0 + RET_HEAD_COLS])
            o_ref[:, c0:c0 + RET_DK] = _rope(out[:, 0:RET_DK], cs, sn).astype(BF16)
            o_ref[:, c0 + RET_DK:c0 + 2 * RET_DK] = (_rope(out[:, RET_DK:2 * RET_DK], cs, sn) * (RET_DK ** -0.5)).astype(BF16)
            o_ref[:, c0 + 2 * RET_DK:c0 + RET_HEAD_COLS] = out[:, 2 * RET_DK:].astype(BF16)

    row = pl.BlockSpec((tm, D_MODEL), lambda i: (i, 0))
    trig = pl.BlockSpec((tm, RET_DK // 2), lambda i: (i % per_seq, 0))
    return _call(body, name=name, grid=(T // tm,),
                 in_specs=[row, pl.BlockSpec((1, D_MODEL), lambda i: (0, 0)), _resident(w.shape), trig, trig],
                 out_specs=[pl.BlockSpec((tm, w.shape[1]), lambda i: (i, 0)), row],
                 out_shape=[jax.ShapeDtypeStruct((T, w.shape[1]), BF16), jax.ShapeDtypeStruct((T, D_MODEL), BF16)],
                 args=(h, g, w, cos, sin), ride=ride)


def _proj_bwd(dy, w, x, g, dres, name, tm, ride=None):
    T = x.shape[0]
    groups = dy.shape[0] if dy.ndim == 3 else 1
    width = w.shape[1] // groups

    def body(dy_ref, w_ref, x_ref, g_ref, dres_ref, dx_ref, dxb_ref, dg_ref):
        @pl.when(pl.program_id(0) == 0)
        def _():
            dg_ref[...] = jnp.zeros_like(dg_ref)

        if groups == 1:
            dn = _dot_nt(dy_ref[...], w_ref[...])
        else:
            dn = sum(_dot_nt(dy_ref[p], w_ref[:, p * width:(p + 1) * width]) for p in range(groups))
        dx_ref[...] = dn
        _rms_bwd_epilogue(x_ref, g_ref, dres_ref, dx_ref, dxb_ref, dg_ref)

    row = pl.BlockSpec((tm, D_MODEL), lambda i: (i, 0))
    vec = pl.BlockSpec((1, D_MODEL), lambda i: (0, 0))
    dy_spec = (pl.BlockSpec((tm, w.shape[1]), lambda i: (i, 0)) if groups == 1
               else pl.BlockSpec((groups, tm, width), lambda i: (0, i, 0)))
    return _call(body, name=name, grid=(T // tm,),
                 in_specs=[dy_spec, _resident(w.shape), row, vec, row], out_specs=[row, row, vec],
                 out_shape=[jax.ShapeDtypeStruct((T, D_MODEL), F32), jax.ShapeDtypeStruct((T, D_MODEL), BF16),
                            jax.ShapeDtypeStruct((1, D_MODEL), F32)],
                 args=(dy, w, x, g, dres), ride=ride)


def _matmul_res(a, w, res, name, tm=1024):
    T, K = a.shape

    def body(a_ref, w_ref, r_ref, o_ref):
        o_ref[...] = r_ref[...] + _dot(a_ref[...], w_ref[...])

    row = pl.BlockSpec((tm, D_MODEL), lambda i: (i, 0))
    return _call(body, name=name, grid=(T // tm,),
                 in_specs=[pl.BlockSpec((tm, K), lambda i: (i, 0)), pl.BlockSpec((K, D_MODEL), lambda i: (0, 0)), row],
                 out_specs=[row], out_shape=[jax.ShapeDtypeStruct((T, D_MODEL), F32)], args=(a, w, res))[0]


def _matmul_nt(dy, w, name, tm=1024):
    T, N = dy.shape
    K = w.shape[0]

    def body(dy_ref, w_ref, o_ref):
        o_ref[...] = _dot_nt(dy_ref[...], w_ref[...]).astype(BF16)

    return _call(body, name=name, grid=(T // tm,),
                 in_specs=[pl.BlockSpec((tm, N), lambda i: (i, 0)), pl.BlockSpec((K, N), lambda i: (0, 0))],
                 out_specs=[pl.BlockSpec((tm, K), lambda i: (i, 0))], out_shape=[jax.ShapeDtypeStruct((T, K), BF16)],
                 args=(dy, w))[0]


def _wgrad(a, a_spec, b, b_specs, m, n, bm, bn, n_k, name, square_a=False, ride=None):
    b_specs = b_specs if isinstance(b_specs, (list, tuple)) else [b_specs]
    n_b = len(b_specs)

    def body(a_ref, *rest):
        b_refs = rest[:n_b]
        o_ref, acc = rest[n_b:]
        k = pl.program_id(2)

        @pl.when(k == 0)
        def _():
            acc[...] = jnp.zeros_like(acc)

        av = a_ref[...]
        if square_a:
            af = av.astype(F32)
            av = (af * af).astype(BF16)
        bv = b_refs[0][...] if n_b == 1 else jnp.concatenate([r[...] for r in b_refs], axis=1)
        acc[...] += _dot_tn(av, bv)

        @pl.when(k == n_k - 1)
        def _():
            o_ref[...] = acc[...].astype(BF16)

    res = _call(body, name=name, grid=(m // bm, n // bn, n_k), in_specs=[a_spec, *b_specs],
                out_specs=[pl.BlockSpec((bm, bn), lambda i, j, k: (i, j))], out_shape=[jax.ShapeDtypeStruct((m, n), BF16)],
                scratch_shapes=[pltpu.VMEM((bm, bn), F32)], args=(a, *([b] * n_b)), ride=ride)
    return res[0] if ride is None else (res[0][0], res[1])


def _mlp_fwd(h, g, w1, w2, name, tm=1024, tf=1024, ride=None):
    T = h.shape[0]

    def body(h_ref, g_ref, w1_ref, w2_ref, ho_ref, n_ref, u_ref):
        @pl.when(pl.program_id(1) == 0)
        def _():
            n_ref[...] = _rms(h_ref[...], g_ref[...]).astype(BF16)
            ho_ref[...] = h_ref[...]

        u = jnp.maximum(_dot(n_ref[...], w1_ref[...]), 0.0)
        u_ref[...] = u.astype(BF16)
        ho_ref[...] += _dot((u * u).astype(BF16), w2_ref[...])

    row = pl.BlockSpec((tm, D_MODEL), lambda i, j: (i, 0))
    return _call(body, name=name, grid=(T // tm, D_FF // tf),
                 in_specs=[row, pl.BlockSpec((1, D_MODEL), lambda i, j: (0, 0)),
                           pl.BlockSpec((D_MODEL, tf), lambda i, j: (0, j)), pl.BlockSpec((tf, D_MODEL), lambda i, j: (j, 0))],
                 out_specs=[row, row, pl.BlockSpec((tm, tf), lambda i, j: (i, j))],
                 out_shape=[jax.ShapeDtypeStruct((T, D_MODEL), F32), jax.ShapeDtypeStruct((T, D_MODEL), BF16),
                            jax.ShapeDtypeStruct((T, D_FF), BF16)],
                 args=(h, g, w1, w2), ride=ride)


def _mlp_bwd(dh, u, w1, w2, h, g, name, tm=1024, tf=1024, ride=None):
    T = h.shape[0]
    n_j = D_FF // tf

    def body(dh_ref, u_ref, w1_ref, w2_ref, h_ref, g_ref, dx_ref, dxb_ref, du_ref, dg_ref, dhb):
        acc = dx_ref
        i, j = pl.program_id(0), pl.program_id(1)

        @pl.when(j == 0)
        def _():
            dhb[...] = dh_ref[...].astype(BF16)
            acc[...] = jnp.zeros_like(acc)

        @pl.when((i == 0) & (j == 0))
        def _():
            dg_ref[...] = jnp.zeros_like(dg_ref)

        da = _dot_nt(dhb[...], w2_ref[...])
        du = (da * (2.0 * u_ref[...].astype(F32))).astype(BF16)
        du_ref[...] = du
        acc[...] += _dot_nt(du, w1_ref[...])

        @pl.when(j == n_j - 1)
        def _():
            _rms_bwd_epilogue(h_ref, g_ref, dh_ref, dx_ref, dxb_ref, dg_ref)

    row = pl.BlockSpec((tm, D_MODEL), lambda i, j: (i, 0))
    vec = pl.BlockSpec((1, D_MODEL), lambda i, j: (0, 0))
    hid = pl.BlockSpec((tm, tf), lambda i, j: (i, j))
    return _call(body, name=name, grid=(T // tm, n_j),
                 in_specs=[row, hid, pl.BlockSpec((D_MODEL, tf), lambda i, j: (0, j)),
                           pl.BlockSpec((tf, D_MODEL), lambda i, j: (j, 0)), row, vec],
                 out_specs=[row, row, hid, vec],
                 out_shape=[jax.ShapeDtypeStruct((T, D_MODEL), F32), jax.ShapeDtypeStruct((T, D_MODEL), BF16),
                            jax.ShapeDtypeStruct((T, D_FF), BF16), jax.ShapeDtypeStruct((1, D_MODEL), F32)],
                 scratch_shapes=[pltpu.VMEM((tm, D_MODEL), BF16)], args=(dh, u, w1, w2, h, g), ride=ride)


def _final_loss(h, g, target, name, tm=512):
    T = h.shape[0]

    def body(h_ref, g_ref, t_ref, dh_ref, dhb_ref, loss_ref, dg_ref):
        @pl.when(pl.program_id(0) == 0)
        def _():
            loss_ref[...] = jnp.zeros_like(loss_ref)
            dg_ref[...] = jnp.zeros_like(dg_ref)

        x = h_ref[...]
        gg = g_ref[...]
        r = lax.rsqrt(jnp.mean(x * x, axis=-1, keepdims=True) + EPS)
        xh = x * r
        e = xh * gg - t_ref[...]
        per_tok = jnp.mean(e * e, axis=-1, keepdims=True)
        loss_ref[...] += 0.5 * jnp.sum(per_tok, axis=0, keepdims=True)
        dy = e * (1.0 / D_MODEL)
        dg_ref[...] += jnp.sum(dy * xh, axis=0, keepdims=True)
        dxh = dy * gg
        dx = r * (dxh - xh * jnp.mean(dxh * xh, axis=-1, keepdims=True))
        dh_ref[...] = dx
        dhb_ref[...] = dx.astype(BF16)

    row = pl.BlockSpec((tm, D_MODEL), lambda i: (i, 0))
    vec = pl.BlockSpec((1, D_MODEL), lambda i: (0, 0))
    return _call(body, name=name, grid=(T // tm,), in_specs=[row, vec, row],
                 out_specs=[row, row, pl.BlockSpec((8, LANES), lambda i: (0, 0)), vec],
                 out_shape=[jax.ShapeDtypeStruct((T, D_MODEL), F32), jax.ShapeDtypeStruct((T, D_MODEL), BF16),
                            jax.ShapeDtypeStruct((8, LANES), F32), jax.ShapeDtypeStruct((1, D_MODEL), F32)],
                 args=(h, g, target))


def _ret_constants(S):
    log_gamma = jnp.log1p(-jnp.exp2(-5.0 - jnp.arange(RET_HEADS, dtype=F32)))
    idx = jnp.arange(RET_SC, dtype=F32)
    i, j = idx[:, None], idx[None, :]
    same_chunk = jnp.floor(i / CHUNK) == jnp.floor(j / CHUNK)
    mask = jnp.where((j <= i) | same_chunk, jnp.exp(log_gamma[:, None, None] * jnp.abs(i - j)[None]), 0.0)
    qdec = jnp.exp(log_gamma[:, None] * (idx + 1.0)[None, :])[:, :, None]
    kdec = jnp.exp(log_gamma[:, None] * (RET_SC - 1 - idx)[None, :])[:, :, None]
    cdec = jnp.exp(log_gamma * RET_SC)[:, None, None]
    half = RET_DK // 2
    inv = jnp.exp(-jnp.log(ROPE_BASE) * jnp.arange(half, dtype=F32) / half)
    ang = jnp.arange(S, dtype=F32)[:, None] * inv[None, :]
    return jnp.cos(ang), jnp.sin(ang), mask.astype(F32), qdec, kdec, cdec


def _rope(t, cs, sn):
    t1, t2 = t[:, :RET_DK // 2], t[:, RET_DK // 2:]
    return jnp.concatenate([t1 * cs - t2 * sn, t1 * sn + t2 * cs], axis=-1)


def _rope_bwd(d, cs, sn):
    d1, d2 = d[:, :RET_DK // 2], d[:, RET_DK // 2:]
    return jnp.concatenate([d1 * cs + d2 * sn, d2 * cs - d1 * sn], axis=-1)


def _ret_specs(B, S, reverse):
    n_sc = S // RET_SC

    def cc(c):
        return n_sc - 1 - c if reverse else c

    return dict(
        proj=pl.BlockSpec((B, RET_SC, RET_HEAD_COLS), lambda h, c: (0, cc(c), h)),
        trig=pl.BlockSpec((RET_SC, RET_DK // 2), lambda h, c: (cc(c), 0)),
        mask=pl.BlockSpec((None, RET_SC, RET_SC), lambda h, c: (h, 0, 0)),
        dec=pl.BlockSpec((None, RET_SC, 1), lambda h, c: (h, 0, 0)),
        cdec=pl.BlockSpec((None, 1, 1), lambda h, c: (h, 0, 0)),
        gn=pl.BlockSpec((1, RET_DV), lambda h, c: (0, h)),
        val=pl.BlockSpec((B, RET_SC, RET_DV), lambda h, c: (0, cc(c), h)),
        state=pl.BlockSpec((B, None, None, RET_DK, RET_DV), lambda h, c: (0, h, cc(c), 0, 0)),
    )


def _ret_qkvg(p_ref, kdec):
    qb = p_ref[:, 0:RET_DK]
    kb = p_ref[:, RET_DK:2 * RET_DK]
    kt = (kb.astype(F32) * kdec).astype(BF16)
    v = p_ref[:, 2 * RET_DK:2 * RET_DK + RET_DV]
    gate = p_ref[:, 2 * RET_DK + RET_DV:RET_HEAD_COLS].astype(F32)
    return qb, kb, kt, v, gate


def _group_norm(o):
    mu = jnp.mean(o, axis=-1, keepdims=True)
    oc = o - mu
    rstd = lax.rsqrt(jnp.mean(oc * oc, axis=-1, keepdims=True) + EPS)
    return oc * rstd, rstd


def _ret_fwd(proj, consts, gn, B, S, name, ride=None):
    T = B * S
    n_sc = S // RET_SC
    sp = _ret_specs(B, S, False)

    def body(p_ref, m_ref, qd_ref, kd_ref, cd_ref, gn_ref, y_ref, o_ref, st_ref, state):
        @pl.when(pl.program_id(1) == 0)
        def _():
            state[...] = jnp.zeros_like(state)

        seqs = range(B)
        qkvg = [_ret_qkvg(p_ref.at[b], kd_ref[...]) for b in seqs]
        qb, kb, kt, v = ([qkvg[b][i] for b in seqs] for i in range(4))
        sc = [_dot_nt(qb[b], kb[b]) for b in seqs]
        sb = [state[b].astype(BF16) for b in seqs]
        cross = [_dot(qb[b], sb[b]) for b in seqs]
        for b in seqs:
            st_ref[b] = sb[b]
        p = [(sc[b] * m_ref[...]).astype(BF16) for b in seqs]
        o = [_dot(p[b], v[b]) + cross[b] * qd_ref[...] for b in seqs]
        upd = [_dot_tn(kt[b], v[b]) for b in seqs]
        for b in seqs:
            o_ref[b] = o[b].astype(BF16)
            ohat, _ = _group_norm(o[b])
            gate = qkvg[b][4]
            y_ref[b] =(gate * jax.nn.sigmoid(gate) * (ohat * gn_ref[...])).astype(BF16)
            state[b] = state[b] * cd_ref[...] + upd[b]

    res = _call(
        body, name=name, grid=(RET_HEADS, n_sc),
        in_specs=[sp["proj"], sp["mask"], sp["dec"], sp["dec"], sp["cdec"], sp["gn"]],
        out_specs=[sp["val"], sp["val"], sp["state"]],
        out_shape=[jax.ShapeDtypeStruct((B, S, RET_VW), BF16), jax.ShapeDtypeStruct((B, S, RET_VW), BF16),
                   jax.ShapeDtypeStruct((B, RET_HEADS, n_sc, RET_DK, RET_DV), BF16)],
        scratch_shapes=[pltpu.VMEM((B, RET_DK, RET_DV), F32)],
        args=(proj.reshape(B, S, -1), *consts[2:], gn), ride=ride)
    (y, o, states), lands = res if ride is not None else (res, None)
    outs = [y.reshape(T, RET_VW), o.reshape(T, RET_VW), states]
    return outs if ride is None else (outs, lands)


def _ret_bwd(proj, consts, gn, o, states, dy, B, S, name, ride=None):
    T = B * S
    n_sc = S // RET_SC
    sp = _ret_specs(B, S, True)

    def body(p_ref, cos_ref, sin_ref, m_ref, qd_ref, kd_ref, cd_ref, gn_ref, o_ref, st_ref, dy_ref,
             dp_ref, dgn_ref, dstate):
        @pl.when(pl.program_id(1) == 0)
        def _():
            dstate[...] = jnp.zeros_like(dstate)
            dgn_ref[...] = jnp.zeros_like(dgn_ref)

        seqs = range(B)
        cs, sn = cos_ref[...], sin_ref[...]
        m, gnv = m_ref[...], gn_ref[...]
        qkvg = [_ret_qkvg(p_ref.at[b], kd_ref[...]) for b in seqs]
        qb, kb, kt, v = ([qkvg[b][i] for b in seqs] for i in range(4))
        sc = [_dot_nt(qb[b], kb[b]) for b in seqs]
        dsb = [dstate[b].astype(BF16) for b in seqs]
        dv_state = [_dot(kt[b], dsb[b]) for b in seqs]
        dkt = [_dot_nt(v[b], dsb[b]) for b in seqs]
        do, dgate = [], []
        for b in seqs:
            gate = qkvg[b][4]
            ohat, rstd = _group_norm(o_ref[b].astype(F32))
            dyv = dy_ref[b].astype(F32)
            sg = jax.nn.sigmoid(gate)
            don = dyv * (gate * sg)
            dgate.append(dyv * (ohat * gnv) * (sg * (1.0 + gate * (1.0 - sg))))
            dgn_ref[...] += jnp.sum(don * ohat, axis=0, keepdims=True)
            dohat = don * gnv
            do.append(rstd * (dohat - jnp.mean(dohat, axis=-1, keepdims=True)
                              - ohat * jnp.mean(dohat * ohat, axis=-1, keepdims=True)))
        dob = [do[b].astype(BF16) for b in seqs]
        doq = [(do[b] * qd_ref[...]).astype(BF16) for b in seqs]
        dsc_f = [_dot_nt(dob[b], v[b]) for b in seqs]
        dq_state = [_dot_nt(doq[b], st_ref[b]) for b in seqs]
        dstate_upd = [_dot_tn(qb[b], doq[b]) for b in seqs]
        p = [(sc[b] * m).astype(BF16) for b in seqs]
        dsc = [(dsc_f[b] * m).astype(BF16) for b in seqs]
        dv = [_dot_tn(p[b], dob[b]) + dv_state[b] for b in seqs]
        dq = [_dot(dsc[b], kb[b]) + dq_state[b] for b in seqs]
        dk = [(_dot_tn(dsc[b], qb[b]) + dkt[b] * kd_ref[...]) * (RET_DK ** -0.5) for b in seqs]
        for b in seqs:
            dstate[b] = dstate[b] * cd_ref[...] + dstate_upd[b]
            dp_ref[b, :, 0:RET_DK] = _rope_bwd(dq[b], cs, sn).astype(BF16)
            dp_ref[b, :, RET_DK:2 * RET_DK] = _rope_bwd(dk[b], cs, sn).astype(BF16)
            dp_ref[b, :, 2 * RET_DK:2 * RET_DK + RET_DV] = dv[b].astype(BF16)
            dp_ref[b, :, 2 * RET_DK + RET_DV:RET_HEAD_COLS] = dgate[b].astype(BF16)

    res = _call(
        body, name=name, grid=(RET_HEADS, n_sc),
        in_specs=[sp["proj"], sp["trig"], sp["trig"], sp["mask"], sp["dec"], sp["dec"], sp["cdec"], sp["gn"],
                  sp["val"], sp["state"], sp["val"]],
        out_specs=[sp["proj"], sp["gn"]],
        out_shape=[jax.ShapeDtypeStruct((B, S, RET_HEADS * RET_HEAD_COLS), BF16), jax.ShapeDtypeStruct((1, RET_VW), F32)],
        scratch_shapes=[pltpu.VMEM((B, RET_DK, RET_DV), F32)],
        args=(proj.reshape(B, S, -1), *consts, gn, o.reshape(B, S, -1), states, dy.reshape(B, S, -1)), ride=ride)
    (dproj, dgn), lands = res if ride is not None else (res, None)
    outs = [dproj.reshape(T, -1), dgn]
    return outs if ride is None else (outs, lands)


def _ring_index():
    u = np.arange(ATT_RING)
    offset = np.where(u < ATT_KW, u, u - ATT_RING)
    return np.clip(offset - BAND_PAST, -MAX_REL, CHUNK - 1) + MAX_REL


def _bias_ring(rel):
    n_clip = BAND_PAST - MAX_REL
    n_hi = ATT_KW - n_clip - REL_TABLE
    ring = jnp.concatenate([jnp.broadcast_to(rel[:, :1], (ATT_HEADS, n_clip)), rel,
                            jnp.broadcast_to(rel[:, -1:], (ATT_HEADS, n_hi)),
                            jnp.broadcast_to(rel[:, :1], (ATT_HEADS, ATT_CQ))], axis=1)
    return jnp.broadcast_to(ring.reshape(ATT_PAIRS, 2, 1, ATT_RING), (ATT_PAIRS, 2, 8, ATT_RING))


def _band_masks():
    i = np.arange(ATT_CQ)[:, None]
    j = np.arange(ATT_KW)[None, :]
    lo = CHUNK * (i // CHUNK)
    band = np.where((j >= lo) & (j < lo + BAND_PAST + CHUNK), 0.0, NEG)
    return np.stack([band + np.where(j + v * ATT_CQ >= BAND_PAST, 0.0, NEG)
                     for v in range(ATT_VARIANTS)]).astype(np.float32)


def _attn_bias(bias_scr, ring_ref, band_ref):
    for hh in range(2):
        rows = jnp.broadcast_to(ring_ref[hh, 0:1, :], (ATT_CQ, ATT_RING))
        toeplitz = pltpu.roll(rows, 0, 1, stride=1, stride_axis=0)[:, :ATT_KW]
        for v in range(ATT_VARIANTS):
            bias_scr[v, hh] = toeplitz * LOG2E + band_ref[v]


ATT_STRIP = 32
ATT_SCALE = ATT_DH ** -0.5
LOG2E = 1.4426950408889634


def _strips(fn):
    def strip(r, carry):
        fn(pl.ds(pl.multiple_of(r * ATT_STRIP, ATT_STRIP), ATT_STRIP))
        return carry

    lax.fori_loop(0, ATT_CQ // ATT_STRIP, strip, 0, unroll=True)


def _attn_prepare(kpad, vpad, qkv_ref):
    kpad[0:BAND_PAST, :] = jnp.zeros((BAND_PAST, LANES), BF16)
    vpad[0:BAND_PAST, :] = jnp.zeros((BAND_PAST, LANES), BF16)
    kpad[BAND_PAST:, :] = qkv_ref[1]
    vpad[BAND_PAST:, :] = qkv_ref[2]


def _attn_queries(qkv_ref, qs):
    lane = lax.broadcasted_iota(jnp.int32, (1, LANES), 1)
    q = (qkv_ref[0, pl.ds(qs, ATT_CQ), :].astype(F32) * (ATT_SCALE * LOG2E)).astype(BF16)
    return [jnp.where((lane >= ATT_DH * hh) & (lane < ATT_DH * (hh + 1)), q, jnp.zeros_like(q)) for hh in range(2)]


def _attn_scores(s_ref, qkv_ref, kpad, b_ref, t, n_qb):
    t = jnp.minimum(t, n_qb - 1)
    qs = pl.multiple_of(t * ATT_CQ, ATT_CQ)
    variant = jnp.minimum(t, ATT_VARIANTS - 1)
    kw = kpad[pl.ds(qs, ATT_KW), :]
    qm = _attn_queries(qkv_ref, qs)
    for hh in range(2):
        s_ref[hh] = _dot_nt(qm[hh], kw) + b_ref[variant, hh]


def _attn_softmax(s_ref, e_ref, linv_ref, m_ref=None):
    def strip(rows):
        s = s_ref[rows, :]
        m = jnp.max(s, axis=-1, keepdims=True)
        e = jnp.exp2(s - m)
        e_ref[rows, :] = e.astype(BF16)
        linv_ref[rows, :] = jnp.broadcast_to(1.0 / jnp.sum(e, axis=-1, keepdims=True), (ATT_STRIP, LANES))
        if m_ref is not None:
            m_ref[rows, :] = jnp.broadcast_to(m, (ATT_STRIP, LANES))

    _strips(strip)


_RING_SPEC = pl.BlockSpec((None, 2, 8, ATT_RING), lambda hp, b: (hp, 0, 0, 0))
_BAND_SPEC = pl.BlockSpec((ATT_VARIANTS, ATT_CQ, ATT_KW), lambda hp, b: (0, 0, 0))


def _attn_fwd(qkv3, ring, B, S, name, ride=None):
    T = B * S
    n_qb = S // ATT_CQ

    def body(qkv_ref, ring_ref, band_ref, o_ref, kpad, vpad, b_ref, s_scr, e_scr, linv_scr):
        @pl.when(pl.program_id(1) == 0)
        def _():
            _attn_bias(b_ref, ring_ref, band_ref)

        _attn_prepare(kpad, vpad, qkv_ref)
        e_scr[...] = jnp.zeros_like(e_scr)
        linv_scr[...] = jnp.zeros_like(linv_scr)
        lane = lax.broadcasted_iota(jnp.int32, (1, LANES), 1)

        def softmax(slot):
            for hh in range(2):
                _attn_softmax(s_scr.at[slot, hh], e_scr.at[slot, hh], linv_scr.at[slot, hh])

        def output(t, slot):
            qs = pl.multiple_of(jnp.maximum(t, 0) * ATT_CQ, ATT_CQ)
            vw = vpad[pl.ds(qs, ATT_KW), :]
            outs = [_dot(e_scr[slot, hh], vw) * linv_scr[slot, hh] for hh in range(2)]
            o_ref[pl.ds(qs, ATT_CQ), :] = jnp.where(lane < ATT_DH, outs[0], outs[1]).astype(BF16)

        def pair(u, carry):
            t = 2 * u
            _attn_scores(s_scr.at[1], qkv_ref, kpad, b_ref, t + 1, n_qb)
            softmax(0)
            output(t - 1, 1)
            _attn_scores(s_scr.at[0], qkv_ref, kpad, b_ref, t + 2, n_qb)
            softmax(1)
            output(t, 0)
            return carry

        _attn_scores(s_scr.at[0], qkv_ref, kpad, b_ref, 0, n_qb)
        lax.fori_loop(0, n_qb // 2, pair, 0)
        output(n_qb - 1, 1)

    return _call(body, name=name, grid=(ATT_PAIRS, B),
                 in_specs=[pl.BlockSpec((3, S, LANES), lambda hp, b: (0, b, hp)), _RING_SPEC, _BAND_SPEC],
                 out_specs=[pl.BlockSpec((S, LANES), lambda hp, b: (b, hp))],
                 out_shape=[jax.ShapeDtypeStruct((T, D_MODEL), BF16)],
                 scratch_shapes=[pltpu.VMEM((S + BAND_PAST, LANES), BF16), pltpu.VMEM((S + BAND_PAST, LANES), BF16),
                                 pltpu.VMEM((ATT_VARIANTS, 2, ATT_CQ, ATT_KW), F32),
                                 pltpu.VMEM((2, 2, ATT_CQ, ATT_KW), F32),
                                 pltpu.VMEM((2, 2, ATT_CQ, ATT_KW), BF16), pltpu.VMEM((2, 2, ATT_CQ, LANES), F32)],
                 args=(qkv3, ring, jnp.asarray(_band_masks())), ride=ride)


def _attn_bwd(qkv3, ring, do, B, S, name, ride=None):
    T = B * S
    n_qb = S // ATT_CQ

    def body(qkv_ref, ring_ref, band_ref, do_ref, dqkv_ref, dring_ref, kpad, vpad, dkacc, dvacc, b_ref, db_ref):
        @pl.when(pl.program_id(1) == 0)
        def _():
            _attn_bias(b_ref, ring_ref, band_ref)
            db_ref[...] = jnp.zeros_like(db_ref)

        _attn_prepare(kpad, vpad, qkv_ref)
        dkacc[...] = jnp.zeros_like(dkacc)
        dvacc[...] = jnp.zeros_like(dvacc)
        lane = lax.broadcasted_iota(jnp.int32, (1, LANES), 1)

        def step(qb, carry):
            qs = pl.multiple_of(qb * ATT_CQ, ATT_CQ)
            variant = jnp.minimum(qb, ATT_VARIANTS - 1)
            dov = do_ref[pl.ds(qs, ATT_CQ), :].astype(F32)
            kw = kpad[pl.ds(qs, ATT_KW), :]
            vw = vpad[pl.ds(qs, ATT_KW), :]
            heads = (0, 1)
            qmh = _attn_queries(qkv_ref, qs)
            s = [_dot_nt(qmh[hh], kw) + b_ref[variant, hh] for hh in heads]
            e, linv, dom, dp = [None, None], [None, None], [None, None], [None, None]
            for hh in heads:
                e[hh] = jnp.exp2(s[hh] - jnp.max(s[hh], axis=-1, keepdims=True))
                linv[hh] = 1.0 / jnp.sum(e[hh], axis=-1, keepdims=True)
                sel = (lane >= ATT_DH * hh) & (lane < ATT_DH * (hh + 1))
                dom[hh] = jnp.where(sel, dov * linv[hh], 0.0).astype(BF16)
                dp[hh] = _dot_nt(dom[hh], vw)
            dqs, dk, dv = [], None, None
            for hh in heads:
                ds = e[hh] * (dp[hh] - jnp.sum(dp[hh] * e[hh], axis=-1, keepdims=True) * linv[hh])
                db_ref[hh] += ds
                dsb = ds.astype(BF16)
                dqs.append(_dot(dsb, kw) * ATT_SCALE)
                dk_h = _dot_tn(qmh[hh], dsb) * (1.0 / LOG2E)
                dv_h = _dot_tn(dom[hh], e[hh].astype(BF16))
                dk = dk_h if dk is None else dk + dk_h
                dv = dv_h if dv is None else dv + dv_h
            dqkv_ref[0, pl.ds(qs, ATT_CQ), :] = jnp.where(lane < ATT_DH, dqs[0], dqs[1]).astype(BF16)
            dkacc[:, pl.ds(qs, ATT_KW)] += dk
            dvacc[:, pl.ds(qs, ATT_KW)] += dv
            return carry

        lax.fori_loop(0, n_qb, step, 0, unroll=4)
        dqkv_ref[1] = dkacc[:, BAND_PAST:].T.astype(BF16)
        dqkv_ref[2] = dvacc[:, BAND_PAST:].T.astype(BF16)

        @pl.when(pl.program_id(1) == B - 1)
        def _():
            r = lax.broadcasted_iota(jnp.int32, (ATT_CQ, ATT_CQ), 0)
            c = lax.broadcasted_iota(jnp.int32, (ATT_CQ, ATT_CQ), 1)
            reverse = jnp.where(r + c == ATT_CQ - 1, 1.0, 0.0).astype(BF16)
            for hh in range(2):
                x = jnp.concatenate([db_ref[hh], jnp.zeros((ATT_CQ, ATT_RING - ATT_KW), F32)], axis=1)
                flipped = jnp.zeros((ATT_CQ, ATT_RING), F32)
                for _ in range(3):
                    part = x.astype(BF16)
                    flipped = flipped + _dot(reverse, part)
                    x = x - part.astype(F32)
                aligned = pltpu.roll(flipped, ATT_KW + 1, 1, stride=1, stride_axis=0)
                dring_ref[hh] = jnp.sum(aligned.reshape(ATT_CQ // 8, 8, ATT_RING), axis=0)

    qkv_spec = pl.BlockSpec((3, S, LANES), lambda hp, b: (0, b, hp))
    return _call(body, name=name, grid=(ATT_PAIRS, B),
                 in_specs=[qkv_spec, _RING_SPEC, _BAND_SPEC, pl.BlockSpec((S, LANES), lambda hp, b: (b, hp))],
                 out_specs=[qkv_spec, _RING_SPEC],
                 out_shape=[jax.ShapeDtypeStruct((3, T, D_MODEL), BF16),
                            jax.ShapeDtypeStruct((ATT_PAIRS, 2, 8, ATT_RING), F32)],
                 scratch_shapes=[pltpu.VMEM((S + BAND_PAST, LANES), BF16), pltpu.VMEM((S + BAND_PAST, LANES), BF16),
                                 pltpu.VMEM((LANES, S + BAND_PAST), F32), pltpu.VMEM((LANES, S + BAND_PAST), F32),
                                 pltpu.VMEM((ATT_VARIANTS, 2, ATT_CQ, ATT_KW), F32), pltpu.VMEM((2, ATT_CQ, ATT_KW), F32)],
                 args=(qkv3, ring, jnp.asarray(_band_masks()), do), ride=ride)


def _rel_bias_grad(dring, name):
    fold = np.zeros((ATT_RING, REL_PAD), np.float32)
    fold[np.arange(ATT_RING), _ring_index()] = 1.0
    fold = jnp.asarray(fold, BF16)

    def body(d_ref, f_ref, o_ref):
        x = jnp.sum(d_ref[...], axis=0, keepdims=True)
        x = jnp.broadcast_to(x, (8, ATT_RING))
        acc = jnp.zeros((8, REL_PAD), F32)
        for _ in range(3):
            part = x.astype(BF16)
            acc = acc + _dot(part, f_ref[...])
            x = x - part.astype(F32)
        o_ref[...] = acc[0:1, :]

    out = _call(body, name=name, grid=(ATT_HEADS,),
                in_specs=[pl.BlockSpec((None, 8, ATT_RING), lambda h: (h, 0, 0)),
                          pl.BlockSpec((ATT_RING, REL_PAD), lambda h: (0, 0))],
                out_specs=[pl.BlockSpec((None, 1, REL_PAD), lambda h: (h, 0, 0))],
                out_shape=[jax.ShapeDtypeStruct((ATT_HEADS, 1, REL_PAD), F32)],
                args=(dring.reshape(ATT_HEADS, 8, ATT_RING), fold))[0]
    return out.reshape(ATT_HEADS, REL_PAD)


def _adamw(w, g, m, v):
    m = ADAM_B1 * m + (1.0 - ADAM_B1) * g
    v = ADAM_B2 * v + (1.0 - ADAM_B2) * (g * g)
    m_hat = m / (1.0 - ADAM_B1 ** ADAM_STEP)
    v_hat = v / (1.0 - ADAM_B2 ** ADAM_STEP)
    delta = -ADAM_LR * (m_hat / (jnp.sqrt(v_hat) + ADAM_EPS) + ADAM_WD * w)
    return delta, m, v


def _sum_devices(ref):
    g = ref[0].astype(F32)
    for d in range(1, N_DEV):
        g = g + ref[d].astype(F32)
    return g


def _adamw_reduce(lands, w, m, v, name, tr=256):
    L, R, C = w.shape
    tr = min(tr, R)
    n_i = R // tr

    def body(*refs):
        l_refs = refs[:L]
        w_ref, m_ref, v_ref, g_out, d_out, m_out, v_out = refs[L:]
        layer = pl.program_id(0)
        for l in range(L):
            @pl.when(layer == l)
            def _(l=l):
                g = _sum_devices(l_refs[l])
                g_out[...] = g
                d_out[...], m_out[...], v_out[...] = _adamw(w_ref[...], g, m_ref[...], v_ref[...])

    def land_spec(l):
        return pl.BlockSpec((N_DEV, tr, C), lambda ly, i: (0, jnp.where(ly == l, i, jnp.where(ly < l, 0, n_i - 1)), 0))

    blk = pl.BlockSpec((None, tr, C), lambda ly, i: (ly, i, 0))
    return _call(body, name=name, grid=(L, n_i), in_specs=[land_spec(l) for l in range(L)] + [blk, blk, blk],
                 out_specs=[blk] * 4, out_shape=[jax.ShapeDtypeStruct((L, R, C), F32)] * 4, args=(*lands, w, m, v))


def _small_update(land_small, land_rel, w, m, v, name):
    def body(ls_ref, lr_ref, w_ref, m_ref, v_ref, g_out, d_out, m_out, v_out, rel_out):
        g = _sum_devices(ls_ref)
        g_out[...] = g
        d_out[...], m_out[...], v_out[...] = _adamw(w_ref[...], g, m_ref[...], v_ref[...])
        rel_out[...] = _sum_devices(lr_ref)

    return pl.pallas_call(
        body, name=name,
        out_shape=[jax.ShapeDtypeStruct(w.shape, F32)] * 4 + [jax.ShapeDtypeStruct(land_rel.shape[1:], F32)],
    )(land_small, land_rel, w, m, v)


def _adamw_plain(w, g, m, v, name):
    def body(w_ref, g_ref, m_ref, v_ref, d_out, m_out, v_out):
        d_out[...], m_out[...], v_out[...] = _adamw(w_ref[...], g_ref[...], m_ref[...], v_ref[...])

    return pl.pallas_call(body, name=name, out_shape=[jax.ShapeDtypeStruct(w.shape, F32)] * 3)(w, g, m, v)


def _ret_piece_of_column_block(p):
    per_head = RET_HEAD_COLS // RET_DK
    qk = jnp.where(p < RET_HEADS, per_head * p, per_head * (p - RET_HEADS) + 1)
    pv = p - 2 * RET_HEADS
    vv = per_head * (pv // 2) + 2 + pv % 2
    pg = p - 4 * RET_HEADS
    gg = per_head * (pg // 2) + 4 + pg % 2
    return jnp.where(p < 2 * RET_HEADS, qk, jnp.where(p < 4 * RET_HEADS, vv, gg))


def _step(x, target, shards, mix_g, gn_g, rel_shard, mlp_g, final_g):
    s_rin, s_rout, s_ain, s_aout, s_w1, s_w2 = shards
    B, S, _ = x.shape
    T = B * S
    tk = min(T, 2048)
    n_k = T // tk
    h0 = x.reshape(T, D_MODEL)
    tgt = target.reshape(T, D_MODEL)
    consts = _ret_constants(S)
    w1_cols = w2_rows = D_FF // N_DEV

    rel_cols = REL_TABLE // N_DEV
    per_shard = RET_IN // N_DEV // RET_DK

    def head_layout(r):
        return lambda ref, blk: ref.at[:, pl.ds(pl.multiple_of(
            _ret_piece_of_column_block(per_shard * blk + r) * RET_DK, LANES), RET_DK)]

    w_rin_heads, rel_all = _ChipGather(
        [s_rin[:, r * RET_DK:(r + 1) * RET_DK] for r in range(per_shard)] + [rel_shard],
        [jax.ShapeDtypeStruct((D_MODEL, RET_IN), BF16), jax.ShapeDtypeStruct((N_DEV, ATT_HEADS, rel_cols), F32)],
        [0] * per_shard + [1], [head_layout(r) for r in range(per_shard)] + [_slot]).run("gather_first")
    ring = _bias_ring(rel_all.transpose(1, 0, 2).reshape(ATT_HEADS, REL_TABLE))

    def gather(*items):
        return _ChipGather([s for s, _, _ in items], [jax.ShapeDtypeStruct(full, BF16) for _, _, full in items],
                           list(range(len(items))), [view for _, view, _ in items])

    def scatter(*items):
        return _Exchange([p for p, _, _ in items], [(a, view, a, _slot) for a, (_, view, _) in enumerate(items)],
                         [jax.ShapeDtypeStruct((N_DEV, *shard), BF16) for _, _, shard in items])

    w1_full, w2_full = (D_MODEL, D_FF), (D_FF, D_MODEL)
    w1_shard, w2_shard = (D_MODEL, w1_cols), (w2_rows, D_MODEL)

    ride = gather((s_rout, _rows(RET_VW // N_DEV), (RET_VW, D_MODEL)), (s_w1[0], _cols(w1_cols), w1_full))
    (proj, n0), (w_rout, w1_0) = _ret_proj(h0, mix_g[0:1], w_rin_heads, consts[0], consts[1], S, "ret_proj", 512, ride=ride)
    (y, o, states), (w2_0,) = _ret_fwd(proj, consts, gn_g, B, S, "ret_fwd", ride=gather((s_w2[0], _rows(w2_rows), w2_full)))
    h1 = _matmul_res(y, w_rout, h0, "ret_out")
    ride = gather((s_ain, _cols(3 * D_MODEL // N_DEV), (D_MODEL, 3 * D_MODEL)),
                  (s_aout, _rows(D_MODEL // N_DEV), (D_MODEL, D_MODEL)))
    (h2, n1, u1), (w_ain, w_aout) = _mlp_fwd(h1, mlp_g[0:1], w1_0, w2_0, "mlp0_fwd", ride=ride)
    qkv3, n2 = _norm_proj(h2, mix_g[1:2], w_ain, 3, "att_proj", 1024)
    ride = gather((s_w1[1], _cols(w1_cols), w1_full), (s_w2[1], _rows(w2_rows), w2_full))
    (o2,), (w1_1, w2_1) = _attn_fwd(qkv3, ring, B, S, "att_fwd", ride=ride)
    h3 = _matmul_res(o2, w_aout, h2, "att_out")
    h4, n3, u3 = _mlp_fwd(h3, mlp_g[1:2], w1_1, w2_1, "mlp1_fwd")
    dh4, dh4b, loss, dg_final = _final_loss(h4, final_g, tgt, "final_loss")

    def tok(width):
        return dict(a=pl.BlockSpec((tk, width), lambda i, j, k: (k, i)), b=pl.BlockSpec((tk, width), lambda i, j, k: (k, j)))

    def mlp_dw2(u, dhb, tag, ride=None):
        return _wgrad(u, tok(1024)["a"], dhb, tok(1024)["b"], D_FF, D_MODEL, 1024, 1024, n_k, tag + "_dw2",
                      square_a=True, ride=ride)

    def mlp_dw1(n, du, tag):
        return _wgrad(n, tok(1024)["a"], du, tok(1024)["b"], D_MODEL, D_FF, 1024, 1024, n_k, tag + "_dw1")

    dh3, dh3b, du3, dg_mlp1 = _mlp_bwd(dh4, u3, w1_1, w2_1, h3, mlp_g[1:2], "mlp1_bwd")
    gw2_1, gw1_1 = mlp_dw2(u3, dh4b, "mlp1"), mlp_dw1(n3, du3, "mlp1")
    do2 = _matmul_nt(dh3b, w_aout, "att_out_bwd")
    g_aout = _wgrad(o2, tok(1024)["a"], dh3b, tok(1024)["b"], D_MODEL, D_MODEL, 1024, 1024, n_k, "att_out_dw")
    ride = scatter((gw1_1, _cols(w1_cols), w1_shard), (gw2_1, _rows(w2_rows), w2_shard),
                   (g_aout, _rows(D_MODEL // N_DEV), (D_MODEL // N_DEV, D_MODEL)))
    (dqkv3, dring), (l_w1_1, l_w2_1, l_aout) = _attn_bwd(qkv3, ring, do2, B, S, "att_bwd", ride=ride)
    g_rel = _rel_bias_grad(dring, "rel_bias_grad")
    dh2, dh2b, dg_mix1 = _proj_bwd(dqkv3, w_ain, h2, mix_g[1:2], dh3, "att_proj_bwd", 1024)
    g_ain = _wgrad(n2, tok(1024)["a"], dqkv3, pl.BlockSpec((None, tk, D_MODEL), lambda i, j, k: (j, k, 0)),
                   D_MODEL, 3 * D_MODEL, 1024, 1024, n_k, "att_proj_dw")
    ride = scatter((g_ain, _cols(3 * D_MODEL // N_DEV), (D_MODEL, 3 * D_MODEL // N_DEV)))
    (dh1, dh1b, du1, dg_mlp0), (l_ain,) = _mlp_bwd(dh2, u1, w1_0, w2_0, h1, mlp_g[0:1], "mlp0_bwd", ride=ride)
    dy = _matmul_nt(dh1b, w_rout, "ret_out_bwd")
    g_rout = _wgrad(y, tok(1024)["a"], dh1b, tok(1024)["b"], RET_VW, D_MODEL, 1024, 1024, n_k, "ret_out_dw")
    gw2_0, (l_rout,) = mlp_dw2(u1, dh2b, "mlp0", ride=scatter((g_rout, _rows(RET_VW // N_DEV), (RET_VW // N_DEV, D_MODEL))))
    gw1_0 = mlp_dw1(n1, du1, "mlp0")
    ride = scatter((gw2_0, _rows(w2_rows), w2_shard), (gw1_0, _cols(w1_cols), w1_shard))
    (dproj, dgn), (l_w2_0, l_w1_0) = _ret_bwd(proj, consts, gn_g, o, states, dy, B, S, "ret_bwd", ride=ride)
    g_rin = _wgrad(n0, tok(1024)["a"], dproj,
                   [pl.BlockSpec((tk, RET_DK), functools.partial(
                       lambda i, j, k, r: (k, _ret_piece_of_column_block(per_shard * j + r)), r=r)) for r in range(per_shard)],
                   D_MODEL, RET_IN, 1024, per_shard * RET_DK, n_k, "ret_proj_dw")
    ride = scatter((g_rin, _cols(RET_IN // N_DEV), (D_MODEL, RET_IN // N_DEV)))
    (dx, _, dg_mix0), (l_rin,) = _proj_bwd(dproj, w_rin_heads, h0, mix_g[0:1], dh1, "ret_proj_bwd", 512, ride=ride)

    small = jnp.concatenate([dg_mix0, dg_mix1, dg_mlp0, dg_mlp1, dg_final, dgn.reshape(2, D_MODEL),
                             jnp.broadcast_to(loss[0:1, 0:1], (1, D_MODEL))], axis=0)
    l_small, l_rel = _Exchange(
        [small, g_rel], [(0, _whole, 0, _slot), (1, _whole, 1, _slot)],
        [jax.ShapeDtypeStruct((N_DEV, 8, D_MODEL), F32), jax.ShapeDtypeStruct((N_DEV, ATT_HEADS, REL_PAD), F32)],
    ).run("scatter_small")
    lands = dict(rin=l_rin, rout=l_rout, ain=l_ain, aout=l_aout, w1=(l_w1_0, l_w1_1), w2=(l_w2_0, l_w2_1),
                 small=l_small, rel=l_rel)
    return dx.reshape(B, S, D_MODEL), lands


def kernel(x, mix_norm_g, ret_w_in, ret_gn_g, ret_w_out, att_w_in, att_rel_bias, att_w_out, mlp_norm_g, mlp_w1, mlp_w2, final_norm_g, loss_target, m_mix_norm_g, m_ret_w_in, m_ret_gn_g, m_ret_w_out, m_att_w_in, m_att_rel_bias, m_att_w_out, m_mlp_norm_g, m_mlp_w1, m_mlp_w2, m_final_norm_g, v_mix_norm_g, v_ret_w_in, v_ret_gn_g, v_ret_w_out, v_att_w_in, v_att_rel_bias, v_att_w_out, v_mlp_norm_g, v_mlp_w1, v_mlp_w2, v_final_norm_g):
    me = _lin(_place())
    rel_cols = REL_TABLE // N_DEV
    shards = (ret_w_in[0].astype(BF16), ret_w_out[0].astype(BF16), att_w_in[0].astype(BF16), att_w_out[0].astype(BF16),
              (mlp_w1[0].astype(BF16), mlp_w1[1].astype(BF16)), (mlp_w2[0].astype(BF16), mlp_w2[1].astype(BF16)))
    grad_x, lands = _step(x, loss_target, shards, mix_norm_g, ret_gn_g, att_rel_bias[0], mlp_norm_g,
                          final_norm_g.reshape(1, D_MODEL))

    u_rin = _adamw_reduce([lands["rin"]], ret_w_in, m_ret_w_in, v_ret_w_in, "update_ret_w_in")
    u_rout = _adamw_reduce([lands["rout"]], ret_w_out, m_ret_w_out, v_ret_w_out, "update_ret_w_out")
    u_ain = _adamw_reduce([lands["ain"]], att_w_in, m_att_w_in, v_att_w_in, "update_att_w_in")
    u_aout = _adamw_reduce([lands["aout"]], att_w_out, m_att_w_out, v_att_w_out, "update_att_w_out")
    u_w1 = _adamw_reduce(lands["w1"], mlp_w1, m_mlp_w1, v_mlp_w1, "update_mlp_w1")
    u_w2 = _adamw_reduce(lands["w2"], mlp_w2, m_mlp_w2, v_mlp_w2, "update_mlp_w2")

    def pack(mix, mlp, fin, gn):
        return jnp.concatenate([mix, mlp, fin.reshape(1, D_MODEL), gn.reshape(2, D_MODEL), jnp.zeros((1, D_MODEL), F32)], axis=0)

    small_w = pack(mix_norm_g, mlp_norm_g, final_norm_g, ret_gn_g)
    small_m = pack(m_mix_norm_g, m_mlp_norm_g, m_final_norm_g, m_ret_gn_g)
    small_v = pack(v_mix_norm_g, v_mlp_norm_g, v_final_norm_g, v_ret_gn_g)
    sg, sd, sm, sv, rel_sum = _small_update(lands["small"], lands["rel"], small_w, small_m, small_v, "update_small")
    g_rel_mine = lax.dynamic_slice(rel_sum, (0, me * rel_cols), (ATT_HEADS, rel_cols))
    rel_d, rel_m, rel_v = _adamw_plain(att_rel_bias[0], g_rel_mine, m_att_rel_bias[0], v_att_rel_bias[0], "update_rel_bias")
    u_rel = [g_rel_mine[None], rel_d[None], rel_m[None], rel_v[None]]

    def unpack(t):
        return dict(mix=t[0:2], mlp=t[2:4], fin=t[4], gn=t[5:7].reshape(1, RET_VW))

    us = [unpack(t) for t in (sg, sd, sm, sv)]
    outs = [sg[7, 0], grad_x]
    for k in range(4):
        outs += [us[k]["mix"], u_rin[k], us[k]["gn"], u_rout[k], u_ain[k], u_rel[k], u_aout[k], us[k]["mlp"],
                 u_w1[k], u_w2[k], us[k]["fin"]]
    return tuple(outs)
```

```python
import functools

import numpy as np
import jax
import jax.numpy as jnp
from jax import lax
from jax.experimental import pallas as pl
from jax.experimental.pallas import tpu as pltpu

F32, BF16 = jnp.float32, jnp.bfloat16

D_MODEL = 1024
CHUNK = 64
RET_HEADS, RET_DK, RET_DV = 4, 256, 512
RET_QK, RET_VW = RET_HEADS * RET_DK, RET_HEADS * RET_DV
RET_IN = 2 * RET_QK + 2 * RET_VW
RET_HEAD_COLS = 2 * RET_DK + 2 * RET_DV
RET_SC = 256
ROPE_BASE = 10000.0
ATT_HEADS, ATT_DH = 16, 64
LANES = 128
ATT_PAIRS = ATT_HEADS * ATT_DH // LANES
BAND_PAST = 8 * CHUNK
MAX_REL = 256
REL_TABLE = MAX_REL + CHUNK
REL_PAD = 384
ATT_CQ = 256
ATT_KW = ATT_CQ + BAND_PAST
ATT_RING = ATT_CQ + ATT_KW
ATT_VARIANTS = BAND_PAST // ATT_CQ + 1
D_FF = 4 * D_MODEL
EPS = 1e-6
EPILOGUE_ROWS = 256
NEG = -1e30
N_DEV = 8
N_PEERS = N_DEV - 1

ADAM_LR, ADAM_B1, ADAM_B2, ADAM_EPS, ADAM_WD, ADAM_STEP = 0.001, 0.9, 0.999, 1e-08, 0.01, 10

VMEM_LIMIT = 56 * 1024 * 1024
MESH = pl.DeviceIdType.MESH
ANY = pl.BlockSpec(memory_space=pl.ANY)


def _dot(a, b):
    return jnp.dot(a, b, preferred_element_type=F32)


def _dot_nt(a, b):
    return lax.dot_general(a, b, (((1,), (1,)), ((), ())), preferred_element_type=F32)


def _dot_tn(a, b):
    return lax.dot_general(a, b, (((0,), (0,)), ((), ())), preferred_element_type=F32)


def _rms(x, g):
    r = lax.rsqrt(jnp.mean(x * x, axis=-1, keepdims=True) + EPS)
    return x * r * g


def _rms_bwd(dn, x, g):
    r = lax.rsqrt(jnp.mean(x * x, axis=-1, keepdims=True) + EPS)
    xh = x * r
    dg = jnp.sum(dn * xh, axis=0, keepdims=True)
    dxh = dn * g
    dx = r * (dxh - xh * jnp.mean(dxh * xh, axis=-1, keepdims=True))
    return dx, dg


def _rms_bwd_epilogue(x_ref, g_ref, dres_ref, dx_ref, dxb_ref, dg_ref):
    for r in range(0, dx_ref.shape[0], EPILOGUE_ROWS):
        rows = slice(r, r + EPILOGUE_ROWS)
        dx, dg = _rms_bwd(dx_ref[rows, :], x_ref[rows, :], g_ref[...])
        dx = dres_ref[rows, :] + dx
        dx_ref[rows, :] = dx
        dxb_ref[rows, :] = dx.astype(BF16)
        dg_ref[...] += dg


def _place():
    return lax.axis_index("x"), lax.axis_index("y"), lax.axis_index("c")


def _lin(p):
    return 4 * p[0] + 2 * p[1] + p[2]


def _cols(width):
    return lambda ref, blk: ref.at[:, pl.ds(pl.multiple_of(blk * width, LANES), width)]


def _rows(height):
    return lambda ref, blk: ref.at[pl.ds(pl.multiple_of(blk * height, 8), height), :]


def _slot(ref, blk):
    return ref.at[blk]


def _whole(ref, blk):
    return ref


class _Exchange:
    def __init__(self, sources, flows, land_shapes):
        self.sources, self.flows, self.land_shapes = list(sources), flows, list(land_shapes)
        n_f = len(flows)
        self.sem_shapes = [pltpu.SemaphoreType.DMA((N_PEERS * n_f,)), pltpu.SemaphoreType.DMA((N_PEERS * n_f,)),
                           pltpu.SemaphoreType.DMA((n_f,))]

    def _copies(self, srcs, lands, sems):
        send_sems, recv_sems, local_sems = sems
        x, y, c = _place()
        me = (x, y, c)
        peers = [(x ^ dx, y ^ dy, c ^ dc) for dx in (0, 1) for dy in (0, 1) for dc in (0, 1)][1:]

        def copy(f, k, sender, to):
            si, sview, li, lview = self.flows[f]
            return pltpu.make_async_remote_copy(
                src_ref=sview(srcs[si], _lin(to)), dst_ref=lview(lands[li], _lin(sender)),
                send_sem=send_sems.at[f * N_PEERS + k], recv_sem=recv_sems.at[f * N_PEERS + k],
                device_id=to, device_id_type=MESH)

        mine, sends, recvs = [], [], []
        for f, (si, sview, li, lview) in enumerate(self.flows):
            mine.append(pltpu.make_async_copy(sview(srcs[si], _lin(me)), lview(lands[li], _lin(me)), local_sems.at[f]))
            for k, peer in enumerate(peers):
                sends.append(copy(f, k, me, peer))
                recvs.append(copy(f, k, peer, me))
        return mine, sends, recvs

    def start(self, srcs, lands, sems):
        mine, sends, _ = self._copies(srcs, lands, sems)
        for cp in mine + sends:
            cp.start()

    def finish(self, srcs, lands, sems):
        mine, sends, recvs = self._copies(srcs, lands, sems)
        for cp in recvs:
            cp.wait_recv()
        for cp in sends:
            cp.wait_send()
        for cp in mine:
            cp.wait()

    def run(self, name):
        n_src, n_land = len(self.sources), len(self.land_shapes)

        def body(*refs):
            srcs, lands, sems = refs[:n_src], refs[n_src:n_src + n_land], refs[n_src + n_land:]
            self.start(srcs, lands, sems)
            self.finish(srcs, lands, sems)

        return pl.pallas_call(body, name=name, in_specs=[ANY] * n_src, out_specs=[ANY] * n_land,
                              out_shape=self.land_shapes, scratch_shapes=self.sem_shapes)(*self.sources)


def _call(body, *, name, grid, in_specs, out_specs, out_shape, args, scratch_shapes=(), ride=None):
    params = pltpu.CompilerParams(dimension_semantics=("arbitrary",) * len(grid), vmem_limit_bytes=VMEM_LIMIT)
    in_specs, out_specs, out_shape, scratch_shapes = list(in_specs), list(out_specs), list(out_shape), list(scratch_shapes)
    if ride is None:
        return pl.pallas_call(body, name=name, grid=grid, in_specs=in_specs, out_specs=out_specs, out_shape=out_shape,
                              scratch_shapes=scratch_shapes, compiler_params=params)(*args)
    n_in, n_out, n_scr = len(in_specs), len(out_specs), len(scratch_shapes)
    n_src, n_land = len(ride.sources), len(ride.land_shapes)

    def riding(*refs):
        bounds = np.cumsum([n_in, n_src, n_out, n_land, n_scr])
        ins, srcs, outs, lands, scr, sems = (refs[a:b] for a, b in zip([0, *bounds], [*bounds, len(refs)]))
        step = functools.reduce(lambda a, d: a * grid[d] + pl.program_id(d), range(len(grid)), 0)
        n_steps = int(np.prod(grid))

        @pl.when(step == 0)
        def _():
            ride.start(srcs, lands, sems)

        if hasattr(ride, "middle"):
            @pl.when(step == (5 * n_steps) // 8)
            def _():
                ride.middle(srcs, lands, sems)

        body(*ins, *outs, *scr)

        @pl.when(step == n_steps - 1)
        def _():
            ride.finish(srcs, lands, sems)

    res = pl.pallas_call(
        riding, name=name, grid=grid, in_specs=in_specs + [ANY] * n_src, out_specs=out_specs + [ANY] * n_land,
        out_shape=out_shape + ride.land_shapes, scratch_shapes=scratch_shapes + ride.sem_shapes,
        compiler_params=params)(*args, *ride.sources)
    return res[:n_out], res[n_out:]


class _ChipGather:
    def __init__(self, shards, out_shapes, out_of, views):
        self.sources, self.land_shapes, self.out_of, self.views = list(shards), list(out_shapes), out_of, views
        n = len(shards)
        self.sem_shapes = [pltpu.SemaphoreType.DMA((N_PEERS * n,)), pltpu.SemaphoreType.DMA((N_PEERS * n,)),
                           pltpu.SemaphoreType.DMA((n,))]

    def _copies(self, srcs, lands, sems):
        send_sems, recv_sems, local_sems = sems
        n = len(srcs)
        outs = [lands[self.out_of[a]] for a in range(n)]
        x, y, c = _place()
        me, sibling = (x, y, c), (x, y, 1 - c)
        chips = [(1 - x, y), (x, 1 - y), (1 - x, 1 - y)]

        def copy(a, k, block, to, own=False):
            dst = self.views[a](outs[a], _lin(block))
            return pltpu.make_async_remote_copy(
                src_ref=srcs[a] if own else dst, dst_ref=dst,
                send_sem=send_sems.at[a * N_PEERS + k], recv_sem=recv_sems.at[a * N_PEERS + k],
                device_id=to, device_id_type=MESH)

        mine = lambda: [pltpu.make_async_copy(srcs[a], self.views[a](outs[a], _lin(me)), local_sems.at[a]) for a in range(n)]
        first = lambda: [cp for a in range(n) for cp in
                         [copy(a, 0, me, sibling, own=True)] + [copy(a, 1 + j, me, (*chip, c), own=True) for j, chip in enumerate(chips)]]
        landed = lambda a, j: copy(a, 1 + j, (*chips[j], c), me)
        passed = lambda a, j: copy(a, 4 + j, (*chips[j], c), sibling)
        from_sibling = lambda a: [copy(a, 0, sibling, me)] + [copy(a, 4 + j, (*chips[j], 1 - c), me) for j in range(3)]
        return n, mine, first, landed, passed, from_sibling

    def start(self, srcs, lands, sems):
        _, mine, first, _, _, _ = self._copies(srcs, lands, sems)
        for cp in mine() + first():
            cp.start()

    def middle(self, srcs, lands, sems):
        n, _, _, landed, passed, _ = self._copies(srcs, lands, sems)
        for j in range(3):
            for a in range(n):
                landed(a, j).wait_recv()
                passed(a, j).start()

    def finish(self, srcs, lands, sems):
        n, mine, first, _, passed, from_sibling = self._copies(srcs, lands, sems)
        for a in range(n):
            for cp in from_sibling(a):
                cp.wait_recv()
        for cp in first() + [passed(a, j) for j in range(3) for a in range(n)]:
            cp.wait_send()
        for cp in mine():
            cp.wait()

    def run(self, name):
        n, n_out = len(self.sources), len(self.land_shapes)

        def body(*refs):
            parts = refs[:n], refs[n:n + n_out], refs[n + n_out:]
            self.start(*parts)
            self.middle(*parts)
            self.finish(*parts)

        return pl.pallas_call(body, name=name, in_specs=[ANY] * n, out_specs=[ANY] * n_out,
                              out_shape=self.land_shapes, scratch_shapes=self.sem_shapes)(*self.sources)


def _resident(shape):
    return pl.BlockSpec(shape, lambda i: (0,) * len(shape), pipeline_mode=pl.Buffered(1))


def _norm_proj(h, g, w, groups, name, tm, ride=None):
    T = h.shape[0]
    N = w.shape[1]
    width = N // groups

    def body(h_ref, g_ref, w_ref, o_ref, n_ref):
        n = _rms(h_ref[...], g_ref[...]).astype(BF16)
        n_ref[...] = n
        if groups == 1:
            o_ref[...] = _dot(n, w_ref[...]).astype(BF16)
        else:
            for p in range(groups):
                o_ref[p] = _dot(n, w_ref[:, p * width:(p + 1) * width]).astype(BF16)

    row = pl.BlockSpec((tm, D_MODEL), lambda i: (i, 0))
    if groups == 1:
        out_spec, out_shape = pl.BlockSpec((tm, N), lambda i: (i, 0)), jax.ShapeDtypeStruct((T, N), BF16)
    else:
        out_spec = pl.BlockSpec((groups, tm, width), lambda i: (0, i, 0))
        out_shape = jax.ShapeDtypeStruct((groups, T, width), BF16)
    return _call(body, name=name, grid=(T // tm,),
                 in_specs=[row, pl.BlockSpec((1, D_MODEL), lambda i: (0, 0)), _resident(w.shape)],
                 out_specs=[out_spec, row], out_shape=[out_shape, jax.ShapeDtypeStruct((T, D_MODEL), BF16)],
                 args=(h, g, w), ride=ride)


def _ret_proj(h, g, w, cos, sin, S, name, tm, ride=None):
    T = h.shape[0]
    per_seq = S // tm

    def body(h_ref, g_ref, w_ref, cos_ref, sin_ref, o_ref, n_ref):
        n = _rms(h_ref[...], g_ref[...]).astype(BF16)
        n_ref[...] = n
        cs, sn = cos_ref[...], sin_ref[...]
        for hd in range(RET_HEADS):
            c0 = hd * RET_HEAD_COLS
            out = _dot(n, w_ref[:, c0:c0 + RET_HEAD_COLS])
            o_ref[:, c0:c0 + RET_DK] = _rope(out[:, 0:RET_DK], cs, sn).astype(BF16)
            o_ref[:, c0 + RET_DK:c0 + 2 * RET_DK] = (_rope(out[:, RET_DK:2 * RET_DK], cs, sn) * (RET_DK ** -0.5)).astype(BF16)
            o_ref[:, c0 + 2 * RET_DK:c0 + RET_HEAD_COLS] = out[:, 2 * RET_DK:].astype(BF16)

    row = pl.BlockSpec((tm, D_MODEL), lambda i: (i, 0))
    trig = pl.BlockSpec((tm, RET_DK // 2), lambda i: (i % per_seq, 0))
    return _call(body, name=name, grid=(T // tm,),
                 in_specs=[row, pl.BlockSpec((1, D_MODEL), lambda i: (0, 0)), _resident(w.shape), trig, trig],
                 out_specs=[pl.BlockSpec((tm, w.shape[1]), lambda i: (i, 0)), row],
                 out_shape=[jax.ShapeDtypeStruct((T, w.shape[1]), BF16), jax.ShapeDtypeStruct((T, D_MODEL), BF16)],
                 args=(h, g, w, cos, sin), ride=ride)


def _proj_bwd(dy, w, x, g, dres, name, tm, ride=None):
    T = x.shape[0]
    groups = dy.shape[0] if dy.ndim == 3 else 1
    width = w.shape[1] // groups

    def body(dy_ref, w_ref, x_ref, g_ref, dres_ref, dx_ref, dxb_ref, dg_ref):
        @pl.when(pl.program_id(0) == 0)
        def _():
            dg_ref[...] = jnp.zeros_like(dg_ref)

        if groups == 1:
            dn = _dot_nt(dy_ref[...], w_ref[...])
        else:
            dn = sum(_dot_nt(dy_ref[p], w_ref[:, p * width:(p + 1) * width]) for p in range(groups))
        dx_ref[...] = dn
        _rms_bwd_epilogue(x_ref, g_ref, dres_ref, dx_ref, dxb_ref, dg_ref)

    row = pl.BlockSpec((tm, D_MODEL), lambda i: (i, 0))
    vec = pl.BlockSpec((1, D_MODEL), lambda i: (0, 0))
    dy_spec = (pl.BlockSpec((tm, w.shape[1]), lambda i: (i, 0)) if groups == 1
               else pl.BlockSpec((groups, tm, width), lambda i: (0, i, 0)))
    return _call(body, name=name, grid=(T // tm,),
                 in_specs=[dy_spec, _resident(w.shape), row, vec, row], out_specs=[row, row, vec],
                 out_shape=[jax.ShapeDtypeStruct((T, D_MODEL), F32), jax.ShapeDtypeStruct((T, D_MODEL), BF16),
                            jax.ShapeDtypeStruct((1, D_MODEL), F32)],
                 args=(dy, w, x, g, dres), ride=ride)


def _matmul_res(a, w, res, name, tm=1024):
    T, K = a.shape

    def body(a_ref, w_ref, r_ref, o_ref):
        o_ref[...] = r_ref[...] + _dot(a_ref[...], w_ref[...])

    row = pl.BlockSpec((tm, D_MODEL), lambda i: (i, 0))
    return _call(body, name=name, grid=(T // tm,),
                 in_specs=[pl.BlockSpec((tm, K), lambda i: (i, 0)), pl.BlockSpec((K, D_MODEL), lambda i: (0, 0)), row],
                 out_specs=[row], out_shape=[jax.ShapeDtypeStruct((T, D_MODEL), F32)], args=(a, w, res))[0]


def _matmul_nt(dy, w, name, tm=1024):
    T, N = dy.shape
    K = w.shape[0]

    def body(dy_ref, w_ref, o_ref):
        o_ref[...] = _dot_nt(dy_ref[...], w_ref[...]).astype(BF16)

    return _call(body, name=name, grid=(T // tm,),
                 in_specs=[pl.BlockSpec((tm, N), lambda i: (i, 0)), pl.BlockSpec((K, N), lambda i: (0, 0))],
                 out_specs=[pl.BlockSpec((tm, K), lambda i: (i, 0))], out_shape=[jax.ShapeDtypeStruct((T, K), BF16)],
                 args=(dy, w))[0]


def _wgrad(a, a_spec, b, b_specs, m, n, bm, bn, n_k, name, square_a=False, ride=None):
    b_specs = b_specs if isinstance(b_specs, (list, tuple)) else [b_specs]
    n_b = len(b_specs)

    def body(a_ref, *rest):
        b_refs = rest[:n_b]
        o_ref, acc = rest[n_b:]
        k = pl.program_id(2)

        @pl.when(k == 0)
        def _():
            acc[...] = jnp.zeros_like(acc)

        av = a_ref[...]
        if square_a:
            af = av.astype(F32)
            av = (af * af).astype(BF16)
        bv = b_refs[0][...] if n_b == 1 else jnp.concatenate([r[...] for r in b_refs], axis=1)
        acc[...] += _dot_tn(av, bv)

        @pl.when(k == n_k - 1)
        def _():
            o_ref[...] = acc[...].astype(BF16)

    res = _call(body, name=name, grid=(m // bm, n // bn, n_k), in_specs=[a_spec, *b_specs],
                out_specs=[pl.BlockSpec((bm, bn), lambda i, j, k: (i, j))], out_shape=[jax.ShapeDtypeStruct((m, n), BF16)],
                scratch_shapes=[pltpu.VMEM((bm, bn), F32)], args=(a, *([b] * n_b)), ride=ride)
    return res[0] if ride is None else (res[0][0], res[1])


def _mlp_fwd(h, g, w1, w2, name, tm=1024, tf=1024, ride=None):
    T = h.shape[0]

    def body(h_ref, g_ref, w1_ref, w2_ref, ho_ref, n_ref, u_ref):
        @pl.when(pl.program_id(1) == 0)
        def _():
            n_ref[...] = _rms(h_ref[...], g_ref[...]).astype(BF16)
            ho_ref[...] = h_ref[...]

        u = jnp.maximum(_dot(n_ref[...], w1_ref[...]), 0.0)
        u_ref[...] = u.astype(BF16)
        ho_ref[...] += _dot((u * u).astype(BF16), w2_ref[...])

    row = pl.BlockSpec((tm, D_MODEL), lambda i, j: (i, 0))
    return _call(body, name=name, grid=(T // tm, D_FF // tf),
                 in_specs=[row, pl.BlockSpec((1, D_MODEL), lambda i, j: (0, 0)),
                           pl.BlockSpec((D_MODEL, tf), lambda i, j: (0, j)), pl.BlockSpec((tf, D_MODEL), lambda i, j: (j, 0))],
                 out_specs=[row, row, pl.BlockSpec((tm, tf), lambda i, j: (i, j))],
                 out_shape=[jax.ShapeDtypeStruct((T, D_MODEL), F32), jax.ShapeDtypeStruct((T, D_MODEL), BF16),
                            jax.ShapeDtypeStruct((T, D_FF), BF16)],
                 args=(h, g, w1, w2), ride=ride)


def _mlp_bwd(dh, u, w1, w2, h, g, name, tm=1024, tf=1024, ride=None):
    T = h.shape[0]
    n_j = D_FF // tf

    def body(dh_ref, u_ref, w1_ref, w2_ref, h_ref, g_ref, dx_ref, dxb_ref, du_ref, dg_ref, dhb):
        acc = dx_ref
        i, j = pl.program_id(0), pl.program_id(1)

        @pl.when(j == 0)
        def _():
            dhb[...] = dh_ref[...].astype(BF16)
            acc[...] = jnp.zeros_like(acc)

        @pl.when((i == 0) & (j == 0))
        def _():
            dg_ref[...] = jnp.zeros_like(dg_ref)

        da = _dot_nt(dhb[...], w2_ref[...])
        du = (da * (2.0 * u_ref[...].astype(F32))).astype(BF16)
        du_ref[...] = du
        acc[...] += _dot_nt(du, w1_ref[...])

        @pl.when(j == n_j - 1)
        def _():
            _rms_bwd_epilogue(h_ref, g_ref, dh_ref, dx_ref, dxb_ref, dg_ref)

    row = pl.BlockSpec((tm, D_MODEL), lambda i, j: (i, 0))
    vec = pl.BlockSpec((1, D_MODEL), lambda i, j: (0, 0))
    hid = pl.BlockSpec((tm, tf), lambda i, j: (i, j))
    return _call(body, name=name, grid=(T // tm, n_j),
                 in_specs=[row, hid, pl.BlockSpec((D_MODEL, tf), lambda i, j: (0, j)),
                           pl.BlockSpec((tf, D_MODEL), lambda i, j: (j, 0)), row, vec],
                 out_specs=[row, row, hid, vec],
                 out_shape=[jax.ShapeDtypeStruct((T, D_MODEL), F32), jax.ShapeDtypeStruct((T, D_MODEL), BF16),
                            jax.ShapeDtypeStruct((T, D_FF), BF16), jax.ShapeDtypeStruct((1, D_MODEL), F32)],
                 scratch_shapes=[pltpu.VMEM((tm, D_MODEL), BF16)], args=(dh, u, w1, w2, h, g), ride=ride)


def _final_loss(h, g, target, name, tm=512):
    T = h.shape[0]

    def body(h_ref, g_ref, t_ref, dh_ref, dhb_ref, loss_ref, dg_ref):
        @pl.when(pl.program_id(0) == 0)
        def _():
            loss_ref[...] = jnp.zeros_like(loss_ref)
            dg_ref[...] = jnp.zeros_like(dg_ref)

        x = h_ref[...]
        gg = g_ref[...]
        r = lax.rsqrt(jnp.mean(x * x, axis=-1, keepdims=True) + EPS)
        xh = x * r
        e = xh * gg - t_ref[...]
        per_tok = jnp.mean(e * e, axis=-1, keepdims=True)
        loss_ref[...] += 0.5 * jnp.sum(per_tok, axis=0, keepdims=True)
        dy = e * (1.0 / D_MODEL)
        dg_ref[...] += jnp.sum(dy * xh, axis=0, keepdims=True)
        dxh = dy * gg
        dx = r * (dxh - xh * jnp.mean(dxh * xh, axis=-1, keepdims=True))
        dh_ref[...] = dx
        dhb_ref[...] = dx.astype(BF16)

    row = pl.BlockSpec((tm, D_MODEL), lambda i: (i, 0))
    vec = pl.BlockSpec((1, D_MODEL), lambda i: (0, 0))
    return _call(body, name=name, grid=(T // tm,), in_specs=[row, vec, row],
                 out_specs=[row, row, pl.BlockSpec((8, LANES), lambda i: (0, 0)), vec],
                 out_shape=[jax.ShapeDtypeStruct((T, D_MODEL), F32), jax.ShapeDtypeStruct((T, D_MODEL), BF16),
                            jax.ShapeDtypeStruct((8, LANES), F32), jax.ShapeDtypeStruct((1, D_MODEL), F32)],
                 args=(h, g, target))


def _ret_constants(S):
    log_gamma = jnp.log1p(-jnp.exp2(-5.0 - jnp.arange(RET_HEADS, dtype=F32)))
    idx = jnp.arange(RET_SC, dtype=F32)
    i, j = idx[:, None], idx[None, :]
    same_chunk = jnp.floor(i / CHUNK) == jnp.floor(j / CHUNK)
    mask = jnp.where((j <= i) | same_chunk, jnp.exp(log_gamma[:, None, None] * jnp.abs(i - j)[None]), 0.0)
    qdec = jnp.exp(log_gamma[:, None] * (idx + 1.0)[None, :])[:, :, None]
    kdec = jnp.exp(log_gamma[:, None] * (RET_SC - 1 - idx)[None, :])[:, :, None]
    cdec = jnp.exp(log_gamma * RET_SC)[:, None, None]
    half = RET_DK // 2
    inv = jnp.exp(-jnp.log(ROPE_BASE) * jnp.arange(half, dtype=F32) / half)
    ang = jnp.arange(S, dtype=F32)[:, None] * inv[None, :]
    return jnp.cos(ang), jnp.sin(ang), mask.astype(F32), qdec, kdec, cdec


def _rope(t, cs, sn):
    t1, t2 = t[:, :RET_DK // 2], t[:, RET_DK // 2:]
    return jnp.concatenate([t1 * cs - t2 * sn, t1 * sn + t2 * cs], axis=-1)


def _rope_bwd(d, cs, sn):
    d1, d2 = d[:, :RET_DK // 2], d[:, RET_DK // 2:]
    return jnp.concatenate([d1 * cs + d2 * sn, d2 * cs - d1 * sn], axis=-1)


def _ret_specs(B, S, reverse):
    n_sc = S // RET_SC

    def cc(c):
        return n_sc - 1 - c if reverse else c

    return dict(
        proj=pl.BlockSpec((B, RET_SC, RET_HEAD_COLS), lambda h, c: (0, cc(c), h)),
        trig=pl.BlockSpec((RET_SC, RET_DK // 2), lambda h, c: (cc(c), 0)),
        mask=pl.BlockSpec((None, RET_SC, RET_SC), lambda h, c: (h, 0, 0)),
        dec=pl.BlockSpec((None, RET_SC, 1), lambda h, c: (h, 0, 0)),
        cdec=pl.BlockSpec((None, 1, 1), lambda h, c: (h, 0, 0)),
        gn=pl.BlockSpec((1, RET_DV), lambda h, c: (0, h)),
        val=pl.BlockSpec((B, RET_SC, RET_DV), lambda h, c: (0, cc(c), h)),
        state=pl.BlockSpec((B, None, None, RET_DK, RET_DV), lambda h, c: (0, h, cc(c), 0, 0)),
    )


def _ret_qkvg(p_ref, kdec):
    qb = p_ref[:, 0:RET_DK]
    kb = p_ref[:, RET_DK:2 * RET_DK]
    kt = (kb.astype(F32) * kdec).astype(BF16)
    v = p_ref[:, 2 * RET_DK:2 * RET_DK + RET_DV]
    gate = p_ref[:, 2 * RET_DK + RET_DV:RET_HEAD_COLS].astype(F32)
    return qb, kb, kt, v, gate


def _group_norm(o):
    mu = jnp.mean(o, axis=-1, keepdims=True)
    oc = o - mu
    rstd = lax.rsqrt(jnp.mean(oc * oc, axis=-1, keepdims=True) + EPS)
    return oc * rstd, rstd


def _ret_fwd(proj, consts, gn, B, S, name, ride=None):
    T = B * S
    n_sc = S // RET_SC
    sp = _ret_specs(B, S, False)

    def body(p_ref, m_ref, qd_ref, kd_ref, cd_ref, gn_ref, y_ref, o_ref, st_ref, state):
        @pl.when(pl.program_id(1) == 0)
        def _():
            state[...] = jnp.zeros_like(state)

        seqs = range(B)
        qkvg = [_ret_qkvg(p_ref.at[b], kd_ref[...]) for b in seqs]
        qb, kb, kt, v = ([qkvg[b][i] for b in seqs] for i in range(4))
        sc = [_dot_nt(qb[b], kb[b]) for b in seqs]
        sb = [state[b].astype(BF16) for b in seqs]
        cross = [_dot(qb[b], sb[b]) for b in seqs]
        for b in seqs:
            st_ref[b] = sb[b]
        p = [(sc[b] * m_ref[...]).astype(BF16) for b in seqs]
        o = [_dot(p[b], v[b]) + cross[b] * qd_ref[...] for b in seqs]
        upd = [_dot_tn(kt[b], v[b]) for b in seqs]
        for b in seqs:
            o_ref[b] = o[b].astype(BF16)
            ohat, _ = _group_norm(o[b])
            gate = qkvg[b][4]
            y_ref[b] =(gate * jax.nn.sigmoid(gate) * (ohat * gn_ref[...])).astype(BF16)
            state[b] = state[b] * cd_ref[...] + upd[b]

    res = _call(
        body, name=name, grid=(RET_HEADS, n_sc),
        in_specs=[sp["proj"], sp["mask"], sp["dec"], sp["dec"], sp["cdec"], sp["gn"]],
        out_specs=[sp["val"], sp["val"], sp["state"]],
        out_shape=[jax.ShapeDtypeStruct((B, S, RET_VW), BF16), jax.ShapeDtypeStruct((B, S, RET_VW), BF16),
                   jax.ShapeDtypeStruct((B, RET_HEADS, n_sc, RET_DK, RET_DV), BF16)],
        scratch_shapes=[pltpu.VMEM((B, RET_DK, RET_DV), F32)],
        args=(proj.reshape(B, S, -1), *consts[2:], gn), ride=ride)
    (y, o, states), lands = res if ride is not None else (res, None)
    outs = [y.reshape(T, RET_VW), o.reshape(T, RET_VW), states]
    return outs if ride is None else (outs, lands)


def _ret_bwd(proj, consts, gn, o, states, dy, B, S, name, ride=None):
    T = B * S
    n_sc = S // RET_SC
    sp = _ret_specs(B, S, True)

    def body(p_ref, cos_ref, sin_ref, m_ref, qd_ref, kd_ref, cd_ref, gn_ref, o_ref, st_ref, dy_ref,
             dp_ref, dgn_ref, dstate):
        @pl.when(pl.program_id(1) == 0)
        def _():
            dstate[...] = jnp.zeros_like(dstate)
            dgn_ref[...] = jnp.zeros_like(dgn_ref)

        seqs = range(B)
        cs, sn = cos_ref[...], sin_ref[...]
        m, gnv = m_ref[...], gn_ref[...]
        qkvg = [_ret_qkvg(p_ref.at[b], kd_ref[...]) for b in seqs]
        qb, kb, kt, v = ([qkvg[b][i] for b in seqs] for i in range(4))
        sc = [_dot_nt(qb[b], kb[b]) for b in seqs]
        dsb = [dstate[b].astype(BF16) for b in seqs]
        dv_state = [_dot(kt[b], dsb[b]) for b in seqs]
        dkt = [_dot_nt(v[b], dsb[b]) for b in seqs]
        do, dgate = [], []
        for b in seqs:
            gate = qkvg[b][4]
            ohat, rstd = _group_norm(o_ref[b].astype(F32))
            dyv = dy_ref[b].astype(F32)
            sg = jax.nn.sigmoid(gate)
            don = dyv * (gate * sg)
            dgate.append(dyv * (ohat * gnv) * (sg * (1.0 + gate * (1.0 - sg))))
            dgn_ref[...] += jnp.sum(don * ohat, axis=0, keepdims=True)
            dohat = don * gnv
            do.append(rstd * (dohat - jnp.mean(dohat, axis=-1, keepdims=True)
                              - ohat * jnp.mean(dohat * ohat, axis=-1, keepdims=True)))
        dob = [do[b].astype(BF16) for b in seqs]
        doq = [(do[b] * qd_ref[...]).astype(BF16) for b in seqs]
        dsc_f = [_dot_nt(dob[b], v[b]) for b in seqs]
        dq_state = [_dot_nt(doq[b], st_ref[b]) for b in seqs]
        dstate_upd = [_dot_tn(qb[b], doq[b]) for b in seqs]
        p = [(sc[b] * m).astype(BF16) for b in seqs]
        dsc = [(dsc_f[b] * m).astype(BF16) for b in seqs]
        dv = [_dot_tn(p[b], dob[b]) + dv_state[b] for b in seqs]
        dq = [_dot(dsc[b], kb[b]) + dq_state[b] for b in seqs]
        dk = [(_dot_tn(dsc[b], qb[b]) + dkt[b] * kd_ref[...]) * (RET_DK ** -0.5) for b in seqs]
        for b in seqs:
            dstate[b] = dstate[b] * cd_ref[...] + dstate_upd[b]
            dp_ref[b, :, 0:RET_DK] = _rope_bwd(dq[b], cs, sn).astype(BF16)
            dp_ref[b, :, RET_DK:2 * RET_DK] = _rope_bwd(dk[b], cs, sn).astype(BF16)
            dp_ref[b, :, 2 * RET_DK:2 * RET_DK + RET_DV] = dv[b].astype(BF16)
            dp_ref[b, :, 2 * RET_DK + RET_DV:RET_HEAD_COLS] = dgate[b].astype(BF16)

    res = _call(
        body, name=name, grid=(RET_HEADS, n_sc),
        in_specs=[sp["proj"], sp["trig"], sp["trig"], sp["mask"], sp["dec"], sp["dec"], sp["cdec"], sp["gn"],
                  sp["val"], sp["state"], sp["val"]],
        out_specs=[sp["proj"], sp["gn"]],
        out_shape=[jax.ShapeDtypeStruct((B, S, RET_HEADS * RET_HEAD_COLS), BF16), jax.ShapeDtypeStruct((1, RET_VW), F32)],
        scratch_shapes=[pltpu.VMEM((B, RET_DK, RET_DV), F32)],
        args=(proj.reshape(B, S, -1), *consts, gn, o.reshape(B, S, -1), states, dy.reshape(B, S, -1)), ride=ride)
    (dproj, dgn), lands = res if ride is not None else (res, None)
    outs = [dproj.reshape(T, -1), dgn]
    return outs if ride is None else (outs, lands)


def _ring_index():
    u = np.arange(ATT_RING)
    offset = np.where(u < ATT_KW, u, u - ATT_RING)
    return np.clip(offset - BAND_PAST, -MAX_REL, CHUNK - 1) + MAX_REL


def _bias_ring(rel):
    n_clip = BAND_PAST - MAX_REL
    n_hi = ATT_KW - n_clip - REL_TABLE
    ring = jnp.concatenate([jnp.broadcast_to(rel[:, :1], (ATT_HEADS, n_clip)), rel,
                            jnp.broadcast_to(rel[:, -1:], (ATT_HEADS, n_hi)),
                            jnp.broadcast_to(rel[:, :1], (ATT_HEADS, ATT_CQ))], axis=1)
    return jnp.broadcast_to(ring.reshape(ATT_PAIRS, 2, 1, ATT_RING), (ATT_PAIRS, 2, 8, ATT_RING))


def _band_masks():
    i = np.arange(ATT_CQ)[:, None]
    j = np.arange(ATT_KW)[None, :]
    lo = CHUNK * (i // CHUNK)
    band = np.where((j >= lo) & (j < lo + BAND_PAST + CHUNK), 0.0, NEG)
    return np.stack([band + np.where(j + v * ATT_CQ >= BAND_PAST, 0.0, NEG)
                     for v in range(ATT_VARIANTS)]).astype(np.float32)


def _attn_bias(bias_scr, ring_ref, band_ref):
    for hh in range(2):
        rows = jnp.broadcast_to(ring_ref[hh, 0:1, :], (ATT_CQ, ATT_RING))
        toeplitz = pltpu.roll(rows, 0, 1, stride=1, stride_axis=0)[:, :ATT_KW]
        for v in range(ATT_VARIANTS):
            bias_scr[v, hh] = toeplitz * LOG2E + band_ref[v]


ATT_STRIP = 32
ATT_SCALE = ATT_DH ** -0.5
LOG2E = 1.4426950408889634


def _strips(fn):
    def strip(r, carry):
        fn(pl.ds(pl.multiple_of(r * ATT_STRIP, ATT_STRIP), ATT_STRIP))
        return carry

    lax.fori_loop(0, ATT_CQ // ATT_STRIP, strip, 0, unroll=True)


def _attn_prepare(kpad, vpad, qkv_ref):
    kpad[0:BAND_PAST, :] = jnp.zeros((BAND_PAST, LANES), BF16)
    vpad[0:BAND_PAST, :] = jnp.zeros((BAND_PAST, LANES), BF16)
    kpad[BAND_PAST:, :] = qkv_ref[1]
    vpad[BAND_PAST:, :] = qkv_ref[2]


def _attn_queries(qkv_ref, qs):
    lane = lax.broadcasted_iota(jnp.int32, (1, LANES), 1)
    q = (qkv_ref[0, pl.ds(qs, ATT_CQ), :].astype(F32) * (ATT_SCALE * LOG2E)).astype(BF16)
    return [jnp.where((lane >= ATT_DH * hh) & (lane < ATT_DH * (hh + 1)), q, jnp.zeros_like(q)) for hh in range(2)]


def _attn_scores(s_ref, qkv_ref, kpad, b_ref, t, n_qb):
    t = jnp.minimum(t, n_qb - 1)
    qs = pl.multiple_of(t * ATT_CQ, ATT_CQ)
    variant = jnp.minimum(t, ATT_VARIANTS - 1)
    kw = kpad[pl.ds(qs, ATT_KW), :]
    qm = _attn_queries(qkv_ref, qs)
    for hh in range(2):
        s_ref[hh] = _dot_nt(qm[hh], kw) + b_ref[variant, hh]


def _attn_softmax(s_ref, e_ref, linv_ref, m_ref=None):
    def strip(rows):
        s = s_ref[rows, :]
        m = jnp.max(s, axis=-1, keepdims=True)
        e = jnp.exp2(s - m)
        e_ref[rows, :] = e.astype(BF16)
        linv_ref[rows, :] = jnp.broadcast_to(1.0 / jnp.sum(e, axis=-1, keepdims=True), (ATT_STRIP, LANES))
        if m_ref is not None:
            m_ref[rows, :] = jnp.broadcast_to(m, (ATT_STRIP, LANES))

    _strips(strip)


_RING_SPEC = pl.BlockSpec((None, 2, 8, ATT_RING), lambda hp, b: (hp, 0, 0, 0))
_BAND_SPEC = pl.BlockSpec((ATT_VARIANTS, ATT_CQ, ATT_KW), lambda hp, b: (0, 0, 0))


def _attn_fwd(qkv3, ring, B, S, name, ride=None):
    T = B * S
    n_qb = S // ATT_CQ

    def body(qkv_ref, ring_ref, band_ref, o_ref, kpad, vpad, b_ref, s_scr, e_scr, linv_scr):
        @pl.when(pl.program_id(1) == 0)
        def _():
            _attn_bias(b_ref, ring_ref, band_ref)

        _attn_prepare(kpad, vpad, qkv_ref)
        e_scr[...] = jnp.zeros_like(e_scr)
        linv_scr[...] = jnp.zeros_like(linv_scr)
        lane = lax.broadcasted_iota(jnp.int32, (1, LANES), 1)

        def softmax(slot):
            for hh in range(2):
                _attn_softmax(s_scr.at[slot, hh], e_scr.at[slot, hh], linv_scr.at[slot, hh])

        def output(t, slot):
            qs = pl.multiple_of(jnp.maximum(t, 0) * ATT_CQ, ATT_CQ)
            vw = vpad[pl.ds(qs, ATT_KW), :]
            outs = [_dot(e_scr[slot, hh], vw) * linv_scr[slot, hh] for hh in range(2)]
            o_ref[pl.ds(qs, ATT_CQ), :] = jnp.where(lane < ATT_DH, outs[0], outs[1]).astype(BF16)

        def pair(u, carry):
            t = 2 * u
            _attn_scores(s_scr.at[1], qkv_ref, kpad, b_ref, t + 1, n_qb)
            softmax(0)
            output(t - 1, 1)
            _attn_scores(s_scr.at[0], qkv_ref, kpad, b_ref, t + 2, n_qb)
            softmax(1)
            output(t, 0)
            return carry

        _attn_scores(s_scr.at[0], qkv_ref, kpad, b_ref, 0, n_qb)
        lax.fori_loop(0, n_qb // 2, pair, 0)
        output(n_qb - 1, 1)

    return _call(body, name=name, grid=(ATT_PAIRS, B),
                 in_specs=[pl.BlockSpec((3, S, LANES), lambda hp, b: (0, b, hp)), _RING_SPEC, _BAND_SPEC],
                 out_specs=[pl.BlockSpec((S, LANES), lambda hp, b: (b, hp))],
                 out_shape=[jax.ShapeDtypeStruct((T, D_MODEL), BF16)],
                 scratch_shapes=[pltpu.VMEM((S + BAND_PAST, LANES), BF16), pltpu.VMEM((S + BAND_PAST, LANES), BF16),
                                 pltpu.VMEM((ATT_VARIANTS, 2, ATT_CQ, ATT_KW), F32),
                                 pltpu.VMEM((2, 2, ATT_CQ, ATT_KW), F32),
                                 pltpu.VMEM((2, 2, ATT_CQ, ATT_KW), BF16), pltpu.VMEM((2, 2, ATT_CQ, LANES), F32)],
                 args=(qkv3, ring, jnp.asarray(_band_masks())), ride=ride)


def _attn_bwd(qkv3, ring, do, B, S, name, ride=None):
    T = B * S
    n_qb = S // ATT_CQ

    def body(qkv_ref, ring_ref, band_ref, do_ref, dqkv_ref, dring_ref, kpad, vpad, dkacc, dvacc, b_ref, db_ref):
        @pl.when(pl.program_id(1) == 0)
        def _():
            _attn_bias(b_ref, ring_ref, band_ref)
            db_ref[...] = jnp.zeros_like(db_ref)

        _attn_prepare(kpad, vpad, qkv_ref)
        dkacc[...] = jnp.zeros_like(dkacc)
        dvacc[...] = jnp.zeros_like(dvacc)
        lane = lax.broadcasted_iota(jnp.int32, (1, LANES), 1)

        def step(qb, carry):
            qs = pl.multiple_of(qb * ATT_CQ, ATT_CQ)
            variant = jnp.minimum(qb, ATT_VARIANTS - 1)
            dov = do_ref[pl.ds(qs, ATT_CQ), :].astype(F32)
            kw = kpad[pl.ds(qs, ATT_KW), :]
            vw = vpad[pl.ds(qs, ATT_KW), :]
            heads = (0, 1)
            qmh = _attn_queries(qkv_ref, qs)
            s = [_dot_nt(qmh[hh], kw) + b_ref[variant, hh] for hh in heads]
            e, linv, dom, dp = [None, None], [None, None], [None, None], [None, None]
            for hh in heads:
                e[hh] = jnp.exp2(s[hh] - jnp.max(s[hh], axis=-1, keepdims=True))
                linv[hh] = 1.0 / jnp.sum(e[hh], axis=-1, keepdims=True)
                sel = (lane >= ATT_DH * hh) & (lane < ATT_DH * (hh + 1))
                dom[hh] = jnp.where(sel, dov * linv[hh], 0.0).astype(BF16)
                dp[hh] = _dot_nt(dom[hh], vw)
            dqs, dk, dv = [], None, None
            for hh in heads:
                ds = e[hh] * (dp[hh] - jnp.sum(dp[hh] * e[hh], axis=-1, keepdims=True) * linv[hh])
                db_ref[hh] += ds
                dsb = ds.astype(BF16)
                dqs.append(_dot(dsb, kw) * ATT_SCALE)
                dk_h = _dot_tn(qmh[hh], dsb) * (1.0 / LOG2E)
                dv_h = _dot_tn(dom[hh], e[hh].astype(BF16))
                dk = dk_h if dk is None else dk + dk_h
                dv = dv_h if dv is None else dv + dv_h
            dqkv_ref[0, pl.ds(qs, ATT_CQ), :] = jnp.where(lane < ATT_DH, dqs[0], dqs[1]).astype(BF16)
            dkacc[:, pl.ds(qs, ATT_KW)] += dk
            dvacc[:, pl.ds(qs, ATT_KW)] += dv
            return carry

        lax.fori_loop(0, n_qb, step, 0, unroll=4)
        dqkv_ref[1] = dkacc[:, BAND_PAST:].T.astype(BF16)
        dqkv_ref[2] = dvacc[:, BAND_PAST:].T.astype(BF16)

        @pl.when(pl.program_id(1) == B - 1)
        def _():
            r = lax.broadcasted_iota(jnp.int32, (ATT_CQ, ATT_CQ), 0)
            c = lax.broadcasted_iota(jnp.int32, (ATT_CQ, ATT_CQ), 1)
            reverse = jnp.where(r + c == ATT_CQ - 1, 1.0, 0.0).astype(BF16)
            for hh in range(2):
                x = jnp.concatenate([db_ref[hh], jnp.zeros((ATT_CQ, ATT_RING - ATT_KW), F32)], axis=1)
                flipped = jnp.zeros((ATT_CQ, ATT_RING), F32)
                for _ in range(3):
                    part = x.astype(BF16)
                    flipped = flipped + _dot(reverse, part)
                    x = x - part.astype(F32)
                aligned = pltpu.roll(flipped, ATT_KW + 1, 1, stride=1, stride_axis=0)
                dring_ref[hh] = jnp.sum(aligned.reshape(ATT_CQ // 8, 8, ATT_RING), axis=0)

    qkv_spec = pl.BlockSpec((3, S, LANES), lambda hp, b: (0, b, hp))
    return _call(body, name=name, grid=(ATT_PAIRS, B),
                 in_specs=[qkv_spec, _RING_SPEC, _BAND_SPEC, pl.BlockSpec((S, LANES), lambda hp, b: (b, hp))],
                 out_specs=[qkv_spec, _RING_SPEC],
                 out_shape=[jax.ShapeDtypeStruct((3, T, D_MODEL), BF16),
                            jax.ShapeDtypeStruct((ATT_PAIRS, 2, 8, ATT_RING), F32)],
                 scratch_shapes=[pltpu.VMEM((S + BAND_PAST, LANES), BF16), pltpu.VMEM((S + BAND_PAST, LANES), BF16),
                                 pltpu.VMEM((LANES, S + BAND_PAST), F32), pltpu.VMEM((LANES, S + BAND_PAST), F32),
                                 pltpu.VMEM((ATT_VARIANTS, 2, ATT_CQ, ATT_KW), F32), pltpu.VMEM((2, ATT_CQ, ATT_KW), F32)],
                 args=(qkv3, ring, jnp.asarray(_band_masks()), do), ride=ride)


def _rel_bias_grad(dring, name):
    fold = np.zeros((ATT_RING, REL_PAD), np.float32)
    fold[np.arange(ATT_RING), _ring_index()] = 1.0
    fold = jnp.asarray(fold, BF16)

    def body(d_ref, f_ref, o_ref):
        x = jnp.sum(d_ref[...], axis=0, keepdims=True)
        x = jnp.broadcast_to(x, (8, ATT_RING))
        acc = jnp.zeros((8, REL_PAD), F32)
        for _ in range(3):
            part = x.astype(BF16)
            acc = acc + _dot(part, f_ref[...])
            x = x - part.astype(F32)
        o_ref[...] = acc[0:1, :]

    out = _call(body, name=name, grid=(ATT_HEADS,),
                in_specs=[pl.BlockSpec((None, 8, ATT_RING), lambda h: (h, 0, 0)),
                          pl.BlockSpec((ATT_RING, REL_PAD), lambda h: (0, 0))],
                out_specs=[pl.BlockSpec((None, 1, REL_PAD), lambda h: (h, 0, 0))],
                out_shape=[jax.ShapeDtypeStruct((ATT_HEADS, 1, REL_PAD), F32)],
                args=(dring.reshape(ATT_HEADS, 8, ATT_RING), fold))[0]
    return out.reshape(ATT_HEADS, REL_PAD)


def _adamw(w, g, m, v):
    m = ADAM_B1 * m + (1.0 - ADAM_B1) * g
    v = ADAM_B2 * v + (1.0 - ADAM_B2) * (g * g)
    m_hat = m / (1.0 - ADAM_B1 ** ADAM_STEP)
    v_hat = v / (1.0 - ADAM_B2 ** ADAM_STEP)
    delta = -ADAM_LR * (m_hat / (jnp.sqrt(v_hat) + ADAM_EPS) + ADAM_WD * w)
    return delta, m, v


def _sum_devices(ref):
    g = ref[0].astype(F32)
    for d in range(1, N_DEV):
        g = g + ref[d].astype(F32)
    return g


def _adamw_reduce(lands, w, m, v, name, tr=256):
    L, R, C = w.shape
    tr = min(tr, R)
    n_i = R // tr

    def body(*refs):
        l_refs = refs[:L]
        w_ref, m_ref, v_ref, g_out, d_out, m_out, v_out = refs[L:]
        layer = pl.program_id(0)
        for l in range(L):
            @pl.when(layer == l)
            def _(l=l):
                g = _sum_devices(l_refs[l])
                g_out[...] = g
                d_out[...], m_out[...], v_out[...] = _adamw(w_ref[...], g, m_ref[...], v_ref[...])

    def land_spec(l):
        return pl.BlockSpec((N_DEV, tr, C), lambda ly, i: (0, jnp.where(ly == l, i, jnp.where(ly < l, 0, n_i - 1)), 0))

    blk = pl.BlockSpec((None, tr, C), lambda ly, i: (ly, i, 0))
    return _call(body, name=name, grid=(L, n_i), in_specs=[land_spec(l) for l in range(L)] + [blk, blk, blk],
                 out_specs=[blk] * 4, out_shape=[jax.ShapeDtypeStruct((L, R, C), F32)] * 4, args=(*lands, w, m, v))


def _small_update(land_small, land_rel, w, m, v, name):
    def body(ls_ref, lr_ref, w_ref, m_ref, v_ref, g_out, d_out, m_out, v_out, rel_out):
        g = _sum_devices(ls_ref)
        g_out[...] = g
        d_out[...], m_out[...], v_out[...] = _adamw(w_ref[...], g, m_ref[...], v_ref[...])
        rel_out[...] = _sum_devices(lr_ref)

    return pl.pallas_call(
        body, name=name,
        out_shape=[jax.ShapeDtypeStruct(w.shape, F32)] * 4 + [jax.ShapeDtypeStruct(land_rel.shape[1:], F32)],
    )(land_small, land_rel, w, m, v)


def _adamw_plain(w, g, m, v, name):
    def body(w_ref, g_ref, m_ref, v_ref, d_out, m_out, v_out):
        d_out[...], m_out[...], v_out[...] = _adamw(w_ref[...], g_ref[...], m_ref[...], v_ref[...])

    return pl.pallas_call(body, name=name, out_shape=[jax.ShapeDtypeStruct(w.shape, F32)] * 3)(w, g, m, v)


def _ret_piece_of_column_block(p):
    per_head = RET_HEAD_COLS // RET_DK
    qk = jnp.where(p < RET_HEADS, per_head * p, per_head * (p - RET_HEADS) + 1)
    pv = p - 2 * RET_HEADS
    vv = per_head * (pv // 2) + 2 + pv % 2
    pg = p - 4 * RET_HEADS
    gg = per_head * (pg // 2) + 4 + pg % 2
    return jnp.where(p < 2 * RET_HEADS, qk, jnp.where(p < 4 * RET_HEADS, vv, gg))


def _step(x, target, shards, mix_g, gn_g, rel_shard, mlp_g, final_g):
    s_rin, s_rout, s_ain, s_aout, s_w1, s_w2 = shards
    B, S, _ = x.shape
    T = B * S
    tk = min(T, 4096)
    n_k = T // tk
    h0 = x.reshape(T, D_MODEL)
    tgt = target.reshape(T, D_MODEL)
    consts = _ret_constants(S)
    w1_cols = w2_rows = D_FF // N_DEV

    rel_cols = REL_TABLE // N_DEV
    per_shard = RET_IN // N_DEV // RET_DK

    def head_layout(r):
        return lambda ref, blk: ref.at[:, pl.ds(pl.multiple_of(
            _ret_piece_of_column_block(per_shard * blk + r) * RET_DK, LANES), RET_DK)]

    w_rin_heads, rel_all = _ChipGather(
        [s_rin[:, r * RET_DK:(r + 1) * RET_DK] for r in range(per_shard)] + [rel_shard],
        [jax.ShapeDtypeStruct((D_MODEL, RET_IN), BF16), jax.ShapeDtypeStruct((N_DEV, ATT_HEADS, rel_cols), F32)],
        [0] * per_shard + [1], [head_layout(r) for r in range(per_shard)] + [_slot]).run("gather_first")
    ring = _bias_ring(rel_all.transpose(1, 0, 2).reshape(ATT_HEADS, REL_TABLE))

    def gather(*items):
        return _ChipGather([s for s, _, _ in items], [jax.ShapeDtypeStruct(full, BF16) for _, _, full in items],
                           list(range(len(items))), [view for _, view, _ in items])

    def scatter(*items):
        return _Exchange([p for p, _, _ in items], [(a, view, a, _slot) for a, (_, view, _) in enumerate(items)],
                         [jax.ShapeDtypeStruct((N_DEV, *shard), BF16) for _, _, shard in items])

    w1_full, w2_full = (D_MODEL, D_FF), (D_FF, D_MODEL)
    w1_shard, w2_shard = (D_MODEL, w1_cols), (w2_rows, D_MODEL)

    ride = gather((s_rout, _rows(RET_VW // N_DEV), (RET_VW, D_MODEL)), (s_w1[0], _cols(w1_cols), w1_full))
    (proj, n0), (w_rout, w1_0) = _ret_proj(h0, mix_g[0:1], w_rin_heads, consts[0], consts[1], S, "ret_proj", 512, ride=ride)
    (y, o, states), (w2_0,) = _ret_fwd(proj, consts, gn_g, B, S, "ret_fwd", ride=gather((s_w2[0], _rows(w2_rows), w2_full)))
    h1 = _matmul_res(y, w_rout, h0, "ret_out")
    ride = gather((s_ain, _cols(3 * D_MODEL // N_DEV), (D_MODEL, 3 * D_MODEL)),
                  (s_aout, _rows(D_MODEL // N_DEV), (D_MODEL, D_MODEL)))
    (h2, n1, u1), (w_ain, w_aout) = _mlp_fwd(h1, mlp_g[0:1], w1_0, w2_0, "mlp0_fwd", ride=ride)
    qkv3, n2 = _norm_proj(h2, mix_g[1:2], w_ain, 3, "att_proj", 1024)
    ride = gather((s_w1[1], _cols(w1_cols), w1_full), (s_w2[1], _rows(w2_rows), w2_full))
    (o2,), (w1_1, w2_1) = _attn_fwd(qkv3, ring, B, S, "att_fwd", ride=ride)
    h3 = _matmul_res(o2, w_aout, h2, "att_out")
    h4, n3, u3 = _mlp_fwd(h3, mlp_g[1:2], w1_1, w2_1, "mlp1_fwd")
    dh4, dh4b, loss, dg_final = _final_loss(h4, final_g, tgt, "final_loss")

    def tok(width):
        return dict(a=pl.BlockSpec((tk, width), lambda i, j, k: (k, i)), b=pl.BlockSpec((tk, width), lambda i, j, k: (k, j)))

    def mlp_dw2(u, dhb, tag, ride=None):
        return _wgrad(u, tok(1024)["a"], dhb, tok(1024)["b"], D_FF, D_MODEL, 1024, 1024, n_k, tag + "_dw2",
                      square_a=True, ride=ride)

    def mlp_dw1(n, du, tag):
        return _wgrad(n, tok(1024)["a"], du, tok(1024)["b"], D_MODEL, D_FF, 1024, 1024, n_k, tag + "_dw1")

    dh3, dh3b, du3, dg_mlp1 = _mlp_bwd(dh4, u3, w1_1, w2_1, h3, mlp_g[1:2], "mlp1_bwd")
    gw2_1, gw1_1 = mlp_dw2(u3, dh4b, "mlp1"), mlp_dw1(n3, du3, "mlp1")
    do2 = _matmul_nt(dh3b, w_aout, "att_out_bwd")
    g_aout = _wgrad(o2, tok(1024)["a"], dh3b, tok(1024)["b"], D_MODEL, D_MODEL, 1024, 1024, n_k, "att_out_dw")
    ride = scatter((gw1_1, _cols(w1_cols), w1_shard), (gw2_1, _rows(w2_rows), w2_shard),
                   (g_aout, _rows(D_MODEL // N_DEV), (D_MODEL // N_DEV, D_MODEL)))
    (dqkv3, dring), (l_w1_1, l_w2_1, l_aout) = _attn_bwd(qkv3, ring, do2, B, S, "att_bwd", ride=ride)
    g_rel = _rel_bias_grad(dring, "rel_bias_grad")
    dh2, dh2b, dg_mix1 = _proj_bwd(dqkv3, w_ain, h2, mix_g[1:2], dh3, "att_proj_bwd", 1024)
    g_ain = _wgrad(n2, tok(1024)["a"], dqkv3, pl.BlockSpec((None, tk, D_MODEL), lambda i, j, k: (j, k, 0)),
                   D_MODEL, 3 * D_MODEL, 1024, 1024, n_k, "att_proj_dw")
    ride = scatter((g_ain, _cols(3 * D_MODEL // N_DEV), (D_MODEL, 3 * D_MODEL // N_DEV)))
    (dh1, dh1b, du1, dg_mlp0), (l_ain,) = _mlp_bwd(dh2, u1, w1_0, w2_0, h1, mlp_g[0:1], "mlp0_bwd", ride=ride)
    dy = _matmul_nt(dh1b, w_rout, "ret_out_bwd")
    g_rout = _wgrad(y, tok(1024)["a"], dh1b, tok(1024)["b"], RET_VW, D_MODEL, 1024, 1024, n_k, "ret_out_dw")
    gw2_0, (l_rout,) = mlp_dw2(u1, dh2b, "mlp0", ride=scatter((g_rout, _rows(RET_VW // N_DEV), (RET_VW // N_DEV, D_MODEL))))
    gw1_0 = mlp_dw1(n1, du1, "mlp0")
    ride = scatter((gw2_0, _rows(w2_rows), w2_shard), (gw1_0, _cols(w1_cols), w1_shard))
    (dproj, dgn), (l_w2_0, l_w1_0) = _ret_bwd(proj, consts, gn_g, o, states, dy, B, S, "ret_bwd", ride=ride)
    g_rin = _wgrad(n0, tok(1024)["a"], dproj,
                   [pl.BlockSpec((tk, RET_DK), functools.partial(
                       lambda i, j, k, r: (k, _ret_piece_of_column_block(per_shard * j + r)), r=r)) for r in range(per_shard)],
                   D_MODEL, RET_IN, 1024, per_shard * RET_DK, n_k, "ret_proj_dw")
    ride = scatter((g_rin, _cols(RET_IN // N_DEV), (D_MODEL, RET_IN // N_DEV)))
    (dx, _, dg_mix0), (l_rin,) = _proj_bwd(dproj, w_rin_heads, h0, mix_g[0:1], dh1, "ret_proj_bwd", 512, ride=ride)

    small = jnp.concatenate([dg_mix0, dg_mix1, dg_mlp0, dg_mlp1, dg_final, dgn.reshape(2, D_MODEL),
                             jnp.broadcast_to(loss[0:1, 0:1], (1, D_MODEL))], axis=0)
    l_small, l_rel = _Exchange(
        [small, g_rel], [(0, _whole, 0, _slot), (1, _whole, 1, _slot)],
        [jax.ShapeDtypeStruct((N_DEV, 8, D_MODEL), F32), jax.ShapeDtypeStruct((N_DEV, ATT_HEADS, REL_PAD), F32)],
    ).run("scatter_small")
    lands = dict(rin=l_rin, rout=l_rout, ain=l_ain, aout=l_aout, w1=(l_w1_0, l_w1_1), w2=(l_w2_0, l_w2_1),
                 small=l_small, rel=l_rel)
    return dx.reshape(B, S, D_MODEL), lands


def kernel(x, mix_norm_g, ret_w_in, ret_gn_g, ret_w_out, att_w_in, att_rel_bias, att_w_out, mlp_norm_g, mlp_w1, mlp_w2, final_norm_g, loss_target, m_mix_norm_g, m_ret_w_in, m_ret_gn_g, m_ret_w_out, m_att_w_in, m_att_rel_bias, m_att_w_out, m_mlp_norm_g, m_mlp_w1, m_mlp_w2, m_final_norm_g, v_mix_norm_g, v_ret_w_in, v_ret_gn_g, v_ret_w_out, v_att_w_in, v_att_rel_bias, v_att_w_out, v_mlp_norm_g, v_mlp_w1, v_mlp_w2, v_final_norm_g):
    me = _lin(_place())
    rel_cols = REL_TABLE // N_DEV
    shards = (ret_w_in[0].astype(BF16), ret_w_out[0].astype(BF16), att_w_in[0].astype(BF16), att_w_out[0].astype(BF16),
              (mlp_w1[0].astype(BF16), mlp_w1[1].astype(BF16)), (mlp_w2[0].astype(BF16), mlp_w2[1].astype(BF16)))
    grad_x, lands = _step(x, loss_target, shards, mix_norm_g, ret_gn_g, att_rel_bias[0], mlp_norm_g,
                          final_norm_g.reshape(1, D_MODEL))

    u_rin = _adamw_reduce([lands["rin"]], ret_w_in, m_ret_w_in, v_ret_w_in, "update_ret_w_in")
    u_rout = _adamw_reduce([lands["rout"]], ret_w_out, m_ret_w_out, v_ret_w_out, "update_ret_w_out")
    u_ain = _adamw_reduce([lands["ain"]], att_w_in, m_att_w_in, v_att_w_in, "update_att_w_in")
    u_aout = _adamw_reduce([lands["aout"]], att_w_out, m_att_w_out, v_att_w_out, "update_att_w_out")
    u_w1 = _adamw_reduce(lands["w1"], mlp_w1, m_mlp_w1, v_mlp_w1, "update_mlp_w1")
    u_w2 = _adamw_reduce(lands["w2"], mlp_w2, m_mlp_w2, v_mlp_w2, "update_mlp_w2")

    def pack(mix, mlp, fin, gn):
        return jnp.concatenate([mix, mlp, fin.reshape(1, D_MODEL), gn.reshape(2, D_MODEL), jnp.zeros((1, D_MODEL), F32)], axis=0)

    small_w = pack(mix_norm_g, mlp_norm_g, final_norm_g, ret_gn_g)
    small_m = pack(m_mix_norm_g, m_mlp_norm_g, m_final_norm_g, m_ret_gn_g)
    small_v = pack(v_mix_norm_g, v_mlp_norm_g, v_final_norm_g, v_ret_gn_g)
    sg, sd, sm, sv, rel_sum = _small_update(lands["small"], lands["rel"], small_w, small_m, small_v, "update_small")
    g_rel_mine = lax.dynamic_slice(rel_sum, (0, me * rel_cols), (ATT_HEADS, rel_cols))
    rel_d, rel_m, rel_v = _adamw_plain(att_rel_bias[0], g_rel_mine, m_att_rel_bias[0], v_att_rel_bias[0], "update_rel_bias")
    u_rel = [g_rel_mine[None], rel_d[None], rel_m[None], rel_v[None]]

    def unpack(t):
        return dict(mix=t[0:2], mlp=t[2:4], fin=t[4], gn=t[5:7].reshape(1, RET_VW))

    us = [unpack(t) for t in (sg, sd, sm, sv)]
    outs = [sg[7, 0], grad_x]
    for k in range(4):
        outs += [us[k]["mix"], u_rin[k], us[k]["gn"], u_rout[k], u_ain[k], u_rel[k], u_aout[k], us[k]["mlp"],
                 u_w1[k], u_w2[k], us[k]["fin"]]
    return tuple(outs)
```

```python
import functools

import numpy as np
import jax
import jax.numpy as jnp
from jax import lax
from jax.experimental import pallas as pl
from jax.experimental.pallas import tpu as pltpu

F32, BF16 = jnp.float32, jnp.bfloat16

D_MODEL = 1024
CHUNK = 64
RET_HEADS, RET_DK, RET_DV = 4, 256, 512
RET_QK, RET_VW = RET_HEADS * RET_DK, RET_HEADS * RET_DV
RET_IN = 2 * RET_QK + 2 * RET_VW
RET_HEAD_COLS = 2 * RET_DK + 2 * RET_DV
RET_SC = 256
ROPE_BASE = 10000.0
ATT_HEADS, ATT_DH = 16, 64
LANES = 128
ATT_PAIRS = ATT_HEADS * ATT_DH // LANES
BAND_PAST = 8 * CHUNK
MAX_REL = 256
REL_TABLE = MAX_REL + CHUNK
REL_PAD = 384
ATT_CQ = 256
ATT_KW = ATT_CQ + BAND_PAST
ATT_RING = ATT_CQ + ATT_KW
ATT_VARIANTS = BAND_PAST // ATT_CQ + 1
D_FF = 4 * D_MODEL
EPS = 1e-6
EPILOGUE_ROWS = 256
NEG = -1e30
N_DEV = 8
N_PEERS = N_DEV - 1

ADAM_LR, ADAM_B1, ADAM_B2, ADAM_EPS, ADAM_WD, ADAM_STEP = 0.001, 0.9, 0.999, 1e-08, 0.01, 10

VMEM_LIMIT = 56 * 1024 * 1024
MESH = pl.DeviceIdType.MESH
ANY = pl.BlockSpec(memory_space=pl.ANY)


def _dot(a, b):
    return jnp.dot(a, b, preferred_element_type=F32)


def _dot_nt(a, b):
    return lax.dot_general(a, b, (((1,), (1,)), ((), ())), preferred_element_type=F32)


def _dot_tn(a, b):
    return lax.dot_general(a, b, (((0,), (0,)), ((), ())), preferred_element_type=F32)


def _rms(x, g):
    r = lax.rsqrt(jnp.mean(x * x, axis=-1, keepdims=True) + EPS)
    return x * r * g


def _rms_bwd(dn, x, g):
    r = lax.rsqrt(jnp.mean(x * x, axis=-1, keepdims=True) + EPS)
    xh = x * r
    dg = jnp.sum(dn * xh, axis=0, keepdims=True)
    dxh = dn * g
    dx = r * (dxh - xh * jnp.mean(dxh * xh, axis=-1, keepdims=True))
    return dx, dg


def _rms_bwd_epilogue(x_ref, g_ref, dres_ref, dx_ref, dxb_ref, dg_ref):
    for r in range(0, dx_ref.shape[0], EPILOGUE_ROWS):
        rows = slice(r, r + EPILOGUE_ROWS)
        dx, dg = _rms_bwd(dx_ref[rows, :], x_ref[rows, :], g_ref[...])
        dx = dres_ref[rows, :] + dx
        dx_ref[rows, :] = dx
        dxb_ref[rows, :] = dx.astype(BF16)
        dg_ref[...] += dg


def _place():
    return lax.axis_index("x"), lax.axis_index("y"), lax.axis_index("c")


def _lin(p):
    return 4 * p[0] + 2 * p[1] + p[2]


def _cols(width):
    return lambda ref, blk: ref.at[:, pl.ds(pl.multiple_of(blk * width, LANES), width)]


def _rows(height):
    return lambda ref, blk: ref.at[pl.ds(pl.multiple_of(blk * height, 8), height), :]


def _slot(ref, blk):
    return ref.at[blk]


def _whole(ref, blk):
    return ref


class _Exchange:
    def __init__(self, sources, flows, land_shapes):
        self.sources, self.flows, self.land_shapes = list(sources), flows, list(land_shapes)
        n_f = len(flows)
        self.sem_shapes = [pltpu.SemaphoreType.DMA((N_PEERS * n_f,)), pltpu.SemaphoreType.DMA((N_PEERS * n_f,)),
                           pltpu.SemaphoreType.DMA((n_f,))]

    def _copies(self, srcs, lands, sems):
        send_sems, recv_sems, local_sems = sems
        x, y, c = _place()
        me = (x, y, c)
        peers = [(x ^ dx, y ^ dy, c ^ dc) for dx in (0, 1) for dy in (0, 1) for dc in (0, 1)][1:]

        def copy(f, k, sender, to):
            si, sview, li, lview = self.flows[f]
            return pltpu.make_async_remote_copy(
                src_ref=sview(srcs[si], _lin(to)), dst_ref=lview(lands[li], _lin(sender)),
                send_sem=send_sems.at[f * N_PEERS + k], recv_sem=recv_sems.at[f * N_PEERS + k],
                device_id=to, device_id_type=MESH)

        mine, sends, recvs = [], [], []
        for f, (si, sview, li, lview) in enumerate(self.flows):
            mine.append(pltpu.make_async_copy(sview(srcs[si], _lin(me)), lview(lands[li], _lin(me)), local_sems.at[f]))
            for k, peer in enumerate(peers):
                sends.append(copy(f, k, me, peer))
                recvs.append(copy(f, k, peer, me))
        return mine, sends, recvs

    def start(self, srcs, lands, sems):
        mine, sends, _ = self._copies(srcs, lands, sems)
        for cp in mine + sends:
            cp.start()

    def finish(self, srcs, lands, sems):
        mine, sends, recvs = self._copies(srcs, lands, sems)
        for cp in recvs:
            cp.wait_recv()
        for cp in sends:
            cp.wait_send()
        for cp in mine:
            cp.wait()

    def run(self, name):
        n_src, n_land = len(self.sources), len(self.land_shapes)

        def body(*refs):
            srcs, lands, sems = refs[:n_src], refs[n_src:n_src + n_land], refs[n_src + n_land:]
            self.start(srcs, lands, sems)
            self.finish(srcs, lands, sems)

        return pl.pallas_call(body, name=name, in_specs=[ANY] * n_src, out_specs=[ANY] * n_land,
                              out_shape=self.land_shapes, scratch_shapes=self.sem_shapes)(*self.sources)


def _call(body, *, name, grid, in_specs, out_specs, out_shape, args, scratch_shapes=(), ride=None):
    params = pltpu.CompilerParams(dimension_semantics=("arbitrary",) * len(grid), vmem_limit_bytes=VMEM_LIMIT)
    in_specs, out_specs, out_shape, scratch_shapes = list(in_specs), list(out_specs), list(out_shape), list(scratch_shapes)
    if ride is None:
        return pl.pallas_call(body, name=name, grid=grid, in_specs=in_specs, out_specs=out_specs, out_shape=out_shape,
                              scratch_shapes=scratch_shapes, compiler_params=params)(*args)
    n_in, n_out, n_scr = len(in_specs), len(out_specs), len(scratch_shapes)
    n_src, n_land = len(ride.sources), len(ride.land_shapes)

    def riding(*refs):
        bounds = np.cumsum([n_in, n_src, n_out, n_land, n_scr])
        ins, srcs, outs, lands, scr, sems = (refs[a:b] for a, b in zip([0, *bounds], [*bounds, len(refs)]))
        step = functools.reduce(lambda a, d: a * grid[d] + pl.program_id(d), range(len(grid)), 0)
        n_steps = int(np.prod(grid))

        @pl.when(step == 0)
        def _():
            ride.start(srcs, lands, sems)

        if hasattr(ride, "middle"):
            @pl.when(step == (5 * n_steps) // 8)
            def _():
                ride.middle(srcs, lands, sems)

        body(*ins, *outs, *scr)

        @pl.when(step == n_steps - 1)
        def _():
            ride.finish(srcs, lands, sems)

    res = pl.pallas_call(
        riding, name=name, grid=grid, in_specs=in_specs + [ANY] * n_src, out_specs=out_specs + [ANY] * n_land,
        out_shape=out_shape + ride.land_shapes, scratch_shapes=scratch_shapes + ride.sem_shapes,
        compiler_params=params)(*args, *ride.sources)
    return res[:n_out], res[n_out:]


class _ChipGather:
    def __init__(self, shards, out_shapes, out_of, views):
        self.sources, self.land_shapes, self.out_of, self.views = list(shards), list(out_shapes), out_of, views
        n = len(shards)
        self.sem_shapes = [pltpu.SemaphoreType.DMA((N_PEERS * n,)), pltpu.SemaphoreType.DMA((N_PEERS * n,)),
                           pltpu.SemaphoreType.DMA((n,))]

    def _copies(self, srcs, lands, sems):
        send_sems, recv_sems, local_sems = sems
        n = len(srcs)
        outs = [lands[self.out_of[a]] for a in range(n)]
        x, y, c = _place()
        me, sibling = (x, y, c), (x, y, 1 - c)
        chips = [(1 - x, y), (x, 1 - y), (1 - x, 1 - y)]

        def copy(a, k, block, to, own=False):
            dst = self.views[a](outs[a], _lin(block))
            return pltpu.make_async_remote_copy(
                src_ref=srcs[a] if own else dst, dst_ref=dst,
                send_sem=send_sems.at[a * N_PEERS + k], recv_sem=recv_sems.at[a * N_PEERS + k],
                device_id=to, device_id_type=MESH)

        mine = lambda: [pltpu.make_async_copy(srcs[a], self.views[a](outs[a], _lin(me)), local_sems.at[a]) for a in range(n)]
        first = lambda: [cp for a in range(n) for cp in
                         [copy(a, 0, me, sibling, own=True)] + [copy(a, 1 + j, me, (*chip, c), own=True) for j, chip in enumerate(chips)]]
        landed = lambda a, j: copy(a, 1 + j, (*chips[j], c), me)
        passed = lambda a, j: copy(a, 4 + j, (*chips[j], c), sibling)
        from_sibling = lambda a: [copy(a, 0, sibling, me)] + [copy(a, 4 + j, (*chips[j], 1 - c), me) for j in range(3)]
        return n, mine, first, landed, passed, from_sibling

    def start(self, srcs, lands, sems):
        _, mine, first, _, _, _ = self._copies(srcs, lands, sems)
        for cp in mine() + first():
            cp.start()

    def middle(self, srcs, lands, sems):
        n, _, _, landed, passed, _ = self._copies(srcs, lands, sems)
        for j in range(3):
            for a in range(n):
                landed(a, j).wait_recv()
                passed(a, j).start()

    def finish(self, srcs, lands, sems):
        n, mine, first, _, passed, from_sibling = self._copies(srcs, lands, sems)
        for a in range(n):
            for cp in from_sibling(a):
                cp.wait_recv()
        for cp in first() + [passed(a, j) for j in range(3) for a in range(n)]:
            cp.wait_send()
        for cp in mine():
            cp.wait()

    def run(self, name):
        n, n_out = len(self.sources), len(self.land_shapes)

        def body(*refs):
            parts = refs[:n], refs[n:n + n_out], refs[n + n_out:]
            self.start(*parts)
            self.middle(*parts)
            self.finish(*parts)

        return pl.pallas_call(body, name=name, in_specs=[ANY] * n, out_specs=[ANY] * n_out,
                              out_shape=self.land_shapes, scratch_shapes=self.sem_shapes)(*self.sources)


def _resident(shape):
    return pl.BlockSpec(shape, lambda i: (0,) * len(shape), pipeline_mode=pl.Buffered(1))


def _norm_proj(h, g, w, groups, name, tm, ride=None):
    T = h.shape[0]
    N = w.shape[1]
    width = N // groups

    def body(h_ref, g_ref, w_ref, o_ref, n_ref):
        n = _rms(h_ref[...], g_ref[...]).astype(BF16)
        n_ref[...] = n
        if groups == 1:
            o_ref[...] = _dot(n, w_ref[...]).astype(BF16)
        else:
            for p in range(groups):
                o_ref[p] = _dot(n, w_ref[:, p * width:(p + 1) * width]).astype(BF16)

    row = pl.BlockSpec((tm, D_MODEL), lambda i: (i, 0))
    if groups == 1:
        out_spec, out_shape = pl.BlockSpec((tm, N), lambda i: (i, 0)), jax.ShapeDtypeStruct((T, N), BF16)
    else:
        out_spec = pl.BlockSpec((groups, tm, width), lambda i: (0, i, 0))
        out_shape = jax.ShapeDtypeStruct((groups, T, width), BF16)
    return _call(body, name=name, grid=(T // tm,),
                 in_specs=[row, pl.BlockSpec((1, D_MODEL), lambda i: (0, 0)), _resident(w.shape)],
                 out_specs=[out_spec, row], out_shape=[out_shape, jax.ShapeDtypeStruct((T, D_MODEL), BF16)],
                 args=(h, g, w), ride=ride)


def _ret_proj(h, g, w, cos, sin, S, name, tm, ride=None):
    T = h.shape[0]
    per_seq = S // tm

    def body(h_ref, g_ref, w_ref, cos_ref, sin_ref, o_ref, n_ref):
        n = _rms(h_ref[...], g_ref[...]).astype(BF16)
        n_ref[...] = n
        cs, sn = cos_ref[...], sin_ref[...]
        for hd in range(RET_HEADS):
            c0 = hd * RET_HEAD_COLS
            out = _dot(n, w_ref[:, c0:c0 + RET_HEAD_COLS])
            o_ref[:, c0:c0 + RET_DK] = _rope(out[:, 0:RET_DK], cs, sn).astype(BF16)
            o_ref[:, c0 + RET_DK:c0 + 2 * RET_DK] = (_rope(out[:, RET_DK:2 * RET_DK], cs, sn) * (RET_DK ** -0.5)).astype(BF16)
            o_ref[:, c0 + 2 * RET_DK:c0 + RET_HEAD_COLS] = out[:, 2 * RET_DK:].astype(BF16)

    row = pl.BlockSpec((tm, D_MODEL), lambda i: (i, 0))
    trig = pl.BlockSpec((tm, RET_DK // 2), lambda i: (i % per_seq, 0))
    return _call(body, name=name, grid=(T // tm,),
                 in_specs=[row, pl.BlockSpec((1, D_MODEL), lambda i: (0, 0)), _resident(w.shape), trig, trig],
                 out_specs=[pl.BlockSpec((tm, w.shape[1]), lambda i: (i, 0)), row],
                 out_shape=[jax.ShapeDtypeStruct((T, w.shape[1]), BF16), jax.ShapeDtypeStruct((T, D_MODEL), BF16)],
                 args=(h, g, w, cos, sin), ride=ride)


def _proj_bwd(dy, w, x, g, dres, name, tm, ride=None):
    T = x.shape[0]
    groups = dy.shape[0] if dy.ndim == 3 else 1
    width = w.shape[1] // groups

    def body(dy_ref, w_ref, x_ref, g_ref, dres_ref, dx_ref, dxb_ref, dg_ref):
        @pl.when(pl.program_id(0) == 0)
        def _():
            dg_ref[...] = jnp.zeros_like(dg_ref)

        if groups == 1:
            dn = _dot_nt(dy_ref[...], w_ref[...])
        else:
            dn = sum(_dot_nt(dy_ref[p], w_ref[:, p * width:(p + 1) * width]) for p in range(groups))
        dx_ref[...] = dn
        _rms_bwd_epilogue(x_ref, g_ref, dres_ref, dx_ref, dxb_ref, dg_ref)

    row = pl.BlockSpec((tm, D_MODEL), lambda i: (i, 0))
    vec = pl.BlockSpec((1, D_MODEL), lambda i: (0, 0))
    dy_spec = (pl.BlockSpec((tm, w.shape[1]), lambda i: (i, 0)) if groups == 1
               else pl.BlockSpec((groups, tm, width), lambda i: (0, i, 0)))
    return _call(body, name=name, grid=(T // tm,),
                 in_specs=[dy_spec, _resident(w.shape), row, vec, row], out_specs=[row, row, vec],
                 out_shape=[jax.ShapeDtypeStruct((T, D_MODEL), F32), jax.ShapeDtypeStruct((T, D_MODEL), BF16),
                            jax.ShapeDtypeStruct((1, D_MODEL), F32)],
                 args=(dy, w, x, g, dres), ride=ride)


def _matmul_res(a, w, res, name, tm=1024):
    T, K = a.shape

    def body(a_ref, w_ref, r_ref, o_ref):
        o_ref[...] = r_ref[...] + _dot(a_ref[...], w_ref[...])

    row = pl.BlockSpec((tm, D_MODEL), lambda i: (i, 0))
    return _call(body, name=name, grid=(T // tm,),
                 in_specs=[pl.BlockSpec((tm, K), lambda i: (i, 0)), pl.BlockSpec((K, D_MODEL), lambda i: (0, 0)), row],
                 out_specs=[row], out_shape=[jax.ShapeDtypeStruct((T, D_MODEL), F32)], args=(a, w, res))[0]


def _matmul_nt(dy, w, name, tm=1024):
    T, N = dy.shape
    K = w.shape[0]

    def body(dy_ref, w_ref, o_ref):
        o_ref[...] = _dot_nt(dy_ref[...], w_ref[...]).astype(BF16)

    return _call(body, name=name, grid=(T // tm,),
                 in_specs=[pl.BlockSpec((tm, N), lambda i: (i, 0)), pl.BlockSpec((K, N), lambda i: (0, 0))],
                 out_specs=[pl.BlockSpec((tm, K), lambda i: (i, 0))], out_shape=[jax.ShapeDtypeStruct((T, K), BF16)],
                 args=(dy, w))[0]


def _wgrad(a, a_spec, b, b_specs, m, n, bm, bn, n_k, name, square_a=False, ride=None):
    b_specs = b_specs if isinstance(b_specs, (list, tuple)) else [b_specs]
    n_b = len(b_specs)

    def body(a_ref, *rest):
        b_refs = rest[:n_b]
        o_ref, acc = rest[n_b:]
        k = pl.program_id(2)

        @pl.when(k == 0)
        def _():
            acc[...] = jnp.zeros_like(acc)

        av = a_ref[...]
        if square_a:
            af = av.astype(F32)
            av = (af * af).astype(BF16)
        bv = b_refs[0][...] if n_b == 1 else jnp.concatenate([r[...] for r in b_refs], axis=1)
        acc[...] += _dot_tn(av, bv)

        @pl.when(k == n_k - 1)
        def _():
            o_ref[...] = acc[...].astype(BF16)

    res = _call(body, name=name, grid=(m // bm, n // bn, n_k), in_specs=[a_spec, *b_specs],
                out_specs=[pl.BlockSpec((bm, bn), lambda i, j, k: (i, j))], out_shape=[jax.ShapeDtypeStruct((m, n), BF16)],
                scratch_shapes=[pltpu.VMEM((bm, bn), F32)], args=(a, *([b] * n_b)), ride=ride)
    return res[0] if ride is None else (res[0][0], res[1])


def _mlp_fwd(h, g, w1, w2, name, tm=1024, tf=1024, ride=None):
    T = h.shape[0]

    def body(h_ref, g_ref, w1_ref, w2_ref, ho_ref, n_ref, u_ref):
        @pl.when(pl.program_id(1) == 0)
        def _():
            n_ref[...] = _rms(h_ref[...], g_ref[...]).astype(BF16)
            ho_ref[...] = h_ref[...]

        u = jnp.maximum(_dot(n_ref[...], w1_ref[...]), 0.0)
        u_ref[...] = u.astype(BF16)
        ho_ref[...] += _dot((u * u).astype(BF16), w2_ref[...])

    row = pl.BlockSpec((tm, D_MODEL), lambda i, j: (i, 0))
    return _call(body, name=name, grid=(T // tm, D_FF // tf),
                 in_specs=[row, pl.BlockSpec((1, D_MODEL), lambda i, j: (0, 0)),
                           pl.BlockSpec((D_MODEL, tf), lambda i, j: (0, j)), pl.BlockSpec((tf, D_MODEL), lambda i, j: (j, 0))],
                 out_specs=[row, row, pl.BlockSpec((tm, tf), lambda i, j: (i, j))],
                 out_shape=[jax.ShapeDtypeStruct((T, D_MODEL), F32), jax.ShapeDtypeStruct((T, D_MODEL), BF16),
                            jax.ShapeDtypeStruct((T, D_FF), BF16)],
                 args=(h, g, w1, w2), ride=ride)


def _mlp_bwd(dh, u, w1, w2, h, g, name, tm=1024, tf=1024, ride=None):
    T = h.shape[0]
    n_j = D_FF // tf

    def body(dh_ref, u_ref, w1_ref, w2_ref, h_ref, g_ref, dx_ref, dxb_ref, du_ref, dg_ref, dhb):
        acc = dx_ref
        i, j = pl.program_id(0), pl.program_id(1)

        @pl.when(j == 0)
        def _():
            dhb[...] = dh_ref[...].astype(BF16)
            acc[...] = jnp.zeros_like(acc)

        @pl.when((i == 0) & (j == 0))
        def _():
            dg_ref[...] = jnp.zeros_like(dg_ref)

        da = _dot_nt(dhb[...], w2_ref[...])
        du = (da * (2.0 * u_ref[...].astype(F32))).astype(BF16)
        du_ref[...] = du
        acc[...] += _dot_nt(du, w1_ref[...])

        @pl.when(j == n_j - 1)
        def _():
            _rms_bwd_epilogue(h_ref, g_ref, dh_ref, dx_ref, dxb_ref, dg_ref)

    row = pl.BlockSpec((tm, D_MODEL), lambda i, j: (i, 0))
    vec = pl.BlockSpec((1, D_MODEL), lambda i, j: (0, 0))
    hid = pl.BlockSpec((tm, tf), lambda i, j: (i, j))
    return _call(body, name=name, grid=(T // tm, n_j),
                 in_specs=[row, hid, pl.BlockSpec((D_MODEL, tf), lambda i, j: (0, j)),
                           pl.BlockSpec((tf, D_MODEL), lambda i, j: (j, 0)), row, vec],
                 out_specs=[row, row, hid, vec],
                 out_shape=[jax.ShapeDtypeStruct((T, D_MODEL), F32), jax.ShapeDtypeStruct((T, D_MODEL), BF16),
                            jax.ShapeDtypeStruct((T, D_FF), BF16), jax.ShapeDtypeStruct((1, D_MODEL), F32)],
                 scratch_shapes=[pltpu.VMEM((tm, D_MODEL), BF16)], args=(dh, u, w1, w2, h, g), ride=ride)


def _final_loss(h, g, target, name, tm=512):
    T = h.shape[0]

    def body(h_ref, g_ref, t_ref, dh_ref, dhb_ref, loss_ref, dg_ref):
        @pl.when(pl.program_id(0) == 0)
        def _():
            loss_ref[...] = jnp.zeros_like(loss_ref)
            dg_ref[...] = jnp.zeros_like(dg_ref)

        x = h_ref[...]
        gg = g_ref[...]
        r = lax.rsqrt(jnp.mean(x * x, axis=-1, keepdims=True) + EPS)
        xh = x * r
        e = xh * gg - t_ref[...]
        per_tok = jnp.mean(e * e, axis=-1, keepdims=True)
        loss_ref[...] += 0.5 * jnp.sum(per_tok, axis=0, keepdims=True)
        dy = e * (1.0 / D_MODEL)
        dg_ref[...] += jnp.sum(dy * xh, axis=0, keepdims=True)
        dxh = dy * gg
        dx = r * (dxh - xh * jnp.mean(dxh * xh, axis=-1, keepdims=True))
        dh_ref[...] = dx
        dhb_ref[...] = dx.astype(BF16)

    row = pl.BlockSpec((tm, D_MODEL), lambda i: (i, 0))
    vec = pl.BlockSpec((1, D_MODEL), lambda i: (0, 0))
    return _call(body, name=name, grid=(T // tm,), in_specs=[row, vec, row],
                 out_specs=[row, row, pl.BlockSpec((8, LANES), lambda i: (0, 0)), vec],
                 out_shape=[jax.ShapeDtypeStruct((T, D_MODEL), F32), jax.ShapeDtypeStruct((T, D_MODEL), BF16),
                            jax.ShapeDtypeStruct((8, LANES), F32), jax.ShapeDtypeStruct((1, D_MODEL), F32)],
                 args=(h, g, target))


def _ret_constants(S):
    log_gamma = jnp.log1p(-jnp.exp2(-5.0 - jnp.arange(RET_HEADS, dtype=F32)))
    idx = jnp.arange(RET_SC, dtype=F32)
    i, j = idx[:, None], idx[None, :]
    same_chunk = jnp.floor(i / CHUNK) == jnp.floor(j / CHUNK)
    mask = jnp.where((j <= i) | same_chunk, jnp.exp(log_gamma[:, None, None] * jnp.abs(i - j)[None]), 0.0)
    qdec = jnp.exp(log_gamma[:, None] * (idx + 1.0)[None, :])[:, :, None]
    kdec = jnp.exp(log_gamma[:, None] * (RET_SC - 1 - idx)[None, :])[:, :, None]
    cdec = jnp.exp(log_gamma * RET_SC)[:, None, None]
    half = RET_DK // 2
    inv = jnp.exp(-jnp.log(ROPE_BASE) * jnp.arange(half, dtype=F32) / half)
    ang = jnp.arange(S, dtype=F32)[:, None] * inv[None, :]
    return jnp.cos(ang), jnp.sin(ang), mask.astype(F32), qdec, kdec, cdec


def _rope(t, cs, sn):
    t1, t2 = t[:, :RET_DK // 2], t[:, RET_DK // 2:]
    return jnp.concatenate([t1 * cs - t2 * sn, t1 * sn + t2 * cs], axis=-1)


def _rope_bwd(d, cs, sn):
    d1, d2 = d[:, :RET_DK // 2], d[:, RET_DK // 2:]
    return jnp.concatenate([d1 * cs + d2 * sn, d2 * cs - d1 * sn], axis=-1)


def _ret_specs(B, S, reverse):
    n_sc = S // RET_SC

    def cc(c):
        return n_sc - 1 - c if reverse else c

    return dict(
        proj=pl.BlockSpec((B, RET_SC, RET_HEAD_COLS), lambda h, c: (0, cc(c), h)),
        trig=pl.BlockSpec((RET_SC, RET_DK // 2), lambda h, c: (cc(c), 0)),
        mask=pl.BlockSpec((None, RET_SC, RET_SC), lambda h, c: (h, 0, 0)),
        dec=pl.BlockSpec((None, RET_SC, 1), lambda h, c: (h, 0, 0)),
        cdec=pl.BlockSpec((None, 1, 1), lambda h, c: (h, 0, 0)),
        gn=pl.BlockSpec((1, RET_DV), lambda h, c: (0, h)),
        val=pl.BlockSpec((B, RET_SC, RET_DV), lambda h, c: (0, cc(c), h)),
        state=pl.BlockSpec((B, None, None, RET_DK, RET_DV), lambda h, c: (0, h, cc(c), 0, 0)),
    )


def _ret_qkvg(p_ref, kdec):
    qb = p_ref[:, 0:RET_DK]
    kb = p_ref[:, RET_DK:2 * RET_DK]
    kt = (kb.astype(F32) * kdec).astype(BF16)
    v = p_ref[:, 2 * RET_DK:2 * RET_DK + RET_DV]
    gate = p_ref[:, 2 * RET_DK + RET_DV:RET_HEAD_COLS].astype(F32)
    return qb, kb, kt, v, gate


def _group_norm(o):
    mu = jnp.mean(o, axis=-1, keepdims=True)
    oc = o - mu
    rstd = lax.rsqrt(jnp.mean(oc * oc, axis=-1, keepdims=True) + EPS)
    return oc * rstd, rstd


def _ret_fwd(proj, consts, gn, B, S, name, ride=None):
    T = B * S
    n_sc = S // RET_SC
    sp = _ret_specs(B, S, False)

    def body(p_ref, m_ref, qd_ref, kd_ref, cd_ref, gn_ref, y_ref, o_ref, st_ref, state):
        @pl.when(pl.program_id(1) == 0)
        def _():
            state[...] = jnp.zeros_like(state)

        seqs = range(B)
        qkvg = [_ret_qkvg(p_ref.at[b], kd_ref[...]) for b in seqs]
        qb, kb, kt, v = ([qkvg[b][i] for b in seqs] for i in range(4))
        sc = [_dot_nt(qb[b], kb[b]) for b in seqs]
        sb = [state[b].astype(BF16) for b in seqs]
        cross = [_dot(qb[b], sb[b]) for b in seqs]
        for b in seqs:
            st_ref[b] = sb[b]
        p = [(sc[b] * m_ref[...]).astype(BF16) for b in seqs]
        o = [_dot(p[b], v[b]) + cross[b] * qd_ref[...] for b in seqs]
        upd = [_dot_tn(kt[b], v[b]) for b in seqs]
        for b in seqs:
            o_ref[b] = o[b].astype(BF16)
            ohat, _ = _group_norm(o[b])
            gate = qkvg[b][4]
            y_ref[b] =(gate * jax.nn.sigmoid(gate) * (ohat * gn_ref[...])).astype(BF16)
            state[b] = state[b] * cd_ref[...] + upd[b]

    res = _call(
        body, name=name, grid=(RET_HEADS, n_sc),
        in_specs=[sp["proj"], sp["mask"], sp["dec"], sp["dec"], sp["cdec"], sp["gn"]],
        out_specs=[sp["val"], sp["val"], sp["state"]],
        out_shape=[jax.ShapeDtypeStruct((B, S, RET_VW), BF16), jax.ShapeDtypeStruct((B, S, RET_VW), BF16),
                   jax.ShapeDtypeStruct((B, RET_HEADS, n_sc, RET_DK, RET_DV), BF16)],
        scratch_shapes=[pltpu.VMEM((B, RET_DK, RET_DV), F32)],
        args=(proj.reshape(B, S, -1), *consts[2:], gn), ride=ride)
    (y, o, states), lands = res if ride is not None else (res, None)
    outs = [y.reshape(T, RET_VW), o.reshape(T, RET_VW), states]
    return outs if ride is None else (outs, lands)


def _ret_bwd(proj, consts, gn, o, states, dy, B, S, name, ride=None):
    T = B * S
    n_sc = S // RET_SC
    sp = _ret_specs(B, S, True)

    def body(p_ref, cos_ref, sin_ref, m_ref, qd_ref, kd_ref, cd_ref, gn_ref, o_ref, st_ref, dy_ref,
             dp_ref, dgn_ref, dstate):
        @pl.when(pl.program_id(1) == 0)
        def _():
            dstate[...] = jnp.zeros_like(dstate)
            dgn_ref[...] = jnp.zeros_like(dgn_ref)

        seqs = range(B)
        cs, sn = cos_ref[...], sin_ref[...]
        m, gnv = m_ref[...], gn_ref[...]
        qkvg = [_ret_qkvg(p_ref.at[b], kd_ref[...]) for b in seqs]
        qb, kb, kt, v = ([qkvg[b][i] for b in seqs] for i in range(4))
        sc = [_dot_nt(qb[b], kb[b]) for b in seqs]
        dsb = [dstate[b].astype(BF16) for b in seqs]
        dv_state = [_dot(kt[b], dsb[b]) for b in seqs]
        dkt = [_dot_nt(v[b], dsb[b]) for b in seqs]
        do, dgate = [], []
        for b in seqs:
            gate = qkvg[b][4]
            ohat, rstd = _group_norm(o_ref[b].astype(F32))
            dyv = dy_ref[b].astype(F32)
            sg = jax.nn.sigmoid(gate)
            don = dyv * (gate * sg)
            dgate.append(dyv * (ohat * gnv) * (sg * (1.0 + gate * (1.0 - sg))))
            dgn_ref[...] += jnp.sum(don * ohat, axis=0, keepdims=True)
            dohat = don * gnv
            do.append(rstd * (dohat - jnp.mean(dohat, axis=-1, keepdims=True)
                              - ohat * jnp.mean(dohat * ohat, axis=-1, keepdims=True)))
        dob = [do[b].astype(BF16) for b in seqs]
        doq = [(do[b] * qd_ref[...]).astype(BF16) for b in seqs]
        dsc_f = [_dot_nt(dob[b], v[b]) for b in seqs]
        dq_state = [_dot_nt(doq[b], st_ref[b]) for b in seqs]
        dstate_upd = [_dot_tn(qb[b], doq[b]) for b in seqs]
        p = [(sc[b] * m).astype(BF16) for b in seqs]
        dsc = [(dsc_f[b] * m).astype(BF16) for b in seqs]
        dv = [_dot_tn(p[b], dob[b]) + dv_state[b] for b in seqs]
        dq = [_dot(dsc[b], kb[b]) + dq_state[b] for b in seqs]
        dk = [(_dot_tn(dsc[b], qb[b]) + dkt[b] * kd_ref[...]) * (RET_DK ** -0.5) for b in seqs]
        for b in seqs:
            dstate[b] = dstate[b] * cd_ref[...] + dstate_upd[b]
            dp_ref[b, :, 0:RET_DK] = _rope_bwd(dq[b], cs, sn).astype(BF16)
            dp_ref[b, :, RET_DK:2 * RET_DK] = _rope_bwd(dk[b], cs, sn).astype(BF16)
            dp_ref[b, :, 2 * RET_DK:2 * RET_DK + RET_DV] = dv[b].astype(BF16)
            dp_ref[b, :, 2 * RET_DK + RET_DV:RET_HEAD_COLS] = dgate[b].astype(BF16)

    res = _call(
        body, name=name, grid=(RET_HEADS, n_sc),
        in_specs=[sp["proj"], sp["trig"], sp["trig"], sp["mask"], sp["dec"], sp["dec"], sp["cdec"], sp["gn"],
                  sp["val"], sp["state"], sp["val"]],
        out_specs=[sp["proj"], sp["gn"]],
        out_shape=[jax.ShapeDtypeStruct((B, S, RET_HEADS * RET_HEAD_COLS), BF16), jax.ShapeDtypeStruct((1, RET_VW), F32)],
        scratch_shapes=[pltpu.VMEM((B, RET_DK, RET_DV), F32)],
        args=(proj.reshape(B, S, -1), *consts, gn, o.reshape(B, S, -1), states, dy.reshape(B, S, -1)), ride=ride)
    (dproj, dgn), lands = res if ride is not None else (res, None)
    outs = [dproj.reshape(T, -1), dgn]
    return outs if ride is None else (outs, lands)


def _ring_index():
    u = np.arange(ATT_RING)
    offset = np.where(u < ATT_KW, u, u - ATT_RING)
    return np.clip(offset - BAND_PAST, -MAX_REL, CHUNK - 1) + MAX_REL


def _bias_ring(rel):
    n_clip = BAND_PAST - MAX_REL
    n_hi = ATT_KW - n_clip - REL_TABLE
    ring = jnp.concatenate([jnp.broadcast_to(rel[:, :1], (ATT_HEADS, n_clip)), rel,
                            jnp.broadcast_to(rel[:, -1:], (ATT_HEADS, n_hi)),
                            jnp.broadcast_to(rel[:, :1], (ATT_HEADS, ATT_CQ))], axis=1)
    return jnp.broadcast_to(ring.reshape(ATT_PAIRS, 2, 1, ATT_RING), (ATT_PAIRS, 2, 8, ATT_RING))


def _band_masks():
    i = np.arange(ATT_CQ)[:, None]
    j = np.arange(ATT_KW)[None, :]
    lo = CHUNK * (i // CHUNK)
    band = np.where((j >= lo) & (j < lo + BAND_PAST + CHUNK), 0.0, NEG)
    return np.stack([band + np.where(j + v * ATT_CQ >= BAND_PAST, 0.0, NEG)
                     for v in range(ATT_VARIANTS)]).astype(np.float32)


def _attn_bias(bias_scr, ring_ref, band_ref):
    for hh in range(2):
        rows = jnp.broadcast_to(ring_ref[hh, 0:1, :], (ATT_CQ, ATT_RING))
        toeplitz = pltpu.roll(rows, 0, 1, stride=1, stride_axis=0)[:, :ATT_KW]
        for v in range(ATT_VARIANTS):
            bias_scr[v, hh] = toeplitz * LOG2E + band_ref[v]


ATT_STRIP = 32
ATT_SCALE = ATT_DH ** -0.5
LOG2E = 1.4426950408889634


def _strips(fn):
    def strip(r, carry):
        fn(pl.ds(pl.multiple_of(r * ATT_STRIP, ATT_STRIP), ATT_STRIP))
        return carry

    lax.fori_loop(0, ATT_CQ // ATT_STRIP, strip, 0, unroll=True)


def _attn_prepare(kpad, vpad, qkv_ref):
    kpad[0:BAND_PAST, :] = jnp.zeros((BAND_PAST, LANES), BF16)
    vpad[0:BAND_PAST, :] = jnp.zeros((BAND_PAST, LANES), BF16)
    kpad[BAND_PAST:, :] = qkv_ref[1]
    vpad[BAND_PAST:, :] = qkv_ref[2]


def _attn_queries(qkv_ref, qs):
    lane = lax.broadcasted_iota(jnp.int32, (1, LANES), 1)
    q = (qkv_ref[0, pl.ds(qs, ATT_CQ), :].astype(F32) * (ATT_SCALE * LOG2E)).astype(BF16)
    return [jnp.where((lane >= ATT_DH * hh) & (lane < ATT_DH * (hh + 1)), q, jnp.zeros_like(q)) for hh in range(2)]


def _attn_scores(s_ref, qkv_ref, kpad, b_ref, t, n_qb):
    t = jnp.minimum(t, n_qb - 1)
    qs = pl.multiple_of(t * ATT_CQ, ATT_CQ)
    variant = jnp.minimum(t, ATT_VARIANTS - 1)
    kw = kpad[pl.ds(qs, ATT_KW), :]
    qm = _attn_queries(qkv_ref, qs)
    for hh in range(2):
        s_ref[hh] = _dot_nt(qm[hh], kw) + b_ref[variant, hh]


def _attn_softmax(s_ref, e_ref, linv_ref, m_ref=None):
    def strip(rows):
        s = s_ref[rows, :]
        m = jnp.max(s, axis=-1, keepdims=True)
        e = jnp.exp2(s - m)
        e_ref[rows, :] = e.astype(BF16)
        linv_ref[rows, :] = jnp.broadcast_to(1.0 / jnp.sum(e, axis=-1, keepdims=True), (ATT_STRIP, LANES))
        if m_ref is not None:
            m_ref[rows, :] = jnp.broadcast_to(m, (ATT_STRIP, LANES))

    _strips(strip)


_RING_SPEC = pl.BlockSpec((None, 2, 8, ATT_RING), lambda hp, b: (hp, 0, 0, 0))
_BAND_SPEC = pl.BlockSpec((ATT_VARIANTS, ATT_CQ, ATT_KW), lambda hp, b: (0, 0, 0))


def _attn_fwd(qkv3, ring, B, S, name, ride=None):
    T = B * S
    n_qb = S // ATT_CQ

    def body(qkv_ref, ring_ref, band_ref, o_ref, kpad, vpad, b_ref, s_scr, e_scr, linv_scr):
        @pl.when(pl.program_id(1) == 0)
        def _():
            _attn_bias(b_ref, ring_ref, band_ref)

        _attn_prepare(kpad, vpad, qkv_ref)
        e_scr[...] = jnp.zeros_like(e_scr)
        linv_scr[...] = jnp.zeros_like(linv_scr)
        lane = lax.broadcasted_iota(jnp.int32, (1, LANES), 1)

        def softmax(slot):
            for hh in range(2):
                _attn_softmax(s_scr.at[slot, hh], e_scr.at[slot, hh], linv_scr.at[slot, hh])

        def output(t, slot):
            qs = pl.multiple_of(jnp.maximum(t, 0) * ATT_CQ, ATT_CQ)
            vw = vpad[pl.ds(qs, ATT_KW), :]
            outs = [_dot(e_scr[slot, hh], vw) * linv_scr[slot, hh] for hh in range(2)]
            o_ref[pl.ds(qs, ATT_CQ), :] = jnp.where(lane < ATT_DH, outs[0], outs[1]).astype(BF16)

        def pair(u, carry):
            t = 2 * u
            _attn_scores(s_scr.at[1], qkv_ref, kpad, b_ref, t + 1, n_qb)
            softmax(0)
            output(t - 1, 1)
            _attn_scores(s_scr.at[0], qkv_ref, kpad, b_ref, t + 2, n_qb)
            softmax(1)
            output(t, 0)
            return carry

        _attn_scores(s_scr.at[0], qkv_ref, kpad, b_ref, 0, n_qb)
        lax.fori_loop(0, n_qb // 2, pair, 0)
        output(n_qb - 1, 1)

    return _call(body, name=name, grid=(ATT_PAIRS, B),
                 in_specs=[pl.BlockSpec((3, S, LANES), lambda hp, b: (0, b, hp)), _RING_SPEC, _BAND_SPEC],
                 out_specs=[pl.BlockSpec((S, LANES), lambda hp, b: (b, hp))],
                 out_shape=[jax.ShapeDtypeStruct((T, D_MODEL), BF16)],
                 scratch_shapes=[pltpu.VMEM((S + BAND_PAST, LANES), BF16), pltpu.VMEM((S + BAND_PAST, LANES), BF16),
                                 pltpu.VMEM((ATT_VARIANTS, 2, ATT_CQ, ATT_KW), F32),
                                 pltpu.VMEM((2, 2, ATT_CQ, ATT_KW), F32),
                                 pltpu.VMEM((2, 2, ATT_CQ, ATT_KW), BF16), pltpu.VMEM((2, 2, ATT_CQ, LANES), F32)],
                 args=(qkv3, ring, jnp.asarray(_band_masks())), ride=ride)


def _attn_bwd(qkv3, ring, do, B, S, name, ride=None):
    T = B * S
    n_qb = S // ATT_CQ

    def body(qkv_ref, ring_ref, band_ref, do_ref, dqkv_ref, dring_ref, kpad, vpad, dkacc, dvacc, b_ref, db_ref):
        @pl.when(pl.program_id(1) == 0)
        def _():
            _attn_bias(b_ref, ring_ref, band_ref)
            db_ref[...] = jnp.zeros_like(db_ref)

        _attn_prepare(kpad, vpad, qkv_ref)
        dkacc[...] = jnp.zeros_like(dkacc)
        dvacc[...] = jnp.zeros_like(dvacc)
        lane = lax.broadcasted_iota(jnp.int32, (1, LANES), 1)

        def step(qb, carry):
            qs = pl.multiple_of(qb * ATT_CQ, ATT_CQ)
            variant = jnp.minimum(qb, ATT_VARIANTS - 1)
            dov = do_ref[pl.ds(qs, ATT_CQ), :].astype(F32)
            kw = kpad[pl.ds(qs, ATT_KW), :]
            vw = vpad[pl.ds(qs, ATT_KW), :]
            heads = (0, 1)
            qmh = _attn_queries(qkv_ref, qs)
            s = [_dot_nt(qmh[hh], kw) + b_ref[variant, hh] for hh in heads]
            e, linv, dom, dp = [None, None], [None, None], [None, None], [None, None]
            for hh in heads:
                e[hh] = jnp.exp2(s[hh] - jnp.max(s[hh], axis=-1, keepdims=True))
                linv[hh] = 1.0 / jnp.sum(e[hh], axis=-1, keepdims=True)
                sel = (lane >= ATT_DH * hh) & (lane < ATT_DH * (hh + 1))
                dom[hh] = jnp.where(sel, dov * linv[hh], 0.0).astype(BF16)
                dp[hh] = _dot_nt(dom[hh], vw)
            dqs, dk, dv = [], None, None
            for hh in heads:
                ds = e[hh] * (dp[hh] - jnp.sum(dp[hh] * e[hh], axis=-1, keepdims=True) * linv[hh])
                db_ref[hh] += ds
                dsb = ds.astype(BF16)
                dqs.append(_dot(dsb, kw) * ATT_SCALE)
                dk_h = _dot_tn(qmh[hh], dsb) * (1.0 / LOG2E)
                dv_h = _dot_tn(dom[hh], e[hh].astype(BF16))
                dk = dk_h if dk is None else dk + dk_h
                dv = dv_h if dv is None else dv + dv_h
            dqkv_ref[0, pl.ds(qs, ATT_CQ), :] = jnp.where(lane < ATT_DH, dqs[0], dqs[1]).astype(BF16)
            dkacc[:, pl.ds(qs, ATT_KW)] += dk
            dvacc[:, pl.ds(qs, ATT_KW)] += dv
            return carry

        lax.fori_loop(0, n_qb, step, 0, unroll=4)
        dqkv_ref[1] = dkacc[:, BAND_PAST:].T.astype(BF16)
        dqkv_ref[2] = dvacc[:, BAND_PAST:].T.astype(BF16)

        @pl.when(pl.program_id(1) == B - 1)
        def _():
            r = lax.broadcasted_iota(jnp.int32, (ATT_CQ, ATT_CQ), 0)
            c = lax.broadcasted_iota(jnp.int32, (ATT_CQ, ATT_CQ), 1)
            reverse = jnp.where(r + c == ATT_CQ - 1, 1.0, 0.0).astype(BF16)
            for hh in range(2):
                x = jnp.concatenate([db_ref[hh], jnp.zeros((ATT_CQ, ATT_RING - ATT_KW), F32)], axis=1)
                flipped = jnp.zeros((ATT_CQ, ATT_RING), F32)
                for _ in range(3):
                    part = x.astype(BF16)
                    flipped = flipped + _dot(reverse, part)
                    x = x - part.astype(F32)
                aligned = pltpu.roll(flipped, ATT_KW + 1, 1, stride=1, stride_axis=0)
                dring_ref[hh] = jnp.sum(aligned.reshape(ATT_CQ // 8, 8, ATT_RING), axis=0)

    qkv_spec = pl.BlockSpec((3, S, LANES), lambda hp, b: (0, b, hp))
    return _call(body, name=name, grid=(ATT_PAIRS, B),
                 in_specs=[qkv_spec, _RING_SPEC, _BAND_SPEC, pl.BlockSpec((S, LANES), lambda hp, b: (b, hp))],
                 out_specs=[qkv_spec, _RING_SPEC],
                 out_shape=[jax.ShapeDtypeStruct((3, T, D_MODEL), BF16),
                            jax.ShapeDtypeStruct((ATT_PAIRS, 2, 8, ATT_RING), F32)],
                 scratch_shapes=[pltpu.VMEM((S + BAND_PAST, LANES), BF16), pltpu.VMEM((S + BAND_PAST, LANES), BF16),
                                 pltpu.VMEM((LANES, S + BAND_PAST), F32), pltpu.VMEM((LANES, S + BAND_PAST), F32),
                                 pltpu.VMEM((ATT_VARIANTS, 2, ATT_CQ, ATT_KW), F32), pltpu.VMEM((2, ATT_CQ, ATT_KW), F32)],
                 args=(qkv3, ring, jnp.asarray(_band_masks()), do), ride=ride)


def _rel_bias_grad(dring, name):
    fold = np.zeros((ATT_RING, REL_PAD), np.float32)
    fold[np.arange(ATT_RING), _ring_index()] = 1.0
    fold = jnp.asarray(fold, BF16)

    def body(d_ref, f_ref, o_ref):
        x = jnp.sum(d_ref[...], axis=0, keepdims=True)
        x = jnp.broadcast_to(x, (8, ATT_RING))
        acc = jnp.zeros((8, REL_PAD), F32)
        for _ in range(3):
            part = x.astype(BF16)
            acc = acc + _dot(part, f_ref[...])
            x = x - part.astype(F32)
        o_ref[...] = acc[0:1, :]

    out = _call(body, name=name, grid=(ATT_HEADS,),
                in_specs=[pl.BlockSpec((None, 8, ATT_RING), lambda h: (h, 0, 0)),
                          pl.BlockSpec((ATT_RING, REL_PAD), lambda h: (0, 0))],
                out_specs=[pl.BlockSpec((None, 1, REL_PAD), lambda h: (h, 0, 0))],
                out_shape=[jax.ShapeDtypeStruct((ATT_HEADS, 1, REL_PAD), F32)],
                args=(dring.reshape(ATT_HEADS, 8, ATT_RING), fold))[0]
    return out.reshape(ATT_HEADS, REL_PAD)


def _adamw(w, g, m, v):
    m = ADAM_B1 * m + (1.0 - ADAM_B1) * g
    v = ADAM_B2 * v + (1.0 - ADAM_B2) * (g * g)
    m_hat = m / (1.0 - ADAM_B1 ** ADAM_STEP)
    v_hat = v / (1.0 - ADAM_B2 ** ADAM_STEP)
    delta = -ADAM_LR * (m_hat / (jnp.sqrt(v_hat) + ADAM_EPS) + ADAM_WD * w)
    return delta, m, v


def _sum_devices(ref):
    g = ref[0].astype(F32)
    for d in range(1, N_DEV):
        g = g + ref[d].astype(F32)
    return g


def _adamw_reduce(lands, w, m, v, name, tr=256):
    L, R, C = w.shape
    tr = min(tr, R)
    n_i = R // tr

    def body(*refs):
        l_refs = refs[:L]
        w_ref, m_ref, v_ref, g_out, d_out, m_out, v_out = refs[L:]
        layer = pl.program_id(0)
        for l in range(L):
            @pl.when(layer == l)
            def _(l=l):
                g = _sum_devices(l_refs[l])
                g_out[...] = g
                d_out[...], m_out[...], v_out[...] = _adamw(w_ref[...], g, m_ref[...], v_ref[...])

    def land_spec(l):
        return pl.BlockSpec((N_DEV, tr, C), lambda ly, i: (0, jnp.where(ly == l, i, jnp.where(ly < l, 0, n_i - 1)), 0))

    blk = pl.BlockSpec((None, tr, C), lambda ly, i: (ly, i, 0))
    return _call(body, name=name, grid=(L, n_i), in_specs=[land_spec(l) for l in range(L)] + [blk, blk, blk],
                 out_specs=[blk] * 4, out_shape=[jax.ShapeDtypeStruct((L, R, C), F32)] * 4, args=(*lands, w, m, v))


def _small_update(land_small, land_rel, w, m, v, name):
    def body(ls_ref, lr_ref, w_ref, m_ref, v_ref, g_out, d_out, m_out, v_out, rel_out):
        g = _sum_devices(ls_ref)
        g_out[...] = g
        d_out[...], m_out[...], v_out[...] = _adamw(w_ref[...], g, m_ref[...], v_ref[...])
        rel_out[...] = _sum_devices(lr_ref)

    return pl.pallas_call(
        body, name=name,
        out_shape=[jax.ShapeDtypeStruct(w.shape, F32)] * 4 + [jax.ShapeDtypeStruct(land_rel.shape[1:], F32)],
    )(land_small, land_rel, w, m, v)


def _adamw_plain(w, g, m, v, name):
    def body(w_ref, g_ref, m_ref, v_ref, d_out, m_out, v_out):
        d_out[...], m_out[...], v_out[...] = _adamw(w_ref[...], g_ref[...], m_ref[...], v_ref[...])

    return pl.pallas_call(body, name=name, out_shape=[jax.ShapeDtypeStruct(w.shape, F32)] * 3)(w, g, m, v)


def _ret_piece_of_column_block(p):
    per_head = RET_HEAD_COLS // RET_DK
    qk = jnp.where(p < RET_HEADS, per_head * p, per_head * (p - RET_HEADS) + 1)
    pv = p - 2 * RET_HEADS
    vv = per_head * (pv // 2) + 2 + pv % 2
    pg = p - 4 * RET_HEADS
    gg = per_head * (pg // 2) + 4 + pg % 2
    return jnp.where(p < 2 * RET_HEADS, qk, jnp.where(p < 4 * RET_HEADS, vv, gg))


def _step(x, target, shards, mix_g, gn_g, rel_shard, mlp_g, final_g):
    s_rin, s_rout, s_ain, s_aout, s_w1, s_w2 = shards
    B, S, _ = x.shape
    T = B * S
    tk = min(T, 4096)
    n_k = T // tk
    h0 = x.reshape(T, D_MODEL)
    tgt = target.reshape(T, D_MODEL)
    consts = _ret_constants(S)
    w1_cols = w2_rows = D_FF // N_DEV

    rel_cols = REL_TABLE // N_DEV
    per_shard = RET_IN // N_DEV // RET_DK

    def head_layout(r):
        return lambda ref, blk: ref.at[:, pl.ds(pl.multiple_of(
            _ret_piece_of_column_block(per_shard * blk + r) * RET_DK, LANES), RET_DK)]

    w_rin_heads, rel_all = _ChipGather(
        [s_rin[:, r * RET_DK:(r + 1) * RET_DK] for r in range(per_shard)] + [rel_shard],
        [jax.ShapeDtypeStruct((D_MODEL, RET_IN), BF16), jax.ShapeDtypeStruct((N_DEV, ATT_HEADS, rel_cols), F32)],
        [0] * per_shard + [1], [head_layout(r) for r in range(per_shard)] + [_slot]).run("gather_first")
    ring = _bias_ring(rel_all.transpose(1, 0, 2).reshape(ATT_HEADS, REL_TABLE))

    def gather(*items):
        return _ChipGather([s for s, _, _ in items], [jax.ShapeDtypeStruct(full, BF16) for _, _, full in items],
                           list(range(len(items))), [view for _, view, _ in items])

    def scatter(*items):
        return _Exchange([p for p, _, _ in items], [(a, view, a, _slot) for a, (_, view, _) in enumerate(items)],
                         [jax.ShapeDtypeStruct((N_DEV, *shard), BF16) for _, _, shard in items])

    w1_full, w2_full = (D_MODEL, D_FF), (D_FF, D_MODEL)
    w1_shard, w2_shard = (D_MODEL, w1_cols), (w2_rows, D_MODEL)

    ride = gather((s_rout, _rows(RET_VW // N_DEV), (RET_VW, D_MODEL)), (s_w1[0], _cols(w1_cols), w1_full))
    (proj, n0), (w_rout, w1_0) = _ret_proj(h0, mix_g[0:1], w_rin_heads, consts[0], consts[1], S, "ret_proj", 512, ride=ride)
    (y, o, states), (w2_0,) = _ret_fwd(proj, consts, gn_g, B, S, "ret_fwd", ride=gather((s_w2[0], _rows(w2_rows), w2_full)))
    h1 = _matmul_res(y, w_rout, h0, "ret_out")
    ride = gather((s_ain, _cols(3 * D_MODEL // N_DEV), (D_MODEL, 3 * D_MODEL)),
                  (s_aout, _rows(D_MODEL // N_DEV), (D_MODEL, D_MODEL)))
    (h2, n1, u1), (w_ain, w_aout) = _mlp_fwd(h1, mlp_g[0:1], w1_0, w2_0, "mlp0_fwd", ride=ride)
    qkv3, n2 = _norm_proj(h2, mix_g[1:2], w_ain, 3, "att_proj", 1024)
    ride = gather((s_w1[1], _cols(w1_cols), w1_full), (s_w2[1], _rows(w2_rows), w2_full))
    (o2,), (w1_1, w2_1) = _attn_fwd(qkv3, ring, B, S, "att_fwd", ride=ride)
    h3 = _matmul_res(o2, w_aout, h2, "att_out")
    h4, n3, u3 = _mlp_fwd(h3, mlp_g[1:2], w1_1, w2_1, "mlp1_fwd")
    dh4, dh4b, loss, dg_final = _final_loss(h4, final_g, tgt, "final_loss")

    def tok(width):
        return dict(a=pl.BlockSpec((tk, width), lambda i, j, k: (k, i)), b=pl.BlockSpec((tk, width), lambda i, j, k: (k, j)))

    def mlp_dw2(u, dhb, tag, ride=None):
        return _wgrad(u, tok(1024)["a"], dhb, tok(1024)["b"], D_FF, D_MODEL, 1024, 1024, n_k, tag + "_dw2",
                      square_a=True, ride=ride)

    def mlp_dw1(n, du, tag):
        return _wgrad(n, tok(1024)["a"], du, tok(1024)["b"], D_MODEL, D_FF, 1024, 1024, n_k, tag + "_dw1")

    dh3, dh3b, du3, dg_mlp1 = _mlp_bwd(dh4, u3, w1_1, w2_1, h3, mlp_g[1:2], "mlp1_bwd")
    gw2_1, gw1_1 = mlp_dw2(u3, dh4b, "mlp1"), mlp_dw1(n3, du3, "mlp1")
    do2 = _matmul_nt(dh3b, w_aout, "att_out_bwd")
    g_aout = _wgrad(o2, tok(1024)["a"], dh3b, tok(1024)["b"], D_MODEL, D_MODEL, 1024, 1024, n_k, "att_out_dw")
    ride = scatter((gw1_1, _cols(w1_cols), w1_shard), (gw2_1, _rows(w2_rows), w2_shard),
                   (g_aout, _rows(D_MODEL // N_DEV), (D_MODEL // N_DEV, D_MODEL)))
    (dqkv3, dring), (l_w1_1, l_w2_1, l_aout) = _attn_bwd(qkv3, ring, do2, B, S, "att_bwd", ride=ride)
    g_rel = _rel_bias_grad(dring, "rel_bias_grad")
    dh2, dh2b, dg_mix1 = _proj_bwd(dqkv3, w_ain, h2, mix_g[1:2], dh3, "att_proj_bwd", 1024)
    g_ain = _wgrad(n2, tok(1024)["a"], dqkv3, pl.BlockSpec((None, tk, D_MODEL), lambda i, j, k: (j, k, 0)),
                   D_MODEL, 3 * D_MODEL, 1024, 1024, n_k, "att_proj_dw")
    ride = scatter((g_ain, _cols(3 * D_MODEL // N_DEV), (D_MODEL, 3 * D_MODEL // N_DEV)))
    (dh1, dh1b, du1, dg_mlp0), (l_ain,) = _mlp_bwd(dh2, u1, w1_0, w2_0, h1, mlp_g[0:1], "mlp0_bwd", ride=ride)
    dy = _matmul_nt(dh1b, w_rout, "ret_out_bwd")
    g_rout = _wgrad(y, tok(1024)["a"], dh1b, tok(1024)["b"], RET_VW, D_MODEL, 1024, 1024, n_k, "ret_out_dw")
    gw2_0, (l_rout,) = mlp_dw2(u1, dh2b, "mlp0", ride=scatter((g_rout, _rows(RET_VW // N_DEV), (RET_VW // N_DEV, D_MODEL))))
    gw1_0 = mlp_dw1(n1, du1, "mlp0")
    ride = scatter((gw2_0, _rows(w2_rows), w2_shard))
    (dproj, dgn), (l_w2_0,) = _ret_bwd(proj, consts, gn_g, o, states, dy, B, S, "ret_bwd", ride=ride)
    g_rin, (l_w1_0,) = _wgrad(
        n0, tok(1024)["a"], dproj,
        [pl.BlockSpec((tk, RET_DK), functools.partial(
            lambda i, j, k, r: (k, _ret_piece_of_column_block(per_shard * j + r)), r=r)) for r in range(per_shard)],
        D_MODEL, RET_IN, 1024, per_shard * RET_DK, n_k, "ret_proj_dw", ride=scatter((gw1_0, _cols(w1_cols), w1_shard)))
    ride = scatter((g_rin, _cols(RET_IN // N_DEV), (D_MODEL, RET_IN // N_DEV)))
    (dx, _, dg_mix0), (l_rin,) = _proj_bwd(dproj, w_rin_heads, h0, mix_g[0:1], dh1, "ret_proj_bwd", 512, ride=ride)

    small = jnp.concatenate([dg_mix0, dg_mix1, dg_mlp0, dg_mlp1, dg_final, dgn.reshape(2, D_MODEL),
                             jnp.broadcast_to(loss[0:1, 0:1], (1, D_MODEL))], axis=0)
    l_small, l_rel = _Exchange(
        [small, g_rel], [(0, _whole, 0, _slot), (1, _whole, 1, _slot)],
        [jax.ShapeDtypeStruct((N_DEV, 8, D_MODEL), F32), jax.ShapeDtypeStruct((N_DEV, ATT_HEADS, REL_PAD), F32)],
    ).run("scatter_small")
    lands = dict(rin=l_rin, rout=l_rout, ain=l_ain, aout=l_aout, w1=(l_w1_0, l_w1_1), w2=(l_w2_0, l_w2_1),
                 small=l_small, rel=l_rel)
    return dx.reshape(B, S, D_MODEL), lands


def kernel(x, mix_norm_g, ret_w_in, ret_gn_g, ret_w_out, att_w_in, att_rel_bias, att_w_out, mlp_norm_g, mlp_w1, mlp_w2, final_norm_g, loss_target, m_mix_norm_g, m_ret_w_in, m_ret_gn_g, m_ret_w_out, m_att_w_in, m_att_rel_bias, m_att_w_out, m_mlp_norm_g, m_mlp_w1, m_mlp_w2, m_final_norm_g, v_mix_norm_g, v_ret_w_in, v_ret_gn_g, v_ret_w_out, v_att_w_in, v_att_rel_bias, v_att_w_out, v_mlp_norm_g, v_mlp_w1, v_mlp_w2, v_final_norm_g):
    me = _lin(_place())
    rel_cols = REL_TABLE // N_DEV
    shards = (ret_w_in[0].astype(BF16), ret_w_out[0].astype(BF16), att_w_in[0].astype(BF16), att_w_out[0].astype(BF16),
              (mlp_w1[0].astype(BF16), mlp_w1[1].astype(BF16)), (mlp_w2[0].astype(BF16), mlp_w2[1].astype(BF16)))
    grad_x, lands = _step(x, loss_target, shards, mix_norm_g, ret_gn_g, att_rel_bias[0], mlp_norm_g,
                          final_norm_g.reshape(1, D_MODEL))

    u_rin = _adamw_reduce([lands["rin"]], ret_w_in, m_ret_w_in, v_ret_w_in, "update_ret_w_in")
    u_rout = _adamw_reduce([lands["rout"]], ret_w_out, m_ret_w_out, v_ret_w_out, "update_ret_w_out")
    u_ain = _adamw_reduce([lands["ain"]], att_w_in, m_att_w_in, v_att_w_in, "update_att_w_in")
    u_aout = _adamw_reduce([lands["aout"]], att_w_out, m_att_w_out, v_att_w_out, "update_att_w_out")
    u_w1 = _adamw_reduce(lands["w1"], mlp_w1, m_mlp_w1, v_mlp_w1, "update_mlp_w1")
    u_w2 = _adamw_reduce(lands["w2"], mlp_w2, m_mlp_w2, v_mlp_w2, "update_mlp_w2")

    def pack(mix, mlp, fin, gn):
        return jnp.concatenate([mix, mlp, fin.reshape(1, D_MODEL), gn.reshape(2, D_MODEL), jnp.zeros((1, D_MODEL), F32)], axis=0)

    small_w = pack(mix_norm_g, mlp_norm_g, final_norm_g, ret_gn_g)
    small_m = pack(m_mix_norm_g, m_mlp_norm_g, m_final_norm_g, m_ret_gn_g)
    small_v = pack(v_mix_norm_g, v_mlp_norm_g, v_final_norm_g, v_ret_gn_g)
    sg, sd, sm, sv, rel_sum = _small_update(lands["small"], lands["rel"], small_w, small_m, small_v, "update_small")
    g_rel_mine = lax.dynamic_slice(rel_sum, (0, me * rel_cols), (ATT_HEADS, rel_cols))
    rel_d, rel_m, rel_v = _adamw_plain(att_rel_bias[0], g_rel_mine, m_att_rel_bias[0], v_att_rel_bias[0], "update_rel_bias")
    u_rel = [g_rel_mine[None], rel_d[None], rel_m[None], rel_v[None]]

    def unpack(t):
        return dict(mix=t[0:2], mlp=t[2:4], fin=t[4], gn=t[5:7].reshape(1, RET_VW))

    us = [unpack(t) for t in (sg, sd, sm, sv)]
    outs = [sg[7, 0], grad_x]
    for k in range(4):
        outs += [us[k]["mix"], u_rin[k], us[k]["gn"], u_rout[k], u_ain[k], u_rel[k], u_aout[k], us[k]["mlp"],
                 u_w1[k], u_w2[k], us[k]["fin"]]
    return tuple(outs)
```

```python
import functools

import numpy as np
import jax
import jax.numpy as jnp
from jax import lax
from jax.experimental import pallas as pl
from jax.experimental.pallas import tpu as pltpu

F32, BF16 = jnp.float32, jnp.bfloat16

D_MODEL = 1024
CHUNK = 64
RET_HEADS, RET_DK, RET_DV = 4, 256, 512
RET_QK, RET_VW = RET_HEADS * RET_DK, RET_HEADS * RET_DV
RET_IN = 2 * RET_QK + 2 * RET_VW
RET_HEAD_COLS = 2 * RET_DK + 2 * RET_DV
RET_SC = 256
ROPE_BASE = 10000.0
ATT_HEADS, ATT_DH = 16, 64
LANES = 128
ATT_PAIRS = ATT_HEADS * ATT_DH // LANES
BAND_PAST = 8 * CHUNK
MAX_REL = 256
REL_TABLE = MAX_REL + CHUNK
REL_PAD = 384
ATT_CQ = 256
ATT_KW = ATT_CQ + BAND_PAST
ATT_RING = ATT_CQ + ATT_KW
ATT_VARIANTS = BAND_PAST // ATT_CQ + 1
D_FF = 4 * D_MODEL
EPS = 1e-6
EPILOGUE_ROWS = 256
NEG = -1e30
N_DEV = 8
N_PEERS = N_DEV - 1

ADAM_LR, ADAM_B1, ADAM_B2, ADAM_EPS, ADAM_WD, ADAM_STEP = 0.001, 0.9, 0.999, 1e-08, 0.01, 10

VMEM_LIMIT = 56 * 1024 * 1024
MESH = pl.DeviceIdType.MESH
ANY = pl.BlockSpec(memory_space=pl.ANY)


def _dot(a, b):
    return jnp.dot(a, b, preferred_element_type=F32)


def _dot_nt(a, b):
    return lax.dot_general(a, b, (((1,), (1,)), ((), ())), preferred_element_type=F32)


def _dot_tn(a, b):
    return lax.dot_general(a, b, (((0,), (0,)), ((), ())), preferred_element_type=F32)


def _rms(x, g):
    r = lax.rsqrt(jnp.mean(x * x, axis=-1, keepdims=True) + EPS)
    return x * r * g


def _rms_bwd(dn, x, g):
    r = lax.rsqrt(jnp.mean(x * x, axis=-1, keepdims=True) + EPS)
    xh = x * r
    dg = jnp.sum(dn * xh, axis=0, keepdims=True)
    dxh = dn * g
    dx = r * (dxh - xh * jnp.mean(dxh * xh, axis=-1, keepdims=True))
    return dx, dg


def _rms_bwd_epilogue(x_ref, g_ref, dres_ref, dx_ref, dxb_ref, dg_ref):
    for r in range(0, dx_ref.shape[0], EPILOGUE_ROWS):
        rows = slice(r, r + EPILOGUE_ROWS)
        dx, dg = _rms_bwd(dx_ref[rows, :], x_ref[rows, :], g_ref[...])
        dx = dres_ref[rows, :] + dx
        dx_ref[rows, :] = dx
        dxb_ref[rows, :] = dx.astype(BF16)
        dg_ref[...] += dg


def _place():
    return lax.axis_index("x"), lax.axis_index("y"), lax.axis_index("c")


def _lin(p):
    return 4 * p[0] + 2 * p[1] + p[2]


def _cols(width):
    return lambda ref, blk: ref.at[:, pl.ds(pl.multiple_of(blk * width, LANES), width)]


def _rows(height):
    return lambda ref, blk: ref.at[pl.ds(pl.multiple_of(blk * height, 8), height), :]


def _slot(ref, blk):
    return ref.at[blk]


def _whole(ref, blk):
    return ref


class _Exchange:
    def __init__(self, sources, flows, land_shapes):
        self.sources, self.flows, self.land_shapes = list(sources), flows, list(land_shapes)
        n_f = len(flows)
        self.sem_shapes = [pltpu.SemaphoreType.DMA((N_PEERS * n_f,)), pltpu.SemaphoreType.DMA((N_PEERS * n_f,)),
                           pltpu.SemaphoreType.DMA((n_f,))]

    def _copies(self, srcs, lands, sems):
        send_sems, recv_sems, local_sems = sems
        x, y, c = _place()
        me = (x, y, c)
        peers = [(x ^ dx, y ^ dy, c ^ dc) for dx in (0, 1) for dy in (0, 1) for dc in (0, 1)][1:]

        def copy(f, k, sender, to):
            si, sview, li, lview = self.flows[f]
            return pltpu.make_async_remote_copy(
                src_ref=sview(srcs[si], _lin(to)), dst_ref=lview(lands[li], _lin(sender)),
                send_sem=send_sems.at[f * N_PEERS + k], recv_sem=recv_sems.at[f * N_PEERS + k],
                device_id=to, device_id_type=MESH)

        mine, sends, recvs = [], [], []
        for f, (si, sview, li, lview) in enumerate(self.flows):
            mine.append(pltpu.make_async_copy(sview(srcs[si], _lin(me)), lview(lands[li], _lin(me)), local_sems.at[f]))
            for k, peer in enumerate(peers):
                sends.append(copy(f, k, me, peer))
                recvs.append(copy(f, k, peer, me))
        return mine, sends, recvs

    def start(self, srcs, lands, sems):
        mine, sends, _ = self._copies(srcs, lands, sems)
        for cp in mine + sends:
            cp.start()

    def finish(self, srcs, lands, sems):
        mine, sends, recvs = self._copies(srcs, lands, sems)
        for cp in recvs:
            cp.wait_recv()
        for cp in sends:
            cp.wait_send()
        for cp in mine:
            cp.wait()

    def run(self, name):
        n_src, n_land = len(self.sources), len(self.land_shapes)

        def body(*refs):
            srcs, lands, sems = refs[:n_src], refs[n_src:n_src + n_land], refs[n_src + n_land:]
            self.start(srcs, lands, sems)
            self.finish(srcs, lands, sems)

        return pl.pallas_call(body, name=name, in_specs=[ANY] * n_src, out_specs=[ANY] * n_land,
                              out_shape=self.land_shapes, scratch_shapes=self.sem_shapes)(*self.sources)


def _call(body, *, name, grid, in_specs, out_specs, out_shape, args, scratch_shapes=(), ride=None):
    params = pltpu.CompilerParams(dimension_semantics=("arbitrary",) * len(grid), vmem_limit_bytes=VMEM_LIMIT)
    in_specs, out_specs, out_shape, scratch_shapes = list(in_specs), list(out_specs), list(out_shape), list(scratch_shapes)
    if ride is None:
        return pl.pallas_call(body, name=name, grid=grid, in_specs=in_specs, out_specs=out_specs, out_shape=out_shape,
                              scratch_shapes=scratch_shapes, compiler_params=params)(*args)
    n_in, n_out, n_scr = len(in_specs), len(out_specs), len(scratch_shapes)
    n_src, n_land = len(ride.sources), len(ride.land_shapes)

    def riding(*refs):
        bounds = np.cumsum([n_in, n_src, n_out, n_land, n_scr])
        ins, srcs, outs, lands, scr, sems = (refs[a:b] for a, b in zip([0, *bounds], [*bounds, len(refs)]))
        step = functools.reduce(lambda a, d: a * grid[d] + pl.program_id(d), range(len(grid)), 0)
        n_steps = int(np.prod(grid))

        @pl.when(step == 0)
        def _():
            ride.start(srcs, lands, sems)

        if hasattr(ride, "middle"):
            @pl.when(step == (5 * n_steps) // 8)
            def _():
                ride.middle(srcs, lands, sems)

        body(*ins, *outs, *scr)

        @pl.when(step == n_steps - 1)
        def _():
            ride.finish(srcs, lands, sems)

    res = pl.pallas_call(
        riding, name=name, grid=grid, in_specs=in_specs + [ANY] * n_src, out_specs=out_specs + [ANY] * n_land,
        out_shape=out_shape + ride.land_shapes, scratch_shapes=scratch_shapes + ride.sem_shapes,
        compiler_params=params)(*args, *ride.sources)
    return res[:n_out], res[n_out:]


class _ChipGather:
    def __init__(self, shards, out_shapes, out_of, views):
        self.sources, self.land_shapes, self.out_of, self.views = list(shards), list(out_shapes), out_of, views
        n = len(shards)
        self.sem_shapes = [pltpu.SemaphoreType.DMA((N_PEERS * n,)), pltpu.SemaphoreType.DMA((N_PEERS * n,)),
                           pltpu.SemaphoreType.DMA((n,))]

    def _copies(self, srcs, lands, sems):
        send_sems, recv_sems, local_sems = sems
        n = len(srcs)
        outs = [lands[self.out_of[a]] for a in range(n)]
        x, y, c = _place()
        me, sibling = (x, y, c), (x, y, 1 - c)
        chips = [(1 - x, y), (x, 1 - y), (1 - x, 1 - y)]

        def copy(a, k, block, to, own=False):
            dst = self.views[a](outs[a], _lin(block))
            return pltpu.make_async_remote_copy(
                src_ref=srcs[a] if own else dst, dst_ref=dst,
                send_sem=send_sems.at[a * N_PEERS + k], recv_sem=recv_sems.at[a * N_PEERS + k],
                device_id=to, device_id_type=MESH)

        mine = lambda: [pltpu.make_async_copy(srcs[a], self.views[a](outs[a], _lin(me)), local_sems.at[a]) for a in range(n)]
        first = lambda: [cp for a in range(n) for cp in
                         [copy(a, 0, me, sibling, own=True)] + [copy(a, 1 + j, me, (*chip, c), own=True) for j, chip in enumerate(chips)]]
        landed = lambda a, j: copy(a, 1 + j, (*chips[j], c), me)
        passed = lambda a, j: copy(a, 4 + j, (*chips[j], c), sibling)
        from_sibling = lambda a: [copy(a, 0, sibling, me)] + [copy(a, 4 + j, (*chips[j], 1 - c), me) for j in range(3)]
        return n, mine, first, landed, passed, from_sibling

    def start(self, srcs, lands, sems):
        _, mine, first, _, _, _ = self._copies(srcs, lands, sems)
        for cp in mine() + first():
            cp.start()

    def middle(self, srcs, lands, sems):
        n, _, _, landed, passed, _ = self._copies(srcs, lands, sems)
        for j in range(3):
            for a in range(n):
                landed(a, j).wait_recv()
                passed(a, j).start()

    def finish(self, srcs, lands, sems):
        n, mine, first, _, passed, from_sibling = self._copies(srcs, lands, sems)
        for a in range(n):
            for cp in from_sibling(a):
                cp.wait_recv()
        for cp in first() + [passed(a, j) for j in range(3) for a in range(n)]:
            cp.wait_send()
        for cp in mine():
            cp.wait()

    def run(self, name):
        n, n_out = len(self.sources), len(self.land_shapes)

        def body(*refs):
            parts = refs[:n], refs[n:n + n_out], refs[n + n_out:]
            self.start(*parts)
            self.middle(*parts)
            self.finish(*parts)

        return pl.pallas_call(body, name=name, in_specs=[ANY] * n, out_specs=[ANY] * n_out,
                              out_shape=self.land_shapes, scratch_shapes=self.sem_shapes)(*self.sources)


def _resident(shape):
    return pl.BlockSpec(shape, lambda i: (0,) * len(shape), pipeline_mode=pl.Buffered(1))


def _norm_proj(h, g, w, groups, name, tm, ride=None):
    T = h.shape[0]
    N = w.shape[1]
    width = N // groups

    def body(h_ref, g_ref, w_ref, o_ref, n_ref):
        n = _rms(h_ref[...], g_ref[...]).astype(BF16)
        n_ref[...] = n
        if groups == 1:
            o_ref[...] = _dot(n, w_ref[...]).astype(BF16)
        else:
            for p in range(groups):
                o_ref[p] = _dot(n, w_ref[:, p * width:(p + 1) * width]).astype(BF16)

    row = pl.BlockSpec((tm, D_MODEL), lambda i: (i, 0))
    if groups == 1:
        out_spec, out_shape = pl.BlockSpec((tm, N), lambda i: (i, 0)), jax.ShapeDtypeStruct((T, N), BF16)
    else:
        out_spec = pl.BlockSpec((groups, tm, width), lambda i: (0, i, 0))
        out_shape = jax.ShapeDtypeStruct((groups, T, width), BF16)
    return _call(body, name=name, grid=(T // tm,),
                 in_specs=[row, pl.BlockSpec((1, D_MODEL), lambda i: (0, 0)), _resident(w.shape)],
                 out_specs=[out_spec, row], out_shape=[out_shape, jax.ShapeDtypeStruct((T, D_MODEL), BF16)],
                 args=(h, g, w), ride=ride)


def _ret_proj(h, g, w, cos, sin, S, name, tm, ride=None):
    T = h.shape[0]
    per_seq = S // tm

    def body(h_ref, g_ref, w_ref, cos_ref, sin_ref, o_ref, n_ref):
        n = _rms(h_ref[...], g_ref[...]).astype(BF16)
        n_ref[...] = n
        cs, sn = cos_ref[...], sin_ref[...]
        for hd in range(RET_HEADS):
            c0 = hd * RET_HEAD_COLS
            out = _dot(n, w_ref[:, c0:c0 + RET_HEAD_COLS])
            o_ref[:, c0:c0 + RET_DK] = _rope(out[:, 0:RET_DK], cs, sn).astype(BF16)
            o_ref[:, c0 + RET_DK:c0 + 2 * RET_DK] = (_rope(out[:, RET_DK:2 * RET_DK], cs, sn) * (RET_DK ** -0.5)).astype(BF16)
            o_ref[:, c0 + 2 * RET_DK:c0 + RET_HEAD_COLS] = out[:, 2 * RET_DK:].astype(BF16)

    row = pl.BlockSpec((tm, D_MODEL), lambda i: (i, 0))
    trig = pl.BlockSpec((tm, RET_DK // 2), lambda i: (i % per_seq, 0))
    return _call(body, name=name, grid=(T // tm,),
                 in_specs=[row, pl.BlockSpec((1, D_MODEL), lambda i: (0, 0)), _resident(w.shape), trig, trig],
                 out_specs=[pl.BlockSpec((tm, w.shape[1]), lambda i: (i, 0)), row],
                 out_shape=[jax.ShapeDtypeStruct((T, w.shape[1]), BF16), jax.ShapeDtypeStruct((T, D_MODEL), BF16)],
                 args=(h, g, w, cos, sin), ride=ride)


def _proj_bwd(dy, w, x, g, dres, name, tm, ride=None):
    T = x.shape[0]
    groups = dy.shape[0] if dy.ndim == 3 else 1
    width = w.shape[1] // groups

    def body(dy_ref, w_ref, x_ref, g_ref, dres_ref, dx_ref, dxb_ref, dg_ref):
        @pl.when(pl.program_id(0) == 0)
        def _():
            dg_ref[...] = jnp.zeros_like(dg_ref)

        if groups == 1:
            dn = _dot_nt(dy_ref[...], w_ref[...])
        else:
            dn = sum(_dot_nt(dy_ref[p], w_ref[:, p * width:(p + 1) * width]) for p in range(groups))
        dx_ref[...] = dn
        _rms_bwd_epilogue(x_ref, g_ref, dres_ref, dx_ref, dxb_ref, dg_ref)

    row = pl.BlockSpec((tm, D_MODEL), lambda i: (i, 0))
    vec = pl.BlockSpec((1, D_MODEL), lambda i: (0, 0))
    dy_spec = (pl.BlockSpec((tm, w.shape[1]), lambda i: (i, 0)) if groups == 1
               else pl.BlockSpec((groups, tm, width), lambda i: (0, i, 0)))
    return _call(body, name=name, grid=(T // tm,),
                 in_specs=[dy_spec, _resident(w.shape), row, vec, row], out_specs=[row, row, vec],
                 out_shape=[jax.ShapeDtypeStruct((T, D_MODEL), F32), jax.ShapeDtypeStruct((T, D_MODEL), BF16),
                            jax.ShapeDtypeStruct((1, D_MODEL), F32)],
                 args=(dy, w, x, g, dres), ride=ride)


def _matmul_res(a, w, res, name, tm=1024):
    T, K = a.shape

    def body(a_ref, w_ref, r_ref, o_ref):
        o_ref[...] = r_ref[...] + _dot(a_ref[...], w_ref[...])

    row = pl.BlockSpec((tm, D_MODEL), lambda i: (i, 0))
    return _call(body, name=name, grid=(T // tm,),
                 in_specs=[pl.BlockSpec((tm, K), lambda i: (i, 0)), _resident(w.shape), row],
                 out_specs=[row], out_shape=[jax.ShapeDtypeStruct((T, D_MODEL), F32)], args=(a, w, res))[0]


def _matmul_nt(dy, w, name, tm=1024):
    T, N = dy.shape
    K = w.shape[0]

    def body(dy_ref, w_ref, o_ref):
        o_ref[...] = _dot_nt(dy_ref[...], w_ref[...]).astype(BF16)

    return _call(body, name=name, grid=(T // tm,),
                 in_specs=[pl.BlockSpec((tm, N), lambda i: (i, 0)), _resident(w.shape)],
                 out_specs=[pl.BlockSpec((tm, K), lambda i: (i, 0))], out_shape=[jax.ShapeDtypeStruct((T, K), BF16)],
                 args=(dy, w))[0]


def _wgrad(a, a_spec, b, b_specs, m, n, bm, bn, n_k, name, square_a=False, ride=None):
    b_specs = b_specs if isinstance(b_specs, (list, tuple)) else [b_specs]
    n_b = len(b_specs)

    def body(a_ref, *rest):
        b_refs = rest[:n_b]
        o_ref, acc = rest[n_b:]
        k = pl.program_id(2)

        @pl.when(k == 0)
        def _():
            acc[...] = jnp.zeros_like(acc)

        av = a_ref[...]
        if square_a:
            af = av.astype(F32)
            av = (af * af).astype(BF16)
        bv = b_refs[0][...] if n_b == 1 else jnp.concatenate([r[...] for r in b_refs], axis=1)
        acc[...] += _dot_tn(av, bv)

        @pl.when(k == n_k - 1)
        def _():
            o_ref[...] = acc[...].astype(BF16)

    res = _call(body, name=name, grid=(m // bm, n // bn, n_k), in_specs=[a_spec, *b_specs],
                out_specs=[pl.BlockSpec((bm, bn), lambda i, j, k: (i, j))], out_shape=[jax.ShapeDtypeStruct((m, n), BF16)],
                scratch_shapes=[pltpu.VMEM((bm, bn), F32)], args=(a, *([b] * n_b)), ride=ride)
    return res[0] if ride is None else (res[0][0], res[1])


def _mlp_fwd(h, g, w1, w2, name, tm=1024, tf=1024, ride=None):
    T = h.shape[0]

    def body(h_ref, g_ref, w1_ref, w2_ref, ho_ref, n_ref, u_ref):
        @pl.when(pl.program_id(1) == 0)
        def _():
            n_ref[...] = _rms(h_ref[...], g_ref[...]).astype(BF16)
            ho_ref[...] = h_ref[...]

        u = jnp.maximum(_dot(n_ref[...], w1_ref[...]), 0.0)
        u_ref[...] = u.astype(BF16)
        ho_ref[...] += _dot((u * u).astype(BF16), w2_ref[...])

    row = pl.BlockSpec((tm, D_MODEL), lambda i, j: (i, 0))
    return _call(body, name=name, grid=(T // tm, D_FF // tf),
                 in_specs=[row, pl.BlockSpec((1, D_MODEL), lambda i, j: (0, 0)),
                           pl.BlockSpec((D_MODEL, tf), lambda i, j: (0, j)), pl.BlockSpec((tf, D_MODEL), lambda i, j: (j, 0))],
                 out_specs=[row, row, pl.BlockSpec((tm, tf), lambda i, j: (i, j))],
                 out_shape=[jax.ShapeDtypeStruct((T, D_MODEL), F32), jax.ShapeDtypeStruct((T, D_MODEL), BF16),
                            jax.ShapeDtypeStruct((T, D_FF), BF16)],
                 args=(h, g, w1, w2), ride=ride)


def _mlp_bwd(dh, u, w1, w2, h, g, name, tm=1024, tf=1024, ride=None):
    T = h.shape[0]
    n_j = D_FF // tf

    def body(dh_ref, u_ref, w1_ref, w2_ref, h_ref, g_ref, dx_ref, dxb_ref, du_ref, dg_ref, dhb):
        acc = dx_ref
        i, j = pl.program_id(0), pl.program_id(1)

        @pl.when(j == 0)
        def _():
            dhb[...] = dh_ref[...].astype(BF16)
            acc[...] = jnp.zeros_like(acc)

        @pl.when((i == 0) & (j == 0))
        def _():
            dg_ref[...] = jnp.zeros_like(dg_ref)

        da = _dot_nt(dhb[...], w2_ref[...])
        du = (da * (2.0 * u_ref[...].astype(F32))).astype(BF16)
        du_ref[...] = du
        acc[...] += _dot_nt(du, w1_ref[...])

        @pl.when(j == n_j - 1)
        def _():
            _rms_bwd_epilogue(h_ref, g_ref, dh_ref, dx_ref, dxb_ref, dg_ref)

    row = pl.BlockSpec((tm, D_MODEL), lambda i, j: (i, 0))
    vec = pl.BlockSpec((1, D_MODEL), lambda i, j: (0, 0))
    hid = pl.BlockSpec((tm, tf), lambda i, j: (i, j))
    return _call(body, name=name, grid=(T // tm, n_j),
                 in_specs=[row, hid, pl.BlockSpec((D_MODEL, tf), lambda i, j: (0, j)),
                           pl.BlockSpec((tf, D_MODEL), lambda i, j: (j, 0)), row, vec],
                 out_specs=[row, row, hid, vec],
                 out_shape=[jax.ShapeDtypeStruct((T, D_MODEL), F32), jax.ShapeDtypeStruct((T, D_MODEL), BF16),
                            jax.ShapeDtypeStruct((T, D_FF), BF16), jax.ShapeDtypeStruct((1, D_MODEL), F32)],
                 scratch_shapes=[pltpu.VMEM((tm, D_MODEL), BF16)], args=(dh, u, w1, w2, h, g), ride=ride)


def _final_loss(h, g, target, name, tm=512):
    T = h.shape[0]

    def body(h_ref, g_ref, t_ref, dh_ref, dhb_ref, loss_ref, dg_ref):
        @pl.when(pl.program_id(0) == 0)
        def _():
            loss_ref[...] = jnp.zeros_like(loss_ref)
            dg_ref[...] = jnp.zeros_like(dg_ref)

        x = h_ref[...]
        gg = g_ref[...]
        r = lax.rsqrt(jnp.mean(x * x, axis=-1, keepdims=True) + EPS)
        xh = x * r
        e = xh * gg - t_ref[...]
        per_tok = jnp.mean(e * e, axis=-1, keepdims=True)
        loss_ref[...] += 0.5 * jnp.sum(per_tok, axis=0, keepdims=True)
        dy = e * (1.0 / D_MODEL)
        dg_ref[...] += jnp.sum(dy * xh, axis=0, keepdims=True)
        dxh = dy * gg
        dx = r * (dxh - xh * jnp.mean(dxh * xh, axis=-1, keepdims=True))
        dh_ref[...] = dx
        dhb_ref[...] = dx.astype(BF16)

    row = pl.BlockSpec((tm, D_MODEL), lambda i: (i, 0))
    vec = pl.BlockSpec((1, D_MODEL), lambda i: (0, 0))
    return _call(body, name=name, grid=(T // tm,), in_specs=[row, vec, row],
                 out_specs=[row, row, pl.BlockSpec((8, LANES), lambda i: (0, 0)), vec],
                 out_shape=[jax.ShapeDtypeStruct((T, D_MODEL), F32), jax.ShapeDtypeStruct((T, D_MODEL), BF16),
                            jax.ShapeDtypeStruct((8, LANES), F32), jax.ShapeDtypeStruct((1, D_MODEL), F32)],
                 args=(h, g, target))


def _ret_constants(S):
    log_gamma = jnp.log1p(-jnp.exp2(-5.0 - jnp.arange(RET_HEADS, dtype=F32)))
    idx = jnp.arange(RET_SC, dtype=F32)
    i, j = idx[:, None], idx[None, :]
    same_chunk = jnp.floor(i / CHUNK) == jnp.floor(j / CHUNK)
    mask = jnp.where((j <= i) | same_chunk, jnp.exp(log_gamma[:, None, None] * jnp.abs(i - j)[None]), 0.0)
    qdec = jnp.exp(log_gamma[:, None] * (idx + 1.0)[None, :])[:, :, None]
    kdec = jnp.exp(log_gamma[:, None] * (RET_SC - 1 - idx)[None, :])[:, :, None]
    cdec = jnp.exp(log_gamma * RET_SC)[:, None, None]
    half = RET_DK // 2
    inv = jnp.exp(-jnp.log(ROPE_BASE) * jnp.arange(half, dtype=F32) / half)
    ang = jnp.arange(S, dtype=F32)[:, None] * inv[None, :]
    return jnp.cos(ang), jnp.sin(ang), mask.astype(F32), qdec, kdec, cdec


def _rope(t, cs, sn):
    t1, t2 = t[:, :RET_DK // 2], t[:, RET_DK // 2:]
    return jnp.concatenate([t1 * cs - t2 * sn, t1 * sn + t2 * cs], axis=-1)


def _rope_bwd(d, cs, sn):
    d1, d2 = d[:, :RET_DK // 2], d[:, RET_DK // 2:]
    return jnp.concatenate([d1 * cs + d2 * sn, d2 * cs - d1 * sn], axis=-1)


def _ret_specs(B, S, reverse):
    n_sc = S // RET_SC

    def cc(c):
        return n_sc - 1 - c if reverse else c

    return dict(
        proj=pl.BlockSpec((B, RET_SC, RET_HEAD_COLS), lambda h, c: (0, cc(c), h)),
        trig=pl.BlockSpec((RET_SC, RET_DK // 2), lambda h, c: (cc(c), 0)),
        mask=pl.BlockSpec((None, RET_SC, RET_SC), lambda h, c: (h, 0, 0)),
        dec=pl.BlockSpec((None, RET_SC, 1), lambda h, c: (h, 0, 0)),
        cdec=pl.BlockSpec((None, 1, 1), lambda h, c: (h, 0, 0)),
        gn=pl.BlockSpec((1, RET_DV), lambda h, c: (0, h)),
        val=pl.BlockSpec((B, RET_SC, RET_DV), lambda h, c: (0, cc(c), h)),
        state=pl.BlockSpec((B, None, None, RET_DK, RET_DV), lambda h, c: (0, h, cc(c), 0, 0)),
    )


def _ret_qkvg(p_ref, kdec):
    qb = p_ref[:, 0:RET_DK]
    kb = p_ref[:, RET_DK:2 * RET_DK]
    kt = (kb.astype(F32) * kdec).astype(BF16)
    v = p_ref[:, 2 * RET_DK:2 * RET_DK + RET_DV]
    gate = p_ref[:, 2 * RET_DK + RET_DV:RET_HEAD_COLS].astype(F32)
    return qb, kb, kt, v, gate


def _group_norm(o):
    mu = jnp.mean(o, axis=-1, keepdims=True)
    oc = o - mu
    rstd = lax.rsqrt(jnp.mean(oc * oc, axis=-1, keepdims=True) + EPS)
    return oc * rstd, rstd


def _ret_fwd(proj, consts, gn, B, S, name, ride=None):
    T = B * S
    n_sc = S // RET_SC
    sp = _ret_specs(B, S, False)

    def body(p_ref, m_ref, qd_ref, kd_ref, cd_ref, gn_ref, y_ref, o_ref, st_ref, state):
        @pl.when(pl.program_id(1) == 0)
        def _():
            state[...] = jnp.zeros_like(state)

        seqs = range(B)
        qkvg = [_ret_qkvg(p_ref.at[b], kd_ref[...]) for b in seqs]
        qb, kb, kt, v = ([qkvg[b][i] for b in seqs] for i in range(4))
        sc = [_dot_nt(qb[b], kb[b]) for b in seqs]
        sb = [state[b].astype(BF16) for b in seqs]
        cross = [_dot(qb[b], sb[b]) for b in seqs]
        for b in seqs:
            st_ref[b] = sb[b]
        p = [(sc[b] * m_ref[...]).astype(BF16) for b in seqs]
        o = [_dot(p[b], v[b]) + cross[b] * qd_ref[...] for b in seqs]
        upd = [_dot_tn(kt[b], v[b]) for b in seqs]
        for b in seqs:
            o_ref[b] = o[b].astype(BF16)
            ohat, _ = _group_norm(o[b])
            gate = qkvg[b][4]
            y_ref[b] =(gate * jax.nn.sigmoid(gate) * (ohat * gn_ref[...])).astype(BF16)
            state[b] = state[b] * cd_ref[...] + upd[b]

    res = _call(
        body, name=name, grid=(RET_HEADS, n_sc),
        in_specs=[sp["proj"], sp["mask"], sp["dec"], sp["dec"], sp["cdec"], sp["gn"]],
        out_specs=[sp["val"], sp["val"], sp["state"]],
        out_shape=[jax.ShapeDtypeStruct((B, S, RET_VW), BF16), jax.ShapeDtypeStruct((B, S, RET_VW), BF16),
                   jax.ShapeDtypeStruct((B, RET_HEADS, n_sc, RET_DK, RET_DV), BF16)],
        scratch_shapes=[pltpu.VMEM((B, RET_DK, RET_DV), F32)],
        args=(proj.reshape(B, S, -1), *consts[2:], gn), ride=ride)
    (y, o, states), lands = res if ride is not None else (res, None)
    outs = [y.reshape(T, RET_VW), o.reshape(T, RET_VW), states]
    return outs if ride is None else (outs, lands)


def _ret_bwd(proj, consts, gn, o, states, dy, B, S, name, ride=None):
    T = B * S
    n_sc = S // RET_SC
    sp = _ret_specs(B, S, True)

    def body(p_ref, cos_ref, sin_ref, m_ref, qd_ref, kd_ref, cd_ref, gn_ref, o_ref, st_ref, dy_ref,
             dp_ref, dgn_ref, dstate):
        @pl.when(pl.program_id(1) == 0)
        def _():
            dstate[...] = jnp.zeros_like(dstate)
            dgn_ref[...] = jnp.zeros_like(dgn_ref)

        seqs = range(B)
        cs, sn = cos_ref[...], sin_ref[...]
        m, gnv = m_ref[...], gn_ref[...]
        qkvg = [_ret_qkvg(p_ref.at[b], kd_ref[...]) for b in seqs]
        qb, kb, kt, v = ([qkvg[b][i] for b in seqs] for i in range(4))
        sc = [_dot_nt(qb[b], kb[b]) for b in seqs]
        dsb = [dstate[b].astype(BF16) for b in seqs]
        dv_state = [_dot(kt[b], dsb[b]) for b in seqs]
        dkt = [_dot_nt(v[b], dsb[b]) for b in seqs]
        do, dgate = [], []
        for b in seqs:
            gate = qkvg[b][4]
            ohat, rstd = _group_norm(o_ref[b].astype(F32))
            dyv = dy_ref[b].astype(F32)
            sg = jax.nn.sigmoid(gate)
            don = dyv * (gate * sg)
            dgate.append(dyv * (ohat * gnv) * (sg * (1.0 + gate * (1.0 - sg))))
            dgn_ref[...] += jnp.sum(don * ohat, axis=0, keepdims=True)
            dohat = don * gnv
            do.append(rstd * (dohat - jnp.mean(dohat, axis=-1, keepdims=True)
                              - ohat * jnp.mean(dohat * ohat, axis=-1, keepdims=True)))
        dob = [do[b].astype(BF16) for b in seqs]
        doq = [(do[b] * qd_ref[...]).astype(BF16) for b in seqs]
        dsc_f = [_dot_nt(dob[b], v[b]) for b in seqs]
        dq_state = [_dot_nt(doq[b], st_ref[b]) for b in seqs]
        dstate_upd = [_dot_tn(qb[b], doq[b]) for b in seqs]
        p = [(sc[b] * m).astype(BF16) for b in seqs]
        dsc = [(dsc_f[b] * m).astype(BF16) for b in seqs]
        dv = [_dot_tn(p[b], dob[b]) + dv_state[b] for b in seqs]
        dq = [_dot(dsc[b], kb[b]) + dq_state[b] for b in seqs]
        dk = [(_dot_tn(dsc[b], qb[b]) + dkt[b] * kd_ref[...]) * (RET_DK ** -0.5) for b in seqs]
        for b in seqs:
            dstate[b] = dstate[b] * cd_ref[...] + dstate_upd[b]
            dp_ref[b, :, 0:RET_DK] = _rope_bwd(dq[b], cs, sn).astype(BF16)
            dp_ref[b, :, RET_DK:2 * RET_DK] = _rope_bwd(dk[b], cs, sn).astype(BF16)
            dp_ref[b, :, 2 * RET_DK:2 * RET_DK + RET_DV] = dv[b].astype(BF16)
            dp_ref[b, :, 2 * RET_DK + RET_DV:RET_HEAD_COLS] = dgate[b].astype(BF16)

    res = _call(
        body, name=name, grid=(RET_HEADS, n_sc),
        in_specs=[sp["proj"], sp["trig"], sp["trig"], sp["mask"], sp["dec"], sp["dec"], sp["cdec"], sp["gn"],
                  sp["val"], sp["state"], sp["val"]],
        out_specs=[sp["proj"], sp["gn"]],
        out_shape=[jax.ShapeDtypeStruct((B, S, RET_HEADS * RET_HEAD_COLS), BF16), jax.ShapeDtypeStruct((1, RET_VW), F32)],
        scratch_shapes=[pltpu.VMEM((B, RET_DK, RET_DV), F32)],
        args=(proj.reshape(B, S, -1), *consts, gn, o.reshape(B, S, -1), states, dy.reshape(B, S, -1)), ride=ride)
    (dproj, dgn), lands = res if ride is not None else (res, None)
    outs = [dproj.reshape(T, -1), dgn]
    return outs if ride is None else (outs, lands)


def _ring_index():
    u = np.arange(ATT_RING)
    offset = np.where(u < ATT_KW, u, u - ATT_RING)
    return np.clip(offset - BAND_PAST, -MAX_REL, CHUNK - 1) + MAX_REL


def _bias_ring(rel):
    n_clip = BAND_PAST - MAX_REL
    n_hi = ATT_KW - n_clip - REL_TABLE
    ring = jnp.concatenate([jnp.broadcast_to(rel[:, :1], (ATT_HEADS, n_clip)), rel,
                            jnp.broadcast_to(rel[:, -1:], (ATT_HEADS, n_hi)),
                            jnp.broadcast_to(rel[:, :1], (ATT_HEADS, ATT_CQ))], axis=1)
    return jnp.broadcast_to(ring.reshape(ATT_PAIRS, 2, 1, ATT_RING), (ATT_PAIRS, 2, 8, ATT_RING))


def _band_masks():
    i = np.arange(ATT_CQ)[:, None]
    j = np.arange(ATT_KW)[None, :]
    lo = CHUNK * (i // CHUNK)
    band = np.where((j >= lo) & (j < lo + BAND_PAST + CHUNK), 0.0, NEG)
    return np.stack([band + np.where(j + v * ATT_CQ >= BAND_PAST, 0.0, NEG)
                     for v in range(ATT_VARIANTS)]).astype(np.float32)


def _attn_bias(bias_scr, ring_ref, band_ref):
    for hh in range(2):
        rows = jnp.broadcast_to(ring_ref[hh, 0:1, :], (ATT_CQ, ATT_RING))
        toeplitz = pltpu.roll(rows, 0, 1, stride=1, stride_axis=0)[:, :ATT_KW]
        for v in range(ATT_VARIANTS):
            bias_scr[v, hh] = toeplitz * LOG2E + band_ref[v]


ATT_STRIP = 32
ATT_SCALE = ATT_DH ** -0.5
LOG2E = 1.4426950408889634


def _strips(fn):
    def strip(r, carry):
        fn(pl.ds(pl.multiple_of(r * ATT_STRIP, ATT_STRIP), ATT_STRIP))
        return carry

    lax.fori_loop(0, ATT_CQ // ATT_STRIP, strip, 0, unroll=True)


def _attn_prepare(kpad, vpad, qkv_ref):
    kpad[0:BAND_PAST, :] = jnp.zeros((BAND_PAST, LANES), BF16)
    vpad[0:BAND_PAST, :] = jnp.zeros((BAND_PAST, LANES), BF16)
    kpad[BAND_PAST:, :] = qkv_ref[1]
    vpad[BAND_PAST:, :] = qkv_ref[2]


def _attn_queries(qkv_ref, qs):
    lane = lax.broadcasted_iota(jnp.int32, (1, LANES), 1)
    q = (qkv_ref[0, pl.ds(qs, ATT_CQ), :].astype(F32) * (ATT_SCALE * LOG2E)).astype(BF16)
    return [jnp.where((lane >= ATT_DH * hh) & (lane < ATT_DH * (hh + 1)), q, jnp.zeros_like(q)) for hh in range(2)]


def _attn_scores(s_ref, qkv_ref, kpad, b_ref, t, n_qb):
    t = jnp.minimum(t, n_qb - 1)
    qs = pl.multiple_of(t * ATT_CQ, ATT_CQ)
    variant = jnp.minimum(t, ATT_VARIANTS - 1)
    kw = kpad[pl.ds(qs, ATT_KW), :]
    qm = _attn_queries(qkv_ref, qs)
    for hh in range(2):
        s_ref[hh] = _dot_nt(qm[hh], kw) + b_ref[variant, hh]


def _attn_softmax(s_ref, e_ref, linv_ref, m_ref=None):
    def strip(rows):
        s = s_ref[rows, :]
        m = jnp.max(s, axis=-1, keepdims=True)
        e = jnp.exp2(s - m)
        e_ref[rows, :] = e.astype(BF16)
        linv_ref[rows, :] = jnp.broadcast_to(1.0 / jnp.sum(e, axis=-1, keepdims=True), (ATT_STRIP, LANES))
        if m_ref is not None:
            m_ref[rows, :] = jnp.broadcast_to(m, (ATT_STRIP, LANES))

    _strips(strip)


_RING_SPEC = pl.BlockSpec((None, 2, 8, ATT_RING), lambda hp, b: (hp, 0, 0, 0))
_BAND_SPEC = pl.BlockSpec((ATT_VARIANTS, ATT_CQ, ATT_KW), lambda hp, b: (0, 0, 0))


def _attn_fwd(qkv3, ring, B, S, name, ride=None):
    T = B * S
    n_qb = S // ATT_CQ

    def body(qkv_ref, ring_ref, band_ref, o_ref, kpad, vpad, b_ref, s_scr, e_scr, linv_scr):
        @pl.when(pl.program_id(1) == 0)
        def _():
            _attn_bias(b_ref, ring_ref, band_ref)

        _attn_prepare(kpad, vpad, qkv_ref)
        e_scr[...] = jnp.zeros_like(e_scr)
        linv_scr[...] = jnp.zeros_like(linv_scr)
        lane = lax.broadcasted_iota(jnp.int32, (1, LANES), 1)

        def softmax(slot):
            for hh in range(2):
                _attn_softmax(s_scr.at[slot, hh], e_scr.at[slot, hh], linv_scr.at[slot, hh])

        def output(t, slot):
            qs = pl.multiple_of(jnp.maximum(t, 0) * ATT_CQ, ATT_CQ)
            vw = vpad[pl.ds(qs, ATT_KW), :]
            outs = [_dot(e_scr[slot, hh], vw) * linv_scr[slot, hh] for hh in range(2)]
            o_ref[pl.ds(qs, ATT_CQ), :] = jnp.where(lane < ATT_DH, outs[0], outs[1]).astype(BF16)

        def pair(u, carry):
            t = 2 * u
            _attn_scores(s_scr.at[1], qkv_ref, kpad, b_ref, t + 1, n_qb)
            softmax(0)
            output(t - 1, 1)
            _attn_scores(s_scr.at[0], qkv_ref, kpad, b_ref, t + 2, n_qb)
            softmax(1)
            output(t, 0)
            return carry

        _attn_scores(s_scr.at[0], qkv_ref, kpad, b_ref, 0, n_qb)
        lax.fori_loop(0, n_qb // 2, pair, 0)
        output(n_qb - 1, 1)

    return _call(body, name=name, grid=(ATT_PAIRS, B),
                 in_specs=[pl.BlockSpec((3, S, LANES), lambda hp, b: (0, b, hp)), _RING_SPEC, _BAND_SPEC],
                 out_specs=[pl.BlockSpec((S, LANES), lambda hp, b: (b, hp))],
                 out_shape=[jax.ShapeDtypeStruct((T, D_MODEL), BF16)],
                 scratch_shapes=[pltpu.VMEM((S + BAND_PAST, LANES), BF16), pltpu.VMEM((S + BAND_PAST, LANES), BF16),
                                 pltpu.VMEM((ATT_VARIANTS, 2, ATT_CQ, ATT_KW), F32),
                                 pltpu.VMEM((2, 2, ATT_CQ, ATT_KW), F32),
                                 pltpu.VMEM((2, 2, ATT_CQ, ATT_KW), BF16), pltpu.VMEM((2, 2, ATT_CQ, LANES), F32)],
                 args=(qkv3, ring, jnp.asarray(_band_masks())), ride=ride)


def _attn_bwd(qkv3, ring, do, B, S, name, ride=None):
    T = B * S
    n_qb = S // ATT_CQ

    def body(qkv_ref, ring_ref, band_ref, do_ref, dqkv_ref, dring_ref, kpad, vpad, dkacc, dvacc, b_ref, db_ref):
        @pl.when(pl.program_id(1) == 0)
        def _():
            _attn_bias(b_ref, ring_ref, band_ref)
            db_ref[...] = jnp.zeros_like(db_ref)

        _attn_prepare(kpad, vpad, qkv_ref)
        dkacc[...] = jnp.zeros_like(dkacc)
        dvacc[...] = jnp.zeros_like(dvacc)
        lane = lax.broadcasted_iota(jnp.int32, (1, LANES), 1)

        def step(qb, carry):
            qs = pl.multiple_of(qb * ATT_CQ, ATT_CQ)
            variant = jnp.minimum(qb, ATT_VARIANTS - 1)
            dov = do_ref[pl.ds(qs, ATT_CQ), :].astype(F32)
            kw = kpad[pl.ds(qs, ATT_KW), :]
            vw = vpad[pl.ds(qs, ATT_KW), :]
            heads = (0, 1)
            qmh = _attn_queries(qkv_ref, qs)
            s = [_dot_nt(qmh[hh], kw) + b_ref[variant, hh] for hh in heads]
            e, linv, dom, dp = [None, None], [None, None], [None, None], [None, None]
            for hh in heads:
                e[hh] = jnp.exp2(s[hh] - jnp.max(s[hh], axis=-1, keepdims=True))
                linv[hh] = 1.0 / jnp.sum(e[hh], axis=-1, keepdims=True)
                sel = (lane >= ATT_DH * hh) & (lane < ATT_DH * (hh + 1))
                dom[hh] = jnp.where(sel, dov * linv[hh], 0.0).astype(BF16)
                dp[hh] = _dot_nt(dom[hh], vw)
            dqs, dk, dv = [], None, None
            for hh in heads:
                ds = e[hh] * (dp[hh] - jnp.sum(dp[hh] * e[hh], axis=-1, keepdims=True) * linv[hh])
                db_ref[hh] += ds
                dsb = ds.astype(BF16)
                dqs.append(_dot(dsb, kw) * ATT_SCALE)
                dk_h = _dot_tn(qmh[hh], dsb) * (1.0 / LOG2E)
                dv_h = _dot_tn(dom[hh], e[hh].astype(BF16))
                dk = dk_h if dk is None else dk + dk_h
                dv = dv_h if dv is None else dv + dv_h
            dqkv_ref[0, pl.ds(qs, ATT_CQ), :] = jnp.where(lane < ATT_DH, dqs[0], dqs[1]).astype(BF16)
            dkacc[:, pl.ds(qs, ATT_KW)] += dk
            dvacc[:, pl.ds(qs, ATT_KW)] += dv
            return carry

        lax.fori_loop(0, n_qb, step, 0, unroll=4)
        dqkv_ref[1] = dkacc[:, BAND_PAST:].T.astype(BF16)
        dqkv_ref[2] = dvacc[:, BAND_PAST:].T.astype(BF16)

        @pl.when(pl.program_id(1) == B - 1)
        def _():
            r = lax.broadcasted_iota(jnp.int32, (ATT_CQ, ATT_CQ), 0)
            c = lax.broadcasted_iota(jnp.int32, (ATT_CQ, ATT_CQ), 1)
            reverse = jnp.where(r + c == ATT_CQ - 1, 1.0, 0.0).astype(BF16)
            for hh in range(2):
                x = jnp.concatenate([db_ref[hh], jnp.zeros((ATT_CQ, ATT_RING - ATT_KW), F32)], axis=1)
                flipped = jnp.zeros((ATT_CQ, ATT_RING), F32)
                for _ in range(3):
                    part = x.astype(BF16)
                    flipped = flipped + _dot(reverse, part)
                    x = x - part.astype(F32)
                aligned = pltpu.roll(flipped, ATT_KW + 1, 1, stride=1, stride_axis=0)
                dring_ref[hh] = jnp.sum(aligned.reshape(ATT_CQ // 8, 8, ATT_RING), axis=0)

    qkv_spec = pl.BlockSpec((3, S, LANES), lambda hp, b: (0, b, hp))
    return _call(body, name=name, grid=(ATT_PAIRS, B),
                 in_specs=[qkv_spec, _RING_SPEC, _BAND_SPEC, pl.BlockSpec((S, LANES), lambda hp, b: (b, hp))],
                 out_specs=[qkv_spec, _RING_SPEC],
                 out_shape=[jax.ShapeDtypeStruct((3, T, D_MODEL), BF16),
                            jax.ShapeDtypeStruct((ATT_PAIRS, 2, 8, ATT_RING), F32)],
                 scratch_shapes=[pltpu.VMEM((S + BAND_PAST, LANES), BF16), pltpu.VMEM((S + BAND_PAST, LANES), BF16),
                                 pltpu.VMEM((LANES, S + BAND_PAST), F32), pltpu.VMEM((LANES, S + BAND_PAST), F32),
                                 pltpu.VMEM((ATT_VARIANTS, 2, ATT_CQ, ATT_KW), F32), pltpu.VMEM((2, ATT_CQ, ATT_KW), F32)],
                 args=(qkv3, ring, jnp.asarray(_band_masks()), do), ride=ride)


def _rel_bias_grad(dring, name):
    fold = np.zeros((ATT_RING, REL_PAD), np.float32)
    fold[np.arange(ATT_RING), _ring_index()] = 1.0
    fold = jnp.asarray(fold, BF16)

    def body(d_ref, f_ref, o_ref):
        x = jnp.sum(d_ref[...], axis=0, keepdims=True)
        x = jnp.broadcast_to(x, (8, ATT_RING))
        acc = jnp.zeros((8, REL_PAD), F32)
        for _ in range(3):
            part = x.astype(BF16)
            acc = acc + _dot(part, f_ref[...])
            x = x - part.astype(F32)
        o_ref[...] = acc[0:1, :]

    out = _call(body, name=name, grid=(ATT_HEADS,),
                in_specs=[pl.BlockSpec((None, 8, ATT_RING), lambda h: (h, 0, 0)),
                          pl.BlockSpec((ATT_RING, REL_PAD), lambda h: (0, 0))],
                out_specs=[pl.BlockSpec((None, 1, REL_PAD), lambda h: (h, 0, 0))],
                out_shape=[jax.ShapeDtypeStruct((ATT_HEADS, 1, REL_PAD), F32)],
                args=(dring.reshape(ATT_HEADS, 8, ATT_RING), fold))[0]
    return out.reshape(ATT_HEADS, REL_PAD)


def _adamw(w, g, m, v):
    m = ADAM_B1 * m + (1.0 - ADAM_B1) * g
    v = ADAM_B2 * v + (1.0 - ADAM_B2) * (g * g)
    m_hat = m / (1.0 - ADAM_B1 ** ADAM_STEP)
    v_hat = v / (1.0 - ADAM_B2 ** ADAM_STEP)
    delta = -ADAM_LR * (m_hat / (jnp.sqrt(v_hat) + ADAM_EPS) + ADAM_WD * w)
    return delta, m, v


def _sum_devices(ref):
    g = ref[0].astype(F32)
    for d in range(1, N_DEV):
        g = g + ref[d].astype(F32)
    return g


def _adamw_reduce(lands, w, m, v, name, tr=256):
    L, R, C = w.shape
    tr = min(tr, R)
    n_i = R // tr

    def body(*refs):
        l_refs = refs[:L]
        w_ref, m_ref, v_ref, g_out, d_out, m_out, v_out = refs[L:]
        layer = pl.program_id(0)
        for l in range(L):
            @pl.when(layer == l)
            def _(l=l):
                g = _sum_devices(l_refs[l])
                g_out[...] = g
                d_out[...], m_out[...], v_out[...] = _adamw(w_ref[...], g, m_ref[...], v_ref[...])

    def land_spec(l):
        return pl.BlockSpec((N_DEV, tr, C), lambda ly, i: (0, jnp.where(ly == l, i, jnp.where(ly < l, 0, n_i - 1)), 0))

    blk = pl.BlockSpec((None, tr, C), lambda ly, i: (ly, i, 0))
    return _call(body, name=name, grid=(L, n_i), in_specs=[land_spec(l) for l in range(L)] + [blk, blk, blk],
                 out_specs=[blk] * 4, out_shape=[jax.ShapeDtypeStruct((L, R, C), F32)] * 4, args=(*lands, w, m, v))


def _small_update(land_small, land_rel, w, m, v, name):
    def body(ls_ref, lr_ref, w_ref, m_ref, v_ref, g_out, d_out, m_out, v_out, rel_out):
        g = _sum_devices(ls_ref)
        g_out[...] = g
        d_out[...], m_out[...], v_out[...] = _adamw(w_ref[...], g, m_ref[...], v_ref[...])
        rel_out[...] = _sum_devices(lr_ref)

    return pl.pallas_call(
        body, name=name,
        out_shape=[jax.ShapeDtypeStruct(w.shape, F32)] * 4 + [jax.ShapeDtypeStruct(land_rel.shape[1:], F32)],
    )(land_small, land_rel, w, m, v)


def _adamw_plain(w, g, m, v, name):
    def body(w_ref, g_ref, m_ref, v_ref, d_out, m_out, v_out):
        d_out[...], m_out[...], v_out[...] = _adamw(w_ref[...], g_ref[...], m_ref[...], v_ref[...])

    return pl.pallas_call(body, name=name, out_shape=[jax.ShapeDtypeStruct(w.shape, F32)] * 3)(w, g, m, v)


def _ret_piece_of_column_block(p):
    per_head = RET_HEAD_COLS // RET_DK
    qk = jnp.where(p < RET_HEADS, per_head * p, per_head * (p - RET_HEADS) + 1)
    pv = p - 2 * RET_HEADS
    vv = per_head * (pv // 2) + 2 + pv % 2
    pg = p - 4 * RET_HEADS
    gg = per_head * (pg // 2) + 4 + pg % 2
    return jnp.where(p < 2 * RET_HEADS, qk, jnp.where(p < 4 * RET_HEADS, vv, gg))


def _step(x, target, shards, mix_g, gn_g, rel_shard, mlp_g, final_g):
    s_rin, s_rout, s_ain, s_aout, s_w1, s_w2 = shards
    B, S, _ = x.shape
    T = B * S
    tk = min(T, 4096)
    n_k = T // tk
    h0 = x.reshape(T, D_MODEL)
    tgt = target.reshape(T, D_MODEL)
    consts = _ret_constants(S)
    w1_cols = w2_rows = D_FF // N_DEV

    rel_cols = REL_TABLE // N_DEV
    per_shard = RET_IN // N_DEV // RET_DK

    def head_layout(r):
        return lambda ref, blk: ref.at[:, pl.ds(pl.multiple_of(
            _ret_piece_of_column_block(per_shard * blk + r) * RET_DK, LANES), RET_DK)]

    w_rin_heads, rel_all = _ChipGather(
        [s_rin[:, r * RET_DK:(r + 1) * RET_DK] for r in range(per_shard)] + [rel_shard],
        [jax.ShapeDtypeStruct((D_MODEL, RET_IN), BF16), jax.ShapeDtypeStruct((N_DEV, ATT_HEADS, rel_cols), F32)],
        [0] * per_shard + [1], [head_layout(r) for r in range(per_shard)] + [_slot]).run("gather_first")
    ring = _bias_ring(rel_all.transpose(1, 0, 2).reshape(ATT_HEADS, REL_TABLE))

    def gather(*items):
        return _ChipGather([s for s, _, _ in items], [jax.ShapeDtypeStruct(full, BF16) for _, _, full in items],
                           list(range(len(items))), [view for _, view, _ in items])

    def scatter(*items):
        return _Exchange([p for p, _, _ in items], [(a, view, a, _slot) for a, (_, view, _) in enumerate(items)],
                         [jax.ShapeDtypeStruct((N_DEV, *shard), BF16) for _, _, shard in items])

    w1_full, w2_full = (D_MODEL, D_FF), (D_FF, D_MODEL)
    w1_shard, w2_shard = (D_MODEL, w1_cols), (w2_rows, D_MODEL)

    ride = gather((s_rout, _rows(RET_VW // N_DEV), (RET_VW, D_MODEL)), (s_w1[0], _cols(w1_cols), w1_full))
    (proj, n0), (w_rout, w1_0) = _ret_proj(h0, mix_g[0:1], w_rin_heads, consts[0], consts[1], S, "ret_proj", 512, ride=ride)
    (y, o, states), (w2_0,) = _ret_fwd(proj, consts, gn_g, B, S, "ret_fwd", ride=gather((s_w2[0], _rows(w2_rows), w2_full)))
    h1 = _matmul_res(y, w_rout, h0, "ret_out")
    ride = gather((s_ain, _cols(3 * D_MODEL // N_DEV), (D_MODEL, 3 * D_MODEL)),
                  (s_aout, _rows(D_MODEL // N_DEV), (D_MODEL, D_MODEL)))
    (h2, n1, u1), (w_ain, w_aout) = _mlp_fwd(h1, mlp_g[0:1], w1_0, w2_0, "mlp0_fwd", ride=ride)
    qkv3, n2 = _norm_proj(h2, mix_g[1:2], w_ain, 3, "att_proj", 1024)
    ride = gather((s_w1[1], _cols(w1_cols), w1_full), (s_w2[1], _rows(w2_rows), w2_full))
    (o2,), (w1_1, w2_1) = _attn_fwd(qkv3, ring, B, S, "att_fwd", ride=ride)
    h3 = _matmul_res(o2, w_aout, h2, "att_out", tm=2048)
    h4, n3, u3 = _mlp_fwd(h3, mlp_g[1:2], w1_1, w2_1, "mlp1_fwd")
    dh4, dh4b, loss, dg_final = _final_loss(h4, final_g, tgt, "final_loss")

    def tok(width):
        return dict(a=pl.BlockSpec((tk, width), lambda i, j, k: (k, i)), b=pl.BlockSpec((tk, width), lambda i, j, k: (k, j)))

    def mlp_dw2(u, dhb, tag, ride=None):
        return _wgrad(u, tok(1024)["a"], dhb, tok(1024)["b"], D_FF, D_MODEL, 1024, 1024, n_k, tag + "_dw2",
                      square_a=True, ride=ride)

    def mlp_dw1(n, du, tag):
        return _wgrad(n, tok(1024)["a"], du, tok(1024)["b"], D_MODEL, D_FF, 1024, 1024, n_k, tag + "_dw1")

    dh3, dh3b, du3, dg_mlp1 = _mlp_bwd(dh4, u3, w1_1, w2_1, h3, mlp_g[1:2], "mlp1_bwd")
    gw2_1, gw1_1 = mlp_dw2(u3, dh4b, "mlp1"), mlp_dw1(n3, du3, "mlp1")
    do2 = _matmul_nt(dh3b, w_aout, "att_out_bwd", tm=2048)
    g_aout = _wgrad(o2, tok(1024)["a"], dh3b, tok(1024)["b"], D_MODEL, D_MODEL, 1024, 1024, n_k, "att_out_dw")
    ride = scatter((gw1_1, _cols(w1_cols), w1_shard), (gw2_1, _rows(w2_rows), w2_shard),
                   (g_aout, _rows(D_MODEL // N_DEV), (D_MODEL // N_DEV, D_MODEL)))
    (dqkv3, dring), (l_w1_1, l_w2_1, l_aout) = _attn_bwd(qkv3, ring, do2, B, S, "att_bwd", ride=ride)
    g_rel = _rel_bias_grad(dring, "rel_bias_grad")
    dh2, dh2b, dg_mix1 = _proj_bwd(dqkv3, w_ain, h2, mix_g[1:2], dh3, "att_proj_bwd", 1024)
    g_ain = _wgrad(n2, tok(1024)["a"], dqkv3, pl.BlockSpec((None, tk, D_MODEL), lambda i, j, k: (j, k, 0)),
                   D_MODEL, 3 * D_MODEL, 1024, 1024, n_k, "att_proj_dw")
    ride = scatter((g_ain, _cols(3 * D_MODEL // N_DEV), (D_MODEL, 3 * D_MODEL // N_DEV)))
    (dh1, dh1b, du1, dg_mlp0), (l_ain,) = _mlp_bwd(dh2, u1, w1_0, w2_0, h1, mlp_g[0:1], "mlp0_bwd", ride=ride)
    dy = _matmul_nt(dh1b, w_rout, "ret_out_bwd")
    g_rout = _wgrad(y, tok(1024)["a"], dh1b, tok(1024)["b"], RET_VW, D_MODEL, 1024, 1024, n_k, "ret_out_dw")
    gw2_0, (l_rout,) = mlp_dw2(u1, dh2b, "mlp0", ride=scatter((g_rout, _rows(RET_VW // N_DEV), (RET_VW // N_DEV, D_MODEL))))
    gw1_0 = mlp_dw1(n1, du1, "mlp0")
    ride = scatter((gw2_0, _rows(w2_rows), w2_shard))
    (dproj, dgn), (l_w2_0,) = _ret_bwd(proj, consts, gn_g, o, states, dy, B, S, "ret_bwd", ride=ride)
    g_rin, (l_w1_0,) = _wgrad(
        n0, tok(1024)["a"], dproj,
        [pl.BlockSpec((tk, RET_DK), functools.partial(
            lambda i, j, k, r: (k, _ret_piece_of_column_block(per_shard * j + r)), r=r)) for r in range(per_shard)],
        D_MODEL, RET_IN, 1024, per_shard * RET_DK, n_k, "ret_proj_dw", ride=scatter((gw1_0, _cols(w1_cols), w1_shard)))
    ride = scatter((g_rin, _cols(RET_IN // N_DEV), (D_MODEL, RET_IN // N_DEV)))
    (dx, _, dg_mix0), (l_rin,) = _proj_bwd(dproj, w_rin_heads, h0, mix_g[0:1], dh1, "ret_proj_bwd", 512, ride=ride)

    small = jnp.concatenate([dg_mix0, dg_mix1, dg_mlp0, dg_mlp1, dg_final, dgn.reshape(2, D_MODEL),
                             jnp.broadcast_to(loss[0:1, 0:1], (1, D_MODEL))], axis=0)
    l_small, l_rel = _Exchange(
        [small, g_rel], [(0, _whole, 0, _slot), (1, _whole, 1, _slot)],
        [jax.ShapeDtypeStruct((N_DEV, 8, D_MODEL), F32), jax.ShapeDtypeStruct((N_DEV, ATT_HEADS, REL_PAD), F32)],
    ).run("scatter_small")
    lands = dict(rin=l_rin, rout=l_rout, ain=l_ain, aout=l_aout, w1=(l_w1_0, l_w1_1), w2=(l_w2_0, l_w2_1),
                 small=l_small, rel=l_rel)
    return dx.reshape(B, S, D_MODEL), lands


def kernel(x, mix_norm_g, ret_w_in, ret_gn_g, ret_w_out, att_w_in, att_rel_bias, att_w_out, mlp_norm_g, mlp_w1, mlp_w2, final_norm_g, loss_target, m_mix_norm_g, m_ret_w_in, m_ret_gn_g, m_ret_w_out, m_att_w_in, m_att_rel_bias, m_att_w_out, m_mlp_norm_g, m_mlp_w1, m_mlp_w2, m_final_norm_g, v_mix_norm_g, v_ret_w_in, v_ret_gn_g, v_ret_w_out, v_att_w_in, v_att_rel_bias, v_att_w_out, v_mlp_norm_g, v_mlp_w1, v_mlp_w2, v_final_norm_g):
    me = _lin(_place())
    rel_cols = REL_TABLE // N_DEV
    shards = (ret_w_in[0].astype(BF16), ret_w_out[0].astype(BF16), att_w_in[0].astype(BF16), att_w_out[0].astype(BF16),
              (mlp_w1[0].astype(BF16), mlp_w1[1].astype(BF16)), (mlp_w2[0].astype(BF16), mlp_w2[1].astype(BF16)))
    grad_x, lands = _step(x, loss_target, shards, mix_norm_g, ret_gn_g, att_rel_bias[0], mlp_norm_g,
                          final_norm_g.reshape(1, D_MODEL))

    u_rin = _adamw_reduce([lands["rin"]], ret_w_in, m_ret_w_in, v_ret_w_in, "update_ret_w_in")
    u_rout = _adamw_reduce([lands["rout"]], ret_w_out, m_ret_w_out, v_ret_w_out, "update_ret_w_out")
    u_ain = _adamw_reduce([lands["ain"]], att_w_in, m_att_w_in, v_att_w_in, "update_att_w_in")
    u_aout = _adamw_reduce([lands["aout"]], att_w_out, m_att_w_out, v_att_w_out, "update_att_w_out")
    u_w1 = _adamw_reduce(lands["w1"], mlp_w1, m_mlp_w1, v_mlp_w1, "update_mlp_w1")
    u_w2 = _adamw_reduce(lands["w2"], mlp_w2, m_mlp_w2, v_mlp_w2, "update_mlp_w2")

    def pack(mix, mlp, fin, gn):
        return jnp.concatenate([mix, mlp, fin.reshape(1, D_MODEL), gn.reshape(2, D_MODEL), jnp.zeros((1, D_MODEL), F32)], axis=0)

    small_w = pack(mix_norm_g, mlp_norm_g, final_norm_g, ret_gn_g)
    small_m = pack(m_mix_norm_g, m_mlp_norm_g, m_final_norm_g, m_ret_gn_g)
    small_v = pack(v_mix_norm_g, v_mlp_norm_g, v_final_norm_g, v_ret_gn_g)
    sg, sd, sm, sv, rel_sum = _small_update(lands["small"], lands["rel"], small_w, small_m, small_v, "update_small")
    g_rel_mine = lax.dynamic_slice(rel_sum, (0, me * rel_cols), (ATT_HEADS, rel_cols))
    rel_d, rel_m, rel_v = _adamw_plain(att_rel_bias[0], g_rel_mine, m_att_rel_bias[0], v_att_rel_bias[0], "update_rel_bias")
    u_rel = [g_rel_mine[None], rel_d[None], rel_m[None], rel_v[None]]

    def unpack(t):
        return dict(mix=t[0:2], mlp=t[2:4], fin=t[4], gn=t[5:7].reshape(1, RET_VW))

    us = [unpack(t) for t in (sg, sd, sm, sv)]
    outs = [sg[7, 0], grad_x]
    for k in range(4):
        outs += [us[k]["mix"], u_rin[k], us[k]["gn"], u_rout[k], u_ain[k], u_rel[k], u_aout[k], us[k]["mlp"],
                 u_w1[k], u_w2[k], us[k]["fin"]]
    return tuple(outs)
```
